```python
import math
import jax, jax.numpy as jnp
from jax import lax
import numpy as np

D_MODEL = 1024
BATCH = 8
SEQ = 8192
DEPTH = 1

CHUNK = 64
Q_BLOCK = 128
EPS = 1e-6

FOX_WIDTH = D_MODEL // 2
FOX_HEAD_DIM = 64
FOX_HEADS = FOX_WIDTH // FOX_HEAD_DIM

S5_WIDTH = D_MODEL // 2
S5_GROUP = 16
S5_GROUPS = S5_WIDTH // S5_GROUP
S5_STATE = 64
DT_MIN = 1e-3
DT_MAX = 1e-1

PROJ_SIZES = (FOX_WIDTH, FOX_WIDTH, FOX_WIDTH, FOX_HEADS, FOX_WIDTH,
              S5_WIDTH, S5_WIDTH,
              D_MODEL, D_MODEL)
PROJ_WIDTH = sum(PROJ_SIZES)

kernel_name = "fox_s5_gated_hybrid_block"


def rmsnorm(x, g):
    xf = x.astype(jnp.float32)
    y = xf * lax.rsqrt(jnp.mean(xf * xf, axis=-1, keepdims=True) + EPS)
    return (y * g.astype(jnp.float32)).astype(x.dtype)


def forgetting_attention(q, k, v, log_f):
    S = q.shape[1]
    Dh = q.shape[-1]
    F = jnp.cumsum(log_f.astype(jnp.float32), axis=1).transpose(0, 2, 1)
    scale = Dh ** -0.5
    neg = jnp.finfo(jnp.float32).min
    outs = []
    for i in range(S // Q_BLOCK):
        q0 = i * Q_BLOCK
        kend = q0 + Q_BLOCK
        qb = q[:, q0:kend]
        kb = k[:, :kend]
        vb = v[:, :kend]
        s = jnp.einsum('bqhd,bkhd->bhqk', qb, kb).astype(jnp.float32) * scale
        s = s + F[:, :, q0:kend, None] - F[:, :, None, :kend]
        t_idx = q0 + jnp.arange(Q_BLOCK)
        s_idx = jnp.arange(kend)
        s = jnp.where(s_idx[None, :] <= t_idx[:, None], s, neg)
        p = jax.nn.softmax(s, axis=-1).astype(v.dtype)
        outs.append(jnp.einsum('bhqk,bkhd->bqhd', p, vb))
    return jnp.concatenate(outs, axis=1)


def s5_ssm(u, a_re, a_im, log_dt, b_re, b_im, c_re, c_im, d_skip):
    Bsz, S, W = u.shape
    uf = u.astype(jnp.float32)
    ug = uf.reshape(Bsz, S, S5_GROUPS, S5_GROUP).astype(jnp.complex64)
    lam = lax.complex(a_re.astype(jnp.float32), a_im.astype(jnp.float32))
    dt = jnp.exp(log_dt.astype(jnp.float32))[:, None]
    a_bar = jnp.exp(lam * dt)
    bmat = lax.complex(b_re.astype(jnp.float32), b_im.astype(jnp.float32))
    b_bar = ((a_bar - 1.0) / lam)[:, :, None] * bmat
    bu = jnp.einsum('bsgc,gpc->sbgp', ug, b_bar)
    a_seq = jnp.broadcast_to(a_bar, bu.shape)

    def combine(left, right):
        return (right[0] * left[0], right[0] * left[1] + right[1])

    _, states = lax.associative_scan(combine, (a_seq, bu), axis=0)
    cmat = lax.complex(c_re.astype(jnp.float32), c_im.astype(jnp.float32))
    y = jnp.einsum('sbgp,gcp->bsgc', states, cmat).real.reshape(Bsz, S, W)
    y = y + d_skip.astype(jnp.float32) * uf
    return y.astype(u.dtype)


def _fwd_setup_inputs(seed: int = 0) -> dict:
    key = jax.random.key(seed)
    ks = jax.random.split(key, 24)
    D, L, G, P, Cg = D_MODEL, DEPTH, S5_GROUPS, S5_STATE, S5_GROUP
    f32 = jnp.float32

    def nrm(k, shape, fan_in, s=1.0):
        return s * jax.random.normal(k, shape, f32) * fan_in ** -0.5

    x = jax.random.normal(ks[0], (BATCH, SEQ, D), f32)
    c = jax.random.normal(ks[1], (BATCH, D), f32)
    w_ada = nrm(ks[2], (L, D, 3 * D), D, 0.3)
    b_ada = 0.02 * jax.random.normal(ks[3], (L, 3 * D), f32)
    g_norm = 1.0 + 0.05 * jax.random.normal(ks[4], (L, D), f32)
    w_in = nrm(ks[5], (L, D, PROJ_WIDTH), D)
    b_f = 3.0 + 0.5 * jax.random.normal(ks[6], (L, FOX_HEADS), f32)
    n_idx = jnp.arange(P, dtype=f32)
    a_re = -0.5 + 0.01 * jax.random.normal(ks[7], (L, G, P), f32)
    a_im = jnp.broadcast_to(math.pi * n_idx, (L, G, P)) + 0.01 * jax.random.normal(ks[8], (L, G, P), f32)
    log_dt = jax.random.uniform(ks[9], (L, G), f32, math.log(DT_MIN), math.log(DT_MAX))
    b_re = nrm(ks[10], (L, G, P, Cg), 2 * Cg)
    b_im = nrm(ks[11], (L, G, P, Cg), 2 * Cg)
    c_re = nrm(ks[12], (L, G, Cg, P), 2 * P)
    c_im = nrm(ks[13], (L, G, Cg, P), 2 * P)
    d_skip = jax.random.normal(ks[14], (L, S5_WIDTH), f32)
    w_glu = nrm(ks[15], (L, S5_WIDTH, S5_WIDTH), S5_WIDTH)
    b_glu = 0.02 * jax.random.normal(ks[16], (L, S5_WIDTH), f32)
    w_up_a = nrm(ks[17], (L, FOX_WIDTH, D), FOX_WIDTH)
    w_up_b = nrm(ks[18], (L, S5_WIDTH, D), S5_WIDTH)
    w_out = nrm(ks[19], (L, D, D), D)
    g_final = 1.0 + 0.05 * jax.random.normal(ks[20], (D,), f32)
    return {"x": x, "c": c, "w_ada": w_ada, "b_ada": b_ada, "g_norm": g_norm,
            "w_in": w_in, "b_f": b_f, "a_re": a_re, "a_im": a_im, "log_dt": log_dt,
            "b_re": b_re, "b_im": b_im, "c_re": c_re, "c_im": c_im, "d_skip": d_skip,
            "w_glu": w_glu, "b_glu": b_glu, "w_up_a": w_up_a, "w_up_b": w_up_b,
            "w_out": w_out, "g_final": g_final}


def _fwd_reference(x, c, w_ada, b_ada, g_norm, w_in, b_f, a_re, a_im, log_dt, b_re, b_im,
              c_re, c_im, d_skip, w_glu, b_glu, w_up_a, w_up_b, w_out, g_final):
    Bsz, S, D = x.shape
    offsets = []
    acc = 0
    for sz in PROJ_SIZES[:-1]:
        acc += sz
        offsets.append(acc)
    for l in range(DEPTH):
        mod = c @ w_ada[l] + b_ada[l]
        shift, scale, gate = jnp.split(mod, 3, axis=-1)
        h = rmsnorm(x, g_norm[l]) * (1.0 + scale[:, None, :]) + shift[:, None, :]

        proj = h @ w_in[l]
        q, k, v, f_logit, z_a, u, z_b, g_a, g_b = jnp.split(proj, offsets, axis=-1)

        hs = (Bsz, S, FOX_HEADS, FOX_HEAD_DIM)
        log_f = jax.nn.log_sigmoid((f_logit + b_f[l]).astype(jnp.float32))
        y_a = forgetting_attention(q.reshape(hs), k.reshape(hs), v.reshape(hs), log_f)
        y_a = y_a.reshape(Bsz, S, FOX_WIDTH) * jax.nn.silu(z_a)

        y_b = s5_ssm(u, a_re[l], a_im[l], log_dt[l], b_re[l], b_im[l], c_re[l], c_im[l], d_skip[l])
        y_b = jax.nn.gelu(y_b)
        y_b = y_b * jax.nn.sigmoid(y_b @ w_glu[l] + b_glu[l])
        y_b = y_b * jax.nn.silu(z_b)

        merged = jax.nn.sigmoid(g_a) * (y_a @ w_up_a[l]) + jax.nn.sigmoid(g_b) * (y_b @ w_up_b[l])
        x = x + gate[:, None, :] * (merged @ w_out[l])
    return rmsnorm(x, g_final)


import jax as _jax
import jax.numpy as _jnp

TWIN_FORMAT = 'train_step'
FWD_PARAMS = ['x', 'c', 'w_ada', 'b_ada', 'g_norm', 'w_in', 'b_f', 'a_re', 'a_im', 'log_dt', 'b_re', 'b_im', 'c_re', 'c_im', 'd_skip', 'w_glu', 'b_glu', 'w_up_a', 'w_up_b', 'w_out', 'g_final']
TWIN_WEIGHTS = ['w_ada', 'b_ada', 'g_norm', 'w_in', 'b_f', 'a_re', 'a_im', 'log_dt', 'b_re', 'b_im', 'c_re', 'c_im', 'd_skip', 'w_glu', 'b_glu', 'w_up_a', 'w_up_b', 'w_out', 'g_final']
TWIN_DIFF_INPUT = 'x'
TWIN_INPUTS = ['x', 'c', 'w_ada', 'b_ada', 'g_norm', 'w_in', 'b_f', 'a_re', 'a_im', 'log_dt', 'b_re', 'b_im', 'c_re', 'c_im', 'd_skip', 'w_glu', 'b_glu', 'w_up_a', 'w_up_b', 'w_out', 'g_final', 'loss_target', 'm_w_ada', 'm_b_ada', 'm_g_norm', 'm_w_in', 'm_b_f', 'm_a_re', 'm_a_im', 'm_log_dt', 'm_b_re', 'm_b_im', 'm_c_re', 'm_c_im', 'm_d_skip', 'm_w_glu', 'm_b_glu', 'm_w_up_a', 'm_w_up_b', 'm_w_out', 'm_g_final', 'v_w_ada', 'v_b_ada', 'v_g_norm', 'v_w_in', 'v_b_f', 'v_a_re', 'v_a_im', 'v_log_dt', 'v_b_re', 'v_b_im', 'v_c_re', 'v_c_im', 'v_d_skip', 'v_w_glu', 'v_b_glu', 'v_w_up_a', 'v_w_up_b', 'v_w_out', 'v_g_final']
TWIN_OUTPUTS = ['loss', 'grad_x', 'grad_w_ada', 'grad_b_ada', 'grad_g_norm', 'grad_w_in', 'grad_b_f', 'grad_a_re', 'grad_a_im', 'grad_log_dt', 'grad_b_re', 'grad_b_im', 'grad_c_re', 'grad_c_im', 'grad_d_skip', 'grad_w_glu', 'grad_b_glu', 'grad_w_up_a', 'grad_w_up_b', 'grad_w_out', 'grad_g_final', 'delta_w_ada', 'delta_b_ada', 'delta_g_norm', 'delta_w_in', 'delta_b_f', 'delta_a_re', 'delta_a_im', 'delta_log_dt', 'delta_b_re', 'delta_b_im', 'delta_c_re', 'delta_c_im', 'delta_d_skip', 'delta_w_glu', 'delta_b_glu', 'delta_w_up_a', 'delta_w_up_b', 'delta_w_out', 'delta_g_final', 'new_m_w_ada', 'new_m_b_ada', 'new_m_g_norm', 'new_m_w_in', 'new_m_b_f', 'new_m_a_re', 'new_m_a_im', 'new_m_log_dt', 'new_m_b_re', 'new_m_b_im', 'new_m_c_re', 'new_m_c_im', 'new_m_d_skip', 'new_m_w_glu', 'new_m_b_glu', 'new_m_w_up_a', 'new_m_w_up_b', 'new_m_w_out', 'new_m_g_final', 'new_v_w_ada', 'new_v_b_ada', 'new_v_g_norm', 'new_v_w_in', 'new_v_b_f', 'new_v_a_re', 'new_v_a_im', 'new_v_log_dt', 'new_v_b_re', 'new_v_b_im', 'new_v_c_re', 'new_v_c_im', 'new_v_d_skip', 'new_v_w_glu', 'new_v_b_glu', 'new_v_w_up_a', 'new_v_w_up_b', 'new_v_w_out', 'new_v_g_final']
TWIN_LEAF_KINDS = {'loss': 'loss', 'grad_x': 'grad_x', 'grad_w_ada': 'grad_w', 'grad_b_ada': 'grad_w', 'grad_g_norm': 'grad_w', 'grad_w_in': 'grad_w', 'grad_b_f': 'grad_w', 'grad_a_re': 'grad_w', 'grad_a_im': 'grad_w', 'grad_log_dt': 'grad_w', 'grad_b_re': 'grad_w', 'grad_b_im': 'grad_w', 'grad_c_re': 'grad_w', 'grad_c_im': 'grad_w', 'grad_d_skip': 'grad_w', 'grad_w_glu': 'grad_w', 'grad_b_glu': 'grad_w', 'grad_w_up_a': 'grad_w', 'grad_w_up_b': 'grad_w', 'grad_w_out': 'grad_w', 'grad_g_final': 'grad_w', 'delta_w_ada': 'delta_w', 'delta_b_ada': 'delta_w', 'delta_g_norm': 'delta_w', 'delta_w_in': 'delta_w', 'delta_b_f': 'delta_w', 'delta_a_re': 'delta_w', 'delta_a_im': 'delta_w', 'delta_log_dt': 'delta_w', 'delta_b_re': 'delta_w', 'delta_b_im': 'delta_w', 'delta_c_re': 'delta_w', 'delta_c_im': 'delta_w', 'delta_d_skip': 'delta_w', 'delta_w_glu': 'delta_w', 'delta_b_glu': 'delta_w', 'delta_w_up_a': 'delta_w', 'delta_w_up_b': 'delta_w', 'delta_w_out': 'delta_w', 'delta_g_final': 'delta_w', 'new_m_w_ada': 'new_m', 'new_m_b_ada': 'new_m', 'new_m_g_norm': 'new_m', 'new_m_w_in': 'new_m', 'new_m_b_f': 'new_m', 'new_m_a_re': 'new_m', 'new_m_a_im': 'new_m', 'new_m_log_dt': 'new_m', 'new_m_b_re': 'new_m', 'new_m_b_im': 'new_m', 'new_m_c_re': 'new_m', 'new_m_c_im': 'new_m', 'new_m_d_skip': 'new_m', 'new_m_w_glu': 'new_m', 'new_m_b_glu': 'new_m', 'new_m_w_up_a': 'new_m', 'new_m_w_up_b': 'new_m', 'new_m_w_out': 'new_m', 'new_m_g_final': 'new_m', 'new_v_w_ada': 'new_v', 'new_v_b_ada': 'new_v', 'new_v_g_norm': 'new_v', 'new_v_w_in': 'new_v', 'new_v_b_f': 'new_v', 'new_v_a_re': 'new_v', 'new_v_a_im': 'new_v', 'new_v_log_dt': 'new_v', 'new_v_b_re': 'new_v', 'new_v_b_im': 'new_v', 'new_v_c_re': 'new_v', 'new_v_c_im': 'new_v', 'new_v_d_skip': 'new_v', 'new_v_w_glu': 'new_v', 'new_v_b_glu': 'new_v', 'new_v_w_up_a': 'new_v', 'new_v_w_up_b': 'new_v', 'new_v_w_out': 'new_v', 'new_v_g_final': 'new_v'}


def _forward(args):
    return _fwd_reference(*[args[k] for k in FWD_PARAMS])


def _output_shape():
    def fwd():
        inp = _fwd_setup_inputs(0)
        return _fwd_reference(*[inp[k] for k in FWD_PARAMS])
    out = _jax.eval_shape(fwd)
    return out.shape, out.dtype

N_MICROBATCH = 1
ADAM_LR = 0.001
ADAM_B1 = 0.9
ADAM_B2 = 0.999
ADAM_EPS = 1e-08
ADAM_WD = 0.01
ADAM_STEP = 10
PER_EXAMPLE_BATCH_AXIS = {'x': 0, 'c': 0, 'loss_target': 0}
SHARED_INPUTS = []
_WEIGHT_DTYPES = {'w_ada': _jnp.float32, 'b_ada': _jnp.float32, 'g_norm': _jnp.float32, 'w_in': _jnp.float32, 'b_f': _jnp.float32, 'a_re': _jnp.float32, 'a_im': _jnp.float32, 'log_dt': _jnp.float32, 'b_re': _jnp.float32, 'b_im': _jnp.float32, 'c_re': _jnp.float32, 'c_im': _jnp.float32, 'd_skip': _jnp.float32, 'w_glu': _jnp.float32, 'b_glu': _jnp.float32, 'w_up_a': _jnp.float32, 'w_up_b': _jnp.float32, 'w_out': _jnp.float32, 'g_final': _jnp.float32}
MOMENT_SCALE = {'w_ada': 4.127353e-02, 'b_ada': 4.134033e-02, 'g_norm': 2.568478e-02, 'w_in': 1.271932e-02, 'b_f': 5.309746e-02, 'a_re': 1.248537e-03, 'a_im': 7.097883e-04, 'log_dt': 6.732080e-01, 'b_re': 6.365907e-04, 'b_im': 6.886993e-04, 'c_re': 1.362698e-03, 'c_im': 1.391356e-03, 'd_skip': 1.810820e-02, 'w_glu': 4.771086e-03, 'b_glu': 7.593295e-03, 'w_up_a': 1.281345e-02, 'w_up_b': 1.057326e-02, 'w_out': 1.656140e-02, 'g_final': 6.420964e+01}


def _to_microbatches(a, axis):
    t = _jnp.moveaxis(a, axis, 0)
    t = t.reshape((N_MICROBATCH, t.shape[0] // N_MICROBATCH) + t.shape[1:])
    return _jnp.moveaxis(t, 1, axis + 1)


def setup_inputs(seed: int = 0) -> dict:
    inp = _fwd_setup_inputs(seed)
    key = _jax.random.fold_in(_jax.random.key(seed), 7919)
    shape, _ = _output_shape()
    out = dict(inp)
    out["loss_target"] = _jax.random.normal(_jax.random.fold_in(key, 0), shape, _jnp.float32)
    for i, name in enumerate(TWIN_WEIGHTS):
        w = inp[name].astype(_jnp.float32)
        if MOMENT_SCALE is None:
            s = _jnp.sqrt(_jnp.mean(_jnp.square(w)) + 1e-30)
        else:
            s = MOMENT_SCALE[name]
        km, kv = _jax.random.split(_jax.random.fold_in(key, i + 1))
        out[name] = w
        out["m_" + name] = s * _jax.random.normal(km, w.shape, _jnp.float32)
        out["v_" + name] = (s * s) * _jax.random.uniform(kv, w.shape, _jnp.float32, 0.5, 1.5)
    if N_MICROBATCH > 1:
        for name, axis in PER_EXAMPLE_BATCH_AXIS.items():
            out[name] = _to_microbatches(out[name], axis)
    return {'x': out['x'], 'c': out['c'], 'w_ada': out['w_ada'], 'b_ada': out['b_ada'], 'g_norm': out['g_norm'], 'w_in': out['w_in'], 'b_f': out['b_f'], 'a_re': out['a_re'], 'a_im': out['a_im'], 'log_dt': out['log_dt'], 'b_re': out['b_re'], 'b_im': out['b_im'], 'c_re': out['c_re'], 'c_im': out['c_im'], 'd_skip': out['d_skip'], 'w_glu': out['w_glu'], 'b_glu': out['b_glu'], 'w_up_a': out['w_up_a'], 'w_up_b': out['w_up_b'], 'w_out': out['w_out'], 'g_final': out['g_final'], 'loss_target': out['loss_target'], 'm_w_ada': out['m_w_ada'], 'm_b_ada': out['m_b_ada'], 'm_g_norm': out['m_g_norm'], 'm_w_in': out['m_w_in'], 'm_b_f': out['m_b_f'], 'm_a_re': out['m_a_re'], 'm_a_im': out['m_a_im'], 'm_log_dt': out['m_log_dt'], 'm_b_re': out['m_b_re'], 'm_b_im': out['m_b_im'], 'm_c_re': out['m_c_re'], 'm_c_im': out['m_c_im'], 'm_d_skip': out['m_d_skip'], 'm_w_glu': out['m_w_glu'], 'm_b_glu': out['m_b_glu'], 'm_w_up_a': out['m_w_up_a'], 'm_w_up_b': out['m_w_up_b'], 'm_w_out': out['m_w_out'], 'm_g_final': out['m_g_final'], 'v_w_ada': out['v_w_ada'], 'v_b_ada': out['v_b_ada'], 'v_g_norm': out['v_g_norm'], 'v_w_in': out['v_w_in'], 'v_b_f': out['v_b_f'], 'v_a_re': out['v_a_re'], 'v_a_im': out['v_a_im'], 'v_log_dt': out['v_log_dt'], 'v_b_re': out['v_b_re'], 'v_b_im': out['v_b_im'], 'v_c_re': out['v_c_re'], 'v_c_im': out['v_c_im'], 'v_d_skip': out['v_d_skip'], 'v_w_glu': out['v_w_glu'], 'v_b_glu': out['v_b_glu'], 'v_w_up_a': out['v_w_up_a'], 'v_w_up_b': out['v_w_up_b'], 'v_w_out': out['v_w_out'], 'v_g_final': out['v_g_final']}


def _loss(weights, diff, rest, loss_target):
    with _jax.named_scope("forward"):
        args = {**rest, TWIN_DIFF_INPUT: diff, **{k: w.astype(_WEIGHT_DTYPES[k]) for k, w in weights.items()}}
        y = _forward(args)
    with _jax.named_scope("loss_head"):
        err = _jnp.square(y.astype(_jnp.float32) - loss_target)
        return 0.5 * _jnp.sum(_jnp.mean(err, axis=-1)) if err.ndim else 0.5 * err


def _adamw(w, g, m, v):
    m = ADAM_B1 * m + (1.0 - ADAM_B1) * g
    v = ADAM_B2 * v + (1.0 - ADAM_B2) * _jnp.square(g)
    m_hat = m / (1.0 - ADAM_B1 ** ADAM_STEP)
    v_hat = v / (1.0 - ADAM_B2 ** ADAM_STEP)
    delta = -ADAM_LR * (m_hat / (_jnp.sqrt(v_hat) + ADAM_EPS) + ADAM_WD * w)
    return delta, m, v


def reference(x, c, w_ada, b_ada, g_norm, w_in, b_f, a_re, a_im, log_dt, b_re, b_im, c_re, c_im, d_skip, w_glu, b_glu, w_up_a, w_up_b, w_out, g_final, loss_target, m_w_ada, m_b_ada, m_g_norm, m_w_in, m_b_f, m_a_re, m_a_im, m_log_dt, m_b_re, m_b_im, m_c_re, m_c_im, m_d_skip, m_w_glu, m_b_glu, m_w_up_a, m_w_up_b, m_w_out, m_g_final, v_w_ada, v_b_ada, v_g_norm, v_w_in, v_b_f, v_a_re, v_a_im, v_log_dt, v_b_re, v_b_im, v_c_re, v_c_im, v_d_skip, v_w_glu, v_b_glu, v_w_up_a, v_w_up_b, v_w_out, v_g_final):
    given = dict(x=x, c=c, w_ada=w_ada, b_ada=b_ada, g_norm=g_norm, w_in=w_in, b_f=b_f, a_re=a_re, a_im=a_im, log_dt=log_dt, b_re=b_re, b_im=b_im, c_re=c_re, c_im=c_im, d_skip=d_skip, w_glu=w_glu, b_glu=b_glu, w_up_a=w_up_a, w_up_b=w_up_b, w_out=w_out, g_final=g_final, loss_target=loss_target, m_w_ada=m_w_ada, m_b_ada=m_b_ada, m_g_norm=m_g_norm, m_w_in=m_w_in, m_b_f=m_b_f, m_a_re=m_a_re, m_a_im=m_a_im, m_log_dt=m_log_dt, m_b_re=m_b_re, m_b_im=m_b_im, m_c_re=m_c_re, m_c_im=m_c_im, m_d_skip=m_d_skip, m_w_glu=m_w_glu, m_b_glu=m_b_glu, m_w_up_a=m_w_up_a, m_w_up_b=m_w_up_b, m_w_out=m_w_out, m_g_final=m_g_final, v_w_ada=v_w_ada, v_b_ada=v_b_ada, v_g_norm=v_g_norm, v_w_in=v_w_in, v_b_f=v_b_f, v_a_re=v_a_re, v_a_im=v_a_im, v_log_dt=v_log_dt, v_b_re=v_b_re, v_b_im=v_b_im, v_c_re=v_c_re, v_c_im=v_c_im, v_d_skip=v_d_skip, v_w_glu=v_w_glu, v_b_glu=v_b_glu, v_w_up_a=v_w_up_a, v_w_up_b=v_w_up_b, v_w_out=v_w_out, v_g_final=v_g_final)
    weights = {n: given[n] for n in TWIN_WEIGHTS}
    shared = {n: given[n] for n in SHARED_INPUTS}
    per_example = {n: given[n] for n in ['x', 'c']}
    grad_fn = _jax.value_and_grad(_loss, argnums=(0, 1))

    def one_microbatch(ex, loss_target):
        ex = dict(ex)
        diff = ex.pop(TWIN_DIFF_INPUT)
        return grad_fn(weights, diff, {**shared, **ex}, loss_target)

    if N_MICROBATCH == 1:
        loss, (grad_w, grad_x) = one_microbatch(per_example, given["loss_target"])
    else:
        def body(carry, xs):
            loss_sum, grad_sum = carry
            l_k, (gw_k, gx_k) = one_microbatch(xs[0], xs[1])
            with _jax.named_scope("update"):
                return (loss_sum + l_k, _jax.tree.map(_jnp.add, grad_sum, gw_k)), gx_k

        init = (_jnp.zeros((), _jnp.float32), _jax.tree.map(_jnp.zeros_like, weights))
        (loss, grad_w), grad_x = _jax.lax.scan(body, init, (per_example, given["loss_target"]))
    with _jax.named_scope("update"):
        delta_w, new_m, new_v = {}, {}, {}
        for n in TWIN_WEIGHTS:
            delta_w[n], new_m[n], new_v[n] = _adamw(weights[n], grad_w[n], given["m_" + n], given["v_" + n])
    return (loss, grad_x, *[grad_w[n] for n in TWIN_WEIGHTS], *[delta_w[n] for n in TWIN_WEIGHTS],
            *[new_m[n] for n in TWIN_WEIGHTS], *[new_v[n] for n in TWIN_WEIGHTS])
```

```python
import functools
import math

import jax
import jax.numpy as jnp
from jax import lax
from jax.experimental import pallas as pl
from jax.experimental.pallas import tpu as pltpu

F32 = jnp.float32
BF16 = jnp.bfloat16
HI = lax.Precision.HIGHEST
MESH = pl.DeviceIdType.MESH

D_MODEL = 1024
HEADS = 8
HEAD_DIM = 64
FOX_W = 512
S5_W = 512
GROUPS = 32
STATE = 64
GCH = 16
NSTATE = GROUPS * STATE
EPS = 1e-6
NEG = -1e30

ADAM_LR = 0.001
ADAM_B1 = 0.9
ADAM_B2 = 0.999
ADAM_EPS = 1e-08
ADAM_WD = 0.01
ADAM_STEP = 10

VMEM_LIMIT = 56 * 1024 * 1024
LANES = 128

TM = 256
T_ATT = 512
TB_SSM = 256
TK_ACC = 512
TB_CUM = 256

O_Q, O_K, O_V, O_F, O_ZA, O_U, O_ZB, O_GA, O_GB, O_END = 0, 512, 1024, 1536, 1544, 2056, 2568, 3080, 4104, 5128
REST_W = 3712
R_GA, R_GB, R_ZA, R_U, R_ZB, R_F = 0, 1024, 2048, 2560, 3072, 3584


def _cparams(sem=None):
    kw = dict(vmem_limit_bytes=VMEM_LIMIT)
    if sem is not None:
        kw["dimension_semantics"] = sem
    return pltpu.CompilerParams(**kw)


def _const(shape):
    nd = len(shape)
    return pl.BlockSpec(shape, lambda *_: (0,) * nd, pipeline_mode=pl.Buffered(1))


def _dot(a, b, precision=None):
    return jnp.dot(a, b, preferred_element_type=F32, precision=precision)


def _dot_nt(a, b):
    return lax.dot_general(a, b, (((1,), (1,)), ((), ())), preferred_element_type=F32)


def _dot_tn(a, b, precision=None):
    return lax.dot_general(a, b, (((0,), (0,)), ((), ())), preferred_element_type=F32, precision=precision)


def _sigmoid(z):
    return 1.0 / (1.0 + jnp.exp(-z))


def _allgather8(xs, name):
    rows = xs.shape[0]

    def body(x_ref, out_ref, sum_ref, send_sems, recv_sems, local_sem):
        x, y, c = lax.axis_index("x"), lax.axis_index("y"), lax.axis_index("c")
        me, sibling = (x, y, c), (x, y, 1 - c)
        chips = [(1 - x, y), (x, 1 - y), (1 - x, 1 - y)]

        def slot(px, py, pc):
            return out_ref.at[4 * px + 2 * py + pc]

        def copy(k, block, to, src=None):
            return pltpu.make_async_remote_copy(
                src_ref=slot(*block) if src is None else src, dst_ref=slot(*block),
                send_sem=send_sems.at[k], recv_sem=recv_sems.at[k], device_id=to, device_id_type=MESH)

        mine = pltpu.make_async_copy(x_ref, slot(*me), local_sem)
        mine.start()
        first = [copy(0, me, sibling, src=x_ref)]
        first += [copy(1 + j, me, (*chip, c), src=x_ref) for j, chip in enumerate(chips)]
        for cp in first:
            cp.start()
        passed = [copy(4 + j, (*chip, c), sibling) for j, chip in enumerate(chips)]
        for j, chip in enumerate(chips):
            copy(1 + j, (*chip, c), me).wait_recv()
            passed[j].start()
        copy(0, sibling, me).wait_recv()
        for j, chip in enumerate(chips):
            copy(4 + j, (*chip, 1 - c), me).wait_recv()
        for cp in first + passed:
            cp.wait_send()
        mine.wait()
        acc = out_ref[0]
        for d in range(1, 8):
            acc = acc + out_ref[d]
        sum_ref[...] = acc

    return pl.pallas_call(
        body, name=name,
        out_shape=(jax.ShapeDtypeStruct((8, rows, LANES), F32), jax.ShapeDtypeStruct((rows, LANES), F32)),
        in_specs=[pl.BlockSpec(memory_space=pltpu.VMEM)],
        out_specs=(pl.BlockSpec(memory_space=pltpu.VMEM), pl.BlockSpec(memory_space=pltpu.VMEM)),
        scratch_shapes=[pltpu.SemaphoreType.DMA((7,)), pltpu.SemaphoreType.DMA((7,)), pltpu.SemaphoreType.DMA],
        compiler_params=_cparams(),
    )(xs)


def _exchange4(src, scatter, name):
    rows = src.shape[-2]

    def body(src_ref, out_ref, send_sems, recv_sems, local_sem):
        x, y, c = lax.axis_index("x"), lax.axis_index("y"), lax.axis_index("c")
        peers = [(1 - x, y), (x, 1 - y), (1 - x, 1 - y)]

        def block_for(px, py):
            return src_ref.at[2 * px + py] if scatter else src_ref

        def copy(k, px, py, slot):
            return pltpu.make_async_remote_copy(
                src_ref=block_for(px, py), dst_ref=out_ref.at[slot],
                send_sem=send_sems.at[k], recv_sem=recv_sems.at[k],
                device_id=(px, py, c), device_id_type=MESH)

        local = pltpu.make_async_copy(block_for(x, y), out_ref.at[2 * x + y], local_sem)
        local.start()
        sends = [copy(k, px, py, 2 * x + y) for k, (px, py) in enumerate(peers)]
        for cp in sends:
            cp.start()
        for k, (px, py) in enumerate(peers):
            copy(k, px, py, 2 * px + py).wait_recv()
        for cp in sends:
            cp.wait_send()
        local.wait()

    return pl.pallas_call(
        body, name=name,
        out_shape=jax.ShapeDtypeStruct((4, rows, LANES), src.dtype),
        in_specs=[pl.BlockSpec(memory_space=pl.ANY)],
        out_specs=pl.BlockSpec(memory_space=pl.ANY),
        scratch_shapes=[pltpu.SemaphoreType.DMA((3,)), pltpu.SemaphoreType.DMA((3,)), pltpu.SemaphoreType.DMA],
        compiler_params=_cparams(),
    )(src)


def _swap_sibling(src, name):
    def body(src_ref, out_ref, send_sem, recv_sem):
        x, y, c = lax.axis_index("x"), lax.axis_index("y"), lax.axis_index("c")
        cp = pltpu.make_async_remote_copy(
            src_ref=src_ref, dst_ref=out_ref, send_sem=send_sem, recv_sem=recv_sem,
            device_id=(x, y, 1 - c), device_id_type=MESH)
        cp.start()
        cp.wait()

    return pl.pallas_call(
        body, name=name,
        out_shape=jax.ShapeDtypeStruct(src.shape, src.dtype),
        in_specs=[pl.BlockSpec(memory_space=pl.ANY)],
        out_specs=pl.BlockSpec(memory_space=pl.ANY),
        scratch_shapes=[pltpu.SemaphoreType.DMA, pltpu.SemaphoreType.DMA],
        compiler_params=_cparams(),
    )(src)


def _mod_cols(c_all, w, b):
    n = w.shape[1]

    def body(c_ref, w_ref, b_ref, o_ref):
        o_ref[...] = _dot(c_ref[...], w_ref[...], HI) + b_ref[...]

    return pl.pallas_call(
        body, name="mod_cols", out_shape=jax.ShapeDtypeStruct((8, n), F32),
        compiler_params=_cparams())(c_all, w, b)


def _grad_w_ada(c_all, dmod_cols):
    n = dmod_cols.shape[1]

    def body(c_ref, d_ref, o_ref):
        o_ref[...] = _dot_tn(c_ref[...], d_ref[...], HI)

    return pl.pallas_call(
        body, name="grad_w_ada", out_shape=jax.ShapeDtypeStruct((D_MODEL, n), F32),
        compiler_params=_cparams())(c_all, dmod_cols)


def _prenorm(x, gs, shift):
    s = x.shape[0]
    tm = min(TM, s)

    def body(x_ref, gs_ref, sh_ref, h_ref):
        xv = x_ref[...]
        r = lax.rsqrt(jnp.mean(xv * xv, axis=-1, keepdims=True) + EPS)
        h_ref[...] = (xv * r * gs_ref[...] + sh_ref[...]).astype(BF16)

    row = pl.BlockSpec((tm, D_MODEL), lambda i: (i, 0))
    vec = pl.BlockSpec((1, D_MODEL), lambda i: (0, 0))
    return pl.pallas_call(
        body, name="prenorm", grid=(s // tm,), in_specs=[row, vec, vec], out_specs=row,
        out_shape=jax.ShapeDtypeStruct((s, D_MODEL), BF16), compiler_params=_cparams(("parallel",)))(x, gs, shift)


def _mm_rows(a, b, out_dtype, name):
    s, k = a.shape
    n = b.shape[1]
    tm = min(TM, s)

    def body(a_ref, b_ref, o_ref):
        o_ref[...] = _dot(a_ref[...], b_ref[...]).astype(out_dtype)

    return pl.pallas_call(
        body, name=name, grid=(s // tm,),
        in_specs=[pl.BlockSpec((tm, k), lambda i: (i, 0)), _const((k, n))],
        out_specs=pl.BlockSpec((tm, n), lambda i: (i, 0)),
        out_shape=jax.ShapeDtypeStruct((s, n), out_dtype), compiler_params=_cparams(("parallel",)))(a, b)


def _mm_kacc(a_t, b, name):
    m, s = a_t.shape
    n = b.shape[1]
    tk = min(TK_ACC, s)

    def body(a_ref, b_ref, o_ref):
        @pl.when(pl.program_id(0) == 0)
        def _():
            o_ref[...] = jnp.zeros_like(o_ref)

        o_ref[...] += _dot(a_ref[...], b_ref[...])

    return pl.pallas_call(
        body, name=name, grid=(s // tk,),
        in_specs=[pl.BlockSpec((m, tk), lambda k: (0, k)), pl.BlockSpec((tk, n), lambda k: (k, 0))],
        out_specs=pl.BlockSpec((m, n), lambda k: (0, 0)),
        out_shape=jax.ShapeDtypeStruct((m, n), F32), compiler_params=_cparams(("arbitrary",)))(a_t, b)


def _head_pair_selector():
    rows = jnp.arange(LANES)[:, None]
    cols = jnp.arange(4 * LANES)[None, :]
    return ((rows < HEADS) & (cols == (rows // 2) * LANES + rows % 2)).astype(F32)


def _fcum(rest, bf128, selp):
    s = rest.shape[0]
    tb = min(TB_CUM, s)

    def body(fz_ref, bf_ref, sel_ref, fpc_ref, ft_ref, carry_ref):
        @pl.when(pl.program_id(0) == 0)
        def _():
            carry_ref[...] = jnp.zeros_like(carry_ref)

        z = fz_ref[...] + bf_ref[...]
        logf = jnp.minimum(z, 0.0) - jnp.log(1.0 + jnp.exp(-jnp.abs(z)))
        r = lax.broadcasted_iota(jnp.int32, (tb, tb), 0)
        c = lax.broadcasted_iota(jnp.int32, (tb, tb), 1)
        tri = (c <= r).astype(F32)
        f = _dot(tri, logf, HI) + carry_ref[0:1, :]
        carry_ref[0:1, :] = f[tb - 1:tb, :]
        fpc_ref[...] = _dot(f, sel_ref[...], HI)
        ft_ref[...] = jnp.transpose(f)[0:HEADS, :]

    return pl.pallas_call(
        body, name="forget_cumsum", grid=(s // tb,),
        in_specs=[pl.BlockSpec((tb, LANES), lambda i: (i, R_F // LANES)), _const((1, LANES)), _const((LANES, 4 * LANES))],
        out_specs=(pl.BlockSpec((tb, 4 * LANES), lambda i: (i, 0)), pl.BlockSpec((HEADS, tb), lambda i: (0, i))),
        out_shape=(jax.ShapeDtypeStruct((s, 4 * LANES), F32), jax.ShapeDtypeStruct((HEADS, s), F32)),
        scratch_shapes=[pltpu.VMEM((8, LANES), F32)],
        compiler_params=_cparams(("arbitrary",)))(rest, bf128, selp)


def _dfcum(dfk, dfq, rest, bf128, selq):
    s = rest.shape[0]
    tb = min(TB_CUM, s)
    nb = s // tb

    def body(dk_ref, dq_ref, fz_ref, bf_ref, sel_ref, df_ref, dbf_ref, carry_ref):
        @pl.when(pl.program_id(0) == 0)
        def _():
            carry_ref[...] = jnp.zeros_like(carry_ref)
            dbf_ref[...] = jnp.zeros_like(dbf_ref)

        d = _dot(dk_ref[...] + dq_ref[...], sel_ref[...], HI)
        r = lax.broadcasted_iota(jnp.int32, (tb, tb), 0)
        c = lax.broadcasted_iota(jnp.int32, (tb, tb), 1)
        triu = (c >= r).astype(F32)
        dlogf = _dot(triu, d, HI) + carry_ref[0:1, :]
        carry_ref[0:1, :] = dlogf[0:1, :]
        z = fz_ref[...] + bf_ref[...]
        df = dlogf * (1.0 / (1.0 + jnp.exp(z)))
        df_ref[...] = df.astype(BF16)
        dbf_ref[0:1, :] += jnp.sum(df, axis=0, keepdims=True)

    return pl.pallas_call(
        body, name="forget_grad", grid=(nb,),
        in_specs=[pl.BlockSpec((tb, 4 * LANES), lambda i: (nb - 1 - i, 0)),
                  pl.BlockSpec((tb, 4 * LANES), lambda i: (nb - 1 - i, 0)),
                  pl.BlockSpec((tb, LANES), lambda i: (nb - 1 - i, R_F // LANES)),
                  _const((1, LANES)), _const((4 * LANES, LANES))],
        out_specs=(pl.BlockSpec((tb, LANES), lambda i: (nb - 1 - i, 0)), pl.BlockSpec((8, LANES), lambda i: (0, 0))),
        out_shape=(jax.ShapeDtypeStruct((s, LANES), BF16), jax.ShapeDtypeStruct((8, LANES), F32)),
        scratch_shapes=[pltpu.VMEM((8, LANES), F32)],
        compiler_params=_cparams(("arbitrary",)))(dfk, dfq, rest, bf128, selq)


def _scaled(q):
    return (q.astype(F32) * (HEAD_DIM ** -0.5)).astype(BF16)


def _attn_fwd(qkv, frow5, fpc):
    s = qkv.shape[0]
    t = min(T_ATT, s)
    n = s // t

    def body(q_ref, k_ref, v_ref, fr_ref, fc_ref, o_ref, lse_ref):
        i = pl.program_id(1)
        lane = lax.broadcasted_iota(jnp.int32, (t, LANES), 1)
        first = lane < HEAD_DIM
        q = _scaled(q_ref[...])
        zq = jnp.zeros_like(q)
        qm = (jnp.where(first, q, zq), jnp.where(first, zq, q))
        fq = (fc_ref[:, 0:1], fc_ref[:, 1:2])
        rq = lax.broadcasted_iota(jnp.int32, (t, t), 0)
        ck = lax.broadcasted_iota(jnp.int32, (t, t), 1)

        def step(j, carry, masked):
            m, l, acc = carry
            r0 = pl.multiple_of(j * t, t)
            kb = k_ref[pl.ds(r0, t), :]
            vb = v_ref[pl.ds(r0, t), :]
            zv = jnp.zeros_like(vb)
            vm = (jnp.where(first, vb, zv), jnp.where(first, zv, vb))
            m_out, l_out, alphas, pv = [], [], [], []
            for hh in range(2):
                fk = fr_ref[0, hh, j]
                sc = _dot_nt(qm[hh], kb) + (fq[hh] - fk)
                if masked:
                    sc = jnp.where(ck <= rq, sc, NEG)
                m_new = jnp.maximum(m[hh], jnp.max(sc, axis=1, keepdims=True))
                alpha = jnp.exp(m[hh] - m_new)
                p = jnp.exp(sc - m_new)
                l_out.append(alpha * l[hh] + jnp.sum(p, axis=1, keepdims=True))
                m_out.append(m_new)
                alphas.append(alpha)
                pv.append(_dot(p.astype(BF16), vm[hh]))
            acc = acc * jnp.where(first, alphas[0], alphas[1]) + pv[0] + pv[1]
            return tuple(m_out), tuple(l_out), acc

        col = jnp.full((t, 1), NEG, F32)
        zero = jnp.zeros((t, 1), F32)
        carry = ((col, col), (zero, zero), jnp.zeros((t, LANES), F32))
        carry = lax.fori_loop(0, i, lambda j, cr: step(j, cr, False), carry)
        m, l, acc = step(i, carry, True)
        o_ref[...] = acc * jnp.where(first, 1.0 / l[0], 1.0 / l[1])
        lse0 = m[0] + jnp.log(l[0])
        lse1 = m[1] + jnp.log(l[1])
        lse_ref[...] = jnp.where(lane == 0, lse0, jnp.where(lane == 1, lse1, 0.0))

    return pl.pallas_call(
        body, name="attn_fwd", grid=(4, n),
        in_specs=[pl.BlockSpec((t, LANES), lambda h, i: (i, h)),
                  pl.BlockSpec((s, LANES), lambda h, i: (0, 4 + h)),
                  pl.BlockSpec((s, LANES), lambda h, i: (0, 8 + h)),
                  pl.BlockSpec((1, 2, n, 1, t), lambda h, i: (h, 0, 0, 0, 0)),
                  pl.BlockSpec((t, LANES), lambda h, i: (i, h))],
        out_specs=(pl.BlockSpec((t, LANES), lambda h, i: (i, h)), pl.BlockSpec((t, LANES), lambda h, i: (i, h))),
        out_shape=(jax.ShapeDtypeStruct((s, FOX_W), F32), jax.ShapeDtypeStruct((s, 4 * LANES), F32)),
        compiler_params=_cparams(("parallel", "arbitrary")))(qkv, qkv, qkv, frow5, fpc)


def _attn_bwd(qkv, do, lse5, dlt5, frow5, fpc):
    s = qkv.shape[0]
    t = min(T_ATT, s)
    n = s // t
    wide = 2 * LANES

    def body(q_ref, do_ref, k_ref, v_ref, lse_ref, dl_ref, fr_ref, fc_ref,
             dq_ref, dk_ref, dv_ref, dfk_ref, dfq_ref, dq_acc):
        j = pl.program_id(1)

        @pl.when(j == 0)
        def _():
            dq_acc[...] = jnp.zeros_like(dq_acc)

        lane = lax.broadcasted_iota(jnp.int32, (t, LANES), 1)
        first = lane < HEAD_DIM
        ones_col = ((lane == 0).astype(BF16), (lane == 1).astype(BF16))
        kb = k_ref[...]
        vb = v_ref[...]
        zk = jnp.zeros_like(kb)
        kaug = (jnp.concatenate([jnp.where(first, kb, zk), ones_col[0]], axis=1),
                jnp.concatenate([jnp.where(first, zk, kb), ones_col[1]], axis=1))
        fk = (fc_ref[:, 0:1], fc_ref[:, 1:2])
        rk = lax.broadcasted_iota(jnp.int32, (t, t), 0)
        cq = lax.broadcasted_iota(jnp.int32, (t, t), 1)

        def step(i, carry, masked):
            dk, dv = carry
            r0 = pl.multiple_of(i * t, t)
            qb = _scaled(q_ref[pl.ds(r0, t), :])
            dob = do_ref[pl.ds(r0, t), :]
            zq = jnp.zeros_like(qb)
            qm = (jnp.where(first, qb, zq), jnp.where(first, zq, qb))
            dom = (jnp.where(first, dob, zq), jnp.where(first, zq, dob))
            dq_add = jnp.zeros((t, wide), F32)
            for hh in range(2):
                lse = lse_ref[0, hh, i]
                dl = dl_ref[0, hh, i]
                fq = fr_ref[0, hh, i]
                st = _dot_nt(kb, qm[hh]) + (fq - fk[hh])
                if masked:
                    st = jnp.where(rk <= cq, st, NEG)
                pt = jnp.exp(st - lse)
                dpt = _dot_nt(vb, dom[hh])
                dsb = (pt * (dpt - dl)).astype(BF16)
                dv = dv + _dot(pt.astype(BF16), dom[hh])
                dk = dk + _dot(dsb, jnp.concatenate([qm[hh], ones_col[hh]], axis=1))
                dq_add = dq_add + _dot_tn(dsb, kaug[hh])
            dq_acc[pl.ds(r0, t), :] += dq_add
            return dk, dv

        carry = step(j, (jnp.zeros((t, wide), F32), jnp.zeros((t, LANES), F32)), True)
        dk, dv = lax.fori_loop(j + 1, n, lambda i, cr: step(i, cr, False), carry)
        dk_ref[...] = dk[:, 0:LANES].astype(BF16)
        dv_ref[...] = dv.astype(BF16)
        dfk_ref[...] = -dk[:, LANES:wide]

        @pl.when(j == n - 1)
        def _():
            dq_ref[...] = (dq_acc[:, 0:LANES] * (HEAD_DIM ** -0.5)).astype(BF16)
            dfq_ref[...] = dq_acc[:, LANES:wide]

    stat = pl.BlockSpec((1, 2, n, 1, t), lambda h, j: (h, 0, 0, 0, 0))
    blk = pl.BlockSpec((t, LANES), lambda h, j: (j, h))
    full = pl.BlockSpec((s, LANES), lambda h, j: (0, h))
    return pl.pallas_call(
        body, name="attn_bwd", grid=(4, n),
        in_specs=[full, full,
                  pl.BlockSpec((t, LANES), lambda h, j: (j, 4 + h)),
                  pl.BlockSpec((t, LANES), lambda h, j: (j, 8 + h)),
                  stat, stat, stat, blk],
        out_specs=(full, blk, blk, blk, full),
        out_shape=(jax.ShapeDtypeStruct((s, FOX_W), BF16), jax.ShapeDtypeStruct((s, FOX_W), BF16),
                   jax.ShapeDtypeStruct((s, FOX_W), BF16), jax.ShapeDtypeStruct((s, 4 * LANES), F32),
                   jax.ShapeDtypeStruct((s, 4 * LANES), F32)),
        scratch_shapes=[pltpu.VMEM((s, wide), F32)],
        compiler_params=_cparams(("parallel", "arbitrary")))(qkv, do, qkv, qkv, lse5, dlt5, frow5, fpc)


def _ssm_block_params(a_re, a_im, log_dt, b_re, b_im):
    dt = jnp.exp(log_dt)[:, None]
    mag = jnp.exp(a_re * dt)
    ar = mag * jnp.cos(a_im * dt)
    ai = mag * jnp.sin(a_im * dt)
    den = a_re * a_re + a_im * a_im
    nr = ar - 1.0
    cr = (nr * a_re + ai * a_im) / den
    ci = (ai * a_re - nr * a_im) / den
    bbr = cr[:, :, None] * b_re - ci[:, :, None] * b_im
    bbi = cr[:, :, None] * b_im + ci[:, :, None] * b_re
    return ar, ai, bbr, bbi


def _block_diag(blocks):
    g, r, c = blocks.shape
    eye = jnp.eye(g, dtype=blocks.dtype)
    return (blocks[:, :, None, :] * eye[:, None, :, None]).reshape(g * r, g * c)


def _diag_blocks(dense, r, c):
    g = dense.shape[0] // r
    eye = jnp.eye(g, dtype=dense.dtype)
    return (dense.reshape(g, r, g, c) * eye[:, None, :, None]).sum(axis=2)


def _scan_consts(a_re, a_im, log_dt, reverse):
    dt = jnp.exp(log_dt)[:, None]
    lr = (a_re * dt).reshape(1, NSTATE)
    li = (a_im * dt).reshape(1, NSTATE)
    if reverse:
        li = -li
    rows = jnp.arange(8, dtype=F32)[:, None]

    def power(k):
        mag = jnp.exp(k * lr)
        return mag * jnp.cos(k * li), mag * jnp.sin(k * li)

    tiles = []
    for k in (1, 2, 4):
        keep = (rows < 8 - k) if reverse else (rows >= k)
        pr, pi_ = power(float(k))
        tiles += [jnp.where(keep, pr, 0.0), jnp.where(keep, pi_, 0.0)]
    expo = (8.0 - rows) if reverse else (rows + 1.0)
    tiles += list(power(expo))
    return jnp.stack([jnp.broadcast_to(tl, (8, NSTATE)) for tl in tiles])


_SCAN_W = 512


def _ssm_fwd(rest, bd, cd, consts):
    s = rest.shape[0]
    tb = min(TB_SSM, s)
    ns2 = 2 * NSTATE

    def body(u_ref, bd_ref, cd_ref, cf_ref, y_ref, x_ref, cb_ref):
        @pl.when(pl.program_id(0) == 0)
        def _():
            cb_ref[...] = jnp.zeros_like(cb_ref)

        x_ref[...] = _dot(u_ref[...].astype(BF16), bd_ref[...])

        def tile(ti, _):
            r0 = pl.multiple_of(ti * 8, 8)
            for cc in range(NSTATE // _SCAN_W):
                cr = pl.ds(cc * _SCAN_W, _SCAN_W)
                ci = pl.ds(NSTATE + cc * _SCAN_W, _SCAN_W)
                re = x_ref[pl.ds(r0, 8), cr]
                im = x_ref[pl.ds(r0, 8), ci]
                for n_, k in enumerate((1, 2, 4)):
                    ar = cf_ref[2 * n_, :, cr]
                    ai = cf_ref[2 * n_ + 1, :, cr]
                    sr = pltpu.roll(re, k, 0)
                    si = pltpu.roll(im, k, 0)
                    re, im = re + ar * sr - ai * si, im + ar * si + ai * sr
                pr = cf_ref[6, :, cr]
                pi_ = cf_ref[7, :, cr]
                cbr = cb_ref[:, cr]
                cbi = cb_ref[:, ci]
                re, im = re + pr * cbr - pi_ * cbi, im + pr * cbi + pi_ * cbr
                x_ref[pl.ds(r0, 8), cr] = re
                x_ref[pl.ds(r0, 8), ci] = im
                cb_ref[:, cr] = jnp.broadcast_to(re[7:8, :], (8, _SCAN_W))
                cb_ref[:, ci] = jnp.broadcast_to(im[7:8, :], (8, _SCAN_W))
            return 0

        lax.fori_loop(0, tb // 8, tile, 0)
        y_ref[...] = _dot(x_ref[...].astype(BF16), cd_ref[...])

    return pl.pallas_call(
        body, name="ssm_fwd", grid=(s // tb,),
        in_specs=[pl.BlockSpec((tb, S5_W), lambda i: (i, R_U // S5_W)), _const((S5_W, ns2)), _const((ns2, S5_W)),
                  _const((8, 8, NSTATE))],
        out_specs=(pl.BlockSpec((tb, S5_W), lambda i: (i, 0)), pl.BlockSpec((tb, ns2), lambda i: (i, 0))),
        out_shape=(jax.ShapeDtypeStruct((s, S5_W), F32), jax.ShapeDtypeStruct((s, ns2), F32)),
        scratch_shapes=[pltpu.VMEM((8, ns2), F32)],
        compiler_params=_cparams(("arbitrary",)))(rest, bd, cd, consts)


def _ssm_bwd(dys, xs, rest, bd_t, cd_t, consts, dskip):
    s = dys.shape[0]
    tb = min(TB_SSM, s)
    nb = s // tb
    ns2 = 2 * NSTATE
    nt = tb // 8

    def body(dy_ref, x_ref, u_ref, bdt_ref, cdt_ref, cf_ref, dsk_ref, du_ref, dbd_hbm, dcd_hbm, da_ref,
             g_ref, cb_ref, acc_b, acc_c):
        step = pl.program_id(0)

        @pl.when(step == 0)
        def _():
            cb_ref[...] = jnp.zeros_like(cb_ref)
            acc_b[...] = jnp.zeros_like(acc_b)
            acc_c[...] = jnp.zeros_like(acc_c)
            da_ref[...] = jnp.zeros_like(da_ref)

        dy = dy_ref[...]
        dyb = dy.astype(BF16)
        g_ref[...] = _dot(dyb, cdt_ref[...])
        last_row = lax.broadcasted_iota(jnp.int32, (8, _SCAN_W), 0) == 7

        def tile(tt, _):
            r0 = pl.multiple_of((nt - 1 - tt) * 8, 8)
            for cc in range(NSTATE // _SCAN_W):
                cr = pl.ds(cc * _SCAN_W, _SCAN_W)
                ci = pl.ds(NSTATE + cc * _SCAN_W, _SCAN_W)
                re = g_ref[pl.ds(r0, 8), cr]
                im = g_ref[pl.ds(r0, 8), ci]
                for n_, k in enumerate((1, 2, 4)):
                    ar = cf_ref[2 * n_, :, cr]
                    ai = cf_ref[2 * n_ + 1, :, cr]
                    sr = pltpu.roll(re, 8 - k, 0)
                    si = pltpu.roll(im, 8 - k, 0)
                    re, im = re + ar * sr - ai * si, im + ar * si + ai * sr
                pr = cf_ref[6, :, cr]
                pi_ = cf_ref[7, :, cr]
                cbr = cb_ref[:, cr]
                cbi = cb_ref[:, ci]
                re, im = re + pr * cbr - pi_ * cbi, im + pr * cbi + pi_ * cbr
                g_ref[pl.ds(r0, 8), cr] = re
                g_ref[pl.ds(r0, 8), ci] = im
                gnr = jnp.where(last_row, cbr, pltpu.roll(re, 7, 0))
                gni = jnp.where(last_row, cbi, pltpu.roll(im, 7, 0))
                xr = x_ref[pl.ds(r0, 8), cr]
                xi = x_ref[pl.ds(r0, 8), ci]
                da_ref[:, cr] += gnr * xr + gni * xi
                da_ref[:, ci] += gni * xr - gnr * xi
                cb_ref[:, cr] = jnp.broadcast_to(re[0:1, :], (8, _SCAN_W))
                cb_ref[:, ci] = jnp.broadcast_to(im[0:1, :], (8, _SCAN_W))
            return 0

        lax.fori_loop(0, nt, tile, 0)
        gb = g_ref[...].astype(BF16)
        du_ref[...] = (_dot(gb, bdt_ref[...]) + dy * dsk_ref[...]).astype(BF16)
        acc_b[...] += _dot_tn(u_ref[...].astype(BF16), gb)
        acc_c[...] += _dot_tn(dyb, x_ref[...].astype(BF16))

        @pl.when(step == nb - 1)
        def _():
            pltpu.sync_copy(acc_b, dbd_hbm)
            pltpu.sync_copy(acc_c, dcd_hbm)

    rev = lambda i: (nb - 1 - i, 0)
    return pl.pallas_call(
        body, name="ssm_bwd", grid=(nb,),
        in_specs=[pl.BlockSpec((tb, S5_W), rev), pl.BlockSpec((tb, ns2), rev),
                  pl.BlockSpec((tb, S5_W), lambda i: (nb - 1 - i, R_U // S5_W)),
                  _const((ns2, S5_W)), _const((S5_W, ns2)), _const((8, 8, NSTATE)), _const((1, S5_W))],
        out_specs=(pl.BlockSpec((tb, S5_W), rev), pl.BlockSpec(memory_space=pl.ANY), pl.BlockSpec(memory_space=pl.ANY),
                   pl.BlockSpec((8, ns2), lambda i: (0, 0))),
        out_shape=(jax.ShapeDtypeStruct((s, S5_W), BF16), jax.ShapeDtypeStruct((S5_W, ns2), F32),
                   jax.ShapeDtypeStruct((S5_W, ns2), F32), jax.ShapeDtypeStruct((8, ns2), F32)),
        scratch_shapes=[pltpu.VMEM((tb, ns2), F32), pltpu.VMEM((8, ns2), F32),
                        pltpu.VMEM((S5_W, ns2), F32), pltpu.VMEM((S5_W, ns2), F32)],
        compiler_params=_cparams(("arbitrary",)))(dys, xs, rest, bd_t, cd_t, consts, dskip)


_GELU_C = math.sqrt(2.0 / math.pi)
_GELU_A = 0.044715


def _mid(o, rest, ys0, x, tgt, w, w_t, vec, hsel):
    s = o.shape[0]
    tm = min(TM, s)
    nsteps = s // tm
    half = FOX_W

    def body(o_ref, ga_ref, gb_ref, za_ref, u_ref, zb_ref, ys0_ref, x_ref, t_ref,
             wglu_ref, wua_ref, wub_ref, wout_ref, wglut_ref, wuat_ref, wubt_ref, woutt_ref, vec_ref, hsel_ref,
             dx2_ref, dga_ref, dgb_ref, do_ref, dza_ref, dzb_ref, dys_ref, dlt_ref,
             gout_hbm, gua_hbm, gub_hbm, gglu_hbm, vout_ref,
             a_out, a_ua, a_ub, a_glu):
        step = pl.program_id(0)

        @pl.when(step == 0)
        def _():
            a_out[...] = jnp.zeros_like(a_out)
            a_ua[...] = jnp.zeros_like(a_ua)
            a_ub[...] = jnp.zeros_like(a_ub)
            a_glu[...] = jnp.zeros_like(a_glu)
            vout_ref[...] = jnp.zeros_like(vout_ref)

        gate = vec_ref[0:1, :]
        gfin = vec_ref[1:2, :]
        dsk = vec_ref[2:3, 0:half]
        bglu = vec_ref[2:3, half:2 * half]

        o_v = o_ref[...]
        za = za_ref[...]
        sza = _sigmoid(za)
        silu_za = za * sza
        ya_b = (o_v * silu_za).astype(BF16)
        u_v = u_ref[...]
        ys = ys0_ref[...] + dsk * u_v
        inner = _GELU_C * (ys + _GELU_A * ys * ys * ys)
        th = jnp.tanh(inner)
        yg = 0.5 * ys * (1.0 + th)
        yg_b = yg.astype(BF16)
        st = _sigmoid(_dot(yg_b, wglu_ref[...]) + bglu)
        yb1 = yg * st
        zb = zb_ref[...]
        szb = _sigmoid(zb)
        silu_zb = zb * szb
        yb_b = (yb1 * silu_zb).astype(BF16)
        ua = _dot(ya_b, wua_ref[...])
        ub = _dot(yb_b, wub_ref[...])
        sga = _sigmoid(ga_ref[...])
        sgb = _sigmoid(gb_ref[...])
        merged_b = (sga * ua + sgb * ub).astype(BF16)
        mo = _dot(merged_b, wout_ref[...])
        x2 = x_ref[...] + gate * mo
        r2 = lax.rsqrt(jnp.mean(x2 * x2, axis=-1, keepdims=True) + EPS)
        x2n = x2 * r2
        diff = x2n * gfin - t_ref[...]
        loss = 0.5 * jnp.sum(jnp.mean(diff * diff, axis=-1, keepdims=True), axis=0, keepdims=True)
        dy = diff * (1.0 / D_MODEL)
        dx2n = dy * gfin
        dx2 = r2 * (dx2n - x2n * jnp.mean(dx2n * x2n, axis=-1, keepdims=True))
        dx2_ref[...] = dx2
        vout_ref[0:1, :] += jnp.sum(dy * x2n, axis=0, keepdims=True)
        vout_ref[1:2, :] += jnp.sum(dx2 * mo, axis=0, keepdims=True)
        vout_ref[3:4, :] += jnp.broadcast_to(loss, (1, D_MODEL))
        dmo_b = (dx2 * gate).astype(BF16)
        dmerged = _dot(dmo_b, woutt_ref[...])
        a_out[...] += _dot_tn(merged_b, dmo_b)
        dua_b = (dmerged * sga).astype(BF16)
        dub_b = (dmerged * sgb).astype(BF16)
        dga_ref[...] = (dmerged * ua * sga * (1.0 - sga)).astype(BF16)
        dgb_ref[...] = (dmerged * ub * sgb * (1.0 - sgb)).astype(BF16)
        dya = _dot(dua_b, wuat_ref[...])
        dyb = _dot(dub_b, wubt_ref[...])
        a_ua[...] += _dot_tn(ya_b, dua_b)
        a_ub[...] += _dot_tn(yb_b, dub_b)
        do_b = (dya * silu_za).astype(BF16)
        do_ref[...] = do_b
        dza_ref[...] = (dya * o_v * (sza * (1.0 + za * (1.0 - sza)))).astype(BF16)
        dlt_ref[...] = lax.dot_general(hsel_ref[...], do_b.astype(F32) * o_v, (((1,), (1,)), ((), ())),
                                       preferred_element_type=F32, precision=HI)
        dyb1 = dyb * silu_zb
        dzb_ref[...] = (dyb * yb1 * (szb * (1.0 + zb * (1.0 - szb)))).astype(BF16)
        dt = dyb1 * yg * st * (1.0 - st)
        dt_b = dt.astype(BF16)
        dyg = dyb1 * st + _dot(dt_b, wglut_ref[...])
        a_glu[...] += _dot_tn(yg_b, dt_b)
        dgelu = 0.5 * (1.0 + th) + 0.5 * ys * (1.0 - th * th) * _GELU_C * (1.0 + 3.0 * _GELU_A * ys * ys)
        dys = dyg * dgelu
        dys_ref[...] = dys
        vout_ref[2:3, 0:half] += jnp.sum(dys * u_v, axis=0, keepdims=True)
        vout_ref[2:3, half:2 * half] += jnp.sum(dt, axis=0, keepdims=True)

        @pl.when(step == nsteps - 1)
        def _():
            pltpu.sync_copy(a_out, gout_hbm)
            pltpu.sync_copy(a_ua, gua_hbm)
            pltpu.sync_copy(a_ub, gub_hbm)
            pltpu.sync_copy(a_glu, gglu_hbm)

    def rows(width, col=0):
        return pl.BlockSpec((tm, width), lambda i, col=col: (i, col))

    anyspace = pl.BlockSpec(memory_space=pl.ANY)
    wshapes = [(S5_W, S5_W), (FOX_W, D_MODEL), (S5_W, D_MODEL), (D_MODEL, D_MODEL)]
    return pl.pallas_call(
        body, name="mid", grid=(nsteps,),
        in_specs=[rows(FOX_W), rows(D_MODEL, R_GA // D_MODEL), rows(D_MODEL, R_GB // D_MODEL),
                  rows(FOX_W, R_ZA // FOX_W), rows(S5_W, R_U // S5_W), rows(S5_W, R_ZB // S5_W),
                  rows(S5_W), rows(D_MODEL), rows(D_MODEL)]
                 + [_const(sh) for sh in wshapes] + [_const(sh[::-1]) for sh in wshapes]
                 + [_const((8, D_MODEL)), _const((HEADS, FOX_W))],
        out_specs=(rows(D_MODEL), rows(D_MODEL), rows(D_MODEL), rows(FOX_W), rows(FOX_W), rows(S5_W), rows(S5_W),
                   pl.BlockSpec((HEADS, tm), lambda i: (0, i)),
                   anyspace, anyspace, anyspace, anyspace, pl.BlockSpec((8, D_MODEL), lambda i: (0, 0))),
        out_shape=(jax.ShapeDtypeStruct((s, D_MODEL), F32), jax.ShapeDtypeStruct((s, D_MODEL), BF16),
                   jax.ShapeDtypeStruct((s, D_MODEL), BF16), jax.ShapeDtypeStruct((s, FOX_W), BF16),
                   jax.ShapeDtypeStruct((s, FOX_W), BF16), jax.ShapeDtypeStruct((s, S5_W), BF16),
                   jax.ShapeDtypeStruct((s, S5_W), F32), jax.ShapeDtypeStruct((HEADS, s), F32),
                   jax.ShapeDtypeStruct((D_MODEL, D_MODEL), F32), jax.ShapeDtypeStruct((FOX_W, D_MODEL), F32),
                   jax.ShapeDtypeStruct((S5_W, D_MODEL), F32), jax.ShapeDtypeStruct((S5_W, S5_W), F32),
                   jax.ShapeDtypeStruct((8, D_MODEL), F32)),
        scratch_shapes=[pltpu.VMEM((D_MODEL, D_MODEL), F32), pltpu.VMEM((FOX_W, D_MODEL), F32),
                        pltpu.VMEM((S5_W, D_MODEL), F32), pltpu.VMEM((S5_W, S5_W), F32)],
        compiler_params=_cparams(("arbitrary",)),
    )(o, rest, rest, rest, rest, rest, ys0, x, tgt, *w, *w_t, vec, hsel)


def _dh(dq, dk, dv, dga, dgb, dza, du, dzb, df, wqkv_t, wrest_t, x, dx2, gs):
    s = x.shape[0]
    tm = min(TM, s)

    def body(dq_ref, dk_ref, dv_ref, dga_ref, dgb_ref, dza_ref, du_ref, dzb_ref, df_ref, wq_ref, wr_ref,
             x_ref, dx2_ref, gs_ref, gx_ref, vout_ref):
        @pl.when(pl.program_id(0) == 0)
        def _():
            vout_ref[...] = jnp.zeros_like(vout_ref)

        dh = _dot(dq_ref[...], wq_ref[0:512, :])
        dh += _dot(dk_ref[...], wq_ref[512:1024, :])
        dh += _dot(dv_ref[...], wq_ref[1024:1536, :])
        dh += _dot(dga_ref[...], wr_ref[R_GA:R_GB, :])
        dh += _dot(dgb_ref[...], wr_ref[R_GB:R_ZA, :])
        dh += _dot(dza_ref[...], wr_ref[R_ZA:R_U, :])
        dh += _dot(du_ref[...], wr_ref[R_U:R_ZB, :])
        dh += _dot(dzb_ref[...], wr_ref[R_ZB:R_F, :])
        dh += _dot(df_ref[...], wr_ref[R_F:REST_W, :])
        xv = x_ref[...]
        r = lax.rsqrt(jnp.mean(xv * xv, axis=-1, keepdims=True) + EPS)
        xn = xv * r
        dxn = dh * gs_ref[...]
        gx_ref[...] = dx2_ref[...] + r * (dxn - xn * jnp.mean(dxn * xn, axis=-1, keepdims=True))
        vout_ref[0:1, :] += jnp.sum(dh * xn, axis=0, keepdims=True)
        vout_ref[1:2, :] += jnp.sum(dh, axis=0, keepdims=True)

    def rows(width):
        return pl.BlockSpec((tm, width), lambda i: (i, 0))

    return pl.pallas_call(
        body, name="dh", grid=(s // tm,),
        in_specs=[rows(512), rows(512), rows(512), rows(1024), rows(1024), rows(512), rows(512), rows(512), rows(128),
                  _const((1536, D_MODEL)), _const((REST_W, D_MODEL)), rows(D_MODEL), rows(D_MODEL), _const((1, D_MODEL))],
        out_specs=(rows(D_MODEL), pl.BlockSpec((8, D_MODEL), lambda i: (0, 0))),
        out_shape=(jax.ShapeDtypeStruct((s, D_MODEL), F32), jax.ShapeDtypeStruct((8, D_MODEL), F32)),
        compiler_params=_cparams(("arbitrary",)),
    )(dq, dk, dv, dga, dgb, dza, du, dzb, df, wqkv_t, wrest_t, x, dx2, gs)


def _sum4(parts):
    rows = parts.shape[1]
    br = BIG_ROW_PAD

    def body(p_ref, o_ref):
        acc = p_ref[0].astype(F32)
        for k in range(1, 4):
            acc = acc + p_ref[k].astype(F32)
        o_ref[...] = acc

    return pl.pallas_call(
        body, name="sum4", grid=(rows // br,),
        in_specs=[pl.BlockSpec((4, br, LANES), lambda i: (0, i, 0))],
        out_specs=pl.BlockSpec((br, LANES), lambda i: (i, 0)),
        out_shape=jax.ShapeDtypeStruct((rows, LANES), F32), compiler_params=_cparams(("parallel",)))(parts)


def _row_block(rows):
    for cand in (512, 256, 128, 64, 32, 16, 8):
        if rows % cand == 0:
            return cand
    return rows


def _add2(a, b):
    rows, cols = a.shape
    br = _row_block(rows)

    def body(a_ref, b_ref, o_ref):
        o_ref[...] = a_ref[...] + b_ref[...]

    spec = pl.BlockSpec((br, cols), lambda i: (i, 0))
    return pl.pallas_call(
        body, name="add2", grid=(rows // br,), in_specs=[spec, spec], out_specs=spec,
        out_shape=jax.ShapeDtypeStruct((rows, cols), F32), compiler_params=_cparams(("parallel",)))(a, b)


def _adamw(w, g, m, v, name):
    rows, cols = w.shape
    br = _row_block(rows)

    def body(w_ref, g_ref, m_ref, v_ref, d_ref, nm_ref, nv_ref):
        gv = g_ref[...]
        nm = ADAM_B1 * m_ref[...] + (1.0 - ADAM_B1) * gv
        nv = ADAM_B2 * v_ref[...] + (1.0 - ADAM_B2) * (gv * gv)
        m_hat = nm / (1.0 - ADAM_B1 ** ADAM_STEP)
        v_hat = nv / (1.0 - ADAM_B2 ** ADAM_STEP)
        d_ref[...] = -ADAM_LR * (m_hat / (jnp.sqrt(v_hat) + ADAM_EPS) + ADAM_WD * w_ref[...])
        nm_ref[...] = nm
        nv_ref[...] = nv

    spec = pl.BlockSpec((br, cols), lambda i: (i, 0))
    shape = jax.ShapeDtypeStruct((rows, cols), F32)
    return pl.pallas_call(
        body, name=name, grid=(rows // br,), in_specs=[spec] * 4, out_specs=(spec,) * 3,
        out_shape=(shape,) * 3, compiler_params=_cparams(("parallel",)))(w, g, m, v)


def _pack(parts):
    flat = []
    for p in parts:
        v = p.reshape(-1).astype(F32)
        pad = (-v.shape[0]) % LANES
        if pad:
            v = jnp.concatenate([v, jnp.zeros((pad,), F32)])
        flat.append(v)
    v = jnp.concatenate(flat)
    rows = v.shape[0] // LANES
    pad_rows = (-rows) % 8
    if pad_rows:
        v = jnp.concatenate([v, jnp.zeros((pad_rows * LANES,), F32)])
    return v.reshape(-1, LANES)


BIG_ROW_PAD = 1024


def _pack_big(parts, dtype):
    v = jnp.concatenate([p.reshape(-1) for p in parts]).astype(dtype)
    pad = (-v.shape[0]) % (BIG_ROW_PAD * LANES)
    v = jnp.concatenate([v, jnp.zeros((pad,), dtype)])
    return v.reshape(-1, LANES)


def _unpack(packed, shapes):
    lead = packed.shape[:-2]
    flat = packed.reshape(lead + (-1,))
    out, off = [], 0
    for sh in shapes:
        size = math.prod(sh)
        out.append(flat[..., off:off + size].reshape(lead + tuple(sh)))
        off += size + (-size) % LANES
    return out


def kernel(x, c, w_ada, b_ada, g_norm, w_in, b_f, a_re, a_im, log_dt, b_re, b_im, c_re, c_im, d_skip, w_glu, b_glu, w_up_a, w_up_b, w_out, g_final, loss_target, m_w_ada, m_b_ada, m_g_norm, m_w_in, m_b_f, m_a_re, m_a_im, m_log_dt, m_b_re, m_b_im, m_c_re, m_c_im, m_d_skip, m_w_glu, m_b_glu, m_w_up_a, m_w_up_b, m_w_out, m_g_final, v_w_ada, v_b_ada, v_g_norm, v_w_in, v_b_f, v_a_re, v_a_im, v_log_dt, v_b_re, v_b_im, v_c_re, v_c_im, v_d_skip, v_w_glu, v_b_glu, v_w_up_a, v_w_up_b, v_w_out, v_g_final):
    xi, yi, ci = lax.axis_index("x"), lax.axis_index("y"), lax.axis_index("c")
    chip = 2 * xi + yi
    me = 4 * xi + 2 * yi + ci
    s = x.shape[1]
    x2d = x[0]
    tgt = loss_target[0]
    n_att = s // min(T_ATT, s)
    t_att = min(T_ATT, s)

    c_all, _ = _allgather8(c.reshape(8, LANES), "gather_c")
    c_all = c_all.reshape(8, D_MODEL)
    ncol = w_ada.shape[2]
    b_cols = lax.dynamic_slice_in_dim(b_ada, chip * ncol, ncol, axis=1)
    mod_cols = _mod_cols(c_all, w_ada[0], b_cols)
    mod_all, _ = _allgather8(mod_cols.reshape(-1, LANES), "gather_mod")
    mod_all = mod_all.reshape(4, 2, 8, ncol)[:, 0]
    mod_me = lax.dynamic_index_in_dim(mod_all, me, axis=1, keepdims=False).reshape(1, 3 * D_MODEL)
    shift, scale, gate = mod_me[:, :D_MODEL], mod_me[:, D_MODEL:2 * D_MODEL], mod_me[:, 2 * D_MODEL:]
    gs = g_norm * (1.0 + scale)

    shard_shapes = [w_in.shape[1:], w_glu.shape[1:], w_up_a.shape[1:], w_up_b.shape[1:], w_out.shape[1:]]
    wpack = _pack_big((w_in, w_glu, w_up_a, w_up_b, w_out), BF16)
    w_all = _exchange4(wpack, False, "gather_weights")
    p_in, p_glu, p_ua, p_ub, p_out = _unpack(w_all, shard_shapes)
    w_in_f = jnp.concatenate([p_in[j] for j in range(4)], axis=1)
    w_glu_f = p_glu.reshape(S5_W, S5_W)
    w_ua_f = jnp.concatenate([p_ua[j] for j in range(4)], axis=1)
    w_ub_f = jnp.concatenate([p_ub[j] for j in range(4)], axis=1)
    w_out_f = p_out.reshape(D_MODEL, D_MODEL)
    wqkv = w_in_f[:, O_Q:O_F]
    wrest = jnp.concatenate([w_in_f[:, O_GA:O_GB], w_in_f[:, O_GB:O_END], w_in_f[:, O_ZA:O_U], w_in_f[:, O_U:O_ZB],
                             w_in_f[:, O_ZB:O_GA], w_in_f[:, O_F:O_ZA],
                             jnp.zeros((D_MODEL, REST_W - R_F - HEADS), BF16)], axis=1)
    wmid = (w_glu_f, w_ua_f, w_ub_f, w_out_f)
    wmid_t = tuple(w.T for w in wmid)

    h = _prenorm(x2d, gs, shift)
    qkv = _mm_rows(h, wqkv, BF16, "proj_qkv")
    rest = _mm_rows(h, wrest, F32, "proj_rest")
    bf128 = jnp.pad(b_f, ((0, 0), (0, LANES - HEADS)))
    selp = _head_pair_selector()
    fpc, f_t = _fcum(rest, bf128, selp)
    frow5 = f_t.reshape(4, 2, n_att, 1, t_att)
    o, lse_pc = _attn_fwd(qkv, frow5, fpc)

    abar_r, abar_i, bb_r, bb_i = _ssm_block_params(a_re[0], a_im[0], log_dt[0], b_re[0], b_im[0])
    bd = jnp.concatenate([_block_diag(jnp.swapaxes(bb_r, 1, 2)), _block_diag(jnp.swapaxes(bb_i, 1, 2))], axis=1)
    cd = jnp.concatenate([_block_diag(jnp.swapaxes(c_re[0], 1, 2)), -_block_diag(jnp.swapaxes(c_im[0], 1, 2))], axis=0)
    bd_b, cd_b = bd.astype(BF16), cd.astype(BF16)
    ys0, xs = _ssm_fwd(rest, bd_b, cd_b, _scan_consts(a_re[0], a_im[0], log_dt[0], False))

    vec = jnp.concatenate([gate, g_final.reshape(1, D_MODEL), jnp.concatenate([d_skip, b_glu], axis=1),
                           jnp.zeros((5, D_MODEL), F32)], axis=0)
    hsel = jnp.repeat(jnp.eye(HEADS, dtype=F32), HEAD_DIM, axis=1)
    (dx2, dga, dgb, do, dza, dzb, dys, dlt_t, g_out, g_ua, g_ub, g_glu, vmid) = _mid(
        o, rest, ys0, x2d, tgt, wmid, wmid_t, vec, hsel)

    lse_t = jnp.transpose(lse_pc.reshape(s, 4, LANES)[:, :, :2], (1, 2, 0))
    lse5 = lse_t.reshape(4, 2, n_att, 1, t_att)
    dlt5 = dlt_t.reshape(4, 2, n_att, 1, t_att)
    dq, dk, dv, dfk, dfq = _attn_bwd(qkv, do, lse5, dlt5, frow5, fpc)
    du, g_bd, g_cdt, da8 = _ssm_bwd(dys, xs, rest, bd_b.T, cd_b.T, _scan_consts(a_re[0], a_im[0], log_dt[0], True), d_skip)
    df, dbf8 = _dfcum(dfk, dfq, rest, bf128, selp.T)

    grad_x, vdh = _dh(dq, dk, dv, dga, dgb, dza, du, dzb, df, wqkv.T, wrest.T, x2d, dx2, gs)
    h_t = h.T
    gw = {name: _mm_kacc(h_t, d, "grad_w_in_" + name)
          for name, d in (("q", dq), ("k", dk), ("v", dv), ("ga", dga), ("gb", dgb), ("za", dza), ("u", du),
                          ("zb", dzb), ("f", df))}
    g_in = jnp.concatenate([gw["q"], gw["k"], gw["v"], gw["f"][:, :HEADS], gw["za"], gw["u"], gw["zb"], gw["ga"],
                            gw["gb"]], axis=1)

    dgs, dshift = vdh[0:1], vdh[1:2]
    dmod = jnp.concatenate([dshift, dgs * g_norm, vmid[1:2]], axis=1)
    da = jnp.sum(da8, axis=0)
    g_bbr = jnp.swapaxes(_diag_blocks(g_bd[:, :NSTATE], GCH, STATE), 1, 2)
    g_bbi = jnp.swapaxes(_diag_blocks(g_bd[:, NSTATE:], GCH, STATE), 1, 2)
    g_cre = _diag_blocks(g_cdt[:, :NSTATE], GCH, STATE)
    g_cim = -_diag_blocks(g_cdt[:, NSTATE:], GCH, STATE)
    small_shapes = [(1,), (3 * D_MODEL,), (D_MODEL,), (HEADS,), (GROUPS, STATE), (GROUPS, STATE),
                    (GROUPS, STATE, GCH), (GROUPS, STATE, GCH), (GROUPS, GCH, STATE), (GROUPS, GCH, STATE),
                    (S5_W,), (S5_W,), (D_MODEL,)]
    small = _pack([vmid[3, 0:1], dmod, dgs * (1.0 + scale), dbf8[0, :HEADS], da[:NSTATE], da[NSTATE:],
                   g_bbr, g_bbi, g_cre, g_cim, vmid[2, :S5_W], vmid[2, S5_W:], vmid[0]])
    small_all, small_sum = _allgather8(small, "gather_small_grads")
    (loss_s, g_b_ada, g_g_norm, g_b_f, g_abr, g_abi, g_bbr_s, g_bbi_s, g_c_re, g_c_im, g_d_skip, g_b_glu,
     g_g_final) = _unpack(small_sum, small_shapes)
    loss = loss_s[0]
    dmod_all = _unpack(small_all, small_shapes)[1]
    dmod_cols = lax.dynamic_slice_in_dim(dmod_all, chip * ncol, ncol, axis=1)
    g_w_ada = _grad_w_ada(c_all, dmod_cols)
    _, ssm_vjp = jax.vjp(_ssm_block_params, a_re[0], a_im[0], log_dt[0], b_re[0], b_im[0])
    g_a_re, g_a_im, g_log_dt, g_b_re, g_b_im = ssm_vjp((g_abr, g_abi, g_bbr_s, g_bbi_s))

    def shard_cols(g, j):
        n = g.shape[1] // 4
        return g[:, j * n:(j + 1) * n]

    def shard_rows(g, j):
        n = g.shape[0] // 4
        return g[j * n:(j + 1) * n]

    gpack = jnp.stack([
        _pack_big((shard_cols(g_in, j), shard_rows(g_glu, j), shard_cols(g_ua, j), shard_cols(g_ub, j),
                   shard_rows(g_out, j)), BF16)
        for j in range(4)])
    parts = _exchange4(gpack, True, "scatter_weight_grads")
    mine = _sum4(parts)
    theirs = _swap_sibling(mine, "swap_weight_grads")
    g_in_s, g_glu_s, g_ua_s, g_ub_s, g_out_s = _unpack(_add2(mine, theirs), shard_shapes)

    def adam(name, w, g, m, v):
        shape = w.shape
        total = math.prod(shape)
        if len(shape) > 1 and shape[-1] >= LANES:
            cols = shape[-1]
        elif total % LANES == 0:
            cols = LANES
        else:
            cols = total
        two = lambda a: a.reshape(-1, cols)
        d, nm, nv = _adamw(two(w), two(g), two(m), two(v), "adamw_" + name)
        return g.reshape(shape), d.reshape(shape), nm.reshape(shape), nv.reshape(shape)

    res = [
        adam("w_ada", w_ada, g_w_ada, m_w_ada, v_w_ada),
        adam("b_ada", b_ada, g_b_ada, m_b_ada, v_b_ada),
        adam("g_norm", g_norm, g_g_norm, m_g_norm, v_g_norm),
        adam("w_in", w_in, g_in_s, m_w_in, v_w_in),
        adam("b_f", b_f, g_b_f, m_b_f, v_b_f),
        adam("a_re", a_re, g_a_re, m_a_re, v_a_re),
        adam("a_im", a_im, g_a_im, m_a_im, v_a_im),
        adam("log_dt", log_dt, g_log_dt, m_log_dt, v_log_dt),
        adam("b_re", b_re, g_b_re, m_b_re, v_b_re),
        adam("b_im", b_im, g_b_im, m_b_im, v_b_im),
        adam("c_re", c_re, g_c_re, m_c_re, v_c_re),
        adam("c_im", c_im, g_c_im, m_c_im, v_c_im),
        adam("d_skip", d_skip, g_d_skip, m_d_skip, v_d_skip),
        adam("w_glu", w_glu, g_glu_s, m_w_glu, v_w_glu),
        adam("b_glu", b_glu, g_b_glu, m_b_glu, v_b_glu),
        adam("w_up_a", w_up_a, g_ua_s, m_w_up_a, v_w_up_a),
        adam("w_up_b", w_up_b, g_ub_s, m_w_up_b, v_w_up_b),
        adam("w_out", w_out, g_out_s, m_w_out, v_w_out),
        adam("g_final", g_final, g_g_final, m_g_final, v_g_final),
    ]
    grads = [r[0] for r in res]
    deltas = [r[1] for r in res]
    new_m = [r[2] for r in res]
    new_v = [r[3] for r in res]
    return (loss, grad_x[None], *grads, *deltas, *new_m, *new_v)
```

```python
import functools
import math

import jax
import jax.numpy as jnp
from jax import lax
from jax.experimental import pallas as pl
from jax.experimental.pallas import tpu as pltpu

F32 = jnp.float32
BF16 = jnp.bfloat16
HI = lax.Precision.HIGHEST
MESH = pl.DeviceIdType.MESH

D_MODEL = 1024
HEADS = 8
HEAD_DIM = 64
FOX_W = 512
S5_W = 512
GROUPS = 32
STATE = 64
GCH = 16
NSTATE = GROUPS * STATE
EPS = 1e-6
NEG = -1e30

ADAM_LR = 0.001
ADAM_B1 = 0.9
ADAM_B2 = 0.999
ADAM_EPS = 1e-08
ADAM_WD = 0.01
ADAM_STEP = 10

VMEM_LIMIT = 56 * 1024 * 1024
LANES = 128

TM = 256
T_ATT = 512
ATT_CHUNK = 32
ATT_PAIRS = 4
TB_SSM = 256
TK_ACC = 512
TB_CUM = 256

O_Q, O_K, O_V, O_F, O_ZA, O_U, O_ZB, O_GA, O_GB, O_END = 0, 512, 1024, 1536, 1544, 2056, 2568, 3080, 4104, 5128
REST_W = 3712
R_GA, R_GB, R_ZA, R_U, R_ZB, R_F = 0, 1024, 2048, 2560, 3072, 3584


def _cparams(sem=None):
    kw = dict(vmem_limit_bytes=VMEM_LIMIT)
    if sem is not None:
        kw["dimension_semantics"] = sem
    return pltpu.CompilerParams(**kw)


def _const(shape):
    nd = len(shape)
    return pl.BlockSpec(shape, lambda *_: (0,) * nd, pipeline_mode=pl.Buffered(1))


def _dot(a, b, precision=None):
    return jnp.dot(a, b, preferred_element_type=F32, precision=precision)


def _dot_nt(a, b):
    return lax.dot_general(a, b, (((1,), (1,)), ((), ())), preferred_element_type=F32)


def _dot_tn(a, b, precision=None):
    return lax.dot_general(a, b, (((0,), (0,)), ((), ())), preferred_element_type=F32, precision=precision)


def _sigmoid(z):
    return 1.0 / (1.0 + jnp.exp(-z))


def _allgather8(xs, name):
    rows = xs.shape[0]

    def body(x_ref, out_ref, sum_ref, send_sems, recv_sems, local_sem):
        x, y, c = lax.axis_index("x"), lax.axis_index("y"), lax.axis_index("c")
        me, sibling = (x, y, c), (x, y, 1 - c)
        chips = [(1 - x, y), (x, 1 - y), (1 - x, 1 - y)]

        def slot(px, py, pc):
            return out_ref.at[4 * px + 2 * py + pc]

        def copy(k, block, to, src=None):
            return pltpu.make_async_remote_copy(
                src_ref=slot(*block) if src is None else src, dst_ref=slot(*block),
                send_sem=send_sems.at[k], recv_sem=recv_sems.at[k], device_id=to, device_id_type=MESH)

        mine = pltpu.make_async_copy(x_ref, slot(*me), local_sem)
        mine.start()
        first = [copy(0, me, sibling, src=x_ref)]
        first += [copy(1 + j, me, (*chip, c), src=x_ref) for j, chip in enumerate(chips)]
        for cp in first:
            cp.start()
        passed = [copy(4 + j, (*chip, c), sibling) for j, chip in enumerate(chips)]
        for j, chip in enumerate(chips):
            copy(1 + j, (*chip, c), me).wait_recv()
            passed[j].start()
        copy(0, sibling, me).wait_recv()
        for j, chip in enumerate(chips):
            copy(4 + j, (*chip, 1 - c), me).wait_recv()
        for cp in first + passed:
            cp.wait_send()
        mine.wait()
        acc = out_ref[0]
        for d in range(1, 8):
            acc = acc + out_ref[d]
        sum_ref[...] = acc

    return pl.pallas_call(
        body, name=name,
        out_shape=(jax.ShapeDtypeStruct((8, rows, LANES), F32), jax.ShapeDtypeStruct((rows, LANES), F32)),
        in_specs=[pl.BlockSpec(memory_space=pltpu.VMEM)],
        out_specs=(pl.BlockSpec(memory_space=pltpu.VMEM), pl.BlockSpec(memory_space=pltpu.VMEM)),
        scratch_shapes=[pltpu.SemaphoreType.DMA((7,)), pltpu.SemaphoreType.DMA((7,)), pltpu.SemaphoreType.DMA],
        compiler_params=_cparams(),
    )(xs)


def _exchange4(src, scatter, name):
    rows = src.shape[-2]

    def body(src_ref, out_ref, send_sems, recv_sems, local_sem):
        x, y, c = lax.axis_index("x"), lax.axis_index("y"), lax.axis_index("c")
        peers = [(1 - x, y), (x, 1 - y), (1 - x, 1 - y)]

        def block_for(px, py):
            return src_ref.at[2 * px + py] if scatter else src_ref

        def copy(k, px, py, slot):
            return pltpu.make_async_remote_copy(
                src_ref=block_for(px, py), dst_ref=out_ref.at[slot],
                send_sem=send_sems.at[k], recv_sem=recv_sems.at[k],
                device_id=(px, py, c), device_id_type=MESH)

        local = pltpu.make_async_copy(block_for(x, y), out_ref.at[2 * x + y], local_sem)
        local.start()
        sends = [copy(k, px, py, 2 * x + y) for k, (px, py) in enumerate(peers)]
        for cp in sends:
            cp.start()
        for k, (px, py) in enumerate(peers):
            copy(k, px, py, 2 * px + py).wait_recv()
        for cp in sends:
            cp.wait_send()
        local.wait()

    return pl.pallas_call(
        body, name=name,
        out_shape=jax.ShapeDtypeStruct((4, rows, LANES), src.dtype),
        in_specs=[pl.BlockSpec(memory_space=pl.ANY)],
        out_specs=pl.BlockSpec(memory_space=pl.ANY),
        scratch_shapes=[pltpu.SemaphoreType.DMA((3,)), pltpu.SemaphoreType.DMA((3,)), pltpu.SemaphoreType.DMA],
        compiler_params=_cparams(),
    )(src)


def _swap_sibling(src, name):
    def body(src_ref, out_ref, send_sem, recv_sem):
        x, y, c = lax.axis_index("x"), lax.axis_index("y"), lax.axis_index("c")
        cp = pltpu.make_async_remote_copy(
            src_ref=src_ref, dst_ref=out_ref, send_sem=send_sem, recv_sem=recv_sem,
            device_id=(x, y, 1 - c), device_id_type=MESH)
        cp.start()
        cp.wait()

    return pl.pallas_call(
        body, name=name,
        out_shape=jax.ShapeDtypeStruct(src.shape, src.dtype),
        in_specs=[pl.BlockSpec(memory_space=pl.ANY)],
        out_specs=pl.BlockSpec(memory_space=pl.ANY),
        scratch_shapes=[pltpu.SemaphoreType.DMA, pltpu.SemaphoreType.DMA],
        compiler_params=_cparams(),
    )(src)


def _mod_cols(c_all, w, b):
    n = w.shape[1]

    def body(c_ref, w_ref, b_ref, o_ref):
        o_ref[...] = _dot(c_ref[...], w_ref[...], HI) + b_ref[...]

    return pl.pallas_call(
        body, name="mod_cols", out_shape=jax.ShapeDtypeStruct((8, n), F32),
        compiler_params=_cparams())(c_all, w, b)


def _grad_w_ada(c_all, dmod_cols):
    n = dmod_cols.shape[1]

    def body(c_ref, d_ref, o_ref):
        o_ref[...] = _dot_tn(c_ref[...], d_ref[...], HI)

    return pl.pallas_call(
        body, name="grad_w_ada", out_shape=jax.ShapeDtypeStruct((D_MODEL, n), F32),
        compiler_params=_cparams())(c_all, dmod_cols)


def _prenorm(x, gs, shift):
    s = x.shape[0]
    tm = min(TM, s)

    def body(x_ref, gs_ref, sh_ref, h_ref):
        xv = x_ref[...]
        r = lax.rsqrt(jnp.mean(xv * xv, axis=-1, keepdims=True) + EPS)
        h_ref[...] = (xv * r * gs_ref[...] + sh_ref[...]).astype(BF16)

    row = pl.BlockSpec((tm, D_MODEL), lambda i: (i, 0))
    vec = pl.BlockSpec((1, D_MODEL), lambda i: (0, 0))
    return pl.pallas_call(
        body, name="prenorm", grid=(s // tm,), in_specs=[row, vec, vec], out_specs=row,
        out_shape=jax.ShapeDtypeStruct((s, D_MODEL), BF16), compiler_params=_cparams(("parallel",)))(x, gs, shift)


def _mm_rows(a, b, out_dtype, name):
    s, k = a.shape
    n = b.shape[1]
    tm = min(TM, s)

    def body(a_ref, b_ref, o_ref):
        o_ref[...] = _dot(a_ref[...], b_ref[...]).astype(out_dtype)

    return pl.pallas_call(
        body, name=name, grid=(s // tm,),
        in_specs=[pl.BlockSpec((tm, k), lambda i: (i, 0)), _const((k, n))],
        out_specs=pl.BlockSpec((tm, n), lambda i: (i, 0)),
        out_shape=jax.ShapeDtypeStruct((s, n), out_dtype), compiler_params=_cparams(("parallel",)))(a, b)


def _mm_kacc(a_t, b, name):
    m, s = a_t.shape
    n = b.shape[1]
    tk = min(TK_ACC, s)

    def body(a_ref, b_ref, o_ref):
        @pl.when(pl.program_id(0) == 0)
        def _():
            o_ref[...] = jnp.zeros_like(o_ref)

        o_ref[...] += _dot(a_ref[...], b_ref[...])

    return pl.pallas_call(
        body, name=name, grid=(s // tk,),
        in_specs=[pl.BlockSpec((m, tk), lambda k: (0, k)), pl.BlockSpec((tk, n), lambda k: (k, 0))],
        out_specs=pl.BlockSpec((m, n), lambda k: (0, 0)),
        out_shape=jax.ShapeDtypeStruct((m, n), F32), compiler_params=_cparams(("arbitrary",)))(a_t, b)


def _head_pair_selector():
    rows = jnp.arange(LANES)[:, None]
    cols = jnp.arange(4 * LANES)[None, :]
    return ((rows < HEADS) & (cols == (rows // 2) * LANES + rows % 2)).astype(F32)


def _fcum(rest, bf128, selp):
    s = rest.shape[0]
    tb = min(TB_CUM, s)

    def body(fz_ref, bf_ref, sel_ref, fpc_ref, ft_ref, carry_ref):
        @pl.when(pl.program_id(0) == 0)
        def _():
            carry_ref[...] = jnp.zeros_like(carry_ref)

        z = fz_ref[...] + bf_ref[...]
        logf = jnp.minimum(z, 0.0) - jnp.log(1.0 + jnp.exp(-jnp.abs(z)))
        r = lax.broadcasted_iota(jnp.int32, (tb, tb), 0)
        c = lax.broadcasted_iota(jnp.int32, (tb, tb), 1)
        tri = (c <= r).astype(F32)
        f = _dot(tri, logf, HI) + carry_ref[0:1, :]
        carry_ref[0:1, :] = f[tb - 1:tb, :]
        fpc_ref[...] = _dot(f, sel_ref[...], HI)
        ft_ref[...] = jnp.transpose(f)[0:HEADS, :]

    return pl.pallas_call(
        body, name="forget_cumsum", grid=(s // tb,),
        in_specs=[pl.BlockSpec((tb, LANES), lambda i: (i, R_F // LANES)), _const((1, LANES)), _const((LANES, 4 * LANES))],
        out_specs=(pl.BlockSpec((tb, 4 * LANES), lambda i: (i, 0)), pl.BlockSpec((HEADS, tb), lambda i: (0, i))),
        out_shape=(jax.ShapeDtypeStruct((s, 4 * LANES), F32), jax.ShapeDtypeStruct((HEADS, s), F32)),
        scratch_shapes=[pltpu.VMEM((8, LANES), F32)],
        compiler_params=_cparams(("arbitrary",)))(rest, bf128, selp)


def _dfcum(dfk, dfq, rest, bf128, selq):
    s = rest.shape[0]
    tb = min(TB_CUM, s)
    nb = s // tb

    def body(dk_ref, dq_ref, fz_ref, bf_ref, sel_ref, df_ref, dbf_ref, carry_ref):
        @pl.when(pl.program_id(0) == 0)
        def _():
            carry_ref[...] = jnp.zeros_like(carry_ref)
            dbf_ref[...] = jnp.zeros_like(dbf_ref)

        d = _dot(dk_ref[...] + dq_ref[...], sel_ref[...], HI)
        r = lax.broadcasted_iota(jnp.int32, (tb, tb), 0)
        c = lax.broadcasted_iota(jnp.int32, (tb, tb), 1)
        triu = (c >= r).astype(F32)
        dlogf = _dot(triu, d, HI) + carry_ref[0:1, :]
        carry_ref[0:1, :] = dlogf[0:1, :]
        z = fz_ref[...] + bf_ref[...]
        df = dlogf * (1.0 / (1.0 + jnp.exp(z)))
        df_ref[...] = df.astype(BF16)
        dbf_ref[0:1, :] += jnp.sum(df, axis=0, keepdims=True)

    return pl.pallas_call(
        body, name="forget_grad", grid=(nb,),
        in_specs=[pl.BlockSpec((tb, 4 * LANES), lambda i: (nb - 1 - i, 0)),
                  pl.BlockSpec((tb, 4 * LANES), lambda i: (nb - 1 - i, 0)),
                  pl.BlockSpec((tb, LANES), lambda i: (nb - 1 - i, R_F // LANES)),
                  _const((1, LANES)), _const((4 * LANES, LANES))],
        out_specs=(pl.BlockSpec((tb, LANES), lambda i: (nb - 1 - i, 0)), pl.BlockSpec((8, LANES), lambda i: (0, 0))),
        out_shape=(jax.ShapeDtypeStruct((s, LANES), BF16), jax.ShapeDtypeStruct((8, LANES), F32)),
        scratch_shapes=[pltpu.VMEM((8, LANES), F32)],
        compiler_params=_cparams(("arbitrary",)))(dfk, dfq, rest, bf128, selq)


def _scaled(q):
    return (q.astype(F32) * (HEAD_DIM ** -0.5)).astype(BF16)


def _attn_fwd(qkv, frow5, fpc):
    s = qkv.shape[0]
    t = min(T_ATT, s)
    n = s // t
    ch = min(ATT_CHUNK, t)
    wide = 2 * LANES
    pairs = ATT_PAIRS
    width = pairs * LANES
    groups = 4 // pairs

    def body(q_ref, k_ref, v_ref, fr_ref, fc_ref, o_ref, lse_ref, s_scr, p_scr, m_scr, a_scr, fq_scr, acc_scr):
        i = pl.program_id(1)
        lane = lax.broadcasted_iota(jnp.int32, (t, LANES), 1)
        first = lane < HEAD_DIM
        ones_col = ((lane == 0).astype(BF16), (lane == 1).astype(BF16))
        m_scr[...] = jnp.full(m_scr.shape, NEG, F32)
        acc_scr[...] = jnp.zeros_like(acc_scr)
        qm = []
        for pp in range(pairs):
            q = _scaled(q_ref[:, pp * LANES:(pp + 1) * LANES])
            zq = jnp.zeros_like(q)
            qm += [jnp.where(first, q, zq), jnp.where(first, zq, q)]
            fq_scr[2 * pp] = fc_ref[:, pp * LANES:pp * LANES + 1]
            fq_scr[2 * pp + 1] = fc_ref[:, pp * LANES + 1:pp * LANES + 2]

        def step(j, masked):
            r0 = pl.multiple_of(j * t, t)
            vaug = []
            for pp in range(pairs):
                kb = k_ref[pl.ds(r0, t), pp * LANES:(pp + 1) * LANES]
                vb = v_ref[pl.ds(r0, t), pp * LANES:(pp + 1) * LANES]
                zv = jnp.zeros_like(vb)
                vaug += [jnp.concatenate([jnp.where(first, vb, zv), ones_col[0]], axis=1),
                         jnp.concatenate([jnp.where(first, zv, vb), ones_col[1]], axis=1)]
                for hh in range(2):
                    s_scr[2 * pp + hh] = _dot_nt(qm[2 * pp + hh], kb)
            pv = []
            for hd in range(2 * pairs):
                fk = fr_ref[hd // 2, hd % 2, j]
                for c in range(t // ch):
                    rows = pl.ds(c * ch, ch)
                    sc = s_scr[hd, rows, :] - fk
                    if masked:
                        rq = c * ch + lax.broadcasted_iota(jnp.int32, (ch, t), 0)
                        ck = lax.broadcasted_iota(jnp.int32, (ch, t), 1)
                        sc = jnp.where(ck <= rq, sc, NEG)
                    fq = fq_scr[hd, rows, :]
                    m_old = m_scr[hd, rows, :]
                    m_new = jnp.maximum(m_old, fq + jnp.max(sc, axis=1, keepdims=True))
                    p_scr[hd, rows, :] = jnp.exp(sc + (fq - m_new)).astype(BF16)
                    a_scr[hd, rows, :] = jnp.exp(m_old - m_new)
                    m_scr[hd, rows, :] = m_new
                pv.append(_dot(p_scr[hd], vaug[hd]))
            for pp in range(pairs):
                a0, a1 = a_scr[2 * pp], a_scr[2 * pp + 1]
                alpha = jnp.concatenate([jnp.where(first, a0, a1), jnp.where(lane == 0, a0, a1)], axis=1)
                acc_scr[pp] = acc_scr[pp] * alpha + pv[2 * pp] + pv[2 * pp + 1]
            return 0

        lax.fori_loop(0, i, lambda j, _: step(j, False), 0)
        step(i, True)
        for pp in range(pairs):
            l0 = acc_scr[pp, :, LANES:LANES + 1]
            l1 = acc_scr[pp, :, LANES + 1:LANES + 2]
            o_ref[:, pp * LANES:(pp + 1) * LANES] = acc_scr[pp, :, 0:LANES] * jnp.where(first, 1.0 / l0, 1.0 / l1)
            lse0 = m_scr[2 * pp] + jnp.log(l0)
            lse1 = m_scr[2 * pp + 1] + jnp.log(l1)
            lse_ref[:, pp * LANES:(pp + 1) * LANES] = jnp.where(lane == 0, lse0, jnp.where(lane == 1, lse1, 0.0))

    blk = pl.BlockSpec((t, width), lambda g, i: (i, g))
    return pl.pallas_call(
        body, name="attn_fwd", grid=(groups, n),
        in_specs=[blk,
                  pl.BlockSpec((s, width), lambda g, i: (0, groups + g)),
                  pl.BlockSpec((s, width), lambda g, i: (0, 2 * groups + g)),
                  pl.BlockSpec((pairs, 2, n, 1, t), lambda g, i: (g, 0, 0, 0, 0)),
                  blk],
        out_specs=(blk, blk),
        out_shape=(jax.ShapeDtypeStruct((s, FOX_W), F32), jax.ShapeDtypeStruct((s, 4 * LANES), F32)),
        scratch_shapes=[pltpu.VMEM((2 * pairs, t, t), F32), pltpu.VMEM((2 * pairs, t, t), BF16),
                        pltpu.VMEM((2 * pairs, t, 1), F32), pltpu.VMEM((2 * pairs, t, 1), F32),
                        pltpu.VMEM((2 * pairs, t, 1), F32), pltpu.VMEM((pairs, t, wide), F32)],
        compiler_params=_cparams(("parallel", "arbitrary")))(qkv, qkv, qkv, frow5, fpc)


def _attn_bwd(qkv, do, lse5, dlt5, frow5, fpc):
    s = qkv.shape[0]
    t = min(T_ATT, s)
    n = s // t
    wide = 2 * LANES

    ch = min(ATT_CHUNK, t)

    def body(q_ref, do_ref, k_ref, v_ref, lse_ref, dl_ref, fr_ref, fc_ref,
             dq_ref, dk_ref, dv_ref, dfk_ref, dfq_ref, dq_acc, st_scr, dp_scr, pt_scr, ds_scr, dk_acc, dv_acc, fk_scr):
        j = pl.program_id(1)

        @pl.when(j == 0)
        def _():
            dq_acc[...] = jnp.zeros_like(dq_acc)

        dk_acc[...] = jnp.zeros_like(dk_acc)
        dv_acc[...] = jnp.zeros_like(dv_acc)
        lane = lax.broadcasted_iota(jnp.int32, (t, LANES), 1)
        first = lane < HEAD_DIM
        ones_col = ((lane == 0).astype(BF16), (lane == 1).astype(BF16))
        kb = k_ref[...]
        vb = v_ref[...]
        zk = jnp.zeros_like(kb)
        kaug = (jnp.concatenate([jnp.where(first, kb, zk), ones_col[0]], axis=1),
                jnp.concatenate([jnp.where(first, zk, kb), ones_col[1]], axis=1))
        fk_scr[0] = fc_ref[:, 0:1]
        fk_scr[1] = fc_ref[:, 1:2]

        def step(i, masked):
            r0 = pl.multiple_of(i * t, t)
            qb = _scaled(q_ref[pl.ds(r0, t), :])
            dob = do_ref[pl.ds(r0, t), :]
            zq = jnp.zeros_like(qb)
            qm = (jnp.where(first, qb, zq), jnp.where(first, zq, qb))
            dom = (jnp.where(first, dob, zq), jnp.where(first, zq, dob))
            dq_add = jnp.zeros((t, wide), F32)
            for hh in range(2):
                st_scr[hh] = _dot_nt(kb, qm[hh])
                dp_scr[hh] = _dot_nt(vb, dom[hh])
                bias = fr_ref[0, hh, i] - lse_ref[0, hh, i]
                dl = dl_ref[0, hh, i]
                for c in range(t // ch):
                    rows = pl.ds(c * ch, ch)
                    st = st_scr[hh, rows, :] + (bias - fk_scr[hh, rows, :])
                    if masked:
                        rk = c * ch + lax.broadcasted_iota(jnp.int32, (ch, t), 0)
                        cq = lax.broadcasted_iota(jnp.int32, (ch, t), 1)
                        st = jnp.where(rk <= cq, st, NEG)
                    pt = jnp.exp(st)
                    pt_scr[hh, rows, :] = pt.astype(BF16)
                    ds_scr[hh, rows, :] = (pt * (dp_scr[hh, rows, :] - dl)).astype(BF16)
                dsb = ds_scr[hh]
                dv_acc[...] += _dot(pt_scr[hh], dom[hh])
                dk_acc[...] += _dot(dsb, jnp.concatenate([qm[hh], ones_col[hh]], axis=1))
                dq_add = dq_add + _dot_tn(dsb, kaug[hh])
            dq_acc[pl.ds(r0, t), :] += dq_add
            return 0

        step(j, True)
        lax.fori_loop(j + 1, n, lambda i, _: step(i, False), 0)
        dk_ref[...] = dk_acc[:, 0:LANES].astype(BF16)
        dv_ref[...] = dv_acc[...].astype(BF16)
        dfk_ref[...] = -dk_acc[:, LANES:wide]

        @pl.when(j == n - 1)
        def _():
            dq_ref[...] = (dq_acc[:, 0:LANES] * (HEAD_DIM ** -0.5)).astype(BF16)
            dfq_ref[...] = dq_acc[:, LANES:wide]

    stat = pl.BlockSpec((1, 2, n, 1, t), lambda h, j: (h, 0, 0, 0, 0))
    blk = pl.BlockSpec((t, LANES), lambda h, j: (j, h))
    full = pl.BlockSpec((s, LANES), lambda h, j: (0, h))
    return pl.pallas_call(
        body, name="attn_bwd", grid=(4, n),
        in_specs=[full, full,
                  pl.BlockSpec((t, LANES), lambda h, j: (j, 4 + h)),
                  pl.BlockSpec((t, LANES), lambda h, j: (j, 8 + h)),
                  stat, stat, stat, blk],
        out_specs=(full, blk, blk, blk, full),
        out_shape=(jax.ShapeDtypeStruct((s, FOX_W), BF16), jax.ShapeDtypeStruct((s, FOX_W), BF16),
                   jax.ShapeDtypeStruct((s, FOX_W), BF16), jax.ShapeDtypeStruct((s, 4 * LANES), F32),
                   jax.ShapeDtypeStruct((s, 4 * LANES), F32)),
        scratch_shapes=[pltpu.VMEM((s, wide), F32), pltpu.VMEM((2, t, t), F32), pltpu.VMEM((2, t, t), F32),
                        pltpu.VMEM((2, t, t), BF16), pltpu.VMEM((2, t, t), BF16), pltpu.VMEM((t, wide), F32),
                        pltpu.VMEM((t, LANES), F32), pltpu.VMEM((2, t, 1), F32)],
        compiler_params=_cparams(("parallel", "arbitrary")))(qkv, do, qkv, qkv, lse5, dlt5, frow5, fpc)


def _ssm_block_params(a_re, a_im, log_dt, b_re, b_im):
    dt = jnp.exp(log_dt)[:, None]
    mag = jnp.exp(a_re * dt)
    ar = mag * jnp.cos(a_im * dt)
    ai = mag * jnp.sin(a_im * dt)
    den = a_re * a_re + a_im * a_im
    nr = ar - 1.0
    cr = (nr * a_re + ai * a_im) / den
    ci = (ai * a_re - nr * a_im) / den
    bbr = cr[:, :, None] * b_re - ci[:, :, None] * b_im
    bbi = cr[:, :, None] * b_im + ci[:, :, None] * b_re
    return ar, ai, bbr, bbi


def _block_diag(blocks):
    g, r, c = blocks.shape
    eye = jnp.eye(g, dtype=blocks.dtype)
    return (blocks[:, :, None, :] * eye[:, None, :, None]).reshape(g * r, g * c)


def _diag_blocks(dense, r, c):
    g = dense.shape[0] // r
    eye = jnp.eye(g, dtype=dense.dtype)
    return (dense.reshape(g, r, g, c) * eye[:, None, :, None]).sum(axis=2)


def _scan_consts(a_re, a_im, log_dt, reverse):
    dt = jnp.exp(log_dt)[:, None]
    lr = (a_re * dt).reshape(1, NSTATE)
    li = (a_im * dt).reshape(1, NSTATE)
    if reverse:
        li = -li
    rows = jnp.arange(8, dtype=F32)[:, None]

    def power(k):
        mag = jnp.exp(k * lr)
        return mag * jnp.cos(k * li), mag * jnp.sin(k * li)

    tiles = []
    for k in (1, 2, 4):
        keep = (rows < 8 - k) if reverse else (rows >= k)
        pr, pi_ = power(float(k))
        tiles += [jnp.where(keep, pr, 0.0), jnp.where(keep, pi_, 0.0)]
    expo = (8.0 - rows) if reverse else (rows + 1.0)
    tiles += list(power(expo))
    return jnp.stack([jnp.broadcast_to(tl, (8, NSTATE)) for tl in tiles])


_SCAN_W = 512


def _ssm_fwd(rest, bd, cd, consts):
    s = rest.shape[0]
    tb = min(TB_SSM, s)
    ns2 = 2 * NSTATE

    def body(u_ref, bd_ref, cd_ref, cf_ref, y_ref, x_ref, cb_ref):
        @pl.when(pl.program_id(0) == 0)
        def _():
            cb_ref[...] = jnp.zeros_like(cb_ref)

        x_ref[...] = _dot(u_ref[...].astype(BF16), bd_ref[...])

        def tile(ti, _):
            r0 = pl.multiple_of(ti * 8, 8)
            for cc in range(NSTATE // _SCAN_W):
                cr = pl.ds(cc * _SCAN_W, _SCAN_W)
                ci = pl.ds(NSTATE + cc * _SCAN_W, _SCAN_W)
                re = x_ref[pl.ds(r0, 8), cr]
                im = x_ref[pl.ds(r0, 8), ci]
                for n_, k in enumerate((1, 2, 4)):
                    ar = cf_ref[2 * n_, :, cr]
                    ai = cf_ref[2 * n_ + 1, :, cr]
                    sr = pltpu.roll(re, k, 0)
                    si = pltpu.roll(im, k, 0)
                    re, im = re + ar * sr - ai * si, im + ar * si + ai * sr
                pr = cf_ref[6, :, cr]
                pi_ = cf_ref[7, :, cr]
                cbr = cb_ref[:, cr]
                cbi = cb_ref[:, ci]
                re, im = re + pr * cbr - pi_ * cbi, im + pr * cbi + pi_ * cbr
                x_ref[pl.ds(r0, 8), cr] = re
                x_ref[pl.ds(r0, 8), ci] = im
                cb_ref[:, cr] = jnp.broadcast_to(re[7:8, :], (8, _SCAN_W))
                cb_ref[:, ci] = jnp.broadcast_to(im[7:8, :], (8, _SCAN_W))
            return 0

        lax.fori_loop(0, tb // 8, tile, 0)
        y_ref[...] = _dot(x_ref[...].astype(BF16), cd_ref[...])

    return pl.pallas_call(
        body, name="ssm_fwd", grid=(s // tb,),
        in_specs=[pl.BlockSpec((tb, S5_W), lambda i: (i, R_U // S5_W)), _const((S5_W, ns2)), _const((ns2, S5_W)),
                  _const((8, 8, NSTATE))],
        out_specs=(pl.BlockSpec((tb, S5_W), lambda i: (i, 0)), pl.BlockSpec((tb, ns2), lambda i: (i, 0))),
        out_shape=(jax.ShapeDtypeStruct((s, S5_W), F32), jax.ShapeDtypeStruct((s, ns2), F32)),
        scratch_shapes=[pltpu.VMEM((8, ns2), F32)],
        compiler_params=_cparams(("arbitrary",)))(rest, bd, cd, consts)


def _ssm_bwd(dys, xs, rest, bd_t, cd_t, consts, dskip):
    s = dys.shape[0]
    tb = min(TB_SSM, s)
    nb = s // tb
    ns2 = 2 * NSTATE
    nt = tb // 8

    def body(dy_ref, x_ref, u_ref, bdt_ref, cdt_ref, cf_ref, dsk_ref, du_ref, dbd_hbm, dcd_hbm, da_ref,
             g_ref, cb_ref, acc_b, acc_c):
        step = pl.program_id(0)

        @pl.when(step == 0)
        def _():
            cb_ref[...] = jnp.zeros_like(cb_ref)
            acc_b[...] = jnp.zeros_like(acc_b)
            acc_c[...] = jnp.zeros_like(acc_c)
            da_ref[...] = jnp.zeros_like(da_ref)

        dy = dy_ref[...]
        dyb = dy.astype(BF16)
        g_ref[...] = _dot(dyb, cdt_ref[...])
        last_row = lax.broadcasted_iota(jnp.int32, (8, _SCAN_W), 0) == 7

        def tile(tt, _):
            r0 = pl.multiple_of((nt - 1 - tt) * 8, 8)
            for cc in range(NSTATE // _SCAN_W):
                cr = pl.ds(cc * _SCAN_W, _SCAN_W)
                ci = pl.ds(NSTATE + cc * _SCAN_W, _SCAN_W)
                re = g_ref[pl.ds(r0, 8), cr]
                im = g_ref[pl.ds(r0, 8), ci]
                for n_, k in enumerate((1, 2, 4)):
                    ar = cf_ref[2 * n_, :, cr]
                    ai = cf_ref[2 * n_ + 1, :, cr]
                    sr = pltpu.roll(re, 8 - k, 0)
                    si = pltpu.roll(im, 8 - k, 0)
                    re, im = re + ar * sr - ai * si, im + ar * si + ai * sr
                pr = cf_ref[6, :, cr]
                pi_ = cf_ref[7, :, cr]
                cbr = cb_ref[:, cr]
                cbi = cb_ref[:, ci]
                re, im = re + pr * cbr - pi_ * cbi, im + pr * cbi + pi_ * cbr
                g_ref[pl.ds(r0, 8), cr] = re
                g_ref[pl.ds(r0, 8), ci] = im
                gnr = jnp.where(last_row, cbr, pltpu.roll(re, 7, 0))
                gni = jnp.where(last_row, cbi, pltpu.roll(im, 7, 0))
                xr = x_ref[pl.ds(r0, 8), cr]
                xi = x_ref[pl.ds(r0, 8), ci]
                da_ref[:, cr] += gnr * xr + gni * xi
                da_ref[:, ci] += gni * xr - gnr * xi
                cb_ref[:, cr] = jnp.broadcast_to(re[0:1, :], (8, _SCAN_W))
                cb_ref[:, ci] = jnp.broadcast_to(im[0:1, :], (8, _SCAN_W))
            return 0

        lax.fori_loop(0, nt, tile, 0)
        gb = g_ref[...].astype(BF16)
        du_ref[...] = (_dot(gb, bdt_ref[...]) + dy * dsk_ref[...]).astype(BF16)
        acc_b[...] += _dot_tn(u_ref[...].astype(BF16), gb)
        acc_c[...] += _dot_tn(dyb, x_ref[...].astype(BF16))

        @pl.when(step == nb - 1)
        def _():
            pltpu.sync_copy(acc_b, dbd_hbm)
            pltpu.sync_copy(acc_c, dcd_hbm)

    rev = lambda i: (nb - 1 - i, 0)
    return pl.pallas_call(
        body, name="ssm_bwd", grid=(nb,),
        in_specs=[pl.BlockSpec((tb, S5_W), rev), pl.BlockSpec((tb, ns2), rev),
                  pl.BlockSpec((tb, S5_W), lambda i: (nb - 1 - i, R_U // S5_W)),
                  _const((ns2, S5_W)), _const((S5_W, ns2)), _const((8, 8, NSTATE)), _const((1, S5_W))],
        out_specs=(pl.BlockSpec((tb, S5_W), rev), pl.BlockSpec(memory_space=pl.ANY), pl.BlockSpec(memory_space=pl.ANY),
                   pl.BlockSpec((8, ns2), lambda i: (0, 0))),
        out_shape=(jax.ShapeDtypeStruct((s, S5_W), BF16), jax.ShapeDtypeStruct((S5_W, ns2), F32),
                   jax.ShapeDtypeStruct((S5_W, ns2), F32), jax.ShapeDtypeStruct((8, ns2), F32)),
        scratch_shapes=[pltpu.VMEM((tb, ns2), F32), pltpu.VMEM((8, ns2), F32),
                        pltpu.VMEM((S5_W, ns2), F32), pltpu.VMEM((S5_W, ns2), F32)],
        compiler_params=_cparams(("arbitrary",)))(dys, xs, rest, bd_t, cd_t, consts, dskip)


_GELU_C = math.sqrt(2.0 / math.pi)
_GELU_A = 0.044715


def _mid(o, rest, ys0, x, tgt, w, w_t, vec, hsel):
    s = o.shape[0]
    tm = min(TM, s)
    nsteps = s // tm
    half = FOX_W

    def body(o_ref, ga_ref, gb_ref, za_ref, u_ref, zb_ref, ys0_ref, x_ref, t_ref,
             wglu_ref, wua_ref, wub_ref, wout_ref, wglut_ref, wuat_ref, wubt_ref, woutt_ref, vec_ref, hsel_ref,
             dx2_ref, dga_ref, dgb_ref, do_ref, dza_ref, dzb_ref, dys_ref, dlt_ref,
             gout_hbm, gua_hbm, gub_hbm, gglu_hbm, vout_ref,
             a_out, a_ua, a_ub, a_glu):
        step = pl.program_id(0)

        @pl.when(step == 0)
        def _():
            a_out[...] = jnp.zeros_like(a_out)
            a_ua[...] = jnp.zeros_like(a_ua)
            a_ub[...] = jnp.zeros_like(a_ub)
            a_glu[...] = jnp.zeros_like(a_glu)
            vout_ref[...] = jnp.zeros_like(vout_ref)

        gate = vec_ref[0:1, :]
        gfin = vec_ref[1:2, :]
        dsk = vec_ref[2:3, 0:half]
        bglu = vec_ref[2:3, half:2 * half]

        o_v = o_ref[...]
        za = za_ref[...]
        sza = _sigmoid(za)
        silu_za = za * sza
        ya_b = (o_v * silu_za).astype(BF16)
        u_v = u_ref[...]
        ys = ys0_ref[...] + dsk * u_v
        inner = _GELU_C * (ys + _GELU_A * ys * ys * ys)
        th = jnp.tanh(inner)
        yg = 0.5 * ys * (1.0 + th)
        yg_b = yg.astype(BF16)
        st = _sigmoid(_dot(yg_b, wglu_ref[...]) + bglu)
        yb1 = yg * st
        zb = zb_ref[...]
        szb = _sigmoid(zb)
        silu_zb = zb * szb
        yb_b = (yb1 * silu_zb).astype(BF16)
        ua = _dot(ya_b, wua_ref[...])
        ub = _dot(yb_b, wub_ref[...])
        sga = _sigmoid(ga_ref[...])
        sgb = _sigmoid(gb_ref[...])
        merged_b = (sga * ua + sgb * ub).astype(BF16)
        mo = _dot(merged_b, wout_ref[...])
        x2 = x_ref[...] + gate * mo
        r2 = lax.rsqrt(jnp.mean(x2 * x2, axis=-1, keepdims=True) + EPS)
        x2n = x2 * r2
        diff = x2n * gfin - t_ref[...]
        loss = 0.5 * jnp.sum(jnp.mean(diff * diff, axis=-1, keepdims=True), axis=0, keepdims=True)
        dy = diff * (1.0 / D_MODEL)
        dx2n = dy * gfin
        dx2 = r2 * (dx2n - x2n * jnp.mean(dx2n * x2n, axis=-1, keepdims=True))
        dx2_ref[...] = dx2
        vout_ref[0:1, :] += jnp.sum(dy * x2n, axis=0, keepdims=True)
        vout_ref[1:2, :] += jnp.sum(dx2 * mo, axis=0, keepdims=True)
        vout_ref[3:4, :] += jnp.broadcast_to(loss, (1, D_MODEL))
        dmo_b = (dx2 * gate).astype(BF16)
        dmerged = _dot(dmo_b, woutt_ref[...])
        a_out[...] += _dot_tn(merged_b, dmo_b)
        dua_b = (dmerged * sga).astype(BF16)
        dub_b = (dmerged * sgb).astype(BF16)
        dga_ref[...] = (dmerged * ua * sga * (1.0 - sga)).astype(BF16)
        dgb_ref[...] = (dmerged * ub * sgb * (1.0 - sgb)).astype(BF16)
        dya = _dot(dua_b, wuat_ref[...])
        dyb = _dot(dub_b, wubt_ref[...])
        a_ua[...] += _dot_tn(ya_b, dua_b)
        a_ub[...] += _dot_tn(yb_b, dub_b)
        do_b = (dya * silu_za).astype(BF16)
        do_ref[...] = do_b
        dza_ref[...] = (dya * o_v * (sza * (1.0 + za * (1.0 - sza)))).astype(BF16)
        dlt_ref[...] = lax.dot_general(hsel_ref[...], do_b.astype(F32) * o_v, (((1,), (1,)), ((), ())),
                                       preferred_element_type=F32, precision=HI)
        dyb1 = dyb * silu_zb
        dzb_ref[...] = (dyb * yb1 * (szb * (1.0 + zb * (1.0 - szb)))).astype(BF16)
        dt = dyb1 * yg * st * (1.0 - st)
        dt_b = dt.astype(BF16)
        dyg = dyb1 * st + _dot(dt_b, wglut_ref[...])
        a_glu[...] += _dot_tn(yg_b, dt_b)
        dgelu = 0.5 * (1.0 + th) + 0.5 * ys * (1.0 - th * th) * _GELU_C * (1.0 + 3.0 * _GELU_A * ys * ys)
        dys = dyg * dgelu
        dys_ref[...] = dys
        vout_ref[2:3, 0:half] += jnp.sum(dys * u_v, axis=0, keepdims=True)
        vout_ref[2:3, half:2 * half] += jnp.sum(dt, axis=0, keepdims=True)

        @pl.when(step == nsteps - 1)
        def _():
            pltpu.sync_copy(a_out, gout_hbm)
            pltpu.sync_copy(a_ua, gua_hbm)
            pltpu.sync_copy(a_ub, gub_hbm)
            pltpu.sync_copy(a_glu, gglu_hbm)

    def rows(width, col=0):
        return pl.BlockSpec((tm, width), lambda i, col=col: (i, col))

    anyspace = pl.BlockSpec(memory_space=pl.ANY)
    wshapes = [(S5_W, S5_W), (FOX_W, D_MODEL), (S5_W, D_MODEL), (D_MODEL, D_MODEL)]
    return pl.pallas_call(
        body, name="mid", grid=(nsteps,),
        in_specs=[rows(FOX_W), rows(D_MODEL, R_GA // D_MODEL), rows(D_MODEL, R_GB // D_MODEL),
                  rows(FOX_W, R_ZA // FOX_W), rows(S5_W, R_U // S5_W), rows(S5_W, R_ZB // S5_W),
                  rows(S5_W), rows(D_MODEL), rows(D_MODEL)]
                 + [_const(sh) for sh in wshapes] + [_const(sh[::-1]) for sh in wshapes]
                 + [_const((8, D_MODEL)), _const((HEADS, FOX_W))],
        out_specs=(rows(D_MODEL), rows(D_MODEL), rows(D_MODEL), rows(FOX_W), rows(FOX_W), rows(S5_W), rows(S5_W),
                   pl.BlockSpec((HEADS, tm), lambda i: (0, i)),
                   anyspace, anyspace, anyspace, anyspace, pl.BlockSpec((8, D_MODEL), lambda i: (0, 0))),
        out_shape=(jax.ShapeDtypeStruct((s, D_MODEL), F32), jax.ShapeDtypeStruct((s, D_MODEL), BF16),
                   jax.ShapeDtypeStruct((s, D_MODEL), BF16), jax.ShapeDtypeStruct((s, FOX_W), BF16),
                   jax.ShapeDtypeStruct((s, FOX_W), BF16), jax.ShapeDtypeStruct((s, S5_W), BF16),
                   jax.ShapeDtypeStruct((s, S5_W), F32), jax.ShapeDtypeStruct((HEADS, s), F32),
                   jax.ShapeDtypeStruct((D_MODEL, D_MODEL), F32), jax.ShapeDtypeStruct((FOX_W, D_MODEL), F32),
                   jax.ShapeDtypeStruct((S5_W, D_MODEL), F32), jax.ShapeDtypeStruct((S5_W, S5_W), F32),
                   jax.ShapeDtypeStruct((8, D_MODEL), F32)),
        scratch_shapes=[pltpu.VMEM((D_MODEL, D_MODEL), F32), pltpu.VMEM((FOX_W, D_MODEL), F32),
                        pltpu.VMEM((S5_W, D_MODEL), F32), pltpu.VMEM((S5_W, S5_W), F32)],
        compiler_params=_cparams(("arbitrary",)),
    )(o, rest, rest, rest, rest, rest, ys0, x, tgt, *w, *w_t, vec, hsel)


def _dh(dq, dk, dv, dga, dgb, dza, du, dzb, df, wqkv_t, wrest_t, x, dx2, gs):
    s = x.shape[0]
    tm = min(TM, s)

    def body(dq_ref, dk_ref, dv_ref, dga_ref, dgb_ref, dza_ref, du_ref, dzb_ref, df_ref, wq_ref, wr_ref,
             x_ref, dx2_ref, gs_ref, gx_ref, vout_ref):
        @pl.when(pl.program_id(0) == 0)
        def _():
            vout_ref[...] = jnp.zeros_like(vout_ref)

        dh = _dot(dq_ref[...], wq_ref[0:512, :])
        dh += _dot(dk_ref[...], wq_ref[512:1024, :])
        dh += _dot(dv_ref[...], wq_ref[1024:1536, :])
        dh += _dot(dga_ref[...], wr_ref[R_GA:R_GB, :])
        dh += _dot(dgb_ref[...], wr_ref[R_GB:R_ZA, :])
        dh += _dot(dza_ref[...], wr_ref[R_ZA:R_U, :])
        dh += _dot(du_ref[...], wr_ref[R_U:R_ZB, :])
        dh += _dot(dzb_ref[...], wr_ref[R_ZB:R_F, :])
        dh += _dot(df_ref[...], wr_ref[R_F:REST_W, :])
        xv = x_ref[...]
        r = lax.rsqrt(jnp.mean(xv * xv, axis=-1, keepdims=True) + EPS)
        xn = xv * r
        dxn = dh * gs_ref[...]
        gx_ref[...] = dx2_ref[...] + r * (dxn - xn * jnp.mean(dxn * xn, axis=-1, keepdims=True))
        vout_ref[0:1, :] += jnp.sum(dh * xn, axis=0, keepdims=True)
        vout_ref[1:2, :] += jnp.sum(dh, axis=0, keepdims=True)

    def rows(width):
        return pl.BlockSpec((tm, width), lambda i: (i, 0))

    return pl.pallas_call(
        body, name="dh", grid=(s // tm,),
        in_specs=[rows(512), rows(512), rows(512), rows(1024), rows(1024), rows(512), rows(512), rows(512), rows(128),
                  _const((1536, D_MODEL)), _const((REST_W, D_MODEL)), rows(D_MODEL), rows(D_MODEL), _const((1, D_MODEL))],
        out_specs=(rows(D_MODEL), pl.BlockSpec((8, D_MODEL), lambda i: (0, 0))),
        out_shape=(jax.ShapeDtypeStruct((s, D_MODEL), F32), jax.ShapeDtypeStruct((8, D_MODEL), F32)),
        compiler_params=_cparams(("arbitrary",)),
    )(dq, dk, dv, dga, dgb, dza, du, dzb, df, wqkv_t, wrest_t, x, dx2, gs)


def _sum4(parts):
    rows = parts.shape[1]
    br = BIG_ROW_PAD

    def body(p_ref, o_ref):
        acc = p_ref[0].astype(F32)
        for k in range(1, 4):
            acc = acc + p_ref[k].astype(F32)
        o_ref[...] = acc

    return pl.pallas_call(
        body, name="sum4", grid=(rows // br,),
        in_specs=[pl.BlockSpec((4, br, LANES), lambda i: (0, i, 0))],
        out_specs=pl.BlockSpec((br, LANES), lambda i: (i, 0)),
        out_shape=jax.ShapeDtypeStruct((rows, LANES), F32), compiler_params=_cparams(("parallel",)))(parts)


def _row_block(rows):
    for cand in (512, 256, 128, 64, 32, 16, 8):
        if rows % cand == 0:
            return cand
    return rows


def _add2(a, b):
    rows, cols = a.shape
    br = _row_block(rows)

    def body(a_ref, b_ref, o_ref):
        o_ref[...] = a_ref[...] + b_ref[...]

    spec = pl.BlockSpec((br, cols), lambda i: (i, 0))
    return pl.pallas_call(
        body, name="add2", grid=(rows // br,), in_specs=[spec, spec], out_specs=spec,
        out_shape=jax.ShapeDtypeStruct((rows, cols), F32), compiler_params=_cparams(("parallel",)))(a, b)


def _adamw(w, g, m, v, name):
    rows, cols = w.shape
    br = _row_block(rows)

    def body(w_ref, g_ref, m_ref, v_ref, d_ref, nm_ref, nv_ref):
        gv = g_ref[...]
        nm = ADAM_B1 * m_ref[...] + (1.0 - ADAM_B1) * gv
        nv = ADAM_B2 * v_ref[...] + (1.0 - ADAM_B2) * (gv * gv)
        m_hat = nm / (1.0 - ADAM_B1 ** ADAM_STEP)
        v_hat = nv / (1.0 - ADAM_B2 ** ADAM_STEP)
        d_ref[...] = -ADAM_LR * (m_hat / (jnp.sqrt(v_hat) + ADAM_EPS) + ADAM_WD * w_ref[...])
        nm_ref[...] = nm
        nv_ref[...] = nv

    spec = pl.BlockSpec((br, cols), lambda i: (i, 0))
    shape = jax.ShapeDtypeStruct((rows, cols), F32)
    return pl.pallas_call(
        body, name=name, grid=(rows // br,), in_specs=[spec] * 4, out_specs=(spec,) * 3,
        out_shape=(shape,) * 3, compiler_params=_cparams(("parallel",)))(w, g, m, v)


def _pack(parts):
    flat = []
    for p in parts:
        v = p.reshape(-1).astype(F32)
        pad = (-v.shape[0]) % LANES
        if pad:
            v = jnp.concatenate([v, jnp.zeros((pad,), F32)])
        flat.append(v)
    v = jnp.concatenate(flat)
    rows = v.shape[0] // LANES
    pad_rows = (-rows) % 8
    if pad_rows:
        v = jnp.concatenate([v, jnp.zeros((pad_rows * LANES,), F32)])
    return v.reshape(-1, LANES)


BIG_ROW_PAD = 1024


def _pack_big(parts, dtype):
    v = jnp.concatenate([p.reshape(-1) for p in parts]).astype(dtype)
    pad = (-v.shape[0]) % (BIG_ROW_PAD * LANES)
    v = jnp.concatenate([v, jnp.zeros((pad,), dtype)])
    return v.reshape(-1, LANES)


def _unpack(packed, shapes):
    lead = packed.shape[:-2]
    flat = packed.reshape(lead + (-1,))
    out, off = [], 0
    for sh in shapes:
        size = math.prod(sh)
        out.append(flat[..., off:off + size].reshape(lead + tuple(sh)))
        off += size + (-size) % LANES
    return out


def kernel(x, c, w_ada, b_ada, g_norm, w_in, b_f, a_re, a_im, log_dt, b_re, b_im, c_re, c_im, d_skip, w_glu, b_glu, w_up_a, w_up_b, w_out, g_final, loss_target, m_w_ada, m_b_ada, m_g_norm, m_w_in, m_b_f, m_a_re, m_a_im, m_log_dt, m_b_re, m_b_im, m_c_re, m_c_im, m_d_skip, m_w_glu, m_b_glu, m_w_up_a, m_w_up_b, m_w_out, m_g_final, v_w_ada, v_b_ada, v_g_norm, v_w_in, v_b_f, v_a_re, v_a_im, v_log_dt, v_b_re, v_b_im, v_c_re, v_c_im, v_d_skip, v_w_glu, v_b_glu, v_w_up_a, v_w_up_b, v_w_out, v_g_final):
    xi, yi, ci = lax.axis_index("x"), lax.axis_index("y"), lax.axis_index("c")
    chip = 2 * xi + yi
    me = 4 * xi + 2 * yi + ci
    s = x.shape[1]
    x2d = x[0]
    tgt = loss_target[0]
    n_att = s // min(T_ATT, s)
    t_att = min(T_ATT, s)

    c_all, _ = _allgather8(c.reshape(8, LANES), "gather_c")
    c_all = c_all.reshape(8, D_MODEL)
    ncol = w_ada.shape[2]
    b_cols = lax.dynamic_slice_in_dim(b_ada, chip * ncol, ncol, axis=1)
    mod_cols = _mod_cols(c_all, w_ada[0], b_cols)
    mod_all, _ = _allgather8(mod_cols.reshape(-1, LANES), "gather_mod")
    mod_all = mod_all.reshape(4, 2, 8, ncol)[:, 0]
    mod_me = lax.dynamic_index_in_dim(mod_all, me, axis=1, keepdims=False).reshape(1, 3 * D_MODEL)
    shift, scale, gate = mod_me[:, :D_MODEL], mod_me[:, D_MODEL:2 * D_MODEL], mod_me[:, 2 * D_MODEL:]
    gs = g_norm * (1.0 + scale)

    shard_shapes = [w_in.shape[1:], w_glu.shape[1:], w_up_a.shape[1:], w_up_b.shape[1:], w_out.shape[1:]]
    wpack = _pack_big((w_in, w_glu, w_up_a, w_up_b, w_out), BF16)
    w_all = _exchange4(wpack, False, "gather_weights")
    p_in, p_glu, p_ua, p_ub, p_out = _unpack(w_all, shard_shapes)
    w_in_f = jnp.concatenate([p_in[j] for j in range(4)], axis=1)
    w_glu_f = p_glu.reshape(S5_W, S5_W)
    w_ua_f = jnp.concatenate([p_ua[j] for j in range(4)], axis=1)
    w_ub_f = jnp.concatenate([p_ub[j] for j in range(4)], axis=1)
    w_out_f = p_out.reshape(D_MODEL, D_MODEL)
    wqkv = w_in_f[:, O_Q:O_F]
    wrest = jnp.concatenate([w_in_f[:, O_GA:O_GB], w_in_f[:, O_GB:O_END], w_in_f[:, O_ZA:O_U], w_in_f[:, O_U:O_ZB],
                             w_in_f[:, O_ZB:O_GA], w_in_f[:, O_F:O_ZA],
                             jnp.zeros((D_MODEL, REST_W - R_F - HEADS), BF16)], axis=1)
    wmid = (w_glu_f, w_ua_f, w_ub_f, w_out_f)
    wmid_t = tuple(w.T for w in wmid)

    h = _prenorm(x2d, gs, shift)
    qkv = _mm_rows(h, wqkv, BF16, "proj_qkv")
    rest = _mm_rows(h, wrest, F32, "proj_rest")
    bf128 = jnp.pad(b_f, ((0, 0), (0, LANES - HEADS)))
    selp = _head_pair_selector()
    fpc, f_t = _fcum(rest, bf128, selp)
    frow5 = f_t.reshape(4, 2, n_att, 1, t_att)
    o, lse_pc = _attn_fwd(qkv, frow5, fpc)

    abar_r, abar_i, bb_r, bb_i = _ssm_block_params(a_re[0], a_im[0], log_dt[0], b_re[0], b_im[0])
    bd = jnp.concatenate([_block_diag(jnp.swapaxes(bb_r, 1, 2)), _block_diag(jnp.swapaxes(bb_i, 1, 2))], axis=1)
    cd = jnp.concatenate([_block_diag(jnp.swapaxes(c_re[0], 1, 2)), -_block_diag(jnp.swapaxes(c_im[0], 1, 2))], axis=0)
    bd_b, cd_b = bd.astype(BF16), cd.astype(BF16)
    ys0, xs = _ssm_fwd(rest, bd_b, cd_b, _scan_consts(a_re[0], a_im[0], log_dt[0], False))

    vec = jnp.concatenate([gate, g_final.reshape(1, D_MODEL), jnp.concatenate([d_skip, b_glu], axis=1),
                           jnp.zeros((5, D_MODEL), F32)], axis=0)
    hsel = jnp.repeat(jnp.eye(HEADS, dtype=F32), HEAD_DIM, axis=1)
    (dx2, dga, dgb, do, dza, dzb, dys, dlt_t, g_out, g_ua, g_ub, g_glu, vmid) = _mid(
        o, rest, ys0, x2d, tgt, wmid, wmid_t, vec, hsel)

    lse_t = jnp.transpose(lse_pc.reshape(s, 4, LANES)[:, :, :2], (1, 2, 0))
    lse5 = lse_t.reshape(4, 2, n_att, 1, t_att)
    dlt5 = dlt_t.reshape(4, 2, n_att, 1, t_att)
    dq, dk, dv, dfk, dfq = _attn_bwd(qkv, do, lse5, dlt5, frow5, fpc)
    du, g_bd, g_cdt, da8 = _ssm_bwd(dys, xs, rest, bd_b.T, cd_b.T, _scan_consts(a_re[0], a_im[0], log_dt[0], True), d_skip)
    df, dbf8 = _dfcum(dfk, dfq, rest, bf128, selp.T)

    grad_x, vdh = _dh(dq, dk, dv, dga, dgb, dza, du, dzb, df, wqkv.T, wrest.T, x2d, dx2, gs)
    h_t = h.T
    gw = {name: _mm_kacc(h_t, d, "grad_w_in_" + name)
          for name, d in (("q", dq), ("k", dk), ("v", dv), ("ga", dga), ("gb", dgb), ("za", dza), ("u", du),
                          ("zb", dzb), ("f", df))}
    g_in = jnp.concatenate([gw["q"], gw["k"], gw["v"], gw["f"][:, :HEADS], gw["za"], gw["u"], gw["zb"], gw["ga"],
                            gw["gb"]], axis=1)

    dgs, dshift = vdh[0:1], vdh[1:2]
    dmod = jnp.concatenate([dshift, dgs * g_norm, vmid[1:2]], axis=1)
    da = jnp.sum(da8, axis=0)
    g_bbr = jnp.swapaxes(_diag_blocks(g_bd[:, :NSTATE], GCH, STATE), 1, 2)
    g_bbi = jnp.swapaxes(_diag_blocks(g_bd[:, NSTATE:], GCH, STATE), 1, 2)
    g_cre = _diag_blocks(g_cdt[:, :NSTATE], GCH, STATE)
    g_cim = -_diag_blocks(g_cdt[:, NSTATE:], GCH, STATE)
    small_shapes = [(1,), (3 * D_MODEL,), (D_MODEL,), (HEADS,), (GROUPS, STATE), (GROUPS, STATE),
                    (GROUPS, STATE, GCH), (GROUPS, STATE, GCH), (GROUPS, GCH, STATE), (GROUPS, GCH, STATE),
                    (S5_W,), (S5_W,), (D_MODEL,)]
    small = _pack([vmid[3, 0:1], dmod, dgs * (1.0 + scale), dbf8[0, :HEADS], da[:NSTATE], da[NSTATE:],
                   g_bbr, g_bbi, g_cre, g_cim, vmid[2, :S5_W], vmid[2, S5_W:], vmid[0]])
    small_all, small_sum = _allgather8(small, "gather_small_grads")
    (loss_s, g_b_ada, g_g_norm, g_b_f, g_abr, g_abi, g_bbr_s, g_bbi_s, g_c_re, g_c_im, g_d_skip, g_b_glu,
     g_g_final) = _unpack(small_sum, small_shapes)
    loss = loss_s[0]
    dmod_all = _unpack(small_all, small_shapes)[1]
    dmod_cols = lax.dynamic_slice_in_dim(dmod_all, chip * ncol, ncol, axis=1)
    g_w_ada = _grad_w_ada(c_all, dmod_cols)
    _, ssm_vjp = jax.vjp(_ssm_block_params, a_re[0], a_im[0], log_dt[0], b_re[0], b_im[0])
    g_a_re, g_a_im, g_log_dt, g_b_re, g_b_im = ssm_vjp((g_abr, g_abi, g_bbr_s, g_bbi_s))

    def shard_cols(g, j):
        n = g.shape[1] // 4
        return g[:, j * n:(j + 1) * n]

    def shard_rows(g, j):
        n = g.shape[0] // 4
        return g[j * n:(j + 1) * n]

    gpack = jnp.stack([
        _pack_big((shard_cols(g_in, j), shard_rows(g_glu, j), shard_cols(g_ua, j), shard_cols(g_ub, j),
                   shard_rows(g_out, j)), BF16)
        for j in range(4)])
    parts = _exchange4(gpack, True, "scatter_weight_grads")
    mine = _sum4(parts)
    theirs = _swap_sibling(mine, "swap_weight_grads")
    g_in_s, g_glu_s, g_ua_s, g_ub_s, g_out_s = _unpack(_add2(mine, theirs), shard_shapes)

    def adam(name, w, g, m, v):
        shape = w.shape
        total = math.prod(shape)
        if len(shape) > 1 and shape[-1] >= LANES:
            cols = shape[-1]
        elif total % LANES == 0:
            cols = LANES
        else:
            cols = total
        two = lambda a: a.reshape(-1, cols)
        d, nm, nv = _adamw(two(w), two(g), two(m), two(v), "adamw_" + name)
        return g.reshape(shape), d.reshape(shape), nm.reshape(shape), nv.reshape(shape)

    res = [
        adam("w_ada", w_ada, g_w_ada, m_w_ada, v_w_ada),
        adam("b_ada", b_ada, g_b_ada, m_b_ada, v_b_ada),
        adam("g_norm", g_norm, g_g_norm, m_g_norm, v_g_norm),
        adam("w_in", w_in, g_in_s, m_w_in, v_w_in),
        adam("b_f", b_f, g_b_f, m_b_f, v_b_f),
        adam("a_re", a_re, g_a_re, m_a_re, v_a_re),
        adam("a_im", a_im, g_a_im, m_a_im, v_a_im),
        adam("log_dt", log_dt, g_log_dt, m_log_dt, v_log_dt),
        adam("b_re", b_re, g_b_re, m_b_re, v_b_re),
        adam("b_im", b_im, g_b_im, m_b_im, v_b_im),
        adam("c_re", c_re, g_c_re, m_c_re, v_c_re),
        adam("c_im", c_im, g_c_im, m_c_im, v_c_im),
        adam("d_skip", d_skip, g_d_skip, m_d_skip, v_d_skip),
        adam("w_glu", w_glu, g_glu_s, m_w_glu, v_w_glu),
        adam("b_glu", b_glu, g_b_glu, m_b_glu, v_b_glu),
        adam("w_up_a", w_up_a, g_ua_s, m_w_up_a, v_w_up_a),
        adam("w_up_b", w_up_b, g_ub_s, m_w_up_b, v_w_up_b),
        adam("w_out", w_out, g_out_s, m_w_out, v_w_out),
        adam("g_final", g_final, g_g_final, m_g_final, v_g_final),
    ]
    grads = [r[0] for r in res]
    deltas = [r[1] for r in res]
    new_m = [r[2] for r in res]
    new_v = [r[3] for r in res]
    return (loss, grad_x[None], *grads, *deltas, *new_m, *new_v)
```

```python
import functools
import math

import jax
import jax.numpy as jnp
from jax import lax
from jax.experimental import pallas as pl
from jax.experimental.pallas import tpu as pltpu

F32 = jnp.float32
BF16 = jnp.bfloat16
HI = lax.Precision.HIGHEST
MESH = pl.DeviceIdType.MESH

D_MODEL = 1024
HEADS = 8
HEAD_DIM = 64
FOX_W = 512
S5_W = 512
GROUPS = 32
STATE = 64
GCH = 16
NSTATE = GROUPS * STATE
EPS = 1e-6
NEG = -1e30

ADAM_LR = 0.001
ADAM_B1 = 0.9
ADAM_B2 = 0.999
ADAM_EPS = 1e-08
ADAM_WD = 0.01
ADAM_STEP = 10

VMEM_LIMIT = 56 * 1024 * 1024
LANES = 128

TM = 256
T_ATT = 512
ATT_CHUNK = 32
ATT_PAIRS = 4
TB_SSM = 256
TK_ACC = 512
TB_CUM = 256
SHARD_ROWS = 1312

O_Q, O_K, O_V, O_F, O_ZA, O_U, O_ZB, O_GA, O_GB, O_END = 0, 512, 1024, 1536, 1544, 2056, 2568, 3080, 4104, 5128
REST_W = 3712
R_GA, R_GB, R_ZA, R_U, R_ZB, R_F = 0, 1024, 2048, 2560, 3072, 3584


def _cparams(sem=None):
    kw = dict(vmem_limit_bytes=VMEM_LIMIT)
    if sem is not None:
        kw["dimension_semantics"] = sem
    return pltpu.CompilerParams(**kw)


def _const(shape):
    nd = len(shape)
    return pl.BlockSpec(shape, lambda *_: (0,) * nd, pipeline_mode=pl.Buffered(1))


def _dot(a, b, precision=None):
    return jnp.dot(a, b, preferred_element_type=F32, precision=precision)


def _dot_nt(a, b):
    return lax.dot_general(a, b, (((1,), (1,)), ((), ())), preferred_element_type=F32)


def _dot_tn(a, b, precision=None):
    return lax.dot_general(a, b, (((0,), (0,)), ((), ())), preferred_element_type=F32, precision=precision)


def _sigmoid(z):
    return 1.0 / (1.0 + jnp.exp(-z))


def _allgather8(xs, name):
    rows = xs.shape[0]

    def body(x_ref, out_ref, sum_ref, send_sems, recv_sems, local_sem):
        x, y, c = lax.axis_index("x"), lax.axis_index("y"), lax.axis_index("c")
        me, sibling = (x, y, c), (x, y, 1 - c)
        chips = [(1 - x, y), (x, 1 - y), (1 - x, 1 - y)]

        def slot(px, py, pc):
            return out_ref.at[4 * px + 2 * py + pc]

        def copy(k, block, to, src=None):
            return pltpu.make_async_remote_copy(
                src_ref=slot(*block) if src is None else src, dst_ref=slot(*block),
                send_sem=send_sems.at[k], recv_sem=recv_sems.at[k], device_id=to, device_id_type=MESH)

        mine = pltpu.make_async_copy(x_ref, slot(*me), local_sem)
        mine.start()
        first = [copy(0, me, sibling, src=x_ref)]
        first += [copy(1 + j, me, (*chip, c), src=x_ref) for j, chip in enumerate(chips)]
        for cp in first:
            cp.start()
        passed = [copy(4 + j, (*chip, c), sibling) for j, chip in enumerate(chips)]
        for j, chip in enumerate(chips):
            copy(1 + j, (*chip, c), me).wait_recv()
            passed[j].start()
        copy(0, sibling, me).wait_recv()
        for j, chip in enumerate(chips):
            copy(4 + j, (*chip, 1 - c), me).wait_recv()
        for cp in first + passed:
            cp.wait_send()
        mine.wait()
        acc = out_ref[0]
        for d in range(1, 8):
            acc = acc + out_ref[d]
        sum_ref[...] = acc

    return pl.pallas_call(
        body, name=name,
        out_shape=(jax.ShapeDtypeStruct((8, rows, LANES), F32), jax.ShapeDtypeStruct((rows, LANES), F32)),
        in_specs=[pl.BlockSpec(memory_space=pltpu.VMEM)],
        out_specs=(pl.BlockSpec(memory_space=pltpu.VMEM), pl.BlockSpec(memory_space=pltpu.VMEM)),
        scratch_shapes=[pltpu.SemaphoreType.DMA((7,)), pltpu.SemaphoreType.DMA((7,)), pltpu.SemaphoreType.DMA],
        compiler_params=_cparams(),
    )(xs)


def _exchange4(srcs, scatter, name):
    na = len(srcs)

    def body(*refs):
        src_refs, out_refs = refs[:na], refs[na:2 * na]
        send_sems, recv_sems, local_sems = refs[2 * na:]
        x, y, c = lax.axis_index("x"), lax.axis_index("y"), lax.axis_index("c")
        peers = [(1 - x, y), (x, 1 - y), (1 - x, 1 - y)]

        def block_for(a, px, py):
            return src_refs[a].at[2 * px + py] if scatter else src_refs[a]

        def copy(a, k, px, py, slot):
            return pltpu.make_async_remote_copy(
                src_ref=block_for(a, px, py), dst_ref=out_refs[a].at[slot],
                send_sem=send_sems.at[a * 3 + k], recv_sem=recv_sems.at[a * 3 + k],
                device_id=(px, py, c), device_id_type=MESH)

        locals_ = [pltpu.make_async_copy(block_for(a, x, y), out_refs[a].at[2 * x + y], local_sems.at[a])
                   for a in range(na)]
        for cp in locals_:
            cp.start()
        sends = [copy(a, k, px, py, 2 * x + y) for a in range(na) for k, (px, py) in enumerate(peers)]
        for cp in sends:
            cp.start()
        for a in range(na):
            for k, (px, py) in enumerate(peers):
                copy(a, k, px, py, 2 * px + py).wait_recv()
        for cp in sends:
            cp.wait_send()
        for cp in locals_:
            cp.wait()

    anyspace = pl.BlockSpec(memory_space=pl.ANY)
    return pl.pallas_call(
        body, name=name,
        out_shape=tuple(jax.ShapeDtypeStruct((4,) + tuple(a.shape[-2:]), a.dtype) for a in srcs),
        in_specs=[anyspace] * na, out_specs=(anyspace,) * na,
        scratch_shapes=[pltpu.SemaphoreType.DMA((3 * na,)), pltpu.SemaphoreType.DMA((3 * na,)),
                        pltpu.SemaphoreType.DMA((na,))],
        compiler_params=_cparams(),
    )(*srcs)


def _swap_sibling(srcs, name):
    na = len(srcs)

    def body(*refs):
        src_refs, out_refs = refs[:na], refs[na:2 * na]
        send_sems, recv_sems = refs[2 * na:]
        x, y, c = lax.axis_index("x"), lax.axis_index("y"), lax.axis_index("c")
        copies = [pltpu.make_async_remote_copy(
            src_ref=src_refs[a], dst_ref=out_refs[a], send_sem=send_sems.at[a], recv_sem=recv_sems.at[a],
            device_id=(x, y, 1 - c), device_id_type=MESH) for a in range(na)]
        for cp in copies:
            cp.start()
        for cp in copies:
            cp.wait()

    anyspace = pl.BlockSpec(memory_space=pl.ANY)
    return pl.pallas_call(
        body, name=name,
        out_shape=tuple(jax.ShapeDtypeStruct(a.shape, a.dtype) for a in srcs),
        in_specs=[anyspace] * na, out_specs=(anyspace,) * na,
        scratch_shapes=[pltpu.SemaphoreType.DMA((na,)), pltpu.SemaphoreType.DMA((na,))],
        compiler_params=_cparams(),
    )(*srcs)


def _mod_cols(c_all, w, b):
    n = w.shape[1]

    def body(c_ref, w_ref, b_ref, o_ref):
        o_ref[...] = _dot(c_ref[...], w_ref[...], HI) + b_ref[...]

    return pl.pallas_call(
        body, name="mod_cols", out_shape=jax.ShapeDtypeStruct((8, n), F32),
        compiler_params=_cparams())(c_all, w, b)


def _grad_w_ada(c_all, dmod_cols):
    n = dmod_cols.shape[1]

    def body(c_ref, d_ref, o_ref):
        o_ref[...] = _dot_tn(c_ref[...], d_ref[...], HI)

    return pl.pallas_call(
        body, name="grad_w_ada", out_shape=jax.ShapeDtypeStruct((D_MODEL, n), F32),
        compiler_params=_cparams())(c_all, dmod_cols)


def _prenorm(x, gs, shift):
    s = x.shape[0]
    tm = min(TM, s)

    def body(x_ref, gs_ref, sh_ref, h_ref):
        xv = x_ref[...]
        r = lax.rsqrt(jnp.mean(xv * xv, axis=-1, keepdims=True) + EPS)
        h_ref[...] = (xv * r * gs_ref[...] + sh_ref[...]).astype(BF16)

    row = pl.BlockSpec((tm, D_MODEL), lambda i: (i, 0))
    vec = pl.BlockSpec((1, D_MODEL), lambda i: (0, 0))
    return pl.pallas_call(
        body, name="prenorm", grid=(s // tm,), in_specs=[row, vec, vec], out_specs=row,
        out_shape=jax.ShapeDtypeStruct((s, D_MODEL), BF16), compiler_params=_cparams(("parallel",)))(x, gs, shift)


def _mm_rows_nt(a, w_t, out_dtype, name):
    s, k = a.shape
    n = w_t.shape[0]
    tm = min(TM, s)

    def body(a_ref, w_ref, o_ref):
        o_ref[...] = _dot_nt(a_ref[...], w_ref[...]).astype(out_dtype)

    return pl.pallas_call(
        body, name=name, grid=(s // tm,),
        in_specs=[pl.BlockSpec((tm, k), lambda i: (i, 0)), _const((n, k))],
        out_specs=pl.BlockSpec((tm, n), lambda i: (i, 0)),
        out_shape=jax.ShapeDtypeStruct((s, n), out_dtype), compiler_params=_cparams(("parallel",)))(a, w_t)


def _grad_w_rows(h, ds):
    s = h.shape[0]
    tk = min(TK_ACC, s)
    nd = len(ds)
    widths = [d.shape[1] for d in ds]

    def body(*refs):
        h_ref, d_refs = refs[0], refs[1:1 + nd]
        out_refs, accs = refs[1 + nd:1 + 2 * nd], refs[1 + 2 * nd:]
        step = pl.program_id(0)

        @pl.when(step == 0)
        def _():
            for acc in accs:
                acc[...] = jnp.zeros_like(acc)

        hv = h_ref[...]
        for d_ref, acc in zip(d_refs, accs):
            acc[...] += _dot_tn(d_ref[...], hv)

        @pl.when(step == s // tk - 1)
        def _():
            for acc, out in zip(accs, out_refs):
                pltpu.sync_copy(acc, out)

    anyspace = pl.BlockSpec(memory_space=pl.ANY)
    return pl.pallas_call(
        body, name="grad_w_in", grid=(s // tk,),
        in_specs=[pl.BlockSpec((tk, D_MODEL), lambda k: (k, 0))]
                 + [pl.BlockSpec((tk, w), lambda k: (k, 0)) for w in widths],
        out_specs=(anyspace,) * nd,
        out_shape=tuple(jax.ShapeDtypeStruct((w, D_MODEL), F32) for w in widths),
        scratch_shapes=[pltpu.VMEM((w, D_MODEL), F32) for w in widths],
        compiler_params=_cparams(("arbitrary",)))(h, *ds)


def _head_pair_selector():
    rows = jnp.arange(LANES)[:, None]
    cols = jnp.arange(4 * LANES)[None, :]
    return ((rows < HEADS) & (cols == (rows // 2) * LANES + rows % 2)).astype(F32)


def _fcum(rest, bf128, selp):
    s = rest.shape[0]
    tb = min(TB_CUM, s)

    def body(fz_ref, bf_ref, sel_ref, fpc_ref, ft_ref, carry_ref):
        @pl.when(pl.program_id(0) == 0)
        def _():
            carry_ref[...] = jnp.zeros_like(carry_ref)

        z = fz_ref[...] + bf_ref[...]
        logf = jnp.minimum(z, 0.0) - jnp.log(1.0 + jnp.exp(-jnp.abs(z)))
        r = lax.broadcasted_iota(jnp.int32, (tb, tb), 0)
        c = lax.broadcasted_iota(jnp.int32, (tb, tb), 1)
        tri = (c <= r).astype(F32)
        f = _dot(tri, logf, HI) + carry_ref[0:1, :]
        carry_ref[0:1, :] = f[tb - 1:tb, :]
        fpc_ref[...] = _dot(f, sel_ref[...], HI)
        ft_ref[...] = jnp.transpose(f)[0:HEADS, :]

    return pl.pallas_call(
        body, name="forget_cumsum", grid=(s // tb,),
        in_specs=[pl.BlockSpec((tb, LANES), lambda i: (i, R_F // LANES)), _const((1, LANES)), _const((LANES, 4 * LANES))],
        out_specs=(pl.BlockSpec((tb, 4 * LANES), lambda i: (i, 0)), pl.BlockSpec((HEADS, tb), lambda i: (0, i))),
        out_shape=(jax.ShapeDtypeStruct((s, 4 * LANES), F32), jax.ShapeDtypeStruct((HEADS, s), F32)),
        scratch_shapes=[pltpu.VMEM((8, LANES), F32)],
        compiler_params=_cparams(("arbitrary",)))(rest, bf128, selp)


def _dfcum(dfk, dfq, rest, bf128, selq):
    s = rest.shape[0]
    tb = min(TB_CUM, s)
    nb = s // tb

    def body(dk_ref, dq_ref, fz_ref, bf_ref, sel_ref, df_ref, dbf_ref, carry_ref):
        @pl.when(pl.program_id(0) == 0)
        def _():
            carry_ref[...] = jnp.zeros_like(carry_ref)
            dbf_ref[...] = jnp.zeros_like(dbf_ref)

        d = _dot(dk_ref[...] + dq_ref[...], sel_ref[...], HI)
        r = lax.broadcasted_iota(jnp.int32, (tb, tb), 0)
        c = lax.broadcasted_iota(jnp.int32, (tb, tb), 1)
        triu = (c >= r).astype(F32)
        dlogf = _dot(triu, d, HI) + carry_ref[0:1, :]
        carry_ref[0:1, :] = dlogf[0:1, :]
        z = fz_ref[...] + bf_ref[...]
        df = dlogf * (1.0 / (1.0 + jnp.exp(z)))
        df_ref[...] = df.astype(BF16)
        dbf_ref[0:1, :] += jnp.sum(df, axis=0, keepdims=True)

    return pl.pallas_call(
        body, name="forget_grad", grid=(nb,),
        in_specs=[pl.BlockSpec((tb, 4 * LANES), lambda i: (nb - 1 - i, 0)),
                  pl.BlockSpec((tb, 4 * LANES), lambda i: (nb - 1 - i, 0)),
                  pl.BlockSpec((tb, LANES), lambda i: (nb - 1 - i, R_F // LANES)),
                  _const((1, LANES)), _const((4 * LANES, LANES))],
        out_specs=(pl.BlockSpec((tb, LANES), lambda i: (nb - 1 - i, 0)), pl.BlockSpec((8, LANES), lambda i: (0, 0))),
        out_shape=(jax.ShapeDtypeStruct((s, LANES), BF16), jax.ShapeDtypeStruct((8, LANES), F32)),
        scratch_shapes=[pltpu.VMEM((8, LANES), F32)],
        compiler_params=_cparams(("arbitrary",)))(dfk, dfq, rest, bf128, selq)


def _scaled(q):
    return (q.astype(F32) * (HEAD_DIM ** -0.5)).astype(BF16)


def _attn_fwd(qkv, frow5, fpc):
    s = qkv.shape[0]
    t = min(T_ATT, s)
    n = s // t
    ch = min(ATT_CHUNK, t)
    wide = 2 * LANES
    pairs = ATT_PAIRS
    width = pairs * LANES
    groups = 4 // pairs

    def body(q_ref, k_ref, v_ref, fr_ref, fc_ref, o_ref, lse_ref, s_scr, p_scr, m_scr, a_scr, fq_scr, acc_scr):
        i = pl.program_id(1)
        lane = lax.broadcasted_iota(jnp.int32, (t, LANES), 1)
        first = lane < HEAD_DIM
        ones_col = ((lane == 0).astype(BF16), (lane == 1).astype(BF16))
        m_scr[...] = jnp.full(m_scr.shape, NEG, F32)
        acc_scr[...] = jnp.zeros_like(acc_scr)
        qm = []
        for pp in range(pairs):
            q = _scaled(q_ref[:, pp * LANES:(pp + 1) * LANES])
            zq = jnp.zeros_like(q)
            qm += [jnp.where(first, q, zq), jnp.where(first, zq, q)]
            fq_scr[2 * pp] = fc_ref[:, pp * LANES:pp * LANES + 1]
            fq_scr[2 * pp + 1] = fc_ref[:, pp * LANES + 1:pp * LANES + 2]

        def step(j, masked):
            r0 = pl.multiple_of(j * t, t)
            vaug = []
            for pp in range(pairs):
                kb = k_ref[pl.ds(r0, t), pp * LANES:(pp + 1) * LANES]
                vb = v_ref[pl.ds(r0, t), pp * LANES:(pp + 1) * LANES]
                zv = jnp.zeros_like(vb)
                vaug += [jnp.concatenate([jnp.where(first, vb, zv), ones_col[0]], axis=1),
                         jnp.concatenate([jnp.where(first, zv, vb), ones_col[1]], axis=1)]
                for hh in range(2):
                    s_scr[2 * pp + hh] = _dot_nt(qm[2 * pp + hh], kb)
            pv = []
            for hd in range(2 * pairs):
                fk = fr_ref[hd // 2, hd % 2, j]
                for c in range(t // ch):
                    rows = pl.ds(c * ch, ch)
                    sc = s_scr[hd, rows, :] - fk
                    if masked:
                        rq = c * ch + lax.broadcasted_iota(jnp.int32, (ch, t), 0)
                        ck = lax.broadcasted_iota(jnp.int32, (ch, t), 1)
                        sc = jnp.where(ck <= rq, sc, NEG)
                    fq = fq_scr[hd, rows, :]
                    m_old = m_scr[hd, rows, :]
                    m_new = jnp.maximum(m_old, fq + jnp.max(sc, axis=1, keepdims=True))
                    p_scr[hd, rows, :] = jnp.exp(sc + (fq - m_new)).astype(BF16)
                    a_scr[hd, rows, :] = jnp.exp(m_old - m_new)
                    m_scr[hd, rows, :] = m_new
                pv.append(_dot(p_scr[hd], vaug[hd]))
            for pp in range(pairs):
                a0, a1 = a_scr[2 * pp], a_scr[2 * pp + 1]
                alpha = jnp.concatenate([jnp.where(first, a0, a1), jnp.where(lane == 0, a0, a1)], axis=1)
                acc_scr[pp] = acc_scr[pp] * alpha + pv[2 * pp] + pv[2 * pp + 1]
            return 0

        lax.fori_loop(0, i, lambda j, _: step(j, False), 0)
        step(i, True)
        for pp in range(pairs):
            l0 = acc_scr[pp, :, LANES:LANES + 1]
            l1 = acc_scr[pp, :, LANES + 1:LANES + 2]
            o_ref[:, pp * LANES:(pp + 1) * LANES] = acc_scr[pp, :, 0:LANES] * jnp.where(first, 1.0 / l0, 1.0 / l1)
            lse0 = m_scr[2 * pp] + jnp.log(l0)
            lse1 = m_scr[2 * pp + 1] + jnp.log(l1)
            lse_ref[:, pp * LANES:(pp + 1) * LANES] = jnp.where(lane == 0, lse0, jnp.where(lane == 1, lse1, 0.0))

    blk = pl.BlockSpec((t, width), lambda g, i: (i, g))
    return pl.pallas_call(
        body, name="attn_fwd", grid=(groups, n),
        in_specs=[blk,
                  pl.BlockSpec((s, width), lambda g, i: (0, groups + g)),
                  pl.BlockSpec((s, width), lambda g, i: (0, 2 * groups + g)),
                  pl.BlockSpec((pairs, 2, n, 1, t), lambda g, i: (g, 0, 0, 0, 0)),
                  blk],
        out_specs=(blk, blk),
        out_shape=(jax.ShapeDtypeStruct((s, FOX_W), F32), jax.ShapeDtypeStruct((s, 4 * LANES), F32)),
        scratch_shapes=[pltpu.VMEM((2 * pairs, t, t), F32), pltpu.VMEM((2 * pairs, t, t), BF16),
                        pltpu.VMEM((2 * pairs, t, 1), F32), pltpu.VMEM((2 * pairs, t, 1), F32),
                        pltpu.VMEM((2 * pairs, t, 1), F32), pltpu.VMEM((pairs, t, wide), F32)],
        compiler_params=_cparams(("parallel", "arbitrary")))(qkv, qkv, qkv, frow5, fpc)


def _attn_bwd(qkv, do, lse5, dlt5, frow5, fpc):
    s = qkv.shape[0]
    t = min(T_ATT, s)
    n = s // t
    wide = 2 * LANES

    ch = min(ATT_CHUNK, t)

    def body(q_ref, do_ref, k_ref, v_ref, lse_ref, dl_ref, fr_ref, fc_ref,
             dq_ref, dk_ref, dv_ref, dfk_ref, dfq_ref, dq_acc, st_scr, dp_scr, pt_scr, ds_scr, dk_acc, dv_acc, fk_scr):
        j = pl.program_id(1)

        @pl.when(j == 0)
        def _():
            dq_acc[...] = jnp.zeros_like(dq_acc)

        dk_acc[...] = jnp.zeros_like(dk_acc)
        dv_acc[...] = jnp.zeros_like(dv_acc)
        lane = lax.broadcasted_iota(jnp.int32, (t, LANES), 1)
        first = lane < HEAD_DIM
        ones_col = ((lane == 0).astype(BF16), (lane == 1).astype(BF16))
        kb = k_ref[...]
        vb = v_ref[...]
        zk = jnp.zeros_like(kb)
        kaug = (jnp.concatenate([jnp.where(first, kb, zk), ones_col[0]], axis=1),
                jnp.concatenate([jnp.where(first, zk, kb), ones_col[1]], axis=1))
        fk_scr[0] = fc_ref[:, 0:1]
        fk_scr[1] = fc_ref[:, 1:2]

        def step(i, masked):
            r0 = pl.multiple_of(i * t, t)
            qb = _scaled(q_ref[pl.ds(r0, t), :])
            dob = do_ref[pl.ds(r0, t), :]
            zq = jnp.zeros_like(qb)
            qm = (jnp.where(first, qb, zq), jnp.where(first, zq, qb))
            dom = (jnp.where(first, dob, zq), jnp.where(first, zq, dob))
            dq_add = jnp.zeros((t, wide), F32)
            for hh in range(2):
                st_scr[hh] = _dot_nt(kb, qm[hh])
                dp_scr[hh] = _dot_nt(vb, dom[hh])
                bias = fr_ref[0, hh, i] - lse_ref[0, hh, i]
                dl = dl_ref[0, hh, i]
                for c in range(t // ch):
                    rows = pl.ds(c * ch, ch)
                    st = st_scr[hh, rows, :] + (bias - fk_scr[hh, rows, :])
                    if masked:
                        rk = c * ch + lax.broadcasted_iota(jnp.int32, (ch, t), 0)
                        cq = lax.broadcasted_iota(jnp.int32, (ch, t), 1)
                        st = jnp.where(rk <= cq, st, NEG)
                    pt = jnp.exp(st)
                    pt_scr[hh, rows, :] = pt.astype(BF16)
                    ds_scr[hh, rows, :] = (pt * (dp_scr[hh, rows, :] - dl)).astype(BF16)
                dsb = ds_scr[hh]
                dv_acc[...] += _dot(pt_scr[hh], dom[hh])
                dk_acc[...] += _dot(dsb, jnp.concatenate([qm[hh], ones_col[hh]], axis=1))
                dq_add = dq_add + _dot_tn(dsb, kaug[hh])
            dq_acc[pl.ds(r0, t), :] += dq_add
            return 0

        step(j, True)
        lax.fori_loop(j + 1, n, lambda i, _: step(i, False), 0)
        dk_ref[...] = dk_acc[:, 0:LANES].astype(BF16)
        dv_ref[...] = dv_acc[...].astype(BF16)
        dfk_ref[...] = -dk_acc[:, LANES:wide]

        @pl.when(j == n - 1)
        def _():
            dq_ref[...] = (dq_acc[:, 0:LANES] * (HEAD_DIM ** -0.5)).astype(BF16)
            dfq_ref[...] = dq_acc[:, LANES:wide]

    stat = pl.BlockSpec((1, 2, n, 1, t), lambda h, j: (h, 0, 0, 0, 0))
    blk = pl.BlockSpec((t, LANES), lambda h, j: (j, h))
    full = pl.BlockSpec((s, LANES), lambda h, j: (0, h))
    return pl.pallas_call(
        body, name="attn_bwd", grid=(4, n),
        in_specs=[full, full,
                  pl.BlockSpec((t, LANES), lambda h, j: (j, 4 + h)),
                  pl.BlockSpec((t, LANES), lambda h, j: (j, 8 + h)),
                  stat, stat, stat, blk],
        out_specs=(full, blk, blk, blk, full),
        out_shape=(jax.ShapeDtypeStruct((s, FOX_W), BF16), jax.ShapeDtypeStruct((s, FOX_W), BF16),
                   jax.ShapeDtypeStruct((s, FOX_W), BF16), jax.ShapeDtypeStruct((s, 4 * LANES), F32),
                   jax.ShapeDtypeStruct((s, 4 * LANES), F32)),
        scratch_shapes=[pltpu.VMEM((s, wide), F32), pltpu.VMEM((2, t, t), F32), pltpu.VMEM((2, t, t), F32),
                        pltpu.VMEM((2, t, t), BF16), pltpu.VMEM((2, t, t), BF16), pltpu.VMEM((t, wide), F32),
                        pltpu.VMEM((t, LANES), F32), pltpu.VMEM((2, t, 1), F32)],
        compiler_params=_cparams(("parallel", "arbitrary")))(qkv, do, qkv, qkv, lse5, dlt5, frow5, fpc)


def _ssm_block_params(a_re, a_im, log_dt, b_re, b_im):
    dt = jnp.exp(log_dt)[:, None]
    mag = jnp.exp(a_re * dt)
    ar = mag * jnp.cos(a_im * dt)
    ai = mag * jnp.sin(a_im * dt)
    den = a_re * a_re + a_im * a_im
    nr = ar - 1.0
    cr = (nr * a_re + ai * a_im) / den
    ci = (ai * a_re - nr * a_im) / den
    bbr = cr[:, :, None] * b_re - ci[:, :, None] * b_im
    bbi = cr[:, :, None] * b_im + ci[:, :, None] * b_re
    return ar, ai, bbr, bbi


def _block_diag(blocks):
    g, r, c = blocks.shape
    eye = jnp.eye(g, dtype=blocks.dtype)
    return (blocks[:, :, None, :] * eye[:, None, :, None]).reshape(g * r, g * c)


def _scan_consts(a_re, a_im, log_dt, reverse):
    dt = jnp.exp(log_dt)[:, None]
    lr = (a_re * dt).reshape(1, NSTATE)
    li = (a_im * dt).reshape(1, NSTATE)
    if reverse:
        li = -li
    rows = jnp.arange(8, dtype=F32)[:, None]

    def power(k):
        mag = jnp.exp(k * lr)
        return mag * jnp.cos(k * li), mag * jnp.sin(k * li)

    tiles = []
    for k in (1, 2, 4):
        keep = (rows < 8 - k) if reverse else (rows >= k)
        pr, pi_ = power(float(k))
        tiles += [jnp.where(keep, pr, 0.0), jnp.where(keep, pi_, 0.0)]
    expo = (8.0 - rows) if reverse else (rows + 1.0)
    tiles += list(power(expo))
    return jnp.stack([jnp.broadcast_to(tl, (8, NSTATE)) for tl in tiles])


_SCAN_W = 512
_HALF_W = S5_W // 2
_HALF_S = NSTATE // 2


def _half_expand(v, w_ref, out_ref):
    for half in range(2):
        vh = v[:, half * _HALF_W:(half + 1) * _HALF_W]
        for part in range(2):
            c0 = part * NSTATE + half * _HALF_S
            out_ref[:, c0:c0 + _HALF_S] = _dot(vh, w_ref[half * _HALF_W:(half + 1) * _HALF_W, c0:c0 + _HALF_S])


def _half_contract(x_ref, w_ref, half):
    out = None
    for part in range(2):
        r0 = part * NSTATE + half * _HALF_S
        term = _dot(x_ref[:, r0:r0 + _HALF_S].astype(BF16),
                    w_ref[r0:r0 + _HALF_S, half * _HALF_W:(half + 1) * _HALF_W])
        out = term if out is None else out + term
    return out


def _half_outer(v, x_ref, acc_ref):
    for half in range(2):
        vh = v[:, half * _HALF_W:(half + 1) * _HALF_W]
        for part in range(2):
            c0 = part * NSTATE + half * _HALF_S
            acc_ref[:, c0:c0 + _HALF_S] += _dot_tn(vh, x_ref[:, c0:c0 + _HALF_S].astype(BF16))


def _ssm_fwd(rest, bd, cd, consts):
    s = rest.shape[0]
    tb = min(TB_SSM, s)
    ns2 = 2 * NSTATE

    def body(u_ref, bd_ref, cd_ref, cf_ref, y_ref, x_ref, cb_ref):
        @pl.when(pl.program_id(0) == 0)
        def _():
            cb_ref[...] = jnp.zeros_like(cb_ref)

        _half_expand(u_ref[...].astype(BF16), bd_ref, x_ref)

        def tile(ti, _):
            r0 = pl.multiple_of(ti * 8, 8)
            for cc in range(NSTATE // _SCAN_W):
                cr = pl.ds(cc * _SCAN_W, _SCAN_W)
                ci = pl.ds(NSTATE + cc * _SCAN_W, _SCAN_W)
                re = x_ref[pl.ds(r0, 8), cr]
                im = x_ref[pl.ds(r0, 8), ci]
                for n_, k in enumerate((1, 2, 4)):
                    ar = cf_ref[2 * n_, :, cr]
                    ai = cf_ref[2 * n_ + 1, :, cr]
                    sr = pltpu.roll(re, k, 0)
                    si = pltpu.roll(im, k, 0)
                    re, im = re + ar * sr - ai * si, im + ar * si + ai * sr
                pr = cf_ref[6, :, cr]
                pi_ = cf_ref[7, :, cr]
                cbr = cb_ref[:, cr]
                cbi = cb_ref[:, ci]
                re, im = re + pr * cbr - pi_ * cbi, im + pr * cbi + pi_ * cbr
                x_ref[pl.ds(r0, 8), cr] = re
                x_ref[pl.ds(r0, 8), ci] = im
                cb_ref[:, cr] = jnp.broadcast_to(re[7:8, :], (8, _SCAN_W))
                cb_ref[:, ci] = jnp.broadcast_to(im[7:8, :], (8, _SCAN_W))
            return 0

        lax.fori_loop(0, tb // 8, tile, 0)
        for half in range(2):
            y_ref[:, half * _HALF_W:(half + 1) * _HALF_W] = _half_contract(x_ref, cd_ref, half)

    return pl.pallas_call(
        body, name="ssm_fwd", grid=(s // tb,),
        in_specs=[pl.BlockSpec((tb, S5_W), lambda i: (i, R_U // S5_W)), _const((S5_W, ns2)), _const((ns2, S5_W)),
                  _const((8, 8, NSTATE))],
        out_specs=(pl.BlockSpec((tb, S5_W), lambda i: (i, 0)), pl.BlockSpec((tb, ns2), lambda i: (i, 0))),
        out_shape=(jax.ShapeDtypeStruct((s, S5_W), F32), jax.ShapeDtypeStruct((s, ns2), F32)),
        scratch_shapes=[pltpu.VMEM((8, ns2), F32)],
        compiler_params=_cparams(("arbitrary",)))(rest, bd, cd, consts)


def _ssm_bwd(dys, xs, rest, bd_t, cd_t, consts, dskip):
    s = dys.shape[0]
    tb = min(TB_SSM, s)
    nb = s // tb
    ns2 = 2 * NSTATE
    nt = tb // 8

    def body(dy_ref, x_ref, u_ref, bdt_ref, cdt_ref, cf_ref, dsk_ref, du_ref, gb_ref, gc_ref, da_ref,
             g_ref, cb_ref, acc_b, acc_c):
        step = pl.program_id(0)

        @pl.when(step == 0)
        def _():
            cb_ref[...] = jnp.zeros_like(cb_ref)
            acc_b[...] = jnp.zeros_like(acc_b)
            acc_c[...] = jnp.zeros_like(acc_c)
            da_ref[...] = jnp.zeros_like(da_ref)

        dy = dy_ref[...]
        dyb = dy.astype(BF16)
        _half_expand(dyb, cdt_ref, g_ref)
        last_row = lax.broadcasted_iota(jnp.int32, (8, _SCAN_W), 0) == 7

        def tile(tt, _):
            r0 = pl.multiple_of((nt - 1 - tt) * 8, 8)
            for cc in range(NSTATE // _SCAN_W):
                cr = pl.ds(cc * _SCAN_W, _SCAN_W)
                ci = pl.ds(NSTATE + cc * _SCAN_W, _SCAN_W)
                re = g_ref[pl.ds(r0, 8), cr]
                im = g_ref[pl.ds(r0, 8), ci]
                for n_, k in enumerate((1, 2, 4)):
                    ar = cf_ref[2 * n_, :, cr]
                    ai = cf_ref[2 * n_ + 1, :, cr]
                    sr = pltpu.roll(re, 8 - k, 0)
                    si = pltpu.roll(im, 8 - k, 0)
                    re, im = re + ar * sr - ai * si, im + ar * si + ai * sr
                pr = cf_ref[6, :, cr]
                pi_ = cf_ref[7, :, cr]
                cbr = cb_ref[:, cr]
                cbi = cb_ref[:, ci]
                re, im = re + pr * cbr - pi_ * cbi, im + pr * cbi + pi_ * cbr
                g_ref[pl.ds(r0, 8), cr] = re
                g_ref[pl.ds(r0, 8), ci] = im
                gnr = jnp.where(last_row, cbr, pltpu.roll(re, 7, 0))
                gni = jnp.where(last_row, cbi, pltpu.roll(im, 7, 0))
                xr = x_ref[pl.ds(r0, 8), cr]
                xi = x_ref[pl.ds(r0, 8), ci]
                da_ref[:, cr] += gnr * xr + gni * xi
                da_ref[:, ci] += gni * xr - gnr * xi
                cb_ref[:, cr] = jnp.broadcast_to(re[0:1, :], (8, _SCAN_W))
                cb_ref[:, ci] = jnp.broadcast_to(im[0:1, :], (8, _SCAN_W))
            return 0

        lax.fori_loop(0, nt, tile, 0)
        for half in range(2):
            cols = slice(half * _HALF_W, (half + 1) * _HALF_W)
            du_ref[:, cols] = (_half_contract(g_ref, bdt_ref, half) + dy[:, cols] * dsk_ref[:, cols]).astype(BF16)
        _half_outer(u_ref[...].astype(BF16), g_ref, acc_b)
        _half_outer(dyb, x_ref, acc_c)

        @pl.when(step == nb - 1)
        def _():
            for g in range(GROUPS):
                src = slice((g % (GROUPS // 2)) * GCH, (g % (GROUPS // 2) + 1) * GCH)
                dst = slice(g * GCH, (g + 1) * GCH)
                for part in range(2):
                    cols = slice(part * NSTATE + g * STATE, part * NSTATE + (g + 1) * STATE)
                    gb_ref[dst, part * STATE:(part + 1) * STATE] = acc_b[src, cols]
                    gc_ref[dst, part * STATE:(part + 1) * STATE] = acc_c[src, cols]

    rev = lambda i: (nb - 1 - i, 0)
    small = pl.BlockSpec((S5_W, 2 * STATE), lambda i: (0, 0))
    return pl.pallas_call(
        body, name="ssm_bwd", grid=(nb,),
        in_specs=[pl.BlockSpec((tb, S5_W), rev), pl.BlockSpec((tb, ns2), rev),
                  pl.BlockSpec((tb, S5_W), lambda i: (nb - 1 - i, R_U // S5_W)),
                  _const((ns2, S5_W)), _const((S5_W, ns2)), _const((8, 8, NSTATE)), _const((1, S5_W))],
        out_specs=(pl.BlockSpec((tb, S5_W), rev), small, small, pl.BlockSpec((8, ns2), lambda i: (0, 0))),
        out_shape=(jax.ShapeDtypeStruct((s, S5_W), BF16), jax.ShapeDtypeStruct((S5_W, 2 * STATE), F32),
                   jax.ShapeDtypeStruct((S5_W, 2 * STATE), F32), jax.ShapeDtypeStruct((8, ns2), F32)),
        scratch_shapes=[pltpu.VMEM((tb, ns2), F32), pltpu.VMEM((8, ns2), F32),
                        pltpu.VMEM((_HALF_W, ns2), F32), pltpu.VMEM((_HALF_W, ns2), F32)],
        compiler_params=_cparams(("arbitrary",)))(dys, xs, rest, bd_t, cd_t, consts, dskip)


_GELU_C = math.sqrt(2.0 / math.pi)
_GELU_A = 0.044715


def _mid(o, rest, ys0, x, tgt, w, vec, hsel):
    s = o.shape[0]
    tm = min(TM, s)
    nsteps = s // tm
    half = FOX_W

    def body(o_ref, ga_ref, gb_ref, za_ref, u_ref, zb_ref, ys0_ref, x_ref, t_ref,
             wglu_ref, wua_ref, wub_ref, wout_ref, vec_ref, hsel_ref,
             dx2_ref, dga_ref, dgb_ref, do_ref, dza_ref, dzb_ref, dys_ref, dlt_ref,
             gout_hbm, gua_hbm, gub_hbm, gglu_hbm, vout_ref,
             a_out, a_ua, a_ub, a_glu):
        step = pl.program_id(0)

        @pl.when(step == 0)
        def _():
            a_out[...] = jnp.zeros_like(a_out)
            a_ua[...] = jnp.zeros_like(a_ua)
            a_ub[...] = jnp.zeros_like(a_ub)
            a_glu[...] = jnp.zeros_like(a_glu)
            vout_ref[...] = jnp.zeros_like(vout_ref)

        gate = vec_ref[0:1, :]
        gfin = vec_ref[1:2, :]
        dsk = vec_ref[2:3, 0:half]
        bglu = vec_ref[2:3, half:2 * half]

        o_v = o_ref[...]
        za = za_ref[...]
        sza = _sigmoid(za)
        silu_za = za * sza
        ya_b = (o_v * silu_za).astype(BF16)
        u_v = u_ref[...]
        ys = ys0_ref[...] + dsk * u_v
        inner = _GELU_C * (ys + _GELU_A * ys * ys * ys)
        th = jnp.tanh(inner)
        yg = 0.5 * ys * (1.0 + th)
        yg_b = yg.astype(BF16)
        st = _sigmoid(_dot(yg_b, wglu_ref[...]) + bglu)
        yb1 = yg * st
        zb = zb_ref[...]
        szb = _sigmoid(zb)
        silu_zb = zb * szb
        yb_b = (yb1 * silu_zb).astype(BF16)
        ua = _dot(ya_b, wua_ref[...])
        ub = _dot(yb_b, wub_ref[...])
        sga = _sigmoid(ga_ref[...])
        sgb = _sigmoid(gb_ref[...])
        merged_b = (sga * ua + sgb * ub).astype(BF16)
        mo = _dot(merged_b, wout_ref[...])
        x2 = x_ref[...] + gate * mo
        r2 = lax.rsqrt(jnp.mean(x2 * x2, axis=-1, keepdims=True) + EPS)
        x2n = x2 * r2
        diff = x2n * gfin - t_ref[...]
        loss = 0.5 * jnp.sum(jnp.mean(diff * diff, axis=-1, keepdims=True), axis=0, keepdims=True)
        dy = diff * (1.0 / D_MODEL)
        dx2n = dy * gfin
        dx2 = r2 * (dx2n - x2n * jnp.mean(dx2n * x2n, axis=-1, keepdims=True))
        dx2_ref[...] = dx2
        vout_ref[0:1, :] += jnp.sum(dy * x2n, axis=0, keepdims=True)
        vout_ref[1:2, :] += jnp.sum(dx2 * mo, axis=0, keepdims=True)
        vout_ref[3:4, :] += jnp.broadcast_to(loss, (1, D_MODEL))
        dmo_b = (dx2 * gate).astype(BF16)
        dmerged = _dot_nt(dmo_b, wout_ref[...])
        a_out[...] += _dot_tn(merged_b, dmo_b)
        dua_b = (dmerged * sga).astype(BF16)
        dub_b = (dmerged * sgb).astype(BF16)
        dga_ref[...] = (dmerged * ua * sga * (1.0 - sga)).astype(BF16)
        dgb_ref[...] = (dmerged * ub * sgb * (1.0 - sgb)).astype(BF16)
        dya = _dot_nt(dua_b, wua_ref[...])
        dyb = _dot_nt(dub_b, wub_ref[...])
        a_ua[...] += _dot_tn(ya_b, dua_b)
        a_ub[...] += _dot_tn(yb_b, dub_b)
        do_b = (dya * silu_za).astype(BF16)
        do_ref[...] = do_b
        dza_ref[...] = (dya * o_v * (sza * (1.0 + za * (1.0 - sza)))).astype(BF16)
        dlt_ref[...] = lax.dot_general(hsel_ref[...], do_b.astype(F32) * o_v, (((1,), (1,)), ((), ())),
                                       preferred_element_type=F32, precision=HI)
        dyb1 = dyb * silu_zb
        dzb_ref[...] = (dyb * yb1 * (szb * (1.0 + zb * (1.0 - szb)))).astype(BF16)
        dt = dyb1 * yg * st * (1.0 - st)
        dt_b = dt.astype(BF16)
        dyg = dyb1 * st + _dot_nt(dt_b, wglu_ref[...])
        a_glu[...] += _dot_tn(yg_b, dt_b)
        dgelu = 0.5 * (1.0 + th) + 0.5 * ys * (1.0 - th * th) * _GELU_C * (1.0 + 3.0 * _GELU_A * ys * ys)
        dys = dyg * dgelu
        dys_ref[...] = dys
        vout_ref[2:3, 0:half] += jnp.sum(dys * u_v, axis=0, keepdims=True)
        vout_ref[2:3, half:2 * half] += jnp.sum(dt, axis=0, keepdims=True)

        @pl.when(step == nsteps - 1)
        def _():
            pltpu.sync_copy(a_out, gout_hbm)
            pltpu.sync_copy(a_ua, gua_hbm)
            pltpu.sync_copy(a_ub, gub_hbm)
            pltpu.sync_copy(a_glu, gglu_hbm)

    def rows(width, col=0):
        return pl.BlockSpec((tm, width), lambda i, col=col: (i, col))

    anyspace = pl.BlockSpec(memory_space=pl.ANY)
    wshapes = [(S5_W, S5_W), (FOX_W, D_MODEL), (S5_W, D_MODEL), (D_MODEL, D_MODEL)]
    return pl.pallas_call(
        body, name="mid", grid=(nsteps,),
        in_specs=[rows(FOX_W), rows(D_MODEL, R_GA // D_MODEL), rows(D_MODEL, R_GB // D_MODEL),
                  rows(FOX_W, R_ZA // FOX_W), rows(S5_W, R_U // S5_W), rows(S5_W, R_ZB // S5_W),
                  rows(S5_W), rows(D_MODEL), rows(D_MODEL)]
                 + [_const(sh) for sh in wshapes]
                 + [_const((8, D_MODEL)), _const((HEADS, FOX_W))],
        out_specs=(rows(D_MODEL), rows(D_MODEL), rows(D_MODEL), rows(FOX_W), rows(FOX_W), rows(S5_W), rows(S5_W),
                   pl.BlockSpec((HEADS, tm), lambda i: (0, i)),
                   anyspace, anyspace, anyspace, anyspace, pl.BlockSpec((8, D_MODEL), lambda i: (0, 0))),
        out_shape=(jax.ShapeDtypeStruct((s, D_MODEL), F32), jax.ShapeDtypeStruct((s, D_MODEL), BF16),
                   jax.ShapeDtypeStruct((s, D_MODEL), BF16), jax.ShapeDtypeStruct((s, FOX_W), BF16),
                   jax.ShapeDtypeStruct((s, FOX_W), BF16), jax.ShapeDtypeStruct((s, S5_W), BF16),
                   jax.ShapeDtypeStruct((s, S5_W), F32), jax.ShapeDtypeStruct((HEADS, s), F32),
                   jax.ShapeDtypeStruct((D_MODEL, D_MODEL), F32), jax.ShapeDtypeStruct((FOX_W, D_MODEL), F32),
                   jax.ShapeDtypeStruct((S5_W, D_MODEL), F32), jax.ShapeDtypeStruct((S5_W, S5_W), F32),
                   jax.ShapeDtypeStruct((8, D_MODEL), F32)),
        scratch_shapes=[pltpu.VMEM((D_MODEL, D_MODEL), F32), pltpu.VMEM((FOX_W, D_MODEL), F32),
                        pltpu.VMEM((S5_W, D_MODEL), F32), pltpu.VMEM((S5_W, S5_W), F32)],
        compiler_params=_cparams(("arbitrary",)),
    )(o, rest, rest, rest, rest, rest, ys0, x, tgt, *w, vec, hsel)


def _dh(dq, dk, dv, dga, dgb, dza, du, dzb, df, wqkv_t, wrest_t, x, dx2, gs):
    s = x.shape[0]
    tm = min(TM, s)

    def body(dq_ref, dk_ref, dv_ref, dga_ref, dgb_ref, dza_ref, du_ref, dzb_ref, df_ref, wq_ref, wr_ref,
             x_ref, dx2_ref, gs_ref, gx_ref, vout_ref):
        @pl.when(pl.program_id(0) == 0)
        def _():
            vout_ref[...] = jnp.zeros_like(vout_ref)

        dh = _dot(dq_ref[...], wq_ref[0:512, :])
        dh += _dot(dk_ref[...], wq_ref[512:1024, :])
        dh += _dot(dv_ref[...], wq_ref[1024:1536, :])
        dh += _dot(dga_ref[...], wr_ref[R_GA:R_GB, :])
        dh += _dot(dgb_ref[...], wr_ref[R_GB:R_ZA, :])
        dh += _dot(dza_ref[...], wr_ref[R_ZA:R_U, :])
        dh += _dot(du_ref[...], wr_ref[R_U:R_ZB, :])
        dh += _dot(dzb_ref[...], wr_ref[R_ZB:R_F, :])
        dh += _dot(df_ref[...], wr_ref[R_F:REST_W, :])
        xv = x_ref[...]
        r = lax.rsqrt(jnp.mean(xv * xv, axis=-1, keepdims=True) + EPS)
        xn = xv * r
        dxn = dh * gs_ref[...]
        gx_ref[...] = dx2_ref[...] + r * (dxn - xn * jnp.mean(dxn * xn, axis=-1, keepdims=True))
        vout_ref[0:1, :] += jnp.sum(dh * xn, axis=0, keepdims=True)
        vout_ref[1:2, :] += jnp.sum(dh, axis=0, keepdims=True)

    def rows(width):
        return pl.BlockSpec((tm, width), lambda i: (i, 0))

    return pl.pallas_call(
        body, name="dh", grid=(s // tm,),
        in_specs=[rows(512), rows(512), rows(512), rows(1024), rows(1024), rows(512), rows(512), rows(512), rows(128),
                  _const((1536, D_MODEL)), _const((REST_W, D_MODEL)), rows(D_MODEL), rows(D_MODEL), _const((1, D_MODEL))],
        out_specs=(rows(D_MODEL), pl.BlockSpec((8, D_MODEL), lambda i: (0, 0))),
        out_shape=(jax.ShapeDtypeStruct((s, D_MODEL), F32), jax.ShapeDtypeStruct((8, D_MODEL), F32)),
        compiler_params=_cparams(("arbitrary",)),
    )(dq, dk, dv, dga, dgb, dza, du, dzb, df, wqkv_t, wrest_t, x, dx2, gs)


def _row_block(rows, mult=8, cap=512):
    if rows <= mult:
        return rows
    padded = -(-rows // mult) * mult
    for cand in range(min(cap, padded) // mult * mult, 0, -mult):
        if padded % cand == 0:
            return cand
    return padded


def _sum4(parts, name):
    rows, cols = parts.shape[1:]
    br = _row_block(rows, 16)

    def body(p_ref, o_ref):
        acc = p_ref[0].astype(F32)
        for k in range(1, 4):
            acc = acc + p_ref[k].astype(F32)
        o_ref[...] = acc

    return pl.pallas_call(
        body, name=name, grid=(pl.cdiv(rows, br),),
        in_specs=[pl.BlockSpec((4, br, cols), lambda i: (0, i, 0))],
        out_specs=pl.BlockSpec((br, cols), lambda i: (i, 0)),
        out_shape=jax.ShapeDtypeStruct((rows, cols), F32), compiler_params=_cparams(("parallel",)))(parts)


def _add2(a, b, name):
    rows, cols = a.shape
    br = _row_block(rows)

    def body(a_ref, b_ref, o_ref):
        o_ref[...] = a_ref[...] + b_ref[...]

    spec = pl.BlockSpec((br, cols), lambda i: (i, 0))
    return pl.pallas_call(
        body, name=name, grid=(pl.cdiv(rows, br),), in_specs=[spec, spec], out_specs=spec,
        out_shape=jax.ShapeDtypeStruct((rows, cols), F32), compiler_params=_cparams(("parallel",)))(a, b)


def _adamw(w, g, m, v, name):
    rows, cols = w.shape
    br = _row_block(rows)

    def body(w_ref, g_ref, m_ref, v_ref, d_ref, nm_ref, nv_ref):
        gv = g_ref[...]
        nm = ADAM_B1 * m_ref[...] + (1.0 - ADAM_B1) * gv
        nv = ADAM_B2 * v_ref[...] + (1.0 - ADAM_B2) * (gv * gv)
        m_hat = nm / (1.0 - ADAM_B1 ** ADAM_STEP)
        v_hat = nv / (1.0 - ADAM_B2 ** ADAM_STEP)
        d_ref[...] = -ADAM_LR * (m_hat / (jnp.sqrt(v_hat) + ADAM_EPS) + ADAM_WD * w_ref[...])
        nm_ref[...] = nm
        nv_ref[...] = nv

    spec = pl.BlockSpec((br, cols), lambda i: (i, 0))
    shape = jax.ShapeDtypeStruct((rows, cols), F32)
    return pl.pallas_call(
        body, name=name, grid=(pl.cdiv(rows, br),), in_specs=[spec] * 4, out_specs=(spec,) * 3,
        out_shape=(shape,) * 3, compiler_params=_cparams(("parallel",)))(w, g, m, v)


def _pack(parts):
    flat = []
    for p in parts:
        v = p.reshape(-1).astype(F32)
        pad = (-v.shape[0]) % LANES
        if pad:
            v = jnp.concatenate([v, jnp.zeros((pad,), F32)])
        flat.append(v)
    v = jnp.concatenate(flat)
    rows = v.shape[0] // LANES
    pad_rows = (-rows) % 8
    if pad_rows:
        v = jnp.concatenate([v, jnp.zeros((pad_rows * LANES,), F32)])
    return v.reshape(-1, LANES)


def _unpack(packed, shapes):
    lead = packed.shape[:-2]
    flat = packed.reshape(lead + (-1,))
    out, off = [], 0
    for sh in shapes:
        size = math.prod(sh)
        out.append(flat[..., off:off + size].reshape(lead + tuple(sh)))
        off += size + (-size) % LANES
    return out


def kernel(x, c, w_ada, b_ada, g_norm, w_in, b_f, a_re, a_im, log_dt, b_re, b_im, c_re, c_im, d_skip, w_glu, b_glu, w_up_a, w_up_b, w_out, g_final, loss_target, m_w_ada, m_b_ada, m_g_norm, m_w_in, m_b_f, m_a_re, m_a_im, m_log_dt, m_b_re, m_b_im, m_c_re, m_c_im, m_d_skip, m_w_glu, m_b_glu, m_w_up_a, m_w_up_b, m_w_out, m_g_final, v_w_ada, v_b_ada, v_g_norm, v_w_in, v_b_f, v_a_re, v_a_im, v_log_dt, v_b_re, v_b_im, v_c_re, v_c_im, v_d_skip, v_w_glu, v_b_glu, v_w_up_a, v_w_up_b, v_w_out, v_g_final):
    xi, yi, ci = lax.axis_index("x"), lax.axis_index("y"), lax.axis_index("c")
    chip = 2 * xi + yi
    me = 4 * xi + 2 * yi + ci
    s = x.shape[1]
    x2d = x[0]
    tgt = loss_target[0]
    n_att = s // min(T_ATT, s)
    t_att = min(T_ATT, s)

    c_all, _ = _allgather8(c.reshape(8, LANES), "gather_c")
    c_all = c_all.reshape(8, D_MODEL)
    ncol = w_ada.shape[2]
    b_cols = lax.dynamic_slice_in_dim(b_ada, chip * ncol, ncol, axis=1)
    mod_cols = _mod_cols(c_all, w_ada[0], b_cols)
    mod_all, _ = _allgather8(mod_cols.reshape(-1, LANES), "gather_mod")
    mod_all = mod_all.reshape(4, 2, 8, ncol)[:, 0]
    mod_me = lax.dynamic_index_in_dim(mod_all, me, axis=1, keepdims=False).reshape(1, 3 * D_MODEL)
    shift, scale, gate = mod_me[:, :D_MODEL], mod_me[:, D_MODEL:2 * D_MODEL], mod_me[:, 2 * D_MODEL:]
    gs = g_norm * (1.0 + scale)

    nshard = w_in.shape[2]
    w_in_t, m_in_t, v_in_t = (jnp.swapaxes(a[0], 0, 1) for a in (w_in, m_w_in, v_w_in))
    wt_pack = jnp.pad(w_in_t.astype(BF16), ((0, SHARD_ROWS - nshard), (0, 0)))
    misc_shapes = [w_glu.shape[1:], w_up_a.shape[1:], w_up_b.shape[1:], w_out.shape[1:]]
    misc_pack = jnp.concatenate([w.reshape(-1) for w in (w_glu, w_up_a, w_up_b, w_out)]).astype(BF16).reshape(-1, LANES)
    wt_all, misc_all = _exchange4([wt_pack, misc_pack], False, "gather_weights")
    p_glu, p_ua, p_ub, p_out = _unpack(misc_all, misc_shapes)

    def w_rows(lo, hi):
        out = []
        for j in range(4):
            a, b = max(lo, j * nshard), min(hi, (j + 1) * nshard)
            if a < b:
                out.append(wt_all[j, a - j * nshard:b - j * nshard])
        return out

    wqkv_t = jnp.concatenate(w_rows(O_Q, O_F), axis=0)
    wrest_t = jnp.concatenate(w_rows(O_GA, O_GB) + w_rows(O_GB, O_END) + w_rows(O_ZA, O_U) + w_rows(O_U, O_ZB)
                              + w_rows(O_ZB, O_GA) + w_rows(O_F, O_ZA)
                              + [jnp.zeros((REST_W - R_F - HEADS, D_MODEL), BF16)], axis=0)
    wmid = (p_glu.reshape(S5_W, S5_W), jnp.concatenate([p_ua[j] for j in range(4)], axis=1),
            jnp.concatenate([p_ub[j] for j in range(4)], axis=1), p_out.reshape(D_MODEL, D_MODEL))

    h = _prenorm(x2d, gs, shift)
    qkv = _mm_rows_nt(h, wqkv_t, BF16, "proj_qkv")
    rest = _mm_rows_nt(h, wrest_t, F32, "proj_rest")
    bf128 = jnp.pad(b_f, ((0, 0), (0, LANES - HEADS)))
    selp = _head_pair_selector()
    fpc, f_t = _fcum(rest, bf128, selp)
    frow5 = f_t.reshape(4, 2, n_att, 1, t_att)
    o, lse_pc = _attn_fwd(qkv, frow5, fpc)

    abar_r, abar_i, bb_r, bb_i = _ssm_block_params(a_re[0], a_im[0], log_dt[0], b_re[0], b_im[0])
    bb_rt, bb_it = jnp.swapaxes(bb_r, 1, 2).astype(BF16), jnp.swapaxes(bb_i, 1, 2).astype(BF16)
    cr_b, ci_b = c_re[0].astype(BF16), (-c_im[0]).astype(BF16)
    bd_b = jnp.concatenate([_block_diag(bb_rt), _block_diag(bb_it)], axis=1)
    cd_b = jnp.concatenate([_block_diag(jnp.swapaxes(cr_b, 1, 2)), _block_diag(jnp.swapaxes(ci_b, 1, 2))], axis=0)
    bd_t = jnp.concatenate([_block_diag(jnp.swapaxes(bb_rt, 1, 2)), _block_diag(jnp.swapaxes(bb_it, 1, 2))], axis=0)
    cd_t = jnp.concatenate([_block_diag(cr_b), _block_diag(ci_b)], axis=1)
    ys0, xs = _ssm_fwd(rest, bd_b, cd_b, _scan_consts(a_re[0], a_im[0], log_dt[0], False))

    vec = jnp.concatenate([gate, g_final.reshape(1, D_MODEL), jnp.concatenate([d_skip, b_glu], axis=1),
                           jnp.zeros((5, D_MODEL), F32)], axis=0)
    hsel = jnp.repeat(jnp.eye(HEADS, dtype=F32), HEAD_DIM, axis=1)
    (dx2, dga, dgb, do, dza, dzb, dys, dlt_t, g_out, g_ua, g_ub, g_glu, vmid) = _mid(
        o, rest, ys0, x2d, tgt, wmid, vec, hsel)

    lse_t = jnp.transpose(lse_pc.reshape(s, 4, LANES)[:, :, :2], (1, 2, 0))
    lse5 = lse_t.reshape(4, 2, n_att, 1, t_att)
    dlt5 = dlt_t.reshape(4, 2, n_att, 1, t_att)
    dq, dk, dv, dfk, dfq = _attn_bwd(qkv, do, lse5, dlt5, frow5, fpc)
    du, g_bd, g_cdt, da8 = _ssm_bwd(dys, xs, rest, bd_t, cd_t, _scan_consts(a_re[0], a_im[0], log_dt[0], True), d_skip)
    df, dbf8 = _dfcum(dfk, dfq, rest, bf128, selp.T)

    grad_x, vdh = _dh(dq, dk, dv, dga, dgb, dza, du, dzb, df, wqkv_t, wrest_t, x2d, dx2, gs)
    gq, gk, gv, gga, ggb, gza, gu, gzb, gf = _grad_w_rows(h, [dq, dk, dv, dga, dgb, dza, du, dzb, df])
    g_in_t = jnp.concatenate([gq, gk, gv, gf[:HEADS], gza, gu, gzb, gga, ggb], axis=0)

    dgs, dshift = vdh[0:1], vdh[1:2]
    dmod = jnp.concatenate([dshift, dgs * g_norm, vmid[1:2]], axis=1)
    da = jnp.sum(da8, axis=0)
    g_bd = g_bd.reshape(GROUPS, GCH, 2 * STATE)
    g_cdt = g_cdt.reshape(GROUPS, GCH, 2 * STATE)
    g_bbr = jnp.swapaxes(g_bd[:, :, :STATE], 1, 2)
    g_bbi = jnp.swapaxes(g_bd[:, :, STATE:], 1, 2)
    g_cre = g_cdt[:, :, :STATE]
    g_cim = -g_cdt[:, :, STATE:]
    small_shapes = [(1,), (3 * D_MODEL,), (D_MODEL,), (HEADS,), (GROUPS, STATE), (GROUPS, STATE),
                    (GROUPS, STATE, GCH), (GROUPS, STATE, GCH), (GROUPS, GCH, STATE), (GROUPS, GCH, STATE),
                    (S5_W,), (S5_W,), (D_MODEL,)]
    small = _pack([vmid[3, 0:1], dmod, dgs * (1.0 + scale), dbf8[0, :HEADS], da[:NSTATE], da[NSTATE:],
                   g_bbr, g_bbi, g_cre, g_cim, vmid[2, :S5_W], vmid[2, S5_W:], vmid[0]])
    small_all, small_sum = _allgather8(small, "gather_small_grads")
    (loss_s, g_b_ada, g_g_norm, g_b_f, g_abr, g_abi, g_bbr_s, g_bbi_s, g_c_re, g_c_im, g_d_skip, g_b_glu,
     g_g_final) = _unpack(small_sum, small_shapes)
    loss = loss_s[0]
    dmod_all = _unpack(small_all, small_shapes)[1]
    dmod_cols = lax.dynamic_slice_in_dim(dmod_all, chip * ncol, ncol, axis=1)
    g_w_ada = _grad_w_ada(c_all, dmod_cols)
    _, ssm_vjp = jax.vjp(_ssm_block_params, a_re[0], a_im[0], log_dt[0], b_re[0], b_im[0])
    g_a_re, g_a_im, g_log_dt, g_b_re, g_b_im = ssm_vjp((g_abr, g_abi, g_bbr_s, g_bbi_s))

    def shard_cols(g, j):
        n = g.shape[1] // 4
        return g[:, j * n:(j + 1) * n]


    def shard_rows(g, j):
        n = g.shape[0] // 4
        return g[j * n:(j + 1) * n]

    gt_pack = jnp.stack([
        jnp.pad(g_in_t[j * nshard:(j + 1) * nshard].astype(BF16), ((0, SHARD_ROWS - nshard), (0, 0)))
        for j in range(4)])
    gm_pack = jnp.stack([
        jnp.concatenate([shard_rows(g_glu, j).reshape(-1), shard_cols(g_ua, j).reshape(-1),
                         shard_cols(g_ub, j).reshape(-1), shard_rows(g_out, j).reshape(-1)]).astype(BF16)
        .reshape(-1, LANES) for j in range(4)])
    parts_in, parts_misc = _exchange4([gt_pack, gm_pack], True, "scatter_weight_grads")
    mine_in, mine_misc = _sum4(parts_in, "sum4_w_in"), _sum4(parts_misc, "sum4_misc")
    theirs_in, theirs_misc = _swap_sibling([mine_in, mine_misc], "swap_weight_grads")
    tot_in = _add2(mine_in, theirs_in, "add2_w_in")
    g_glu_s, g_ua_s, g_ub_s, g_out_s = _unpack(_add2(mine_misc, theirs_misc, "add2_misc"), misc_shapes)

    def adam(name, w, g, m, v):
        shape = w.shape
        total = math.prod(shape)
        if len(shape) > 1 and shape[-1] >= LANES:
            cols = shape[-1]
        elif total % LANES == 0:
            cols = LANES
        else:
            cols = total
        two = lambda a: a.reshape(-1, cols)
        d, nm, nv = _adamw(two(w), two(g), two(m), two(v), "adamw_" + name)
        return g.reshape(shape), d.reshape(shape), nm.reshape(shape), nv.reshape(shape)

    back = lambda a: jnp.swapaxes(a, 0, 1)[None]
    d_in_t, nm_in_t, nv_in_t = _adamw(w_in_t, tot_in, m_in_t, v_in_t, "adamw_w_in")
    res_w_in = (back(tot_in[:nshard]), back(d_in_t), back(nm_in_t), back(nv_in_t))

    res = [
        adam("w_ada", w_ada, g_w_ada, m_w_ada, v_w_ada),
        adam("b_ada", b_ada, g_b_ada, m_b_ada, v_b_ada),
        adam("g_norm", g_norm, g_g_norm, m_g_norm, v_g_norm),
        res_w_in,
        adam("b_f", b_f, g_b_f, m_b_f, v_b_f),
        adam("a_re", a_re, g_a_re, m_a_re, v_a_re),
        adam("a_im", a_im, g_a_im, m_a_im, v_a_im),
        adam("log_dt", log_dt, g_log_dt, m_log_dt, v_log_dt),
        adam("b_re", b_re, g_b_re, m_b_re, v_b_re),
        adam("b_im", b_im, g_b_im, m_b_im, v_b_im),
        adam("c_re", c_re, g_c_re, m_c_re, v_c_re),
        adam("c_im", c_im, g_c_im, m_c_im, v_c_im),
        adam("d_skip", d_skip, g_d_skip, m_d_skip, v_d_skip),
        adam("w_glu", w_glu, g_glu_s, m_w_glu, v_w_glu),
        adam("b_glu", b_glu, g_b_glu, m_b_glu, v_b_glu),
        adam("w_up_a", w_up_a, g_ua_s, m_w_up_a, v_w_up_a),
        adam("w_up_b", w_up_b, g_ub_s, m_w_up_b, v_w_up_b),
        adam("w_out", w_out, g_out_s, m_w_out, v_w_out),
        adam("g_final", g_final, g_g_final, m_g_final, v_g_final),
    ]
    grads = [r[0] for r in res]
    deltas = [r[1] for r in res]
    new_m = [r[2] for r in res]
    new_v = [r[3] for r in res]
    return (loss, grad_x[None], *grads, *deltas, *new_m, *new_v)
```

```python
import functools
import math

import jax
import jax.numpy as jnp
from jax import lax
from jax.experimental import pallas as pl
from jax.experimental.pallas import tpu as pltpu

F32 = jnp.float32
BF16 = jnp.bfloat16
HI = lax.Precision.HIGHEST
MESH = pl.DeviceIdType.MESH

D_MODEL = 1024
HEADS = 8
HEAD_DIM = 64
FOX_W = 512
S5_W = 512
GROUPS = 32
STATE = 64
GCH = 16
NSTATE = GROUPS * STATE
EPS = 1e-6
NEG = -1e30

ADAM_LR = 0.001
ADAM_B1 = 0.9
ADAM_B2 = 0.999
ADAM_EPS = 1e-08
ADAM_WD = 0.01
ADAM_STEP = 10

VMEM_LIMIT = 56 * 1024 * 1024
LANES = 128

TM = 256
T_ATT = 512
ATT_CHUNK = 32
ATT_PAIRS = 4
TB_SSM = 256
TK_ACC = 512
TB_CUM = 256
SHARD_ROWS = 1312

O_Q, O_K, O_V, O_F, O_ZA, O_U, O_ZB, O_GA, O_GB, O_END = 0, 512, 1024, 1536, 1544, 2056, 2568, 3080, 4104, 5128
REST_W = 3712
R_GA, R_GB, R_ZA, R_U, R_ZB, R_F = 0, 1024, 2048, 2560, 3072, 3584


def _cparams(sem=None):
    kw = dict(vmem_limit_bytes=VMEM_LIMIT)
    if sem is not None:
        kw["dimension_semantics"] = sem
    return pltpu.CompilerParams(**kw)


def _const(shape):
    nd = len(shape)
    return pl.BlockSpec(shape, lambda *_: (0,) * nd, pipeline_mode=pl.Buffered(1))


def _dot(a, b, precision=None):
    return jnp.dot(a, b, preferred_element_type=F32, precision=precision)


def _dot_nt(a, b):
    return lax.dot_general(a, b, (((1,), (1,)), ((), ())), preferred_element_type=F32)


def _dot_tn(a, b, precision=None):
    return lax.dot_general(a, b, (((0,), (0,)), ((), ())), preferred_element_type=F32, precision=precision)


def _sigmoid(z):
    return 1.0 / (1.0 + jnp.exp(-z))


def _allgather8(xs, name):
    rows = xs.shape[0]

    def body(x_ref, out_ref, sum_ref, send_sems, recv_sems, local_sem):
        x, y, c = lax.axis_index("x"), lax.axis_index("y"), lax.axis_index("c")
        me, sibling = (x, y, c), (x, y, 1 - c)
        chips = [(1 - x, y), (x, 1 - y), (1 - x, 1 - y)]

        def slot(px, py, pc):
            return out_ref.at[4 * px + 2 * py + pc]

        def copy(k, block, to, src=None):
            return pltpu.make_async_remote_copy(
                src_ref=slot(*block) if src is None else src, dst_ref=slot(*block),
                send_sem=send_sems.at[k], recv_sem=recv_sems.at[k], device_id=to, device_id_type=MESH)

        mine = pltpu.make_async_copy(x_ref, slot(*me), local_sem)
        mine.start()
        first = [copy(0, me, sibling, src=x_ref)]
        first += [copy(1 + j, me, (*chip, c), src=x_ref) for j, chip in enumerate(chips)]
        for cp in first:
            cp.start()
        passed = [copy(4 + j, (*chip, c), sibling) for j, chip in enumerate(chips)]
        for j, chip in enumerate(chips):
            copy(1 + j, (*chip, c), me).wait_recv()
            passed[j].start()
        copy(0, sibling, me).wait_recv()
        for j, chip in enumerate(chips):
            copy(4 + j, (*chip, 1 - c), me).wait_recv()
        for cp in first + passed:
            cp.wait_send()
        mine.wait()
        acc = out_ref[0]
        for d in range(1, 8):
            acc = acc + out_ref[d]
        sum_ref[...] = acc

    return pl.pallas_call(
        body, name=name,
        out_shape=(jax.ShapeDtypeStruct((8, rows, LANES), F32), jax.ShapeDtypeStruct((rows, LANES), F32)),
        in_specs=[pl.BlockSpec(memory_space=pltpu.VMEM)],
        out_specs=(pl.BlockSpec(memory_space=pltpu.VMEM), pl.BlockSpec(memory_space=pltpu.VMEM)),
        scratch_shapes=[pltpu.SemaphoreType.DMA((7,)), pltpu.SemaphoreType.DMA((7,)), pltpu.SemaphoreType.DMA],
        compiler_params=_cparams(),
    )(xs)


def _exchange4(srcs, scatter, name):
    na = len(srcs)

    def body(*refs):
        src_refs, out_refs = refs[:na], refs[na:2 * na]
        send_sems, recv_sems, local_sems = refs[2 * na:]
        x, y, c = lax.axis_index("x"), lax.axis_index("y"), lax.axis_index("c")
        peers = [(1 - x, y), (x, 1 - y), (1 - x, 1 - y)]

        def block_for(a, px, py):
            return src_refs[a].at[2 * px + py] if scatter else src_refs[a]

        def copy(a, k, px, py, slot):
            return pltpu.make_async_remote_copy(
                src_ref=block_for(a, px, py), dst_ref=out_refs[a].at[slot],
                send_sem=send_sems.at[a * 3 + k], recv_sem=recv_sems.at[a * 3 + k],
                device_id=(px, py, c), device_id_type=MESH)

        locals_ = [pltpu.make_async_copy(block_for(a, x, y), out_refs[a].at[2 * x + y], local_sems.at[a])
                   for a in range(na)]
        for cp in locals_:
            cp.start()
        sends = [copy(a, k, px, py, 2 * x + y) for a in range(na) for k, (px, py) in enumerate(peers)]
        for cp in sends:
            cp.start()
        for a in range(na):
            for k, (px, py) in enumerate(peers):
                copy(a, k, px, py, 2 * px + py).wait_recv()
        for cp in sends:
            cp.wait_send()
        for cp in locals_:
            cp.wait()

    anyspace = pl.BlockSpec(memory_space=pl.ANY)
    return pl.pallas_call(
        body, name=name,
        out_shape=tuple(jax.ShapeDtypeStruct((4,) + tuple(a.shape[-2:]), a.dtype) for a in srcs),
        in_specs=[anyspace] * na, out_specs=(anyspace,) * na,
        scratch_shapes=[pltpu.SemaphoreType.DMA((3 * na,)), pltpu.SemaphoreType.DMA((3 * na,)),
                        pltpu.SemaphoreType.DMA((na,))],
        compiler_params=_cparams(),
    )(*srcs)


def _gather_shards(srcs, name):
    na = len(srcs)
    halves = [a.shape[0] // 2 for a in srcs]

    def body(*refs):
        src_refs, out_refs = refs[:na], refs[na:2 * na]
        send_sems, recv_sems, local_sems = refs[2 * na:]
        x, y, c = lax.axis_index("x"), lax.axis_index("y"), lax.axis_index("c")
        sibling = (x, y, 1 - c)
        peers = [(1 - x, y), (x, 1 - y), (1 - x, 1 - y)]

        def rows(a, which):
            return pl.ds(pl.multiple_of(which * halves[a], 16), halves[a])

        def copy(a, k, src, slot, which, to):
            return pltpu.make_async_remote_copy(
                src_ref=src, dst_ref=out_refs[a].at[slot, rows(a, which)],
                send_sem=send_sems.at[a * 6 + k], recv_sem=recv_sems.at[a * 6 + k],
                device_id=to, device_id_type=MESH)

        mine = 2 * x + y
        locals_ = [pltpu.make_async_copy(src_refs[a], out_refs[a].at[mine], local_sems.at[a]) for a in range(na)]
        for cp in locals_:
            cp.start()
        first = [copy(a, k, src_refs[a].at[rows(a, c)], mine, c, (px, py, c))
                 for a in range(na) for k, (px, py) in enumerate(peers)]
        for cp in first:
            cp.start()
        passed = []
        for a in range(na):
            for k, (px, py) in enumerate(peers):
                slot = 2 * px + py
                landed = out_refs[a].at[slot, rows(a, c)]
                copy(a, k, landed, slot, c, (px, py, c)).wait_recv()
                fwd = copy(a, 3 + k, landed, slot, c, sibling)
                fwd.start()
                passed.append(fwd)
        for a in range(na):
            for k, (px, py) in enumerate(peers):
                slot = 2 * px + py
                copy(a, 3 + k, out_refs[a].at[slot, rows(a, 1 - c)], slot, 1 - c, sibling).wait_recv()
        for cp in first + passed:
            cp.wait_send()
        for cp in locals_:
            cp.wait()

    anyspace = pl.BlockSpec(memory_space=pl.ANY)
    return pl.pallas_call(
        body, name=name,
        out_shape=tuple(jax.ShapeDtypeStruct((4,) + tuple(a.shape), a.dtype) for a in srcs),
        in_specs=[anyspace] * na, out_specs=(anyspace,) * na,
        scratch_shapes=[pltpu.SemaphoreType.DMA((6 * na,)), pltpu.SemaphoreType.DMA((6 * na,)),
                        pltpu.SemaphoreType.DMA((na,))],
        compiler_params=_cparams(),
    )(*srcs)


def _send_other_halves(srcs, name):
    na = len(srcs)
    halves = [a.shape[1] // 2 for a in srcs]

    def body(*refs):
        src_refs, out_refs = refs[:na], refs[na:2 * na]
        send_sems, recv_sems = refs[2 * na:]
        x, y, c = lax.axis_index("x"), lax.axis_index("y"), lax.axis_index("c")
        copies = [pltpu.make_async_remote_copy(
            src_ref=src_refs[a].at[:, pl.ds(pl.multiple_of((1 - c) * halves[a], 16), halves[a])],
            dst_ref=out_refs[a], send_sem=send_sems.at[a], recv_sem=recv_sems.at[a],
            device_id=(x, y, 1 - c), device_id_type=MESH) for a in range(na)]
        for cp in copies:
            cp.start()
        for cp in copies:
            cp.wait()

    anyspace = pl.BlockSpec(memory_space=pl.ANY)
    return pl.pallas_call(
        body, name=name,
        out_shape=tuple(jax.ShapeDtypeStruct((4, h) + tuple(a.shape[2:]), a.dtype) for a, h in zip(srcs, halves)),
        in_specs=[anyspace] * na, out_specs=(anyspace,) * na,
        scratch_shapes=[pltpu.SemaphoreType.DMA((na,)), pltpu.SemaphoreType.DMA((na,))],
        compiler_params=_cparams(),
    )(*srcs)


def _place_halves(srcs, name):
    na = len(srcs)

    def body(*refs):
        src_refs, out_refs = refs[:na], refs[na:2 * na]
        send_sems, recv_sems, local_sems = refs[2 * na:]
        x, y, c = lax.axis_index("x"), lax.axis_index("y"), lax.axis_index("c")
        locals_, remotes = [], []
        for a in range(na):
            h = srcs[a].shape[0]
            place = out_refs[a].at[pl.ds(pl.multiple_of(c * h, 8), h)]
            locals_.append(pltpu.make_async_copy(src_refs[a], place, local_sems.at[a]))
            remotes.append(pltpu.make_async_remote_copy(
                src_ref=src_refs[a], dst_ref=place, send_sem=send_sems.at[a], recv_sem=recv_sems.at[a],
                device_id=(x, y, 1 - c), device_id_type=MESH))
        for cp in locals_ + remotes:
            cp.start()
        for cp in remotes:
            cp.wait()
        for cp in locals_:
            cp.wait()

    anyspace = pl.BlockSpec(memory_space=pl.ANY)
    return pl.pallas_call(
        body, name=name,
        out_shape=tuple(jax.ShapeDtypeStruct((2 * a.shape[0],) + tuple(a.shape[1:]), a.dtype) for a in srcs),
        in_specs=[anyspace] * na, out_specs=(anyspace,) * na,
        scratch_shapes=[pltpu.SemaphoreType.DMA((na,)), pltpu.SemaphoreType.DMA((na,)),
                        pltpu.SemaphoreType.DMA((na,))],
        compiler_params=_cparams(),
    )(*srcs)


def _mod_cols(c_all, w, b):
    n = w.shape[1]

    def body(c_ref, w_ref, b_ref, o_ref):
        o_ref[...] = _dot(c_ref[...], w_ref[...], HI) + b_ref[...]

    return pl.pallas_call(
        body, name="mod_cols", out_shape=jax.ShapeDtypeStruct((8, n), F32),
        compiler_params=_cparams())(c_all, w, b)


def _grad_w_ada(c_all, dmod_cols):
    n = dmod_cols.shape[1]

    def body(c_ref, d_ref, o_ref):
        o_ref[...] = _dot_tn(c_ref[...], d_ref[...], HI)

    return pl.pallas_call(
        body, name="grad_w_ada", out_shape=jax.ShapeDtypeStruct((D_MODEL, n), F32),
        compiler_params=_cparams())(c_all, dmod_cols)


def _prenorm(x, gs, shift):
    s = x.shape[0]
    tm = min(TM, s)

    def body(x_ref, gs_ref, sh_ref, h_ref):
        xv = x_ref[...]
        r = lax.rsqrt(jnp.mean(xv * xv, axis=-1, keepdims=True) + EPS)
        h_ref[...] = (xv * r * gs_ref[...] + sh_ref[...]).astype(BF16)

    row = pl.BlockSpec((tm, D_MODEL), lambda i: (i, 0))
    vec = pl.BlockSpec((1, D_MODEL), lambda i: (0, 0))
    return pl.pallas_call(
        body, name="prenorm", grid=(s // tm,), in_specs=[row, vec, vec], out_specs=row,
        out_shape=jax.ShapeDtypeStruct((s, D_MODEL), BF16), compiler_params=_cparams(("parallel",)))(x, gs, shift)


def _mm_rows_nt(a, w_t, out_dtype, name):
    s, k = a.shape
    n = w_t.shape[0]
    tm = min(TM, s)

    def body(a_ref, w_ref, o_ref):
        o_ref[...] = _dot_nt(a_ref[...], w_ref[...]).astype(out_dtype)

    return pl.pallas_call(
        body, name=name, grid=(s // tm,),
        in_specs=[pl.BlockSpec((tm, k), lambda i: (i, 0)), _const((n, k))],
        out_specs=pl.BlockSpec((tm, n), lambda i: (i, 0)),
        out_shape=jax.ShapeDtypeStruct((s, n), out_dtype), compiler_params=_cparams(("parallel",)))(a, w_t)


def _grad_w_rows(h, ds):
    s = h.shape[0]
    tk = min(TK_ACC, s)
    nd = len(ds)
    widths = [d.shape[1] for d in ds]

    def body(*refs):
        h_ref, d_refs = refs[0], refs[1:1 + nd]
        out_refs, accs = refs[1 + nd:1 + 2 * nd], refs[1 + 2 * nd:]
        step = pl.program_id(0)

        @pl.when(step == 0)
        def _():
            for acc in accs:
                acc[...] = jnp.zeros_like(acc)

        hv = h_ref[...]
        for d_ref, acc in zip(d_refs, accs):
            acc[...] += _dot_tn(d_ref[...], hv)

        @pl.when(step == s // tk - 1)
        def _():
            for acc, out in zip(accs, out_refs):
                pltpu.sync_copy(acc, out)

    anyspace = pl.BlockSpec(memory_space=pl.ANY)
    return pl.pallas_call(
        body, name="grad_w_in", grid=(s // tk,),
        in_specs=[pl.BlockSpec((tk, D_MODEL), lambda k: (k, 0))]
                 + [pl.BlockSpec((tk, w), lambda k: (k, 0)) for w in widths],
        out_specs=(anyspace,) * nd,
        out_shape=tuple(jax.ShapeDtypeStruct((w, D_MODEL), F32) for w in widths),
        scratch_shapes=[pltpu.VMEM((w, D_MODEL), F32) for w in widths],
        compiler_params=_cparams(("arbitrary",)))(h, *ds)


def _head_pair_selector():
    rows = jnp.arange(LANES)[:, None]
    cols = jnp.arange(4 * LANES)[None, :]
    return ((rows < HEADS) & (cols == (rows // 2) * LANES + rows % 2)).astype(F32)


def _fcum(rest, bf128, selp):
    s = rest.shape[0]
    tb = min(TB_CUM, s)

    def body(fz_ref, bf_ref, sel_ref, fpc_ref, ft_ref, carry_ref):
        @pl.when(pl.program_id(0) == 0)
        def _():
            carry_ref[...] = jnp.zeros_like(carry_ref)

        z = fz_ref[...] + bf_ref[...]
        logf = jnp.minimum(z, 0.0) - jnp.log(1.0 + jnp.exp(-jnp.abs(z)))
        r = lax.broadcasted_iota(jnp.int32, (tb, tb), 0)
        c = lax.broadcasted_iota(jnp.int32, (tb, tb), 1)
        tri = (c <= r).astype(F32)
        f = _dot(tri, logf, HI) + carry_ref[0:1, :]
        carry_ref[0:1, :] = f[tb - 1:tb, :]
        fpc_ref[...] = _dot(f, sel_ref[...], HI)
        ft_ref[...] = jnp.transpose(f)[0:HEADS, :]

    return pl.pallas_call(
        body, name="forget_cumsum", grid=(s // tb,),
        in_specs=[pl.BlockSpec((tb, LANES), lambda i: (i, R_F // LANES)), _const((1, LANES)), _const((LANES, 4 * LANES))],
        out_specs=(pl.BlockSpec((tb, 4 * LANES), lambda i: (i, 0)), pl.BlockSpec((HEADS, tb), lambda i: (0, i))),
        out_shape=(jax.ShapeDtypeStruct((s, 4 * LANES), F32), jax.ShapeDtypeStruct((HEADS, s), F32)),
        scratch_shapes=[pltpu.VMEM((8, LANES), F32)],
        compiler_params=_cparams(("arbitrary",)))(rest, bf128, selp)


def _dfcum(dfk, dfq, rest, bf128, selq):
    s = rest.shape[0]
    tb = min(TB_CUM, s)
    nb = s // tb

    def body(dk_ref, dq_ref, fz_ref, bf_ref, sel_ref, df_ref, dbf_ref, carry_ref):
        @pl.when(pl.program_id(0) == 0)
        def _():
            carry_ref[...] = jnp.zeros_like(carry_ref)
            dbf_ref[...] = jnp.zeros_like(dbf_ref)

        d = _dot(dk_ref[...] + dq_ref[...], sel_ref[...], HI)
        r = lax.broadcasted_iota(jnp.int32, (tb, tb), 0)
        c = lax.broadcasted_iota(jnp.int32, (tb, tb), 1)
        triu = (c >= r).astype(F32)
        dlogf = _dot(triu, d, HI) + carry_ref[0:1, :]
        carry_ref[0:1, :] = dlogf[0:1, :]
        z = fz_ref[...] + bf_ref[...]
        df = dlogf * (1.0 / (1.0 + jnp.exp(z)))
        df_ref[...] = df.astype(BF16)
        dbf_ref[0:1, :] += jnp.sum(df, axis=0, keepdims=True)

    return pl.pallas_call(
        body, name="forget_grad", grid=(nb,),
        in_specs=[pl.BlockSpec((tb, 4 * LANES), lambda i: (nb - 1 - i, 0)),
                  pl.BlockSpec((tb, 4 * LANES), lambda i: (nb - 1 - i, 0)),
                  pl.BlockSpec((tb, LANES), lambda i: (nb - 1 - i, R_F // LANES)),
                  _const((1, LANES)), _const((4 * LANES, LANES))],
        out_specs=(pl.BlockSpec((tb, LANES), lambda i: (nb - 1 - i, 0)), pl.BlockSpec((8, LANES), lambda i: (0, 0))),
        out_shape=(jax.ShapeDtypeStruct((s, LANES), BF16), jax.ShapeDtypeStruct((8, LANES), F32)),
        scratch_shapes=[pltpu.VMEM((8, LANES), F32)],
        compiler_params=_cparams(("arbitrary",)))(dfk, dfq, rest, bf128, selq)


def _scaled(q):
    return (q.astype(F32) * (HEAD_DIM ** -0.5)).astype(BF16)


def _attn_fwd(qkv, frow5, fpc):
    s = qkv.shape[0]
    t = min(T_ATT, s)
    n = s // t
    ch = min(ATT_CHUNK, t)
    wide = 2 * LANES
    pairs = ATT_PAIRS
    width = pairs * LANES
    groups = 4 // pairs

    def body(q_ref, k_ref, v_ref, fr_ref, fc_ref, o_ref, lse_ref, s_scr, p_scr, m_scr, a_scr, fq_scr, acc_scr):
        i = pl.program_id(1)
        lane = lax.broadcasted_iota(jnp.int32, (t, LANES), 1)
        first = lane < HEAD_DIM
        ones_col = ((lane == 0).astype(BF16), (lane == 1).astype(BF16))
        m_scr[...] = jnp.full(m_scr.shape, NEG, F32)
        acc_scr[...] = jnp.zeros_like(acc_scr)
        qm = []
        for pp in range(pairs):
            q = _scaled(q_ref[:, pp * LANES:(pp + 1) * LANES])
            zq = jnp.zeros_like(q)
            qm += [jnp.where(first, q, zq), jnp.where(first, zq, q)]
            fq_scr[2 * pp] = fc_ref[:, pp * LANES:pp * LANES + 1]
            fq_scr[2 * pp + 1] = fc_ref[:, pp * LANES + 1:pp * LANES + 2]

        def step(j, masked):
            r0 = pl.multiple_of(j * t, t)
            vaug = []
            for pp in range(pairs):
                kb = k_ref[pl.ds(r0, t), pp * LANES:(pp + 1) * LANES]
                vb = v_ref[pl.ds(r0, t), pp * LANES:(pp + 1) * LANES]
                zv = jnp.zeros_like(vb)
                vaug += [jnp.concatenate([jnp.where(first, vb, zv), ones_col[0]], axis=1),
                         jnp.concatenate([jnp.where(first, zv, vb), ones_col[1]], axis=1)]
                for hh in range(2):
                    s_scr[2 * pp + hh] = _dot_nt(qm[2 * pp + hh], kb)
            pv = []
            for hd in range(2 * pairs):
                fk = fr_ref[hd // 2, hd % 2, j]
                for c in range(t // ch):
                    rows = pl.ds(c * ch, ch)
                    sc = s_scr[hd, rows, :] - fk
                    if masked:
                        rq = c * ch + lax.broadcasted_iota(jnp.int32, (ch, t), 0)
                        ck = lax.broadcasted_iota(jnp.int32, (ch, t), 1)
                        sc = jnp.where(ck <= rq, sc, NEG)
                    fq = fq_scr[hd, rows, :]
                    m_old = m_scr[hd, rows, :]
                    m_new = jnp.maximum(m_old, fq + jnp.max(sc, axis=1, keepdims=True))
                    p_scr[hd, rows, :] = jnp.exp(sc + (fq - m_new)).astype(BF16)
                    a_scr[hd, rows, :] = jnp.exp(m_old - m_new)
                    m_scr[hd, rows, :] = m_new
                pv.append(_dot(p_scr[hd], vaug[hd]))
            for pp in range(pairs):
                a0, a1 = a_scr[2 * pp], a_scr[2 * pp + 1]
                alpha = jnp.concatenate([jnp.where(first, a0, a1), jnp.where(lane == 0, a0, a1)], axis=1)
                acc_scr[pp] = acc_scr[pp] * alpha + pv[2 * pp] + pv[2 * pp + 1]
            return 0

        lax.fori_loop(0, i, lambda j, _: step(j, False), 0)
        step(i, True)
        for pp in range(pairs):
            l0 = acc_scr[pp, :, LANES:LANES + 1]
            l1 = acc_scr[pp, :, LANES + 1:LANES + 2]
            o_ref[:, pp * LANES:(pp + 1) * LANES] = acc_scr[pp, :, 0:LANES] * jnp.where(first, 1.0 / l0, 1.0 / l1)
            lse0 = m_scr[2 * pp] + jnp.log(l0)
            lse1 = m_scr[2 * pp + 1] + jnp.log(l1)
            lse_ref[:, pp * LANES:(pp + 1) * LANES] = jnp.where(lane == 0, lse0, jnp.where(lane == 1, lse1, 0.0))

    blk = pl.BlockSpec((t, width), lambda g, i: (i, g))
    return pl.pallas_call(
        body, name="attn_fwd", grid=(groups, n),
        in_specs=[blk,
                  pl.BlockSpec((s, width), lambda g, i: (0, groups + g)),
                  pl.BlockSpec((s, width), lambda g, i: (0, 2 * groups + g)),
                  pl.BlockSpec((pairs, 2, n, 1, t), lambda g, i: (g, 0, 0, 0, 0)),
                  blk],
        out_specs=(blk, blk),
        out_shape=(jax.ShapeDtypeStruct((s, FOX_W), F32), jax.ShapeDtypeStruct((s, 4 * LANES), F32)),
        scratch_shapes=[pltpu.VMEM((2 * pairs, t, t), F32), pltpu.VMEM((2 * pairs, t, t), BF16),
                        pltpu.VMEM((2 * pairs, t, 1), F32), pltpu.VMEM((2 * pairs, t, 1), F32),
                        pltpu.VMEM((2 * pairs, t, 1), F32), pltpu.VMEM((pairs, t, wide), F32)],
        compiler_params=_cparams(("parallel", "arbitrary")))(qkv, qkv, qkv, frow5, fpc)


def _attn_bwd(qkv, do, lse5, dlt5, frow5, fpc):
    s = qkv.shape[0]
    t = min(T_ATT, s)
    n = s // t
    wide = 2 * LANES

    ch = min(ATT_CHUNK, t)

    def body(q_ref, do_ref, k_ref, v_ref, lse_ref, dl_ref, fr_ref, fc_ref,
             dq_ref, dk_ref, dv_ref, dfk_ref, dfq_ref, dq_acc, st_scr, dp_scr, pt_scr, ds_scr, dk_acc, dv_acc, fk_scr):
        j = pl.program_id(1)

        @pl.when(j == 0)
        def _():
            dq_acc[...] = jnp.zeros_like(dq_acc)

        dk_acc[...] = jnp.zeros_like(dk_acc)
        dv_acc[...] = jnp.zeros_like(dv_acc)
        lane = lax.broadcasted_iota(jnp.int32, (t, LANES), 1)
        first = lane < HEAD_DIM
        ones_col = ((lane == 0).astype(BF16), (lane == 1).astype(BF16))
        kb = k_ref[...]
        vb = v_ref[...]
        zk = jnp.zeros_like(kb)
        kaug = (jnp.concatenate([jnp.where(first, kb, zk), ones_col[0]], axis=1),
                jnp.concatenate([jnp.where(first, zk, kb), ones_col[1]], axis=1))
        fk_scr[0] = fc_ref[:, 0:1]
        fk_scr[1] = fc_ref[:, 1:2]

        def step(i, masked):
            r0 = pl.multiple_of(i * t, t)
            qb = _scaled(q_ref[pl.ds(r0, t), :])
            dob = do_ref[pl.ds(r0, t), :]
            zq = jnp.zeros_like(qb)
            qm = (jnp.where(first, qb, zq), jnp.where(first, zq, qb))
            dom = (jnp.where(first, dob, zq), jnp.where(first, zq, dob))
            dq_add = jnp.zeros((t, wide), F32)
            for hh in range(2):
                st_scr[hh] = _dot_nt(kb, qm[hh])
                dp_scr[hh] = _dot_nt(vb, dom[hh])
                bias = fr_ref[0, hh, i] - lse_ref[0, hh, i]
                dl = dl_ref[0, hh, i]
                for c in range(t // ch):
                    rows = pl.ds(c * ch, ch)
                    st = st_scr[hh, rows, :] + (bias - fk_scr[hh, rows, :])
                    if masked:
                        rk = c * ch + lax.broadcasted_iota(jnp.int32, (ch, t), 0)
                        cq = lax.broadcasted_iota(jnp.int32, (ch, t), 1)
                        st = jnp.where(rk <= cq, st, NEG)
                    pt = jnp.exp(st)
                    pt_scr[hh, rows, :] = pt.astype(BF16)
                    ds_scr[hh, rows, :] = (pt * (dp_scr[hh, rows, :] - dl)).astype(BF16)
                dsb = ds_scr[hh]
                dv_acc[...] += _dot(pt_scr[hh], dom[hh])
                dk_acc[...] += _dot(dsb, jnp.concatenate([qm[hh], ones_col[hh]], axis=1))
                dq_add = dq_add + _dot_tn(dsb, kaug[hh])
            dq_acc[pl.ds(r0, t), :] += dq_add
            return 0

        step(j, True)
        lax.fori_loop(j + 1, n, lambda i, _: step(i, False), 0)
        dk_ref[...] = dk_acc[:, 0:LANES].astype(BF16)
        dv_ref[...] = dv_acc[...].astype(BF16)
        dfk_ref[...] = -dk_acc[:, LANES:wide]

        @pl.when(j == n - 1)
        def _():
            dq_ref[...] = (dq_acc[:, 0:LANES] * (HEAD_DIM ** -0.5)).astype(BF16)
            dfq_ref[...] = dq_acc[:, LANES:wide]

    stat = pl.BlockSpec((1, 2, n, 1, t), lambda h, j: (h, 0, 0, 0, 0))
    blk = pl.BlockSpec((t, LANES), lambda h, j: (j, h))
    full = pl.BlockSpec((s, LANES), lambda h, j: (0, h))
    return pl.pallas_call(
        body, name="attn_bwd", grid=(4, n),
        in_specs=[full, full,
                  pl.BlockSpec((t, LANES), lambda h, j: (j, 4 + h)),
                  pl.BlockSpec((t, LANES), lambda h, j: (j, 8 + h)),
                  stat, stat, stat, blk],
        out_specs=(full, blk, blk, blk, full),
        out_shape=(jax.ShapeDtypeStruct((s, FOX_W), BF16), jax.ShapeDtypeStruct((s, FOX_W), BF16),
                   jax.ShapeDtypeStruct((s, FOX_W), BF16), jax.ShapeDtypeStruct((s, 4 * LANES), F32),
                   jax.ShapeDtypeStruct((s, 4 * LANES), F32)),
        scratch_shapes=[pltpu.VMEM((s, wide), F32), pltpu.VMEM((2, t, t), F32), pltpu.VMEM((2, t, t), F32),
                        pltpu.VMEM((2, t, t), BF16), pltpu.VMEM((2, t, t), BF16), pltpu.VMEM((t, wide), F32),
                        pltpu.VMEM((t, LANES), F32), pltpu.VMEM((2, t, 1), F32)],
        compiler_params=_cparams(("parallel", "arbitrary")))(qkv, do, qkv, qkv, lse5, dlt5, frow5, fpc)


def _ssm_block_params(a_re, a_im, log_dt, b_re, b_im):
    dt = jnp.exp(log_dt)[:, None]
    mag = jnp.exp(a_re * dt)
    ar = mag * jnp.cos(a_im * dt)
    ai = mag * jnp.sin(a_im * dt)
    den = a_re * a_re + a_im * a_im
    nr = ar - 1.0
    cr = (nr * a_re + ai * a_im) / den
    ci = (ai * a_re - nr * a_im) / den
    bbr = cr[:, :, None] * b_re - ci[:, :, None] * b_im
    bbi = cr[:, :, None] * b_im + ci[:, :, None] * b_re
    return ar, ai, bbr, bbi


def _block_diag(blocks):
    g, r, c = blocks.shape
    eye = jnp.eye(g, dtype=blocks.dtype)
    return (blocks[:, :, None, :] * eye[:, None, :, None]).reshape(g * r, g * c)


def _scan_consts(a_re, a_im, log_dt, reverse):
    dt = jnp.exp(log_dt)[:, None]
    lr = (a_re * dt).reshape(1, NSTATE)
    li = (a_im * dt).reshape(1, NSTATE)
    if reverse:
        li = -li
    rows = jnp.arange(8, dtype=F32)[:, None]

    def power(k):
        mag = jnp.exp(k * lr)
        return mag * jnp.cos(k * li), mag * jnp.sin(k * li)

    tiles = []
    for k in (1, 2, 4):
        keep = (rows < 8 - k) if reverse else (rows >= k)
        pr, pi_ = power(float(k))
        tiles += [jnp.where(keep, pr, 0.0), jnp.where(keep, pi_, 0.0)]
    expo = (8.0 - rows) if reverse else (rows + 1.0)
    tiles += list(power(expo))
    return jnp.stack([jnp.broadcast_to(tl, (8, NSTATE)) for tl in tiles])


_SCAN_W = 512
_HALF_W = S5_W // 2
_HALF_S = NSTATE // 2


def _half_expand(v, w_ref, out_ref):
    for half in range(2):
        vh = v[:, half * _HALF_W:(half + 1) * _HALF_W]
        for part in range(2):
            c0 = part * NSTATE + half * _HALF_S
            out_ref[:, c0:c0 + _HALF_S] = _dot(vh, w_ref[half * _HALF_W:(half + 1) * _HALF_W, c0:c0 + _HALF_S])


def _half_contract(x_ref, w_ref, half):
    out = None
    for part in range(2):
        r0 = part * NSTATE + half * _HALF_S
        term = _dot(x_ref[:, r0:r0 + _HALF_S].astype(BF16),
                    w_ref[r0:r0 + _HALF_S, half * _HALF_W:(half + 1) * _HALF_W])
        out = term if out is None else out + term
    return out


def _half_outer(v, x_ref, acc_ref):
    for half in range(2):
        vh = v[:, half * _HALF_W:(half + 1) * _HALF_W]
        for part in range(2):
            c0 = part * NSTATE + half * _HALF_S
            acc_ref[:, c0:c0 + _HALF_S] += _dot_tn(vh, x_ref[:, c0:c0 + _HALF_S].astype(BF16))


def _ssm_fwd(rest, bd, cd, consts):
    s = rest.shape[0]
    tb = min(TB_SSM, s)
    ns2 = 2 * NSTATE

    def body(u_ref, bd_ref, cd_ref, cf_ref, y_ref, x_ref, cb_ref):
        @pl.when(pl.program_id(0) == 0)
        def _():
            cb_ref[...] = jnp.zeros_like(cb_ref)

        _half_expand(u_ref[...].astype(BF16), bd_ref, x_ref)

        def tile(ti, _):
            r0 = pl.multiple_of(ti * 8, 8)
            for cc in range(NSTATE // _SCAN_W):
                cr = pl.ds(cc * _SCAN_W, _SCAN_W)
                ci = pl.ds(NSTATE + cc * _SCAN_W, _SCAN_W)
                re = x_ref[pl.ds(r0, 8), cr]
                im = x_ref[pl.ds(r0, 8), ci]
                for n_, k in enumerate((1, 2, 4)):
                    ar = cf_ref[2 * n_, :, cr]
                    ai = cf_ref[2 * n_ + 1, :, cr]
                    sr = pltpu.roll(re, k, 0)
                    si = pltpu.roll(im, k, 0)
                    re, im = re + ar * sr - ai * si, im + ar * si + ai * sr
                pr = cf_ref[6, :, cr]
                pi_ = cf_ref[7, :, cr]
                cbr = cb_ref[:, cr]
                cbi = cb_ref[:, ci]
                re, im = re + pr * cbr - pi_ * cbi, im + pr * cbi + pi_ * cbr
                x_ref[pl.ds(r0, 8), cr] = re
                x_ref[pl.ds(r0, 8), ci] = im
                cb_ref[:, cr] = jnp.broadcast_to(re[7:8, :], (8, _SCAN_W))
                cb_ref[:, ci] = jnp.broadcast_to(im[7:8, :], (8, _SCAN_W))
            return 0

        lax.fori_loop(0, tb // 8, tile, 0)
        for half in range(2):
            y_ref[:, half * _HALF_W:(half + 1) * _HALF_W] = _half_contract(x_ref, cd_ref, half)

    return pl.pallas_call(
        body, name="ssm_fwd", grid=(s // tb,),
        in_specs=[pl.BlockSpec((tb, S5_W), lambda i: (i, R_U // S5_W)), _const((S5_W, ns2)), _const((ns2, S5_W)),
                  _const((8, 8, NSTATE))],
        out_specs=(pl.BlockSpec((tb, S5_W), lambda i: (i, 0)), pl.BlockSpec((tb, ns2), lambda i: (i, 0))),
        out_shape=(jax.ShapeDtypeStruct((s, S5_W), F32), jax.ShapeDtypeStruct((s, ns2), F32)),
        scratch_shapes=[pltpu.VMEM((8, ns2), F32)],
        compiler_params=_cparams(("arbitrary",)))(rest, bd, cd, consts)


def _ssm_bwd(dys, xs, rest, bd_t, cd_t, consts, dskip):
    s = dys.shape[0]
    tb = min(TB_SSM, s)
    nb = s // tb
    ns2 = 2 * NSTATE
    nt = tb // 8

    def body(dy_ref, x_ref, u_ref, bdt_ref, cdt_ref, cf_ref, dsk_ref, du_ref, gb_ref, gc_ref, da_ref,
             g_ref, cb_ref, acc_b, acc_c):
        step = pl.program_id(0)

        @pl.when(step == 0)
        def _():
            cb_ref[...] = jnp.zeros_like(cb_ref)
            acc_b[...] = jnp.zeros_like(acc_b)
            acc_c[...] = jnp.zeros_like(acc_c)
            da_ref[...] = jnp.zeros_like(da_ref)

        dy = dy_ref[...]
        dyb = dy.astype(BF16)
        _half_expand(dyb, cdt_ref, g_ref)
        last_row = lax.broadcasted_iota(jnp.int32, (8, _SCAN_W), 0) == 7

        def tile(tt, _):
            r0 = pl.multiple_of((nt - 1 - tt) * 8, 8)
            for cc in range(NSTATE // _SCAN_W):
                cr = pl.ds(cc * _SCAN_W, _SCAN_W)
                ci = pl.ds(NSTATE + cc * _SCAN_W, _SCAN_W)
                re = g_ref[pl.ds(r0, 8), cr]
                im = g_ref[pl.ds(r0, 8), ci]
                for n_, k in enumerate((1, 2, 4)):
                    ar = cf_ref[2 * n_, :, cr]
                    ai = cf_ref[2 * n_ + 1, :, cr]
                    sr = pltpu.roll(re, 8 - k, 0)
                    si = pltpu.roll(im, 8 - k, 0)
                    re, im = re + ar * sr - ai * si, im + ar * si + ai * sr
                pr = cf_ref[6, :, cr]
                pi_ = cf_ref[7, :, cr]
                cbr = cb_ref[:, cr]
                cbi = cb_ref[:, ci]
                re, im = re + pr * cbr - pi_ * cbi, im + pr * cbi + pi_ * cbr
                g_ref[pl.ds(r0, 8), cr] = re
                g_ref[pl.ds(r0, 8), ci] = im
                gnr = jnp.where(last_row, cbr, pltpu.roll(re, 7, 0))
                gni = jnp.where(last_row, cbi, pltpu.roll(im, 7, 0))
                xr = x_ref[pl.ds(r0, 8), cr]
                xi = x_ref[pl.ds(r0, 8), ci]
                da_ref[:, cr] += gnr * xr + gni * xi
                da_ref[:, ci] += gni * xr - gnr * xi
                cb_ref[:, cr] = jnp.broadcast_to(re[0:1, :], (8, _SCAN_W))
                cb_ref[:, ci] = jnp.broadcast_to(im[0:1, :], (8, _SCAN_W))
            return 0

        lax.fori_loop(0, nt, tile, 0)
        for half in range(2):
            cols = slice(half * _HALF_W, (half + 1) * _HALF_W)
            du_ref[:, cols] = (_half_contract(g_ref, bdt_ref, half) + dy[:, cols] * dsk_ref[:, cols]).astype(BF16)
        _half_outer(u_ref[...].astype(BF16), g_ref, acc_b)
        _half_outer(dyb, x_ref, acc_c)

        @pl.when(step == nb - 1)
        def _():
            for g in range(GROUPS):
                src = slice((g % (GROUPS // 2)) * GCH, (g % (GROUPS // 2) + 1) * GCH)
                dst = slice(g * GCH, (g + 1) * GCH)
                for part in range(2):
                    cols = slice(part * NSTATE + g * STATE, part * NSTATE + (g + 1) * STATE)
                    gb_ref[dst, part * STATE:(part + 1) * STATE] = acc_b[src, cols]
                    gc_ref[dst, part * STATE:(part + 1) * STATE] = acc_c[src, cols]

    rev = lambda i: (nb - 1 - i, 0)
    small = pl.BlockSpec((S5_W, 2 * STATE), lambda i: (0, 0))
    return pl.pallas_call(
        body, name="ssm_bwd", grid=(nb,),
        in_specs=[pl.BlockSpec((tb, S5_W), rev), pl.BlockSpec((tb, ns2), rev),
                  pl.BlockSpec((tb, S5_W), lambda i: (nb - 1 - i, R_U // S5_W)),
                  _const((ns2, S5_W)), _const((S5_W, ns2)), _const((8, 8, NSTATE)), _const((1, S5_W))],
        out_specs=(pl.BlockSpec((tb, S5_W), rev), small, small, pl.BlockSpec((8, ns2), lambda i: (0, 0))),
        out_shape=(jax.ShapeDtypeStruct((s, S5_W), BF16), jax.ShapeDtypeStruct((S5_W, 2 * STATE), F32),
                   jax.ShapeDtypeStruct((S5_W, 2 * STATE), F32), jax.ShapeDtypeStruct((8, ns2), F32)),
        scratch_shapes=[pltpu.VMEM((tb, ns2), F32), pltpu.VMEM((8, ns2), F32),
                        pltpu.VMEM((_HALF_W, ns2), F32), pltpu.VMEM((_HALF_W, ns2), F32)],
        compiler_params=_cparams(("arbitrary",)))(dys, xs, rest, bd_t, cd_t, consts, dskip)


_GELU_C = math.sqrt(2.0 / math.pi)
_GELU_A = 0.044715


def _mid(o, rest, ys0, x, tgt, w, vec, hsel):
    s = o.shape[0]
    tm = min(TM, s)
    nsteps = s // tm
    half = FOX_W

    def body(o_ref, ga_ref, gb_ref, za_ref, u_ref, zb_ref, ys0_ref, x_ref, t_ref,
             wglu_ref, wua_ref, wub_ref, wout_ref, vec_ref, hsel_ref,
             dx2_ref, dga_ref, dgb_ref, do_ref, dza_ref, dzb_ref, dys_ref, dlt_ref,
             gout_hbm, gua_hbm, gub_hbm, gglu_hbm, vout_ref,
             a_out, a_ua, a_ub, a_glu):
        step = pl.program_id(0)

        @pl.when(step == 0)
        def _():
            a_out[...] = jnp.zeros_like(a_out)
            a_ua[...] = jnp.zeros_like(a_ua)
            a_ub[...] = jnp.zeros_like(a_ub)
            a_glu[...] = jnp.zeros_like(a_glu)
            vout_ref[...] = jnp.zeros_like(vout_ref)

        gate = vec_ref[0:1, :]
        gfin = vec_ref[1:2, :]
        dsk = vec_ref[2:3, 0:half]
        bglu = vec_ref[2:3, half:2 * half]

        o_v = o_ref[...]
        za = za_ref[...]
        sza = _sigmoid(za)
        silu_za = za * sza
        ya_b = (o_v * silu_za).astype(BF16)
        u_v = u_ref[...]
        ys = ys0_ref[...] + dsk * u_v
        inner = _GELU_C * (ys + _GELU_A * ys * ys * ys)
        th = jnp.tanh(inner)
        yg = 0.5 * ys * (1.0 + th)
        yg_b = yg.astype(BF16)
        st = _sigmoid(_dot(yg_b, wglu_ref[...]) + bglu)
        yb1 = yg * st
        zb = zb_ref[...]
        szb = _sigmoid(zb)
        silu_zb = zb * szb
        yb_b = (yb1 * silu_zb).astype(BF16)
        ua = _dot(ya_b, wua_ref[...])
        ub = _dot(yb_b, wub_ref[...])
        sga = _sigmoid(ga_ref[...])
        sgb = _sigmoid(gb_ref[...])
        merged_b = (sga * ua + sgb * ub).astype(BF16)
        mo = _dot(merged_b, wout_ref[...])
        x2 = x_ref[...] + gate * mo
        r2 = lax.rsqrt(jnp.mean(x2 * x2, axis=-1, keepdims=True) + EPS)
        x2n = x2 * r2
        diff = x2n * gfin - t_ref[...]
        loss = 0.5 * jnp.sum(jnp.mean(diff * diff, axis=-1, keepdims=True), axis=0, keepdims=True)
        dy = diff * (1.0 / D_MODEL)
        dx2n = dy * gfin
        dx2 = r2 * (dx2n - x2n * jnp.mean(dx2n * x2n, axis=-1, keepdims=True))
        dx2_ref[...] = dx2
        vout_ref[0:1, :] += jnp.sum(dy * x2n, axis=0, keepdims=True)
        vout_ref[1:2, :] += jnp.sum(dx2 * mo, axis=0, keepdims=True)
        vout_ref[3:4, :] += jnp.broadcast_to(loss, (1, D_MODEL))
        dmo_b = (dx2 * gate).astype(BF16)
        dmerged = _dot_nt(dmo_b, wout_ref[...])
        a_out[...] += _dot_tn(merged_b, dmo_b)
        dua_b = (dmerged * sga).astype(BF16)
        dub_b = (dmerged * sgb).astype(BF16)
        dga_ref[...] = (dmerged * ua * sga * (1.0 - sga)).astype(BF16)
        dgb_ref[...] = (dmerged * ub * sgb * (1.0 - sgb)).astype(BF16)
        dya = _dot_nt(dua_b, wua_ref[...])
        dyb = _dot_nt(dub_b, wub_ref[...])
        a_ua[...] += _dot_tn(ya_b, dua_b)
        a_ub[...] += _dot_tn(yb_b, dub_b)
        do_b = (dya * silu_za).astype(BF16)
        do_ref[...] = do_b
        dza_ref[...] = (dya * o_v * (sza * (1.0 + za * (1.0 - sza)))).astype(BF16)
        dlt_ref[...] = lax.dot_general(hsel_ref[...], do_b.astype(F32) * o_v, (((1,), (1,)), ((), ())),
                                       preferred_element_type=F32, precision=HI)
        dyb1 = dyb * silu_zb
        dzb_ref[...] = (dyb * yb1 * (szb * (1.0 + zb * (1.0 - szb)))).astype(BF16)
        dt = dyb1 * yg * st * (1.0 - st)
        dt_b = dt.astype(BF16)
        dyg = dyb1 * st + _dot_nt(dt_b, wglu_ref[...])
        a_glu[...] += _dot_tn(yg_b, dt_b)
        dgelu = 0.5 * (1.0 + th) + 0.5 * ys * (1.0 - th * th) * _GELU_C * (1.0 + 3.0 * _GELU_A * ys * ys)
        dys = dyg * dgelu
        dys_ref[...] = dys
        vout_ref[2:3, 0:half] += jnp.sum(dys * u_v, axis=0, keepdims=True)
        vout_ref[2:3, half:2 * half] += jnp.sum(dt, axis=0, keepdims=True)

        @pl.when(step == nsteps - 1)
        def _():
            pltpu.sync_copy(a_out, gout_hbm)
            pltpu.sync_copy(a_ua, gua_hbm)
            pltpu.sync_copy(a_ub, gub_hbm)
            pltpu.sync_copy(a_glu, gglu_hbm)

    def rows(width, col=0):
        return pl.BlockSpec((tm, width), lambda i, col=col: (i, col))

    anyspace = pl.BlockSpec(memory_space=pl.ANY)
    wshapes = [(S5_W, S5_W), (FOX_W, D_MODEL), (S5_W, D_MODEL), (D_MODEL, D_MODEL)]
    return pl.pallas_call(
        body, name="mid", grid=(nsteps,),
        in_specs=[rows(FOX_W), rows(D_MODEL, R_GA // D_MODEL), rows(D_MODEL, R_GB // D_MODEL),
                  rows(FOX_W, R_ZA // FOX_W), rows(S5_W, R_U // S5_W), rows(S5_W, R_ZB // S5_W),
                  rows(S5_W), rows(D_MODEL), rows(D_MODEL)]
                 + [_const(sh) for sh in wshapes]
                 + [_const((8, D_MODEL)), _const((HEADS, FOX_W))],
        out_specs=(rows(D_MODEL), rows(D_MODEL), rows(D_MODEL), rows(FOX_W), rows(FOX_W), rows(S5_W), rows(S5_W),
                   pl.BlockSpec((HEADS, tm), lambda i: (0, i)),
                   anyspace, anyspace, anyspace, anyspace, pl.BlockSpec((8, D_MODEL), lambda i: (0, 0))),
        out_shape=(jax.ShapeDtypeStruct((s, D_MODEL), F32), jax.ShapeDtypeStruct((s, D_MODEL), BF16),
                   jax.ShapeDtypeStruct((s, D_MODEL), BF16), jax.ShapeDtypeStruct((s, FOX_W), BF16),
                   jax.ShapeDtypeStruct((s, FOX_W), BF16), jax.ShapeDtypeStruct((s, S5_W), BF16),
                   jax.ShapeDtypeStruct((s, S5_W), F32), jax.ShapeDtypeStruct((HEADS, s), F32),
                   jax.ShapeDtypeStruct((D_MODEL, D_MODEL), F32), jax.ShapeDtypeStruct((FOX_W, D_MODEL), F32),
                   jax.ShapeDtypeStruct((S5_W, D_MODEL), F32), jax.ShapeDtypeStruct((S5_W, S5_W), F32),
                   jax.ShapeDtypeStruct((8, D_MODEL), F32)),
        scratch_shapes=[pltpu.VMEM((D_MODEL, D_MODEL), F32), pltpu.VMEM((FOX_W, D_MODEL), F32),
                        pltpu.VMEM((S5_W, D_MODEL), F32), pltpu.VMEM((S5_W, S5_W), F32)],
        compiler_params=_cparams(("arbitrary",)),
    )(o, rest, rest, rest, rest, rest, ys0, x, tgt, *w, vec, hsel)


def _dh(dq, dk, dv, dga, dgb, dza, du, dzb, df, wqkv_t, wrest_t, x, dx2, gs):
    s = x.shape[0]
    tm = min(TM, s)

    def body(dq_ref, dk_ref, dv_ref, dga_ref, dgb_ref, dza_ref, du_ref, dzb_ref, df_ref, wq_ref, wr_ref,
             x_ref, dx2_ref, gs_ref, gx_ref, vout_ref):
        @pl.when(pl.program_id(0) == 0)
        def _():
            vout_ref[...] = jnp.zeros_like(vout_ref)

        dh = _dot(dq_ref[...], wq_ref[0:512, :])
        dh += _dot(dk_ref[...], wq_ref[512:1024, :])
        dh += _dot(dv_ref[...], wq_ref[1024:1536, :])
        dh += _dot(dga_ref[...], wr_ref[R_GA:R_GB, :])
        dh += _dot(dgb_ref[...], wr_ref[R_GB:R_ZA, :])
        dh += _dot(dza_ref[...], wr_ref[R_ZA:R_U, :])
        dh += _dot(du_ref[...], wr_ref[R_U:R_ZB, :])
        dh += _dot(dzb_ref[...], wr_ref[R_ZB:R_F, :])
        dh += _dot(df_ref[...], wr_ref[R_F:REST_W, :])
        xv = x_ref[...]
        r = lax.rsqrt(jnp.mean(xv * xv, axis=-1, keepdims=True) + EPS)
        xn = xv * r
        dxn = dh * gs_ref[...]
        gx_ref[...] = dx2_ref[...] + r * (dxn - xn * jnp.mean(dxn * xn, axis=-1, keepdims=True))
        vout_ref[0:1, :] += jnp.sum(dh * xn, axis=0, keepdims=True)
        vout_ref[1:2, :] += jnp.sum(dh, axis=0, keepdims=True)

    def rows(width):
        return pl.BlockSpec((tm, width), lambda i: (i, 0))

    return pl.pallas_call(
        body, name="dh", grid=(s // tm,),
        in_specs=[rows(512), rows(512), rows(512), rows(1024), rows(1024), rows(512), rows(512), rows(512), rows(128),
                  _const((1536, D_MODEL)), _const((REST_W, D_MODEL)), rows(D_MODEL), rows(D_MODEL), _const((1, D_MODEL))],
        out_specs=(rows(D_MODEL), pl.BlockSpec((8, D_MODEL), lambda i: (0, 0))),
        out_shape=(jax.ShapeDtypeStruct((s, D_MODEL), F32), jax.ShapeDtypeStruct((8, D_MODEL), F32)),
        compiler_params=_cparams(("arbitrary",)),
    )(dq, dk, dv, dga, dgb, dza, du, dzb, df, wqkv_t, wrest_t, x, dx2, gs)


def _row_block(rows, mult=8, cap=512):
    if rows <= mult:
        return rows
    padded = -(-rows // mult) * mult
    for cand in range(min(cap, padded) // mult * mult, 0, -mult):
        if padded % cand == 0:
            return cand
    return padded


def _sum4(parts, name):
    rows, cols = parts.shape[1:]
    br = _row_block(rows, 16)

    def body(p_ref, o_ref):
        acc = p_ref[0].astype(F32)
        for k in range(1, 4):
            acc = acc + p_ref[k].astype(F32)
        o_ref[...] = acc

    return pl.pallas_call(
        body, name=name, grid=(pl.cdiv(rows, br),),
        in_specs=[pl.BlockSpec((4, br, cols), lambda i: (0, i, 0))],
        out_specs=pl.BlockSpec((br, cols), lambda i: (i, 0)),
        out_shape=jax.ShapeDtypeStruct((rows, cols), F32), compiler_params=_cparams(("parallel",)))(parts)


def _pair_add(a, b, name):
    shape = a.shape
    a, b = a.reshape(-1, shape[-1]), b.reshape(-1, shape[-1])
    rows, cols = a.shape
    br = _row_block(rows, 16, 1024)

    def body(a_ref, b_ref, o_ref):
        o_ref[...] = (a_ref[...].astype(F32) + b_ref[...].astype(F32)).astype(BF16)

    spec = pl.BlockSpec((br, cols), lambda i: (i, 0))
    return pl.pallas_call(
        body, name=name, grid=(pl.cdiv(rows, br),), in_specs=[spec, spec], out_specs=spec,
        out_shape=jax.ShapeDtypeStruct((rows, cols), BF16), compiler_params=_cparams(("parallel",)))(a, b).reshape(shape)


def _adamw(w, g, m, v, name):
    rows, cols = w.shape
    br = _row_block(rows)

    def body(w_ref, g_ref, m_ref, v_ref, d_ref, nm_ref, nv_ref):
        gv = g_ref[...]
        nm = ADAM_B1 * m_ref[...] + (1.0 - ADAM_B1) * gv
        nv = ADAM_B2 * v_ref[...] + (1.0 - ADAM_B2) * (gv * gv)
        m_hat = nm / (1.0 - ADAM_B1 ** ADAM_STEP)
        v_hat = nv / (1.0 - ADAM_B2 ** ADAM_STEP)
        d_ref[...] = -ADAM_LR * (m_hat / (jnp.sqrt(v_hat) + ADAM_EPS) + ADAM_WD * w_ref[...])
        nm_ref[...] = nm
        nv_ref[...] = nv

    spec = pl.BlockSpec((br, cols), lambda i: (i, 0))
    shape = jax.ShapeDtypeStruct((rows, cols), F32)
    return pl.pallas_call(
        body, name=name, grid=(pl.cdiv(rows, br),), in_specs=[spec] * 4, out_specs=(spec,) * 3,
        out_shape=(shape,) * 3, compiler_params=_cparams(("parallel",)))(w, g, m, v)


def _pack(parts):
    flat = []
    for p in parts:
        v = p.reshape(-1).astype(F32)
        pad = (-v.shape[0]) % LANES
        if pad:
            v = jnp.concatenate([v, jnp.zeros((pad,), F32)])
        flat.append(v)
    v = jnp.concatenate(flat)
    rows = v.shape[0] // LANES
    pad_rows = (-rows) % 8
    if pad_rows:
        v = jnp.concatenate([v, jnp.zeros((pad_rows * LANES,), F32)])
    return v.reshape(-1, LANES)


def _unpack(packed, shapes):
    lead = packed.shape[:-2]
    flat = packed.reshape(lead + (-1,))
    out, off = [], 0
    for sh in shapes:
        size = math.prod(sh)
        out.append(flat[..., off:off + size].reshape(lead + tuple(sh)))
        off += size + (-size) % LANES
    return out


def kernel(x, c, w_ada, b_ada, g_norm, w_in, b_f, a_re, a_im, log_dt, b_re, b_im, c_re, c_im, d_skip, w_glu, b_glu, w_up_a, w_up_b, w_out, g_final, loss_target, m_w_ada, m_b_ada, m_g_norm, m_w_in, m_b_f, m_a_re, m_a_im, m_log_dt, m_b_re, m_b_im, m_c_re, m_c_im, m_d_skip, m_w_glu, m_b_glu, m_w_up_a, m_w_up_b, m_w_out, m_g_final, v_w_ada, v_b_ada, v_g_norm, v_w_in, v_b_f, v_a_re, v_a_im, v_log_dt, v_b_re, v_b_im, v_c_re, v_c_im, v_d_skip, v_w_glu, v_b_glu, v_w_up_a, v_w_up_b, v_w_out, v_g_final):
    xi, yi, ci = lax.axis_index("x"), lax.axis_index("y"), lax.axis_index("c")
    chip = 2 * xi + yi
    me = 4 * xi + 2 * yi + ci
    s = x.shape[1]
    x2d = x[0]
    tgt = loss_target[0]
    n_att = s // min(T_ATT, s)
    t_att = min(T_ATT, s)

    c_all, _ = _allgather8(c.reshape(8, LANES), "gather_c")
    c_all = c_all.reshape(8, D_MODEL)
    ncol = w_ada.shape[2]
    b_cols = lax.dynamic_slice_in_dim(b_ada, chip * ncol, ncol, axis=1)
    mod_cols = _mod_cols(c_all, w_ada[0], b_cols)
    mod_all, _ = _allgather8(mod_cols.reshape(-1, LANES), "gather_mod")
    mod_all = mod_all.reshape(4, 2, 8, ncol)[:, 0]
    mod_me = lax.dynamic_index_in_dim(mod_all, me, axis=1, keepdims=False).reshape(1, 3 * D_MODEL)
    shift, scale, gate = mod_me[:, :D_MODEL], mod_me[:, D_MODEL:2 * D_MODEL], mod_me[:, 2 * D_MODEL:]
    gs = g_norm * (1.0 + scale)

    nshard = w_in.shape[2]
    w_in_t, m_in_t, v_in_t = (jnp.swapaxes(a[0], 0, 1) for a in (w_in, m_w_in, v_w_in))
    wt_pack = jnp.pad(w_in_t.astype(BF16), ((0, SHARD_ROWS - nshard), (0, 0)))
    misc_shapes = [w_glu.shape[1:], w_up_a.shape[1:], w_up_b.shape[1:], w_out.shape[1:]]
    misc_pack = jnp.concatenate([w.reshape(-1) for w in (w_glu, w_up_a, w_up_b, w_out)]).astype(BF16).reshape(-1, LANES)
    wt_all, misc_all = _gather_shards([wt_pack, misc_pack], "gather_weights")
    p_glu, p_ua, p_ub, p_out = _unpack(misc_all, misc_shapes)

    def w_rows(lo, hi):
        out = []
        for j in range(4):
            a, b = max(lo, j * nshard), min(hi, (j + 1) * nshard)
            if a < b:
                out.append(wt_all[j, a - j * nshard:b - j * nshard])
        return out

    wqkv_t = jnp.concatenate(w_rows(O_Q, O_F), axis=0)
    wrest_t = jnp.concatenate(w_rows(O_GA, O_GB) + w_rows(O_GB, O_END) + w_rows(O_ZA, O_U) + w_rows(O_U, O_ZB)
                              + w_rows(O_ZB, O_GA) + w_rows(O_F, O_ZA)
                              + [jnp.zeros((REST_W - R_F - HEADS, D_MODEL), BF16)], axis=0)
    wmid = (p_glu.reshape(S5_W, S5_W), jnp.concatenate([p_ua[j] for j in range(4)], axis=1),
            jnp.concatenate([p_ub[j] for j in range(4)], axis=1), p_out.reshape(D_MODEL, D_MODEL))

    h = _prenorm(x2d, gs, shift)
    qkv = _mm_rows_nt(h, wqkv_t, BF16, "proj_qkv")
    rest = _mm_rows_nt(h, wrest_t, F32, "proj_rest")
    bf128 = jnp.pad(b_f, ((0, 0), (0, LANES - HEADS)))
    selp = _head_pair_selector()
    fpc, f_t = _fcum(rest, bf128, selp)
    frow5 = f_t.reshape(4, 2, n_att, 1, t_att)
    o, lse_pc = _attn_fwd(qkv, frow5, fpc)

    abar_r, abar_i, bb_r, bb_i = _ssm_block_params(a_re[0], a_im[0], log_dt[0], b_re[0], b_im[0])
    bb_rt, bb_it = jnp.swapaxes(bb_r, 1, 2).astype(BF16), jnp.swapaxes(bb_i, 1, 2).astype(BF16)
    cr_b, ci_b = c_re[0].astype(BF16), (-c_im[0]).astype(BF16)
    bd_b = jnp.concatenate([_block_diag(bb_rt), _block_diag(bb_it)], axis=1)
    cd_b = jnp.concatenate([_block_diag(jnp.swapaxes(cr_b, 1, 2)), _block_diag(jnp.swapaxes(ci_b, 1, 2))], axis=0)
    bd_t = jnp.concatenate([_block_diag(jnp.swapaxes(bb_rt, 1, 2)), _block_diag(jnp.swapaxes(bb_it, 1, 2))], axis=0)
    cd_t = jnp.concatenate([_block_diag(cr_b), _block_diag(ci_b)], axis=1)
    ys0, xs = _ssm_fwd(rest, bd_b, cd_b, _scan_consts(a_re[0], a_im[0], log_dt[0], False))

    vec = jnp.concatenate([gate, g_final.reshape(1, D_MODEL), jnp.concatenate([d_skip, b_glu], axis=1),
                           jnp.zeros((5, D_MODEL), F32)], axis=0)
    hsel = jnp.repeat(jnp.eye(HEADS, dtype=F32), HEAD_DIM, axis=1)
    (dx2, dga, dgb, do, dza, dzb, dys, dlt_t, g_out, g_ua, g_ub, g_glu, vmid) = _mid(
        o, rest, ys0, x2d, tgt, wmid, vec, hsel)

    lse_t = jnp.transpose(lse_pc.reshape(s, 4, LANES)[:, :, :2], (1, 2, 0))
    lse5 = lse_t.reshape(4, 2, n_att, 1, t_att)
    dlt5 = dlt_t.reshape(4, 2, n_att, 1, t_att)
    dq, dk, dv, dfk, dfq = _attn_bwd(qkv, do, lse5, dlt5, frow5, fpc)
    du, g_bd, g_cdt, da8 = _ssm_bwd(dys, xs, rest, bd_t, cd_t, _scan_consts(a_re[0], a_im[0], log_dt[0], True), d_skip)
    df, dbf8 = _dfcum(dfk, dfq, rest, bf128, selp.T)

    grad_x, vdh = _dh(dq, dk, dv, dga, dgb, dza, du, dzb, df, wqkv_t, wrest_t, x2d, dx2, gs)
    gq, gk, gv, gga, ggb, gza, gu, gzb, gf = _grad_w_rows(h, [dq, dk, dv, dga, dgb, dza, du, dzb, df])
    g_in_t = jnp.concatenate([gq, gk, gv, gf[:HEADS], gza, gu, gzb, gga, ggb], axis=0)

    dgs, dshift = vdh[0:1], vdh[1:2]
    dmod = jnp.concatenate([dshift, dgs * g_norm, vmid[1:2]], axis=1)
    da = jnp.sum(da8, axis=0)
    g_bd = g_bd.reshape(GROUPS, GCH, 2 * STATE)
    g_cdt = g_cdt.reshape(GROUPS, GCH, 2 * STATE)
    g_bbr = jnp.swapaxes(g_bd[:, :, :STATE], 1, 2)
    g_bbi = jnp.swapaxes(g_bd[:, :, STATE:], 1, 2)
    g_cre = g_cdt[:, :, :STATE]
    g_cim = -g_cdt[:, :, STATE:]
    small_shapes = [(1,), (3 * D_MODEL,), (D_MODEL,), (HEADS,), (GROUPS, STATE), (GROUPS, STATE),
                    (GROUPS, STATE, GCH), (GROUPS, STATE, GCH), (GROUPS, GCH, STATE), (GROUPS, GCH, STATE),
                    (S5_W,), (S5_W,), (D_MODEL,)]
    small = _pack([vmid[3, 0:1], dmod, dgs * (1.0 + scale), dbf8[0, :HEADS], da[:NSTATE], da[NSTATE:],
                   g_bbr, g_bbi, g_cre, g_cim, vmid[2, :S5_W], vmid[2, S5_W:], vmid[0]])
    small_all, small_sum = _allgather8(small, "gather_small_grads")
    (loss_s, g_b_ada, g_g_norm, g_b_f, g_abr, g_abi, g_bbr_s, g_bbi_s, g_c_re, g_c_im, g_d_skip, g_b_glu,
     g_g_final) = _unpack(small_sum, small_shapes)
    loss = loss_s[0]
    dmod_all = _unpack(small_all, small_shapes)[1]
    dmod_cols = lax.dynamic_slice_in_dim(dmod_all, chip * ncol, ncol, axis=1)
    g_w_ada = _grad_w_ada(c_all, dmod_cols)
    _, ssm_vjp = jax.vjp(_ssm_block_params, a_re[0], a_im[0], log_dt[0], b_re[0], b_im[0])
    g_a_re, g_a_im, g_log_dt, g_b_re, g_b_im = ssm_vjp((g_abr, g_abi, g_bbr_s, g_bbi_s))

    def shard_cols(g, j):
        n = g.shape[1] // 4
        return g[:, j * n:(j + 1) * n]


    def shard_rows(g, j):
        n = g.shape[0] // 4
        return g[j * n:(j + 1) * n]

    gt_pack = jnp.stack([
        jnp.pad(g_in_t[j * nshard:(j + 1) * nshard].astype(BF16), ((0, SHARD_ROWS - nshard), (0, 0)))
        for j in range(4)])
    gm_pack = jnp.stack([
        jnp.concatenate([shard_rows(g_glu, j).reshape(-1), shard_cols(g_ua, j).reshape(-1),
                         shard_cols(g_ub, j).reshape(-1), shard_rows(g_out, j).reshape(-1)]).astype(BF16)
        .reshape(-1, LANES) for j in range(4)])
    recv_in, recv_misc = _send_other_halves([gt_pack, gm_pack], "pair_swap_weight_grads")
    own_in = lax.dynamic_slice_in_dim(gt_pack, ci * (SHARD_ROWS // 2), SHARD_ROWS // 2, axis=1)
    own_misc = lax.dynamic_slice_in_dim(gm_pack, ci * (gm_pack.shape[1] // 2), gm_pack.shape[1] // 2, axis=1)
    pair_in = _pair_add(own_in, recv_in, "pair_add_w_in")
    pair_misc = _pair_add(own_misc, recv_misc, "pair_add_misc")
    parts_in, parts_misc = _exchange4([pair_in, pair_misc], True, "scatter_weight_grads")
    half_in, half_misc = _sum4(parts_in, "sum4_w_in"), _sum4(parts_misc, "sum4_misc")
    tot_in, tot_misc = _place_halves([half_in, half_misc], "place_weight_grads")
    g_glu_s, g_ua_s, g_ub_s, g_out_s = _unpack(tot_misc, misc_shapes)

    def adam(name, w, g, m, v):
        shape = w.shape
        total = math.prod(shape)
        if len(shape) > 1 and shape[-1] >= LANES:
            cols = shape[-1]
        elif total % LANES == 0:
            cols = LANES
        else:
            cols = total
        two = lambda a: a.reshape(-1, cols)
        d, nm, nv = _adamw(two(w), two(g), two(m), two(v), "adamw_" + name)
        return g.reshape(shape), d.reshape(shape), nm.reshape(shape), nv.reshape(shape)

    back = lambda a: jnp.swapaxes(a, 0, 1)[None]
    d_in_t, nm_in_t, nv_in_t = _adamw(w_in_t, tot_in, m_in_t, v_in_t, "adamw_w_in")
    res_w_in = (back(tot_in[:nshard]), back(d_in_t), back(nm_in_t), back(nv_in_t))

    res = [
        adam("w_ada", w_ada, g_w_ada, m_w_ada, v_w_ada),
        adam("b_ada", b_ada, g_b_ada, m_b_ada, v_b_ada),
        adam("g_norm", g_norm, g_g_norm, m_g_norm, v_g_norm),
        res_w_in,
        adam("b_f", b_f, g_b_f, m_b_f, v_b_f),
        adam("a_re", a_re, g_a_re, m_a_re, v_a_re),
        adam("a_im", a_im, g_a_im, m_a_im, v_a_im),
        adam("log_dt", log_dt, g_log_dt, m_log_dt, v_log_dt),
        adam("b_re", b_re, g_b_re, m_b_re, v_b_re),
        adam("b_im", b_im, g_b_im, m_b_im, v_b_im),
        adam("c_re", c_re, g_c_re, m_c_re, v_c_re),
        adam("c_im", c_im, g_c_im, m_c_im, v_c_im),
        adam("d_skip", d_skip, g_d_skip, m_d_skip, v_d_skip),
        adam("w_glu", w_glu, g_glu_s, m_w_glu, v_w_glu),
        adam("b_glu", b_glu, g_b_glu, m_b_glu, v_b_glu),
        adam("w_up_a", w_up_a, g_ua_s, m_w_up_a, v_w_up_a),
        adam("w_up_b", w_up_b, g_ub_s, m_w_up_b, v_w_up_b),
        adam("w_out", w_out, g_out_s, m_w_out, v_w_out),
        adam("g_final", g_final, g_g_final, m_g_final, v_g_final),
    ]
    grads = [r[0] for r in res]
    deltas = [r[1] for r in res]
    new_m = [r[2] for r in res]
    new_v = [r[3] for r in res]
    return (loss, grad_x[None], *grads, *deltas, *new_m, *new_v)
```

```python
import functools
import math

import jax
import jax.numpy as jnp
from jax import lax
from jax.experimental import pallas as pl
from jax.experimental.pallas import tpu as pltpu

F32 = jnp.float32
BF16 = jnp.bfloat16
HI = lax.Precision.HIGHEST
MESH = pl.DeviceIdType.MESH

D_MODEL = 1024
HEADS = 8
HEAD_DIM = 64
FOX_W = 512
S5_W = 512
GROUPS = 32
STATE = 64
GCH = 16
NSTATE = GROUPS * STATE
EPS = 1e-6
NEG = -1e30

ADAM_LR = 0.001
ADAM_B1 = 0.9
ADAM_B2 = 0.999
ADAM_EPS = 1e-08
ADAM_WD = 0.01
ADAM_STEP = 10

VMEM_LIMIT = 56 * 1024 * 1024
LANES = 128

TM = 256
T_ATT = 512
ATT_CHUNK = 32
ATT_PAIRS = 4
TB_SSM = 256
TK_ACC = 512
TB_CUM = 256
SHARD_ROWS = 1312

O_Q, O_K, O_V, O_F, O_ZA, O_U, O_ZB, O_GA, O_GB, O_END = 0, 512, 1024, 1536, 1544, 2056, 2568, 3080, 4104, 5128
REST_W = 3712
R_GA, R_GB, R_ZA, R_U, R_ZB, R_F = 0, 1024, 2048, 2560, 3072, 3584


def _cparams(sem=None):
    kw = dict(vmem_limit_bytes=VMEM_LIMIT)
    if sem is not None:
        kw["dimension_semantics"] = sem
    return pltpu.CompilerParams(**kw)


def _const(shape):
    nd = len(shape)
    return pl.BlockSpec(shape, lambda *_: (0,) * nd, pipeline_mode=pl.Buffered(1))


def _dot(a, b, precision=None):
    return jnp.dot(a, b, preferred_element_type=F32, precision=precision)


def _dot_nt(a, b):
    return lax.dot_general(a, b, (((1,), (1,)), ((), ())), preferred_element_type=F32)


def _dot_tn(a, b, precision=None):
    return lax.dot_general(a, b, (((0,), (0,)), ((), ())), preferred_element_type=F32, precision=precision)


def _sigmoid(z):
    return 1.0 / (1.0 + jnp.exp(-z))


def _allgather8(xs, name):
    rows = xs.shape[0]

    def body(x_ref, out_ref, sum_ref, send_sems, recv_sems, local_sem):
        x, y, c = lax.axis_index("x"), lax.axis_index("y"), lax.axis_index("c")
        me, sibling = (x, y, c), (x, y, 1 - c)
        chips = [(1 - x, y), (x, 1 - y), (1 - x, 1 - y)]

        def slot(px, py, pc):
            return out_ref.at[4 * px + 2 * py + pc]

        def copy(k, block, to, src=None):
            return pltpu.make_async_remote_copy(
                src_ref=slot(*block) if src is None else src, dst_ref=slot(*block),
                send_sem=send_sems.at[k], recv_sem=recv_sems.at[k], device_id=to, device_id_type=MESH)

        mine = pltpu.make_async_copy(x_ref, slot(*me), local_sem)
        mine.start()
        first = [copy(0, me, sibling, src=x_ref)]
        first += [copy(1 + j, me, (*chip, c), src=x_ref) for j, chip in enumerate(chips)]
        for cp in first:
            cp.start()
        passed = [copy(4 + j, (*chip, c), sibling) for j, chip in enumerate(chips)]
        for j, chip in enumerate(chips):
            copy(1 + j, (*chip, c), me).wait_recv()
            passed[j].start()
        copy(0, sibling, me).wait_recv()
        for j, chip in enumerate(chips):
            copy(4 + j, (*chip, 1 - c), me).wait_recv()
        for cp in first + passed:
            cp.wait_send()
        mine.wait()
        acc = out_ref[0]
        for d in range(1, 8):
            acc = acc + out_ref[d]
        sum_ref[...] = acc

    return pl.pallas_call(
        body, name=name,
        out_shape=(jax.ShapeDtypeStruct((8, rows, LANES), F32), jax.ShapeDtypeStruct((rows, LANES), F32)),
        in_specs=[pl.BlockSpec(memory_space=pltpu.VMEM)],
        out_specs=(pl.BlockSpec(memory_space=pltpu.VMEM), pl.BlockSpec(memory_space=pltpu.VMEM)),
        scratch_shapes=[pltpu.SemaphoreType.DMA((7,)), pltpu.SemaphoreType.DMA((7,)), pltpu.SemaphoreType.DMA],
        compiler_params=_cparams(),
    )(xs)


def _scatter4(srcs, name):
    na = len(srcs)

    def body(*refs):
        src_refs, out_refs = refs[:na], refs[na:2 * na]
        send_sems, recv_sems = refs[2 * na:]
        x, y, c = lax.axis_index("x"), lax.axis_index("y"), lax.axis_index("c")
        peers = [(1 - x, y), (x, 1 - y), (1 - x, 1 - y)]

        def copy(a, k, px, py, slot):
            return pltpu.make_async_remote_copy(
                src_ref=src_refs[a].at[2 * px + py], dst_ref=out_refs[a].at[slot],
                send_sem=send_sems.at[a * 3 + k], recv_sem=recv_sems.at[a * 3 + k],
                device_id=(px, py, c), device_id_type=MESH)

        sends = [copy(a, k, px, py, 2 * x + y) for a in range(na) for k, (px, py) in enumerate(peers)]
        for cp in sends:
            cp.start()
        for a in range(na):
            for k, (px, py) in enumerate(peers):
                copy(a, k, px, py, 2 * px + py).wait_recv()
        for cp in sends:
            cp.wait_send()

    anyspace = pl.BlockSpec(memory_space=pl.ANY)
    return pl.pallas_call(
        body, name=name,
        out_shape=tuple(jax.ShapeDtypeStruct(a.shape, a.dtype) for a in srcs),
        in_specs=[anyspace] * na, out_specs=(anyspace,) * na,
        scratch_shapes=[pltpu.SemaphoreType.DMA((3 * na,)), pltpu.SemaphoreType.DMA((3 * na,))],
        compiler_params=_cparams(),
    )(*srcs)


def _gather_shards(srcs, name):
    na = len(srcs)

    def body(*refs):
        src_refs, out_refs = refs[:na], refs[na:2 * na]
        send_sems, recv_sems = refs[2 * na:]
        x, y, c = lax.axis_index("x"), lax.axis_index("y"), lax.axis_index("c")
        sibling = (x, y, 1 - c)
        peers = [(1 - x, y), (x, 1 - y), (1 - x, 1 - y)]

        def copy(a, k, src, slot, which, to):
            return pltpu.make_async_remote_copy(
                src_ref=src, dst_ref=out_refs[a].at[slot, which],
                send_sem=send_sems.at[a * 6 + k], recv_sem=recv_sems.at[a * 6 + k],
                device_id=to, device_id_type=MESH)

        mine = 2 * x + y
        first = [copy(a, k, src_refs[a].at[c], mine, c, (px, py, c))
                 for a in range(na) for k, (px, py) in enumerate(peers)]
        for cp in first:
            cp.start()
        passed = []
        for a in range(na):
            for k, (px, py) in enumerate(peers):
                slot = 2 * px + py
                landed = out_refs[a].at[slot, c]
                copy(a, k, landed, slot, c, (px, py, c)).wait_recv()
                fwd = copy(a, 3 + k, landed, slot, c, sibling)
                fwd.start()
                passed.append(fwd)
        for a in range(na):
            for k, (px, py) in enumerate(peers):
                slot = 2 * px + py
                copy(a, 3 + k, out_refs[a].at[slot, 1 - c], slot, 1 - c, sibling).wait_recv()
        for cp in first + passed:
            cp.wait_send()

    anyspace = pl.BlockSpec(memory_space=pl.ANY)
    return pl.pallas_call(
        body, name=name,
        out_shape=tuple(jax.ShapeDtypeStruct((4,) + tuple(a.shape), a.dtype) for a in srcs),
        in_specs=[anyspace] * na, out_specs=(anyspace,) * na,
        scratch_shapes=[pltpu.SemaphoreType.DMA((6 * na,)), pltpu.SemaphoreType.DMA((6 * na,))],
        compiler_params=_cparams(),
    )(*srcs)


def _swap_sibling(srcs, name, other_half=False):
    na = len(srcs)

    def body(*refs):
        src_refs, out_refs = refs[:na], refs[na:2 * na]
        send_sems, recv_sems = refs[2 * na:]
        x, y, c = lax.axis_index("x"), lax.axis_index("y"), lax.axis_index("c")
        copies = [pltpu.make_async_remote_copy(
            src_ref=src_refs[a].at[:, 1 - c] if other_half else src_refs[a],
            dst_ref=out_refs[a], send_sem=send_sems.at[a], recv_sem=recv_sems.at[a],
            device_id=(x, y, 1 - c), device_id_type=MESH) for a in range(na)]
        for cp in copies:
            cp.start()
        for cp in copies:
            cp.wait()

    def out_of(a):
        shape = (a.shape[0],) + tuple(a.shape[2:]) if other_half else a.shape
        return jax.ShapeDtypeStruct(shape, a.dtype)

    anyspace = pl.BlockSpec(memory_space=pl.ANY)
    return pl.pallas_call(
        body, name=name, out_shape=tuple(out_of(a) for a in srcs),
        in_specs=[anyspace] * na, out_specs=(anyspace,) * na,
        scratch_shapes=[pltpu.SemaphoreType.DMA((na,)), pltpu.SemaphoreType.DMA((na,))],
        compiler_params=_cparams(),
    )(*srcs)


def _mod_cols(c_all, w, b):
    n = w.shape[1]

    def body(c_ref, w_ref, b_ref, o_ref):
        o_ref[...] = _dot(c_ref[...], w_ref[...], HI) + b_ref[...]

    return pl.pallas_call(
        body, name="mod_cols", out_shape=jax.ShapeDtypeStruct((8, n), F32),
        compiler_params=_cparams())(c_all, w, b)


def _grad_w_ada(c_all, dmod_cols):
    n = dmod_cols.shape[1]

    def body(c_ref, d_ref, o_ref):
        o_ref[...] = _dot_tn(c_ref[...], d_ref[...], HI)

    return pl.pallas_call(
        body, name="grad_w_ada", out_shape=jax.ShapeDtypeStruct((D_MODEL, n), F32),
        compiler_params=_cparams())(c_all, dmod_cols)


def _prenorm(x, gs, shift):
    s = x.shape[0]
    tm = min(TM, s)

    def body(x_ref, gs_ref, sh_ref, h_ref):
        xv = x_ref[...]
        r = lax.rsqrt(jnp.mean(xv * xv, axis=-1, keepdims=True) + EPS)
        h_ref[...] = (xv * r * gs_ref[...] + sh_ref[...]).astype(BF16)

    row = pl.BlockSpec((tm, D_MODEL), lambda i: (i, 0))
    vec = pl.BlockSpec((1, D_MODEL), lambda i: (0, 0))
    return pl.pallas_call(
        body, name="prenorm", grid=(s // tm,), in_specs=[row, vec, vec], out_specs=row,
        out_shape=jax.ShapeDtypeStruct((s, D_MODEL), BF16), compiler_params=_cparams(("parallel",)))(x, gs, shift)


def _mm_rows_nt(a, w_t, out_dtype, name):
    s, k = a.shape
    n = w_t.shape[0]
    tm = min(TM, s)

    def body(a_ref, w_ref, o_ref):
        o_ref[...] = _dot_nt(a_ref[...], w_ref[...]).astype(out_dtype)

    return pl.pallas_call(
        body, name=name, grid=(s // tm,),
        in_specs=[pl.BlockSpec((tm, k), lambda i: (i, 0)), _const((n, k))],
        out_specs=pl.BlockSpec((tm, n), lambda i: (i, 0)),
        out_shape=jax.ShapeDtypeStruct((s, n), out_dtype), compiler_params=_cparams(("parallel",)))(a, w_t)


def _grad_w_rows(h, ds):
    s = h.shape[0]
    tk = min(TK_ACC, s)
    nd = len(ds)
    widths = [d.shape[1] for d in ds]

    def body(*refs):
        h_ref, d_refs = refs[0], refs[1:1 + nd]
        out_refs, accs = refs[1 + nd:1 + 2 * nd], refs[1 + 2 * nd:]
        step = pl.program_id(0)

        @pl.when(step == 0)
        def _():
            for acc in accs:
                acc[...] = jnp.zeros_like(acc)

        hv = h_ref[...]
        for d_ref, acc in zip(d_refs, accs):
            acc[...] += _dot_tn(d_ref[...], hv)

        @pl.when(step == s // tk - 1)
        def _():
            for acc, out in zip(accs, out_refs):
                pltpu.sync_copy(acc, out)

    anyspace = pl.BlockSpec(memory_space=pl.ANY)
    return pl.pallas_call(
        body, name="grad_w_in", grid=(s // tk,),
        in_specs=[pl.BlockSpec((tk, D_MODEL), lambda k: (k, 0))]
                 + [pl.BlockSpec((tk, w), lambda k: (k, 0)) for w in widths],
        out_specs=(anyspace,) * nd,
        out_shape=tuple(jax.ShapeDtypeStruct((w, D_MODEL), F32) for w in widths),
        scratch_shapes=[pltpu.VMEM((w, D_MODEL), F32) for w in widths],
        compiler_params=_cparams(("arbitrary",)))(h, *ds)


def _head_pair_selector():
    rows = jnp.arange(LANES)[:, None]
    cols = jnp.arange(4 * LANES)[None, :]
    return ((rows < HEADS) & (cols == (rows // 2) * LANES + rows % 2)).astype(F32)


def _fcum(rest, bf128, selp):
    s = rest.shape[0]
    tb = min(TB_CUM, s)

    def body(fz_ref, bf_ref, sel_ref, fpc_ref, ft_ref, carry_ref):
        @pl.when(pl.program_id(0) == 0)
        def _():
            carry_ref[...] = jnp.zeros_like(carry_ref)

        z = fz_ref[...] + bf_ref[...]
        logf = jnp.minimum(z, 0.0) - jnp.log(1.0 + jnp.exp(-jnp.abs(z)))
        r = lax.broadcasted_iota(jnp.int32, (tb, tb), 0)
        c = lax.broadcasted_iota(jnp.int32, (tb, tb), 1)
        tri = (c <= r).astype(F32)
        f = _dot(tri, logf, HI) + carry_ref[0:1, :]
        carry_ref[0:1, :] = f[tb - 1:tb, :]
        fpc_ref[...] = _dot(f, sel_ref[...], HI)
        ft_ref[...] = jnp.transpose(f)[0:HEADS, :]

    return pl.pallas_call(
        body, name="forget_cumsum", grid=(s // tb,),
        in_specs=[pl.BlockSpec((tb, LANES), lambda i: (i, R_F // LANES)), _const((1, LANES)), _const((LANES, 4 * LANES))],
        out_specs=(pl.BlockSpec((tb, 4 * LANES), lambda i: (i, 0)), pl.BlockSpec((HEADS, tb), lambda i: (0, i))),
        out_shape=(jax.ShapeDtypeStruct((s, 4 * LANES), F32), jax.ShapeDtypeStruct((HEADS, s), F32)),
        scratch_shapes=[pltpu.VMEM((8, LANES), F32)],
        compiler_params=_cparams(("arbitrary",)))(rest, bf128, selp)


def _dfcum(dfk, dfq, rest, bf128, selq):
    s = rest.shape[0]
    tb = min(TB_CUM, s)
    nb = s // tb

    def body(dk_ref, dq_ref, fz_ref, bf_ref, sel_ref, df_ref, dbf_ref, carry_ref):
        @pl.when(pl.program_id(0) == 0)
        def _():
            carry_ref[...] = jnp.zeros_like(carry_ref)
            dbf_ref[...] = jnp.zeros_like(dbf_ref)

        d = _dot(dk_ref[...] + dq_ref[...], sel_ref[...], HI)
        r = lax.broadcasted_iota(jnp.int32, (tb, tb), 0)
        c = lax.broadcasted_iota(jnp.int32, (tb, tb), 1)
        triu = (c >= r).astype(F32)
        dlogf = _dot(triu, d, HI) + carry_ref[0:1, :]
        carry_ref[0:1, :] = dlogf[0:1, :]
        z = fz_ref[...] + bf_ref[...]
        df = dlogf * (1.0 / (1.0 + jnp.exp(z)))
        df_ref[...] = df.astype(BF16)
        dbf_ref[0:1, :] += jnp.sum(df, axis=0, keepdims=True)

    return pl.pallas_call(
        body, name="forget_grad", grid=(nb,),
        in_specs=[pl.BlockSpec((tb, 4 * LANES), lambda i: (nb - 1 - i, 0)),
                  pl.BlockSpec((tb, 4 * LANES), lambda i: (nb - 1 - i, 0)),
                  pl.BlockSpec((tb, LANES), lambda i: (nb - 1 - i, R_F // LANES)),
                  _const((1, LANES)), _const((4 * LANES, LANES))],
        out_specs=(pl.BlockSpec((tb, LANES), lambda i: (nb - 1 - i, 0)), pl.BlockSpec((8, LANES), lambda i: (0, 0))),
        out_shape=(jax.ShapeDtypeStruct((s, LANES), BF16), jax.ShapeDtypeStruct((8, LANES), F32)),
        scratch_shapes=[pltpu.VMEM((8, LANES), F32)],
        compiler_params=_cparams(("arbitrary",)))(dfk, dfq, rest, bf128, selq)


def _scaled(q):
    return (q.astype(F32) * (HEAD_DIM ** -0.5)).astype(BF16)


def _attn_fwd(qkv, frow5, fpc):
    s = qkv.shape[0]
    t = min(T_ATT, s)
    n = s // t
    ch = min(ATT_CHUNK, t)
    wide = 2 * LANES
    pairs = ATT_PAIRS
    width = pairs * LANES
    groups = 4 // pairs

    def body(q_ref, k_ref, v_ref, fr_ref, fc_ref, o_ref, lse_ref, s_scr, p_scr, m_scr, a_scr, fq_scr, acc_scr):
        i = pl.program_id(1)
        lane = lax.broadcasted_iota(jnp.int32, (t, LANES), 1)
        first = lane < HEAD_DIM
        ones_col = ((lane == 0).astype(BF16), (lane == 1).astype(BF16))
        m_scr[...] = jnp.full(m_scr.shape, NEG, F32)
        acc_scr[...] = jnp.zeros_like(acc_scr)
        qm = []
        for pp in range(pairs):
            q = _scaled(q_ref[:, pp * LANES:(pp + 1) * LANES])
            zq = jnp.zeros_like(q)
            qm += [jnp.where(first, q, zq), jnp.where(first, zq, q)]
            fq_scr[2 * pp] = fc_ref[:, pp * LANES:pp * LANES + 1]
            fq_scr[2 * pp + 1] = fc_ref[:, pp * LANES + 1:pp * LANES + 2]

        def step(j, masked):
            r0 = pl.multiple_of(j * t, t)
            vaug = []
            for pp in range(pairs):
                kb = k_ref[pl.ds(r0, t), pp * LANES:(pp + 1) * LANES]
                vb = v_ref[pl.ds(r0, t), pp * LANES:(pp + 1) * LANES]
                zv = jnp.zeros_like(vb)
                vaug += [jnp.concatenate([jnp.where(first, vb, zv), ones_col[0]], axis=1),
                         jnp.concatenate([jnp.where(first, zv, vb), ones_col[1]], axis=1)]
                for hh in range(2):
                    s_scr[2 * pp + hh] = _dot_nt(qm[2 * pp + hh], kb)
            pv = []
            for hd in range(2 * pairs):
                fk = fr_ref[hd // 2, hd % 2, j]
                for c in range(t // ch):
                    rows = pl.ds(c * ch, ch)
                    sc = s_scr[hd, rows, :] - fk
                    if masked:
                        rq = c * ch + lax.broadcasted_iota(jnp.int32, (ch, t), 0)
                        ck = lax.broadcasted_iota(jnp.int32, (ch, t), 1)
                        sc = jnp.where(ck <= rq, sc, NEG)
                    fq = fq_scr[hd, rows, :]
                    m_old = m_scr[hd, rows, :]
                    m_new = jnp.maximum(m_old, fq + jnp.max(sc, axis=1, keepdims=True))
                    p_scr[hd, rows, :] = jnp.exp(sc + (fq - m_new)).astype(BF16)
                    a_scr[hd, rows, :] = jnp.exp(m_old - m_new)
                    m_scr[hd, rows, :] = m_new
                pv.append(_dot(p_scr[hd], vaug[hd]))
            for pp in range(pairs):
                a0, a1 = a_scr[2 * pp], a_scr[2 * pp + 1]
                alpha = jnp.concatenate([jnp.where(first, a0, a1), jnp.where(lane == 0, a0, a1)], axis=1)
                acc_scr[pp] = acc_scr[pp] * alpha + pv[2 * pp] + pv[2 * pp + 1]
            return 0

        lax.fori_loop(0, i, lambda j, _: step(j, False), 0)
        step(i, True)
        for pp in range(pairs):
            l0 = acc_scr[pp, :, LANES:LANES + 1]
            l1 = acc_scr[pp, :, LANES + 1:LANES + 2]
            o_ref[:, pp * LANES:(pp + 1) * LANES] = acc_scr[pp, :, 0:LANES] * jnp.where(first, 1.0 / l0, 1.0 / l1)
            lse0 = m_scr[2 * pp] + jnp.log(l0)
            lse1 = m_scr[2 * pp + 1] + jnp.log(l1)
            lse_ref[:, pp * LANES:(pp + 1) * LANES] = jnp.where(lane == 0, lse0, jnp.where(lane == 1, lse1, 0.0))

    blk = pl.BlockSpec((t, width), lambda g, i: (i, g))
    return pl.pallas_call(
        body, name="attn_fwd", grid=(groups, n),
        in_specs=[blk,
                  pl.BlockSpec((s, width), lambda g, i: (0, groups + g)),
                  pl.BlockSpec((s, width), lambda g, i: (0, 2 * groups + g)),
                  pl.BlockSpec((pairs, 2, n, 1, t), lambda g, i: (g, 0, 0, 0, 0)),
                  blk],
        out_specs=(blk, blk),
        out_shape=(jax.ShapeDtypeStruct((s, FOX_W), F32), jax.ShapeDtypeStruct((s, 4 * LANES), F32)),
        scratch_shapes=[pltpu.VMEM((2 * pairs, t, t), F32), pltpu.VMEM((2 * pairs, t, t), BF16),
                        pltpu.VMEM((2 * pairs, t, 1), F32), pltpu.VMEM((2 * pairs, t, 1), F32),
                        pltpu.VMEM((2 * pairs, t, 1), F32), pltpu.VMEM((pairs, t, wide), F32)],
        compiler_params=_cparams(("parallel", "arbitrary")))(qkv, qkv, qkv, frow5, fpc)


def _attn_bwd(qkv, do, lse5, dlt5, frow5, fpc):
    s = qkv.shape[0]
    t = min(T_ATT, s)
    n = s // t
    wide = 2 * LANES

    ch = min(ATT_CHUNK, t)

    def body(q_ref, do_ref, k_ref, v_ref, lse_ref, dl_ref, fr_ref, fc_ref,
             dq_ref, dk_ref, dv_ref, dfk_ref, dfq_ref, dq_acc, st_scr, dp_scr, pt_scr, ds_scr, dk_acc, dv_acc, fk_scr):
        j = pl.program_id(1)

        @pl.when(j == 0)
        def _():
            dq_acc[...] = jnp.zeros_like(dq_acc)

        dk_acc[...] = jnp.zeros_like(dk_acc)
        dv_acc[...] = jnp.zeros_like(dv_acc)
        lane = lax.broadcasted_iota(jnp.int32, (t, LANES), 1)
        first = lane < HEAD_DIM
        ones_col = ((lane == 0).astype(BF16), (lane == 1).astype(BF16))
        kb = k_ref[...]
        vb = v_ref[...]
        zk = jnp.zeros_like(kb)
        kaug = (jnp.concatenate([jnp.where(first, kb, zk), ones_col[0]], axis=1),
                jnp.concatenate([jnp.where(first, zk, kb), ones_col[1]], axis=1))
        fk_scr[0] = fc_ref[:, 0:1]
        fk_scr[1] = fc_ref[:, 1:2]

        def step(i, masked):
            r0 = pl.multiple_of(i * t, t)
            qb = _scaled(q_ref[pl.ds(r0, t), :])
            dob = do_ref[pl.ds(r0, t), :]
            zq = jnp.zeros_like(qb)
            qm = (jnp.where(first, qb, zq), jnp.where(first, zq, qb))
            dom = (jnp.where(first, dob, zq), jnp.where(first, zq, dob))
            dq_add = jnp.zeros((t, wide), F32)
            for hh in range(2):
                st_scr[hh] = _dot_nt(kb, qm[hh])
                dp_scr[hh] = _dot_nt(vb, dom[hh])
                bias = fr_ref[0, hh, i] - lse_ref[0, hh, i]
                dl = dl_ref[0, hh, i]
                for c in range(t // ch):
                    rows = pl.ds(c * ch, ch)
                    st = st_scr[hh, rows, :] + (bias - fk_scr[hh, rows, :])
                    if masked:
                        rk = c * ch + lax.broadcasted_iota(jnp.int32, (ch, t), 0)
                        cq = lax.broadcasted_iota(jnp.int32, (ch, t), 1)
                        st = jnp.where(rk <= cq, st, NEG)
                    pt = jnp.exp(st)
                    pt_scr[hh, rows, :] = pt.astype(BF16)
                    ds_scr[hh, rows, :] = (pt * (dp_scr[hh, rows, :] - dl)).astype(BF16)
                dsb = ds_scr[hh]
                dv_acc[...] += _dot(pt_scr[hh], dom[hh])
                dk_acc[...] += _dot(dsb, jnp.concatenate([qm[hh], ones_col[hh]], axis=1))
                dq_add = dq_add + _dot_tn(dsb, kaug[hh])
            dq_acc[pl.ds(r0, t), :] += dq_add
            return 0

        step(j, True)
        lax.fori_loop(j + 1, n, lambda i, _: step(i, False), 0)
        dk_ref[...] = dk_acc[:, 0:LANES].astype(BF16)
        dv_ref[...] = dv_acc[...].astype(BF16)
        dfk_ref[...] = -dk_acc[:, LANES:wide]

        @pl.when(j == n - 1)
        def _():
            dq_ref[...] = (dq_acc[:, 0:LANES] * (HEAD_DIM ** -0.5)).astype(BF16)
            dfq_ref[...] = dq_acc[:, LANES:wide]

    stat = pl.BlockSpec((1, 2, n, 1, t), lambda h, j: (h, 0, 0, 0, 0))
    blk = pl.BlockSpec((t, LANES), lambda h, j: (j, h))
    full = pl.BlockSpec((s, LANES), lambda h, j: (0, h))
    return pl.pallas_call(
        body, name="attn_bwd", grid=(4, n),
        in_specs=[full, full,
                  pl.BlockSpec((t, LANES), lambda h, j: (j, 4 + h)),
                  pl.BlockSpec((t, LANES), lambda h, j: (j, 8 + h)),
                  stat, stat, stat, blk],
        out_specs=(full, blk, blk, blk, full),
        out_shape=(jax.ShapeDtypeStruct((s, FOX_W), BF16), jax.ShapeDtypeStruct((s, FOX_W), BF16),
                   jax.ShapeDtypeStruct((s, FOX_W), BF16), jax.ShapeDtypeStruct((s, 4 * LANES), F32),
                   jax.ShapeDtypeStruct((s, 4 * LANES), F32)),
        scratch_shapes=[pltpu.VMEM((s, wide), F32), pltpu.VMEM((2, t, t), F32), pltpu.VMEM((2, t, t), F32),
                        pltpu.VMEM((2, t, t), BF16), pltpu.VMEM((2, t, t), BF16), pltpu.VMEM((t, wide), F32),
                        pltpu.VMEM((t, LANES), F32), pltpu.VMEM((2, t, 1), F32)],
        compiler_params=_cparams(("parallel", "arbitrary")))(qkv, do, qkv, qkv, lse5, dlt5, frow5, fpc)


def _ssm_block_params(a_re, a_im, log_dt, b_re, b_im):
    dt = jnp.exp(log_dt)[:, None]
    mag = jnp.exp(a_re * dt)
    ar = mag * jnp.cos(a_im * dt)
    ai = mag * jnp.sin(a_im * dt)
    den = a_re * a_re + a_im * a_im
    nr = ar - 1.0
    cr = (nr * a_re + ai * a_im) / den
    ci = (ai * a_re - nr * a_im) / den
    bbr = cr[:, :, None] * b_re - ci[:, :, None] * b_im
    bbi = cr[:, :, None] * b_im + ci[:, :, None] * b_re
    return ar, ai, bbr, bbi


def _block_diag(blocks):
    g, r, c = blocks.shape
    eye = jnp.eye(g, dtype=blocks.dtype)
    return (blocks[:, :, None, :] * eye[:, None, :, None]).reshape(g * r, g * c)


def _scan_consts(a_re, a_im, log_dt, reverse):
    dt = jnp.exp(log_dt)[:, None]
    lr = (a_re * dt).reshape(1, NSTATE)
    li = (a_im * dt).reshape(1, NSTATE)
    if reverse:
        li = -li
    rows = jnp.arange(8, dtype=F32)[:, None]

    def power(k):
        mag = jnp.exp(k * lr)
        return mag * jnp.cos(k * li), mag * jnp.sin(k * li)

    tiles = []
    for k in (1, 2, 4):
        keep = (rows < 8 - k) if reverse else (rows >= k)
        pr, pi_ = power(float(k))
        tiles += [jnp.where(keep, pr, 0.0), jnp.where(keep, pi_, 0.0)]
    expo = (8.0 - rows) if reverse else (rows + 1.0)
    tiles += list(power(expo))
    return jnp.stack([jnp.broadcast_to(tl, (8, NSTATE)) for tl in tiles])


_SCAN_W = 512
_HALF_W = S5_W // 2
_HALF_S = NSTATE // 2


def _half_expand(v, w_ref, out_ref):
    for half in range(2):
        vh = v[:, half * _HALF_W:(half + 1) * _HALF_W]
        for part in range(2):
            c0 = part * NSTATE + half * _HALF_S
            out_ref[:, c0:c0 + _HALF_S] = _dot(vh, w_ref[half * _HALF_W:(half + 1) * _HALF_W, c0:c0 + _HALF_S])


def _half_contract(x_ref, w_ref, half):
    out = None
    for part in range(2):
        r0 = part * NSTATE + half * _HALF_S
        term = _dot(x_ref[:, r0:r0 + _HALF_S].astype(BF16),
                    w_ref[r0:r0 + _HALF_S, half * _HALF_W:(half + 1) * _HALF_W])
        out = term if out is None else out + term
    return out


def _half_outer(v, x_ref, acc_ref):
    for half in range(2):
        vh = v[:, half * _HALF_W:(half + 1) * _HALF_W]
        for part in range(2):
            c0 = part * NSTATE + half * _HALF_S
            acc_ref[:, c0:c0 + _HALF_S] += _dot_tn(vh, x_ref[:, c0:c0 + _HALF_S].astype(BF16))


def _ssm_fwd(rest, bd, cd, consts):
    s = rest.shape[0]
    tb = min(TB_SSM, s)
    ns2 = 2 * NSTATE

    def body(u_ref, bd_ref, cd_ref, cf_ref, y_ref, x_ref, cb_ref):
        @pl.when(pl.program_id(0) == 0)
        def _():
            cb_ref[...] = jnp.zeros_like(cb_ref)

        _half_expand(u_ref[...].astype(BF16), bd_ref, x_ref)

        def tile(ti, _):
            r0 = pl.multiple_of(ti * 8, 8)
            for cc in range(NSTATE // _SCAN_W):
                cr = pl.ds(cc * _SCAN_W, _SCAN_W)
                ci = pl.ds(NSTATE + cc * _SCAN_W, _SCAN_W)
                re = x_ref[pl.ds(r0, 8), cr]
                im = x_ref[pl.ds(r0, 8), ci]
                for n_, k in enumerate((1, 2, 4)):
                    ar = cf_ref[2 * n_, :, cr]
                    ai = cf_ref[2 * n_ + 1, :, cr]
                    sr = pltpu.roll(re, k, 0)
                    si = pltpu.roll(im, k, 0)
                    re, im = re + ar * sr - ai * si, im + ar * si + ai * sr
                pr = cf_ref[6, :, cr]
                pi_ = cf_ref[7, :, cr]
                cbr = cb_ref[:, cr]
                cbi = cb_ref[:, ci]
                re, im = re + pr * cbr - pi_ * cbi, im + pr * cbi + pi_ * cbr
                x_ref[pl.ds(r0, 8), cr] = re
                x_ref[pl.ds(r0, 8), ci] = im
                cb_ref[:, cr] = jnp.broadcast_to(re[7:8, :], (8, _SCAN_W))
                cb_ref[:, ci] = jnp.broadcast_to(im[7:8, :], (8, _SCAN_W))
            return 0

        lax.fori_loop(0, tb // 8, tile, 0)
        for half in range(2):
            y_ref[:, half * _HALF_W:(half + 1) * _HALF_W] = _half_contract(x_ref, cd_ref, half)

    return pl.pallas_call(
        body, name="ssm_fwd", grid=(s // tb,),
        in_specs=[pl.BlockSpec((tb, S5_W), lambda i: (i, R_U // S5_W)), _const((S5_W, ns2)), _const((ns2, S5_W)),
                  _const((8, 8, NSTATE))],
        out_specs=(pl.BlockSpec((tb, S5_W), lambda i: (i, 0)), pl.BlockSpec((tb, ns2), lambda i: (i, 0))),
        out_shape=(jax.ShapeDtypeStruct((s, S5_W), F32), jax.ShapeDtypeStruct((s, ns2), F32)),
        scratch_shapes=[pltpu.VMEM((8, ns2), F32)],
        compiler_params=_cparams(("arbitrary",)))(rest, bd, cd, consts)


def _ssm_bwd(dys, xs, rest, bd_t, cd_t, consts, dskip):
    s = dys.shape[0]
    tb = min(TB_SSM, s)
    nb = s // tb
    ns2 = 2 * NSTATE
    nt = tb // 8

    def body(dy_ref, x_ref, u_ref, bdt_ref, cdt_ref, cf_ref, dsk_ref, du_ref, gb_ref, gc_ref, da_ref,
             g_ref, cb_ref, acc_b, acc_c):
        step = pl.program_id(0)

        @pl.when(step == 0)
        def _():
            cb_ref[...] = jnp.zeros_like(cb_ref)
            acc_b[...] = jnp.zeros_like(acc_b)
            acc_c[...] = jnp.zeros_like(acc_c)
            da_ref[...] = jnp.zeros_like(da_ref)

        dy = dy_ref[...]
        dyb = dy.astype(BF16)
        _half_expand(dyb, cdt_ref, g_ref)
        last_row = lax.broadcasted_iota(jnp.int32, (8, _SCAN_W), 0) == 7

        def tile(tt, _):
            r0 = pl.multiple_of((nt - 1 - tt) * 8, 8)
            for cc in range(NSTATE // _SCAN_W):
                cr = pl.ds(cc * _SCAN_W, _SCAN_W)
                ci = pl.ds(NSTATE + cc * _SCAN_W, _SCAN_W)
                re = g_ref[pl.ds(r0, 8), cr]
                im = g_ref[pl.ds(r0, 8), ci]
                for n_, k in enumerate((1, 2, 4)):
                    ar = cf_ref[2 * n_, :, cr]
                    ai = cf_ref[2 * n_ + 1, :, cr]
                    sr = pltpu.roll(re, 8 - k, 0)
                    si = pltpu.roll(im, 8 - k, 0)
                    re, im = re + ar * sr - ai * si, im + ar * si + ai * sr
                pr = cf_ref[6, :, cr]
                pi_ = cf_ref[7, :, cr]
                cbr = cb_ref[:, cr]
                cbi = cb_ref[:, ci]
                re, im = re + pr * cbr - pi_ * cbi, im + pr * cbi + pi_ * cbr
                g_ref[pl.ds(r0, 8), cr] = re
                g_ref[pl.ds(r0, 8), ci] = im
                gnr = jnp.where(last_row, cbr, pltpu.roll(re, 7, 0))
                gni = jnp.where(last_row, cbi, pltpu.roll(im, 7, 0))
                xr = x_ref[pl.ds(r0, 8), cr]
                xi = x_ref[pl.ds(r0, 8), ci]
                da_ref[:, cr] += gnr * xr + gni * xi
                da_ref[:, ci] += gni * xr - gnr * xi
                cb_ref[:, cr] = jnp.broadcast_to(re[0:1, :], (8, _SCAN_W))
                cb_ref[:, ci] = jnp.broadcast_to(im[0:1, :], (8, _SCAN_W))
            return 0

        lax.fori_loop(0, nt, tile, 0)
        for half in range(2):
            cols = slice(half * _HALF_W, (half + 1) * _HALF_W)
            du_ref[:, cols] = (_half_contract(g_ref, bdt_ref, half) + dy[:, cols] * dsk_ref[:, cols]).astype(BF16)
        _half_outer(u_ref[...].astype(BF16), g_ref, acc_b)
        _half_outer(dyb, x_ref, acc_c)

        @pl.when(step == nb - 1)
        def _():
            for g in range(GROUPS):
                src = slice((g % (GROUPS // 2)) * GCH, (g % (GROUPS // 2) + 1) * GCH)
                dst = slice(g * GCH, (g + 1) * GCH)
                for part in range(2):
                    cols = slice(part * NSTATE + g * STATE, part * NSTATE + (g + 1) * STATE)
                    gb_ref[dst, part * STATE:(part + 1) * STATE] = acc_b[src, cols]
                    gc_ref[dst, part * STATE:(part + 1) * STATE] = acc_c[src, cols]

    rev = lambda i: (nb - 1 - i, 0)
    small = pl.BlockSpec((S5_W, 2 * STATE), lambda i: (0, 0))
    return pl.pallas_call(
        body, name="ssm_bwd", grid=(nb,),
        in_specs=[pl.BlockSpec((tb, S5_W), rev), pl.BlockSpec((tb, ns2), rev),
                  pl.BlockSpec((tb, S5_W), lambda i: (nb - 1 - i, R_U // S5_W)),
                  _const((ns2, S5_W)), _const((S5_W, ns2)), _const((8, 8, NSTATE)), _const((1, S5_W))],
        out_specs=(pl.BlockSpec((tb, S5_W), rev), small, small, pl.BlockSpec((8, ns2), lambda i: (0, 0))),
        out_shape=(jax.ShapeDtypeStruct((s, S5_W), BF16), jax.ShapeDtypeStruct((S5_W, 2 * STATE), F32),
                   jax.ShapeDtypeStruct((S5_W, 2 * STATE), F32), jax.ShapeDtypeStruct((8, ns2), F32)),
        scratch_shapes=[pltpu.VMEM((tb, ns2), F32), pltpu.VMEM((8, ns2), F32),
                        pltpu.VMEM((_HALF_W, ns2), F32), pltpu.VMEM((_HALF_W, ns2), F32)],
        compiler_params=_cparams(("arbitrary",)))(dys, xs, rest, bd_t, cd_t, consts, dskip)


_GELU_C = math.sqrt(2.0 / math.pi)
_GELU_A = 0.044715


def _mid(o, rest, ys0, x, tgt, w, vec, hsel):
    s = o.shape[0]
    tm = min(TM, s)
    nsteps = s // tm
    half = FOX_W

    def body(o_ref, ga_ref, gb_ref, za_ref, u_ref, zb_ref, ys0_ref, x_ref, t_ref,
             wglu_ref, wua_ref, wub_ref, wout_ref, vec_ref, hsel_ref,
             dx2_ref, dga_ref, dgb_ref, do_ref, dza_ref, dzb_ref, dys_ref, dlt_ref,
             gout_hbm, gua_hbm, gub_hbm, gglu_hbm, vout_ref,
             a_out, a_ua, a_ub, a_glu):
        step = pl.program_id(0)

        @pl.when(step == 0)
        def _():
            a_out[...] = jnp.zeros_like(a_out)
            a_ua[...] = jnp.zeros_like(a_ua)
            a_ub[...] = jnp.zeros_like(a_ub)
            a_glu[...] = jnp.zeros_like(a_glu)
            vout_ref[...] = jnp.zeros_like(vout_ref)

        gate = vec_ref[0:1, :]
        gfin = vec_ref[1:2, :]
        dsk = vec_ref[2:3, 0:half]
        bglu = vec_ref[2:3, half:2 * half]

        o_v = o_ref[...]
        za = za_ref[...]
        sza = _sigmoid(za)
        silu_za = za * sza
        ya_b = (o_v * silu_za).astype(BF16)
        u_v = u_ref[...]
        ys = ys0_ref[...] + dsk * u_v
        inner = _GELU_C * (ys + _GELU_A * ys * ys * ys)
        th = jnp.tanh(inner)
        yg = 0.5 * ys * (1.0 + th)
        yg_b = yg.astype(BF16)
        st = _sigmoid(_dot(yg_b, wglu_ref[...]) + bglu)
        yb1 = yg * st
        zb = zb_ref[...]
        szb = _sigmoid(zb)
        silu_zb = zb * szb
        yb_b = (yb1 * silu_zb).astype(BF16)
        ua = _dot(ya_b, wua_ref[...])
        ub = _dot(yb_b, wub_ref[...])
        sga = _sigmoid(ga_ref[...])
        sgb = _sigmoid(gb_ref[...])
        merged_b = (sga * ua + sgb * ub).astype(BF16)
        mo = _dot(merged_b, wout_ref[...])
        x2 = x_ref[...] + gate * mo
        r2 = lax.rsqrt(jnp.mean(x2 * x2, axis=-1, keepdims=True) + EPS)
        x2n = x2 * r2
        diff = x2n * gfin - t_ref[...]
        loss = 0.5 * jnp.sum(jnp.mean(diff * diff, axis=-1, keepdims=True), axis=0, keepdims=True)
        dy = diff * (1.0 / D_MODEL)
        dx2n = dy * gfin
        dx2 = r2 * (dx2n - x2n * jnp.mean(dx2n * x2n, axis=-1, keepdims=True))
        dx2_ref[...] = dx2
        vout_ref[0:1, :] += jnp.sum(dy * x2n, axis=0, keepdims=True)
        vout_ref[1:2, :] += jnp.sum(dx2 * mo, axis=0, keepdims=True)
        vout_ref[3:4, :] += jnp.broadcast_to(loss, (1, D_MODEL))
        dmo_b = (dx2 * gate).astype(BF16)
        dmerged = _dot_nt(dmo_b, wout_ref[...])
        a_out[...] += _dot_tn(merged_b, dmo_b)
        dua_b = (dmerged * sga).astype(BF16)
        dub_b = (dmerged * sgb).astype(BF16)
        dga_ref[...] = (dmerged * ua * sga * (1.0 - sga)).astype(BF16)
        dgb_ref[...] = (dmerged * ub * sgb * (1.0 - sgb)).astype(BF16)
        dya = _dot_nt(dua_b, wua_ref[...])
        dyb = _dot_nt(dub_b, wub_ref[...])
        a_ua[...] += _dot_tn(ya_b, dua_b)
        a_ub[...] += _dot_tn(yb_b, dub_b)
        do_b = (dya * silu_za).astype(BF16)
        do_ref[...] = do_b
        dza_ref[...] = (dya * o_v * (sza * (1.0 + za * (1.0 - sza)))).astype(BF16)
        dlt_ref[...] = lax.dot_general(hsel_ref[...], do_b.astype(F32) * o_v, (((1,), (1,)), ((), ())),
                                       preferred_element_type=F32, precision=HI)
        dyb1 = dyb * silu_zb
        dzb_ref[...] = (dyb * yb1 * (szb * (1.0 + zb * (1.0 - szb)))).astype(BF16)
        dt = dyb1 * yg * st * (1.0 - st)
        dt_b = dt.astype(BF16)
        dyg = dyb1 * st + _dot_nt(dt_b, wglu_ref[...])
        a_glu[...] += _dot_tn(yg_b, dt_b)
        dgelu = 0.5 * (1.0 + th) + 0.5 * ys * (1.0 - th * th) * _GELU_C * (1.0 + 3.0 * _GELU_A * ys * ys)
        dys = dyg * dgelu
        dys_ref[...] = dys
        vout_ref[2:3, 0:half] += jnp.sum(dys * u_v, axis=0, keepdims=True)
        vout_ref[2:3, half:2 * half] += jnp.sum(dt, axis=0, keepdims=True)

        @pl.when(step == nsteps - 1)
        def _():
            pltpu.sync_copy(a_out, gout_hbm)
            pltpu.sync_copy(a_ua, gua_hbm)
            pltpu.sync_copy(a_ub, gub_hbm)
            pltpu.sync_copy(a_glu, gglu_hbm)

    def rows(width, col=0):
        return pl.BlockSpec((tm, width), lambda i, col=col: (i, col))

    anyspace = pl.BlockSpec(memory_space=pl.ANY)
    wshapes = [(S5_W, S5_W), (FOX_W, D_MODEL), (S5_W, D_MODEL), (D_MODEL, D_MODEL)]
    return pl.pallas_call(
        body, name="mid", grid=(nsteps,),
        in_specs=[rows(FOX_W), rows(D_MODEL, R_GA // D_MODEL), rows(D_MODEL, R_GB // D_MODEL),
                  rows(FOX_W, R_ZA // FOX_W), rows(S5_W, R_U // S5_W), rows(S5_W, R_ZB // S5_W),
                  rows(S5_W), rows(D_MODEL), rows(D_MODEL)]
                 + [_const(sh) for sh in wshapes]
                 + [_const((8, D_MODEL)), _const((HEADS, FOX_W))],
        out_specs=(rows(D_MODEL), rows(D_MODEL), rows(D_MODEL), rows(FOX_W), rows(FOX_W), rows(S5_W), rows(S5_W),
                   pl.BlockSpec((HEADS, tm), lambda i: (0, i)),
                   anyspace, anyspace, anyspace, anyspace, pl.BlockSpec((8, D_MODEL), lambda i: (0, 0))),
        out_shape=(jax.ShapeDtypeStruct((s, D_MODEL), F32), jax.ShapeDtypeStruct((s, D_MODEL), BF16),
                   jax.ShapeDtypeStruct((s, D_MODEL), BF16), jax.ShapeDtypeStruct((s, FOX_W), BF16),
                   jax.ShapeDtypeStruct((s, FOX_W), BF16), jax.ShapeDtypeStruct((s, S5_W), BF16),
                   jax.ShapeDtypeStruct((s, S5_W), F32), jax.ShapeDtypeStruct((HEADS, s), F32),
                   jax.ShapeDtypeStruct((D_MODEL, D_MODEL), F32), jax.ShapeDtypeStruct((FOX_W, D_MODEL), F32),
                   jax.ShapeDtypeStruct((S5_W, D_MODEL), F32), jax.ShapeDtypeStruct((S5_W, S5_W), F32),
                   jax.ShapeDtypeStruct((8, D_MODEL), F32)),
        scratch_shapes=[pltpu.VMEM((D_MODEL, D_MODEL), F32), pltpu.VMEM((FOX_W, D_MODEL), F32),
                        pltpu.VMEM((S5_W, D_MODEL), F32), pltpu.VMEM((S5_W, S5_W), F32)],
        compiler_params=_cparams(("arbitrary",)),
    )(o, rest, rest, rest, rest, rest, ys0, x, tgt, *w, vec, hsel)


def _dh(dq, dk, dv, dga, dgb, dza, du, dzb, df, wqkv_t, wrest_t, x, dx2, gs):
    s = x.shape[0]
    tm = min(TM, s)

    def body(dq_ref, dk_ref, dv_ref, dga_ref, dgb_ref, dza_ref, du_ref, dzb_ref, df_ref, wq_ref, wr_ref,
             x_ref, dx2_ref, gs_ref, gx_ref, vout_ref):
        @pl.when(pl.program_id(0) == 0)
        def _():
            vout_ref[...] = jnp.zeros_like(vout_ref)

        dh = _dot(dq_ref[...], wq_ref[0:512, :])
        dh += _dot(dk_ref[...], wq_ref[512:1024, :])
        dh += _dot(dv_ref[...], wq_ref[1024:1536, :])
        dh += _dot(dga_ref[...], wr_ref[R_GA:R_GB, :])
        dh += _dot(dgb_ref[...], wr_ref[R_GB:R_ZA, :])
        dh += _dot(dza_ref[...], wr_ref[R_ZA:R_U, :])
        dh += _dot(du_ref[...], wr_ref[R_U:R_ZB, :])
        dh += _dot(dzb_ref[...], wr_ref[R_ZB:R_F, :])
        dh += _dot(df_ref[...], wr_ref[R_F:REST_W, :])
        xv = x_ref[...]
        r = lax.rsqrt(jnp.mean(xv * xv, axis=-1, keepdims=True) + EPS)
        xn = xv * r
        dxn = dh * gs_ref[...]
        gx_ref[...] = dx2_ref[...] + r * (dxn - xn * jnp.mean(dxn * xn, axis=-1, keepdims=True))
        vout_ref[0:1, :] += jnp.sum(dh * xn, axis=0, keepdims=True)
        vout_ref[1:2, :] += jnp.sum(dh, axis=0, keepdims=True)

    def rows(width):
        return pl.BlockSpec((tm, width), lambda i: (i, 0))

    return pl.pallas_call(
        body, name="dh", grid=(s // tm,),
        in_specs=[rows(512), rows(512), rows(512), rows(1024), rows(1024), rows(512), rows(512), rows(512), rows(128),
                  _const((1536, D_MODEL)), _const((REST_W, D_MODEL)), rows(D_MODEL), rows(D_MODEL), _const((1, D_MODEL))],
        out_specs=(rows(D_MODEL), pl.BlockSpec((8, D_MODEL), lambda i: (0, 0))),
        out_shape=(jax.ShapeDtypeStruct((s, D_MODEL), F32), jax.ShapeDtypeStruct((8, D_MODEL), F32)),
        compiler_params=_cparams(("arbitrary",)),
    )(dq, dk, dv, dga, dgb, dza, du, dzb, df, wqkv_t, wrest_t, x, dx2, gs)


def _row_block(rows, mult=8, cap=512):
    if rows <= mult:
        return rows
    padded = -(-rows // mult) * mult
    for cand in range(min(cap, padded) // mult * mult, 0, -mult):
        if padded % cand == 0:
            return cand
    return padded


def _sum4(parts, name):
    rows, cols = parts.shape[1:]
    br = _row_block(rows, 16)

    def body(p_ref, o_ref):
        acc = p_ref[0].astype(F32)
        for k in range(1, 4):
            acc = acc + p_ref[k].astype(F32)
        o_ref[...] = acc

    return pl.pallas_call(
        body, name=name, grid=(pl.cdiv(rows, br),),
        in_specs=[pl.BlockSpec((4, br, cols), lambda i: (0, i, 0))],
        out_specs=pl.BlockSpec((br, cols), lambda i: (i, 0)),
        out_shape=jax.ShapeDtypeStruct((rows, cols), F32), compiler_params=_cparams(("parallel",)))(parts)


def _pair_add(a, b, name):
    shape = a.shape
    a, b = a.reshape(-1, shape[-1]), b.reshape(-1, shape[-1])
    rows, cols = a.shape
    br = _row_block(rows, 16, 1024)

    def body(a_ref, b_ref, o_ref):
        o_ref[...] = (a_ref[...].astype(F32) + b_ref[...].astype(F32)).astype(BF16)

    spec = pl.BlockSpec((br, cols), lambda i: (i, 0))
    return pl.pallas_call(
        body, name=name, grid=(pl.cdiv(rows, br),), in_specs=[spec, spec], out_specs=spec,
        out_shape=jax.ShapeDtypeStruct((rows, cols), BF16), compiler_params=_cparams(("parallel",)))(a, b).reshape(shape)


def _adamw(w, g, m, v, name):
    rows, cols = w.shape
    br = _row_block(rows)

    def body(w_ref, g_ref, m_ref, v_ref, d_ref, nm_ref, nv_ref):
        gv = g_ref[...]
        nm = ADAM_B1 * m_ref[...] + (1.0 - ADAM_B1) * gv
        nv = ADAM_B2 * v_ref[...] + (1.0 - ADAM_B2) * (gv * gv)
        m_hat = nm / (1.0 - ADAM_B1 ** ADAM_STEP)
        v_hat = nv / (1.0 - ADAM_B2 ** ADAM_STEP)
        d_ref[...] = -ADAM_LR * (m_hat / (jnp.sqrt(v_hat) + ADAM_EPS) + ADAM_WD * w_ref[...])
        nm_ref[...] = nm
        nv_ref[...] = nv

    spec = pl.BlockSpec((br, cols), lambda i: (i, 0))
    shape = jax.ShapeDtypeStruct((rows, cols), F32)
    return pl.pallas_call(
        body, name=name, grid=(pl.cdiv(rows, br),), in_specs=[spec] * 4, out_specs=(spec,) * 3,
        out_shape=(shape,) * 3, compiler_params=_cparams(("parallel",)))(w, g, m, v)


def _pack(parts):
    flat = []
    for p in parts:
        v = p.reshape(-1).astype(F32)
        pad = (-v.shape[0]) % LANES
        if pad:
            v = jnp.concatenate([v, jnp.zeros((pad,), F32)])
        flat.append(v)
    v = jnp.concatenate(flat)
    rows = v.shape[0] // LANES
    pad_rows = (-rows) % 8
    if pad_rows:
        v = jnp.concatenate([v, jnp.zeros((pad_rows * LANES,), F32)])
    return v.reshape(-1, LANES)


def _unpack(packed, shapes):
    lead = packed.shape[:-2]
    flat = packed.reshape(lead + (-1,))
    out, off = [], 0
    for sh in shapes:
        size = math.prod(sh)
        out.append(flat[..., off:off + size].reshape(lead + tuple(sh)))
        off += size + (-size) % LANES
    return out


def kernel(x, c, w_ada, b_ada, g_norm, w_in, b_f, a_re, a_im, log_dt, b_re, b_im, c_re, c_im, d_skip, w_glu, b_glu, w_up_a, w_up_b, w_out, g_final, loss_target, m_w_ada, m_b_ada, m_g_norm, m_w_in, m_b_f, m_a_re, m_a_im, m_log_dt, m_b_re, m_b_im, m_c_re, m_c_im, m_d_skip, m_w_glu, m_b_glu, m_w_up_a, m_w_up_b, m_w_out, m_g_final, v_w_ada, v_b_ada, v_g_norm, v_w_in, v_b_f, v_a_re, v_a_im, v_log_dt, v_b_re, v_b_im, v_c_re, v_c_im, v_d_skip, v_w_glu, v_b_glu, v_w_up_a, v_w_up_b, v_w_out, v_g_final):
    xi, yi, ci = lax.axis_index("x"), lax.axis_index("y"), lax.axis_index("c")
    chip = 2 * xi + yi
    me = 4 * xi + 2 * yi + ci
    s = x.shape[1]
    x2d = x[0]
    tgt = loss_target[0]
    n_att = s // min(T_ATT, s)
    t_att = min(T_ATT, s)

    c_all, _ = _allgather8(c.reshape(8, LANES), "gather_c")
    c_all = c_all.reshape(8, D_MODEL)
    ncol = w_ada.shape[2]
    b_cols = lax.dynamic_slice_in_dim(b_ada, chip * ncol, ncol, axis=1)
    mod_cols = _mod_cols(c_all, w_ada[0], b_cols)
    mod_all, _ = _allgather8(mod_cols.reshape(-1, LANES), "gather_mod")
    mod_all = mod_all.reshape(4, 2, 8, ncol)[:, 0]
    mod_me = lax.dynamic_index_in_dim(mod_all, me, axis=1, keepdims=False).reshape(1, 3 * D_MODEL)
    shift, scale, gate = mod_me[:, :D_MODEL], mod_me[:, D_MODEL:2 * D_MODEL], mod_me[:, 2 * D_MODEL:]
    gs = g_norm * (1.0 + scale)

    nshard = w_in.shape[2]
    w_in_t, m_in_t, v_in_t = (jnp.swapaxes(a[0], 0, 1) for a in (w_in, m_w_in, v_w_in))
    wt_pack = jnp.pad(w_in_t.astype(BF16), ((0, SHARD_ROWS - nshard), (0, 0)))
    misc_shapes = [w_glu.shape[1:], w_up_a.shape[1:], w_up_b.shape[1:], w_out.shape[1:]]
    misc_pack = jnp.concatenate([w.reshape(-1) for w in (w_glu, w_up_a, w_up_b, w_out)]).astype(BF16).reshape(-1, LANES)
    def halves(a):
        return a.reshape((2, a.shape[0] // 2) + a.shape[1:])

    wt_all, misc_all = _gather_shards([halves(wt_pack), halves(misc_pack)], "gather_weights")
    wt_all = lax.dynamic_update_index_in_dim(wt_all, halves(wt_pack), chip, 0).reshape((4,) + wt_pack.shape)
    misc_all = lax.dynamic_update_index_in_dim(misc_all, halves(misc_pack), chip, 0).reshape((4,) + misc_pack.shape)
    p_glu, p_ua, p_ub, p_out = _unpack(misc_all, misc_shapes)

    def w_rows(lo, hi):
        out = []
        for j in range(4):
            a, b = max(lo, j * nshard), min(hi, (j + 1) * nshard)
            if a < b:
                out.append(wt_all[j, a - j * nshard:b - j * nshard])
        return out

    wqkv_t = jnp.concatenate(w_rows(O_Q, O_F), axis=0)
    wrest_t = jnp.concatenate(w_rows(O_GA, O_GB) + w_rows(O_GB, O_END) + w_rows(O_ZA, O_U) + w_rows(O_U, O_ZB)
                              + w_rows(O_ZB, O_GA) + w_rows(O_F, O_ZA)
                              + [jnp.zeros((REST_W - R_F - HEADS, D_MODEL), BF16)], axis=0)
    wmid = (p_glu.reshape(S5_W, S5_W), jnp.concatenate([p_ua[j] for j in range(4)], axis=1),
            jnp.concatenate([p_ub[j] for j in range(4)], axis=1), p_out.reshape(D_MODEL, D_MODEL))

    h = _prenorm(x2d, gs, shift)
    qkv = _mm_rows_nt(h, wqkv_t, BF16, "proj_qkv")
    rest = _mm_rows_nt(h, wrest_t, F32, "proj_rest")
    bf128 = jnp.pad(b_f, ((0, 0), (0, LANES - HEADS)))
    selp = _head_pair_selector()
    fpc, f_t = _fcum(rest, bf128, selp)
    frow5 = f_t.reshape(4, 2, n_att, 1, t_att)
    o, lse_pc = _attn_fwd(qkv, frow5, fpc)

    abar_r, abar_i, bb_r, bb_i = _ssm_block_params(a_re[0], a_im[0], log_dt[0], b_re[0], b_im[0])
    bb_rt, bb_it = jnp.swapaxes(bb_r, 1, 2).astype(BF16), jnp.swapaxes(bb_i, 1, 2).astype(BF16)
    cr_b, ci_b = c_re[0].astype(BF16), (-c_im[0]).astype(BF16)
    bd_b = jnp.concatenate([_block_diag(bb_rt), _block_diag(bb_it)], axis=1)
    cd_b = jnp.concatenate([_block_diag(jnp.swapaxes(cr_b, 1, 2)), _block_diag(jnp.swapaxes(ci_b, 1, 2))], axis=0)
    bd_t = jnp.concatenate([_block_diag(jnp.swapaxes(bb_rt, 1, 2)), _block_diag(jnp.swapaxes(bb_it, 1, 2))], axis=0)
    cd_t = jnp.concatenate([_block_diag(cr_b), _block_diag(ci_b)], axis=1)
    ys0, xs = _ssm_fwd(rest, bd_b, cd_b, _scan_consts(a_re[0], a_im[0], log_dt[0], False))

    vec = jnp.concatenate([gate, g_final.reshape(1, D_MODEL), jnp.concatenate([d_skip, b_glu], axis=1),
                           jnp.zeros((5, D_MODEL), F32)], axis=0)
    hsel = jnp.repeat(jnp.eye(HEADS, dtype=F32), HEAD_DIM, axis=1)
    (dx2, dga, dgb, do, dza, dzb, dys, dlt_t, g_out, g_ua, g_ub, g_glu, vmid) = _mid(
        o, rest, ys0, x2d, tgt, wmid, vec, hsel)

    lse_t = jnp.transpose(lse_pc.reshape(s, 4, LANES)[:, :, :2], (1, 2, 0))
    lse5 = lse_t.reshape(4, 2, n_att, 1, t_att)
    dlt5 = dlt_t.reshape(4, 2, n_att, 1, t_att)
    dq, dk, dv, dfk, dfq = _attn_bwd(qkv, do, lse5, dlt5, frow5, fpc)
    du, g_bd, g_cdt, da8 = _ssm_bwd(dys, xs, rest, bd_t, cd_t, _scan_consts(a_re[0], a_im[0], log_dt[0], True), d_skip)
    df, dbf8 = _dfcum(dfk, dfq, rest, bf128, selp.T)

    grad_x, vdh = _dh(dq, dk, dv, dga, dgb, dza, du, dzb, df, wqkv_t, wrest_t, x2d, dx2, gs)
    gq, gk, gv, gga, ggb, gza, gu, gzb, gf = _grad_w_rows(h, [dq, dk, dv, dga, dgb, dza, du, dzb, df])
    g_in_t = jnp.concatenate([gq, gk, gv, gf[:HEADS], gza, gu, gzb, gga, ggb], axis=0)

    dgs, dshift = vdh[0:1], vdh[1:2]
    dmod = jnp.concatenate([dshift, dgs * g_norm, vmid[1:2]], axis=1)
    da = jnp.sum(da8, axis=0)
    g_bd = g_bd.reshape(GROUPS, GCH, 2 * STATE)
    g_cdt = g_cdt.reshape(GROUPS, GCH, 2 * STATE)
    g_bbr = jnp.swapaxes(g_bd[:, :, :STATE], 1, 2)
    g_bbi = jnp.swapaxes(g_bd[:, :, STATE:], 1, 2)
    g_cre = g_cdt[:, :, :STATE]
    g_cim = -g_cdt[:, :, STATE:]
    small_shapes = [(1,), (3 * D_MODEL,), (D_MODEL,), (HEADS,), (GROUPS, STATE), (GROUPS, STATE),
                    (GROUPS, STATE, GCH), (GROUPS, STATE, GCH), (GROUPS, GCH, STATE), (GROUPS, GCH, STATE),
                    (S5_W,), (S5_W,), (D_MODEL,)]
    small = _pack([vmid[3, 0:1], dmod, dgs * (1.0 + scale), dbf8[0, :HEADS], da[:NSTATE], da[NSTATE:],
                   g_bbr, g_bbi, g_cre, g_cim, vmid[2, :S5_W], vmid[2, S5_W:], vmid[0]])
    small_all, small_sum = _allgather8(small, "gather_small_grads")
    (loss_s, g_b_ada, g_g_norm, g_b_f, g_abr, g_abi, g_bbr_s, g_bbi_s, g_c_re, g_c_im, g_d_skip, g_b_glu,
     g_g_final) = _unpack(small_sum, small_shapes)
    loss = loss_s[0]
    dmod_all = _unpack(small_all, small_shapes)[1]
    dmod_cols = lax.dynamic_slice_in_dim(dmod_all, chip * ncol, ncol, axis=1)
    g_w_ada = _grad_w_ada(c_all, dmod_cols)
    _, ssm_vjp = jax.vjp(_ssm_block_params, a_re[0], a_im[0], log_dt[0], b_re[0], b_im[0])
    g_a_re, g_a_im, g_log_dt, g_b_re, g_b_im = ssm_vjp((g_abr, g_abi, g_bbr_s, g_bbi_s))

    def shard_cols(g, j):
        n = g.shape[1] // 4
        return g[:, j * n:(j + 1) * n]


    def shard_rows(g, j):
        n = g.shape[0] // 4
        return g[j * n:(j + 1) * n]

    gt_pack = jnp.stack([
        jnp.pad(g_in_t[j * nshard:(j + 1) * nshard].astype(BF16), ((0, SHARD_ROWS - nshard), (0, 0)))
        for j in range(4)])
    gm_pack = jnp.stack([
        jnp.concatenate([shard_rows(g_glu, j).reshape(-1), shard_cols(g_ua, j).reshape(-1),
                         shard_cols(g_ub, j).reshape(-1), shard_rows(g_out, j).reshape(-1)]).astype(BF16)
        .reshape(-1, LANES) for j in range(4)])
    def halves4(a):
        return a.reshape((4, 2, a.shape[1] // 2) + a.shape[2:])

    gt_pack, gm_pack = halves4(gt_pack), halves4(gm_pack)
    recv_in, recv_misc = _swap_sibling([gt_pack, gm_pack], "pair_swap_weight_grads", other_half=True)
    own_in = lax.dynamic_index_in_dim(gt_pack, ci, axis=1, keepdims=False)
    own_misc = lax.dynamic_index_in_dim(gm_pack, ci, axis=1, keepdims=False)
    pair_in = _pair_add(own_in, recv_in, "pair_add_w_in")
    pair_misc = _pair_add(own_misc, recv_misc, "pair_add_misc")
    parts_in, parts_misc = _scatter4([pair_in, pair_misc], "scatter_weight_grads")
    parts_in = lax.dynamic_update_slice_in_dim(parts_in, lax.dynamic_slice_in_dim(pair_in, chip, 1, 0), chip, 0)
    parts_misc = lax.dynamic_update_slice_in_dim(parts_misc, lax.dynamic_slice_in_dim(pair_misc, chip, 1, 0), chip, 0)
    half_in, half_misc = _sum4(parts_in, "sum4_w_in"), _sum4(parts_misc, "sum4_misc")
    sib_in, sib_misc = _swap_sibling([half_in, half_misc], "swap_weight_grads")

    def both_halves(mine, theirs):
        return jnp.concatenate([jnp.where(ci == 0, mine, theirs), jnp.where(ci == 0, theirs, mine)], axis=0)

    tot_in, tot_misc = both_halves(half_in, sib_in), both_halves(half_misc, sib_misc)
    g_glu_s, g_ua_s, g_ub_s, g_out_s = _unpack(tot_misc, misc_shapes)

    def adam(name, w, g, m, v):
        shape = w.shape
        total = math.prod(shape)
        if len(shape) > 1 and shape[-1] >= LANES:
            cols = shape[-1]
        elif total % LANES == 0:
            cols = LANES
        else:
            cols = total
        two = lambda a: a.reshape(-1, cols)
        d, nm, nv = _adamw(two(w), two(g), two(m), two(v), "adamw_" + name)
        return g.reshape(shape), d.reshape(shape), nm.reshape(shape), nv.reshape(shape)

    back = lambda a: jnp.swapaxes(a, 0, 1)[None]
    d_in_t, nm_in_t, nv_in_t = _adamw(w_in_t, tot_in, m_in_t, v_in_t, "adamw_w_in")
    res_w_in = (back(tot_in[:nshard]), back(d_in_t), back(nm_in_t), back(nv_in_t))

    res = [
        adam("w_ada", w_ada, g_w_ada, m_w_ada, v_w_ada),
        adam("b_ada", b_ada, g_b_ada, m_b_ada, v_b_ada),
        adam("g_norm", g_norm, g_g_norm, m_g_norm, v_g_norm),
        res_w_in,
        adam("b_f", b_f, g_b_f, m_b_f, v_b_f),
        adam("a_re", a_re, g_a_re, m_a_re, v_a_re),
        adam("a_im", a_im, g_a_im, m_a_im, v_a_im),
        adam("log_dt", log_dt, g_log_dt, m_log_dt, v_log_dt),
        adam("b_re", b_re, g_b_re, m_b_re, v_b_re),
        adam("b_im", b_im, g_b_im, m_b_im, v_b_im),
        adam("c_re", c_re, g_c_re, m_c_re, v_c_re),
        adam("c_im", c_im, g_c_im, m_c_im, v_c_im),
        adam("d_skip", d_skip, g_d_skip, m_d_skip, v_d_skip),
        adam("w_glu", w_glu, g_glu_s, m_w_glu, v_w_glu),
        adam("b_glu", b_glu, g_b_glu, m_b_glu, v_b_glu),
        adam("w_up_a", w_up_a, g_ua_s, m_w_up_a, v_w_up_a),
        adam("w_up_b", w_up_b, g_ub_s, m_w_up_b, v_w_up_b),
        adam("w_out", w_out, g_out_s, m_w_out, v_w_out),
        adam("g_final", g_final, g_g_final, m_g_final, v_g_final),
    ]
    grads = [r[0] for r in res]
    deltas = [r[1] for r in res]
    new_m = [r[2] for r in res]
    new_v = [r[3] for r in res]
    return (loss, grad_x[None], *grads, *deltas, *new_m, *new_v)
```

```python
import functools
import math

import jax
import jax.numpy as jnp
from jax import lax
from jax.experimental import pallas as pl
from jax.experimental.pallas import tpu as pltpu

F32 = jnp.float32
BF16 = jnp.bfloat16
HI = lax.Precision.HIGHEST
MESH = pl.DeviceIdType.MESH

D_MODEL = 1024
HEADS = 8
HEAD_DIM = 64
FOX_W = 512
S5_W = 512
GROUPS = 32
STATE = 64
GCH = 16
NSTATE = GROUPS * STATE
EPS = 1e-6
NEG = -1e30

ADAM_LR = 0.001
ADAM_B1 = 0.9
ADAM_B2 = 0.999
ADAM_EPS = 1e-08
ADAM_WD = 0.01
ADAM_STEP = 10

VMEM_LIMIT = 56 * 1024 * 1024
LANES = 128

TM = 256
T_ATT = 512
ATT_CHUNK = 32
ATT_PAIRS = 4
TB_SSM = 256
TK_ACC = 512
TB_CUM = 256
SHARD_ROWS = 1312

O_Q, O_K, O_V, O_F, O_ZA, O_U, O_ZB, O_GA, O_GB, O_END = 0, 512, 1024, 1536, 1544, 2056, 2568, 3080, 4104, 5128
REST_W = 3712
R_GA, R_GB, R_ZA, R_U, R_ZB, R_F = 0, 1024, 2048, 2560, 3072, 3584


def _cparams(sem=None):
    kw = dict(vmem_limit_bytes=VMEM_LIMIT)
    if sem is not None:
        kw["dimension_semantics"] = sem
    return pltpu.CompilerParams(**kw)


def _const(shape):
    nd = len(shape)
    return pl.BlockSpec(shape, lambda *_: (0,) * nd, pipeline_mode=pl.Buffered(1))


def _dot(a, b, precision=None):
    return jnp.dot(a, b, preferred_element_type=F32, precision=precision)


def _dot_nt(a, b):
    return lax.dot_general(a, b, (((1,), (1,)), ((), ())), preferred_element_type=F32)


def _dot_tn(a, b, precision=None):
    return lax.dot_general(a, b, (((0,), (0,)), ((), ())), preferred_element_type=F32, precision=precision)


def _sigmoid(z):
    return 1.0 / (1.0 + jnp.exp(-z))


def _allgather8(xs, name):
    rows = xs.shape[0]

    def body(x_ref, out_ref, sum_ref, send_sems, recv_sems, local_sem):
        x, y, c = lax.axis_index("x"), lax.axis_index("y"), lax.axis_index("c")
        me, sibling = (x, y, c), (x, y, 1 - c)
        chips = [(1 - x, y), (x, 1 - y), (1 - x, 1 - y)]

        def slot(px, py, pc):
            return out_ref.at[4 * px + 2 * py + pc]

        def copy(k, block, to, src=None):
            return pltpu.make_async_remote_copy(
                src_ref=slot(*block) if src is None else src, dst_ref=slot(*block),
                send_sem=send_sems.at[k], recv_sem=recv_sems.at[k], device_id=to, device_id_type=MESH)

        mine = pltpu.make_async_copy(x_ref, slot(*me), local_sem)
        mine.start()
        first = [copy(0, me, sibling, src=x_ref)]
        first += [copy(1 + j, me, (*chip, c), src=x_ref) for j, chip in enumerate(chips)]
        for cp in first:
            cp.start()
        passed = [copy(4 + j, (*chip, c), sibling) for j, chip in enumerate(chips)]
        for j, chip in enumerate(chips):
            copy(1 + j, (*chip, c), me).wait_recv()
            passed[j].start()
        copy(0, sibling, me).wait_recv()
        for j, chip in enumerate(chips):
            copy(4 + j, (*chip, 1 - c), me).wait_recv()
        for cp in first + passed:
            cp.wait_send()
        mine.wait()
        acc = out_ref[0]
        for d in range(1, 8):
            acc = acc + out_ref[d]
        sum_ref[...] = acc

    return pl.pallas_call(
        body, name=name,
        out_shape=(jax.ShapeDtypeStruct((8, rows, LANES), F32), jax.ShapeDtypeStruct((rows, LANES), F32)),
        in_specs=[pl.BlockSpec(memory_space=pltpu.VMEM)],
        out_specs=(pl.BlockSpec(memory_space=pltpu.VMEM), pl.BlockSpec(memory_space=pltpu.VMEM)),
        scratch_shapes=[pltpu.SemaphoreType.DMA((7,)), pltpu.SemaphoreType.DMA((7,)), pltpu.SemaphoreType.DMA],
        compiler_params=_cparams(),
    )(xs)


def _scatter4(srcs, name):
    na = len(srcs)

    def body(*refs):
        src_refs, out_refs = refs[:na], refs[na:2 * na]
        send_sems, recv_sems = refs[2 * na:]
        x, y, c = lax.axis_index("x"), lax.axis_index("y"), lax.axis_index("c")
        peers = [(1 - x, y), (x, 1 - y), (1 - x, 1 - y)]

        def copy(a, k, px, py, slot):
            return pltpu.make_async_remote_copy(
                src_ref=src_refs[a].at[2 * px + py], dst_ref=out_refs[a].at[slot],
                send_sem=send_sems.at[a * 3 + k], recv_sem=recv_sems.at[a * 3 + k],
                device_id=(px, py, c), device_id_type=MESH)

        sends = [copy(a, k, px, py, 2 * x + y) for a in range(na) for k, (px, py) in enumerate(peers)]
        for cp in sends:
            cp.start()
        for a in range(na):
            for k, (px, py) in enumerate(peers):
                copy(a, k, px, py, 2 * px + py).wait_recv()
        for cp in sends:
            cp.wait_send()

    anyspace = pl.BlockSpec(memory_space=pl.ANY)
    return pl.pallas_call(
        body, name=name,
        out_shape=tuple(jax.ShapeDtypeStruct(a.shape, a.dtype) for a in srcs),
        in_specs=[anyspace] * na, out_specs=(anyspace,) * na,
        scratch_shapes=[pltpu.SemaphoreType.DMA((3 * na,)), pltpu.SemaphoreType.DMA((3 * na,))],
        compiler_params=_cparams(),
    )(*srcs)


def _gather_shards(srcs, name):
    na = len(srcs)

    def body(*refs):
        src_refs, out_refs = refs[:na], refs[na:2 * na]
        send_sems, recv_sems = refs[2 * na:]
        x, y, c = lax.axis_index("x"), lax.axis_index("y"), lax.axis_index("c")
        sibling = (x, y, 1 - c)
        peers = [(1 - x, y), (x, 1 - y), (1 - x, 1 - y)]

        def copy(a, k, src, slot, which, to):
            return pltpu.make_async_remote_copy(
                src_ref=src, dst_ref=out_refs[a].at[slot, which],
                send_sem=send_sems.at[a * 6 + k], recv_sem=recv_sems.at[a * 6 + k],
                device_id=to, device_id_type=MESH)

        mine = 2 * x + y
        first = [copy(a, k, src_refs[a].at[c], mine, c, (px, py, c))
                 for a in range(na) for k, (px, py) in enumerate(peers)]
        for cp in first:
            cp.start()
        passed = []
        for a in range(na):
            for k, (px, py) in enumerate(peers):
                slot = 2 * px + py
                landed = out_refs[a].at[slot, c]
                copy(a, k, landed, slot, c, (px, py, c)).wait_recv()
                fwd = copy(a, 3 + k, landed, slot, c, sibling)
                fwd.start()
                passed.append(fwd)
        for a in range(na):
            for k, (px, py) in enumerate(peers):
                slot = 2 * px + py
                copy(a, 3 + k, out_refs[a].at[slot, 1 - c], slot, 1 - c, sibling).wait_recv()
        for cp in first + passed:
            cp.wait_send()

    anyspace = pl.BlockSpec(memory_space=pl.ANY)
    return pl.pallas_call(
        body, name=name,
        out_shape=tuple(jax.ShapeDtypeStruct((4,) + tuple(a.shape), a.dtype) for a in srcs),
        in_specs=[anyspace] * na, out_specs=(anyspace,) * na,
        scratch_shapes=[pltpu.SemaphoreType.DMA((6 * na,)), pltpu.SemaphoreType.DMA((6 * na,))],
        compiler_params=_cparams(),
    )(*srcs)


def _swap_sibling(srcs, name, other_half=False):
    na = len(srcs)

    def body(*refs):
        src_refs, out_refs = refs[:na], refs[na:2 * na]
        send_sems, recv_sems = refs[2 * na:]
        x, y, c = lax.axis_index("x"), lax.axis_index("y"), lax.axis_index("c")
        copies = [pltpu.make_async_remote_copy(
            src_ref=src_refs[a].at[:, 1 - c] if other_half else src_refs[a],
            dst_ref=out_refs[a], send_sem=send_sems.at[a], recv_sem=recv_sems.at[a],
            device_id=(x, y, 1 - c), device_id_type=MESH) for a in range(na)]
        for cp in copies:
            cp.start()
        for cp in copies:
            cp.wait()

    def out_of(a):
        shape = (a.shape[0],) + tuple(a.shape[2:]) if other_half else a.shape
        return jax.ShapeDtypeStruct(shape, a.dtype)

    anyspace = pl.BlockSpec(memory_space=pl.ANY)
    return pl.pallas_call(
        body, name=name, out_shape=tuple(out_of(a) for a in srcs),
        in_specs=[anyspace] * na, out_specs=(anyspace,) * na,
        scratch_shapes=[pltpu.SemaphoreType.DMA((na,)), pltpu.SemaphoreType.DMA((na,))],
        compiler_params=_cparams(),
    )(*srcs)


def _mod_cols(c_all, w, b):
    n = w.shape[1]

    def body(c_ref, w_ref, b_ref, o_ref):
        o_ref[...] = _dot(c_ref[...], w_ref[...], HI) + b_ref[...]

    return pl.pallas_call(
        body, name="mod_cols", out_shape=jax.ShapeDtypeStruct((8, n), F32),
        compiler_params=_cparams())(c_all, w, b)


def _grad_w_ada(c_all, dmod_cols):
    n = dmod_cols.shape[1]

    def body(c_ref, d_ref, o_ref):
        o_ref[...] = _dot_tn(c_ref[...], d_ref[...], HI)

    return pl.pallas_call(
        body, name="grad_w_ada", out_shape=jax.ShapeDtypeStruct((D_MODEL, n), F32),
        compiler_params=_cparams())(c_all, dmod_cols)


def _prenorm(x, gs, shift):
    s = x.shape[0]
    tm = min(TM, s)

    def body(x_ref, gs_ref, sh_ref, h_ref):
        xv = x_ref[...]
        r = lax.rsqrt(jnp.mean(xv * xv, axis=-1, keepdims=True) + EPS)
        h_ref[...] = (xv * r * gs_ref[...] + sh_ref[...]).astype(BF16)

    row = pl.BlockSpec((tm, D_MODEL), lambda i: (i, 0))
    vec = pl.BlockSpec((1, D_MODEL), lambda i: (0, 0))
    return pl.pallas_call(
        body, name="prenorm", grid=(s // tm,), in_specs=[row, vec, vec], out_specs=row,
        out_shape=jax.ShapeDtypeStruct((s, D_MODEL), BF16), compiler_params=_cparams(("parallel",)))(x, gs, shift)


def _mm_rows_nt(a, w_t, out_dtype, name):
    s, k = a.shape
    n = w_t.shape[0]
    tm = min(TM, s)

    def body(a_ref, w_ref, o_ref):
        o_ref[...] = _dot_nt(a_ref[...], w_ref[...]).astype(out_dtype)

    return pl.pallas_call(
        body, name=name, grid=(s // tm,),
        in_specs=[pl.BlockSpec((tm, k), lambda i: (i, 0)), _const((n, k))],
        out_specs=pl.BlockSpec((tm, n), lambda i: (i, 0)),
        out_shape=jax.ShapeDtypeStruct((s, n), out_dtype), compiler_params=_cparams(("parallel",)))(a, w_t)


def _grad_w_rows(h, ds):
    s = h.shape[0]
    tk = min(TK_ACC, s)
    nd = len(ds)
    widths = [d.shape[1] for d in ds]

    def body(*refs):
        h_ref, d_refs = refs[0], refs[1:1 + nd]
        out_refs, accs = refs[1 + nd:1 + 2 * nd], refs[1 + 2 * nd:]
        step = pl.program_id(0)

        @pl.when(step == 0)
        def _():
            for acc in accs:
                acc[...] = jnp.zeros_like(acc)

        hv = h_ref[...]
        for d_ref, acc in zip(d_refs, accs):
            acc[...] += _dot_tn(d_ref[...], hv)

        @pl.when(step == s // tk - 1)
        def _():
            for acc, out in zip(accs, out_refs):
                pltpu.sync_copy(acc, out)

    anyspace = pl.BlockSpec(memory_space=pl.ANY)
    return pl.pallas_call(
        body, name="grad_w_in", grid=(s // tk,),
        in_specs=[pl.BlockSpec((tk, D_MODEL), lambda k: (k, 0))]
                 + [pl.BlockSpec((tk, w), lambda k: (k, 0)) for w in widths],
        out_specs=(anyspace,) * nd,
        out_shape=tuple(jax.ShapeDtypeStruct((w, D_MODEL), F32) for w in widths),
        scratch_shapes=[pltpu.VMEM((w, D_MODEL), F32) for w in widths],
        compiler_params=_cparams(("arbitrary",)))(h, *ds)


def _head_pair_selector():
    rows = jnp.arange(LANES)[:, None]
    cols = jnp.arange(4 * LANES)[None, :]
    return ((rows < HEADS) & (cols == (rows // 2) * LANES + rows % 2)).astype(F32)


def _fcum(rest, bf128, selp):
    s = rest.shape[0]
    tb = min(TB_CUM, s)

    def body(fz_ref, bf_ref, sel_ref, fpc_ref, ft_ref, carry_ref):
        @pl.when(pl.program_id(0) == 0)
        def _():
            carry_ref[...] = jnp.zeros_like(carry_ref)

        z = fz_ref[...] + bf_ref[...]
        logf = jnp.minimum(z, 0.0) - jnp.log(1.0 + jnp.exp(-jnp.abs(z)))
        r = lax.broadcasted_iota(jnp.int32, (tb, tb), 0)
        c = lax.broadcasted_iota(jnp.int32, (tb, tb), 1)
        tri = (c <= r).astype(F32)
        f = _dot(tri, logf, HI) + carry_ref[0:1, :]
        carry_ref[0:1, :] = f[tb - 1:tb, :]
        fpc_ref[...] = _dot(f, sel_ref[...], HI)
        ft_ref[...] = jnp.transpose(f)[0:HEADS, :]

    return pl.pallas_call(
        body, name="forget_cumsum", grid=(s // tb,),
        in_specs=[pl.BlockSpec((tb, LANES), lambda i: (i, R_F // LANES)), _const((1, LANES)), _const((LANES, 4 * LANES))],
        out_specs=(pl.BlockSpec((tb, 4 * LANES), lambda i: (i, 0)), pl.BlockSpec((HEADS, tb), lambda i: (0, i))),
        out_shape=(jax.ShapeDtypeStruct((s, 4 * LANES), F32), jax.ShapeDtypeStruct((HEADS, s), F32)),
        scratch_shapes=[pltpu.VMEM((8, LANES), F32)],
        compiler_params=_cparams(("arbitrary",)))(rest, bf128, selp)


def _dfcum(dfk, dfq, rest, bf128, selq):
    s = rest.shape[0]
    tb = min(TB_CUM, s)
    nb = s // tb

    def body(dk_ref, dq_ref, fz_ref, bf_ref, sel_ref, df_ref, dbf_ref, carry_ref):
        @pl.when(pl.program_id(0) == 0)
        def _():
            carry_ref[...] = jnp.zeros_like(carry_ref)
            dbf_ref[...] = jnp.zeros_like(dbf_ref)

        d = _dot(dk_ref[...] + dq_ref[...], sel_ref[...], HI)
        r = lax.broadcasted_iota(jnp.int32, (tb, tb), 0)
        c = lax.broadcasted_iota(jnp.int32, (tb, tb), 1)
        triu = (c >= r).astype(F32)
        dlogf = _dot(triu, d, HI) + carry_ref[0:1, :]
        carry_ref[0:1, :] = dlogf[0:1, :]
        z = fz_ref[...] + bf_ref[...]
        df = dlogf * (1.0 / (1.0 + jnp.exp(z)))
        df_ref[...] = df.astype(BF16)
        dbf_ref[0:1, :] += jnp.sum(df, axis=0, keepdims=True)

    return pl.pallas_call(
        body, name="forget_grad", grid=(nb,),
        in_specs=[pl.BlockSpec((tb, 4 * LANES), lambda i: (nb - 1 - i, 0)),
                  pl.BlockSpec((tb, 4 * LANES), lambda i: (nb - 1 - i, 0)),
                  pl.BlockSpec((tb, LANES), lambda i: (nb - 1 - i, R_F // LANES)),
                  _const((1, LANES)), _const((4 * LANES, LANES))],
        out_specs=(pl.BlockSpec((tb, LANES), lambda i: (nb - 1 - i, 0)), pl.BlockSpec((8, LANES), lambda i: (0, 0))),
        out_shape=(jax.ShapeDtypeStruct((s, LANES), BF16), jax.ShapeDtypeStruct((8, LANES), F32)),
        scratch_shapes=[pltpu.VMEM((8, LANES), F32)],
        compiler_params=_cparams(("arbitrary",)))(dfk, dfq, rest, bf128, selq)


def _scaled(q):
    return (q.astype(F32) * (HEAD_DIM ** -0.5)).astype(BF16)


def _attn_fwd(qkv, frow5, fpc):
    s = qkv.shape[0]
    t = min(T_ATT, s)
    n = s // t
    ch = min(ATT_CHUNK, t)
    wide = 2 * LANES
    pairs = ATT_PAIRS
    width = pairs * LANES
    groups = 4 // pairs

    def body(q_ref, k_ref, v_ref, fr_ref, fc_ref, o_ref, lse_ref, s_scr, p_scr, m_scr, a_scr, fq_scr, acc_scr):
        i = pl.program_id(1)
        lane = lax.broadcasted_iota(jnp.int32, (t, LANES), 1)
        first = lane < HEAD_DIM
        ones_col = ((lane == 0).astype(BF16), (lane == 1).astype(BF16))
        m_scr[...] = jnp.full(m_scr.shape, NEG, F32)
        acc_scr[...] = jnp.zeros_like(acc_scr)
        qm = []
        for pp in range(pairs):
            q = _scaled(q_ref[:, pp * LANES:(pp + 1) * LANES])
            zq = jnp.zeros_like(q)
            qm += [jnp.where(first, q, zq), jnp.where(first, zq, q)]
            fq_scr[2 * pp] = fc_ref[:, pp * LANES:pp * LANES + 1]
            fq_scr[2 * pp + 1] = fc_ref[:, pp * LANES + 1:pp * LANES + 2]

        def step(j, masked):
            r0 = pl.multiple_of(j * t, t)
            vaug = []
            for pp in range(pairs):
                kb = k_ref[pl.ds(r0, t), pp * LANES:(pp + 1) * LANES]
                vb = v_ref[pl.ds(r0, t), pp * LANES:(pp + 1) * LANES]
                zv = jnp.zeros_like(vb)
                vaug += [jnp.concatenate([jnp.where(first, vb, zv), ones_col[0]], axis=1),
                         jnp.concatenate([jnp.where(first, zv, vb), ones_col[1]], axis=1)]
                for hh in range(2):
                    s_scr[2 * pp + hh] = _dot_nt(qm[2 * pp + hh], kb)
            pv = []
            for hd in range(2 * pairs):
                fk = fr_ref[hd // 2, hd % 2, j]
                for c in range(t // ch):
                    rows = pl.ds(c * ch, ch)
                    hi = min(t, (c * ch // LANES + 1) * LANES) if masked else t
                    sc = s_scr[hd, rows, 0:hi] - fk[:, 0:hi]
                    if masked:
                        rq = c * ch + lax.broadcasted_iota(jnp.int32, (ch, hi), 0)
                        ck = lax.broadcasted_iota(jnp.int32, (ch, hi), 1)
                        sc = jnp.where(ck <= rq, sc, NEG)
                    fq = fq_scr[hd, rows, :]
                    m_old = m_scr[hd, rows, :]
                    m_new = jnp.maximum(m_old, fq + jnp.max(sc, axis=1, keepdims=True))
                    p_scr[hd, rows, 0:hi] = jnp.exp(sc + (fq - m_new)).astype(BF16)
                    if hi < t:
                        p_scr[hd, rows, hi:t] = jnp.zeros((ch, t - hi), BF16)
                    a_scr[hd, rows, :] = jnp.exp(m_old - m_new)
                    m_scr[hd, rows, :] = m_new
                pv.append(_dot(p_scr[hd], vaug[hd]))
            for pp in range(pairs):
                a0, a1 = a_scr[2 * pp], a_scr[2 * pp + 1]
                alpha = jnp.concatenate([jnp.where(first, a0, a1), jnp.where(lane == 0, a0, a1)], axis=1)
                acc_scr[pp] = acc_scr[pp] * alpha + pv[2 * pp] + pv[2 * pp + 1]
            return 0

        lax.fori_loop(0, i, lambda j, _: step(j, False), 0)
        step(i, True)
        for pp in range(pairs):
            l0 = acc_scr[pp, :, LANES:LANES + 1]
            l1 = acc_scr[pp, :, LANES + 1:LANES + 2]
            o_ref[:, pp * LANES:(pp + 1) * LANES] = acc_scr[pp, :, 0:LANES] * jnp.where(first, 1.0 / l0, 1.0 / l1)
            lse0 = m_scr[2 * pp] + jnp.log(l0)
            lse1 = m_scr[2 * pp + 1] + jnp.log(l1)
            lse_ref[:, pp * LANES:(pp + 1) * LANES] = jnp.where(lane == 0, lse0, jnp.where(lane == 1, lse1, 0.0))

    blk = pl.BlockSpec((t, width), lambda g, i: (i, g))
    return pl.pallas_call(
        body, name="attn_fwd", grid=(groups, n),
        in_specs=[blk,
                  pl.BlockSpec((s, width), lambda g, i: (0, groups + g)),
                  pl.BlockSpec((s, width), lambda g, i: (0, 2 * groups + g)),
                  pl.BlockSpec((pairs, 2, n, 1, t), lambda g, i: (g, 0, 0, 0, 0)),
                  blk],
        out_specs=(blk, blk),
        out_shape=(jax.ShapeDtypeStruct((s, FOX_W), F32), jax.ShapeDtypeStruct((s, 4 * LANES), F32)),
        scratch_shapes=[pltpu.VMEM((2 * pairs, t, t), F32), pltpu.VMEM((2 * pairs, t, t), BF16),
                        pltpu.VMEM((2 * pairs, t, 1), F32), pltpu.VMEM((2 * pairs, t, 1), F32),
                        pltpu.VMEM((2 * pairs, t, 1), F32), pltpu.VMEM((pairs, t, wide), F32)],
        compiler_params=_cparams(("parallel", "arbitrary")))(qkv, qkv, qkv, frow5, fpc)


def _attn_bwd(qkv, do, lse5, dlt5, frow5, fpc):
    s = qkv.shape[0]
    t = min(T_ATT, s)
    n = s // t
    wide = 2 * LANES

    ch = min(ATT_CHUNK, t)

    def body(q_ref, do_ref, k_ref, v_ref, lse_ref, dl_ref, fr_ref, fc_ref,
             dq_ref, dk_ref, dv_ref, dfk_ref, dfq_ref, dq_acc, st_scr, dp_scr, pt_scr, ds_scr, dk_acc, dv_acc, fk_scr):
        j = pl.program_id(1)

        @pl.when(j == 0)
        def _():
            dq_acc[...] = jnp.zeros_like(dq_acc)

        dk_acc[...] = jnp.zeros_like(dk_acc)
        dv_acc[...] = jnp.zeros_like(dv_acc)
        lane = lax.broadcasted_iota(jnp.int32, (t, LANES), 1)
        first = lane < HEAD_DIM
        ones_col = ((lane == 0).astype(BF16), (lane == 1).astype(BF16))
        kb = k_ref[...]
        vb = v_ref[...]
        zk = jnp.zeros_like(kb)
        kaug = (jnp.concatenate([jnp.where(first, kb, zk), ones_col[0]], axis=1),
                jnp.concatenate([jnp.where(first, zk, kb), ones_col[1]], axis=1))
        fk_scr[0] = fc_ref[:, 0:1]
        fk_scr[1] = fc_ref[:, 1:2]

        def step(i, masked):
            r0 = pl.multiple_of(i * t, t)
            qb = _scaled(q_ref[pl.ds(r0, t), :])
            dob = do_ref[pl.ds(r0, t), :]
            zq = jnp.zeros_like(qb)
            qm = (jnp.where(first, qb, zq), jnp.where(first, zq, qb))
            dom = (jnp.where(first, dob, zq), jnp.where(first, zq, dob))
            dq_add = jnp.zeros((t, wide), F32)
            for hh in range(2):
                st_scr[hh] = _dot_nt(kb, qm[hh])
                dp_scr[hh] = _dot_nt(vb, dom[hh])
                bias = fr_ref[0, hh, i] - lse_ref[0, hh, i]
                dl = dl_ref[0, hh, i]
                for c in range(t // ch):
                    rows = pl.ds(c * ch, ch)
                    lo = c * ch // LANES * LANES if masked else 0
                    st = st_scr[hh, rows, lo:t] + (bias[:, lo:t] - fk_scr[hh, rows, :])
                    if masked:
                        rk = c * ch + lax.broadcasted_iota(jnp.int32, (ch, t - lo), 0)
                        cq = lo + lax.broadcasted_iota(jnp.int32, (ch, t - lo), 1)
                        st = jnp.where(rk <= cq, st, NEG)
                    pt = jnp.exp(st)
                    pt_scr[hh, rows, lo:t] = pt.astype(BF16)
                    ds_scr[hh, rows, lo:t] = (pt * (dp_scr[hh, rows, lo:t] - dl[:, lo:t])).astype(BF16)
                    if lo > 0:
                        pt_scr[hh, rows, 0:lo] = jnp.zeros((ch, lo), BF16)
                        ds_scr[hh, rows, 0:lo] = jnp.zeros((ch, lo), BF16)
                dsb = ds_scr[hh]
                dv_acc[...] += _dot(pt_scr[hh], dom[hh])
                dk_acc[...] += _dot(dsb, jnp.concatenate([qm[hh], ones_col[hh]], axis=1))
                dq_add = dq_add + _dot_tn(dsb, kaug[hh])
            dq_acc[pl.ds(r0, t), :] += dq_add
            return 0

        step(j, True)
        lax.fori_loop(j + 1, n, lambda i, _: step(i, False), 0)
        dk_ref[...] = dk_acc[:, 0:LANES].astype(BF16)
        dv_ref[...] = dv_acc[...].astype(BF16)
        dfk_ref[...] = -dk_acc[:, LANES:wide]

        @pl.when(j == n - 1)
        def _():
            dq_ref[...] = (dq_acc[:, 0:LANES] * (HEAD_DIM ** -0.5)).astype(BF16)
            dfq_ref[...] = dq_acc[:, LANES:wide]

    stat = pl.BlockSpec((1, 2, n, 1, t), lambda h, j: (h, 0, 0, 0, 0))
    blk = pl.BlockSpec((t, LANES), lambda h, j: (j, h))
    full = pl.BlockSpec((s, LANES), lambda h, j: (0, h))
    return pl.pallas_call(
        body, name="attn_bwd", grid=(4, n),
        in_specs=[full, full,
                  pl.BlockSpec((t, LANES), lambda h, j: (j, 4 + h)),
                  pl.BlockSpec((t, LANES), lambda h, j: (j, 8 + h)),
                  stat, stat, stat, blk],
        out_specs=(full, blk, blk, blk, full),
        out_shape=(jax.ShapeDtypeStruct((s, FOX_W), BF16), jax.ShapeDtypeStruct((s, FOX_W), BF16),
                   jax.ShapeDtypeStruct((s, FOX_W), BF16), jax.ShapeDtypeStruct((s, 4 * LANES), F32),
                   jax.ShapeDtypeStruct((s, 4 * LANES), F32)),
        scratch_shapes=[pltpu.VMEM((s, wide), F32), pltpu.VMEM((2, t, t), F32), pltpu.VMEM((2, t, t), F32),
                        pltpu.VMEM((2, t, t), BF16), pltpu.VMEM((2, t, t), BF16), pltpu.VMEM((t, wide), F32),
                        pltpu.VMEM((t, LANES), F32), pltpu.VMEM((2, t, 1), F32)],
        compiler_params=_cparams(("parallel", "arbitrary")))(qkv, do, qkv, qkv, lse5, dlt5, frow5, fpc)


def _ssm_block_params(a_re, a_im, log_dt, b_re, b_im):
    dt = jnp.exp(log_dt)[:, None]
    mag = jnp.exp(a_re * dt)
    ar = mag * jnp.cos(a_im * dt)
    ai = mag * jnp.sin(a_im * dt)
    den = a_re * a_re + a_im * a_im
    nr = ar - 1.0
    cr = (nr * a_re + ai * a_im) / den
    ci = (ai * a_re - nr * a_im) / den
    bbr = cr[:, :, None] * b_re - ci[:, :, None] * b_im
    bbi = cr[:, :, None] * b_im + ci[:, :, None] * b_re
    return ar, ai, bbr, bbi


def _block_diag(blocks):
    g, r, c = blocks.shape
    eye = jnp.eye(g, dtype=blocks.dtype)
    return (blocks[:, :, None, :] * eye[:, None, :, None]).reshape(g * r, g * c)


def _scan_consts(a_re, a_im, log_dt, reverse):
    dt = jnp.exp(log_dt)[:, None]
    lr = (a_re * dt).reshape(1, NSTATE)
    li = (a_im * dt).reshape(1, NSTATE)
    if reverse:
        li = -li
    rows = jnp.arange(8, dtype=F32)[:, None]

    def power(k):
        mag = jnp.exp(k * lr)
        return mag * jnp.cos(k * li), mag * jnp.sin(k * li)

    tiles = []
    for k in (1, 2, 4):
        keep = (rows < 8 - k) if reverse else (rows >= k)
        pr, pi_ = power(float(k))
        tiles += [jnp.where(keep, pr, 0.0), jnp.where(keep, pi_, 0.0)]
    expo = (8.0 - rows) if reverse else (rows + 1.0)
    tiles += list(power(expo))
    return jnp.stack([jnp.broadcast_to(tl, (8, NSTATE)) for tl in tiles])


_SCAN_W = 512
_HALF_W = S5_W // 2
_HALF_S = NSTATE // 2


def _compact_diag(blocks_re, blocks_im):
    hg = GROUPS // 2
    return jnp.concatenate([_block_diag(b[h * hg:(h + 1) * hg]) for b in (blocks_re, blocks_im) for h in range(2)],
                           axis=1)


def _half_expand(v, w_ref, out_ref):
    for half in range(2):
        vh = v[:, half * _HALF_W:(half + 1) * _HALF_W]
        for part in range(2):
            c0 = part * NSTATE + half * _HALF_S
            out_ref[:, c0:c0 + _HALF_S] = _dot(vh, w_ref[:, c0:c0 + _HALF_S])


def _half_contract(x_ref, w_ref, half):
    out = None
    for part in range(2):
        r0 = part * NSTATE + half * _HALF_S
        term = _dot_nt(x_ref[:, r0:r0 + _HALF_S].astype(BF16), w_ref[:, r0:r0 + _HALF_S])
        out = term if out is None else out + term
    return out


def _half_outer(v, x_ref, acc_ref):
    for half in range(2):
        vh = v[:, half * _HALF_W:(half + 1) * _HALF_W]
        for part in range(2):
            c0 = part * NSTATE + half * _HALF_S
            acc_ref[:, c0:c0 + _HALF_S] += _dot_tn(vh, x_ref[:, c0:c0 + _HALF_S].astype(BF16))


def _ssm_fwd(rest, bd, cd, consts):
    s = rest.shape[0]
    tb = min(TB_SSM, s)
    ns2 = 2 * NSTATE

    def body(u_ref, bd_ref, cd_ref, cf_ref, y_ref, x_ref, cb_ref):
        @pl.when(pl.program_id(0) == 0)
        def _():
            cb_ref[...] = jnp.zeros_like(cb_ref)

        _half_expand(u_ref[...].astype(BF16), bd_ref, x_ref)

        def tile(ti, _):
            r0 = pl.multiple_of(ti * 8, 8)
            for cc in range(NSTATE // _SCAN_W):
                cr = pl.ds(cc * _SCAN_W, _SCAN_W)
                ci = pl.ds(NSTATE + cc * _SCAN_W, _SCAN_W)
                re = x_ref[pl.ds(r0, 8), cr]
                im = x_ref[pl.ds(r0, 8), ci]
                for n_, k in enumerate((1, 2, 4)):
                    ar = cf_ref[2 * n_, :, cr]
                    ai = cf_ref[2 * n_ + 1, :, cr]
                    sr = pltpu.roll(re, k, 0)
                    si = pltpu.roll(im, k, 0)
                    re, im = re + ar * sr - ai * si, im + ar * si + ai * sr
                pr = cf_ref[6, :, cr]
                pi_ = cf_ref[7, :, cr]
                cbr = cb_ref[:, cr]
                cbi = cb_ref[:, ci]
                re, im = re + pr * cbr - pi_ * cbi, im + pr * cbi + pi_ * cbr
                x_ref[pl.ds(r0, 8), cr] = re
                x_ref[pl.ds(r0, 8), ci] = im
                cb_ref[:, cr] = jnp.broadcast_to(re[7:8, :], (8, _SCAN_W))
                cb_ref[:, ci] = jnp.broadcast_to(im[7:8, :], (8, _SCAN_W))
            return 0

        lax.fori_loop(0, tb // 8, tile, 0)
        for half in range(2):
            y_ref[:, half * _HALF_W:(half + 1) * _HALF_W] = _half_contract(x_ref, cd_ref, half)

    return pl.pallas_call(
        body, name="ssm_fwd", grid=(s // tb,),
        in_specs=[pl.BlockSpec((tb, S5_W), lambda i: (i, R_U // S5_W)), _const((_HALF_W, ns2)), _const((_HALF_W, ns2)),
                  _const((8, 8, NSTATE))],
        out_specs=(pl.BlockSpec((tb, S5_W), lambda i: (i, 0)), pl.BlockSpec((tb, ns2), lambda i: (i, 0))),
        out_shape=(jax.ShapeDtypeStruct((s, S5_W), F32), jax.ShapeDtypeStruct((s, ns2), F32)),
        scratch_shapes=[pltpu.VMEM((8, ns2), F32)],
        compiler_params=_cparams(("arbitrary",)))(rest, bd, cd, consts)


def _ssm_bwd(dys, xs, rest, bd, cd, consts, dskip):
    s = dys.shape[0]
    tb = min(TB_SSM, s)
    nb = s // tb
    ns2 = 2 * NSTATE
    nt = tb // 8

    def body(dy_ref, x_ref, u_ref, bd_ref, cd_ref, cf_ref, dsk_ref, du_ref, gb_ref, gc_ref, da_ref,
             g_ref, cb_ref, acc_b, acc_c):
        step = pl.program_id(0)

        @pl.when(step == 0)
        def _():
            cb_ref[...] = jnp.zeros_like(cb_ref)
            acc_b[...] = jnp.zeros_like(acc_b)
            acc_c[...] = jnp.zeros_like(acc_c)
            da_ref[...] = jnp.zeros_like(da_ref)

        dy = dy_ref[...]
        dyb = dy.astype(BF16)
        _half_expand(dyb, cd_ref, g_ref)
        last_row = lax.broadcasted_iota(jnp.int32, (8, _SCAN_W), 0) == 7

        def tile(tt, _):
            r0 = pl.multiple_of((nt - 1 - tt) * 8, 8)
            for cc in range(NSTATE // _SCAN_W):
                cr = pl.ds(cc * _SCAN_W, _SCAN_W)
                ci = pl.ds(NSTATE + cc * _SCAN_W, _SCAN_W)
                re = g_ref[pl.ds(r0, 8), cr]
                im = g_ref[pl.ds(r0, 8), ci]
                for n_, k in enumerate((1, 2, 4)):
                    ar = cf_ref[2 * n_, :, cr]
                    ai = cf_ref[2 * n_ + 1, :, cr]
                    sr = pltpu.roll(re, 8 - k, 0)
                    si = pltpu.roll(im, 8 - k, 0)
                    re, im = re + ar * sr - ai * si, im + ar * si + ai * sr
                pr = cf_ref[6, :, cr]
                pi_ = cf_ref[7, :, cr]
                cbr = cb_ref[:, cr]
                cbi = cb_ref[:, ci]
                re, im = re + pr * cbr - pi_ * cbi, im + pr * cbi + pi_ * cbr
                g_ref[pl.ds(r0, 8), cr] = re
                g_ref[pl.ds(r0, 8), ci] = im
                gnr = jnp.where(last_row, cbr, pltpu.roll(re, 7, 0))
                gni = jnp.where(last_row, cbi, pltpu.roll(im, 7, 0))
                xr = x_ref[pl.ds(r0, 8), cr]
                xi = x_ref[pl.ds(r0, 8), ci]
                da_ref[:, cr] += gnr * xr + gni * xi
                da_ref[:, ci] += gni * xr - gnr * xi
                cb_ref[:, cr] = jnp.broadcast_to(re[0:1, :], (8, _SCAN_W))
                cb_ref[:, ci] = jnp.broadcast_to(im[0:1, :], (8, _SCAN_W))
            return 0

        lax.fori_loop(0, nt, tile, 0)
        for half in range(2):
            cols = slice(half * _HALF_W, (half + 1) * _HALF_W)
            du_ref[:, cols] = (_half_contract(g_ref, bd_ref, half) + dy[:, cols] * dsk_ref[:, cols]).astype(BF16)
        _half_outer(u_ref[...].astype(BF16), g_ref, acc_b)
        _half_outer(dyb, x_ref, acc_c)

        @pl.when(step == nb - 1)
        def _():
            for g in range(GROUPS):
                src = slice((g % (GROUPS // 2)) * GCH, (g % (GROUPS // 2) + 1) * GCH)
                dst = slice(g * GCH, (g + 1) * GCH)
                for part in range(2):
                    cols = slice(part * NSTATE + g * STATE, part * NSTATE + (g + 1) * STATE)
                    gb_ref[dst, part * STATE:(part + 1) * STATE] = acc_b[src, cols]
                    gc_ref[dst, part * STATE:(part + 1) * STATE] = acc_c[src, cols]

    rev = lambda i: (nb - 1 - i, 0)
    small = pl.BlockSpec((S5_W, 2 * STATE), lambda i: (0, 0))
    return pl.pallas_call(
        body, name="ssm_bwd", grid=(nb,),
        in_specs=[pl.BlockSpec((tb, S5_W), rev), pl.BlockSpec((tb, ns2), rev),
                  pl.BlockSpec((tb, S5_W), lambda i: (nb - 1 - i, R_U // S5_W)),
                  _const((_HALF_W, ns2)), _const((_HALF_W, ns2)), _const((8, 8, NSTATE)), _const((1, S5_W))],
        out_specs=(pl.BlockSpec((tb, S5_W), rev), small, small, pl.BlockSpec((8, ns2), lambda i: (0, 0))),
        out_shape=(jax.ShapeDtypeStruct((s, S5_W), BF16), jax.ShapeDtypeStruct((S5_W, 2 * STATE), F32),
                   jax.ShapeDtypeStruct((S5_W, 2 * STATE), F32), jax.ShapeDtypeStruct((8, ns2), F32)),
        scratch_shapes=[pltpu.VMEM((tb, ns2), F32), pltpu.VMEM((8, ns2), F32),
                        pltpu.VMEM((_HALF_W, ns2), F32), pltpu.VMEM((_HALF_W, ns2), F32)],
        compiler_params=_cparams(("arbitrary",)))(dys, xs, rest, bd, cd, consts, dskip)


_GELU_C = math.sqrt(2.0 / math.pi)
_GELU_A = 0.044715


def _mid(o, rest, ys0, x, tgt, w, vec, hsel):
    s = o.shape[0]
    tm = min(TM, s)
    nsteps = s // tm
    half = FOX_W

    def body(o_ref, ga_ref, gb_ref, za_ref, u_ref, zb_ref, ys0_ref, x_ref, t_ref,
             wglu_ref, wua_ref, wub_ref, wout_ref, vec_ref, hsel_ref,
             dx2_ref, dga_ref, dgb_ref, do_ref, dza_ref, dzb_ref, dys_ref, dlt_ref,
             gout_hbm, gua_hbm, gub_hbm, gglu_hbm, vout_ref,
             a_out, a_ua, a_ub, a_glu):
        step = pl.program_id(0)

        @pl.when(step == 0)
        def _():
            a_out[...] = jnp.zeros_like(a_out)
            a_ua[...] = jnp.zeros_like(a_ua)
            a_ub[...] = jnp.zeros_like(a_ub)
            a_glu[...] = jnp.zeros_like(a_glu)
            vout_ref[...] = jnp.zeros_like(vout_ref)

        gate = vec_ref[0:1, :]
        gfin = vec_ref[1:2, :]
        dsk = vec_ref[2:3, 0:half]
        bglu = vec_ref[2:3, half:2 * half]

        o_v = o_ref[...]
        za = za_ref[...]
        sza = _sigmoid(za)
        silu_za = za * sza
        ya_b = (o_v * silu_za).astype(BF16)
        u_v = u_ref[...]
        ys = ys0_ref[...] + dsk * u_v
        inner = _GELU_C * (ys + _GELU_A * ys * ys * ys)
        th = jnp.tanh(inner)
        yg = 0.5 * ys * (1.0 + th)
        yg_b = yg.astype(BF16)
        st = _sigmoid(_dot(yg_b, wglu_ref[...]) + bglu)
        yb1 = yg * st
        zb = zb_ref[...]
        szb = _sigmoid(zb)
        silu_zb = zb * szb
        yb_b = (yb1 * silu_zb).astype(BF16)
        ua = _dot(ya_b, wua_ref[...])
        ub = _dot(yb_b, wub_ref[...])
        sga = _sigmoid(ga_ref[...])
        sgb = _sigmoid(gb_ref[...])
        merged_b = (sga * ua + sgb * ub).astype(BF16)
        mo = _dot(merged_b, wout_ref[...])
        x2 = x_ref[...] + gate * mo
        r2 = lax.rsqrt(jnp.mean(x2 * x2, axis=-1, keepdims=True) + EPS)
        x2n = x2 * r2
        diff = x2n * gfin - t_ref[...]
        loss = 0.5 * jnp.sum(jnp.mean(diff * diff, axis=-1, keepdims=True), axis=0, keepdims=True)
        dy = diff * (1.0 / D_MODEL)
        dx2n = dy * gfin
        dx2 = r2 * (dx2n - x2n * jnp.mean(dx2n * x2n, axis=-1, keepdims=True))
        dx2_ref[...] = dx2
        vout_ref[0:1, :] += jnp.sum(dy * x2n, axis=0, keepdims=True)
        vout_ref[1:2, :] += jnp.sum(dx2 * mo, axis=0, keepdims=True)
        vout_ref[3:4, :] += jnp.broadcast_to(loss, (1, D_MODEL))
        dmo_b = (dx2 * gate).astype(BF16)
        dmerged = _dot_nt(dmo_b, wout_ref[...])
        a_out[...] += _dot_tn(merged_b, dmo_b)
        dua_b = (dmerged * sga).astype(BF16)
        dub_b = (dmerged * sgb).astype(BF16)
        dga_ref[...] = (dmerged * ua * sga * (1.0 - sga)).astype(BF16)
        dgb_ref[...] = (dmerged * ub * sgb * (1.0 - sgb)).astype(BF16)
        dya = _dot_nt(dua_b, wua_ref[...])
        dyb = _dot_nt(dub_b, wub_ref[...])
        a_ua[...] += _dot_tn(ya_b, dua_b)
        a_ub[...] += _dot_tn(yb_b, dub_b)
        do_b = (dya * silu_za).astype(BF16)
        do_ref[...] = do_b
        dza_ref[...] = (dya * o_v * (sza * (1.0 + za * (1.0 - sza)))).astype(BF16)
        dlt_ref[...] = lax.dot_general(hsel_ref[...], do_b.astype(F32) * o_v, (((1,), (1,)), ((), ())),
                                       preferred_element_type=F32, precision=HI)
        dyb1 = dyb * silu_zb
        dzb_ref[...] = (dyb * yb1 * (szb * (1.0 + zb * (1.0 - szb)))).astype(BF16)
        dt = dyb1 * yg * st * (1.0 - st)
        dt_b = dt.astype(BF16)
        dyg = dyb1 * st + _dot_nt(dt_b, wglu_ref[...])
        a_glu[...] += _dot_tn(yg_b, dt_b)
        dgelu = 0.5 * (1.0 + th) + 0.5 * ys * (1.0 - th * th) * _GELU_C * (1.0 + 3.0 * _GELU_A * ys * ys)
        dys = dyg * dgelu
        dys_ref[...] = dys
        vout_ref[2:3, 0:half] += jnp.sum(dys * u_v, axis=0, keepdims=True)
        vout_ref[2:3, half:2 * half] += jnp.sum(dt, axis=0, keepdims=True)

        @pl.when(step == nsteps - 1)
        def _():
            pltpu.sync_copy(a_out, gout_hbm)
            pltpu.sync_copy(a_ua, gua_hbm)
            pltpu.sync_copy(a_ub, gub_hbm)
            pltpu.sync_copy(a_glu, gglu_hbm)

    def rows(width, col=0):
        return pl.BlockSpec((tm, width), lambda i, col=col: (i, col))

    anyspace = pl.BlockSpec(memory_space=pl.ANY)
    wshapes = [(S5_W, S5_W), (FOX_W, D_MODEL), (S5_W, D_MODEL), (D_MODEL, D_MODEL)]
    return pl.pallas_call(
        body, name="mid", grid=(nsteps,),
        in_specs=[rows(FOX_W), rows(D_MODEL, R_GA // D_MODEL), rows(D_MODEL, R_GB // D_MODEL),
                  rows(FOX_W, R_ZA // FOX_W), rows(S5_W, R_U // S5_W), rows(S5_W, R_ZB // S5_W),
                  rows(S5_W), rows(D_MODEL), rows(D_MODEL)]
                 + [_const(sh) for sh in wshapes]
                 + [_const((8, D_MODEL)), _const((HEADS, FOX_W))],
        out_specs=(rows(D_MODEL), rows(D_MODEL), rows(D_MODEL), rows(FOX_W), rows(FOX_W), rows(S5_W), rows(S5_W),
                   pl.BlockSpec((HEADS, tm), lambda i: (0, i)),
                   anyspace, anyspace, anyspace, anyspace, pl.BlockSpec((8, D_MODEL), lambda i: (0, 0))),
        out_shape=(jax.ShapeDtypeStruct((s, D_MODEL), F32), jax.ShapeDtypeStruct((s, D_MODEL), BF16),
                   jax.ShapeDtypeStruct((s, D_MODEL), BF16), jax.ShapeDtypeStruct((s, FOX_W), BF16),
                   jax.ShapeDtypeStruct((s, FOX_W), BF16), jax.ShapeDtypeStruct((s, S5_W), BF16),
                   jax.ShapeDtypeStruct((s, S5_W), F32), jax.ShapeDtypeStruct((HEADS, s), F32),
                   jax.ShapeDtypeStruct((D_MODEL, D_MODEL), F32), jax.ShapeDtypeStruct((FOX_W, D_MODEL), F32),
                   jax.ShapeDtypeStruct((S5_W, D_MODEL), F32), jax.ShapeDtypeStruct((S5_W, S5_W), F32),
                   jax.ShapeDtypeStruct((8, D_MODEL), F32)),
        scratch_shapes=[pltpu.VMEM((D_MODEL, D_MODEL), F32), pltpu.VMEM((FOX_W, D_MODEL), F32),
                        pltpu.VMEM((S5_W, D_MODEL), F32), pltpu.VMEM((S5_W, S5_W), F32)],
        compiler_params=_cparams(("arbitrary",)),
    )(o, rest, rest, rest, rest, rest, ys0, x, tgt, *w, vec, hsel)


def _dh(dq, dk, dv, dga, dgb, dza, du, dzb, df, wqkv_t, wrest_t, x, dx2, gs):
    s = x.shape[0]
    tm = min(TM, s)

    def body(dq_ref, dk_ref, dv_ref, dga_ref, dgb_ref, dza_ref, du_ref, dzb_ref, df_ref, wq_ref, wr_ref,
             x_ref, dx2_ref, gs_ref, gx_ref, vout_ref):
        @pl.when(pl.program_id(0) == 0)
        def _():
            vout_ref[...] = jnp.zeros_like(vout_ref)

        dh = _dot(dq_ref[...], wq_ref[0:512, :])
        dh += _dot(dk_ref[...], wq_ref[512:1024, :])
        dh += _dot(dv_ref[...], wq_ref[1024:1536, :])
        dh += _dot(dga_ref[...], wr_ref[R_GA:R_GB, :])
        dh += _dot(dgb_ref[...], wr_ref[R_GB:R_ZA, :])
        dh += _dot(dza_ref[...], wr_ref[R_ZA:R_U, :])
        dh += _dot(du_ref[...], wr_ref[R_U:R_ZB, :])
        dh += _dot(dzb_ref[...], wr_ref[R_ZB:R_F, :])
        dh += _dot(df_ref[...], wr_ref[R_F:REST_W, :])
        xv = x_ref[...]
        r = lax.rsqrt(jnp.mean(xv * xv, axis=-1, keepdims=True) + EPS)
        xn = xv * r
        dxn = dh * gs_ref[...]
        gx_ref[...] = dx2_ref[...] + r * (dxn - xn * jnp.mean(dxn * xn, axis=-1, keepdims=True))
        vout_ref[0:1, :] += jnp.sum(dh * xn, axis=0, keepdims=True)
        vout_ref[1:2, :] += jnp.sum(dh, axis=0, keepdims=True)

    def rows(width):
        return pl.BlockSpec((tm, width), lambda i: (i, 0))

    return pl.pallas_call(
        body, name="dh", grid=(s // tm,),
        in_specs=[rows(512), rows(512), rows(512), rows(1024), rows(1024), rows(512), rows(512), rows(512), rows(128),
                  _const((1536, D_MODEL)), _const((REST_W, D_MODEL)), rows(D_MODEL), rows(D_MODEL), _const((1, D_MODEL))],
        out_specs=(rows(D_MODEL), pl.BlockSpec((8, D_MODEL), lambda i: (0, 0))),
        out_shape=(jax.ShapeDtypeStruct((s, D_MODEL), F32), jax.ShapeDtypeStruct((8, D_MODEL), F32)),
        compiler_params=_cparams(("arbitrary",)),
    )(dq, dk, dv, dga, dgb, dza, du, dzb, df, wqkv_t, wrest_t, x, dx2, gs)


def _row_block(rows, mult=8, cap=512):
    if rows <= mult:
        return rows
    padded = -(-rows // mult) * mult
    for cand in range(min(cap, padded) // mult * mult, 0, -mult):
        if padded % cand == 0:
            return cand
    return padded


def _sum4(parts, name):
    rows, cols = parts.shape[1:]
    br = _row_block(rows, 16)

    def body(p_ref, o_ref):
        acc = p_ref[0].astype(F32)
        for k in range(1, 4):
            acc = acc + p_ref[k].astype(F32)
        o_ref[...] = acc

    return pl.pallas_call(
        body, name=name, grid=(pl.cdiv(rows, br),),
        in_specs=[pl.BlockSpec((4, br, cols), lambda i: (0, i, 0))],
        out_specs=pl.BlockSpec((br, cols), lambda i: (i, 0)),
        out_shape=jax.ShapeDtypeStruct((rows, cols), F32), compiler_params=_cparams(("parallel",)))(parts)


def _pair_add(a, b, name):
    shape = a.shape
    a, b = a.reshape(-1, shape[-1]), b.reshape(-1, shape[-1])
    rows, cols = a.shape
    br = _row_block(rows, 16, 1024)

    def body(a_ref, b_ref, o_ref):
        o_ref[...] = (a_ref[...].astype(F32) + b_ref[...].astype(F32)).astype(BF16)

    spec = pl.BlockSpec((br, cols), lambda i: (i, 0))
    return pl.pallas_call(
        body, name=name, grid=(pl.cdiv(rows, br),), in_specs=[spec, spec], out_specs=spec,
        out_shape=jax.ShapeDtypeStruct((rows, cols), BF16), compiler_params=_cparams(("parallel",)))(a, b).reshape(shape)


def _adamw(w, g, m, v, name):
    rows, cols = w.shape
    br = _row_block(rows)

    def body(w_ref, g_ref, m_ref, v_ref, d_ref, nm_ref, nv_ref):
        gv = g_ref[...]
        nm = ADAM_B1 * m_ref[...] + (1.0 - ADAM_B1) * gv
        nv = ADAM_B2 * v_ref[...] + (1.0 - ADAM_B2) * (gv * gv)
        m_hat = nm / (1.0 - ADAM_B1 ** ADAM_STEP)
        v_hat = nv / (1.0 - ADAM_B2 ** ADAM_STEP)
        d_ref[...] = -ADAM_LR * (m_hat / (jnp.sqrt(v_hat) + ADAM_EPS) + ADAM_WD * w_ref[...])
        nm_ref[...] = nm
        nv_ref[...] = nv

    spec = pl.BlockSpec((br, cols), lambda i: (i, 0))
    shape = jax.ShapeDtypeStruct((rows, cols), F32)
    return pl.pallas_call(
        body, name=name, grid=(pl.cdiv(rows, br),), in_specs=[spec] * 4, out_specs=(spec,) * 3,
        out_shape=(shape,) * 3, compiler_params=_cparams(("parallel",)))(w, g, m, v)


def _pack(parts):
    flat = []
    for p in parts:
        v = p.reshape(-1).astype(F32)
        pad = (-v.shape[0]) % LANES
        if pad:
            v = jnp.concatenate([v, jnp.zeros((pad,), F32)])
        flat.append(v)
    v = jnp.concatenate(flat)
    rows = v.shape[0] // LANES
    pad_rows = (-rows) % 8
    if pad_rows:
        v = jnp.concatenate([v, jnp.zeros((pad_rows * LANES,), F32)])
    return v.reshape(-1, LANES)


def _unpack(packed, shapes):
    lead = packed.shape[:-2]
    flat = packed.reshape(lead + (-1,))
    out, off = [], 0
    for sh in shapes:
        size = math.prod(sh)
        out.append(flat[..., off:off + size].reshape(lead + tuple(sh)))
        off += size + (-size) % LANES
    return out


def kernel(x, c, w_ada, b_ada, g_norm, w_in, b_f, a_re, a_im, log_dt, b_re, b_im, c_re, c_im, d_skip, w_glu, b_glu, w_up_a, w_up_b, w_out, g_final, loss_target, m_w_ada, m_b_ada, m_g_norm, m_w_in, m_b_f, m_a_re, m_a_im, m_log_dt, m_b_re, m_b_im, m_c_re, m_c_im, m_d_skip, m_w_glu, m_b_glu, m_w_up_a, m_w_up_b, m_w_out, m_g_final, v_w_ada, v_b_ada, v_g_norm, v_w_in, v_b_f, v_a_re, v_a_im, v_log_dt, v_b_re, v_b_im, v_c_re, v_c_im, v_d_skip, v_w_glu, v_b_glu, v_w_up_a, v_w_up_b, v_w_out, v_g_final):
    xi, yi, ci = lax.axis_index("x"), lax.axis_index("y"), lax.axis_index("c")
    chip = 2 * xi + yi
    me = 4 * xi + 2 * yi + ci
    s = x.shape[1]
    x2d = x[0]
    tgt = loss_target[0]
    n_att = s // min(T_ATT, s)
    t_att = min(T_ATT, s)

    c_all, _ = _allgather8(c.reshape(8, LANES), "gather_c")
    c_all = c_all.reshape(8, D_MODEL)
    ncol = w_ada.shape[2]
    b_cols = lax.dynamic_slice_in_dim(b_ada, chip * ncol, ncol, axis=1)
    mod_cols = _mod_cols(c_all, w_ada[0], b_cols)
    mod_all, _ = _allgather8(mod_cols.reshape(-1, LANES), "gather_mod")
    mod_all = mod_all.reshape(4, 2, 8, ncol)[:, 0]
    mod_me = lax.dynamic_index_in_dim(mod_all, me, axis=1, keepdims=False).reshape(1, 3 * D_MODEL)
    shift, scale, gate = mod_me[:, :D_MODEL], mod_me[:, D_MODEL:2 * D_MODEL], mod_me[:, 2 * D_MODEL:]
    gs = g_norm * (1.0 + scale)

    nshard = w_in.shape[2]
    w_in_t, m_in_t, v_in_t = (jnp.swapaxes(a[0], 0, 1) for a in (w_in, m_w_in, v_w_in))
    wt_pack = jnp.pad(w_in_t.astype(BF16), ((0, SHARD_ROWS - nshard), (0, 0)))
    misc_shapes = [w_glu.shape[1:], w_up_a.shape[1:], w_up_b.shape[1:], w_out.shape[1:]]
    misc_pack = jnp.concatenate([w.reshape(-1) for w in (w_glu, w_up_a, w_up_b, w_out)]).astype(BF16).reshape(-1, LANES)
    def halves(a):
        return a.reshape((2, a.shape[0] // 2) + a.shape[1:])

    wt_all, misc_all = _gather_shards([halves(wt_pack), halves(misc_pack)], "gather_weights")
    wt_all = lax.dynamic_update_index_in_dim(wt_all, halves(wt_pack), chip, 0).reshape((4,) + wt_pack.shape)
    misc_all = lax.dynamic_update_index_in_dim(misc_all, halves(misc_pack), chip, 0).reshape((4,) + misc_pack.shape)
    p_glu, p_ua, p_ub, p_out = _unpack(misc_all, misc_shapes)

    def w_rows(lo, hi):
        out = []
        for j in range(4):
            a, b = max(lo, j * nshard), min(hi, (j + 1) * nshard)
            if a < b:
                out.append(wt_all[j, a - j * nshard:b - j * nshard])
        return out

    wqkv_t = jnp.concatenate(w_rows(O_Q, O_F), axis=0)
    wrest_t = jnp.concatenate(w_rows(O_GA, O_GB) + w_rows(O_GB, O_END) + w_rows(O_ZA, O_U) + w_rows(O_U, O_ZB)
                              + w_rows(O_ZB, O_GA) + w_rows(O_F, O_ZA)
                              + [jnp.zeros((REST_W - R_F - HEADS, D_MODEL), BF16)], axis=0)
    wmid = (p_glu.reshape(S5_W, S5_W), jnp.concatenate([p_ua[j] for j in range(4)], axis=1),
            jnp.concatenate([p_ub[j] for j in range(4)], axis=1), p_out.reshape(D_MODEL, D_MODEL))

    h = _prenorm(x2d, gs, shift)
    qkv = _mm_rows_nt(h, wqkv_t, BF16, "proj_qkv")
    rest = _mm_rows_nt(h, wrest_t, F32, "proj_rest")
    bf128 = jnp.pad(b_f, ((0, 0), (0, LANES - HEADS)))
    selp = _head_pair_selector()
    fpc, f_t = _fcum(rest, bf128, selp)
    frow5 = f_t.reshape(4, 2, n_att, 1, t_att)
    o, lse_pc = _attn_fwd(qkv, frow5, fpc)

    abar_r, abar_i, bb_r, bb_i = _ssm_block_params(a_re[0], a_im[0], log_dt[0], b_re[0], b_im[0])
    bb_rt, bb_it = jnp.swapaxes(bb_r, 1, 2).astype(BF16), jnp.swapaxes(bb_i, 1, 2).astype(BF16)
    cr_b, ci_b = c_re[0].astype(BF16), (-c_im[0]).astype(BF16)
    bd_c, cd_c = _compact_diag(bb_rt, bb_it), _compact_diag(cr_b, ci_b)
    ys0, xs = _ssm_fwd(rest, bd_c, cd_c, _scan_consts(a_re[0], a_im[0], log_dt[0], False))

    vec = jnp.concatenate([gate, g_final.reshape(1, D_MODEL), jnp.concatenate([d_skip, b_glu], axis=1),
                           jnp.zeros((5, D_MODEL), F32)], axis=0)
    hsel = jnp.repeat(jnp.eye(HEADS, dtype=F32), HEAD_DIM, axis=1)
    (dx2, dga, dgb, do, dza, dzb, dys, dlt_t, g_out, g_ua, g_ub, g_glu, vmid) = _mid(
        o, rest, ys0, x2d, tgt, wmid, vec, hsel)

    lse_t = jnp.transpose(lse_pc.reshape(s, 4, LANES)[:, :, :2], (1, 2, 0))
    lse5 = lse_t.reshape(4, 2, n_att, 1, t_att)
    dlt5 = dlt_t.reshape(4, 2, n_att, 1, t_att)
    dq, dk, dv, dfk, dfq = _attn_bwd(qkv, do, lse5, dlt5, frow5, fpc)
    du, g_bd, g_cdt, da8 = _ssm_bwd(dys, xs, rest, bd_c, cd_c, _scan_consts(a_re[0], a_im[0], log_dt[0], True), d_skip)
    df, dbf8 = _dfcum(dfk, dfq, rest, bf128, selp.T)

    grad_x, vdh = _dh(dq, dk, dv, dga, dgb, dza, du, dzb, df, wqkv_t, wrest_t, x2d, dx2, gs)
    gq, gk, gv, gga, ggb, gza, gu, gzb, gf = _grad_w_rows(h, [dq, dk, dv, dga, dgb, dza, du, dzb, df])
    g_in_t = jnp.concatenate([gq, gk, gv, gf[:HEADS], gza, gu, gzb, gga, ggb], axis=0)

    dgs, dshift = vdh[0:1], vdh[1:2]
    dmod = jnp.concatenate([dshift, dgs * g_norm, vmid[1:2]], axis=1)
    da = jnp.sum(da8, axis=0)
    g_bd = g_bd.reshape(GROUPS, GCH, 2 * STATE)
    g_cdt = g_cdt.reshape(GROUPS, GCH, 2 * STATE)
    g_bbr = jnp.swapaxes(g_bd[:, :, :STATE], 1, 2)
    g_bbi = jnp.swapaxes(g_bd[:, :, STATE:], 1, 2)
    g_cre = g_cdt[:, :, :STATE]
    g_cim = -g_cdt[:, :, STATE:]
    small_shapes = [(1,), (3 * D_MODEL,), (D_MODEL,), (HEADS,), (GROUPS, STATE), (GROUPS, STATE),
                    (GROUPS, STATE, GCH), (GROUPS, STATE, GCH), (GROUPS, GCH, STATE), (GROUPS, GCH, STATE),
                    (S5_W,), (S5_W,), (D_MODEL,)]
    small = _pack([vmid[3, 0:1], dmod, dgs * (1.0 + scale), dbf8[0, :HEADS], da[:NSTATE], da[NSTATE:],
                   g_bbr, g_bbi, g_cre, g_cim, vmid[2, :S5_W], vmid[2, S5_W:], vmid[0]])
    small_all, small_sum = _allgather8(small, "gather_small_grads")
    (loss_s, g_b_ada, g_g_norm, g_b_f, g_abr, g_abi, g_bbr_s, g_bbi_s, g_c_re, g_c_im, g_d_skip, g_b_glu,
     g_g_final) = _unpack(small_sum, small_shapes)
    loss = loss_s[0]
    dmod_all = _unpack(small_all, small_shapes)[1]
    dmod_cols = lax.dynamic_slice_in_dim(dmod_all, chip * ncol, ncol, axis=1)
    g_w_ada = _grad_w_ada(c_all, dmod_cols)
    _, ssm_vjp = jax.vjp(_ssm_block_params, a_re[0], a_im[0], log_dt[0], b_re[0], b_im[0])
    g_a_re, g_a_im, g_log_dt, g_b_re, g_b_im = ssm_vjp((g_abr, g_abi, g_bbr_s, g_bbi_s))

    def shard_cols(g, j):
        n = g.shape[1] // 4
        return g[:, j * n:(j + 1) * n]


    def shard_rows(g, j):
        n = g.shape[0] // 4
        return g[j * n:(j + 1) * n]

    gt_pack = jnp.stack([
        jnp.pad(g_in_t[j * nshard:(j + 1) * nshard].astype(BF16), ((0, SHARD_ROWS - nshard), (0, 0)))
        for j in range(4)])
    gm_pack = jnp.stack([
        jnp.concatenate([shard_rows(g_glu, j).reshape(-1), shard_cols(g_ua, j).reshape(-1),
                         shard_cols(g_ub, j).reshape(-1), shard_rows(g_out, j).reshape(-1)]).astype(BF16)
        .reshape(-1, LANES) for j in range(4)])
    def halves4(a):
        return a.reshape((4, 2, a.shape[1] // 2) + a.shape[2:])

    gt_pack, gm_pack = halves4(gt_pack), halves4(gm_pack)
    recv_in, recv_misc = _swap_sibling([gt_pack, gm_pack], "pair_swap_weight_grads", other_half=True)
    own_in = lax.dynamic_index_in_dim(gt_pack, ci, axis=1, keepdims=False)
    own_misc = lax.dynamic_index_in_dim(gm_pack, ci, axis=1, keepdims=False)
    pair_in = _pair_add(own_in, recv_in, "pair_add_w_in")
    pair_misc = _pair_add(own_misc, recv_misc, "pair_add_misc")
    parts_in, parts_misc = _scatter4([pair_in, pair_misc], "scatter_weight_grads")
    parts_in = lax.dynamic_update_slice_in_dim(parts_in, lax.dynamic_slice_in_dim(pair_in, chip, 1, 0), chip, 0)
    parts_misc = lax.dynamic_update_slice_in_dim(parts_misc, lax.dynamic_slice_in_dim(pair_misc, chip, 1, 0), chip, 0)
    half_in, half_misc = _sum4(parts_in, "sum4_w_in"), _sum4(parts_misc, "sum4_misc")
    sib_in, sib_misc = _swap_sibling([half_in, half_misc], "swap_weight_grads")

    def both_halves(mine, theirs):
        return jnp.concatenate([jnp.where(ci == 0, mine, theirs), jnp.where(ci == 0, theirs, mine)], axis=0)

    tot_in, tot_misc = both_halves(half_in, sib_in), both_halves(half_misc, sib_misc)
    g_glu_s, g_ua_s, g_ub_s, g_out_s = _unpack(tot_misc, misc_shapes)

    def adam(name, w, g, m, v):
        shape = w.shape
        total = math.prod(shape)
        if len(shape) > 1 and shape[-1] >= LANES:
            cols = shape[-1]
        elif total % LANES == 0:
            cols = LANES
        else:
            cols = total
        two = lambda a: a.reshape(-1, cols)
        d, nm, nv = _adamw(two(w), two(g), two(m), two(v), "adamw_" + name)
        return g.reshape(shape), d.reshape(shape), nm.reshape(shape), nv.reshape(shape)

    back = lambda a: jnp.swapaxes(a, 0, 1)[None]
    d_in_t, nm_in_t, nv_in_t = _adamw(w_in_t, tot_in, m_in_t, v_in_t, "adamw_w_in")
    res_w_in = (back(tot_in[:nshard]), back(d_in_t), back(nm_in_t), back(nv_in_t))

    res = [
        adam("w_ada", w_ada, g_w_ada, m_w_ada, v_w_ada),
        adam("b_ada", b_ada, g_b_ada, m_b_ada, v_b_ada),
        adam("g_norm", g_norm, g_g_norm, m_g_norm, v_g_norm),
        res_w_in,
        adam("b_f", b_f, g_b_f, m_b_f, v_b_f),
        adam("a_re", a_re, g_a_re, m_a_re, v_a_re),
        adam("a_im", a_im, g_a_im, m_a_im, v_a_im),
        adam("log_dt", log_dt, g_log_dt, m_log_dt, v_log_dt),
        adam("b_re", b_re, g_b_re, m_b_re, v_b_re),
        adam("b_im", b_im, g_b_im, m_b_im, v_b_im),
        adam("c_re", c_re, g_c_re, m_c_re, v_c_re),
        adam("c_im", c_im, g_c_im, m_c_im, v_c_im),
        adam("d_skip", d_skip, g_d_skip, m_d_skip, v_d_skip),
        adam("w_glu", w_glu, g_glu_s, m_w_glu, v_w_glu),
        adam("b_glu", b_glu, g_b_glu, m_b_glu, v_b_glu),
        adam("w_up_a", w_up_a, g_ua_s, m_w_up_a, v_w_up_a),
        adam("w_up_b", w_up_b, g_ub_s, m_w_up_b, v_w_up_b),
        adam("w_out", w_out, g_out_s, m_w_out, v_w_out),
        adam("g_final", g_final, g_g_final, m_g_final, v_g_final),
    ]
    grads = [r[0] for r in res]
    deltas = [r[1] for r in res]
    new_m = [r[2] for r in res]
    new_v = [r[3] for r in res]
    return (loss, grad_x[None], *grads, *deltas, *new_m, *new_v)
```

```python
import functools
import math

import jax
import jax.numpy as jnp
from jax import lax
from jax.experimental import pallas as pl
from jax.experimental.pallas import tpu as pltpu

F32 = jnp.float32
BF16 = jnp.bfloat16
HI = lax.Precision.HIGHEST
MESH = pl.DeviceIdType.MESH

D_MODEL = 1024
HEADS = 8
HEAD_DIM = 64
FOX_W = 512
S5_W = 512
GROUPS = 32
STATE = 64
GCH = 16
NSTATE = GROUPS * STATE
EPS = 1e-6
NEG = -1e30

ADAM_LR = 0.001
ADAM_B1 = 0.9
ADAM_B2 = 0.999
ADAM_EPS = 1e-08
ADAM_WD = 0.01
ADAM_STEP = 10

VMEM_LIMIT = 56 * 1024 * 1024
LANES = 128

TM = 256
T_ATT = 512
ATT_CHUNK = 32
ATT_PAIRS = 4
TB_SSM = 256
TK_ACC = 512
TB_CUM = 512
SHARD_ROWS = 1312

O_Q, O_K, O_V, O_F, O_ZA, O_U, O_ZB, O_GA, O_GB, O_END = 0, 512, 1024, 1536, 1544, 2056, 2568, 3080, 4104, 5128
REST_W = 3712
R_GA, R_GB, R_ZA, R_U, R_ZB, R_F = 0, 1024, 2048, 2560, 3072, 3584


def _cparams(sem=None):
    kw = dict(vmem_limit_bytes=VMEM_LIMIT)
    if sem is not None:
        kw["dimension_semantics"] = sem
    return pltpu.CompilerParams(**kw)


def _const(shape):
    nd = len(shape)
    return pl.BlockSpec(shape, lambda *_: (0,) * nd, pipeline_mode=pl.Buffered(1))


def _dot(a, b, precision=None):
    return jnp.dot(a, b, preferred_element_type=F32, precision=precision)


def _dot_nt(a, b):
    return lax.dot_general(a, b, (((1,), (1,)), ((), ())), preferred_element_type=F32)


def _dot_tn(a, b, precision=None):
    return lax.dot_general(a, b, (((0,), (0,)), ((), ())), preferred_element_type=F32, precision=precision)


def _sigmoid(z):
    return 1.0 / (1.0 + jnp.exp(-z))


def _allgather8(xs, name):
    rows = xs.shape[0]

    def body(x_ref, out_ref, sum_ref, send_sems, recv_sems, local_sem):
        x, y, c = lax.axis_index("x"), lax.axis_index("y"), lax.axis_index("c")
        me, sibling = (x, y, c), (x, y, 1 - c)
        chips = [(1 - x, y), (x, 1 - y), (1 - x, 1 - y)]

        def slot(px, py, pc):
            return out_ref.at[4 * px + 2 * py + pc]

        def copy(k, block, to, src=None):
            return pltpu.make_async_remote_copy(
                src_ref=slot(*block) if src is None else src, dst_ref=slot(*block),
                send_sem=send_sems.at[k], recv_sem=recv_sems.at[k], device_id=to, device_id_type=MESH)

        mine = pltpu.make_async_copy(x_ref, slot(*me), local_sem)
        mine.start()
        first = [copy(0, me, sibling, src=x_ref)]
        first += [copy(1 + j, me, (*chip, c), src=x_ref) for j, chip in enumerate(chips)]
        for cp in first:
            cp.start()
        passed = [copy(4 + j, (*chip, c), sibling) for j, chip in enumerate(chips)]
        for j, chip in enumerate(chips):
            copy(1 + j, (*chip, c), me).wait_recv()
            passed[j].start()
        copy(0, sibling, me).wait_recv()
        for j, chip in enumerate(chips):
            copy(4 + j, (*chip, 1 - c), me).wait_recv()
        for cp in first + passed:
            cp.wait_send()
        mine.wait()
        acc = out_ref[0]
        for d in range(1, 8):
            acc = acc + out_ref[d]
        sum_ref[...] = acc

    return pl.pallas_call(
        body, name=name,
        out_shape=(jax.ShapeDtypeStruct((8, rows, LANES), F32), jax.ShapeDtypeStruct((rows, LANES), F32)),
        in_specs=[pl.BlockSpec(memory_space=pltpu.VMEM)],
        out_specs=(pl.BlockSpec(memory_space=pltpu.VMEM), pl.BlockSpec(memory_space=pltpu.VMEM)),
        scratch_shapes=[pltpu.SemaphoreType.DMA((7,)), pltpu.SemaphoreType.DMA((7,)), pltpu.SemaphoreType.DMA],
        compiler_params=_cparams(),
    )(xs)


def _scatter4(srcs, name):
    na = len(srcs)

    def body(*refs):
        src_refs, out_refs = refs[:na], refs[na:2 * na]
        send_sems, recv_sems = refs[2 * na:]
        x, y, c = lax.axis_index("x"), lax.axis_index("y"), lax.axis_index("c")
        peers = [(1 - x, y), (x, 1 - y), (1 - x, 1 - y)]

        def copy(a, k, px, py, slot):
            return pltpu.make_async_remote_copy(
                src_ref=src_refs[a].at[2 * px + py], dst_ref=out_refs[a].at[slot],
                send_sem=send_sems.at[a * 3 + k], recv_sem=recv_sems.at[a * 3 + k],
                device_id=(px, py, c), device_id_type=MESH)

        sends = [copy(a, k, px, py, 2 * x + y) for a in range(na) for k, (px, py) in enumerate(peers)]
        for cp in sends:
            cp.start()
        for a in range(na):
            for k, (px, py) in enumerate(peers):
                copy(a, k, px, py, 2 * px + py).wait_recv()
        for cp in sends:
            cp.wait_send()

    anyspace = pl.BlockSpec(memory_space=pl.ANY)
    return pl.pallas_call(
        body, name=name,
        out_shape=tuple(jax.ShapeDtypeStruct(a.shape, a.dtype) for a in srcs),
        in_specs=[anyspace] * na, out_specs=(anyspace,) * na,
        scratch_shapes=[pltpu.SemaphoreType.DMA((3 * na,)), pltpu.SemaphoreType.DMA((3 * na,))],
        compiler_params=_cparams(),
    )(*srcs)


def _gather_shards(srcs, name):
    na = len(srcs)

    def body(*refs):
        src_refs, out_refs = refs[:na], refs[na:2 * na]
        send_sems, recv_sems = refs[2 * na:]
        x, y, c = lax.axis_index("x"), lax.axis_index("y"), lax.axis_index("c")
        sibling = (x, y, 1 - c)
        peers = [(1 - x, y), (x, 1 - y), (1 - x, 1 - y)]

        def copy(a, k, src, slot, which, to):
            return pltpu.make_async_remote_copy(
                src_ref=src, dst_ref=out_refs[a].at[slot, which],
                send_sem=send_sems.at[a * 6 + k], recv_sem=recv_sems.at[a * 6 + k],
                device_id=to, device_id_type=MESH)

        mine = 2 * x + y
        first = [copy(a, k, src_refs[a].at[c], mine, c, (px, py, c))
                 for a in range(na) for k, (px, py) in enumerate(peers)]
        for cp in first:
            cp.start()
        passed = []
        for a in range(na):
            for k, (px, py) in enumerate(peers):
                slot = 2 * px + py
                landed = out_refs[a].at[slot, c]
                copy(a, k, landed, slot, c, (px, py, c)).wait_recv()
                fwd = copy(a, 3 + k, landed, slot, c, sibling)
                fwd.start()
                passed.append(fwd)
        for a in range(na):
            for k, (px, py) in enumerate(peers):
                slot = 2 * px + py
                copy(a, 3 + k, out_refs[a].at[slot, 1 - c], slot, 1 - c, sibling).wait_recv()
        for cp in first + passed:
            cp.wait_send()

    anyspace = pl.BlockSpec(memory_space=pl.ANY)
    return pl.pallas_call(
        body, name=name,
        out_shape=tuple(jax.ShapeDtypeStruct((4,) + tuple(a.shape), a.dtype) for a in srcs),
        in_specs=[anyspace] * na, out_specs=(anyspace,) * na,
        scratch_shapes=[pltpu.SemaphoreType.DMA((6 * na,)), pltpu.SemaphoreType.DMA((6 * na,))],
        compiler_params=_cparams(),
    )(*srcs)


def _swap_sibling(srcs, name, other_half=False):
    na = len(srcs)

    def body(*refs):
        src_refs, out_refs = refs[:na], refs[na:2 * na]
        send_sems, recv_sems = refs[2 * na:]
        x, y, c = lax.axis_index("x"), lax.axis_index("y"), lax.axis_index("c")
        copies = [pltpu.make_async_remote_copy(
            src_ref=src_refs[a].at[:, 1 - c] if other_half else src_refs[a],
            dst_ref=out_refs[a], send_sem=send_sems.at[a], recv_sem=recv_sems.at[a],
            device_id=(x, y, 1 - c), device_id_type=MESH) for a in range(na)]
        for cp in copies:
            cp.start()
        for cp in copies:
            cp.wait()

    def out_of(a):
        shape = (a.shape[0],) + tuple(a.shape[2:]) if other_half else a.shape
        return jax.ShapeDtypeStruct(shape, a.dtype)

    anyspace = pl.BlockSpec(memory_space=pl.ANY)
    return pl.pallas_call(
        body, name=name, out_shape=tuple(out_of(a) for a in srcs),
        in_specs=[anyspace] * na, out_specs=(anyspace,) * na,
        scratch_shapes=[pltpu.SemaphoreType.DMA((na,)), pltpu.SemaphoreType.DMA((na,))],
        compiler_params=_cparams(),
    )(*srcs)


def _mod_cols(c_all, w, b):
    n = w.shape[1]

    def body(c_ref, w_ref, b_ref, o_ref):
        o_ref[...] = _dot(c_ref[...], w_ref[...], HI) + b_ref[...]

    return pl.pallas_call(
        body, name="mod_cols", out_shape=jax.ShapeDtypeStruct((8, n), F32),
        compiler_params=_cparams())(c_all, w, b)


def _grad_w_ada(c_all, dmod_cols):
    n = dmod_cols.shape[1]

    def body(c_ref, d_ref, o_ref):
        o_ref[...] = _dot_tn(c_ref[...], d_ref[...], HI)

    return pl.pallas_call(
        body, name="grad_w_ada", out_shape=jax.ShapeDtypeStruct((D_MODEL, n), F32),
        compiler_params=_cparams())(c_all, dmod_cols)


def _prenorm_proj(x, gs, shift, wqkv_t, wrest_t):
    s = x.shape[0]
    tm = min(TM, s)
    nq, nr = wqkv_t.shape[0], wrest_t.shape[0]

    def body(x_ref, gs_ref, sh_ref, wq_ref, wr_ref, h_ref, qkv_ref, rest_ref):
        xv = x_ref[...]
        r = lax.rsqrt(jnp.mean(xv * xv, axis=-1, keepdims=True) + EPS)
        h = (xv * r * gs_ref[...] + sh_ref[...]).astype(BF16)
        h_ref[...] = h
        qkv_ref[...] = _dot_nt(h, wq_ref[...]).astype(BF16)
        rest_ref[...] = _dot_nt(h, wr_ref[...])

    def rows(width):
        return pl.BlockSpec((tm, width), lambda i: (i, 0))

    return pl.pallas_call(
        body, name="prenorm_proj", grid=(s // tm,),
        in_specs=[rows(D_MODEL), _const((1, D_MODEL)), _const((1, D_MODEL)), _const((nq, D_MODEL)),
                  _const((nr, D_MODEL))],
        out_specs=(rows(D_MODEL), rows(nq), rows(nr)),
        out_shape=(jax.ShapeDtypeStruct((s, D_MODEL), BF16), jax.ShapeDtypeStruct((s, nq), BF16),
                   jax.ShapeDtypeStruct((s, nr), F32)),
        compiler_params=_cparams(("parallel",)))(x, gs, shift, wqkv_t, wrest_t)


def _grad_w_rows(h, ds):
    s = h.shape[0]
    tk = min(TK_ACC, s)
    nd = len(ds)
    widths = [d.shape[1] for d in ds]

    def body(*refs):
        h_ref, d_refs = refs[0], refs[1:1 + nd]
        out_refs, accs = refs[1 + nd:1 + 2 * nd], refs[1 + 2 * nd:]
        step = pl.program_id(0)

        @pl.when(step == 0)
        def _():
            for acc in accs:
                acc[...] = jnp.zeros_like(acc)

        hv = h_ref[...]
        for d_ref, acc in zip(d_refs, accs):
            acc[...] += _dot_tn(d_ref[...], hv)

        @pl.when(step == s // tk - 1)
        def _():
            for acc, out in zip(accs, out_refs):
                pltpu.sync_copy(acc, out)

    anyspace = pl.BlockSpec(memory_space=pl.ANY)
    return pl.pallas_call(
        body, name="grad_w_in", grid=(s // tk,),
        in_specs=[pl.BlockSpec((tk, D_MODEL), lambda k: (k, 0))]
                 + [pl.BlockSpec((tk, w), lambda k: (k, 0)) for w in widths],
        out_specs=(anyspace,) * nd,
        out_shape=tuple(jax.ShapeDtypeStruct((w, D_MODEL), F32) for w in widths),
        scratch_shapes=[pltpu.VMEM((w, D_MODEL), F32) for w in widths],
        compiler_params=_cparams(("arbitrary",)))(h, *ds)


def _head_pair_selector():
    rows = jnp.arange(LANES)[:, None]
    cols = jnp.arange(4 * LANES)[None, :]
    return ((rows < HEADS) & (cols == (rows // 2) * LANES + rows % 2)).astype(F32)


def _fcum(rest, bf128, selp):
    s = rest.shape[0]
    tb = min(TB_CUM, s)

    def body(fz_ref, bf_ref, sel_ref, fpc_ref, ft_ref, carry_ref):
        @pl.when(pl.program_id(0) == 0)
        def _():
            carry_ref[...] = jnp.zeros_like(carry_ref)

        z = fz_ref[...] + bf_ref[...]
        logf = jnp.minimum(z, 0.0) - jnp.log(1.0 + jnp.exp(-jnp.abs(z)))
        r = lax.broadcasted_iota(jnp.int32, (tb, tb), 0)
        c = lax.broadcasted_iota(jnp.int32, (tb, tb), 1)
        tri = (c <= r).astype(F32)
        f = _dot(tri, logf, HI) + carry_ref[0:1, :]
        carry_ref[0:1, :] = f[tb - 1:tb, :]
        fpc_ref[...] = _dot(f, sel_ref[...], HI)
        ft_ref[...] = jnp.transpose(f)[0:HEADS, :]

    return pl.pallas_call(
        body, name="forget_cumsum", grid=(s // tb,),
        in_specs=[pl.BlockSpec((tb, LANES), lambda i: (i, R_F // LANES)), _const((1, LANES)), _const((LANES, 4 * LANES))],
        out_specs=(pl.BlockSpec((tb, 4 * LANES), lambda i: (i, 0)), pl.BlockSpec((HEADS, tb), lambda i: (0, i))),
        out_shape=(jax.ShapeDtypeStruct((s, 4 * LANES), F32), jax.ShapeDtypeStruct((HEADS, s), F32)),
        scratch_shapes=[pltpu.VMEM((8, LANES), F32)],
        compiler_params=_cparams(("arbitrary",)))(rest, bf128, selp)


def _dfcum(dfk, dfq, rest, bf128, selq):
    s = rest.shape[0]
    tb = min(TB_CUM, s)
    nb = s // tb

    def body(dk_ref, dq_ref, fz_ref, bf_ref, sel_ref, df_ref, dbf_ref, carry_ref):
        @pl.when(pl.program_id(0) == 0)
        def _():
            carry_ref[...] = jnp.zeros_like(carry_ref)
            dbf_ref[...] = jnp.zeros_like(dbf_ref)

        d = _dot(dk_ref[...] + dq_ref[...], sel_ref[...], HI)
        r = lax.broadcasted_iota(jnp.int32, (tb, tb), 0)
        c = lax.broadcasted_iota(jnp.int32, (tb, tb), 1)
        triu = (c >= r).astype(F32)
        dlogf = _dot(triu, d, HI) + carry_ref[0:1, :]
        carry_ref[0:1, :] = dlogf[0:1, :]
        z = fz_ref[...] + bf_ref[...]
        df = dlogf * (1.0 / (1.0 + jnp.exp(z)))
        df_ref[...] = df.astype(BF16)
        dbf_ref[0:1, :] += jnp.sum(df, axis=0, keepdims=True)

    return pl.pallas_call(
        body, name="forget_grad", grid=(nb,),
        in_specs=[pl.BlockSpec((tb, 4 * LANES), lambda i: (nb - 1 - i, 0)),
                  pl.BlockSpec((tb, 4 * LANES), lambda i: (nb - 1 - i, 0)),
                  pl.BlockSpec((tb, LANES), lambda i: (nb - 1 - i, R_F // LANES)),
                  _const((1, LANES)), _const((4 * LANES, LANES))],
        out_specs=(pl.BlockSpec((tb, LANES), lambda i: (nb - 1 - i, 0)), pl.BlockSpec((8, LANES), lambda i: (0, 0))),
        out_shape=(jax.ShapeDtypeStruct((s, LANES), BF16), jax.ShapeDtypeStruct((8, LANES), F32)),
        scratch_shapes=[pltpu.VMEM((8, LANES), F32)],
        compiler_params=_cparams(("arbitrary",)))(dfk, dfq, rest, bf128, selq)


def _scaled(q):
    return (q.astype(F32) * (HEAD_DIM ** -0.5)).astype(BF16)


def _attn_fwd(qkv, frow5, fpc):
    s = qkv.shape[0]
    t = min(T_ATT, s)
    n = s // t
    ch = min(ATT_CHUNK, t)
    wide = 2 * LANES
    pairs = ATT_PAIRS
    width = pairs * LANES
    groups = 4 // pairs

    def body(q_ref, k_ref, v_ref, fr_ref, fc_ref, o_ref, lse_ref, s_scr, p_scr, m_scr, a_scr, fq_scr, acc_scr):
        i = pl.program_id(1)
        lane = lax.broadcasted_iota(jnp.int32, (t, LANES), 1)
        first = lane < HEAD_DIM
        ones_col = ((lane == 0).astype(BF16), (lane == 1).astype(BF16))
        m_scr[...] = jnp.full(m_scr.shape, NEG, F32)
        acc_scr[...] = jnp.zeros_like(acc_scr)
        qm = []
        for pp in range(pairs):
            q = _scaled(q_ref[:, pp * LANES:(pp + 1) * LANES])
            zq = jnp.zeros_like(q)
            qm += [jnp.where(first, q, zq), jnp.where(first, zq, q)]
            fq_scr[2 * pp] = fc_ref[:, pp * LANES:pp * LANES + 1]
            fq_scr[2 * pp + 1] = fc_ref[:, pp * LANES + 1:pp * LANES + 2]

        def step(j, masked):
            r0 = pl.multiple_of(j * t, t)
            vaug = []
            for pp in range(pairs):
                kb = k_ref[pl.ds(r0, t), pp * LANES:(pp + 1) * LANES]
                vb = v_ref[pl.ds(r0, t), pp * LANES:(pp + 1) * LANES]
                zv = jnp.zeros_like(vb)
                vaug += [jnp.concatenate([jnp.where(first, vb, zv), ones_col[0]], axis=1),
                         jnp.concatenate([jnp.where(first, zv, vb), ones_col[1]], axis=1)]
                for hh in range(2):
                    s_scr[2 * pp + hh] = _dot_nt(qm[2 * pp + hh], kb)
            pv = []
            for hd in range(2 * pairs):
                fk = fr_ref[hd // 2, hd % 2, j]
                for c in range(t // ch):
                    rows = pl.ds(c * ch, ch)
                    hi = min(t, (c * ch // LANES + 1) * LANES) if masked else t
                    sc = s_scr[hd, rows, 0:hi] - fk[:, 0:hi]
                    if masked:
                        rq = c * ch + lax.broadcasted_iota(jnp.int32, (ch, hi), 0)
                        ck = lax.broadcasted_iota(jnp.int32, (ch, hi), 1)
                        sc = jnp.where(ck <= rq, sc, NEG)
                    fq = fq_scr[hd, rows, :]
                    m_old = m_scr[hd, rows, :]
                    m_new = jnp.maximum(m_old, fq + jnp.max(sc, axis=1, keepdims=True))
                    p_scr[hd, rows, 0:hi] = jnp.exp(sc + (fq - m_new)).astype(BF16)
                    if hi < t:
                        p_scr[hd, rows, hi:t] = jnp.zeros((ch, t - hi), BF16)
                    a_scr[hd, rows, :] = jnp.exp(m_old - m_new)
                    m_scr[hd, rows, :] = m_new
                pv.append(_dot(p_scr[hd], vaug[hd]))
            for pp in range(pairs):
                a0, a1 = a_scr[2 * pp], a_scr[2 * pp + 1]
                alpha = jnp.concatenate([jnp.where(first, a0, a1), jnp.where(lane == 0, a0, a1)], axis=1)
                acc_scr[pp] = acc_scr[pp] * alpha + pv[2 * pp] + pv[2 * pp + 1]
            return 0

        lax.fori_loop(0, i, lambda j, _: step(j, False), 0)
        step(i, True)
        lse = jnp.zeros((t, LANES), F32)
        for pp in range(pairs):
            l0 = acc_scr[pp, :, LANES:LANES + 1]
            l1 = acc_scr[pp, :, LANES + 1:LANES + 2]
            o_ref[:, pp * LANES:(pp + 1) * LANES] = acc_scr[pp, :, 0:LANES] * jnp.where(first, 1.0 / l0, 1.0 / l1)
            lse = jnp.where(lane == 2 * pp, m_scr[2 * pp] + jnp.log(l0), lse)
            lse = jnp.where(lane == 2 * pp + 1, m_scr[2 * pp + 1] + jnp.log(l1), lse)
        lse_ref[...] = lse

    blk = pl.BlockSpec((t, width), lambda g, i: (i, g))
    return pl.pallas_call(
        body, name="attn_fwd", grid=(groups, n),
        in_specs=[blk,
                  pl.BlockSpec((s, width), lambda g, i: (0, groups + g)),
                  pl.BlockSpec((s, width), lambda g, i: (0, 2 * groups + g)),
                  pl.BlockSpec((pairs, 2, n, 1, t), lambda g, i: (g, 0, 0, 0, 0)),
                  blk],
        out_specs=(blk, pl.BlockSpec((t, LANES), lambda g, i: (i, g))),
        out_shape=(jax.ShapeDtypeStruct((s, FOX_W), F32), jax.ShapeDtypeStruct((s, groups * LANES), F32)),
        scratch_shapes=[pltpu.VMEM((2 * pairs, t, t), F32), pltpu.VMEM((2 * pairs, t, t), BF16),
                        pltpu.VMEM((2 * pairs, t, 1), F32), pltpu.VMEM((2 * pairs, t, 1), F32),
                        pltpu.VMEM((2 * pairs, t, 1), F32), pltpu.VMEM((pairs, t, wide), F32)],
        compiler_params=_cparams(("parallel", "arbitrary")))(qkv, qkv, qkv, frow5, fpc)


def _attn_bwd(qkv, do, lse5, dlt5, frow5, fpc):
    s = qkv.shape[0]
    t = min(T_ATT, s)
    n = s // t
    wide = 2 * LANES

    ch = min(ATT_CHUNK, t)

    def body(q_ref, do_ref, k_ref, v_ref, lse_ref, dl_ref, fr_ref, fc_ref,
             dq_ref, dk_ref, dv_ref, dfk_ref, dfq_ref, dq_acc, st_scr, dp_scr, pt_scr, ds_scr, dk_acc, dv_acc, fk_scr):
        j = pl.program_id(1)

        @pl.when(j == 0)
        def _():
            dq_acc[...] = jnp.zeros_like(dq_acc)

        dk_acc[...] = jnp.zeros_like(dk_acc)
        dv_acc[...] = jnp.zeros_like(dv_acc)
        lane = lax.broadcasted_iota(jnp.int32, (t, LANES), 1)
        first = lane < HEAD_DIM
        ones_col = ((lane == 0).astype(BF16), (lane == 1).astype(BF16))
        kb = k_ref[...]
        vb = v_ref[...]
        zk = jnp.zeros_like(kb)
        kaug = (jnp.concatenate([jnp.where(first, kb, zk), ones_col[0]], axis=1),
                jnp.concatenate([jnp.where(first, zk, kb), ones_col[1]], axis=1))
        fk_scr[0] = fc_ref[:, 0:1]
        fk_scr[1] = fc_ref[:, 1:2]

        def step(i, masked):
            r0 = pl.multiple_of(i * t, t)
            qb = _scaled(q_ref[pl.ds(r0, t), :])
            dob = do_ref[pl.ds(r0, t), :]
            zq = jnp.zeros_like(qb)
            qm = (jnp.where(first, qb, zq), jnp.where(first, zq, qb))
            dom = (jnp.where(first, dob, zq), jnp.where(first, zq, dob))
            dq_add = jnp.zeros((t, wide), F32)
            for hh in range(2):
                st_scr[hh] = _dot_nt(kb, qm[hh])
                dp_scr[hh] = _dot_nt(vb, dom[hh])
                bias = fr_ref[0, hh, i] - lse_ref[0, hh, i]
                dl = dl_ref[0, hh, i]
                for c in range(t // ch):
                    rows = pl.ds(c * ch, ch)
                    lo = c * ch // LANES * LANES if masked else 0
                    st = st_scr[hh, rows, lo:t] + (bias[:, lo:t] - fk_scr[hh, rows, :])
                    if masked:
                        rk = c * ch + lax.broadcasted_iota(jnp.int32, (ch, t - lo), 0)
                        cq = lo + lax.broadcasted_iota(jnp.int32, (ch, t - lo), 1)
                        st = jnp.where(rk <= cq, st, NEG)
                    pt = jnp.exp(st)
                    pt_scr[hh, rows, lo:t] = pt.astype(BF16)
                    ds_scr[hh, rows, lo:t] = (pt * (dp_scr[hh, rows, lo:t] - dl[:, lo:t])).astype(BF16)
                    if lo > 0:
                        pt_scr[hh, rows, 0:lo] = jnp.zeros((ch, lo), BF16)
                        ds_scr[hh, rows, 0:lo] = jnp.zeros((ch, lo), BF16)
                dsb = ds_scr[hh]
                dv_acc[...] += _dot(pt_scr[hh], dom[hh])
                dk_acc[...] += _dot(dsb, jnp.concatenate([qm[hh], ones_col[hh]], axis=1))
                dq_add = dq_add + _dot_tn(dsb, kaug[hh])
            dq_acc[pl.ds(r0, t), :] += dq_add
            return 0

        step(j, True)
        lax.fori_loop(j + 1, n, lambda i, _: step(i, False), 0)
        dk_ref[...] = dk_acc[:, 0:LANES].astype(BF16)
        dv_ref[...] = dv_acc[...].astype(BF16)
        dfk_ref[...] = -dk_acc[:, LANES:wide]

        @pl.when(j == n - 1)
        def _():
            dq_ref[...] = (dq_acc[:, 0:LANES] * (HEAD_DIM ** -0.5)).astype(BF16)
            dfq_ref[...] = dq_acc[:, LANES:wide]

    stat = pl.BlockSpec((1, 2, n, 1, t), lambda h, j: (h, 0, 0, 0, 0))
    blk = pl.BlockSpec((t, LANES), lambda h, j: (j, h))
    full = pl.BlockSpec((s, LANES), lambda h, j: (0, h))
    return pl.pallas_call(
        body, name="attn_bwd", grid=(4, n),
        in_specs=[full, full,
                  pl.BlockSpec((t, LANES), lambda h, j: (j, 4 + h)),
                  pl.BlockSpec((t, LANES), lambda h, j: (j, 8 + h)),
                  stat, stat, stat, blk],
        out_specs=(full, blk, blk, blk, full),
        out_shape=(jax.ShapeDtypeStruct((s, FOX_W), BF16), jax.ShapeDtypeStruct((s, FOX_W), BF16),
                   jax.ShapeDtypeStruct((s, FOX_W), BF16), jax.ShapeDtypeStruct((s, 4 * LANES), F32),
                   jax.ShapeDtypeStruct((s, 4 * LANES), F32)),
        scratch_shapes=[pltpu.VMEM((s, wide), F32), pltpu.VMEM((2, t, t), F32), pltpu.VMEM((2, t, t), F32),
                        pltpu.VMEM((2, t, t), BF16), pltpu.VMEM((2, t, t), BF16), pltpu.VMEM((t, wide), F32),
                        pltpu.VMEM((t, LANES), F32), pltpu.VMEM((2, t, 1), F32)],
        compiler_params=_cparams(("parallel", "arbitrary")))(qkv, do, qkv, qkv, lse5, dlt5, frow5, fpc)


def _ssm_block_params(a_re, a_im, log_dt, b_re, b_im):
    dt = jnp.exp(log_dt)[:, None]
    mag = jnp.exp(a_re * dt)
    ar = mag * jnp.cos(a_im * dt)
    ai = mag * jnp.sin(a_im * dt)
    den = a_re * a_re + a_im * a_im
    nr = ar - 1.0
    cr = (nr * a_re + ai * a_im) / den
    ci = (ai * a_re - nr * a_im) / den
    bbr = cr[:, :, None] * b_re - ci[:, :, None] * b_im
    bbi = cr[:, :, None] * b_im + ci[:, :, None] * b_re
    return ar, ai, bbr, bbi


def _block_diag(blocks):
    g, r, c = blocks.shape
    eye = jnp.eye(g, dtype=blocks.dtype)
    return (blocks[:, :, None, :] * eye[:, None, :, None]).reshape(g * r, g * c)


def _scan_consts(a_re, a_im, log_dt, reverse):
    dt = jnp.exp(log_dt)[:, None]
    lr = (a_re * dt).reshape(1, NSTATE)
    li = (a_im * dt).reshape(1, NSTATE)
    if reverse:
        li = -li
    rows = jnp.arange(8, dtype=F32)[:, None]

    def power(k):
        mag = jnp.exp(k * lr)
        return mag * jnp.cos(k * li), mag * jnp.sin(k * li)

    tiles = []
    for k in (1, 2, 4):
        keep = (rows < 8 - k) if reverse else (rows >= k)
        pr, pi_ = power(float(k))
        tiles += [jnp.where(keep, pr, 0.0), jnp.where(keep, pi_, 0.0)]
    expo = (8.0 - rows) if reverse else (rows + 1.0)
    tiles += list(power(expo))
    return jnp.stack([jnp.broadcast_to(tl, (8, NSTATE)) for tl in tiles])


_SCAN_W = 512
_HALF_W = S5_W // 2
_HALF_S = NSTATE // 2


def _compact_diag(blocks_re, blocks_im):
    hg = GROUPS // 2
    return jnp.concatenate([_block_diag(b[h * hg:(h + 1) * hg]) for b in (blocks_re, blocks_im) for h in range(2)],
                           axis=1)


def _half_expand(v, w_ref, out_ref):
    for half in range(2):
        vh = v[:, half * _HALF_W:(half + 1) * _HALF_W]
        for part in range(2):
            c0 = part * NSTATE + half * _HALF_S
            out_ref[:, c0:c0 + _HALF_S] = _dot(vh, w_ref[:, c0:c0 + _HALF_S])


def _half_contract(x_ref, w_ref, half):
    out = None
    for part in range(2):
        r0 = part * NSTATE + half * _HALF_S
        term = _dot_nt(x_ref[:, r0:r0 + _HALF_S].astype(BF16), w_ref[:, r0:r0 + _HALF_S])
        out = term if out is None else out + term
    return out


def _half_outer(v, x_ref, acc_ref):
    for half in range(2):
        vh = v[:, half * _HALF_W:(half + 1) * _HALF_W]
        for part in range(2):
            c0 = part * NSTATE + half * _HALF_S
            acc_ref[:, c0:c0 + _HALF_S] += _dot_tn(vh, x_ref[:, c0:c0 + _HALF_S].astype(BF16))


def _ssm_fwd(rest, bd, cd, consts):
    s = rest.shape[0]
    tb = min(TB_SSM, s)
    ns2 = 2 * NSTATE

    def body(u_ref, bd_ref, cd_ref, cf_ref, y_ref, x_ref, cb_ref):
        @pl.when(pl.program_id(0) == 0)
        def _():
            cb_ref[...] = jnp.zeros_like(cb_ref)

        _half_expand(u_ref[...].astype(BF16), bd_ref, x_ref)

        def tile(ti, _):
            r0 = pl.multiple_of(ti * 8, 8)
            for cc in range(NSTATE // _SCAN_W):
                cr = pl.ds(cc * _SCAN_W, _SCAN_W)
                ci = pl.ds(NSTATE + cc * _SCAN_W, _SCAN_W)
                re = x_ref[pl.ds(r0, 8), cr]
                im = x_ref[pl.ds(r0, 8), ci]
                for n_, k in enumerate((1, 2, 4)):
                    ar = cf_ref[2 * n_, :, cr]
                    ai = cf_ref[2 * n_ + 1, :, cr]
                    sr = pltpu.roll(re, k, 0)
                    si = pltpu.roll(im, k, 0)
                    re, im = re + ar * sr - ai * si, im + ar * si + ai * sr
                pr = cf_ref[6, :, cr]
                pi_ = cf_ref[7, :, cr]
                cbr = cb_ref[:, cr]
                cbi = cb_ref[:, ci]
                re, im = re + pr * cbr - pi_ * cbi, im + pr * cbi + pi_ * cbr
                x_ref[pl.ds(r0, 8), cr] = re
                x_ref[pl.ds(r0, 8), ci] = im
                cb_ref[:, cr] = jnp.broadcast_to(re[7:8, :], (8, _SCAN_W))
                cb_ref[:, ci] = jnp.broadcast_to(im[7:8, :], (8, _SCAN_W))
            return 0

        lax.fori_loop(0, tb // 8, tile, 0)
        for half in range(2):
            y_ref[:, half * _HALF_W:(half + 1) * _HALF_W] = _half_contract(x_ref, cd_ref, half)

    return pl.pallas_call(
        body, name="ssm_fwd", grid=(s // tb,),
        in_specs=[pl.BlockSpec((tb, S5_W), lambda i: (i, R_U // S5_W)), _const((_HALF_W, ns2)), _const((_HALF_W, ns2)),
                  _const((8, 8, NSTATE))],
        out_specs=(pl.BlockSpec((tb, S5_W), lambda i: (i, 0)), pl.BlockSpec((tb, ns2), lambda i: (i, 0))),
        out_shape=(jax.ShapeDtypeStruct((s, S5_W), F32), jax.ShapeDtypeStruct((s, ns2), F32)),
        scratch_shapes=[pltpu.VMEM((8, ns2), F32)],
        compiler_params=_cparams(("arbitrary",)))(rest, bd, cd, consts)


def _ssm_bwd(dys, xs, rest, bd, cd, consts, dskip):
    s = dys.shape[0]
    tb = min(TB_SSM, s)
    nb = s // tb
    ns2 = 2 * NSTATE
    nt = tb // 8

    def body(dy_ref, x_ref, u_ref, bd_ref, cd_ref, cf_ref, dsk_ref, du_ref, gb_ref, gc_ref, da_ref,
             g_ref, cb_ref, acc_b, acc_c):
        step = pl.program_id(0)

        @pl.when(step == 0)
        def _():
            cb_ref[...] = jnp.zeros_like(cb_ref)
            acc_b[...] = jnp.zeros_like(acc_b)
            acc_c[...] = jnp.zeros_like(acc_c)
            da_ref[...] = jnp.zeros_like(da_ref)

        dy = dy_ref[...]
        dyb = dy.astype(BF16)
        _half_expand(dyb, cd_ref, g_ref)
        last_row = lax.broadcasted_iota(jnp.int32, (8, _SCAN_W), 0) == 7

        def tile(tt, _):
            r0 = pl.multiple_of((nt - 1 - tt) * 8, 8)
            for cc in range(NSTATE // _SCAN_W):
                cr = pl.ds(cc * _SCAN_W, _SCAN_W)
                ci = pl.ds(NSTATE + cc * _SCAN_W, _SCAN_W)
                re = g_ref[pl.ds(r0, 8), cr]
                im = g_ref[pl.ds(r0, 8), ci]
                for n_, k in enumerate((1, 2, 4)):
                    ar = cf_ref[2 * n_, :, cr]
                    ai = cf_ref[2 * n_ + 1, :, cr]
                    sr = pltpu.roll(re, 8 - k, 0)
                    si = pltpu.roll(im, 8 - k, 0)
                    re, im = re + ar * sr - ai * si, im + ar * si + ai * sr
                pr = cf_ref[6, :, cr]
                pi_ = cf_ref[7, :, cr]
                cbr = cb_ref[:, cr]
                cbi = cb_ref[:, ci]
                re, im = re + pr * cbr - pi_ * cbi, im + pr * cbi + pi_ * cbr
                g_ref[pl.ds(r0, 8), cr] = re
                g_ref[pl.ds(r0, 8), ci] = im
                gnr = jnp.where(last_row, cbr, pltpu.roll(re, 7, 0))
                gni = jnp.where(last_row, cbi, pltpu.roll(im, 7, 0))
                xr = x_ref[pl.ds(r0, 8), cr]
                xi = x_ref[pl.ds(r0, 8), ci]
                da_ref[:, cr] += gnr * xr + gni * xi
                da_ref[:, ci] += gni * xr - gnr * xi
                cb_ref[:, cr] = jnp.broadcast_to(re[0:1, :], (8, _SCAN_W))
                cb_ref[:, ci] = jnp.broadcast_to(im[0:1, :], (8, _SCAN_W))
            return 0

        lax.fori_loop(0, nt, tile, 0)
        for half in range(2):
            cols = slice(half * _HALF_W, (half + 1) * _HALF_W)
            du_ref[:, cols] = (_half_contract(g_ref, bd_ref, half) + dy[:, cols] * dsk_ref[:, cols]).astype(BF16)
        _half_outer(u_ref[...].astype(BF16), g_ref, acc_b)
        _half_outer(dyb, x_ref, acc_c)

        @pl.when(step == nb - 1)
        def _():
            for g in range(GROUPS):
                src = slice((g % (GROUPS // 2)) * GCH, (g % (GROUPS // 2) + 1) * GCH)
                dst = slice(g * GCH, (g + 1) * GCH)
                for part in range(2):
                    cols = slice(part * NSTATE + g * STATE, part * NSTATE + (g + 1) * STATE)
                    gb_ref[dst, part * STATE:(part + 1) * STATE] = acc_b[src, cols]
                    gc_ref[dst, part * STATE:(part + 1) * STATE] = acc_c[src, cols]

    rev = lambda i: (nb - 1 - i, 0)
    small = pl.BlockSpec((S5_W, 2 * STATE), lambda i: (0, 0))
    return pl.pallas_call(
        body, name="ssm_bwd", grid=(nb,),
        in_specs=[pl.BlockSpec((tb, S5_W), rev), pl.BlockSpec((tb, ns2), rev),
                  pl.BlockSpec((tb, S5_W), lambda i: (nb - 1 - i, R_U // S5_W)),
                  _const((_HALF_W, ns2)), _const((_HALF_W, ns2)), _const((8, 8, NSTATE)), _const((1, S5_W))],
        out_specs=(pl.BlockSpec((tb, S5_W), rev), small, small, pl.BlockSpec((8, ns2), lambda i: (0, 0))),
        out_shape=(jax.ShapeDtypeStruct((s, S5_W), BF16), jax.ShapeDtypeStruct((S5_W, 2 * STATE), F32),
                   jax.ShapeDtypeStruct((S5_W, 2 * STATE), F32), jax.ShapeDtypeStruct((8, ns2), F32)),
        scratch_shapes=[pltpu.VMEM((tb, ns2), F32), pltpu.VMEM((8, ns2), F32),
                        pltpu.VMEM((_HALF_W, ns2), F32), pltpu.VMEM((_HALF_W, ns2), F32)],
        compiler_params=_cparams(("arbitrary",)))(dys, xs, rest, bd, cd, consts, dskip)


_GELU_C = math.sqrt(2.0 / math.pi)
_GELU_A = 0.044715


def _mid(o, rest, ys0, x, tgt, w, vec, hsel):
    s = o.shape[0]
    tm = min(TM, s)
    nsteps = s // tm
    half = FOX_W

    def body(o_ref, ga_ref, gb_ref, za_ref, u_ref, zb_ref, ys0_ref, x_ref, t_ref,
             wglu_ref, wua_ref, wub_ref, wout_ref, vec_ref, hsel_ref,
             dx2_ref, dga_ref, dgb_ref, do_ref, dza_ref, dzb_ref, dys_ref, dlt_ref,
             gout_hbm, gua_hbm, gub_hbm, gglu_hbm, vout_ref,
             a_out, a_ua, a_ub, a_glu):
        step = pl.program_id(0)

        @pl.when(step == 0)
        def _():
            a_out[...] = jnp.zeros_like(a_out)
            a_ua[...] = jnp.zeros_like(a_ua)
            a_ub[...] = jnp.zeros_like(a_ub)
            a_glu[...] = jnp.zeros_like(a_glu)
            vout_ref[...] = jnp.zeros_like(vout_ref)

        gate = vec_ref[0:1, :]
        gfin = vec_ref[1:2, :]
        dsk = vec_ref[2:3, 0:half]
        bglu = vec_ref[2:3, half:2 * half]

        o_v = o_ref[...]
        za = za_ref[...]
        sza = _sigmoid(za)
        silu_za = za * sza
        ya_b = (o_v * silu_za).astype(BF16)
        u_v = u_ref[...]
        ys = ys0_ref[...] + dsk * u_v
        inner = _GELU_C * (ys + _GELU_A * ys * ys * ys)
        th = jnp.tanh(inner)
        yg = 0.5 * ys * (1.0 + th)
        yg_b = yg.astype(BF16)
        st = _sigmoid(_dot(yg_b, wglu_ref[...]) + bglu)
        yb1 = yg * st
        zb = zb_ref[...]
        szb = _sigmoid(zb)
        silu_zb = zb * szb
        yb_b = (yb1 * silu_zb).astype(BF16)
        ua = _dot(ya_b, wua_ref[...])
        ub = _dot(yb_b, wub_ref[...])
        sga = _sigmoid(ga_ref[...])
        sgb = _sigmoid(gb_ref[...])
        merged_b = (sga * ua + sgb * ub).astype(BF16)
        mo = _dot(merged_b, wout_ref[...])
        x2 = x_ref[...] + gate * mo
        r2 = lax.rsqrt(jnp.mean(x2 * x2, axis=-1, keepdims=True) + EPS)
        x2n = x2 * r2
        diff = x2n * gfin - t_ref[...]
        loss = 0.5 * jnp.sum(jnp.mean(diff * diff, axis=-1, keepdims=True), axis=0, keepdims=True)
        dy = diff * (1.0 / D_MODEL)
        dx2n = dy * gfin
        dx2 = r2 * (dx2n - x2n * jnp.mean(dx2n * x2n, axis=-1, keepdims=True))
        dx2_ref[...] = dx2
        vout_ref[0:1, :] += jnp.sum(dy * x2n, axis=0, keepdims=True)
        vout_ref[1:2, :] += jnp.sum(dx2 * mo, axis=0, keepdims=True)
        vout_ref[3:4, :] += jnp.broadcast_to(loss, (1, D_MODEL))
        dmo_b = (dx2 * gate).astype(BF16)
        dmerged = _dot_nt(dmo_b, wout_ref[...])
        a_out[...] += _dot_tn(merged_b, dmo_b)
        dua_b = (dmerged * sga).astype(BF16)
        dub_b = (dmerged * sgb).astype(BF16)
        dga_ref[...] = (dmerged * ua * sga * (1.0 - sga)).astype(BF16)
        dgb_ref[...] = (dmerged * ub * sgb * (1.0 - sgb)).astype(BF16)
        dya = _dot_nt(dua_b, wua_ref[...])
        dyb = _dot_nt(dub_b, wub_ref[...])
        a_ua[...] += _dot_tn(ya_b, dua_b)
        a_ub[...] += _dot_tn(yb_b, dub_b)
        do_b = (dya * silu_za).astype(BF16)
        do_ref[...] = do_b
        dza_ref[...] = (dya * o_v * (sza * (1.0 + za * (1.0 - sza)))).astype(BF16)
        dlt_ref[...] = lax.dot_general(hsel_ref[...], do_b.astype(F32) * o_v, (((1,), (1,)), ((), ())),
                                       preferred_element_type=F32, precision=HI)
        dyb1 = dyb * silu_zb
        dzb_ref[...] = (dyb * yb1 * (szb * (1.0 + zb * (1.0 - szb)))).astype(BF16)
        dt = dyb1 * yg * st * (1.0 - st)
        dt_b = dt.astype(BF16)
        dyg = dyb1 * st + _dot_nt(dt_b, wglu_ref[...])
        a_glu[...] += _dot_tn(yg_b, dt_b)
        dgelu = 0.5 * (1.0 + th) + 0.5 * ys * (1.0 - th * th) * _GELU_C * (1.0 + 3.0 * _GELU_A * ys * ys)
        dys = dyg * dgelu
        dys_ref[...] = dys
        vout_ref[2:3, 0:half] += jnp.sum(dys * u_v, axis=0, keepdims=True)
        vout_ref[2:3, half:2 * half] += jnp.sum(dt, axis=0, keepdims=True)

        @pl.when(step == nsteps - 1)
        def _():
            pltpu.sync_copy(a_out, gout_hbm)
            pltpu.sync_copy(a_ua, gua_hbm)
            pltpu.sync_copy(a_ub, gub_hbm)
            pltpu.sync_copy(a_glu, gglu_hbm)

    def rows(width, col=0):
        return pl.BlockSpec((tm, width), lambda i, col=col: (i, col))

    anyspace = pl.BlockSpec(memory_space=pl.ANY)
    wshapes = [(S5_W, S5_W), (FOX_W, D_MODEL), (S5_W, D_MODEL), (D_MODEL, D_MODEL)]
    return pl.pallas_call(
        body, name="mid", grid=(nsteps,),
        in_specs=[rows(FOX_W), rows(D_MODEL, R_GA // D_MODEL), rows(D_MODEL, R_GB // D_MODEL),
                  rows(FOX_W, R_ZA // FOX_W), rows(S5_W, R_U // S5_W), rows(S5_W, R_ZB // S5_W),
                  rows(S5_W), rows(D_MODEL), rows(D_MODEL)]
                 + [_const(sh) for sh in wshapes]
                 + [_const((8, D_MODEL)), _const((HEADS, FOX_W))],
        out_specs=(rows(D_MODEL), rows(D_MODEL), rows(D_MODEL), rows(FOX_W), rows(FOX_W), rows(S5_W), rows(S5_W),
                   pl.BlockSpec((HEADS, tm), lambda i: (0, i)),
                   anyspace, anyspace, anyspace, anyspace, pl.BlockSpec((8, D_MODEL), lambda i: (0, 0))),
        out_shape=(jax.ShapeDtypeStruct((s, D_MODEL), F32), jax.ShapeDtypeStruct((s, D_MODEL), BF16),
                   jax.ShapeDtypeStruct((s, D_MODEL), BF16), jax.ShapeDtypeStruct((s, FOX_W), BF16),
                   jax.ShapeDtypeStruct((s, FOX_W), BF16), jax.ShapeDtypeStruct((s, S5_W), BF16),
                   jax.ShapeDtypeStruct((s, S5_W), F32), jax.ShapeDtypeStruct((HEADS, s), F32),
                   jax.ShapeDtypeStruct((D_MODEL, D_MODEL), F32), jax.ShapeDtypeStruct((FOX_W, D_MODEL), F32),
                   jax.ShapeDtypeStruct((S5_W, D_MODEL), F32), jax.ShapeDtypeStruct((S5_W, S5_W), F32),
                   jax.ShapeDtypeStruct((8, D_MODEL), F32)),
        scratch_shapes=[pltpu.VMEM((D_MODEL, D_MODEL), F32), pltpu.VMEM((FOX_W, D_MODEL), F32),
                        pltpu.VMEM((S5_W, D_MODEL), F32), pltpu.VMEM((S5_W, S5_W), F32)],
        compiler_params=_cparams(("arbitrary",)),
    )(o, rest, rest, rest, rest, rest, ys0, x, tgt, *w, vec, hsel)


def _dh(dq, dk, dv, dga, dgb, dza, du, dzb, df, wqkv_t, wrest_t, x, dx2, gs):
    s = x.shape[0]
    tm = min(TM, s)

    def body(dq_ref, dk_ref, dv_ref, dga_ref, dgb_ref, dza_ref, du_ref, dzb_ref, df_ref, wq_ref, wr_ref,
             x_ref, dx2_ref, gs_ref, gx_ref, vout_ref):
        @pl.when(pl.program_id(0) == 0)
        def _():
            vout_ref[...] = jnp.zeros_like(vout_ref)

        dh = _dot(dq_ref[...], wq_ref[0:512, :])
        dh += _dot(dk_ref[...], wq_ref[512:1024, :])
        dh += _dot(dv_ref[...], wq_ref[1024:1536, :])
        dh += _dot(dga_ref[...], wr_ref[R_GA:R_GB, :])
        dh += _dot(dgb_ref[...], wr_ref[R_GB:R_ZA, :])
        dh += _dot(dza_ref[...], wr_ref[R_ZA:R_U, :])
        dh += _dot(du_ref[...], wr_ref[R_U:R_ZB, :])
        dh += _dot(dzb_ref[...], wr_ref[R_ZB:R_F, :])
        dh += _dot(df_ref[...], wr_ref[R_F:REST_W, :])
        xv = x_ref[...]
        r = lax.rsqrt(jnp.mean(xv * xv, axis=-1, keepdims=True) + EPS)
        xn = xv * r
        dxn = dh * gs_ref[...]
        gx_ref[...] = dx2_ref[...] + r * (dxn - xn * jnp.mean(dxn * xn, axis=-1, keepdims=True))
        vout_ref[0:1, :] += jnp.sum(dh * xn, axis=0, keepdims=True)
        vout_ref[1:2, :] += jnp.sum(dh, axis=0, keepdims=True)

    def rows(width):
        return pl.BlockSpec((tm, width), lambda i: (i, 0))

    return pl.pallas_call(
        body, name="dh", grid=(s // tm,),
        in_specs=[rows(512), rows(512), rows(512), rows(1024), rows(1024), rows(512), rows(512), rows(512), rows(128),
                  _const((1536, D_MODEL)), _const((REST_W, D_MODEL)), rows(D_MODEL), rows(D_MODEL), _const((1, D_MODEL))],
        out_specs=(rows(D_MODEL), pl.BlockSpec((8, D_MODEL), lambda i: (0, 0))),
        out_shape=(jax.ShapeDtypeStruct((s, D_MODEL), F32), jax.ShapeDtypeStruct((8, D_MODEL), F32)),
        compiler_params=_cparams(("arbitrary",)),
    )(dq, dk, dv, dga, dgb, dza, du, dzb, df, wqkv_t, wrest_t, x, dx2, gs)


def _row_block(rows, mult=8, cap=512):
    if rows <= mult:
        return rows
    padded = -(-rows // mult) * mult
    for cand in range(min(cap, padded) // mult * mult, 0, -mult):
        if padded % cand == 0:
            return cand
    return padded


def _sum4(parts, name):
    rows, cols = parts.shape[1:]
    br = _row_block(rows, 16)

    def body(p_ref, o_ref):
        acc = p_ref[0].astype(F32)
        for k in range(1, 4):
            acc = acc + p_ref[k].astype(F32)
        o_ref[...] = acc

    return pl.pallas_call(
        body, name=name, grid=(pl.cdiv(rows, br),),
        in_specs=[pl.BlockSpec((4, br, cols), lambda i: (0, i, 0))],
        out_specs=pl.BlockSpec((br, cols), lambda i: (i, 0)),
        out_shape=jax.ShapeDtypeStruct((rows, cols), F32), compiler_params=_cparams(("parallel",)))(parts)


def _pair_add(a, b, name):
    shape = a.shape
    a, b = a.reshape(-1, shape[-1]), b.reshape(-1, shape[-1])
    rows, cols = a.shape
    br = _row_block(rows, 16, 1024)

    def body(a_ref, b_ref, o_ref):
        o_ref[...] = (a_ref[...].astype(F32) + b_ref[...].astype(F32)).astype(BF16)

    spec = pl.BlockSpec((br, cols), lambda i: (i, 0))
    return pl.pallas_call(
        body, name=name, grid=(pl.cdiv(rows, br),), in_specs=[spec, spec], out_specs=spec,
        out_shape=jax.ShapeDtypeStruct((rows, cols), BF16), compiler_params=_cparams(("parallel",)))(a, b).reshape(shape)


def _adamw(w, g, m, v, name):
    rows, cols = w.shape
    br = _row_block(rows)

    def body(w_ref, g_ref, m_ref, v_ref, d_ref, nm_ref, nv_ref):
        gv = g_ref[...]
        nm = ADAM_B1 * m_ref[...] + (1.0 - ADAM_B1) * gv
        nv = ADAM_B2 * v_ref[...] + (1.0 - ADAM_B2) * (gv * gv)
        m_hat = nm / (1.0 - ADAM_B1 ** ADAM_STEP)
        v_hat = nv / (1.0 - ADAM_B2 ** ADAM_STEP)
        d_ref[...] = -ADAM_LR * (m_hat / (jnp.sqrt(v_hat) + ADAM_EPS) + ADAM_WD * w_ref[...])
        nm_ref[...] = nm
        nv_ref[...] = nv

    spec = pl.BlockSpec((br, cols), lambda i: (i, 0))
    shape = jax.ShapeDtypeStruct((rows, cols), F32)
    return pl.pallas_call(
        body, name=name, grid=(pl.cdiv(rows, br),), in_specs=[spec] * 4, out_specs=(spec,) * 3,
        out_shape=(shape,) * 3, compiler_params=_cparams(("parallel",)))(w, g, m, v)


def _pack(parts, row_multiple=8):
    flat = []
    for p in parts:
        v = p.reshape(-1).astype(F32)
        pad = (-v.shape[0]) % LANES
        if pad:
            v = jnp.concatenate([v, jnp.zeros((pad,), F32)])
        flat.append(v)
    v = jnp.concatenate(flat)
    rows = v.shape[0] // LANES
    pad_rows = (-rows) % row_multiple
    if pad_rows:
        v = jnp.concatenate([v, jnp.zeros((pad_rows * LANES,), F32)])
    return v.reshape(-1, LANES)


def _unpack(packed, shapes):
    lead = packed.shape[:-2]
    flat = packed.reshape(lead + (-1,))
    out, off = [], 0
    for sh in shapes:
        size = math.prod(sh)
        out.append(flat[..., off:off + size].reshape(lead + tuple(sh)))
        off += size + (-size) % LANES
    return out


def kernel(x, c, w_ada, b_ada, g_norm, w_in, b_f, a_re, a_im, log_dt, b_re, b_im, c_re, c_im, d_skip, w_glu, b_glu, w_up_a, w_up_b, w_out, g_final, loss_target, m_w_ada, m_b_ada, m_g_norm, m_w_in, m_b_f, m_a_re, m_a_im, m_log_dt, m_b_re, m_b_im, m_c_re, m_c_im, m_d_skip, m_w_glu, m_b_glu, m_w_up_a, m_w_up_b, m_w_out, m_g_final, v_w_ada, v_b_ada, v_g_norm, v_w_in, v_b_f, v_a_re, v_a_im, v_log_dt, v_b_re, v_b_im, v_c_re, v_c_im, v_d_skip, v_w_glu, v_b_glu, v_w_up_a, v_w_up_b, v_w_out, v_g_final):
    xi, yi, ci = lax.axis_index("x"), lax.axis_index("y"), lax.axis_index("c")
    chip = 2 * xi + yi
    me = 4 * xi + 2 * yi + ci
    s = x.shape[1]
    x2d = x[0]
    tgt = loss_target[0]
    n_att = s // min(T_ATT, s)
    t_att = min(T_ATT, s)

    c_all, _ = _allgather8(c.reshape(8, LANES), "gather_c")
    c_all = c_all.reshape(8, D_MODEL)
    ncol = w_ada.shape[2]
    b_cols = lax.dynamic_slice_in_dim(b_ada, chip * ncol, ncol, axis=1)
    mod_cols = _mod_cols(c_all, w_ada[0], b_cols)
    mod_all, _ = _allgather8(mod_cols.reshape(-1, LANES), "gather_mod")
    mod_all = mod_all.reshape(4, 2, 8, ncol)[:, 0]
    mod_me = lax.dynamic_index_in_dim(mod_all, me, axis=1, keepdims=False).reshape(1, 3 * D_MODEL)
    shift, scale, gate = mod_me[:, :D_MODEL], mod_me[:, D_MODEL:2 * D_MODEL], mod_me[:, 2 * D_MODEL:]
    gs = g_norm * (1.0 + scale)

    nshard = w_in.shape[2]
    w_in_t, m_in_t, v_in_t = (jnp.swapaxes(a[0], 0, 1) for a in (w_in, m_w_in, v_w_in))
    wt_pack = jnp.pad(w_in_t.astype(BF16), ((0, SHARD_ROWS - nshard), (0, 0)))
    misc_shapes = [w_glu.shape[1:], w_up_a.shape[1:], w_up_b.shape[1:], w_out.shape[1:]]
    misc_pack = jnp.concatenate([w.reshape(-1) for w in (w_glu, w_up_a, w_up_b, w_out)]).astype(BF16).reshape(-1, LANES)
    def halves(a):
        return a.reshape((2, a.shape[0] // 2) + a.shape[1:])

    wt_all, misc_all = _gather_shards([halves(wt_pack), halves(misc_pack)], "gather_weights")
    wt_all = lax.dynamic_update_index_in_dim(wt_all, halves(wt_pack), chip, 0).reshape((4,) + wt_pack.shape)
    misc_all = lax.dynamic_update_index_in_dim(misc_all, halves(misc_pack), chip, 0).reshape((4,) + misc_pack.shape)
    p_glu, p_ua, p_ub, p_out = _unpack(misc_all, misc_shapes)

    def w_rows(lo, hi):
        out = []
        for j in range(4):
            a, b = max(lo, j * nshard), min(hi, (j + 1) * nshard)
            if a < b:
                out.append(wt_all[j, a - j * nshard:b - j * nshard])
        return out

    wqkv_t = jnp.concatenate(w_rows(O_Q, O_F), axis=0)
    wrest_t = jnp.concatenate(w_rows(O_GA, O_GB) + w_rows(O_GB, O_END) + w_rows(O_ZA, O_U) + w_rows(O_U, O_ZB)
                              + w_rows(O_ZB, O_GA) + w_rows(O_F, O_ZA)
                              + [jnp.zeros((REST_W - R_F - HEADS, D_MODEL), BF16)], axis=0)
    wmid = (p_glu.reshape(S5_W, S5_W), jnp.concatenate([p_ua[j] for j in range(4)], axis=1),
            jnp.concatenate([p_ub[j] for j in range(4)], axis=1), p_out.reshape(D_MODEL, D_MODEL))

    h, qkv, rest = _prenorm_proj(x2d, gs, shift, wqkv_t, wrest_t)
    bf128 = jnp.pad(b_f, ((0, 0), (0, LANES - HEADS)))
    selp = _head_pair_selector()
    fpc, f_t = _fcum(rest, bf128, selp)
    frow5 = f_t.reshape(4, 2, n_att, 1, t_att)
    o, lse_pc = _attn_fwd(qkv, frow5, fpc)

    abar_r, abar_i, bb_r, bb_i = _ssm_block_params(a_re[0], a_im[0], log_dt[0], b_re[0], b_im[0])
    bb_rt, bb_it = jnp.swapaxes(bb_r, 1, 2).astype(BF16), jnp.swapaxes(bb_i, 1, 2).astype(BF16)
    cr_b, ci_b = c_re[0].astype(BF16), (-c_im[0]).astype(BF16)
    bd_c, cd_c = _compact_diag(bb_rt, bb_it), _compact_diag(cr_b, ci_b)
    ys0, xs = _ssm_fwd(rest, bd_c, cd_c, _scan_consts(a_re[0], a_im[0], log_dt[0], False))

    vec = jnp.concatenate([gate, g_final.reshape(1, D_MODEL), jnp.concatenate([d_skip, b_glu], axis=1),
                           jnp.zeros((5, D_MODEL), F32)], axis=0)
    hsel = jnp.repeat(jnp.eye(HEADS, dtype=F32), HEAD_DIM, axis=1)
    (dx2, dga, dgb, do, dza, dzb, dys, dlt_t, g_out, g_ua, g_ub, g_glu, vmid) = _mid(
        o, rest, ys0, x2d, tgt, wmid, vec, hsel)

    lse_t = jnp.transpose(lse_pc.reshape(s, 4 // ATT_PAIRS, LANES)[:, :, :2 * ATT_PAIRS], (1, 2, 0))
    lse5 = lse_t.reshape(4, 2, n_att, 1, t_att)
    dlt5 = dlt_t.reshape(4, 2, n_att, 1, t_att)
    dq, dk, dv, dfk, dfq = _attn_bwd(qkv, do, lse5, dlt5, frow5, fpc)
    du, g_bd, g_cdt, da8 = _ssm_bwd(dys, xs, rest, bd_c, cd_c, _scan_consts(a_re[0], a_im[0], log_dt[0], True), d_skip)
    df, dbf8 = _dfcum(dfk, dfq, rest, bf128, selp.T)

    grad_x, vdh = _dh(dq, dk, dv, dga, dgb, dza, du, dzb, df, wqkv_t, wrest_t, x2d, dx2, gs)
    gq, gk, gv, gga, ggb, gza, gu, gzb, gf = _grad_w_rows(h, [dq, dk, dv, dga, dgb, dza, du, dzb, df])
    g_in_t = jnp.concatenate([gq, gk, gv, gf[:HEADS], gza, gu, gzb, gga, ggb], axis=0)

    dgs, dshift = vdh[0:1], vdh[1:2]
    dmod = jnp.concatenate([dshift, dgs * g_norm, vmid[1:2]], axis=1)
    da = jnp.sum(da8, axis=0)
    g_bd = g_bd.reshape(GROUPS, GCH, 2 * STATE)
    g_cdt = g_cdt.reshape(GROUPS, GCH, 2 * STATE)
    g_bbr = jnp.swapaxes(g_bd[:, :, :STATE], 1, 2)
    g_bbi = jnp.swapaxes(g_bd[:, :, STATE:], 1, 2)
    g_cre = g_cdt[:, :, :STATE]
    g_cim = -g_cdt[:, :, STATE:]
    small_shapes = [(1,), (3 * D_MODEL,), (D_MODEL,), (HEADS,), (GROUPS, STATE), (GROUPS, STATE),
                    (GROUPS, STATE, GCH), (GROUPS, STATE, GCH), (GROUPS, GCH, STATE), (GROUPS, GCH, STATE),
                    (S5_W,), (S5_W,), (D_MODEL,)]
    small = _pack([vmid[3, 0:1], dmod, dgs * (1.0 + scale), dbf8[0, :HEADS], da[:NSTATE], da[NSTATE:],
                   g_bbr, g_bbi, g_cre, g_cim, vmid[2, :S5_W], vmid[2, S5_W:], vmid[0]])
    small_all, small_sum = _allgather8(small, "gather_small_grads")
    (loss_s, g_b_ada, g_g_norm, g_b_f, g_abr, g_abi, g_bbr_s, g_bbi_s, g_c_re, g_c_im, g_d_skip, g_b_glu,
     g_g_final) = _unpack(small_sum, small_shapes)
    loss = loss_s[0]
    dmod_all = _unpack(small_all, small_shapes)[1]
    dmod_cols = lax.dynamic_slice_in_dim(dmod_all, chip * ncol, ncol, axis=1)
    g_w_ada = _grad_w_ada(c_all, dmod_cols)
    _, ssm_vjp = jax.vjp(_ssm_block_params, a_re[0], a_im[0], log_dt[0], b_re[0], b_im[0])
    g_a_re, g_a_im, g_log_dt, g_b_re, g_b_im = ssm_vjp((g_abr, g_abi, g_bbr_s, g_bbi_s))

    def shard_cols(g, j):
        n = g.shape[1] // 4
        return g[:, j * n:(j + 1) * n]


    def shard_rows(g, j):
        n = g.shape[0] // 4
        return g[j * n:(j + 1) * n]

    gt_pack = jnp.stack([
        jnp.pad(g_in_t[j * nshard:(j + 1) * nshard].astype(BF16), ((0, SHARD_ROWS - nshard), (0, 0)))
        for j in range(4)])
    gm_pack = jnp.stack([
        jnp.concatenate([shard_rows(g_glu, j).reshape(-1), shard_cols(g_ua, j).reshape(-1),
                         shard_cols(g_ub, j).reshape(-1), shard_rows(g_out, j).reshape(-1)]).astype(BF16)
        .reshape(-1, LANES) for j in range(4)])
    def halves4(a):
        return a.reshape((4, 2, a.shape[1] // 2) + a.shape[2:])

    gt_pack, gm_pack = halves4(gt_pack), halves4(gm_pack)
    recv_in, recv_misc = _swap_sibling([gt_pack, gm_pack], "pair_swap_weight_grads", other_half=True)
    own_in = lax.dynamic_index_in_dim(gt_pack, ci, axis=1, keepdims=False)
    own_misc = lax.dynamic_index_in_dim(gm_pack, ci, axis=1, keepdims=False)
    pair_in = _pair_add(own_in, recv_in, "pair_add_w_in")
    pair_misc = _pair_add(own_misc, recv_misc, "pair_add_misc")
    parts_in, parts_misc = _scatter4([pair_in, pair_misc], "scatter_weight_grads")
    parts_in = lax.dynamic_update_slice_in_dim(parts_in, lax.dynamic_slice_in_dim(pair_in, chip, 1, 0), chip, 0)
    parts_misc = lax.dynamic_update_slice_in_dim(parts_misc, lax.dynamic_slice_in_dim(pair_misc, chip, 1, 0), chip, 0)
    half_in, half_misc = _sum4(parts_in, "sum4_w_in"), _sum4(parts_misc, "sum4_misc")
    sib_in, sib_misc = _swap_sibling([half_in, half_misc], "swap_weight_grads")

    def both_halves(mine, theirs):
        return jnp.concatenate([jnp.where(ci == 0, mine, theirs), jnp.where(ci == 0, theirs, mine)], axis=0)

    tot_in, tot_misc = both_halves(half_in, sib_in), both_halves(half_misc, sib_misc)
    g_glu_s, g_ua_s, g_ub_s, g_out_s = _unpack(tot_misc, misc_shapes)

    def adam(name, w, g, m, v):
        shape = w.shape
        total = math.prod(shape)
        if len(shape) > 1 and shape[-1] >= LANES:
            cols = shape[-1]
        elif total % LANES == 0:
            cols = LANES
        else:
            cols = total
        two = lambda a: a.reshape(-1, cols)
        d, nm, nv = _adamw(two(w), two(g), two(m), two(v), "adamw_" + name)
        return g.reshape(shape), d.reshape(shape), nm.reshape(shape), nv.reshape(shape)

    back = lambda a: jnp.swapaxes(a, 0, 1)[None]
    d_in_t, nm_in_t, nv_in_t = _adamw(w_in_t, tot_in, m_in_t, v_in_t, "adamw_w_in")
    res_w_in = (back(tot_in[:nshard]), back(d_in_t), back(nm_in_t), back(nv_in_t))

    small_w = [b_ada, g_norm, b_f, a_re, a_im, log_dt, b_re, b_im, c_re, c_im, d_skip, b_glu, g_final]
    small_g = [g_b_ada, g_g_norm, g_b_f, g_a_re, g_a_im, g_log_dt, g_b_re, g_b_im, g_c_re, g_c_im, g_d_skip,
               g_b_glu, g_g_final]
    small_m = [m_b_ada, m_g_norm, m_b_f, m_a_re, m_a_im, m_log_dt, m_b_re, m_b_im, m_c_re, m_c_im, m_d_skip,
               m_b_glu, m_g_final]
    small_v = [v_b_ada, v_g_norm, v_b_f, v_a_re, v_a_im, v_log_dt, v_b_re, v_b_im, v_c_re, v_c_im, v_d_skip,
               v_b_glu, v_g_final]
    packed = _adamw(*(_pack(p, 128) for p in (small_w, small_g, small_m, small_v)), "adamw_small")
    shapes = [w.shape for w in small_w]
    sd, sm, sv = (_unpack(p, shapes) for p in packed)
    sres = [(g.reshape(sh), d, nm, nv) for g, sh, d, nm, nv in zip(small_g, shapes, sd, sm, sv)]
    res = [
        adam("w_ada", w_ada, g_w_ada, m_w_ada, v_w_ada), sres[0], sres[1], res_w_in, *sres[2:11],
        adam("w_glu", w_glu, g_glu_s, m_w_glu, v_w_glu), sres[11],
        adam("w_up_a", w_up_a, g_ua_s, m_w_up_a, v_w_up_a),
        adam("w_up_b", w_up_b, g_ub_s, m_w_up_b, v_w_up_b),
        adam("w_out", w_out, g_out_s, m_w_out, v_w_out), sres[12],
    ]
    grads = [r[0] for r in res]
    deltas = [r[1] for r in res]
    new_m = [r[2] for r in res]
    new_v = [r[3] for r in res]
    return (loss, grad_x[None], *grads, *deltas, *new_m, *new_v)
```

```python
import functools
import math

import jax
import jax.numpy as jnp
from jax import lax
from jax.experimental import pallas as pl
from jax.experimental.pallas import tpu as pltpu

F32 = jnp.float32
BF16 = jnp.bfloat16
HI = lax.Precision.HIGHEST
MESH = pl.DeviceIdType.MESH

D_MODEL = 1024
HEADS = 8
HEAD_DIM = 64
FOX_W = 512
S5_W = 512
GROUPS = 32
STATE = 64
GCH = 16
NSTATE = GROUPS * STATE
EPS = 1e-6
NEG = -1e30

ADAM_LR = 0.001
ADAM_B1 = 0.9
ADAM_B2 = 0.999
ADAM_EPS = 1e-08
ADAM_WD = 0.01
ADAM_STEP = 10

VMEM_LIMIT = 56 * 1024 * 1024
LANES = 128

TM = 256
T_ATT = 512
ATT_CHUNK = 32
ATT_PAIRS = 4
TB_SSM = 256
TK_ACC = 512
TB_CUM = 256
SHARD_ROWS = 1312

O_Q, O_K, O_V, O_F, O_ZA, O_U, O_ZB, O_GA, O_GB, O_END = 0, 512, 1024, 1536, 1544, 2056, 2568, 3080, 4104, 5128
REST_W = 3712
R_GA, R_GB, R_ZA, R_U, R_ZB, R_F = 0, 1024, 2048, 2560, 3072, 3584


def _cparams(sem=None):
    kw = dict(vmem_limit_bytes=VMEM_LIMIT)
    if sem is not None:
        kw["dimension_semantics"] = sem
    return pltpu.CompilerParams(**kw)


def _const(shape):
    nd = len(shape)
    return pl.BlockSpec(shape, lambda *_: (0,) * nd, pipeline_mode=pl.Buffered(1))


def _dot(a, b, precision=None):
    return jnp.dot(a, b, preferred_element_type=F32, precision=precision)


def _dot_nt(a, b):
    return lax.dot_general(a, b, (((1,), (1,)), ((), ())), preferred_element_type=F32)


def _dot_tn(a, b, precision=None):
    return lax.dot_general(a, b, (((0,), (0,)), ((), ())), preferred_element_type=F32, precision=precision)


def _sigmoid(z):
    return 1.0 / (1.0 + jnp.exp(-z))


def _allgather8(xs, name):
    rows = xs.shape[0]

    def body(x_ref, out_ref, sum_ref, send_sems, recv_sems, local_sem):
        x, y, c = lax.axis_index("x"), lax.axis_index("y"), lax.axis_index("c")
        me, sibling = (x, y, c), (x, y, 1 - c)
        chips = [(1 - x, y), (x, 1 - y), (1 - x, 1 - y)]

        def slot(px, py, pc):
            return out_ref.at[4 * px + 2 * py + pc]

        def copy(k, block, to, src=None):
            return pltpu.make_async_remote_copy(
                src_ref=slot(*block) if src is None else src, dst_ref=slot(*block),
                send_sem=send_sems.at[k], recv_sem=recv_sems.at[k], device_id=to, device_id_type=MESH)

        mine = pltpu.make_async_copy(x_ref, slot(*me), local_sem)
        mine.start()
        first = [copy(0, me, sibling, src=x_ref)]
        first += [copy(1 + j, me, (*chip, c), src=x_ref) for j, chip in enumerate(chips)]
        for cp in first:
            cp.start()
        passed = [copy(4 + j, (*chip, c), sibling) for j, chip in enumerate(chips)]
        for j, chip in enumerate(chips):
            copy(1 + j, (*chip, c), me).wait_recv()
            passed[j].start()
        copy(0, sibling, me).wait_recv()
        for j, chip in enumerate(chips):
            copy(4 + j, (*chip, 1 - c), me).wait_recv()
        for cp in first + passed:
            cp.wait_send()
        mine.wait()
        acc = out_ref[0]
        for d in range(1, 8):
            acc = acc + out_ref[d]
        sum_ref[...] = acc

    return pl.pallas_call(
        body, name=name,
        out_shape=(jax.ShapeDtypeStruct((8, rows, LANES), F32), jax.ShapeDtypeStruct((rows, LANES), F32)),
        in_specs=[pl.BlockSpec(memory_space=pltpu.VMEM)],
        out_specs=(pl.BlockSpec(memory_space=pltpu.VMEM), pl.BlockSpec(memory_space=pltpu.VMEM)),
        scratch_shapes=[pltpu.SemaphoreType.DMA((7,)), pltpu.SemaphoreType.DMA((7,)), pltpu.SemaphoreType.DMA],
        compiler_params=_cparams(),
    )(xs)


def _gather_shards(srcs, name):
    na = len(srcs)

    def body(*refs):
        src_refs, out_refs = refs[:na], refs[na:2 * na]
        send_sems, recv_sems = refs[2 * na:]
        x, y, c = lax.axis_index("x"), lax.axis_index("y"), lax.axis_index("c")
        sibling = (x, y, 1 - c)
        peers = [(1 - x, y), (x, 1 - y), (1 - x, 1 - y)]

        def copy(a, k, src, slot, which, to):
            return pltpu.make_async_remote_copy(
                src_ref=src, dst_ref=out_refs[a].at[slot, which],
                send_sem=send_sems.at[a * 6 + k], recv_sem=recv_sems.at[a * 6 + k],
                device_id=to, device_id_type=MESH)

        mine = 2 * x + y
        first = [copy(a, k, src_refs[a].at[c], mine, c, (px, py, c))
                 for a in range(na) for k, (px, py) in enumerate(peers)]
        for cp in first:
            cp.start()
        passed = []
        for a in range(na):
            for k, (px, py) in enumerate(peers):
                slot = 2 * px + py
                landed = out_refs[a].at[slot, c]
                copy(a, k, landed, slot, c, (px, py, c)).wait_recv()
                fwd = copy(a, 3 + k, landed, slot, c, sibling)
                fwd.start()
                passed.append(fwd)
        for a in range(na):
            for k, (px, py) in enumerate(peers):
                slot = 2 * px + py
                copy(a, 3 + k, out_refs[a].at[slot, 1 - c], slot, 1 - c, sibling).wait_recv()
        for cp in first + passed:
            cp.wait_send()

    anyspace = pl.BlockSpec(memory_space=pl.ANY)
    return pl.pallas_call(
        body, name=name,
        out_shape=tuple(jax.ShapeDtypeStruct((4,) + tuple(a.shape), a.dtype) for a in srcs),
        in_specs=[anyspace] * na, out_specs=(anyspace,) * na,
        scratch_shapes=[pltpu.SemaphoreType.DMA((6 * na,)), pltpu.SemaphoreType.DMA((6 * na,))],
        compiler_params=_cparams(),
    )(*srcs)


def _swap_sibling(srcs, name, other_half=False):
    na = len(srcs)

    def body(*refs):
        src_refs, out_refs = refs[:na], refs[na:2 * na]
        send_sems, recv_sems = refs[2 * na:]
        x, y, c = lax.axis_index("x"), lax.axis_index("y"), lax.axis_index("c")
        copies = [pltpu.make_async_remote_copy(
            src_ref=src_refs[a].at[:, 1 - c] if other_half else src_refs[a],
            dst_ref=out_refs[a], send_sem=send_sems.at[a], recv_sem=recv_sems.at[a],
            device_id=(x, y, 1 - c), device_id_type=MESH) for a in range(na)]
        for cp in copies:
            cp.start()
        for cp in copies:
            cp.wait()

    def out_of(a):
        shape = (a.shape[0],) + tuple(a.shape[2:]) if other_half else a.shape
        return jax.ShapeDtypeStruct(shape, a.dtype)

    anyspace = pl.BlockSpec(memory_space=pl.ANY)
    return pl.pallas_call(
        body, name=name, out_shape=tuple(out_of(a) for a in srcs),
        in_specs=[anyspace] * na, out_specs=(anyspace,) * na,
        scratch_shapes=[pltpu.SemaphoreType.DMA((na,)), pltpu.SemaphoreType.DMA((na,))],
        compiler_params=_cparams(),
    )(*srcs)


def _mod_cols(c_all, w, b):
    n = w.shape[1]

    def body(c_ref, w_ref, b_ref, o_ref):
        o_ref[...] = _dot(c_ref[...], w_ref[...], HI) + b_ref[...]

    return pl.pallas_call(
        body, name="mod_cols", out_shape=jax.ShapeDtypeStruct((8, n), F32),
        compiler_params=_cparams())(c_all, w, b)


def _grad_w_ada(c_all, dmod_cols):
    n = dmod_cols.shape[1]

    def body(c_ref, d_ref, o_ref):
        o_ref[...] = _dot_tn(c_ref[...], d_ref[...], HI)

    return pl.pallas_call(
        body, name="grad_w_ada", out_shape=jax.ShapeDtypeStruct((D_MODEL, n), F32),
        compiler_params=_cparams())(c_all, dmod_cols)


def _prenorm_proj(x, gs, shift, wqkv_t, wrest_t):
    s = x.shape[0]
    tm = min(TM, s)
    nq, nr = wqkv_t.shape[0], wrest_t.shape[0]

    def body(x_ref, gs_ref, sh_ref, wq_ref, wr_ref, h_ref, qkv_ref, rest_ref):
        xv = x_ref[...]
        r = lax.rsqrt(jnp.mean(xv * xv, axis=-1, keepdims=True) + EPS)
        h = (xv * r * gs_ref[...] + sh_ref[...]).astype(BF16)
        h_ref[...] = h
        qkv_ref[...] = _dot_nt(h, wq_ref[...]).astype(BF16)
        rest_ref[...] = _dot_nt(h, wr_ref[...])

    def rows(width):
        return pl.BlockSpec((tm, width), lambda i: (i, 0))

    return pl.pallas_call(
        body, name="prenorm_proj", grid=(s // tm,),
        in_specs=[rows(D_MODEL), _const((1, D_MODEL)), _const((1, D_MODEL)), _const((nq, D_MODEL)),
                  _const((nr, D_MODEL))],
        out_specs=(rows(D_MODEL), rows(nq), rows(nr)),
        out_shape=(jax.ShapeDtypeStruct((s, D_MODEL), BF16), jax.ShapeDtypeStruct((s, nq), BF16),
                   jax.ShapeDtypeStruct((s, nr), F32)),
        compiler_params=_cparams(("parallel",)))(x, gs, shift, wqkv_t, wrest_t)


def _grad_w_rows(h, ds):
    s = h.shape[0]
    tk = min(TK_ACC, s)
    nd = len(ds)
    widths = [d.shape[1] for d in ds]

    def body(*refs):
        h_ref, d_refs = refs[0], refs[1:1 + nd]
        out_refs, accs = refs[1 + nd:1 + 2 * nd], refs[1 + 2 * nd:]
        step = pl.program_id(0)

        @pl.when(step == 0)
        def _():
            for acc in accs:
                acc[...] = jnp.zeros_like(acc)

        hv = h_ref[...]
        for d_ref, acc in zip(d_refs, accs):
            acc[...] += _dot_tn(d_ref[...], hv)

        @pl.when(step == s // tk - 1)
        def _():
            for acc, out in zip(accs, out_refs):
                pltpu.sync_copy(acc, out)

    anyspace = pl.BlockSpec(memory_space=pl.ANY)
    return pl.pallas_call(
        body, name="grad_w_in", grid=(s // tk,),
        in_specs=[pl.BlockSpec((tk, D_MODEL), lambda k: (k, 0))]
                 + [pl.BlockSpec((tk, w), lambda k: (k, 0)) for w in widths],
        out_specs=(anyspace,) * nd,
        out_shape=tuple(jax.ShapeDtypeStruct((w, D_MODEL), F32) for w in widths),
        scratch_shapes=[pltpu.VMEM((w, D_MODEL), F32) for w in widths],
        compiler_params=_cparams(("arbitrary",)))(h, *ds)


def _head_pair_selector():
    rows = jnp.arange(LANES)[:, None]
    cols = jnp.arange(4 * LANES)[None, :]
    return ((rows < HEADS) & (cols == (rows // 2) * LANES + rows % 2)).astype(F32)


def _fcum(rest, bf128, selp):
    s = rest.shape[0]
    tb = min(TB_CUM, s)

    def body(fz_ref, bf_ref, sel_ref, fpc_ref, ft_ref, carry_ref):
        @pl.when(pl.program_id(0) == 0)
        def _():
            carry_ref[...] = jnp.zeros_like(carry_ref)

        z = fz_ref[...] + bf_ref[...]
        logf = jnp.minimum(z, 0.0) - jnp.log(1.0 + jnp.exp(-jnp.abs(z)))
        r = lax.broadcasted_iota(jnp.int32, (tb, tb), 0)
        c = lax.broadcasted_iota(jnp.int32, (tb, tb), 1)
        tri = (c <= r).astype(F32)
        f = _dot(tri, logf, HI) + carry_ref[0:1, :]
        carry_ref[0:1, :] = f[tb - 1:tb, :]
        fpc_ref[...] = _dot(f, sel_ref[...], HI)
        ft_ref[...] = jnp.transpose(f)[0:HEADS, :]

    return pl.pallas_call(
        body, name="forget_cumsum", grid=(s // tb,),
        in_specs=[pl.BlockSpec((tb, LANES), lambda i: (i, R_F // LANES)), _const((1, LANES)), _const((LANES, 4 * LANES))],
        out_specs=(pl.BlockSpec((tb, 4 * LANES), lambda i: (i, 0)), pl.BlockSpec((HEADS, tb), lambda i: (0, i))),
        out_shape=(jax.ShapeDtypeStruct((s, 4 * LANES), F32), jax.ShapeDtypeStruct((HEADS, s), F32)),
        scratch_shapes=[pltpu.VMEM((8, LANES), F32)],
        compiler_params=_cparams(("arbitrary",)))(rest, bf128, selp)


def _dfcum(dfk, dfq, rest, bf128, selq):
    s = rest.shape[0]
    tb = min(TB_CUM, s)
    nb = s // tb

    def body(dk_ref, dq_ref, fz_ref, bf_ref, sel_ref, df_ref, dbf_ref, carry_ref):
        @pl.when(pl.program_id(0) == 0)
        def _():
            carry_ref[...] = jnp.zeros_like(carry_ref)
            dbf_ref[...] = jnp.zeros_like(dbf_ref)

        d = _dot(dk_ref[...] + dq_ref[...], sel_ref[...], HI)
        r = lax.broadcasted_iota(jnp.int32, (tb, tb), 0)
        c = lax.broadcasted_iota(jnp.int32, (tb, tb), 1)
        triu = (c >= r).astype(F32)
        dlogf = _dot(triu, d, HI) + carry_ref[0:1, :]
        carry_ref[0:1, :] = dlogf[0:1, :]
        z = fz_ref[...] + bf_ref[...]
        df = dlogf * (1.0 / (1.0 + jnp.exp(z)))
        df_ref[...] = df.astype(BF16)
        dbf_ref[0:1, :] += jnp.sum(df, axis=0, keepdims=True)

    return pl.pallas_call(
        body, name="forget_grad", grid=(nb,),
        in_specs=[pl.BlockSpec((tb, 4 * LANES), lambda i: (nb - 1 - i, 0)),
                  pl.BlockSpec((tb, 4 * LANES), lambda i: (nb - 1 - i, 0)),
                  pl.BlockSpec((tb, LANES), lambda i: (nb - 1 - i, R_F // LANES)),
                  _const((1, LANES)), _const((4 * LANES, LANES))],
        out_specs=(pl.BlockSpec((tb, LANES), lambda i: (nb - 1 - i, 0)), pl.BlockSpec((8, LANES), lambda i: (0, 0))),
        out_shape=(jax.ShapeDtypeStruct((s, LANES), BF16), jax.ShapeDtypeStruct((8, LANES), F32)),
        scratch_shapes=[pltpu.VMEM((8, LANES), F32)],
        compiler_params=_cparams(("arbitrary",)))(dfk, dfq, rest, bf128, selq)


def _scaled(q):
    return (q.astype(F32) * (HEAD_DIM ** -0.5)).astype(BF16)


def _attn_fwd(qkv, frow5, fpc):
    s = qkv.shape[0]
    t = min(T_ATT, s)
    n = s // t
    ch = min(ATT_CHUNK, t)
    wide = 2 * LANES
    pairs = ATT_PAIRS
    width = pairs * LANES
    groups = 4 // pairs

    def body(q_ref, k_ref, v_ref, fr_ref, fc_ref, o_ref, lse_ref, s_scr, p_scr, m_scr, a_scr, fq_scr, acc_scr):
        i = pl.program_id(1)
        lane = lax.broadcasted_iota(jnp.int32, (t, LANES), 1)
        first = lane < HEAD_DIM
        ones_col = ((lane == 0).astype(BF16), (lane == 1).astype(BF16))
        m_scr[...] = jnp.full(m_scr.shape, NEG, F32)
        acc_scr[...] = jnp.zeros_like(acc_scr)
        qm = []
        for pp in range(pairs):
            q = _scaled(q_ref[:, pp * LANES:(pp + 1) * LANES])
            zq = jnp.zeros_like(q)
            qm += [jnp.where(first, q, zq), jnp.where(first, zq, q)]
            fq_scr[2 * pp] = fc_ref[:, pp * LANES:pp * LANES + 1]
            fq_scr[2 * pp + 1] = fc_ref[:, pp * LANES + 1:pp * LANES + 2]

        def step(j, masked):
            r0 = pl.multiple_of(j * t, t)
            vaug = []
            for pp in range(pairs):
                kb = k_ref[pl.ds(r0, t), pp * LANES:(pp + 1) * LANES]
                vb = v_ref[pl.ds(r0, t), pp * LANES:(pp + 1) * LANES]
                zv = jnp.zeros_like(vb)
                vaug += [jnp.concatenate([jnp.where(first, vb, zv), ones_col[0]], axis=1),
                         jnp.concatenate([jnp.where(first, zv, vb), ones_col[1]], axis=1)]
                for hh in range(2):
                    s_scr[2 * pp + hh] = _dot_nt(qm[2 * pp + hh], kb)
            pv = []
            for hd in range(2 * pairs):
                fk = fr_ref[hd // 2, hd % 2, j]
                for c in range(t // ch):
                    rows = pl.ds(c * ch, ch)
                    hi = min(t, (c * ch // LANES + 1) * LANES) if masked else t
                    sc = s_scr[hd, rows, 0:hi] - fk[:, 0:hi]
                    if masked:
                        rq = c * ch + lax.broadcasted_iota(jnp.int32, (ch, hi), 0)
                        ck = lax.broadcasted_iota(jnp.int32, (ch, hi), 1)
                        sc = jnp.where(ck <= rq, sc, NEG)
                    fq = fq_scr[hd, rows, :]
                    m_old = m_scr[hd, rows, :]
                    m_new = jnp.maximum(m_old, fq + jnp.max(sc, axis=1, keepdims=True))
                    p_scr[hd, rows, 0:hi] = jnp.exp(sc + (fq - m_new)).astype(BF16)
                    if hi < t:
                        p_scr[hd, rows, hi:t] = jnp.zeros((ch, t - hi), BF16)
                    a_scr[hd, rows, :] = jnp.exp(m_old - m_new)
                    m_scr[hd, rows, :] = m_new
                pv.append(_dot(p_scr[hd], vaug[hd]))
            for pp in range(pairs):
                a0, a1 = a_scr[2 * pp], a_scr[2 * pp + 1]
                alpha = jnp.concatenate([jnp.where(first, a0, a1), jnp.where(lane == 0, a0, a1)], axis=1)
                acc_scr[pp] = acc_scr[pp] * alpha + pv[2 * pp] + pv[2 * pp + 1]
            return 0

        lax.fori_loop(0, i, lambda j, _: step(j, False), 0)
        step(i, True)
        lse = jnp.zeros((t, LANES), F32)
        for pp in range(pairs):
            l0 = acc_scr[pp, :, LANES:LANES + 1]
            l1 = acc_scr[pp, :, LANES + 1:LANES + 2]
            o_ref[:, pp * LANES:(pp + 1) * LANES] = acc_scr[pp, :, 0:LANES] * jnp.where(first, 1.0 / l0, 1.0 / l1)
            lse = jnp.where(lane == 2 * pp, m_scr[2 * pp] + jnp.log(l0), lse)
            lse = jnp.where(lane == 2 * pp + 1, m_scr[2 * pp + 1] + jnp.log(l1), lse)
        lse_ref[...] = lse

    blk = pl.BlockSpec((t, width), lambda g, i: (i, g))
    return pl.pallas_call(
        body, name="attn_fwd", grid=(groups, n),
        in_specs=[blk,
                  pl.BlockSpec((s, width), lambda g, i: (0, groups + g)),
                  pl.BlockSpec((s, width), lambda g, i: (0, 2 * groups + g)),
                  pl.BlockSpec((pairs, 2, n, 1, t), lambda g, i: (g, 0, 0, 0, 0)),
                  blk],
        out_specs=(blk, pl.BlockSpec((t, LANES), lambda g, i: (i, g))),
        out_shape=(jax.ShapeDtypeStruct((s, FOX_W), F32), jax.ShapeDtypeStruct((s, groups * LANES), F32)),
        scratch_shapes=[pltpu.VMEM((2 * pairs, t, t), F32), pltpu.VMEM((2 * pairs, t, t), BF16),
                        pltpu.VMEM((2 * pairs, t, 1), F32), pltpu.VMEM((2 * pairs, t, 1), F32),
                        pltpu.VMEM((2 * pairs, t, 1), F32), pltpu.VMEM((pairs, t, wide), F32)],
        compiler_params=_cparams(("parallel", "arbitrary")))(qkv, qkv, qkv, frow5, fpc)


def _attn_bwd(qkv, do, lse5, dlt5, frow5, fpc):
    s = qkv.shape[0]
    t = min(T_ATT, s)
    n = s // t
    wide = 2 * LANES

    ch = min(ATT_CHUNK, t)

    def body(q_ref, do_ref, k_ref, v_ref, lse_ref, dl_ref, fr_ref, fc_ref,
             dq_ref, dk_ref, dv_ref, dfk_ref, dfq_ref, dq_acc, st_scr, dp_scr, pt_scr, ds_scr, dk_acc, dv_acc, fk_scr):
        j = pl.program_id(1)

        @pl.when(j == 0)
        def _():
            dq_acc[...] = jnp.zeros_like(dq_acc)

        dk_acc[...] = jnp.zeros_like(dk_acc)
        dv_acc[...] = jnp.zeros_like(dv_acc)
        lane = lax.broadcasted_iota(jnp.int32, (t, LANES), 1)
        first = lane < HEAD_DIM
        ones_col = ((lane == 0).astype(BF16), (lane == 1).astype(BF16))
        kb = k_ref[...]
        vb = v_ref[...]
        zk = jnp.zeros_like(kb)
        kaug = (jnp.concatenate([jnp.where(first, kb, zk), ones_col[0]], axis=1),
                jnp.concatenate([jnp.where(first, zk, kb), ones_col[1]], axis=1))
        fk_scr[0] = fc_ref[:, 0:1]
        fk_scr[1] = fc_ref[:, 1:2]

        def step(i, masked):
            r0 = pl.multiple_of(i * t, t)
            qb = _scaled(q_ref[pl.ds(r0, t), :])
            dob = do_ref[pl.ds(r0, t), :]
            zq = jnp.zeros_like(qb)
            qm = (jnp.where(first, qb, zq), jnp.where(first, zq, qb))
            dom = (jnp.where(first, dob, zq), jnp.where(first, zq, dob))
            dq_add = jnp.zeros((t, wide), F32)
            for hh in range(2):
                st_scr[hh] = _dot_nt(kb, qm[hh])
                dp_scr[hh] = _dot_nt(vb, dom[hh])
                bias = fr_ref[0, hh, i] - lse_ref[0, hh, i]
                dl = dl_ref[0, hh, i]
                for c in range(t // ch):
                    rows = pl.ds(c * ch, ch)
                    lo = c * ch // LANES * LANES if masked else 0
                    st = st_scr[hh, rows, lo:t] + (bias[:, lo:t] - fk_scr[hh, rows, :])
                    if masked:
                        rk = c * ch + lax.broadcasted_iota(jnp.int32, (ch, t - lo), 0)
                        cq = lo + lax.broadcasted_iota(jnp.int32, (ch, t - lo), 1)
                        st = jnp.where(rk <= cq, st, NEG)
                    pt = jnp.exp(st)
                    pt_scr[hh, rows, lo:t] = pt.astype(BF16)
                    ds_scr[hh, rows, lo:t] = (pt * (dp_scr[hh, rows, lo:t] - dl[:, lo:t])).astype(BF16)
                    if lo > 0:
                        pt_scr[hh, rows, 0:lo] = jnp.zeros((ch, lo), BF16)
                        ds_scr[hh, rows, 0:lo] = jnp.zeros((ch, lo), BF16)
                dsb = ds_scr[hh]
                dv_acc[...] += _dot(pt_scr[hh], dom[hh])
                dk_acc[...] += _dot(dsb, jnp.concatenate([qm[hh], ones_col[hh]], axis=1))
                dq_add = dq_add + _dot_tn(dsb, kaug[hh])
            dq_acc[pl.ds(r0, t), :] += dq_add
            return 0

        step(j, True)
        lax.fori_loop(j + 1, n, lambda i, _: step(i, False), 0)
        dk_ref[...] = dk_acc[:, 0:LANES].astype(BF16)
        dv_ref[...] = dv_acc[...].astype(BF16)
        dfk_ref[...] = -dk_acc[:, LANES:wide]

        @pl.when(j == n - 1)
        def _():
            dq_ref[...] = (dq_acc[:, 0:LANES] * (HEAD_DIM ** -0.5)).astype(BF16)
            dfq_ref[...] = dq_acc[:, LANES:wide]

    stat = pl.BlockSpec((1, 2, n, 1, t), lambda h, j: (h, 0, 0, 0, 0))
    blk = pl.BlockSpec((t, LANES), lambda h, j: (j, h))
    full = pl.BlockSpec((s, LANES), lambda h, j: (0, h))
    return pl.pallas_call(
        body, name="attn_bwd", grid=(4, n),
        in_specs=[full, full,
                  pl.BlockSpec((t, LANES), lambda h, j: (j, 4 + h)),
                  pl.BlockSpec((t, LANES), lambda h, j: (j, 8 + h)),
                  stat, stat, stat, blk],
        out_specs=(full, blk, blk, blk, full),
        out_shape=(jax.ShapeDtypeStruct((s, FOX_W), BF16), jax.ShapeDtypeStruct((s, FOX_W), BF16),
                   jax.ShapeDtypeStruct((s, FOX_W), BF16), jax.ShapeDtypeStruct((s, 4 * LANES), F32),
                   jax.ShapeDtypeStruct((s, 4 * LANES), F32)),
        scratch_shapes=[pltpu.VMEM((s, wide), F32), pltpu.VMEM((2, t, t), F32), pltpu.VMEM((2, t, t), F32),
                        pltpu.VMEM((2, t, t), BF16), pltpu.VMEM((2, t, t), BF16), pltpu.VMEM((t, wide), F32),
                        pltpu.VMEM((t, LANES), F32), pltpu.VMEM((2, t, 1), F32)],
        compiler_params=_cparams(("parallel", "arbitrary")))(qkv, do, qkv, qkv, lse5, dlt5, frow5, fpc)


def _ssm_block_params(a_re, a_im, log_dt, b_re, b_im):
    dt = jnp.exp(log_dt)[:, None]
    mag = jnp.exp(a_re * dt)
    ar = mag * jnp.cos(a_im * dt)
    ai = mag * jnp.sin(a_im * dt)
    den = a_re * a_re + a_im * a_im
    nr = ar - 1.0
    cr = (nr * a_re + ai * a_im) / den
    ci = (ai * a_re - nr * a_im) / den
    bbr = cr[:, :, None] * b_re - ci[:, :, None] * b_im
    bbi = cr[:, :, None] * b_im + ci[:, :, None] * b_re
    return ar, ai, bbr, bbi


def _block_diag(blocks):
    g, r, c = blocks.shape
    eye = jnp.eye(g, dtype=blocks.dtype)
    return (blocks[:, :, None, :] * eye[:, None, :, None]).reshape(g * r, g * c)


def _scan_consts(a_re, a_im, log_dt, reverse):
    dt = jnp.exp(log_dt)[:, None]
    lr = (a_re * dt).reshape(1, NSTATE)
    li = (a_im * dt).reshape(1, NSTATE)
    if reverse:
        li = -li
    rows = jnp.arange(8, dtype=F32)[:, None]

    def power(k):
        mag = jnp.exp(k * lr)
        return mag * jnp.cos(k * li), mag * jnp.sin(k * li)

    tiles = []
    for k in (1, 2, 4):
        keep = (rows < 8 - k) if reverse else (rows >= k)
        pr, pi_ = power(float(k))
        tiles += [jnp.where(keep, pr, 0.0), jnp.where(keep, pi_, 0.0)]
    expo = (8.0 - rows) if reverse else (rows + 1.0)
    tiles += list(power(expo))
    return jnp.stack([jnp.broadcast_to(tl, (8, NSTATE)) for tl in tiles])


_SCAN_W = 512
_HALF_W = S5_W // 2
_HALF_S = NSTATE // 2


def _compact_diag(blocks_re, blocks_im):
    hg = GROUPS // 2
    return jnp.concatenate([_block_diag(b[h * hg:(h + 1) * hg]) for b in (blocks_re, blocks_im) for h in range(2)],
                           axis=1)


def _half_expand(v, w_ref, out_ref):
    for half in range(2):
        vh = v[:, half * _HALF_W:(half + 1) * _HALF_W]
        for part in range(2):
            c0 = part * NSTATE + half * _HALF_S
            out_ref[:, c0:c0 + _HALF_S] = _dot(vh, w_ref[:, c0:c0 + _HALF_S])


def _half_contract(x_ref, w_ref, half):
    out = None
    for part in range(2):
        r0 = part * NSTATE + half * _HALF_S
        term = _dot_nt(x_ref[:, r0:r0 + _HALF_S].astype(BF16), w_ref[:, r0:r0 + _HALF_S])
        out = term if out is None else out + term
    return out


def _half_outer(v, x_ref, acc_ref):
    for half in range(2):
        vh = v[:, half * _HALF_W:(half + 1) * _HALF_W]
        for part in range(2):
            c0 = part * NSTATE + half * _HALF_S
            acc_ref[:, c0:c0 + _HALF_S] += _dot_tn(vh, x_ref[:, c0:c0 + _HALF_S].astype(BF16))


def _ssm_fwd(rest, bd, cd, consts):
    s = rest.shape[0]
    tb = min(TB_SSM, s)
    ns2 = 2 * NSTATE

    def body(u_ref, bd_ref, cd_ref, cf_ref, y_ref, x_ref, cb_ref):
        @pl.when(pl.program_id(0) == 0)
        def _():
            cb_ref[...] = jnp.zeros_like(cb_ref)

        _half_expand(u_ref[...].astype(BF16), bd_ref, x_ref)

        def tile(ti, _):
            r0 = pl.multiple_of(ti * 8, 8)
            for cc in range(NSTATE // _SCAN_W):
                cr = pl.ds(cc * _SCAN_W, _SCAN_W)
                ci = pl.ds(NSTATE + cc * _SCAN_W, _SCAN_W)
                re = x_ref[pl.ds(r0, 8), cr]
                im = x_ref[pl.ds(r0, 8), ci]
                for n_, k in enumerate((1, 2, 4)):
                    ar = cf_ref[2 * n_, :, cr]
                    ai = cf_ref[2 * n_ + 1, :, cr]
                    sr = pltpu.roll(re, k, 0)
                    si = pltpu.roll(im, k, 0)
                    re, im = re + ar * sr - ai * si, im + ar * si + ai * sr
                pr = cf_ref[6, :, cr]
                pi_ = cf_ref[7, :, cr]
                cbr = cb_ref[:, cr]
                cbi = cb_ref[:, ci]
                re, im = re + pr * cbr - pi_ * cbi, im + pr * cbi + pi_ * cbr
                x_ref[pl.ds(r0, 8), cr] = re
                x_ref[pl.ds(r0, 8), ci] = im
                cb_ref[:, cr] = jnp.broadcast_to(re[7:8, :], (8, _SCAN_W))
                cb_ref[:, ci] = jnp.broadcast_to(im[7:8, :], (8, _SCAN_W))
            return 0

        lax.fori_loop(0, tb // 8, tile, 0)
        for half in range(2):
            y_ref[:, half * _HALF_W:(half + 1) * _HALF_W] = _half_contract(x_ref, cd_ref, half)

    return pl.pallas_call(
        body, name="ssm_fwd", grid=(s // tb,),
        in_specs=[pl.BlockSpec((tb, S5_W), lambda i: (i, R_U // S5_W)), _const((_HALF_W, ns2)), _const((_HALF_W, ns2)),
                  _const((8, 8, NSTATE))],
        out_specs=(pl.BlockSpec((tb, S5_W), lambda i: (i, 0)), pl.BlockSpec((tb, ns2), lambda i: (i, 0))),
        out_shape=(jax.ShapeDtypeStruct((s, S5_W), F32), jax.ShapeDtypeStruct((s, ns2), F32)),
        scratch_shapes=[pltpu.VMEM((8, ns2), F32)],
        compiler_params=_cparams(("arbitrary",)))(rest, bd, cd, consts)


def _ssm_bwd(dys, xs, rest, bd, cd, consts, dskip):
    s = dys.shape[0]
    tb = min(TB_SSM, s)
    nb = s // tb
    ns2 = 2 * NSTATE
    nt = tb // 8

    def body(dy_ref, x_ref, u_ref, bd_ref, cd_ref, cf_ref, dsk_ref, du_ref, gb_ref, gc_ref, da_ref,
             g_ref, cb_ref, acc_b, acc_c):
        step = pl.program_id(0)

        @pl.when(step == 0)
        def _():
            cb_ref[...] = jnp.zeros_like(cb_ref)
            acc_b[...] = jnp.zeros_like(acc_b)
            acc_c[...] = jnp.zeros_like(acc_c)
            da_ref[...] = jnp.zeros_like(da_ref)

        dy = dy_ref[...]
        dyb = dy.astype(BF16)
        _half_expand(dyb, cd_ref, g_ref)
        last_row = lax.broadcasted_iota(jnp.int32, (8, _SCAN_W), 0) == 7

        def tile(tt, _):
            r0 = pl.multiple_of((nt - 1 - tt) * 8, 8)
            for cc in range(NSTATE // _SCAN_W):
                cr = pl.ds(cc * _SCAN_W, _SCAN_W)
                ci = pl.ds(NSTATE + cc * _SCAN_W, _SCAN_W)
                re = g_ref[pl.ds(r0, 8), cr]
                im = g_ref[pl.ds(r0, 8), ci]
                for n_, k in enumerate((1, 2, 4)):
                    ar = cf_ref[2 * n_, :, cr]
                    ai = cf_ref[2 * n_ + 1, :, cr]
                    sr = pltpu.roll(re, 8 - k, 0)
                    si = pltpu.roll(im, 8 - k, 0)
                    re, im = re + ar * sr - ai * si, im + ar * si + ai * sr
                pr = cf_ref[6, :, cr]
                pi_ = cf_ref[7, :, cr]
                cbr = cb_ref[:, cr]
                cbi = cb_ref[:, ci]
                re, im = re + pr * cbr - pi_ * cbi, im + pr * cbi + pi_ * cbr
                g_ref[pl.ds(r0, 8), cr] = re
                g_ref[pl.ds(r0, 8), ci] = im
                gnr = jnp.where(last_row, cbr, pltpu.roll(re, 7, 0))
                gni = jnp.where(last_row, cbi, pltpu.roll(im, 7, 0))
                xr = x_ref[pl.ds(r0, 8), cr]
                xi = x_ref[pl.ds(r0, 8), ci]
                da_ref[:, cr] += gnr * xr + gni * xi
                da_ref[:, ci] += gni * xr - gnr * xi
                cb_ref[:, cr] = jnp.broadcast_to(re[0:1, :], (8, _SCAN_W))
                cb_ref[:, ci] = jnp.broadcast_to(im[0:1, :], (8, _SCAN_W))
            return 0

        lax.fori_loop(0, nt, tile, 0)
        for half in range(2):
            cols = slice(half * _HALF_W, (half + 1) * _HALF_W)
            du_ref[:, cols] = (_half_contract(g_ref, bd_ref, half) + dy[:, cols] * dsk_ref[:, cols]).astype(BF16)
        _half_outer(u_ref[...].astype(BF16), g_ref, acc_b)
        _half_outer(dyb, x_ref, acc_c)

        @pl.when(step == nb - 1)
        def _():
            for g in range(GROUPS):
                src = slice((g % (GROUPS // 2)) * GCH, (g % (GROUPS // 2) + 1) * GCH)
                dst = slice(g * GCH, (g + 1) * GCH)
                for part in range(2):
                    cols = slice(part * NSTATE + g * STATE, part * NSTATE + (g + 1) * STATE)
                    gb_ref[dst, part * STATE:(part + 1) * STATE] = acc_b[src, cols]
                    gc_ref[dst, part * STATE:(part + 1) * STATE] = acc_c[src, cols]

    rev = lambda i: (nb - 1 - i, 0)
    small = pl.BlockSpec((S5_W, 2 * STATE), lambda i: (0, 0))
    return pl.pallas_call(
        body, name="ssm_bwd", grid=(nb,),
        in_specs=[pl.BlockSpec((tb, S5_W), rev), pl.BlockSpec((tb, ns2), rev),
                  pl.BlockSpec((tb, S5_W), lambda i: (nb - 1 - i, R_U // S5_W)),
                  _const((_HALF_W, ns2)), _const((_HALF_W, ns2)), _const((8, 8, NSTATE)), _const((1, S5_W))],
        out_specs=(pl.BlockSpec((tb, S5_W), rev), small, small, pl.BlockSpec((8, ns2), lambda i: (0, 0))),
        out_shape=(jax.ShapeDtypeStruct((s, S5_W), BF16), jax.ShapeDtypeStruct((S5_W, 2 * STATE), F32),
                   jax.ShapeDtypeStruct((S5_W, 2 * STATE), F32), jax.ShapeDtypeStruct((8, ns2), F32)),
        scratch_shapes=[pltpu.VMEM((tb, ns2), F32), pltpu.VMEM((8, ns2), F32),
                        pltpu.VMEM((_HALF_W, ns2), F32), pltpu.VMEM((_HALF_W, ns2), F32)],
        compiler_params=_cparams(("arbitrary",)))(dys, xs, rest, bd, cd, consts, dskip)


_GELU_C = math.sqrt(2.0 / math.pi)
_GELU_A = 0.044715


def _mid(o, rest, ys0, x, tgt, w, vec, hsel):
    s = o.shape[0]
    tm = min(TM, s)
    nsteps = s // tm
    half = FOX_W

    def body(o_ref, ga_ref, gb_ref, za_ref, u_ref, zb_ref, ys0_ref, x_ref, t_ref,
             wglu_ref, wua_ref, wub_ref, wout_ref, vec_ref, hsel_ref,
             dx2_ref, dga_ref, dgb_ref, do_ref, dza_ref, dzb_ref, dys_ref, dlt_ref,
             gout_hbm, gua_hbm, gub_hbm, gglu_hbm, vout_ref,
             a_out, a_ua, a_ub, a_glu):
        step = pl.program_id(0)

        @pl.when(step == 0)
        def _():
            a_out[...] = jnp.zeros_like(a_out)
            a_ua[...] = jnp.zeros_like(a_ua)
            a_ub[...] = jnp.zeros_like(a_ub)
            a_glu[...] = jnp.zeros_like(a_glu)
            vout_ref[...] = jnp.zeros_like(vout_ref)

        gate = vec_ref[0:1, :]
        gfin = vec_ref[1:2, :]
        dsk = vec_ref[2:3, 0:half]
        bglu = vec_ref[2:3, half:2 * half]

        o_v = o_ref[...]
        za = za_ref[...]
        sza = _sigmoid(za)
        silu_za = za * sza
        ya_b = (o_v * silu_za).astype(BF16)
        u_v = u_ref[...]
        ys = ys0_ref[...] + dsk * u_v
        inner = _GELU_C * (ys + _GELU_A * ys * ys * ys)
        th = jnp.tanh(inner)
        yg = 0.5 * ys * (1.0 + th)
        yg_b = yg.astype(BF16)
        st = _sigmoid(_dot(yg_b, wglu_ref[...]) + bglu)
        yb1 = yg * st
        zb = zb_ref[...]
        szb = _sigmoid(zb)
        silu_zb = zb * szb
        yb_b = (yb1 * silu_zb).astype(BF16)
        ua = _dot(ya_b, wua_ref[...])
        ub = _dot(yb_b, wub_ref[...])
        sga = _sigmoid(ga_ref[...])
        sgb = _sigmoid(gb_ref[...])
        merged_b = (sga * ua + sgb * ub).astype(BF16)
        mo = _dot(merged_b, wout_ref[...])
        x2 = x_ref[...] + gate * mo
        r2 = lax.rsqrt(jnp.mean(x2 * x2, axis=-1, keepdims=True) + EPS)
        x2n = x2 * r2
        diff = x2n * gfin - t_ref[...]
        loss = 0.5 * jnp.sum(jnp.mean(diff * diff, axis=-1, keepdims=True), axis=0, keepdims=True)
        dy = diff * (1.0 / D_MODEL)
        dx2n = dy * gfin
        dx2 = r2 * (dx2n - x2n * jnp.mean(dx2n * x2n, axis=-1, keepdims=True))
        dx2_ref[...] = dx2
        vout_ref[0:1, :] += jnp.sum(dy * x2n, axis=0, keepdims=True)
        vout_ref[1:2, :] += jnp.sum(dx2 * mo, axis=0, keepdims=True)
        vout_ref[3:4, :] += jnp.broadcast_to(loss, (1, D_MODEL))
        dmo_b = (dx2 * gate).astype(BF16)
        dmerged = _dot_nt(dmo_b, wout_ref[...])
        a_out[...] += _dot_tn(merged_b, dmo_b)
        dua_b = (dmerged * sga).astype(BF16)
        dub_b = (dmerged * sgb).astype(BF16)
        dga_ref[...] = (dmerged * ua * sga * (1.0 - sga)).astype(BF16)
        dgb_ref[...] = (dmerged * ub * sgb * (1.0 - sgb)).astype(BF16)
        dya = _dot_nt(dua_b, wua_ref[...])
        dyb = _dot_nt(dub_b, wub_ref[...])
        a_ua[...] += _dot_tn(ya_b, dua_b)
        a_ub[...] += _dot_tn(yb_b, dub_b)
        do_b = (dya * silu_za).astype(BF16)
        do_ref[...] = do_b
        dza_ref[...] = (dya * o_v * (sza * (1.0 + za * (1.0 - sza)))).astype(BF16)
        dlt_ref[...] = lax.dot_general(hsel_ref[...], do_b.astype(F32) * o_v, (((1,), (1,)), ((), ())),
                                       preferred_element_type=F32, precision=HI)
        dyb1 = dyb * silu_zb
        dzb_ref[...] = (dyb * yb1 * (szb * (1.0 + zb * (1.0 - szb)))).astype(BF16)
        dt = dyb1 * yg * st * (1.0 - st)
        dt_b = dt.astype(BF16)
        dyg = dyb1 * st + _dot_nt(dt_b, wglu_ref[...])
        a_glu[...] += _dot_tn(yg_b, dt_b)
        dgelu = 0.5 * (1.0 + th) + 0.5 * ys * (1.0 - th * th) * _GELU_C * (1.0 + 3.0 * _GELU_A * ys * ys)
        dys = dyg * dgelu
        dys_ref[...] = dys
        vout_ref[2:3, 0:half] += jnp.sum(dys * u_v, axis=0, keepdims=True)
        vout_ref[2:3, half:2 * half] += jnp.sum(dt, axis=0, keepdims=True)

        @pl.when(step == nsteps - 1)
        def _():
            pltpu.sync_copy(a_out, gout_hbm)
            pltpu.sync_copy(a_ua, gua_hbm)
            pltpu.sync_copy(a_ub, gub_hbm)
            pltpu.sync_copy(a_glu, gglu_hbm)

    def rows(width, col=0):
        return pl.BlockSpec((tm, width), lambda i, col=col: (i, col))

    anyspace = pl.BlockSpec(memory_space=pl.ANY)
    wshapes = [(S5_W, S5_W), (FOX_W, D_MODEL), (S5_W, D_MODEL), (D_MODEL, D_MODEL)]
    return pl.pallas_call(
        body, name="mid", grid=(nsteps,),
        in_specs=[rows(FOX_W), rows(D_MODEL, R_GA // D_MODEL), rows(D_MODEL, R_GB // D_MODEL),
                  rows(FOX_W, R_ZA // FOX_W), rows(S5_W, R_U // S5_W), rows(S5_W, R_ZB // S5_W),
                  rows(S5_W), rows(D_MODEL), rows(D_MODEL)]
                 + [_const(sh) for sh in wshapes]
                 + [_const((8, D_MODEL)), _const((HEADS, FOX_W))],
        out_specs=(rows(D_MODEL), rows(D_MODEL), rows(D_MODEL), rows(FOX_W), rows(FOX_W), rows(S5_W), rows(S5_W),
                   pl.BlockSpec((HEADS, tm), lambda i: (0, i)),
                   anyspace, anyspace, anyspace, anyspace, pl.BlockSpec((8, D_MODEL), lambda i: (0, 0))),
        out_shape=(jax.ShapeDtypeStruct((s, D_MODEL), F32), jax.ShapeDtypeStruct((s, D_MODEL), BF16),
                   jax.ShapeDtypeStruct((s, D_MODEL), BF16), jax.ShapeDtypeStruct((s, FOX_W), BF16),
                   jax.ShapeDtypeStruct((s, FOX_W), BF16), jax.ShapeDtypeStruct((s, S5_W), BF16),
                   jax.ShapeDtypeStruct((s, S5_W), F32), jax.ShapeDtypeStruct((HEADS, s), F32),
                   jax.ShapeDtypeStruct((D_MODEL, D_MODEL), F32), jax.ShapeDtypeStruct((FOX_W, D_MODEL), F32),
                   jax.ShapeDtypeStruct((S5_W, D_MODEL), F32), jax.ShapeDtypeStruct((S5_W, S5_W), F32),
                   jax.ShapeDtypeStruct((8, D_MODEL), F32)),
        scratch_shapes=[pltpu.VMEM((D_MODEL, D_MODEL), F32), pltpu.VMEM((FOX_W, D_MODEL), F32),
                        pltpu.VMEM((S5_W, D_MODEL), F32), pltpu.VMEM((S5_W, S5_W), F32)],
        compiler_params=_cparams(("arbitrary",)),
    )(o, rest, rest, rest, rest, rest, ys0, x, tgt, *w, vec, hsel)


def _dh(dq, dk, dv, dga, dgb, dza, du, dzb, df, wqkv_t, wrest_t, x, dx2, gs, scatter_srcs):
    s = x.shape[0]
    tm = min(TM, s)
    nsteps = s // tm
    na = len(scatter_srcs)

    def body(dq_ref, dk_ref, dv_ref, dga_ref, dgb_ref, dza_ref, du_ref, dzb_ref, df_ref, wq_ref, wr_ref,
             x_ref, dx2_ref, gs_ref, *rest_refs):
        src_refs = rest_refs[:na]
        gx_ref, vout_ref = rest_refs[na:na + 2]
        out_refs = rest_refs[na + 2:2 * na + 2]
        send_sems, recv_sems = rest_refs[2 * na + 2:]
        step = pl.program_id(0)
        cx, cy, cc = lax.axis_index("x"), lax.axis_index("y"), lax.axis_index("c")
        peers = [(1 - cx, cy), (cx, 1 - cy), (1 - cx, 1 - cy)]

        def copy(a, k, px, py, slot):
            return pltpu.make_async_remote_copy(
                src_ref=src_refs[a].at[2 * px + py], dst_ref=out_refs[a].at[slot],
                send_sem=send_sems.at[a * 3 + k], recv_sem=recv_sems.at[a * 3 + k],
                device_id=(px, py, cc), device_id_type=MESH)

        @pl.when(step == 0)
        def _():
            vout_ref[...] = jnp.zeros_like(vout_ref)
            for a in range(na):
                for k, (px, py) in enumerate(peers):
                    copy(a, k, px, py, 2 * cx + cy).start()

        dh = _dot(dq_ref[...], wq_ref[0:512, :])
        dh += _dot(dk_ref[...], wq_ref[512:1024, :])
        dh += _dot(dv_ref[...], wq_ref[1024:1536, :])
        dh += _dot(dga_ref[...], wr_ref[R_GA:R_GB, :])
        dh += _dot(dgb_ref[...], wr_ref[R_GB:R_ZA, :])
        dh += _dot(dza_ref[...], wr_ref[R_ZA:R_U, :])
        dh += _dot(du_ref[...], wr_ref[R_U:R_ZB, :])
        dh += _dot(dzb_ref[...], wr_ref[R_ZB:R_F, :])
        dh += _dot(df_ref[...], wr_ref[R_F:REST_W, :])
        xv = x_ref[...]
        r = lax.rsqrt(jnp.mean(xv * xv, axis=-1, keepdims=True) + EPS)
        xn = xv * r
        dxn = dh * gs_ref[...]
        gx_ref[...] = dx2_ref[...] + r * (dxn - xn * jnp.mean(dxn * xn, axis=-1, keepdims=True))
        vout_ref[0:1, :] += jnp.sum(dh * xn, axis=0, keepdims=True)
        vout_ref[1:2, :] += jnp.sum(dh, axis=0, keepdims=True)

        @pl.when(step == nsteps - 1)
        def _():
            for a in range(na):
                for k, (px, py) in enumerate(peers):
                    copy(a, k, px, py, 2 * px + py).wait_recv()
            for a in range(na):
                for k, (px, py) in enumerate(peers):
                    copy(a, k, px, py, 2 * cx + cy).wait_send()

    def rows(width):
        return pl.BlockSpec((tm, width), lambda i: (i, 0))

    anyspace = pl.BlockSpec(memory_space=pl.ANY)
    return pl.pallas_call(
        body, name="dh", grid=(nsteps,),
        in_specs=[rows(512), rows(512), rows(512), rows(1024), rows(1024), rows(512), rows(512), rows(512), rows(128),
                  _const((1536, D_MODEL)), _const((REST_W, D_MODEL)), rows(D_MODEL), rows(D_MODEL), _const((1, D_MODEL))]
                 + [anyspace] * na,
        out_specs=(rows(D_MODEL), pl.BlockSpec((8, D_MODEL), lambda i: (0, 0))) + (anyspace,) * na,
        out_shape=(jax.ShapeDtypeStruct((s, D_MODEL), F32), jax.ShapeDtypeStruct((8, D_MODEL), F32))
                  + tuple(jax.ShapeDtypeStruct(a.shape, a.dtype) for a in scatter_srcs),
        scratch_shapes=[pltpu.SemaphoreType.DMA((3 * na,)), pltpu.SemaphoreType.DMA((3 * na,))],
        compiler_params=_cparams(("arbitrary",)),
    )(dq, dk, dv, dga, dgb, dza, du, dzb, df, wqkv_t, wrest_t, x, dx2, gs, *scatter_srcs)


def _row_block(rows, mult=8, cap=512):
    if rows <= mult:
        return rows
    padded = -(-rows // mult) * mult
    for cand in range(min(cap, padded) // mult * mult, 0, -mult):
        if padded % cand == 0:
            return cand
    return padded


def _sum4(parts, name):
    rows, cols = parts.shape[1:]
    br = _row_block(rows, 16)

    def body(p_ref, o_ref):
        acc = p_ref[0].astype(F32)
        for k in range(1, 4):
            acc = acc + p_ref[k].astype(F32)
        o_ref[...] = acc

    return pl.pallas_call(
        body, name=name, grid=(pl.cdiv(rows, br),),
        in_specs=[pl.BlockSpec((4, br, cols), lambda i: (0, i, 0))],
        out_specs=pl.BlockSpec((br, cols), lambda i: (i, 0)),
        out_shape=jax.ShapeDtypeStruct((rows, cols), F32), compiler_params=_cparams(("parallel",)))(parts)


def _pair_add(a, b, name):
    shape = a.shape
    a, b = a.reshape(-1, shape[-1]), b.reshape(-1, shape[-1])
    rows, cols = a.shape
    br = _row_block(rows, 16, 1024)

    def body(a_ref, b_ref, o_ref):
        o_ref[...] = (a_ref[...].astype(F32) + b_ref[...].astype(F32)).astype(BF16)

    spec = pl.BlockSpec((br, cols), lambda i: (i, 0))
    return pl.pallas_call(
        body, name=name, grid=(pl.cdiv(rows, br),), in_specs=[spec, spec], out_specs=spec,
        out_shape=jax.ShapeDtypeStruct((rows, cols), BF16), compiler_params=_cparams(("parallel",)))(a, b).reshape(shape)


def _adamw(w, g, m, v, name):
    rows, cols = w.shape
    br = _row_block(rows)

    def body(w_ref, g_ref, m_ref, v_ref, d_ref, nm_ref, nv_ref):
        gv = g_ref[...]
        nm = ADAM_B1 * m_ref[...] + (1.0 - ADAM_B1) * gv
        nv = ADAM_B2 * v_ref[...] + (1.0 - ADAM_B2) * (gv * gv)
        m_hat = nm / (1.0 - ADAM_B1 ** ADAM_STEP)
        v_hat = nv / (1.0 - ADAM_B2 ** ADAM_STEP)
        d_ref[...] = -ADAM_LR * (m_hat / (jnp.sqrt(v_hat) + ADAM_EPS) + ADAM_WD * w_ref[...])
        nm_ref[...] = nm
        nv_ref[...] = nv

    spec = pl.BlockSpec((br, cols), lambda i: (i, 0))
    shape = jax.ShapeDtypeStruct((rows, cols), F32)
    return pl.pallas_call(
        body, name=name, grid=(pl.cdiv(rows, br),), in_specs=[spec] * 4, out_specs=(spec,) * 3,
        out_shape=(shape,) * 3, compiler_params=_cparams(("parallel",)))(w, g, m, v)


def _pack(parts, row_multiple=8):
    flat = []
    for p in parts:
        v = p.reshape(-1).astype(F32)
        pad = (-v.shape[0]) % LANES
        if pad:
            v = jnp.concatenate([v, jnp.zeros((pad,), F32)])
        flat.append(v)
    v = jnp.concatenate(flat)
    rows = v.shape[0] // LANES
    pad_rows = (-rows) % row_multiple
    if pad_rows:
        v = jnp.concatenate([v, jnp.zeros((pad_rows * LANES,), F32)])
    return v.reshape(-1, LANES)


def _unpack(packed, shapes):
    lead = packed.shape[:-2]
    flat = packed.reshape(lead + (-1,))
    out, off = [], 0
    for sh in shapes:
        size = math.prod(sh)
        out.append(flat[..., off:off + size].reshape(lead + tuple(sh)))
        off += size + (-size) % LANES
    return out


def kernel(x, c, w_ada, b_ada, g_norm, w_in, b_f, a_re, a_im, log_dt, b_re, b_im, c_re, c_im, d_skip, w_glu, b_glu, w_up_a, w_up_b, w_out, g_final, loss_target, m_w_ada, m_b_ada, m_g_norm, m_w_in, m_b_f, m_a_re, m_a_im, m_log_dt, m_b_re, m_b_im, m_c_re, m_c_im, m_d_skip, m_w_glu, m_b_glu, m_w_up_a, m_w_up_b, m_w_out, m_g_final, v_w_ada, v_b_ada, v_g_norm, v_w_in, v_b_f, v_a_re, v_a_im, v_log_dt, v_b_re, v_b_im, v_c_re, v_c_im, v_d_skip, v_w_glu, v_b_glu, v_w_up_a, v_w_up_b, v_w_out, v_g_final):
    xi, yi, ci = lax.axis_index("x"), lax.axis_index("y"), lax.axis_index("c")
    chip = 2 * xi + yi
    me = 4 * xi + 2 * yi + ci
    s = x.shape[1]
    x2d = x[0]
    tgt = loss_target[0]
    n_att = s // min(T_ATT, s)
    t_att = min(T_ATT, s)

    c_all, _ = _allgather8(c.reshape(8, LANES), "gather_c")
    c_all = c_all.reshape(8, D_MODEL)
    ncol = w_ada.shape[2]
    b_cols = lax.dynamic_slice_in_dim(b_ada, chip * ncol, ncol, axis=1)
    mod_cols = _mod_cols(c_all, w_ada[0], b_cols)
    mod_all, _ = _allgather8(mod_cols.reshape(-1, LANES), "gather_mod")
    mod_all = mod_all.reshape(4, 2, 8, ncol)[:, 0]
    mod_me = lax.dynamic_index_in_dim(mod_all, me, axis=1, keepdims=False).reshape(1, 3 * D_MODEL)
    shift, scale, gate = mod_me[:, :D_MODEL], mod_me[:, D_MODEL:2 * D_MODEL], mod_me[:, 2 * D_MODEL:]
    gs = g_norm * (1.0 + scale)

    nshard = w_in.shape[2]
    w_in_t, m_in_t, v_in_t = (jnp.swapaxes(a[0], 0, 1) for a in (w_in, m_w_in, v_w_in))
    wt_pack = jnp.pad(w_in_t.astype(BF16), ((0, SHARD_ROWS - nshard), (0, 0)))
    misc_shapes = [w_glu.shape[1:], w_up_a.shape[1:], w_up_b.shape[1:], w_out.shape[1:]]
    misc_pack = jnp.concatenate([w.reshape(-1) for w in (w_glu, w_up_a, w_up_b, w_out)]).astype(BF16).reshape(-1, LANES)
    def halves(a):
        return a.reshape((2, a.shape[0] // 2) + a.shape[1:])

    wt_all, misc_all = _gather_shards([halves(wt_pack), halves(misc_pack)], "gather_weights")
    wt_all = lax.dynamic_update_index_in_dim(wt_all, halves(wt_pack), chip, 0).reshape((4,) + wt_pack.shape)
    misc_all = lax.dynamic_update_index_in_dim(misc_all, halves(misc_pack), chip, 0).reshape((4,) + misc_pack.shape)
    p_glu, p_ua, p_ub, p_out = _unpack(misc_all, misc_shapes)

    def w_rows(lo, hi):
        out = []
        for j in range(4):
            a, b = max(lo, j * nshard), min(hi, (j + 1) * nshard)
            if a < b:
                out.append(wt_all[j, a - j * nshard:b - j * nshard])
        return out

    wqkv_t = jnp.concatenate(w_rows(O_Q, O_F), axis=0)
    wrest_t = jnp.concatenate(w_rows(O_GA, O_GB) + w_rows(O_GB, O_END) + w_rows(O_ZA, O_U) + w_rows(O_U, O_ZB)
                              + w_rows(O_ZB, O_GA) + w_rows(O_F, O_ZA)
                              + [jnp.zeros((REST_W - R_F - HEADS, D_MODEL), BF16)], axis=0)
    wmid = (p_glu.reshape(S5_W, S5_W), jnp.concatenate([p_ua[j] for j in range(4)], axis=1),
            jnp.concatenate([p_ub[j] for j in range(4)], axis=1), p_out.reshape(D_MODEL, D_MODEL))

    h, qkv, rest = _prenorm_proj(x2d, gs, shift, wqkv_t, wrest_t)
    bf128 = jnp.pad(b_f, ((0, 0), (0, LANES - HEADS)))
    selp = _head_pair_selector()
    fpc, f_t = _fcum(rest, bf128, selp)
    frow5 = f_t.reshape(4, 2, n_att, 1, t_att)
    o, lse_pc = _attn_fwd(qkv, frow5, fpc)

    abar_r, abar_i, bb_r, bb_i = _ssm_block_params(a_re[0], a_im[0], log_dt[0], b_re[0], b_im[0])
    bb_rt, bb_it = jnp.swapaxes(bb_r, 1, 2).astype(BF16), jnp.swapaxes(bb_i, 1, 2).astype(BF16)
    cr_b, ci_b = c_re[0].astype(BF16), (-c_im[0]).astype(BF16)
    bd_c, cd_c = _compact_diag(bb_rt, bb_it), _compact_diag(cr_b, ci_b)
    ys0, xs = _ssm_fwd(rest, bd_c, cd_c, _scan_consts(a_re[0], a_im[0], log_dt[0], False))

    vec = jnp.concatenate([gate, g_final.reshape(1, D_MODEL), jnp.concatenate([d_skip, b_glu], axis=1),
                           jnp.zeros((5, D_MODEL), F32)], axis=0)
    hsel = jnp.repeat(jnp.eye(HEADS, dtype=F32), HEAD_DIM, axis=1)
    (dx2, dga, dgb, do, dza, dzb, dys, dlt_t, g_out, g_ua, g_ub, g_glu, vmid) = _mid(
        o, rest, ys0, x2d, tgt, wmid, vec, hsel)

    lse_t = jnp.transpose(lse_pc.reshape(s, 4 // ATT_PAIRS, LANES)[:, :, :2 * ATT_PAIRS], (1, 2, 0))
    lse5 = lse_t.reshape(4, 2, n_att, 1, t_att)
    dlt5 = dlt_t.reshape(4, 2, n_att, 1, t_att)
    dq, dk, dv, dfk, dfq = _attn_bwd(qkv, do, lse5, dlt5, frow5, fpc)
    du, g_bd, g_cdt, da8 = _ssm_bwd(dys, xs, rest, bd_c, cd_c, _scan_consts(a_re[0], a_im[0], log_dt[0], True), d_skip)
    df, dbf8 = _dfcum(dfk, dfq, rest, bf128, selp.T)

    gq, gk, gv, gga, ggb, gza, gu, gzb, gf = _grad_w_rows(h, [dq, dk, dv, dga, dgb, dza, du, dzb, df])
    g_in_t = jnp.concatenate([gq, gk, gv, gf[:HEADS], gza, gu, gzb, gga, ggb], axis=0)

    def shard_cols(g, j):
        n = g.shape[1] // 4
        return g[:, j * n:(j + 1) * n]

    def shard_rows(g, j):
        n = g.shape[0] // 4
        return g[j * n:(j + 1) * n]

    def halves4(a):
        return a.reshape((4, 2, a.shape[1] // 2) + a.shape[2:])

    gt_pack = halves4(jnp.stack([
        jnp.pad(g_in_t[j * nshard:(j + 1) * nshard].astype(BF16), ((0, SHARD_ROWS - nshard), (0, 0)))
        for j in range(4)]))
    gm_pack = halves4(jnp.stack([
        jnp.concatenate([shard_rows(g_glu, j).reshape(-1), shard_cols(g_ua, j).reshape(-1),
                         shard_cols(g_ub, j).reshape(-1), shard_rows(g_out, j).reshape(-1)]).astype(BF16)
        .reshape(-1, LANES) for j in range(4)]))
    recv_in, recv_misc = _swap_sibling([gt_pack, gm_pack], "pair_swap_weight_grads", other_half=True)
    own_in = lax.dynamic_index_in_dim(gt_pack, ci, axis=1, keepdims=False)
    own_misc = lax.dynamic_index_in_dim(gm_pack, ci, axis=1, keepdims=False)
    pair_in = _pair_add(own_in, recv_in, "pair_add_w_in")
    pair_misc = _pair_add(own_misc, recv_misc, "pair_add_misc")

    grad_x, vdh, parts_in, parts_misc = _dh(dq, dk, dv, dga, dgb, dza, du, dzb, df, wqkv_t, wrest_t, x2d, dx2, gs,
                                            [pair_in, pair_misc])
    parts_in = lax.dynamic_update_slice_in_dim(parts_in, lax.dynamic_slice_in_dim(pair_in, chip, 1, 0), chip, 0)
    parts_misc = lax.dynamic_update_slice_in_dim(parts_misc, lax.dynamic_slice_in_dim(pair_misc, chip, 1, 0), chip, 0)
    half_in, half_misc = _sum4(parts_in, "sum4_w_in"), _sum4(parts_misc, "sum4_misc")
    sib_in, sib_misc = _swap_sibling([half_in, half_misc], "swap_weight_grads")

    def both_halves(mine, theirs):
        return jnp.concatenate([jnp.where(ci == 0, mine, theirs), jnp.where(ci == 0, theirs, mine)], axis=0)

    tot_in, tot_misc = both_halves(half_in, sib_in), both_halves(half_misc, sib_misc)
    g_glu_s, g_ua_s, g_ub_s, g_out_s = _unpack(tot_misc, misc_shapes)

    dgs, dshift = vdh[0:1], vdh[1:2]
    dmod = jnp.concatenate([dshift, dgs * g_norm, vmid[1:2]], axis=1)
    da = jnp.sum(da8, axis=0)
    g_bd = g_bd.reshape(GROUPS, GCH, 2 * STATE)
    g_cdt = g_cdt.reshape(GROUPS, GCH, 2 * STATE)
    g_bbr = jnp.swapaxes(g_bd[:, :, :STATE], 1, 2)
    g_bbi = jnp.swapaxes(g_bd[:, :, STATE:], 1, 2)
    g_cre = g_cdt[:, :, :STATE]
    g_cim = -g_cdt[:, :, STATE:]
    small_shapes = [(1,), (3 * D_MODEL,), (D_MODEL,), (HEADS,), (GROUPS, STATE), (GROUPS, STATE),
                    (GROUPS, STATE, GCH), (GROUPS, STATE, GCH), (GROUPS, GCH, STATE), (GROUPS, GCH, STATE),
                    (S5_W,), (S5_W,), (D_MODEL,)]
    small = _pack([vmid[3, 0:1], dmod, dgs * (1.0 + scale), dbf8[0, :HEADS], da[:NSTATE], da[NSTATE:],
                   g_bbr, g_bbi, g_cre, g_cim, vmid[2, :S5_W], vmid[2, S5_W:], vmid[0]])
    small_all, small_sum = _allgather8(small, "gather_small_grads")
    (loss_s, g_b_ada, g_g_norm, g_b_f, g_abr, g_abi, g_bbr_s, g_bbi_s, g_c_re, g_c_im, g_d_skip, g_b_glu,
     g_g_final) = _unpack(small_sum, small_shapes)
    loss = loss_s[0]
    dmod_all = _unpack(small_all, small_shapes)[1]
    dmod_cols = lax.dynamic_slice_in_dim(dmod_all, chip * ncol, ncol, axis=1)
    g_w_ada = _grad_w_ada(c_all, dmod_cols)
    _, ssm_vjp = jax.vjp(_ssm_block_params, a_re[0], a_im[0], log_dt[0], b_re[0], b_im[0])
    g_a_re, g_a_im, g_log_dt, g_b_re, g_b_im = ssm_vjp((g_abr, g_abi, g_bbr_s, g_bbi_s))

    def adam(name, w, g, m, v):
        shape = w.shape
        total = math.prod(shape)
        if len(shape) > 1 and shape[-1] >= LANES:
            cols = shape[-1]
        elif total % LANES == 0:
            cols = LANES
        else:
            cols = total
        two = lambda a: a.reshape(-1, cols)
        d, nm, nv = _adamw(two(w), two(g), two(m), two(v), "adamw_" + name)
        return g.reshape(shape), d.reshape(shape), nm.reshape(shape), nv.reshape(shape)

    back = lambda a: jnp.swapaxes(a, 0, 1)[None]
    d_in_t, nm_in_t, nv_in_t = _adamw(w_in_t, tot_in, m_in_t, v_in_t, "adamw_w_in")
    res_w_in = (back(tot_in[:nshard]), back(d_in_t), back(nm_in_t), back(nv_in_t))

    res = [
        adam("w_ada", w_ada, g_w_ada, m_w_ada, v_w_ada),
        adam("b_ada", b_ada, g_b_ada, m_b_ada, v_b_ada),
        adam("g_norm", g_norm, g_g_norm, m_g_norm, v_g_norm),
        res_w_in,
        adam("b_f", b_f, g_b_f, m_b_f, v_b_f),
        adam("a_re", a_re, g_a_re, m_a_re, v_a_re),
        adam("a_im", a_im, g_a_im, m_a_im, v_a_im),
        adam("log_dt", log_dt, g_log_dt, m_log_dt, v_log_dt),
        adam("b_re", b_re, g_b_re, m_b_re, v_b_re),
        adam("b_im", b_im, g_b_im, m_b_im, v_b_im),
        adam("c_re", c_re, g_c_re, m_c_re, v_c_re),
        adam("c_im", c_im, g_c_im, m_c_im, v_c_im),
        adam("d_skip", d_skip, g_d_skip, m_d_skip, v_d_skip),
        adam("w_glu", w_glu, g_glu_s, m_w_glu, v_w_glu),
        adam("b_glu", b_glu, g_b_glu, m_b_glu, v_b_glu),
        adam("w_up_a", w_up_a, g_ua_s, m_w_up_a, v_w_up_a),
        adam("w_up_b", w_up_b, g_ub_s, m_w_up_b, v_w_up_b),
        adam("w_out", w_out, g_out_s, m_w_out, v_w_out),
        adam("g_final", g_final, g_g_final, m_g_final, v_g_final),
    ]
    grads = [r[0] for r in res]
    deltas = [r[1] for r in res]
    new_m = [r[2] for r in res]
    new_v = [r[3] for r in res]
    return (loss, grad_x[None], *grads, *deltas, *new_m, *new_v)
```

```python
import functools
import math

import jax
import jax.numpy as jnp
from jax import lax
from jax.experimental import pallas as pl
from jax.experimental.pallas import tpu as pltpu

F32 = jnp.float32
BF16 = jnp.bfloat16
HI = lax.Precision.HIGHEST
MESH = pl.DeviceIdType.MESH

D_MODEL = 1024
HEADS = 8
HEAD_DIM = 64
FOX_W = 512
S5_W = 512
GROUPS = 32
STATE = 64
GCH = 16
NSTATE = GROUPS * STATE
EPS = 1e-6
NEG = -1e30

ADAM_LR = 0.001
ADAM_B1 = 0.9
ADAM_B2 = 0.999
ADAM_EPS = 1e-08
ADAM_WD = 0.01
ADAM_STEP = 10

VMEM_LIMIT = 56 * 1024 * 1024
LANES = 128

TM = 256
T_ATT = 512
ATT_CHUNK = 32
ATT_PAIRS = 4
TB_SSM = 256
TK_ACC = 512
TB_CUM = 256
SHARD_ROWS = 1312

O_Q, O_K, O_V, O_F, O_ZA, O_U, O_ZB, O_GA, O_GB, O_END = 0, 512, 1024, 1536, 1544, 2056, 2568, 3080, 4104, 5128
REST_W = 3712
R_GA, R_GB, R_ZA, R_U, R_ZB, R_F = 0, 1024, 2048, 2560, 3072, 3584


def _cparams(sem=None):
    kw = dict(vmem_limit_bytes=VMEM_LIMIT)
    if sem is not None:
        kw["dimension_semantics"] = sem
    return pltpu.CompilerParams(**kw)


def _const(shape):
    nd = len(shape)
    return pl.BlockSpec(shape, lambda *_: (0,) * nd, pipeline_mode=pl.Buffered(1))


def _dot(a, b, precision=None):
    return jnp.dot(a, b, preferred_element_type=F32, precision=precision)


def _dot_nt(a, b):
    return lax.dot_general(a, b, (((1,), (1,)), ((), ())), preferred_element_type=F32)


def _dot_tn(a, b, precision=None):
    return lax.dot_general(a, b, (((0,), (0,)), ((), ())), preferred_element_type=F32, precision=precision)


def _sigmoid(z):
    return 1.0 / (1.0 + jnp.exp(-z))


def _allgather8(xs, name):
    rows = xs.shape[0]

    def body(x_ref, out_ref, sum_ref, send_sems, recv_sems, local_sem):
        x, y, c = lax.axis_index("x"), lax.axis_index("y"), lax.axis_index("c")
        me, sibling = (x, y, c), (x, y, 1 - c)
        chips = [(1 - x, y), (x, 1 - y), (1 - x, 1 - y)]

        def slot(px, py, pc):
            return out_ref.at[4 * px + 2 * py + pc]

        def copy(k, block, to, src=None):
            return pltpu.make_async_remote_copy(
                src_ref=slot(*block) if src is None else src, dst_ref=slot(*block),
                send_sem=send_sems.at[k], recv_sem=recv_sems.at[k], device_id=to, device_id_type=MESH)

        mine = pltpu.make_async_copy(x_ref, slot(*me), local_sem)
        mine.start()
        first = [copy(0, me, sibling, src=x_ref)]
        first += [copy(1 + j, me, (*chip, c), src=x_ref) for j, chip in enumerate(chips)]
        for cp in first:
            cp.start()
        passed = [copy(4 + j, (*chip, c), sibling) for j, chip in enumerate(chips)]
        for j, chip in enumerate(chips):
            copy(1 + j, (*chip, c), me).wait_recv()
            passed[j].start()
        copy(0, sibling, me).wait_recv()
        for j, chip in enumerate(chips):
            copy(4 + j, (*chip, 1 - c), me).wait_recv()
        for cp in first + passed:
            cp.wait_send()
        mine.wait()
        acc = out_ref[0]
        for d in range(1, 8):
            acc = acc + out_ref[d]
        sum_ref[...] = acc

    return pl.pallas_call(
        body, name=name,
        out_shape=(jax.ShapeDtypeStruct((8, rows, LANES), F32), jax.ShapeDtypeStruct((rows, LANES), F32)),
        in_specs=[pl.BlockSpec(memory_space=pltpu.VMEM)],
        out_specs=(pl.BlockSpec(memory_space=pltpu.VMEM), pl.BlockSpec(memory_space=pltpu.VMEM)),
        scratch_shapes=[pltpu.SemaphoreType.DMA((7,)), pltpu.SemaphoreType.DMA((7,)), pltpu.SemaphoreType.DMA],
        compiler_params=_cparams(),
    )(xs)


def _gather_shards(srcs, name):
    na = len(srcs)

    def body(*refs):
        src_refs, out_refs = refs[:na], refs[na:2 * na]
        send_sems, recv_sems = refs[2 * na:]
        x, y, c = lax.axis_index("x"), lax.axis_index("y"), lax.axis_index("c")
        sibling = (x, y, 1 - c)
        peers = [(1 - x, y), (x, 1 - y), (1 - x, 1 - y)]

        def copy(a, k, src, slot, which, to):
            return pltpu.make_async_remote_copy(
                src_ref=src, dst_ref=out_refs[a].at[slot, which],
                send_sem=send_sems.at[a * 6 + k], recv_sem=recv_sems.at[a * 6 + k],
                device_id=to, device_id_type=MESH)

        mine = 2 * x + y
        first = [copy(a, k, src_refs[a].at[c], mine, c, (px, py, c))
                 for a in range(na) for k, (px, py) in enumerate(peers)]
        for cp in first:
            cp.start()
        passed = []
        for a in range(na):
            for k, (px, py) in enumerate(peers):
                slot = 2 * px + py
                landed = out_refs[a].at[slot, c]
                copy(a, k, landed, slot, c, (px, py, c)).wait_recv()
                fwd = copy(a, 3 + k, landed, slot, c, sibling)
                fwd.start()
                passed.append(fwd)
        for a in range(na):
            for k, (px, py) in enumerate(peers):
                slot = 2 * px + py
                copy(a, 3 + k, out_refs[a].at[slot, 1 - c], slot, 1 - c, sibling).wait_recv()
        for cp in first + passed:
            cp.wait_send()

    anyspace = pl.BlockSpec(memory_space=pl.ANY)
    return pl.pallas_call(
        body, name=name,
        out_shape=tuple(jax.ShapeDtypeStruct((4,) + tuple(a.shape), a.dtype) for a in srcs),
        in_specs=[anyspace] * na, out_specs=(anyspace,) * na,
        scratch_shapes=[pltpu.SemaphoreType.DMA((6 * na,)), pltpu.SemaphoreType.DMA((6 * na,))],
        compiler_params=_cparams(),
    )(*srcs)


def _swap_sibling(srcs, name, other_half=False):
    na = len(srcs)

    def body(*refs):
        src_refs, out_refs = refs[:na], refs[na:2 * na]
        send_sems, recv_sems = refs[2 * na:]
        x, y, c = lax.axis_index("x"), lax.axis_index("y"), lax.axis_index("c")
        copies = [pltpu.make_async_remote_copy(
            src_ref=src_refs[a].at[:, 1 - c] if other_half else src_refs[a],
            dst_ref=out_refs[a], send_sem=send_sems.at[a], recv_sem=recv_sems.at[a],
            device_id=(x, y, 1 - c), device_id_type=MESH) for a in range(na)]
        for cp in copies:
            cp.start()
        for cp in copies:
            cp.wait()

    def out_of(a):
        shape = (a.shape[0],) + tuple(a.shape[2:]) if other_half else a.shape
        return jax.ShapeDtypeStruct(shape, a.dtype)

    anyspace = pl.BlockSpec(memory_space=pl.ANY)
    return pl.pallas_call(
        body, name=name, out_shape=tuple(out_of(a) for a in srcs),
        in_specs=[anyspace] * na, out_specs=(anyspace,) * na,
        scratch_shapes=[pltpu.SemaphoreType.DMA((na,)), pltpu.SemaphoreType.DMA((na,))],
        compiler_params=_cparams(),
    )(*srcs)


def _mod_cols(c_all, w, b):
    n = w.shape[1]

    def body(c_ref, w_ref, b_ref, o_ref):
        o_ref[...] = _dot(c_ref[...], w_ref[...], HI) + b_ref[...]

    return pl.pallas_call(
        body, name="mod_cols", out_shape=jax.ShapeDtypeStruct((8, n), F32),
        compiler_params=_cparams())(c_all, w, b)


def _grad_w_ada(c_all, dmod_cols):
    n = dmod_cols.shape[1]

    def body(c_ref, d_ref, o_ref):
        o_ref[...] = _dot_tn(c_ref[...], d_ref[...], HI)

    return pl.pallas_call(
        body, name="grad_w_ada", out_shape=jax.ShapeDtypeStruct((D_MODEL, n), F32),
        compiler_params=_cparams())(c_all, dmod_cols)


def _prenorm_proj(x, gs, shift, wqkv_t, wrest_t):
    s = x.shape[0]
    tm = min(TM, s)
    nq, nr = wqkv_t.shape[0], wrest_t.shape[0]

    def body(x_ref, gs_ref, sh_ref, wq_ref, wr_ref, h_ref, qkv_ref, rest_ref):
        xv = x_ref[...]
        r = lax.rsqrt(jnp.mean(xv * xv, axis=-1, keepdims=True) + EPS)
        h = (xv * r * gs_ref[...] + sh_ref[...]).astype(BF16)
        h_ref[...] = h
        qkv_ref[...] = _dot_nt(h, wq_ref[...]).astype(BF16)
        rest_ref[...] = _dot_nt(h, wr_ref[...])

    def rows(width):
        return pl.BlockSpec((tm, width), lambda i: (i, 0))

    return pl.pallas_call(
        body, name="prenorm_proj", grid=(s // tm,),
        in_specs=[rows(D_MODEL), _const((1, D_MODEL)), _const((1, D_MODEL)), _const((nq, D_MODEL)),
                  _const((nr, D_MODEL))],
        out_specs=(rows(D_MODEL), rows(nq), rows(nr)),
        out_shape=(jax.ShapeDtypeStruct((s, D_MODEL), BF16), jax.ShapeDtypeStruct((s, nq), BF16),
                   jax.ShapeDtypeStruct((s, nr), F32)),
        compiler_params=_cparams(("parallel",)))(x, gs, shift, wqkv_t, wrest_t)


def _grad_w_rows(h, ds):
    s = h.shape[0]
    tk = min(TK_ACC, s)
    nd = len(ds)
    widths = [d.shape[1] for d in ds]

    def body(*refs):
        h_ref, d_refs = refs[0], refs[1:1 + nd]
        out_refs, accs = refs[1 + nd:1 + 2 * nd], refs[1 + 2 * nd:]
        step = pl.program_id(0)

        @pl.when(step == 0)
        def _():
            for acc in accs:
                acc[...] = jnp.zeros_like(acc)

        hv = h_ref[...]
        for d_ref, acc in zip(d_refs, accs):
            acc[...] += _dot_tn(d_ref[...], hv)

        @pl.when(step == s // tk - 1)
        def _():
            for acc, out in zip(accs, out_refs):
                pltpu.sync_copy(acc, out)

    anyspace = pl.BlockSpec(memory_space=pl.ANY)
    return pl.pallas_call(
        body, name="grad_w_in", grid=(s // tk,),
        in_specs=[pl.BlockSpec((tk, D_MODEL), lambda k: (k, 0))]
                 + [pl.BlockSpec((tk, w), lambda k: (k, 0)) for w in widths],
        out_specs=(anyspace,) * nd,
        out_shape=tuple(jax.ShapeDtypeStruct((w, D_MODEL), F32) for w in widths),
        scratch_shapes=[pltpu.VMEM((w, D_MODEL), F32) for w in widths],
        compiler_params=_cparams(("arbitrary",)))(h, *ds)


def _head_pair_selector():
    rows = jnp.arange(LANES)[:, None]
    cols = jnp.arange(4 * LANES)[None, :]
    return ((rows < HEADS) & (cols == (rows // 2) * LANES + rows % 2)).astype(F32)


def _fcum(rest, bf128, selp):
    s = rest.shape[0]
    tb = min(TB_CUM, s)

    def body(fz_ref, bf_ref, sel_ref, fpc_ref, ft_ref, carry_ref):
        @pl.when(pl.program_id(0) == 0)
        def _():
            carry_ref[...] = jnp.zeros_like(carry_ref)

        z = fz_ref[...] + bf_ref[...]
        logf = jnp.minimum(z, 0.0) - jnp.log(1.0 + jnp.exp(-jnp.abs(z)))
        r = lax.broadcasted_iota(jnp.int32, (tb, tb), 0)
        c = lax.broadcasted_iota(jnp.int32, (tb, tb), 1)
        tri = (c <= r).astype(F32)
        f = _dot(tri, logf, HI) + carry_ref[0:1, :]
        carry_ref[0:1, :] = f[tb - 1:tb, :]
        fpc_ref[...] = _dot(f, sel_ref[...], HI)
        ft_ref[...] = jnp.transpose(f)[0:HEADS, :]

    return pl.pallas_call(
        body, name="forget_cumsum", grid=(s // tb,),
        in_specs=[pl.BlockSpec((tb, LANES), lambda i: (i, R_F // LANES)), _const((1, LANES)), _const((LANES, 4 * LANES))],
        out_specs=(pl.BlockSpec((tb, 4 * LANES), lambda i: (i, 0)), pl.BlockSpec((HEADS, tb), lambda i: (0, i))),
        out_shape=(jax.ShapeDtypeStruct((s, 4 * LANES), F32), jax.ShapeDtypeStruct((HEADS, s), F32)),
        scratch_shapes=[pltpu.VMEM((8, LANES), F32)],
        compiler_params=_cparams(("arbitrary",)))(rest, bf128, selp)


def _dfcum(dfk, dfq, rest, bf128, selq):
    s = rest.shape[0]
    tb = min(TB_CUM, s)
    nb = s // tb

    def body(dk_ref, dq_ref, fz_ref, bf_ref, sel_ref, df_ref, dbf_ref, carry_ref):
        @pl.when(pl.program_id(0) == 0)
        def _():
            carry_ref[...] = jnp.zeros_like(carry_ref)
            dbf_ref[...] = jnp.zeros_like(dbf_ref)

        d = _dot(dk_ref[...] + dq_ref[...], sel_ref[...], HI)
        r = lax.broadcasted_iota(jnp.int32, (tb, tb), 0)
        c = lax.broadcasted_iota(jnp.int32, (tb, tb), 1)
        triu = (c >= r).astype(F32)
        dlogf = _dot(triu, d, HI) + carry_ref[0:1, :]
        carry_ref[0:1, :] = dlogf[0:1, :]
        z = fz_ref[...] + bf_ref[...]
        df = dlogf * (1.0 / (1.0 + jnp.exp(z)))
        df_ref[...] = df.astype(BF16)
        dbf_ref[0:1, :] += jnp.sum(df, axis=0, keepdims=True)

    return pl.pallas_call(
        body, name="forget_grad", grid=(nb,),
        in_specs=[pl.BlockSpec((tb, 4 * LANES), lambda i: (nb - 1 - i, 0)),
                  pl.BlockSpec((tb, 4 * LANES), lambda i: (nb - 1 - i, 0)),
                  pl.BlockSpec((tb, LANES), lambda i: (nb - 1 - i, R_F // LANES)),
                  _const((1, LANES)), _const((4 * LANES, LANES))],
        out_specs=(pl.BlockSpec((tb, LANES), lambda i: (nb - 1 - i, 0)), pl.BlockSpec((8, LANES), lambda i: (0, 0))),
        out_shape=(jax.ShapeDtypeStruct((s, LANES), BF16), jax.ShapeDtypeStruct((8, LANES), F32)),
        scratch_shapes=[pltpu.VMEM((8, LANES), F32)],
        compiler_params=_cparams(("arbitrary",)))(dfk, dfq, rest, bf128, selq)


def _scaled(q):
    return (q.astype(F32) * (HEAD_DIM ** -0.5)).astype(BF16)


def _attn_fwd(qkv, frow5, fpc):
    s = qkv.shape[0]
    t = min(T_ATT, s)
    n = s // t
    ch = min(ATT_CHUNK, t)
    wide = 2 * LANES
    pairs = ATT_PAIRS
    width = pairs * LANES
    groups = 4 // pairs

    def body(q_ref, k_ref, v_ref, fr_ref, fc_ref, o_ref, lse_ref, s_scr, p_scr, m_scr, a_scr, fq_scr, acc_scr):
        i = pl.program_id(1)
        lane = lax.broadcasted_iota(jnp.int32, (t, LANES), 1)
        first = lane < HEAD_DIM
        ones_col = ((lane == 0).astype(BF16), (lane == 1).astype(BF16))
        m_scr[...] = jnp.full(m_scr.shape, NEG, F32)
        acc_scr[...] = jnp.zeros_like(acc_scr)
        qm = []
        for pp in range(pairs):
            q = _scaled(q_ref[:, pp * LANES:(pp + 1) * LANES])
            zq = jnp.zeros_like(q)
            qm += [jnp.where(first, q, zq), jnp.where(first, zq, q)]
            fq_scr[2 * pp] = fc_ref[:, pp * LANES:pp * LANES + 1]
            fq_scr[2 * pp + 1] = fc_ref[:, pp * LANES + 1:pp * LANES + 2]

        def step(j, masked):
            r0 = pl.multiple_of(j * t, t)
            vaug = []
            for pp in range(pairs):
                kb = k_ref[pl.ds(r0, t), pp * LANES:(pp + 1) * LANES]
                vb = v_ref[pl.ds(r0, t), pp * LANES:(pp + 1) * LANES]
                zv = jnp.zeros_like(vb)
                vaug += [jnp.concatenate([jnp.where(first, vb, zv), ones_col[0]], axis=1),
                         jnp.concatenate([jnp.where(first, zv, vb), ones_col[1]], axis=1)]
                for hh in range(2):
                    s_scr[2 * pp + hh] = _dot_nt(qm[2 * pp + hh], kb)
            pv = []
            for hd in range(2 * pairs):
                fk = fr_ref[hd // 2, hd % 2, j]
                for c in range(t // ch):
                    rows = pl.ds(c * ch, ch)
                    hi = min(t, (c * ch // LANES + 1) * LANES) if masked else t
                    sc = s_scr[hd, rows, 0:hi] - fk[:, 0:hi]
                    if masked:
                        rq = c * ch + lax.broadcasted_iota(jnp.int32, (ch, hi), 0)
                        ck = lax.broadcasted_iota(jnp.int32, (ch, hi), 1)
                        sc = jnp.where(ck <= rq, sc, NEG)
                    fq = fq_scr[hd, rows, :]
                    m_old = m_scr[hd, rows, :]
                    m_new = jnp.maximum(m_old, fq + jnp.max(sc, axis=1, keepdims=True))
                    p_scr[hd, rows, 0:hi] = jnp.exp(sc + (fq - m_new)).astype(BF16)
                    if hi < t:
                        p_scr[hd, rows, hi:t] = jnp.zeros((ch, t - hi), BF16)
                    a_scr[hd, rows, :] = jnp.exp(m_old - m_new)
                    m_scr[hd, rows, :] = m_new
                pv.append(_dot(p_scr[hd], vaug[hd]))
            for pp in range(pairs):
                a0, a1 = a_scr[2 * pp], a_scr[2 * pp + 1]
                alpha = jnp.concatenate([jnp.where(first, a0, a1), jnp.where(lane == 0, a0, a1)], axis=1)
                acc_scr[pp] = acc_scr[pp] * alpha + pv[2 * pp] + pv[2 * pp + 1]
            return 0

        lax.fori_loop(0, i, lambda j, _: step(j, False), 0)
        step(i, True)
        lse = jnp.zeros((t, LANES), F32)
        for pp in range(pairs):
            l0 = acc_scr[pp, :, LANES:LANES + 1]
            l1 = acc_scr[pp, :, LANES + 1:LANES + 2]
            o_ref[:, pp * LANES:(pp + 1) * LANES] = acc_scr[pp, :, 0:LANES] * jnp.where(first, 1.0 / l0, 1.0 / l1)
            lse = jnp.where(lane == 2 * pp, m_scr[2 * pp] + jnp.log(l0), lse)
            lse = jnp.where(lane == 2 * pp + 1, m_scr[2 * pp + 1] + jnp.log(l1), lse)
        lse_ref[...] = lse

    blk = pl.BlockSpec((t, width), lambda g, i: (i, g))
    return pl.pallas_call(
        body, name="attn_fwd", grid=(groups, n),
        in_specs=[blk,
                  pl.BlockSpec((s, width), lambda g, i: (0, groups + g)),
                  pl.BlockSpec((s, width), lambda g, i: (0, 2 * groups + g)),
                  pl.BlockSpec((pairs, 2, n, 1, t), lambda g, i: (g, 0, 0, 0, 0)),
                  blk],
        out_specs=(blk, pl.BlockSpec((t, LANES), lambda g, i: (i, g))),
        out_shape=(jax.ShapeDtypeStruct((s, FOX_W), F32), jax.ShapeDtypeStruct((s, groups * LANES), F32)),
        scratch_shapes=[pltpu.VMEM((2 * pairs, t, t), F32), pltpu.VMEM((2 * pairs, t, t), BF16),
                        pltpu.VMEM((2 * pairs, t, 1), F32), pltpu.VMEM((2 * pairs, t, 1), F32),
                        pltpu.VMEM((2 * pairs, t, 1), F32), pltpu.VMEM((pairs, t, wide), F32)],
        compiler_params=_cparams(("parallel", "arbitrary")))(qkv, qkv, qkv, frow5, fpc)


def _attn_bwd(qkv, do, lse5, dlt5, frow5, fpc):
    s = qkv.shape[0]
    t = min(T_ATT, s)
    n = s // t
    wide = 2 * LANES

    ch = min(ATT_CHUNK, t)

    def body(q_ref, do_ref, k_ref, v_ref, lse_ref, dl_ref, fr_ref, fc_ref,
             dq_ref, dk_ref, dv_ref, dfk_ref, dfq_ref, dq_acc, st_scr, dp_scr, pt_scr, ds_scr, dk_acc, dv_acc, fk_scr):
        j = pl.program_id(1)

        @pl.when(j == 0)
        def _():
            dq_acc[...] = jnp.zeros_like(dq_acc)

        dk_acc[...] = jnp.zeros_like(dk_acc)
        dv_acc[...] = jnp.zeros_like(dv_acc)
        lane = lax.broadcasted_iota(jnp.int32, (t, LANES), 1)
        first = lane < HEAD_DIM
        ones_col = ((lane == 0).astype(BF16), (lane == 1).astype(BF16))
        kb = k_ref[...]
        vb = v_ref[...]
        zk = jnp.zeros_like(kb)
        kaug = (jnp.concatenate([jnp.where(first, kb, zk), ones_col[0]], axis=1),
                jnp.concatenate([jnp.where(first, zk, kb), ones_col[1]], axis=1))
        fk_scr[0] = fc_ref[:, 0:1]
        fk_scr[1] = fc_ref[:, 1:2]

        def step(blocks, masked):
            chains = []
            for bi, i in enumerate(blocks):
                r0 = pl.multiple_of(i * t, t)
                qb = _scaled(q_ref[pl.ds(r0, t), :])
                dob = do_ref[pl.ds(r0, t), :]
                zq = jnp.zeros_like(qb)
                qm = (jnp.where(first, qb, zq), jnp.where(first, zq, qb))
                dom = (jnp.where(first, dob, zq), jnp.where(first, zq, dob))
                for hh in range(2):
                    st_scr[2 * bi + hh] = _dot_nt(kb, qm[hh])
                    dp_scr[2 * bi + hh] = _dot_nt(vb, dom[hh])
                    chains.append((i, hh, qm[hh], dom[hh]))
            dq_add = [jnp.zeros((t, wide), F32) for _ in blocks]
            for cn, (i, hh, qmh, domh) in enumerate(chains):
                bias = fr_ref[0, hh, i] - lse_ref[0, hh, i]
                dl = dl_ref[0, hh, i]
                for c in range(t // ch):
                    rows = pl.ds(c * ch, ch)
                    lo = c * ch // LANES * LANES if masked else 0
                    st = st_scr[cn, rows, lo:t] + (bias[:, lo:t] - fk_scr[hh, rows, :])
                    if masked:
                        rk = c * ch + lax.broadcasted_iota(jnp.int32, (ch, t - lo), 0)
                        cq = lo + lax.broadcasted_iota(jnp.int32, (ch, t - lo), 1)
                        st = jnp.where(rk <= cq, st, NEG)
                    pt = jnp.exp(st)
                    pt_scr[cn, rows, lo:t] = pt.astype(BF16)
                    ds_scr[cn, rows, lo:t] = (pt * (dp_scr[cn, rows, lo:t] - dl[:, lo:t])).astype(BF16)
                    if lo > 0:
                        pt_scr[cn, rows, 0:lo] = jnp.zeros((ch, lo), BF16)
                        ds_scr[cn, rows, 0:lo] = jnp.zeros((ch, lo), BF16)
                dsb = ds_scr[cn]
                dv_acc[...] += _dot(pt_scr[cn], domh)
                dk_acc[...] += _dot(dsb, jnp.concatenate([qmh, ones_col[hh]], axis=1))
                dq_add[cn // 2] = dq_add[cn // 2] + _dot_tn(dsb, kaug[hh])
            for bi, i in enumerate(blocks):
                dq_acc[pl.ds(pl.multiple_of(i * t, t), t), :] += dq_add[bi]
            return 0

        step([j], True)
        odd = (n - 1 - j) % 2
        lax.fori_loop(0, odd, lambda _, carry: step([j + 1], False), 0)
        first_pair = j + 1 + odd
        lax.fori_loop(0, (n - first_pair) // 2,
                      lambda p, _: step([first_pair + 2 * p, first_pair + 2 * p + 1], False), 0)
        dk_ref[...] = dk_acc[:, 0:LANES].astype(BF16)
        dv_ref[...] = dv_acc[...].astype(BF16)
        dfk_ref[...] = -dk_acc[:, LANES:wide]

        @pl.when(j == n - 1)
        def _():
            dq_ref[...] = (dq_acc[:, 0:LANES] * (HEAD_DIM ** -0.5)).astype(BF16)
            dfq_ref[...] = dq_acc[:, LANES:wide]

    stat = pl.BlockSpec((1, 2, n, 1, t), lambda h, j: (h, 0, 0, 0, 0))
    blk = pl.BlockSpec((t, LANES), lambda h, j: (j, h))
    full = pl.BlockSpec((s, LANES), lambda h, j: (0, h))
    return pl.pallas_call(
        body, name="attn_bwd", grid=(4, n),
        in_specs=[full, full,
                  pl.BlockSpec((t, LANES), lambda h, j: (j, 4 + h)),
                  pl.BlockSpec((t, LANES), lambda h, j: (j, 8 + h)),
                  stat, stat, stat, blk],
        out_specs=(full, blk, blk, blk, full),
        out_shape=(jax.ShapeDtypeStruct((s, FOX_W), BF16), jax.ShapeDtypeStruct((s, FOX_W), BF16),
                   jax.ShapeDtypeStruct((s, FOX_W), BF16), jax.ShapeDtypeStruct((s, 4 * LANES), F32),
                   jax.ShapeDtypeStruct((s, 4 * LANES), F32)),
        scratch_shapes=[pltpu.VMEM((s, wide), F32), pltpu.VMEM((4, t, t), F32), pltpu.VMEM((4, t, t), F32),
                        pltpu.VMEM((4, t, t), BF16), pltpu.VMEM((4, t, t), BF16), pltpu.VMEM((t, wide), F32),
                        pltpu.VMEM((t, LANES), F32), pltpu.VMEM((2, t, 1), F32)],
        compiler_params=_cparams(("parallel", "arbitrary")))(qkv, do, qkv, qkv, lse5, dlt5, frow5, fpc)


def _ssm_block_params(a_re, a_im, log_dt, b_re, b_im):
    dt = jnp.exp(log_dt)[:, None]
    mag = jnp.exp(a_re * dt)
    ar = mag * jnp.cos(a_im * dt)
    ai = mag * jnp.sin(a_im * dt)
    den = a_re * a_re + a_im * a_im
    nr = ar - 1.0
    cr = (nr * a_re + ai * a_im) / den
    ci = (ai * a_re - nr * a_im) / den
    bbr = cr[:, :, None] * b_re - ci[:, :, None] * b_im
    bbi = cr[:, :, None] * b_im + ci[:, :, None] * b_re
    return ar, ai, bbr, bbi


def _block_diag(blocks):
    g, r, c = blocks.shape
    eye = jnp.eye(g, dtype=blocks.dtype)
    return (blocks[:, :, None, :] * eye[:, None, :, None]).reshape(g * r, g * c)


def _scan_consts(a_re, a_im, log_dt, reverse):
    dt = jnp.exp(log_dt)[:, None]
    lr = (a_re * dt).reshape(1, NSTATE)
    li = (a_im * dt).reshape(1, NSTATE)
    if reverse:
        li = -li
    rows = jnp.arange(8, dtype=F32)[:, None]

    def power(k):
        mag = jnp.exp(k * lr)
        return mag * jnp.cos(k * li), mag * jnp.sin(k * li)

    tiles = []
    for k in (1, 2, 4):
        keep = (rows < 8 - k) if reverse else (rows >= k)
        pr, pi_ = power(float(k))
        tiles += [jnp.where(keep, pr, 0.0), jnp.where(keep, pi_, 0.0)]
    expo = (8.0 - rows) if reverse else (rows + 1.0)
    tiles += list(power(expo))
    return jnp.stack([jnp.broadcast_to(tl, (8, NSTATE)) for tl in tiles])


_SCAN_W = 512
_HALF_W = S5_W // 2
_HALF_S = NSTATE // 2


def _compact_diag(blocks_re, blocks_im):
    hg = GROUPS // 2
    return jnp.concatenate([_block_diag(b[h * hg:(h + 1) * hg]) for b in (blocks_re, blocks_im) for h in range(2)],
                           axis=1)


def _half_expand(v, w_ref, out_ref):
    for half in range(2):
        vh = v[:, half * _HALF_W:(half + 1) * _HALF_W]
        for part in range(2):
            c0 = part * NSTATE + half * _HALF_S
            out_ref[:, c0:c0 + _HALF_S] = _dot(vh, w_ref[:, c0:c0 + _HALF_S])


def _half_contract(x_ref, w_ref, half):
    out = None
    for part in range(2):
        r0 = part * NSTATE + half * _HALF_S
        term = _dot_nt(x_ref[:, r0:r0 + _HALF_S].astype(BF16), w_ref[:, r0:r0 + _HALF_S])
        out = term if out is None else out + term
    return out


def _half_outer(v, x_ref, acc_ref):
    for half in range(2):
        vh = v[:, half * _HALF_W:(half + 1) * _HALF_W]
        for part in range(2):
            c0 = part * NSTATE + half * _HALF_S
            acc_ref[:, c0:c0 + _HALF_S] += _dot_tn(vh, x_ref[:, c0:c0 + _HALF_S].astype(BF16))


def _ssm_fwd(rest, bd, cd, consts):
    s = rest.shape[0]
    tb = min(TB_SSM, s)
    ns2 = 2 * NSTATE

    def body(u_ref, bd_ref, cd_ref, cf_ref, y_ref, x_ref, cb_ref):
        @pl.when(pl.program_id(0) == 0)
        def _():
            cb_ref[...] = jnp.zeros_like(cb_ref)

        _half_expand(u_ref[...].astype(BF16), bd_ref, x_ref)

        def tile(ti, _):
            r0 = pl.multiple_of(ti * 8, 8)
            for cc in range(NSTATE // _SCAN_W):
                cr = pl.ds(cc * _SCAN_W, _SCAN_W)
                ci = pl.ds(NSTATE + cc * _SCAN_W, _SCAN_W)
                re = x_ref[pl.ds(r0, 8), cr]
                im = x_ref[pl.ds(r0, 8), ci]
                for n_, k in enumerate((1, 2, 4)):
                    ar = cf_ref[2 * n_, :, cr]
                    ai = cf_ref[2 * n_ + 1, :, cr]
                    sr = pltpu.roll(re, k, 0)
                    si = pltpu.roll(im, k, 0)
                    re, im = re + ar * sr - ai * si, im + ar * si + ai * sr
                pr = cf_ref[6, :, cr]
                pi_ = cf_ref[7, :, cr]
                cbr = cb_ref[:, cr]
                cbi = cb_ref[:, ci]
                re, im = re + pr * cbr - pi_ * cbi, im + pr * cbi + pi_ * cbr
                x_ref[pl.ds(r0, 8), cr] = re
                x_ref[pl.ds(r0, 8), ci] = im
                cb_ref[:, cr] = jnp.broadcast_to(re[7:8, :], (8, _SCAN_W))
                cb_ref[:, ci] = jnp.broadcast_to(im[7:8, :], (8, _SCAN_W))
            return 0

        lax.fori_loop(0, tb // 8, tile, 0)
        for half in range(2):
            y_ref[:, half * _HALF_W:(half + 1) * _HALF_W] = _half_contract(x_ref, cd_ref, half)

    return pl.pallas_call(
        body, name="ssm_fwd", grid=(s // tb,),
        in_specs=[pl.BlockSpec((tb, S5_W), lambda i: (i, R_U // S5_W)), _const((_HALF_W, ns2)), _const((_HALF_W, ns2)),
                  _const((8, 8, NSTATE))],
        out_specs=(pl.BlockSpec((tb, S5_W), lambda i: (i, 0)), pl.BlockSpec((tb, ns2), lambda i: (i, 0))),
        out_shape=(jax.ShapeDtypeStruct((s, S5_W), F32), jax.ShapeDtypeStruct((s, ns2), F32)),
        scratch_shapes=[pltpu.VMEM((8, ns2), F32)],
        compiler_params=_cparams(("arbitrary",)))(rest, bd, cd, consts)


def _ssm_bwd(dys, xs, rest, bd, cd, consts, dskip):
    s = dys.shape[0]
    tb = min(TB_SSM, s)
    nb = s // tb
    ns2 = 2 * NSTATE
    nt = tb // 8

    def body(dy_ref, x_ref, u_ref, bd_ref, cd_ref, cf_ref, dsk_ref, du_ref, gb_ref, gc_ref, da_ref,
             g_ref, cb_ref, acc_b, acc_c):
        step = pl.program_id(0)

        @pl.when(step == 0)
        def _():
            cb_ref[...] = jnp.zeros_like(cb_ref)
            acc_b[...] = jnp.zeros_like(acc_b)
            acc_c[...] = jnp.zeros_like(acc_c)
            da_ref[...] = jnp.zeros_like(da_ref)

        dy = dy_ref[...]
        dyb = dy.astype(BF16)
        _half_expand(dyb, cd_ref, g_ref)
        last_row = lax.broadcasted_iota(jnp.int32, (8, _SCAN_W), 0) == 7

        def tile(tt, _):
            r0 = pl.multiple_of((nt - 1 - tt) * 8, 8)
            for cc in range(NSTATE // _SCAN_W):
                cr = pl.ds(cc * _SCAN_W, _SCAN_W)
                ci = pl.ds(NSTATE + cc * _SCAN_W, _SCAN_W)
                re = g_ref[pl.ds(r0, 8), cr]
                im = g_ref[pl.ds(r0, 8), ci]
                for n_, k in enumerate((1, 2, 4)):
                    ar = cf_ref[2 * n_, :, cr]
                    ai = cf_ref[2 * n_ + 1, :, cr]
                    sr = pltpu.roll(re, 8 - k, 0)
                    si = pltpu.roll(im, 8 - k, 0)
                    re, im = re + ar * sr - ai * si, im + ar * si + ai * sr
                pr = cf_ref[6, :, cr]
                pi_ = cf_ref[7, :, cr]
                cbr = cb_ref[:, cr]
                cbi = cb_ref[:, ci]
                re, im = re + pr * cbr - pi_ * cbi, im + pr * cbi + pi_ * cbr
                g_ref[pl.ds(r0, 8), cr] = re
                g_ref[pl.ds(r0, 8), ci] = im
                gnr = jnp.where(last_row, cbr, pltpu.roll(re, 7, 0))
                gni = jnp.where(last_row, cbi, pltpu.roll(im, 7, 0))
                xr = x_ref[pl.ds(r0, 8), cr]
                xi = x_ref[pl.ds(r0, 8), ci]
                da_ref[:, cr] += gnr * xr + gni * xi
                da_ref[:, ci] += gni * xr - gnr * xi
                cb_ref[:, cr] = jnp.broadcast_to(re[0:1, :], (8, _SCAN_W))
                cb_ref[:, ci] = jnp.broadcast_to(im[0:1, :], (8, _SCAN_W))
            return 0

        lax.fori_loop(0, nt, tile, 0)
        for half in range(2):
            cols = slice(half * _HALF_W, (half + 1) * _HALF_W)
            du_ref[:, cols] = (_half_contract(g_ref, bd_ref, half) + dy[:, cols] * dsk_ref[:, cols]).astype(BF16)
        _half_outer(u_ref[...].astype(BF16), g_ref, acc_b)
        _half_outer(dyb, x_ref, acc_c)

        @pl.when(step == nb - 1)
        def _():
            for g in range(GROUPS):
                src = slice((g % (GROUPS // 2)) * GCH, (g % (GROUPS // 2) + 1) * GCH)
                dst = slice(g * GCH, (g + 1) * GCH)
                for part in range(2):
                    cols = slice(part * NSTATE + g * STATE, part * NSTATE + (g + 1) * STATE)
                    gb_ref[dst, part * STATE:(part + 1) * STATE] = acc_b[src, cols]
                    gc_ref[dst, part * STATE:(part + 1) * STATE] = acc_c[src, cols]

    rev = lambda i: (nb - 1 - i, 0)
    small = pl.BlockSpec((S5_W, 2 * STATE), lambda i: (0, 0))
    return pl.pallas_call(
        body, name="ssm_bwd", grid=(nb,),
        in_specs=[pl.BlockSpec((tb, S5_W), rev), pl.BlockSpec((tb, ns2), rev),
                  pl.BlockSpec((tb, S5_W), lambda i: (nb - 1 - i, R_U // S5_W)),
                  _const((_HALF_W, ns2)), _const((_HALF_W, ns2)), _const((8, 8, NSTATE)), _const((1, S5_W))],
        out_specs=(pl.BlockSpec((tb, S5_W), rev), small, small, pl.BlockSpec((8, ns2), lambda i: (0, 0))),
        out_shape=(jax.ShapeDtypeStruct((s, S5_W), BF16), jax.ShapeDtypeStruct((S5_W, 2 * STATE), F32),
                   jax.ShapeDtypeStruct((S5_W, 2 * STATE), F32), jax.ShapeDtypeStruct((8, ns2), F32)),
        scratch_shapes=[pltpu.VMEM((tb, ns2), F32), pltpu.VMEM((8, ns2), F32),
                        pltpu.VMEM((_HALF_W, ns2), F32), pltpu.VMEM((_HALF_W, ns2), F32)],
        compiler_params=_cparams(("arbitrary",)))(dys, xs, rest, bd, cd, consts, dskip)


_GELU_C = math.sqrt(2.0 / math.pi)
_GELU_A = 0.044715


def _mid(o, rest, ys0, x, tgt, w, vec, hsel):
    s = o.shape[0]
    tm = min(TM, s)
    nsteps = s // tm
    half = FOX_W

    def body(o_ref, ga_ref, gb_ref, za_ref, u_ref, zb_ref, ys0_ref, x_ref, t_ref,
             wglu_ref, wua_ref, wub_ref, wout_ref, vec_ref, hsel_ref,
             dx2_ref, dga_ref, dgb_ref, do_ref, dza_ref, dzb_ref, dys_ref, dlt_ref,
             gout_hbm, gua_hbm, gub_hbm, gglu_hbm, vout_ref,
             a_out, a_ua, a_ub, a_glu):
        step = pl.program_id(0)

        @pl.when(step == 0)
        def _():
            a_out[...] = jnp.zeros_like(a_out)
            a_ua[...] = jnp.zeros_like(a_ua)
            a_ub[...] = jnp.zeros_like(a_ub)
            a_glu[...] = jnp.zeros_like(a_glu)
            vout_ref[...] = jnp.zeros_like(vout_ref)

        gate = vec_ref[0:1, :]
        gfin = vec_ref[1:2, :]
        dsk = vec_ref[2:3, 0:half]
        bglu = vec_ref[2:3, half:2 * half]

        o_v = o_ref[...]
        za = za_ref[...]
        sza = _sigmoid(za)
        silu_za = za * sza
        ya_b = (o_v * silu_za).astype(BF16)
        u_v = u_ref[...]
        ys = ys0_ref[...] + dsk * u_v
        inner = _GELU_C * (ys + _GELU_A * ys * ys * ys)
        th = jnp.tanh(inner)
        yg = 0.5 * ys * (1.0 + th)
        yg_b = yg.astype(BF16)
        st = _sigmoid(_dot(yg_b, wglu_ref[...]) + bglu)
        yb1 = yg * st
        zb = zb_ref[...]
        szb = _sigmoid(zb)
        silu_zb = zb * szb
        yb_b = (yb1 * silu_zb).astype(BF16)
        ua = _dot(ya_b, wua_ref[...])
        ub = _dot(yb_b, wub_ref[...])
        sga = _sigmoid(ga_ref[...])
        sgb = _sigmoid(gb_ref[...])
        merged_b = (sga * ua + sgb * ub).astype(BF16)
        mo = _dot(merged_b, wout_ref[...])
        x2 = x_ref[...] + gate * mo
        r2 = lax.rsqrt(jnp.mean(x2 * x2, axis=-1, keepdims=True) + EPS)
        x2n = x2 * r2
        diff = x2n * gfin - t_ref[...]
        loss = 0.5 * jnp.sum(jnp.mean(diff * diff, axis=-1, keepdims=True), axis=0, keepdims=True)
        dy = diff * (1.0 / D_MODEL)
        dx2n = dy * gfin
        dx2 = r2 * (dx2n - x2n * jnp.mean(dx2n * x2n, axis=-1, keepdims=True))
        dx2_ref[...] = dx2
        vout_ref[0:1, :] += jnp.sum(dy * x2n, axis=0, keepdims=True)
        vout_ref[1:2, :] += jnp.sum(dx2 * mo, axis=0, keepdims=True)
        vout_ref[3:4, :] += jnp.broadcast_to(loss, (1, D_MODEL))
        dmo_b = (dx2 * gate).astype(BF16)
        dmerged = _dot_nt(dmo_b, wout_ref[...])
        a_out[...] += _dot_tn(merged_b, dmo_b)
        dua_b = (dmerged * sga).astype(BF16)
        dub_b = (dmerged * sgb).astype(BF16)
        dga_ref[...] = (dmerged * ua * sga * (1.0 - sga)).astype(BF16)
        dgb_ref[...] = (dmerged * ub * sgb * (1.0 - sgb)).astype(BF16)
        dya = _dot_nt(dua_b, wua_ref[...])
        dyb = _dot_nt(dub_b, wub_ref[...])
        a_ua[...] += _dot_tn(ya_b, dua_b)
        a_ub[...] += _dot_tn(yb_b, dub_b)
        do_b = (dya * silu_za).astype(BF16)
        do_ref[...] = do_b
        dza_ref[...] = (dya * o_v * (sza * (1.0 + za * (1.0 - sza)))).astype(BF16)
        dlt_ref[...] = lax.dot_general(hsel_ref[...], do_b.astype(F32) * o_v, (((1,), (1,)), ((), ())),
                                       preferred_element_type=F32, precision=HI)
        dyb1 = dyb * silu_zb
        dzb_ref[...] = (dyb * yb1 * (szb * (1.0 + zb * (1.0 - szb)))).astype(BF16)
        dt = dyb1 * yg * st * (1.0 - st)
        dt_b = dt.astype(BF16)
        dyg = dyb1 * st + _dot_nt(dt_b, wglu_ref[...])
        a_glu[...] += _dot_tn(yg_b, dt_b)
        dgelu = 0.5 * (1.0 + th) + 0.5 * ys * (1.0 - th * th) * _GELU_C * (1.0 + 3.0 * _GELU_A * ys * ys)
        dys = dyg * dgelu
        dys_ref[...] = dys
        vout_ref[2:3, 0:half] += jnp.sum(dys * u_v, axis=0, keepdims=True)
        vout_ref[2:3, half:2 * half] += jnp.sum(dt, axis=0, keepdims=True)

        @pl.when(step == nsteps - 1)
        def _():
            pltpu.sync_copy(a_out, gout_hbm)
            pltpu.sync_copy(a_ua, gua_hbm)
            pltpu.sync_copy(a_ub, gub_hbm)
            pltpu.sync_copy(a_glu, gglu_hbm)

    def rows(width, col=0):
        return pl.BlockSpec((tm, width), lambda i, col=col: (i, col))

    anyspace = pl.BlockSpec(memory_space=pl.ANY)
    wshapes = [(S5_W, S5_W), (FOX_W, D_MODEL), (S5_W, D_MODEL), (D_MODEL, D_MODEL)]
    return pl.pallas_call(
        body, name="mid", grid=(nsteps,),
        in_specs=[rows(FOX_W), rows(D_MODEL, R_GA // D_MODEL), rows(D_MODEL, R_GB // D_MODEL),
                  rows(FOX_W, R_ZA // FOX_W), rows(S5_W, R_U // S5_W), rows(S5_W, R_ZB // S5_W),
                  rows(S5_W), rows(D_MODEL), rows(D_MODEL)]
                 + [_const(sh) for sh in wshapes]
                 + [_const((8, D_MODEL)), _const((HEADS, FOX_W))],
        out_specs=(rows(D_MODEL), rows(D_MODEL), rows(D_MODEL), rows(FOX_W), rows(FOX_W), rows(S5_W), rows(S5_W),
                   pl.BlockSpec((HEADS, tm), lambda i: (0, i)),
                   anyspace, anyspace, anyspace, anyspace, pl.BlockSpec((8, D_MODEL), lambda i: (0, 0))),
        out_shape=(jax.ShapeDtypeStruct((s, D_MODEL), F32), jax.ShapeDtypeStruct((s, D_MODEL), BF16),
                   jax.ShapeDtypeStruct((s, D_MODEL), BF16), jax.ShapeDtypeStruct((s, FOX_W), BF16),
                   jax.ShapeDtypeStruct((s, FOX_W), BF16), jax.ShapeDtypeStruct((s, S5_W), BF16),
                   jax.ShapeDtypeStruct((s, S5_W), F32), jax.ShapeDtypeStruct((HEADS, s), F32),
                   jax.ShapeDtypeStruct((D_MODEL, D_MODEL), F32), jax.ShapeDtypeStruct((FOX_W, D_MODEL), F32),
                   jax.ShapeDtypeStruct((S5_W, D_MODEL), F32), jax.ShapeDtypeStruct((S5_W, S5_W), F32),
                   jax.ShapeDtypeStruct((8, D_MODEL), F32)),
        scratch_shapes=[pltpu.VMEM((D_MODEL, D_MODEL), F32), pltpu.VMEM((FOX_W, D_MODEL), F32),
                        pltpu.VMEM((S5_W, D_MODEL), F32), pltpu.VMEM((S5_W, S5_W), F32)],
        compiler_params=_cparams(("arbitrary",)),
    )(o, rest, rest, rest, rest, rest, ys0, x, tgt, *w, vec, hsel)


def _dh(dq, dk, dv, dga, dgb, dza, du, dzb, df, wqkv_t, wrest_t, x, dx2, gs, scatter_srcs):
    s = x.shape[0]
    tm = min(TM, s)
    nsteps = s // tm
    na = len(scatter_srcs)

    def body(dq_ref, dk_ref, dv_ref, dga_ref, dgb_ref, dza_ref, du_ref, dzb_ref, df_ref, wq_ref, wr_ref,
             x_ref, dx2_ref, gs_ref, *rest_refs):
        src_refs = rest_refs[:na]
        gx_ref, vout_ref = rest_refs[na:na + 2]
        out_refs = rest_refs[na + 2:2 * na + 2]
        send_sems, recv_sems = rest_refs[2 * na + 2:]
        step = pl.program_id(0)
        cx, cy, cc = lax.axis_index("x"), lax.axis_index("y"), lax.axis_index("c")
        peers = [(1 - cx, cy), (cx, 1 - cy), (1 - cx, 1 - cy)]

        def copy(a, k, px, py, slot):
            return pltpu.make_async_remote_copy(
                src_ref=src_refs[a].at[2 * px + py], dst_ref=out_refs[a].at[slot],
                send_sem=send_sems.at[a * 3 + k], recv_sem=recv_sems.at[a * 3 + k],
                device_id=(px, py, cc), device_id_type=MESH)

        @pl.when(step == 0)
        def _():
            vout_ref[...] = jnp.zeros_like(vout_ref)
            for a in range(na):
                for k, (px, py) in enumerate(peers):
                    copy(a, k, px, py, 2 * cx + cy).start()

        dh = _dot(dq_ref[...], wq_ref[0:512, :])
        dh += _dot(dk_ref[...], wq_ref[512:1024, :])
        dh += _dot(dv_ref[...], wq_ref[1024:1536, :])
        dh += _dot(dga_ref[...], wr_ref[R_GA:R_GB, :])
        dh += _dot(dgb_ref[...], wr_ref[R_GB:R_ZA, :])
        dh += _dot(dza_ref[...], wr_ref[R_ZA:R_U, :])
        dh += _dot(du_ref[...], wr_ref[R_U:R_ZB, :])
        dh += _dot(dzb_ref[...], wr_ref[R_ZB:R_F, :])
        dh += _dot(df_ref[...], wr_ref[R_F:REST_W, :])
        xv = x_ref[...]
        r = lax.rsqrt(jnp.mean(xv * xv, axis=-1, keepdims=True) + EPS)
        xn = xv * r
        dxn = dh * gs_ref[...]
        gx_ref[...] = dx2_ref[...] + r * (dxn - xn * jnp.mean(dxn * xn, axis=-1, keepdims=True))
        vout_ref[0:1, :] += jnp.sum(dh * xn, axis=0, keepdims=True)
        vout_ref[1:2, :] += jnp.sum(dh, axis=0, keepdims=True)

        @pl.when(step == nsteps - 1)
        def _():
            for a in range(na):
                for k, (px, py) in enumerate(peers):
                    copy(a, k, px, py, 2 * px + py).wait_recv()
            for a in range(na):
                for k, (px, py) in enumerate(peers):
                    copy(a, k, px, py, 2 * cx + cy).wait_send()

    def rows(width):
        return pl.BlockSpec((tm, width), lambda i: (i, 0))

    anyspace = pl.BlockSpec(memory_space=pl.ANY)
    return pl.pallas_call(
        body, name="dh", grid=(nsteps,),
        in_specs=[rows(512), rows(512), rows(512), rows(1024), rows(1024), rows(512), rows(512), rows(512), rows(128),
                  _const((1536, D_MODEL)), _const((REST_W, D_MODEL)), rows(D_MODEL), rows(D_MODEL), _const((1, D_MODEL))]
                 + [anyspace] * na,
        out_specs=(rows(D_MODEL), pl.BlockSpec((8, D_MODEL), lambda i: (0, 0))) + (anyspace,) * na,
        out_shape=(jax.ShapeDtypeStruct((s, D_MODEL), F32), jax.ShapeDtypeStruct((8, D_MODEL), F32))
                  + tuple(jax.ShapeDtypeStruct(a.shape, a.dtype) for a in scatter_srcs),
        scratch_shapes=[pltpu.SemaphoreType.DMA((3 * na,)), pltpu.SemaphoreType.DMA((3 * na,))],
        compiler_params=_cparams(("arbitrary",)),
    )(dq, dk, dv, dga, dgb, dza, du, dzb, df, wqkv_t, wrest_t, x, dx2, gs, *scatter_srcs)


def _row_block(rows, mult=8, cap=512):
    if rows <= mult:
        return rows
    padded = -(-rows // mult) * mult
    for cand in range(min(cap, padded) // mult * mult, 0, -mult):
        if padded % cand == 0:
            return cand
    return padded


def _sum4(parts, name):
    rows, cols = parts.shape[1:]
    br = _row_block(rows, 16)

    def body(p_ref, o_ref):
        acc = p_ref[0].astype(F32)
        for k in range(1, 4):
            acc = acc + p_ref[k].astype(F32)
        o_ref[...] = acc

    return pl.pallas_call(
        body, name=name, grid=(pl.cdiv(rows, br),),
        in_specs=[pl.BlockSpec((4, br, cols), lambda i: (0, i, 0))],
        out_specs=pl.BlockSpec((br, cols), lambda i: (i, 0)),
        out_shape=jax.ShapeDtypeStruct((rows, cols), F32), compiler_params=_cparams(("parallel",)))(parts)


def _pair_add(a, b, name):
    shape = a.shape
    a, b = a.reshape(-1, shape[-1]), b.reshape(-1, shape[-1])
    rows, cols = a.shape
    br = _row_block(rows, 16, 1024)

    def body(a_ref, b_ref, o_ref):
        o_ref[...] = (a_ref[...].astype(F32) + b_ref[...].astype(F32)).astype(BF16)

    spec = pl.BlockSpec((br, cols), lambda i: (i, 0))
    return pl.pallas_call(
        body, name=name, grid=(pl.cdiv(rows, br),), in_specs=[spec, spec], out_specs=spec,
        out_shape=jax.ShapeDtypeStruct((rows, cols), BF16), compiler_params=_cparams(("parallel",)))(a, b).reshape(shape)


def _adamw(w, g, m, v, name):
    rows, cols = w.shape
    br = _row_block(rows)

    def body(w_ref, g_ref, m_ref, v_ref, d_ref, nm_ref, nv_ref):
        gv = g_ref[...]
        nm = ADAM_B1 * m_ref[...] + (1.0 - ADAM_B1) * gv
        nv = ADAM_B2 * v_ref[...] + (1.0 - ADAM_B2) * (gv * gv)
        m_hat = nm / (1.0 - ADAM_B1 ** ADAM_STEP)
        v_hat = nv / (1.0 - ADAM_B2 ** ADAM_STEP)
        d_ref[...] = -ADAM_LR * (m_hat / (jnp.sqrt(v_hat) + ADAM_EPS) + ADAM_WD * w_ref[...])
        nm_ref[...] = nm
        nv_ref[...] = nv

    spec = pl.BlockSpec((br, cols), lambda i: (i, 0))
    shape = jax.ShapeDtypeStruct((rows, cols), F32)
    return pl.pallas_call(
        body, name=name, grid=(pl.cdiv(rows, br),), in_specs=[spec] * 4, out_specs=(spec,) * 3,
        out_shape=(shape,) * 3, compiler_params=_cparams(("parallel",)))(w, g, m, v)


def _pack(parts, row_multiple=8):
    flat = []
    for p in parts:
        v = p.reshape(-1).astype(F32)
        pad = (-v.shape[0]) % LANES
        if pad:
            v = jnp.concatenate([v, jnp.zeros((pad,), F32)])
        flat.append(v)
    v = jnp.concatenate(flat)
    rows = v.shape[0] // LANES
    pad_rows = (-rows) % row_multiple
    if pad_rows:
        v = jnp.concatenate([v, jnp.zeros((pad_rows * LANES,), F32)])
    return v.reshape(-1, LANES)


def _unpack(packed, shapes):
    lead = packed.shape[:-2]
    flat = packed.reshape(lead + (-1,))
    out, off = [], 0
    for sh in shapes:
        size = math.prod(sh)
        out.append(flat[..., off:off + size].reshape(lead + tuple(sh)))
        off += size + (-size) % LANES
    return out


def kernel(x, c, w_ada, b_ada, g_norm, w_in, b_f, a_re, a_im, log_dt, b_re, b_im, c_re, c_im, d_skip, w_glu, b_glu, w_up_a, w_up_b, w_out, g_final, loss_target, m_w_ada, m_b_ada, m_g_norm, m_w_in, m_b_f, m_a_re, m_a_im, m_log_dt, m_b_re, m_b_im, m_c_re, m_c_im, m_d_skip, m_w_glu, m_b_glu, m_w_up_a, m_w_up_b, m_w_out, m_g_final, v_w_ada, v_b_ada, v_g_norm, v_w_in, v_b_f, v_a_re, v_a_im, v_log_dt, v_b_re, v_b_im, v_c_re, v_c_im, v_d_skip, v_w_glu, v_b_glu, v_w_up_a, v_w_up_b, v_w_out, v_g_final):
    xi, yi, ci = lax.axis_index("x"), lax.axis_index("y"), lax.axis_index("c")
    chip = 2 * xi + yi
    me = 4 * xi + 2 * yi + ci
    s = x.shape[1]
    x2d = x[0]
    tgt = loss_target[0]
    n_att = s // min(T_ATT, s)
    t_att = min(T_ATT, s)

    c_all, _ = _allgather8(c.reshape(8, LANES), "gather_c")
    c_all = c_all.reshape(8, D_MODEL)
    ncol = w_ada.shape[2]
    b_cols = lax.dynamic_slice_in_dim(b_ada, chip * ncol, ncol, axis=1)
    mod_cols = _mod_cols(c_all, w_ada[0], b_cols)
    mod_all, _ = _allgather8(mod_cols.reshape(-1, LANES), "gather_mod")
    mod_all = mod_all.reshape(4, 2, 8, ncol)[:, 0]
    mod_me = lax.dynamic_index_in_dim(mod_all, me, axis=1, keepdims=False).reshape(1, 3 * D_MODEL)
    shift, scale, gate = mod_me[:, :D_MODEL], mod_me[:, D_MODEL:2 * D_MODEL], mod_me[:, 2 * D_MODEL:]
    gs = g_norm * (1.0 + scale)

    nshard = w_in.shape[2]
    w_in_t, m_in_t, v_in_t = (jnp.swapaxes(a[0], 0, 1) for a in (w_in, m_w_in, v_w_in))
    wt_pack = jnp.pad(w_in_t.astype(BF16), ((0, SHARD_ROWS - nshard), (0, 0)))
    misc_shapes = [w_glu.shape[1:], w_up_a.shape[1:], w_up_b.shape[1:], w_out.shape[1:]]
    misc_pack = jnp.concatenate([w.reshape(-1) for w in (w_glu, w_up_a, w_up_b, w_out)]).astype(BF16).reshape(-1, LANES)
    def halves(a):
        return a.reshape((2, a.shape[0] // 2) + a.shape[1:])

    wt_all, misc_all = _gather_shards([halves(wt_pack), halves(misc_pack)], "gather_weights")
    wt_all = lax.dynamic_update_index_in_dim(wt_all, halves(wt_pack), chip, 0).reshape((4,) + wt_pack.shape)
    misc_all = lax.dynamic_update_index_in_dim(misc_all, halves(misc_pack), chip, 0).reshape((4,) + misc_pack.shape)
    p_glu, p_ua, p_ub, p_out = _unpack(misc_all, misc_shapes)

    def w_rows(lo, hi):
        out = []
        for j in range(4):
            a, b = max(lo, j * nshard), min(hi, (j + 1) * nshard)
            if a < b:
                out.append(wt_all[j, a - j * nshard:b - j * nshard])
        return out

    wqkv_t = jnp.concatenate(w_rows(O_Q, O_F), axis=0)
    wrest_t = jnp.concatenate(w_rows(O_GA, O_GB) + w_rows(O_GB, O_END) + w_rows(O_ZA, O_U) + w_rows(O_U, O_ZB)
                              + w_rows(O_ZB, O_GA) + w_rows(O_F, O_ZA)
                              + [jnp.zeros((REST_W - R_F - HEADS, D_MODEL), BF16)], axis=0)
    wmid = (p_glu.reshape(S5_W, S5_W), jnp.concatenate([p_ua[j] for j in range(4)], axis=1),
            jnp.concatenate([p_ub[j] for j in range(4)], axis=1), p_out.reshape(D_MODEL, D_MODEL))

    h, qkv, rest = _prenorm_proj(x2d, gs, shift, wqkv_t, wrest_t)
    bf128 = jnp.pad(b_f, ((0, 0), (0, LANES - HEADS)))
    selp = _head_pair_selector()
    fpc, f_t = _fcum(rest, bf128, selp)
    frow5 = f_t.reshape(4, 2, n_att, 1, t_att)
    o, lse_pc = _attn_fwd(qkv, frow5, fpc)

    abar_r, abar_i, bb_r, bb_i = _ssm_block_params(a_re[0], a_im[0], log_dt[0], b_re[0], b_im[0])
    bb_rt, bb_it = jnp.swapaxes(bb_r, 1, 2).astype(BF16), jnp.swapaxes(bb_i, 1, 2).astype(BF16)
    cr_b, ci_b = c_re[0].astype(BF16), (-c_im[0]).astype(BF16)
    bd_c, cd_c = _compact_diag(bb_rt, bb_it), _compact_diag(cr_b, ci_b)
    ys0, xs = _ssm_fwd(rest, bd_c, cd_c, _scan_consts(a_re[0], a_im[0], log_dt[0], False))

    vec = jnp.concatenate([gate, g_final.reshape(1, D_MODEL), jnp.concatenate([d_skip, b_glu], axis=1),
                           jnp.zeros((5, D_MODEL), F32)], axis=0)
    hsel = jnp.repeat(jnp.eye(HEADS, dtype=F32), HEAD_DIM, axis=1)
    (dx2, dga, dgb, do, dza, dzb, dys, dlt_t, g_out, g_ua, g_ub, g_glu, vmid) = _mid(
        o, rest, ys0, x2d, tgt, wmid, vec, hsel)

    lse_t = jnp.transpose(lse_pc.reshape(s, 4 // ATT_PAIRS, LANES)[:, :, :2 * ATT_PAIRS], (1, 2, 0))
    lse5 = lse_t.reshape(4, 2, n_att, 1, t_att)
    dlt5 = dlt_t.reshape(4, 2, n_att, 1, t_att)
    dq, dk, dv, dfk, dfq = _attn_bwd(qkv, do, lse5, dlt5, frow5, fpc)
    du, g_bd, g_cdt, da8 = _ssm_bwd(dys, xs, rest, bd_c, cd_c, _scan_consts(a_re[0], a_im[0], log_dt[0], True), d_skip)
    df, dbf8 = _dfcum(dfk, dfq, rest, bf128, selp.T)

    gq, gk, gv, gga, ggb, gza, gu, gzb, gf = _grad_w_rows(h, [dq, dk, dv, dga, dgb, dza, du, dzb, df])
    g_in_t = jnp.concatenate([gq, gk, gv, gf[:HEADS], gza, gu, gzb, gga, ggb], axis=0)

    def shard_cols(g, j):
        n = g.shape[1] // 4
        return g[:, j * n:(j + 1) * n]

    def shard_rows(g, j):
        n = g.shape[0] // 4
        return g[j * n:(j + 1) * n]

    def halves4(a):
        return a.reshape((4, 2, a.shape[1] // 2) + a.shape[2:])

    gt_pack = halves4(jnp.stack([
        jnp.pad(g_in_t[j * nshard:(j + 1) * nshard].astype(BF16), ((0, SHARD_ROWS - nshard), (0, 0)))
        for j in range(4)]))
    gm_pack = halves4(jnp.stack([
        jnp.concatenate([shard_rows(g_glu, j).reshape(-1), shard_cols(g_ua, j).reshape(-1),
                         shard_cols(g_ub, j).reshape(-1), shard_rows(g_out, j).reshape(-1)]).astype(BF16)
        .reshape(-1, LANES) for j in range(4)]))
    recv_in, recv_misc = _swap_sibling([gt_pack, gm_pack], "pair_swap_weight_grads", other_half=True)
    own_in = lax.dynamic_index_in_dim(gt_pack, ci, axis=1, keepdims=False)
    own_misc = lax.dynamic_index_in_dim(gm_pack, ci, axis=1, keepdims=False)
    pair_in = _pair_add(own_in, recv_in, "pair_add_w_in")
    pair_misc = _pair_add(own_misc, recv_misc, "pair_add_misc")

    grad_x, vdh, parts_in, parts_misc = _dh(dq, dk, dv, dga, dgb, dza, du, dzb, df, wqkv_t, wrest_t, x2d, dx2, gs,
                                            [pair_in, pair_misc])
    parts_in = lax.dynamic_update_slice_in_dim(parts_in, lax.dynamic_slice_in_dim(pair_in, chip, 1, 0), chip, 0)
    parts_misc = lax.dynamic_update_slice_in_dim(parts_misc, lax.dynamic_slice_in_dim(pair_misc, chip, 1, 0), chip, 0)
    half_in, half_misc = _sum4(parts_in, "sum4_w_in"), _sum4(parts_misc, "sum4_misc")
    sib_in, sib_misc = _swap_sibling([half_in, half_misc], "swap_weight_grads")

    def both_halves(mine, theirs):
        return jnp.concatenate([jnp.where(ci == 0, mine, theirs), jnp.where(ci == 0, theirs, mine)], axis=0)

    tot_in, tot_misc = both_halves(half_in, sib_in), both_halves(half_misc, sib_misc)
    g_glu_s, g_ua_s, g_ub_s, g_out_s = _unpack(tot_misc, misc_shapes)

    dgs, dshift = vdh[0:1], vdh[1:2]
    dmod = jnp.concatenate([dshift, dgs * g_norm, vmid[1:2]], axis=1)
    da = jnp.sum(da8, axis=0)
    g_bd = g_bd.reshape(GROUPS, GCH, 2 * STATE)
    g_cdt = g_cdt.reshape(GROUPS, GCH, 2 * STATE)
    g_bbr = jnp.swapaxes(g_bd[:, :, :STATE], 1, 2)
    g_bbi = jnp.swapaxes(g_bd[:, :, STATE:], 1, 2)
    g_cre = g_cdt[:, :, :STATE]
    g_cim = -g_cdt[:, :, STATE:]
    small_shapes = [(1,), (3 * D_MODEL,), (D_MODEL,), (HEADS,), (GROUPS, STATE), (GROUPS, STATE),
                    (GROUPS, STATE, GCH), (GROUPS, STATE, GCH), (GROUPS, GCH, STATE), (GROUPS, GCH, STATE),
                    (S5_W,), (S5_W,), (D_MODEL,)]
    small = _pack([vmid[3, 0:1], dmod, dgs * (1.0 + scale), dbf8[0, :HEADS], da[:NSTATE], da[NSTATE:],
                   g_bbr, g_bbi, g_cre, g_cim, vmid[2, :S5_W], vmid[2, S5_W:], vmid[0]])
    small_all, small_sum = _allgather8(small, "gather_small_grads")
    (loss_s, g_b_ada, g_g_norm, g_b_f, g_abr, g_abi, g_bbr_s, g_bbi_s, g_c_re, g_c_im, g_d_skip, g_b_glu,
     g_g_final) = _unpack(small_sum, small_shapes)
    loss = loss_s[0]
    dmod_all = _unpack(small_all, small_shapes)[1]
    dmod_cols = lax.dynamic_slice_in_dim(dmod_all, chip * ncol, ncol, axis=1)
    g_w_ada = _grad_w_ada(c_all, dmod_cols)
    _, ssm_vjp = jax.vjp(_ssm_block_params, a_re[0], a_im[0], log_dt[0], b_re[0], b_im[0])
    g_a_re, g_a_im, g_log_dt, g_b_re, g_b_im = ssm_vjp((g_abr, g_abi, g_bbr_s, g_bbi_s))

    def adam(name, w, g, m, v):
        shape = w.shape
        total = math.prod(shape)
        if len(shape) > 1 and shape[-1] >= LANES:
            cols = shape[-1]
        elif total % LANES == 0:
            cols = LANES
        else:
            cols = total
        two = lambda a: a.reshape(-1, cols)
        d, nm, nv = _adamw(two(w), two(g), two(m), two(v), "adamw_" + name)
        return g.reshape(shape), d.reshape(shape), nm.reshape(shape), nv.reshape(shape)

    back = lambda a: jnp.swapaxes(a, 0, 1)[None]
    d_in_t, nm_in_t, nv_in_t = _adamw(w_in_t, tot_in, m_in_t, v_in_t, "adamw_w_in")
    res_w_in = (back(tot_in[:nshard]), back(d_in_t), back(nm_in_t), back(nv_in_t))

    res = [
        adam("w_ada", w_ada, g_w_ada, m_w_ada, v_w_ada),
        adam("b_ada", b_ada, g_b_ada, m_b_ada, v_b_ada),
        adam("g_norm", g_norm, g_g_norm, m_g_norm, v_g_norm),
        res_w_in,
        adam("b_f", b_f, g_b_f, m_b_f, v_b_f),
        adam("a_re", a_re, g_a_re, m_a_re, v_a_re),
        adam("a_im", a_im, g_a_im, m_a_im, v_a_im),
        adam("log_dt", log_dt, g_log_dt, m_log_dt, v_log_dt),
        adam("b_re", b_re, g_b_re, m_b_re, v_b_re),
        adam("b_im", b_im, g_b_im, m_b_im, v_b_im),
        adam("c_re", c_re, g_c_re, m_c_re, v_c_re),
        adam("c_im", c_im, g_c_im, m_c_im, v_c_im),
        adam("d_skip", d_skip, g_d_skip, m_d_skip, v_d_skip),
        adam("w_glu", w_glu, g_glu_s, m_w_glu, v_w_glu),
        adam("b_glu", b_glu, g_b_glu, m_b_glu, v_b_glu),
        adam("w_up_a", w_up_a, g_ua_s, m_w_up_a, v_w_up_a),
        adam("w_up_b", w_up_b, g_ub_s, m_w_up_b, v_w_up_b),
        adam("w_out", w_out, g_out_s, m_w_out, v_w_out),
        adam("g_final", g_final, g_g_final, m_g_final, v_g_final),
    ]
    grads = [r[0] for r in res]
    deltas = [r[1] for r in res]
    new_m = [r[2] for r in res]
    new_v = [r[3] for r in res]
    return (loss, grad_x[None], *grads, *deltas, *new_m, *new_v)
```

```python
import functools
import math

import jax
import jax.numpy as jnp
from jax import lax
from jax.experimental import pallas as pl
from jax.experimental.pallas import tpu as pltpu

F32 = jnp.float32
BF16 = jnp.bfloat16
HI = lax.Precision.HIGHEST
MESH = pl.DeviceIdType.MESH

D_MODEL = 1024
HEADS = 8
HEAD_DIM = 64
FOX_W = 512
S5_W = 512
GROUPS = 32
STATE = 64
GCH = 16
NSTATE = GROUPS * STATE
EPS = 1e-6
NEG = -1e30

ADAM_LR = 0.001
ADAM_B1 = 0.9
ADAM_B2 = 0.999
ADAM_EPS = 1e-08
ADAM_WD = 0.01
ADAM_STEP = 10

VMEM_LIMIT = 56 * 1024 * 1024
LANES = 128

TM = 256
T_ATT = 512
ATT_CHUNK = 32
ATT_PAIRS = 4
TB_SSM = 256
TK_ACC = 512
TB_CUM = 256
SHARD_ROWS = 1312

O_Q, O_K, O_V, O_F, O_ZA, O_U, O_ZB, O_GA, O_GB, O_END = 0, 512, 1024, 1536, 1544, 2056, 2568, 3080, 4104, 5128
REST_W = 3712
R_GA, R_GB, R_ZA, R_U, R_ZB, R_F = 0, 1024, 2048, 2560, 3072, 3584


def _cparams(sem=None):
    kw = dict(vmem_limit_bytes=VMEM_LIMIT)
    if sem is not None:
        kw["dimension_semantics"] = sem
    return pltpu.CompilerParams(**kw)


def _const(shape):
    nd = len(shape)
    return pl.BlockSpec(shape, lambda *_: (0,) * nd, pipeline_mode=pl.Buffered(1))


def _dot(a, b, precision=None):
    return jnp.dot(a, b, preferred_element_type=F32, precision=precision)


def _dot_nt(a, b):
    return lax.dot_general(a, b, (((1,), (1,)), ((), ())), preferred_element_type=F32)


def _dot_tn(a, b, precision=None):
    return lax.dot_general(a, b, (((0,), (0,)), ((), ())), preferred_element_type=F32, precision=precision)


def _sigmoid(z):
    return 1.0 / (1.0 + jnp.exp(-z))


def _split3(x):
    hi = x.astype(BF16)
    r1 = x - hi.astype(F32)
    mid = r1.astype(BF16)
    lo = (r1 - mid.astype(F32)).astype(BF16)
    return hi, mid, lo


def _dot_sel(sel, x):
    s16 = sel.astype(BF16)
    return sum(_dot(s16, part) for part in _split3(x))


def _dot_by_sel(x, sel):
    s16 = sel.astype(BF16)
    return sum(_dot(part, s16) for part in _split3(x))


def _allgather8(xs, name):
    rows = xs.shape[0]

    def body(x_ref, out_ref, sum_ref, send_sems, recv_sems, local_sem):
        x, y, c = lax.axis_index("x"), lax.axis_index("y"), lax.axis_index("c")
        me, sibling = (x, y, c), (x, y, 1 - c)
        chips = [(1 - x, y), (x, 1 - y), (1 - x, 1 - y)]

        def slot(px, py, pc):
            return out_ref.at[4 * px + 2 * py + pc]

        def copy(k, block, to, src=None):
            return pltpu.make_async_remote_copy(
                src_ref=slot(*block) if src is None else src, dst_ref=slot(*block),
                send_sem=send_sems.at[k], recv_sem=recv_sems.at[k], device_id=to, device_id_type=MESH)

        mine = pltpu.make_async_copy(x_ref, slot(*me), local_sem)
        mine.start()
        first = [copy(0, me, sibling, src=x_ref)]
        first += [copy(1 + j, me, (*chip, c), src=x_ref) for j, chip in enumerate(chips)]
        for cp in first:
            cp.start()
        passed = [copy(4 + j, (*chip, c), sibling) for j, chip in enumerate(chips)]
        for j, chip in enumerate(chips):
            copy(1 + j, (*chip, c), me).wait_recv()
            passed[j].start()
        copy(0, sibling, me).wait_recv()
        for j, chip in enumerate(chips):
            copy(4 + j, (*chip, 1 - c), me).wait_recv()
        for cp in first + passed:
            cp.wait_send()
        mine.wait()
        acc = out_ref[0]
        for d in range(1, 8):
            acc = acc + out_ref[d]
        sum_ref[...] = acc

    return pl.pallas_call(
        body, name=name,
        out_shape=(jax.ShapeDtypeStruct((8, rows, LANES), F32), jax.ShapeDtypeStruct((rows, LANES), F32)),
        in_specs=[pl.BlockSpec(memory_space=pltpu.VMEM)],
        out_specs=(pl.BlockSpec(memory_space=pltpu.VMEM), pl.BlockSpec(memory_space=pltpu.VMEM)),
        scratch_shapes=[pltpu.SemaphoreType.DMA((7,)), pltpu.SemaphoreType.DMA((7,)), pltpu.SemaphoreType.DMA],
        compiler_params=_cparams(),
    )(xs)


def _gather_shards(srcs, name):
    na = len(srcs)

    def body(*refs):
        src_refs, out_refs = refs[:na], refs[na:2 * na]
        send_sems, recv_sems = refs[2 * na:]
        x, y, c = lax.axis_index("x"), lax.axis_index("y"), lax.axis_index("c")
        sibling = (x, y, 1 - c)
        peers = [(1 - x, y), (x, 1 - y), (1 - x, 1 - y)]

        def copy(a, k, src, slot, which, to):
            return pltpu.make_async_remote_copy(
                src_ref=src, dst_ref=out_refs[a].at[slot, which],
                send_sem=send_sems.at[a * 6 + k], recv_sem=recv_sems.at[a * 6 + k],
                device_id=to, device_id_type=MESH)

        mine = 2 * x + y
        first = [copy(a, k, src_refs[a].at[c], mine, c, (px, py, c))
                 for a in range(na) for k, (px, py) in enumerate(peers)]
        for cp in first:
            cp.start()
        passed = []
        for a in range(na):
            for k, (px, py) in enumerate(peers):
                slot = 2 * px + py
                landed = out_refs[a].at[slot, c]
                copy(a, k, landed, slot, c, (px, py, c)).wait_recv()
                fwd = copy(a, 3 + k, landed, slot, c, sibling)
                fwd.start()
                passed.append(fwd)
        for a in range(na):
            for k, (px, py) in enumerate(peers):
                slot = 2 * px + py
                copy(a, 3 + k, out_refs[a].at[slot, 1 - c], slot, 1 - c, sibling).wait_recv()
        for cp in first + passed:
            cp.wait_send()

    anyspace = pl.BlockSpec(memory_space=pl.ANY)
    return pl.pallas_call(
        body, name=name,
        out_shape=tuple(jax.ShapeDtypeStruct((4,) + tuple(a.shape), a.dtype) for a in srcs),
        in_specs=[anyspace] * na, out_specs=(anyspace,) * na,
        scratch_shapes=[pltpu.SemaphoreType.DMA((6 * na,)), pltpu.SemaphoreType.DMA((6 * na,))],
        compiler_params=_cparams(),
    )(*srcs)


def _swap_sibling(srcs, name, other_half=False):
    na = len(srcs)

    def body(*refs):
        src_refs, out_refs = refs[:na], refs[na:2 * na]
        send_sems, recv_sems = refs[2 * na:]
        x, y, c = lax.axis_index("x"), lax.axis_index("y"), lax.axis_index("c")
        copies = [pltpu.make_async_remote_copy(
            src_ref=src_refs[a].at[:, 1 - c] if other_half else src_refs[a],
            dst_ref=out_refs[a], send_sem=send_sems.at[a], recv_sem=recv_sems.at[a],
            device_id=(x, y, 1 - c), device_id_type=MESH) for a in range(na)]
        for cp in copies:
            cp.start()
        for cp in copies:
            cp.wait()

    def out_of(a):
        shape = (a.shape[0],) + tuple(a.shape[2:]) if other_half else a.shape
        return jax.ShapeDtypeStruct(shape, a.dtype)

    anyspace = pl.BlockSpec(memory_space=pl.ANY)
    return pl.pallas_call(
        body, name=name, out_shape=tuple(out_of(a) for a in srcs),
        in_specs=[anyspace] * na, out_specs=(anyspace,) * na,
        scratch_shapes=[pltpu.SemaphoreType.DMA((na,)), pltpu.SemaphoreType.DMA((na,))],
        compiler_params=_cparams(),
    )(*srcs)


def _mod_cols(c_all, w, b):
    n = w.shape[1]

    def body(c_ref, w_ref, b_ref, o_ref):
        o_ref[...] = _dot(c_ref[...], w_ref[...], HI) + b_ref[...]

    return pl.pallas_call(
        body, name="mod_cols", out_shape=jax.ShapeDtypeStruct((8, n), F32),
        compiler_params=_cparams())(c_all, w, b)


def _grad_w_ada(c_all, dmod_cols):
    n = dmod_cols.shape[1]

    def body(c_ref, d_ref, o_ref):
        o_ref[...] = _dot_tn(c_ref[...], d_ref[...], HI)

    return pl.pallas_call(
        body, name="grad_w_ada", out_shape=jax.ShapeDtypeStruct((D_MODEL, n), F32),
        compiler_params=_cparams())(c_all, dmod_cols)


def _prenorm_proj(x, gs, shift, wqkv_t, wrest_t):
    s = x.shape[0]
    tm = min(TM, s)
    nq, nr = wqkv_t.shape[0], wrest_t.shape[0]

    def body(x_ref, gs_ref, sh_ref, wq_ref, wr_ref, h_ref, qkv_ref, rest_ref):
        xv = x_ref[...]
        r = lax.rsqrt(jnp.mean(xv * xv, axis=-1, keepdims=True) + EPS)
        h = (xv * r * gs_ref[...] + sh_ref[...]).astype(BF16)
        h_ref[...] = h
        qkv_ref[...] = _dot_nt(h, wq_ref[...]).astype(BF16)
        rest_ref[...] = _dot_nt(h, wr_ref[...])

    def rows(width):
        return pl.BlockSpec((tm, width), lambda i: (i, 0))

    return pl.pallas_call(
        body, name="prenorm_proj", grid=(s // tm,),
        in_specs=[rows(D_MODEL), _const((1, D_MODEL)), _const((1, D_MODEL)), _const((nq, D_MODEL)),
                  _const((nr, D_MODEL))],
        out_specs=(rows(D_MODEL), rows(nq), rows(nr)),
        out_shape=(jax.ShapeDtypeStruct((s, D_MODEL), BF16), jax.ShapeDtypeStruct((s, nq), BF16),
                   jax.ShapeDtypeStruct((s, nr), F32)),
        compiler_params=_cparams(("parallel",)))(x, gs, shift, wqkv_t, wrest_t)


def _grad_w_rows(h, ds):
    s = h.shape[0]
    tk = min(TK_ACC, s)
    nd = len(ds)
    widths = [d.shape[1] for d in ds]

    def body(*refs):
        h_ref, d_refs = refs[0], refs[1:1 + nd]
        out_refs, accs = refs[1 + nd:1 + 2 * nd], refs[1 + 2 * nd:]
        step = pl.program_id(0)

        @pl.when(step == 0)
        def _():
            for acc in accs:
                acc[...] = jnp.zeros_like(acc)

        hv = h_ref[...]
        for d_ref, acc in zip(d_refs, accs):
            acc[...] += _dot_tn(d_ref[...], hv)

        @pl.when(step == s // tk - 1)
        def _():
            for acc, out in zip(accs, out_refs):
                pltpu.sync_copy(acc, out)

    anyspace = pl.BlockSpec(memory_space=pl.ANY)
    return pl.pallas_call(
        body, name="grad_w_in", grid=(s // tk,),
        in_specs=[pl.BlockSpec((tk, D_MODEL), lambda k: (k, 0))]
                 + [pl.BlockSpec((tk, w), lambda k: (k, 0)) for w in widths],
        out_specs=(anyspace,) * nd,
        out_shape=tuple(jax.ShapeDtypeStruct((w, D_MODEL), F32) for w in widths),
        scratch_shapes=[pltpu.VMEM((w, D_MODEL), F32) for w in widths],
        compiler_params=_cparams(("arbitrary",)))(h, *ds)


def _head_pair_selector():
    rows = jnp.arange(LANES)[:, None]
    cols = jnp.arange(4 * LANES)[None, :]
    return ((rows < HEADS) & (cols == (rows // 2) * LANES + rows % 2)).astype(F32)


def _fcum(rest, bf128, selp):
    s = rest.shape[0]
    tb = min(TB_CUM, s)

    def body(fz_ref, bf_ref, sel_ref, fpc_ref, ft_ref, carry_ref):
        @pl.when(pl.program_id(0) == 0)
        def _():
            carry_ref[...] = jnp.zeros_like(carry_ref)

        z = fz_ref[...] + bf_ref[...]
        logf = jnp.minimum(z, 0.0) - jnp.log(1.0 + jnp.exp(-jnp.abs(z)))
        r = lax.broadcasted_iota(jnp.int32, (tb, tb), 0)
        c = lax.broadcasted_iota(jnp.int32, (tb, tb), 1)
        tri = (c <= r).astype(F32)
        f = _dot_sel(tri, logf) + carry_ref[0:1, :]
        carry_ref[0:1, :] = f[tb - 1:tb, :]
        fpc_ref[...] = _dot_by_sel(f, sel_ref[...])
        ft_ref[...] = jnp.transpose(f)[0:HEADS, :]

    return pl.pallas_call(
        body, name="forget_cumsum", grid=(s // tb,),
        in_specs=[pl.BlockSpec((tb, LANES), lambda i: (i, R_F // LANES)), _const((1, LANES)), _const((LANES, 4 * LANES))],
        out_specs=(pl.BlockSpec((tb, 4 * LANES), lambda i: (i, 0)), pl.BlockSpec((HEADS, tb), lambda i: (0, i))),
        out_shape=(jax.ShapeDtypeStruct((s, 4 * LANES), F32), jax.ShapeDtypeStruct((HEADS, s), F32)),
        scratch_shapes=[pltpu.VMEM((8, LANES), F32)],
        compiler_params=_cparams(("arbitrary",)))(rest, bf128, selp)


def _dfcum(dfk, dfq, rest, bf128, selq):
    s = rest.shape[0]
    tb = min(TB_CUM, s)
    nb = s // tb

    def body(dk_ref, dq_ref, fz_ref, bf_ref, sel_ref, df_ref, dbf_ref, carry_ref):
        @pl.when(pl.program_id(0) == 0)
        def _():
            carry_ref[...] = jnp.zeros_like(carry_ref)
            dbf_ref[...] = jnp.zeros_like(dbf_ref)

        d = _dot_by_sel(dk_ref[...] + dq_ref[...], sel_ref[...])
        r = lax.broadcasted_iota(jnp.int32, (tb, tb), 0)
        c = lax.broadcasted_iota(jnp.int32, (tb, tb), 1)
        triu = (c >= r).astype(F32)
        dlogf = _dot_sel(triu, d) + carry_ref[0:1, :]
        carry_ref[0:1, :] = dlogf[0:1, :]
        z = fz_ref[...] + bf_ref[...]
        df = dlogf * (1.0 / (1.0 + jnp.exp(z)))
        df_ref[...] = df.astype(BF16)
        dbf_ref[0:1, :] += jnp.sum(df, axis=0, keepdims=True)

    return pl.pallas_call(
        body, name="forget_grad", grid=(nb,),
        in_specs=[pl.BlockSpec((tb, 4 * LANES), lambda i: (nb - 1 - i, 0)),
                  pl.BlockSpec((tb, 4 * LANES), lambda i: (nb - 1 - i, 0)),
                  pl.BlockSpec((tb, LANES), lambda i: (nb - 1 - i, R_F // LANES)),
                  _const((1, LANES)), _const((4 * LANES, LANES))],
        out_specs=(pl.BlockSpec((tb, LANES), lambda i: (nb - 1 - i, 0)), pl.BlockSpec((8, LANES), lambda i: (0, 0))),
        out_shape=(jax.ShapeDtypeStruct((s, LANES), BF16), jax.ShapeDtypeStruct((8, LANES), F32)),
        scratch_shapes=[pltpu.VMEM((8, LANES), F32)],
        compiler_params=_cparams(("arbitrary",)))(dfk, dfq, rest, bf128, selq)


def _scaled(q):
    return (q.astype(F32) * (HEAD_DIM ** -0.5)).astype(BF16)


def _attn_fwd(qkv, frow5, fpc):
    s = qkv.shape[0]
    t = min(T_ATT, s)
    n = s // t
    ch = min(ATT_CHUNK, t)
    wide = 2 * LANES
    pairs = ATT_PAIRS
    width = pairs * LANES
    groups = 4 // pairs

    def body(q_ref, k_ref, v_ref, fr_ref, fc_ref, o_ref, lse_ref, s_scr, p_scr, m_scr, a_scr, fq_scr, acc_scr):
        i = pl.program_id(1)
        lane = lax.broadcasted_iota(jnp.int32, (t, LANES), 1)
        first = lane < HEAD_DIM
        ones_col = ((lane == 0).astype(BF16), (lane == 1).astype(BF16))
        m_scr[...] = jnp.full(m_scr.shape, NEG, F32)
        acc_scr[...] = jnp.zeros_like(acc_scr)
        qm = []
        for pp in range(pairs):
            q = _scaled(q_ref[:, pp * LANES:(pp + 1) * LANES])
            zq = jnp.zeros_like(q)
            qm += [jnp.where(first, q, zq), jnp.where(first, zq, q)]
            fq_scr[2 * pp] = fc_ref[:, pp * LANES:pp * LANES + 1]
            fq_scr[2 * pp + 1] = fc_ref[:, pp * LANES + 1:pp * LANES + 2]

        def step(j, masked):
            r0 = pl.multiple_of(j * t, t)
            vaug = []
            for pp in range(pairs):
                kb = k_ref[pl.ds(r0, t), pp * LANES:(pp + 1) * LANES]
                vb = v_ref[pl.ds(r0, t), pp * LANES:(pp + 1) * LANES]
                zv = jnp.zeros_like(vb)
                vaug += [jnp.concatenate([jnp.where(first, vb, zv), ones_col[0]], axis=1),
                         jnp.concatenate([jnp.where(first, zv, vb), ones_col[1]], axis=1)]
                for hh in range(2):
                    s_scr[2 * pp + hh] = _dot_nt(qm[2 * pp + hh], kb)
            pv = []
            for hd in range(2 * pairs):
                fk = fr_ref[hd // 2, hd % 2, j]
                for c in range(t // ch):
                    rows = pl.ds(c * ch, ch)
                    hi = min(t, (c * ch // LANES + 1) * LANES) if masked else t
                    sc = s_scr[hd, rows, 0:hi] - fk[:, 0:hi]
                    if masked:
                        rq = c * ch + lax.broadcasted_iota(jnp.int32, (ch, hi), 0)
                        ck = lax.broadcasted_iota(jnp.int32, (ch, hi), 1)
                        sc = jnp.where(ck <= rq, sc, NEG)
                    fq = fq_scr[hd, rows, :]
                    m_old = m_scr[hd, rows, :]
                    m_new = jnp.maximum(m_old, fq + jnp.max(sc, axis=1, keepdims=True))
                    p_scr[hd, rows, 0:hi] = jnp.exp(sc + (fq - m_new)).astype(BF16)
                    if hi < t:
                        p_scr[hd, rows, hi:t] = jnp.zeros((ch, t - hi), BF16)
                    a_scr[hd, rows, :] = jnp.exp(m_old - m_new)
                    m_scr[hd, rows, :] = m_new
                pv.append(_dot(p_scr[hd], vaug[hd]))
            for pp in range(pairs):
                a0, a1 = a_scr[2 * pp], a_scr[2 * pp + 1]
                alpha = jnp.concatenate([jnp.where(first, a0, a1), jnp.where(lane == 0, a0, a1)], axis=1)
                acc_scr[pp] = acc_scr[pp] * alpha + pv[2 * pp] + pv[2 * pp + 1]
            return 0

        lax.fori_loop(0, i, lambda j, _: step(j, False), 0)
        step(i, True)
        lse = jnp.zeros((t, LANES), F32)
        for pp in range(pairs):
            l0 = acc_scr[pp, :, LANES:LANES + 1]
            l1 = acc_scr[pp, :, LANES + 1:LANES + 2]
            o_ref[:, pp * LANES:(pp + 1) * LANES] = acc_scr[pp, :, 0:LANES] * jnp.where(first, 1.0 / l0, 1.0 / l1)
            lse = jnp.where(lane == 2 * pp, m_scr[2 * pp] + jnp.log(l0), lse)
            lse = jnp.where(lane == 2 * pp + 1, m_scr[2 * pp + 1] + jnp.log(l1), lse)
        lse_ref[...] = lse

    blk = pl.BlockSpec((t, width), lambda g, i: (i, g))
    return pl.pallas_call(
        body, name="attn_fwd", grid=(groups, n),
        in_specs=[blk,
                  pl.BlockSpec((s, width), lambda g, i: (0, groups + g)),
                  pl.BlockSpec((s, width), lambda g, i: (0, 2 * groups + g)),
                  pl.BlockSpec((pairs, 2, n, 1, t), lambda g, i: (g, 0, 0, 0, 0)),
                  blk],
        out_specs=(blk, pl.BlockSpec((t, LANES), lambda g, i: (i, g))),
        out_shape=(jax.ShapeDtypeStruct((s, FOX_W), F32), jax.ShapeDtypeStruct((s, groups * LANES), F32)),
        scratch_shapes=[pltpu.VMEM((2 * pairs, t, t), F32), pltpu.VMEM((2 * pairs, t, t), BF16),
                        pltpu.VMEM((2 * pairs, t, 1), F32), pltpu.VMEM((2 * pairs, t, 1), F32),
                        pltpu.VMEM((2 * pairs, t, 1), F32), pltpu.VMEM((pairs, t, wide), F32)],
        compiler_params=_cparams(("parallel", "arbitrary")))(qkv, qkv, qkv, frow5, fpc)


def _attn_bwd(qkv, do, lse5, dlt5, frow5, fpc):
    s = qkv.shape[0]
    t = min(T_ATT, s)
    n = s // t
    wide = 2 * LANES

    ch = min(ATT_CHUNK, t)

    def body(q_ref, do_ref, k_ref, v_ref, lse_ref, dl_ref, fr_ref, fc_ref,
             dq_ref, dk_ref, dv_ref, dfk_ref, dfq_ref, dq_acc, st_scr, dp_scr, pt_scr, ds_scr, dk_acc, dv_acc, fk_scr):
        j = pl.program_id(1)

        @pl.when(j == 0)
        def _():
            dq_acc[...] = jnp.zeros_like(dq_acc)

        dk_acc[...] = jnp.zeros_like(dk_acc)
        dv_acc[...] = jnp.zeros_like(dv_acc)
        lane = lax.broadcasted_iota(jnp.int32, (t, LANES), 1)
        first = lane < HEAD_DIM
        ones_col = ((lane == 0).astype(BF16), (lane == 1).astype(BF16))
        kb = k_ref[...]
        vb = v_ref[...]
        zk = jnp.zeros_like(kb)
        kaug = (jnp.concatenate([jnp.where(first, kb, zk), ones_col[0]], axis=1),
                jnp.concatenate([jnp.where(first, zk, kb), ones_col[1]], axis=1))
        fk_scr[0] = fc_ref[:, 0:1]
        fk_scr[1] = fc_ref[:, 1:2]

        def step(blocks, masked):
            chains = []
            for bi, i in enumerate(blocks):
                r0 = pl.multiple_of(i * t, t)
                qb = _scaled(q_ref[pl.ds(r0, t), :])
                dob = do_ref[pl.ds(r0, t), :]
                zq = jnp.zeros_like(qb)
                qm = (jnp.where(first, qb, zq), jnp.where(first, zq, qb))
                dom = (jnp.where(first, dob, zq), jnp.where(first, zq, dob))
                for hh in range(2):
                    st_scr[2 * bi + hh] = _dot_nt(kb, qm[hh])
                    dp_scr[2 * bi + hh] = _dot_nt(vb, dom[hh])
                    chains.append((i, hh, qm[hh], dom[hh]))
            dq_add = [jnp.zeros((t, wide), F32) for _ in blocks]
            for cn, (i, hh, qmh, domh) in enumerate(chains):
                bias = fr_ref[0, hh, i] - lse_ref[0, hh, i]
                dl = dl_ref[0, hh, i]
                for c in range(t // ch):
                    rows = pl.ds(c * ch, ch)
                    lo = c * ch // LANES * LANES if masked else 0
                    st = st_scr[cn, rows, lo:t] + (bias[:, lo:t] - fk_scr[hh, rows, :])
                    if masked:
                        rk = c * ch + lax.broadcasted_iota(jnp.int32, (ch, t - lo), 0)
                        cq = lo + lax.broadcasted_iota(jnp.int32, (ch, t - lo), 1)
                        st = jnp.where(rk <= cq, st, NEG)
                    pt = jnp.exp(st)
                    pt_scr[cn, rows, lo:t] = pt.astype(BF16)
                    ds_scr[cn, rows, lo:t] = (pt * (dp_scr[cn, rows, lo:t] - dl[:, lo:t])).astype(BF16)
                    if lo > 0:
                        pt_scr[cn, rows, 0:lo] = jnp.zeros((ch, lo), BF16)
                        ds_scr[cn, rows, 0:lo] = jnp.zeros((ch, lo), BF16)
                dsb = ds_scr[cn]
                dv_acc[...] += _dot(pt_scr[cn], domh)
                dk_acc[...] += _dot(dsb, jnp.concatenate([qmh, ones_col[hh]], axis=1))
                dq_add[cn // 2] = dq_add[cn // 2] + _dot_tn(dsb, kaug[hh])
            for bi, i in enumerate(blocks):
                dq_acc[pl.ds(pl.multiple_of(i * t, t), t), :] += dq_add[bi]
            return 0

        step([j], True)
        odd = (n - 1 - j) % 2
        lax.fori_loop(0, odd, lambda _, carry: step([j + 1], False), 0)
        first_pair = j + 1 + odd
        lax.fori_loop(0, (n - first_pair) // 2,
                      lambda p, _: step([first_pair + 2 * p, first_pair + 2 * p + 1], False), 0)
        dk_ref[...] = dk_acc[:, 0:LANES].astype(BF16)
        dv_ref[...] = dv_acc[...].astype(BF16)
        dfk_ref[...] = -dk_acc[:, LANES:wide]

        @pl.when(j == n - 1)
        def _():
            dq_ref[...] = (dq_acc[:, 0:LANES] * (HEAD_DIM ** -0.5)).astype(BF16)
            dfq_ref[...] = dq_acc[:, LANES:wide]

    stat = pl.BlockSpec((1, 2, n, 1, t), lambda h, j: (h, 0, 0, 0, 0))
    blk = pl.BlockSpec((t, LANES), lambda h, j: (j, h))
    full = pl.BlockSpec((s, LANES), lambda h, j: (0, h))
    return pl.pallas_call(
        body, name="attn_bwd", grid=(4, n),
        in_specs=[full, full,
                  pl.BlockSpec((t, LANES), lambda h, j: (j, 4 + h)),
                  pl.BlockSpec((t, LANES), lambda h, j: (j, 8 + h)),
                  stat, stat, stat, blk],
        out_specs=(full, blk, blk, blk, full),
        out_shape=(jax.ShapeDtypeStruct((s, FOX_W), BF16), jax.ShapeDtypeStruct((s, FOX_W), BF16),
                   jax.ShapeDtypeStruct((s, FOX_W), BF16), jax.ShapeDtypeStruct((s, 4 * LANES), F32),
                   jax.ShapeDtypeStruct((s, 4 * LANES), F32)),
        scratch_shapes=[pltpu.VMEM((s, wide), F32), pltpu.VMEM((4, t, t), F32), pltpu.VMEM((4, t, t), F32),
                        pltpu.VMEM((4, t, t), BF16), pltpu.VMEM((4, t, t), BF16), pltpu.VMEM((t, wide), F32),
                        pltpu.VMEM((t, LANES), F32), pltpu.VMEM((2, t, 1), F32)],
        compiler_params=_cparams(("parallel", "arbitrary")))(qkv, do, qkv, qkv, lse5, dlt5, frow5, fpc)


def _ssm_block_params(a_re, a_im, log_dt, b_re, b_im):
    dt = jnp.exp(log_dt)[:, None]
    mag = jnp.exp(a_re * dt)
    ar = mag * jnp.cos(a_im * dt)
    ai = mag * jnp.sin(a_im * dt)
    den = a_re * a_re + a_im * a_im
    nr = ar - 1.0
    cr = (nr * a_re + ai * a_im) / den
    ci = (ai * a_re - nr * a_im) / den
    bbr = cr[:, :, None] * b_re - ci[:, :, None] * b_im
    bbi = cr[:, :, None] * b_im + ci[:, :, None] * b_re
    return ar, ai, bbr, bbi


def _block_diag(blocks):
    g, r, c = blocks.shape
    eye = jnp.eye(g, dtype=blocks.dtype)
    return (blocks[:, :, None, :] * eye[:, None, :, None]).reshape(g * r, g * c)


def _scan_consts(a_re, a_im, log_dt, reverse):
    dt = jnp.exp(log_dt)[:, None]
    lr = (a_re * dt).reshape(1, NSTATE)
    li = (a_im * dt).reshape(1, NSTATE)
    if reverse:
        li = -li
    rows = jnp.arange(8, dtype=F32)[:, None]

    def power(k):
        mag = jnp.exp(k * lr)
        return mag * jnp.cos(k * li), mag * jnp.sin(k * li)

    tiles = []
    for k in (1, 2, 4):
        keep = (rows < 8 - k) if reverse else (rows >= k)
        pr, pi_ = power(float(k))
        tiles += [jnp.where(keep, pr, 0.0), jnp.where(keep, pi_, 0.0)]
    expo = (8.0 - rows) if reverse else (rows + 1.0)
    tiles += list(power(expo))
    return jnp.stack([jnp.broadcast_to(tl, (8, NSTATE)) for tl in tiles])


_SCAN_W = 512
_HALF_W = S5_W // 2
_HALF_S = NSTATE // 2


def _compact_diag(blocks_re, blocks_im):
    hg = GROUPS // 2
    return jnp.concatenate([_block_diag(b[h * hg:(h + 1) * hg]) for b in (blocks_re, blocks_im) for h in range(2)],
                           axis=1)


def _half_expand(v, w_ref, out_ref):
    for half in range(2):
        vh = v[:, half * _HALF_W:(half + 1) * _HALF_W]
        for part in range(2):
            c0 = part * NSTATE + half * _HALF_S
            out_ref[:, c0:c0 + _HALF_S] = _dot(vh, w_ref[:, c0:c0 + _HALF_S])


def _half_contract(x_ref, w_ref, half):
    out = None
    for part in range(2):
        r0 = part * NSTATE + half * _HALF_S
        term = _dot_nt(x_ref[:, r0:r0 + _HALF_S].astype(BF16), w_ref[:, r0:r0 + _HALF_S])
        out = term if out is None else out + term
    return out


def _half_outer(v, x_ref, acc_ref):
    for half in range(2):
        vh = v[:, half * _HALF_W:(half + 1) * _HALF_W]
        for part in range(2):
            c0 = part * NSTATE + half * _HALF_S
            acc_ref[:, c0:c0 + _HALF_S] += _dot_tn(vh, x_ref[:, c0:c0 + _HALF_S].astype(BF16))


def _ssm_fwd(rest, bd, cd, consts):
    s = rest.shape[0]
    tb = min(TB_SSM, s)
    ns2 = 2 * NSTATE

    def body(u_ref, bd_ref, cd_ref, cf_ref, y_ref, x_ref, cb_ref):
        @pl.when(pl.program_id(0) == 0)
        def _():
            cb_ref[...] = jnp.zeros_like(cb_ref)

        _half_expand(u_ref[...].astype(BF16), bd_ref, x_ref)

        def tile(ti, _):
            r0 = pl.multiple_of(ti * 8, 8)
            for cc in range(NSTATE // _SCAN_W):
                cr = pl.ds(cc * _SCAN_W, _SCAN_W)
                ci = pl.ds(NSTATE + cc * _SCAN_W, _SCAN_W)
                re = x_ref[pl.ds(r0, 8), cr]
                im = x_ref[pl.ds(r0, 8), ci]
                for n_, k in enumerate((1, 2, 4)):
                    ar = cf_ref[2 * n_, :, cr]
                    ai = cf_ref[2 * n_ + 1, :, cr]
                    sr = pltpu.roll(re, k, 0)
                    si = pltpu.roll(im, k, 0)
                    re, im = re + ar * sr - ai * si, im + ar * si + ai * sr
                pr = cf_ref[6, :, cr]
                pi_ = cf_ref[7, :, cr]
                cbr = cb_ref[:, cr]
                cbi = cb_ref[:, ci]
                re, im = re + pr * cbr - pi_ * cbi, im + pr * cbi + pi_ * cbr
                x_ref[pl.ds(r0, 8), cr] = re
                x_ref[pl.ds(r0, 8), ci] = im
                cb_ref[:, cr] = jnp.broadcast_to(re[7:8, :], (8, _SCAN_W))
                cb_ref[:, ci] = jnp.broadcast_to(im[7:8, :], (8, _SCAN_W))
            return 0

        lax.fori_loop(0, tb // 8, tile, 0)
        for half in range(2):
            y_ref[:, half * _HALF_W:(half + 1) * _HALF_W] = _half_contract(x_ref, cd_ref, half)

    return pl.pallas_call(
        body, name="ssm_fwd", grid=(s // tb,),
        in_specs=[pl.BlockSpec((tb, S5_W), lambda i: (i, R_U // S5_W)), _const((_HALF_W, ns2)), _const((_HALF_W, ns2)),
                  _const((8, 8, NSTATE))],
        out_specs=(pl.BlockSpec((tb, S5_W), lambda i: (i, 0)), pl.BlockSpec((tb, ns2), lambda i: (i, 0))),
        out_shape=(jax.ShapeDtypeStruct((s, S5_W), F32), jax.ShapeDtypeStruct((s, ns2), F32)),
        scratch_shapes=[pltpu.VMEM((8, ns2), F32)],
        compiler_params=_cparams(("arbitrary",)))(rest, bd, cd, consts)


def _ssm_bwd(dys, xs, rest, bd, cd, consts, dskip):
    s = dys.shape[0]
    tb = min(TB_SSM, s)
    nb = s // tb
    ns2 = 2 * NSTATE
    nt = tb // 8

    def body(dy_ref, x_ref, u_ref, bd_ref, cd_ref, cf_ref, dsk_ref, du_ref, gb_ref, gc_ref, da_ref,
             g_ref, cb_ref, acc_b, acc_c):
        step = pl.program_id(0)

        @pl.when(step == 0)
        def _():
            cb_ref[...] = jnp.zeros_like(cb_ref)
            acc_b[...] = jnp.zeros_like(acc_b)
            acc_c[...] = jnp.zeros_like(acc_c)
            da_ref[...] = jnp.zeros_like(da_ref)

        dy = dy_ref[...]
        dyb = dy.astype(BF16)
        _half_expand(dyb, cd_ref, g_ref)
        last_row = lax.broadcasted_iota(jnp.int32, (8, _SCAN_W), 0) == 7

        def tile(tt, _):
            r0 = pl.multiple_of((nt - 1 - tt) * 8, 8)
            for cc in range(NSTATE // _SCAN_W):
                cr = pl.ds(cc * _SCAN_W, _SCAN_W)
                ci = pl.ds(NSTATE + cc * _SCAN_W, _SCAN_W)
                re = g_ref[pl.ds(r0, 8), cr]
                im = g_ref[pl.ds(r0, 8), ci]
                for n_, k in enumerate((1, 2, 4)):
                    ar = cf_ref[2 * n_, :, cr]
                    ai = cf_ref[2 * n_ + 1, :, cr]
                    sr = pltpu.roll(re, 8 - k, 0)
                    si = pltpu.roll(im, 8 - k, 0)
                    re, im = re + ar * sr - ai * si, im + ar * si + ai * sr
                pr = cf_ref[6, :, cr]
                pi_ = cf_ref[7, :, cr]
                cbr = cb_ref[:, cr]
                cbi = cb_ref[:, ci]
                re, im = re + pr * cbr - pi_ * cbi, im + pr * cbi + pi_ * cbr
                g_ref[pl.ds(r0, 8), cr] = re
                g_ref[pl.ds(r0, 8), ci] = im
                gnr = jnp.where(last_row, cbr, pltpu.roll(re, 7, 0))
                gni = jnp.where(last_row, cbi, pltpu.roll(im, 7, 0))
                xr = x_ref[pl.ds(r0, 8), cr]
                xi = x_ref[pl.ds(r0, 8), ci]
                da_ref[:, cr] += gnr * xr + gni * xi
                da_ref[:, ci] += gni * xr - gnr * xi
                cb_ref[:, cr] = jnp.broadcast_to(re[0:1, :], (8, _SCAN_W))
                cb_ref[:, ci] = jnp.broadcast_to(im[0:1, :], (8, _SCAN_W))
            return 0

        lax.fori_loop(0, nt, tile, 0)
        for half in range(2):
            cols = slice(half * _HALF_W, (half + 1) * _HALF_W)
            du_ref[:, cols] = (_half_contract(g_ref, bd_ref, half) + dy[:, cols] * dsk_ref[:, cols]).astype(BF16)
        _half_outer(u_ref[...].astype(BF16), g_ref, acc_b)
        _half_outer(dyb, x_ref, acc_c)

        @pl.when(step == nb - 1)
        def _():
            for g in range(GROUPS):
                src = slice((g % (GROUPS // 2)) * GCH, (g % (GROUPS // 2) + 1) * GCH)
                dst = slice(g * GCH, (g + 1) * GCH)
                for part in range(2):
                    cols = slice(part * NSTATE + g * STATE, part * NSTATE + (g + 1) * STATE)
                    gb_ref[dst, part * STATE:(part + 1) * STATE] = acc_b[src, cols]
                    gc_ref[dst, part * STATE:(part + 1) * STATE] = acc_c[src, cols]

    rev = lambda i: (nb - 1 - i, 0)
    small = pl.BlockSpec((S5_W, 2 * STATE), lambda i: (0, 0))
    return pl.pallas_call(
        body, name="ssm_bwd", grid=(nb,),
        in_specs=[pl.BlockSpec((tb, S5_W), rev), pl.BlockSpec((tb, ns2), rev),
                  pl.BlockSpec((tb, S5_W), lambda i: (nb - 1 - i, R_U // S5_W)),
                  _const((_HALF_W, ns2)), _const((_HALF_W, ns2)), _const((8, 8, NSTATE)), _const((1, S5_W))],
        out_specs=(pl.BlockSpec((tb, S5_W), rev), small, small, pl.BlockSpec((8, ns2), lambda i: (0, 0))),
        out_shape=(jax.ShapeDtypeStruct((s, S5_W), BF16), jax.ShapeDtypeStruct((S5_W, 2 * STATE), F32),
                   jax.ShapeDtypeStruct((S5_W, 2 * STATE), F32), jax.ShapeDtypeStruct((8, ns2), F32)),
        scratch_shapes=[pltpu.VMEM((tb, ns2), F32), pltpu.VMEM((8, ns2), F32),
                        pltpu.VMEM((_HALF_W, ns2), F32), pltpu.VMEM((_HALF_W, ns2), F32)],
        compiler_params=_cparams(("arbitrary",)))(dys, xs, rest, bd, cd, consts, dskip)


_GELU_C = math.sqrt(2.0 / math.pi)
_GELU_A = 0.044715


def _mid(o, rest, ys0, x, tgt, w, vec, hsel):
    s = o.shape[0]
    tm = min(TM, s)
    nsteps = s // tm
    half = FOX_W

    def body(o_ref, ga_ref, gb_ref, za_ref, u_ref, zb_ref, ys0_ref, x_ref, t_ref,
             wglu_ref, wua_ref, wub_ref, wout_ref, vec_ref, hsel_ref,
             dx2_ref, dga_ref, dgb_ref, do_ref, dza_ref, dzb_ref, dys_ref, dlt_ref,
             gout_hbm, gua_hbm, gub_hbm, gglu_hbm, vout_ref,
             a_out, a_ua, a_ub, a_glu):
        step = pl.program_id(0)

        @pl.when(step == 0)
        def _():
            a_out[...] = jnp.zeros_like(a_out)
            a_ua[...] = jnp.zeros_like(a_ua)
            a_ub[...] = jnp.zeros_like(a_ub)
            a_glu[...] = jnp.zeros_like(a_glu)
            vout_ref[...] = jnp.zeros_like(vout_ref)

        gate = vec_ref[0:1, :]
        gfin = vec_ref[1:2, :]
        dsk = vec_ref[2:3, 0:half]
        bglu = vec_ref[2:3, half:2 * half]

        o_v = o_ref[...]
        za = za_ref[...]
        sza = _sigmoid(za)
        silu_za = za * sza
        ya_b = (o_v * silu_za).astype(BF16)
        u_v = u_ref[...]
        ys = ys0_ref[...] + dsk * u_v
        inner = _GELU_C * (ys + _GELU_A * ys * ys * ys)
        th = jnp.tanh(inner)
        yg = 0.5 * ys * (1.0 + th)
        yg_b = yg.astype(BF16)
        st = _sigmoid(_dot(yg_b, wglu_ref[...]) + bglu)
        yb1 = yg * st
        zb = zb_ref[...]
        szb = _sigmoid(zb)
        silu_zb = zb * szb
        yb_b = (yb1 * silu_zb).astype(BF16)
        ua = _dot(ya_b, wua_ref[...])
        ub = _dot(yb_b, wub_ref[...])
        sga = _sigmoid(ga_ref[...])
        sgb = _sigmoid(gb_ref[...])
        merged_b = (sga * ua + sgb * ub).astype(BF16)
        mo = _dot(merged_b, wout_ref[...])
        x2 = x_ref[...] + gate * mo
        r2 = lax.rsqrt(jnp.mean(x2 * x2, axis=-1, keepdims=True) + EPS)
        x2n = x2 * r2
        diff = x2n * gfin - t_ref[...]
        loss = 0.5 * jnp.sum(jnp.mean(diff * diff, axis=-1, keepdims=True), axis=0, keepdims=True)
        dy = diff * (1.0 / D_MODEL)
        dx2n = dy * gfin
        dx2 = r2 * (dx2n - x2n * jnp.mean(dx2n * x2n, axis=-1, keepdims=True))
        dx2_ref[...] = dx2
        vout_ref[0:1, :] += jnp.sum(dy * x2n, axis=0, keepdims=True)
        vout_ref[1:2, :] += jnp.sum(dx2 * mo, axis=0, keepdims=True)
        vout_ref[3:4, :] += jnp.broadcast_to(loss, (1, D_MODEL))
        dmo_b = (dx2 * gate).astype(BF16)
        dmerged = _dot_nt(dmo_b, wout_ref[...])
        a_out[...] += _dot_tn(merged_b, dmo_b)
        dua_b = (dmerged * sga).astype(BF16)
        dub_b = (dmerged * sgb).astype(BF16)
        dga_ref[...] = (dmerged * ua * sga * (1.0 - sga)).astype(BF16)
        dgb_ref[...] = (dmerged * ub * sgb * (1.0 - sgb)).astype(BF16)
        dya = _dot_nt(dua_b, wua_ref[...])
        dyb = _dot_nt(dub_b, wub_ref[...])
        a_ua[...] += _dot_tn(ya_b, dua_b)
        a_ub[...] += _dot_tn(yb_b, dub_b)
        do_b = (dya * silu_za).astype(BF16)
        do_ref[...] = do_b
        dza_ref[...] = (dya * o_v * (sza * (1.0 + za * (1.0 - sza)))).astype(BF16)
        hsel = hsel_ref[...].astype(BF16)
        dlt_ref[...] = sum(_dot_nt(hsel, part) for part in _split3(do_b.astype(F32) * o_v))
        dyb1 = dyb * silu_zb
        dzb_ref[...] = (dyb * yb1 * (szb * (1.0 + zb * (1.0 - szb)))).astype(BF16)
        dt = dyb1 * yg * st * (1.0 - st)
        dt_b = dt.astype(BF16)
        dyg = dyb1 * st + _dot_nt(dt_b, wglu_ref[...])
        a_glu[...] += _dot_tn(yg_b, dt_b)
        dgelu = 0.5 * (1.0 + th) + 0.5 * ys * (1.0 - th * th) * _GELU_C * (1.0 + 3.0 * _GELU_A * ys * ys)
        dys = dyg * dgelu
        dys_ref[...] = dys
        vout_ref[2:3, 0:half] += jnp.sum(dys * u_v, axis=0, keepdims=True)
        vout_ref[2:3, half:2 * half] += jnp.sum(dt, axis=0, keepdims=True)

        @pl.when(step == nsteps - 1)
        def _():
            pltpu.sync_copy(a_out, gout_hbm)
            pltpu.sync_copy(a_ua, gua_hbm)
            pltpu.sync_copy(a_ub, gub_hbm)
            pltpu.sync_copy(a_glu, gglu_hbm)

    def rows(width, col=0):
        return pl.BlockSpec((tm, width), lambda i, col=col: (i, col))

    anyspace = pl.BlockSpec(memory_space=pl.ANY)
    wshapes = [(S5_W, S5_W), (FOX_W, D_MODEL), (S5_W, D_MODEL), (D_MODEL, D_MODEL)]
    return pl.pallas_call(
        body, name="mid", grid=(nsteps,),
        in_specs=[rows(FOX_W), rows(D_MODEL, R_GA // D_MODEL), rows(D_MODEL, R_GB // D_MODEL),
                  rows(FOX_W, R_ZA // FOX_W), rows(S5_W, R_U // S5_W), rows(S5_W, R_ZB // S5_W),
                  rows(S5_W), rows(D_MODEL), rows(D_MODEL)]
                 + [_const(sh) for sh in wshapes]
                 + [_const((8, D_MODEL)), _const((HEADS, FOX_W))],
        out_specs=(rows(D_MODEL), rows(D_MODEL), rows(D_MODEL), rows(FOX_W), rows(FOX_W), rows(S5_W), rows(S5_W),
                   pl.BlockSpec((HEADS, tm), lambda i: (0, i)),
                   anyspace, anyspace, anyspace, anyspace, pl.BlockSpec((8, D_MODEL), lambda i: (0, 0))),
        out_shape=(jax.ShapeDtypeStruct((s, D_MODEL), F32), jax.ShapeDtypeStruct((s, D_MODEL), BF16),
                   jax.ShapeDtypeStruct((s, D_MODEL), BF16), jax.ShapeDtypeStruct((s, FOX_W), BF16),
                   jax.ShapeDtypeStruct((s, FOX_W), BF16), jax.ShapeDtypeStruct((s, S5_W), BF16),
                   jax.ShapeDtypeStruct((s, S5_W), F32), jax.ShapeDtypeStruct((HEADS, s), F32),
                   jax.ShapeDtypeStruct((D_MODEL, D_MODEL), F32), jax.ShapeDtypeStruct((FOX_W, D_MODEL), F32),
                   jax.ShapeDtypeStruct((S5_W, D_MODEL), F32), jax.ShapeDtypeStruct((S5_W, S5_W), F32),
                   jax.ShapeDtypeStruct((8, D_MODEL), F32)),
        scratch_shapes=[pltpu.VMEM((D_MODEL, D_MODEL), F32), pltpu.VMEM((FOX_W, D_MODEL), F32),
                        pltpu.VMEM((S5_W, D_MODEL), F32), pltpu.VMEM((S5_W, S5_W), F32)],
        compiler_params=_cparams(("arbitrary",)),
    )(o, rest, rest, rest, rest, rest, ys0, x, tgt, *w, vec, hsel)


def _dh(dq, dk, dv, dga, dgb, dza, du, dzb, df, wqkv_t, wrest_t, x, dx2, gs, scatter_srcs):
    s = x.shape[0]
    tm = min(TM, s)
    nsteps = s // tm
    na = len(scatter_srcs)

    def body(dq_ref, dk_ref, dv_ref, dga_ref, dgb_ref, dza_ref, du_ref, dzb_ref, df_ref, wq_ref, wr_ref,
             x_ref, dx2_ref, gs_ref, *rest_refs):
        src_refs = rest_refs[:na]
        gx_ref, vout_ref = rest_refs[na:na + 2]
        out_refs = rest_refs[na + 2:2 * na + 2]
        send_sems, recv_sems = rest_refs[2 * na + 2:]
        step = pl.program_id(0)
        cx, cy, cc = lax.axis_index("x"), lax.axis_index("y"), lax.axis_index("c")
        peers = [(1 - cx, cy), (cx, 1 - cy), (1 - cx, 1 - cy)]

        def copy(a, k, px, py, slot):
            return pltpu.make_async_remote_copy(
                src_ref=src_refs[a].at[2 * px + py], dst_ref=out_refs[a].at[slot],
                send_sem=send_sems.at[a * 3 + k], recv_sem=recv_sems.at[a * 3 + k],
                device_id=(px, py, cc), device_id_type=MESH)

        @pl.when(step == 0)
        def _():
            vout_ref[...] = jnp.zeros_like(vout_ref)
            for a in range(na):
                for k, (px, py) in enumerate(peers):
                    copy(a, k, px, py, 2 * cx + cy).start()

        dh = _dot(dq_ref[...], wq_ref[0:512, :])
        dh += _dot(dk_ref[...], wq_ref[512:1024, :])
        dh += _dot(dv_ref[...], wq_ref[1024:1536, :])
        dh += _dot(dga_ref[...], wr_ref[R_GA:R_GB, :])
        dh += _dot(dgb_ref[...], wr_ref[R_GB:R_ZA, :])
        dh += _dot(dza_ref[...], wr_ref[R_ZA:R_U, :])
        dh += _dot(du_ref[...], wr_ref[R_U:R_ZB, :])
        dh += _dot(dzb_ref[...], wr_ref[R_ZB:R_F, :])
        dh += _dot(df_ref[...], wr_ref[R_F:REST_W, :])
        xv = x_ref[...]
        r = lax.rsqrt(jnp.mean(xv * xv, axis=-1, keepdims=True) + EPS)
        xn = xv * r
        dxn = dh * gs_ref[...]
        gx_ref[...] = dx2_ref[...] + r * (dxn - xn * jnp.mean(dxn * xn, axis=-1, keepdims=True))
        vout_ref[0:1, :] += jnp.sum(dh * xn, axis=0, keepdims=True)
        vout_ref[1:2, :] += jnp.sum(dh, axis=0, keepdims=True)

        @pl.when(step == nsteps - 1)
        def _():
            for a in range(na):
                for k, (px, py) in enumerate(peers):
                    copy(a, k, px, py, 2 * px + py).wait_recv()
            for a in range(na):
                for k, (px, py) in enumerate(peers):
                    copy(a, k, px, py, 2 * cx + cy).wait_send()

    def rows(width):
        return pl.BlockSpec((tm, width), lambda i: (i, 0))

    anyspace = pl.BlockSpec(memory_space=pl.ANY)
    return pl.pallas_call(
        body, name="dh", grid=(nsteps,),
        in_specs=[rows(512), rows(512), rows(512), rows(1024), rows(1024), rows(512), rows(512), rows(512), rows(128),
                  _const((1536, D_MODEL)), _const((REST_W, D_MODEL)), rows(D_MODEL), rows(D_MODEL), _const((1, D_MODEL))]
                 + [anyspace] * na,
        out_specs=(rows(D_MODEL), pl.BlockSpec((8, D_MODEL), lambda i: (0, 0))) + (anyspace,) * na,
        out_shape=(jax.ShapeDtypeStruct((s, D_MODEL), F32), jax.ShapeDtypeStruct((8, D_MODEL), F32))
                  + tuple(jax.ShapeDtypeStruct(a.shape, a.dtype) for a in scatter_srcs),
        scratch_shapes=[pltpu.SemaphoreType.DMA((3 * na,)), pltpu.SemaphoreType.DMA((3 * na,))],
        compiler_params=_cparams(("arbitrary",)),
    )(dq, dk, dv, dga, dgb, dza, du, dzb, df, wqkv_t, wrest_t, x, dx2, gs, *scatter_srcs)


def _row_block(rows, mult=8, cap=512):
    if rows <= mult:
        return rows
    padded = -(-rows // mult) * mult
    for cand in range(min(cap, padded) // mult * mult, 0, -mult):
        if padded % cand == 0:
            return cand
    return padded


def _sum4(parts, name):
    rows, cols = parts.shape[1:]
    br = _row_block(rows, 16, 1024)

    def body(p_ref, o_ref):
        acc = p_ref[0].astype(F32)
        for k in range(1, 4):
            acc = acc + p_ref[k].astype(F32)
        o_ref[...] = acc

    return pl.pallas_call(
        body, name=name, grid=(pl.cdiv(rows, br),),
        in_specs=[pl.BlockSpec((4, br, cols), lambda i: (0, i, 0))],
        out_specs=pl.BlockSpec((br, cols), lambda i: (i, 0)),
        out_shape=jax.ShapeDtypeStruct((rows, cols), F32), compiler_params=_cparams(("parallel",)))(parts)


def _pair_add(a, b, name):
    shape = a.shape
    a, b = a.reshape(-1, shape[-1]), b.reshape(-1, shape[-1])
    rows, cols = a.shape
    br = _row_block(rows, 16, 1024)

    def body(a_ref, b_ref, o_ref):
        o_ref[...] = (a_ref[...].astype(F32) + b_ref[...].astype(F32)).astype(BF16)

    spec = pl.BlockSpec((br, cols), lambda i: (i, 0))
    return pl.pallas_call(
        body, name=name, grid=(pl.cdiv(rows, br),), in_specs=[spec, spec], out_specs=spec,
        out_shape=jax.ShapeDtypeStruct((rows, cols), BF16), compiler_params=_cparams(("parallel",)))(a, b).reshape(shape)


def _adamw(w, g, m, v, name):
    rows, cols = w.shape
    br = _row_block(rows)

    def body(w_ref, g_ref, m_ref, v_ref, d_ref, nm_ref, nv_ref):
        gv = g_ref[...]
        nm = ADAM_B1 * m_ref[...] + (1.0 - ADAM_B1) * gv
        nv = ADAM_B2 * v_ref[...] + (1.0 - ADAM_B2) * (gv * gv)
        m_hat = nm / (1.0 - ADAM_B1 ** ADAM_STEP)
        v_hat = nv / (1.0 - ADAM_B2 ** ADAM_STEP)
        d_ref[...] = -ADAM_LR * (m_hat / (jnp.sqrt(v_hat) + ADAM_EPS) + ADAM_WD * w_ref[...])
        nm_ref[...] = nm
        nv_ref[...] = nv

    spec = pl.BlockSpec((br, cols), lambda i: (i, 0))
    shape = jax.ShapeDtypeStruct((rows, cols), F32)
    return pl.pallas_call(
        body, name=name, grid=(pl.cdiv(rows, br),), in_specs=[spec] * 4, out_specs=(spec,) * 3,
        out_shape=(shape,) * 3, compiler_params=_cparams(("parallel",)))(w, g, m, v)


def _pack(parts, row_multiple=8):
    flat = []
    for p in parts:
        v = p.reshape(-1).astype(F32)
        pad = (-v.shape[0]) % LANES
        if pad:
            v = jnp.concatenate([v, jnp.zeros((pad,), F32)])
        flat.append(v)
    v = jnp.concatenate(flat)
    rows = v.shape[0] // LANES
    pad_rows = (-rows) % row_multiple
    if pad_rows:
        v = jnp.concatenate([v, jnp.zeros((pad_rows * LANES,), F32)])
    return v.reshape(-1, LANES)


def _unpack(packed, shapes):
    lead = packed.shape[:-2]
    flat = packed.reshape(lead + (-1,))
    out, off = [], 0
    for sh in shapes:
        size = math.prod(sh)
        out.append(flat[..., off:off + size].reshape(lead + tuple(sh)))
        off += size + (-size) % LANES
    return out


def kernel(x, c, w_ada, b_ada, g_norm, w_in, b_f, a_re, a_im, log_dt, b_re, b_im, c_re, c_im, d_skip, w_glu, b_glu, w_up_a, w_up_b, w_out, g_final, loss_target, m_w_ada, m_b_ada, m_g_norm, m_w_in, m_b_f, m_a_re, m_a_im, m_log_dt, m_b_re, m_b_im, m_c_re, m_c_im, m_d_skip, m_w_glu, m_b_glu, m_w_up_a, m_w_up_b, m_w_out, m_g_final, v_w_ada, v_b_ada, v_g_norm, v_w_in, v_b_f, v_a_re, v_a_im, v_log_dt, v_b_re, v_b_im, v_c_re, v_c_im, v_d_skip, v_w_glu, v_b_glu, v_w_up_a, v_w_up_b, v_w_out, v_g_final):
    xi, yi, ci = lax.axis_index("x"), lax.axis_index("y"), lax.axis_index("c")
    chip = 2 * xi + yi
    me = 4 * xi + 2 * yi + ci
    s = x.shape[1]
    x2d = x[0]
    tgt = loss_target[0]
    n_att = s // min(T_ATT, s)
    t_att = min(T_ATT, s)

    c_all, _ = _allgather8(c.reshape(8, LANES), "gather_c")
    c_all = c_all.reshape(8, D_MODEL)
    ncol = w_ada.shape[2]
    b_cols = lax.dynamic_slice_in_dim(b_ada, chip * ncol, ncol, axis=1)
    mod_cols = _mod_cols(c_all, w_ada[0], b_cols)
    mod_all, _ = _allgather8(mod_cols.reshape(-1, LANES), "gather_mod")
    mod_all = mod_all.reshape(4, 2, 8, ncol)[:, 0]
    mod_me = lax.dynamic_index_in_dim(mod_all, me, axis=1, keepdims=False).reshape(1, 3 * D_MODEL)
    shift, scale, gate = mod_me[:, :D_MODEL], mod_me[:, D_MODEL:2 * D_MODEL], mod_me[:, 2 * D_MODEL:]
    gs = g_norm * (1.0 + scale)

    nshard = w_in.shape[2]
    w_in_t, m_in_t, v_in_t = (jnp.swapaxes(a[0], 0, 1) for a in (w_in, m_w_in, v_w_in))
    wt_pack = jnp.pad(w_in_t.astype(BF16), ((0, SHARD_ROWS - nshard), (0, 0)))
    misc_shapes = [w_glu.shape[1:], w_up_a.shape[1:], w_up_b.shape[1:], w_out.shape[1:]]
    misc_pack = jnp.concatenate([w.reshape(-1) for w in (w_glu, w_up_a, w_up_b, w_out)]).astype(BF16).reshape(-1, LANES)
    def halves(a):
        return a.reshape((2, a.shape[0] // 2) + a.shape[1:])

    wt_all, misc_all = _gather_shards([halves(wt_pack), halves(misc_pack)], "gather_weights")
    wt_all = lax.dynamic_update_index_in_dim(wt_all, halves(wt_pack), chip, 0).reshape((4,) + wt_pack.shape)
    misc_all = lax.dynamic_update_index_in_dim(misc_all, halves(misc_pack), chip, 0).reshape((4,) + misc_pack.shape)
    p_glu, p_ua, p_ub, p_out = _unpack(misc_all, misc_shapes)

    def w_rows(lo, hi):
        out = []
        for j in range(4):
            a, b = max(lo, j * nshard), min(hi, (j + 1) * nshard)
            if a < b:
                out.append(wt_all[j, a - j * nshard:b - j * nshard])
        return out

    wqkv_t = jnp.concatenate(w_rows(O_Q, O_F), axis=0)
    wrest_t = jnp.concatenate(w_rows(O_GA, O_GB) + w_rows(O_GB, O_END) + w_rows(O_ZA, O_U) + w_rows(O_U, O_ZB)
                              + w_rows(O_ZB, O_GA) + w_rows(O_F, O_ZA)
                              + [jnp.zeros((REST_W - R_F - HEADS, D_MODEL), BF16)], axis=0)
    wmid = (p_glu.reshape(S5_W, S5_W), jnp.concatenate([p_ua[j] for j in range(4)], axis=1),
            jnp.concatenate([p_ub[j] for j in range(4)], axis=1), p_out.reshape(D_MODEL, D_MODEL))

    h, qkv, rest = _prenorm_proj(x2d, gs, shift, wqkv_t, wrest_t)
    bf128 = jnp.pad(b_f, ((0, 0), (0, LANES - HEADS)))
    selp = _head_pair_selector()
    fpc, f_t = _fcum(rest, bf128, selp)
    frow5 = f_t.reshape(4, 2, n_att, 1, t_att)
    o, lse_pc = _attn_fwd(qkv, frow5, fpc)

    abar_r, abar_i, bb_r, bb_i = _ssm_block_params(a_re[0], a_im[0], log_dt[0], b_re[0], b_im[0])
    bb_rt, bb_it = jnp.swapaxes(bb_r, 1, 2).astype(BF16), jnp.swapaxes(bb_i, 1, 2).astype(BF16)
    cr_b, ci_b = c_re[0].astype(BF16), (-c_im[0]).astype(BF16)
    bd_c, cd_c = _compact_diag(bb_rt, bb_it), _compact_diag(cr_b, ci_b)
    ys0, xs = _ssm_fwd(rest, bd_c, cd_c, _scan_consts(a_re[0], a_im[0], log_dt[0], False))

    vec = jnp.concatenate([gate, g_final.reshape(1, D_MODEL), jnp.concatenate([d_skip, b_glu], axis=1),
                           jnp.zeros((5, D_MODEL), F32)], axis=0)
    hsel = jnp.repeat(jnp.eye(HEADS, dtype=F32), HEAD_DIM, axis=1)
    (dx2, dga, dgb, do, dza, dzb, dys, dlt_t, g_out, g_ua, g_ub, g_glu, vmid) = _mid(
        o, rest, ys0, x2d, tgt, wmid, vec, hsel)

    lse_t = jnp.transpose(lse_pc.reshape(s, 4 // ATT_PAIRS, LANES)[:, :, :2 * ATT_PAIRS], (1, 2, 0))
    lse5 = lse_t.reshape(4, 2, n_att, 1, t_att)
    dlt5 = dlt_t.reshape(4, 2, n_att, 1, t_att)
    dq, dk, dv, dfk, dfq = _attn_bwd(qkv, do, lse5, dlt5, frow5, fpc)
    du, g_bd, g_cdt, da8 = _ssm_bwd(dys, xs, rest, bd_c, cd_c, _scan_consts(a_re[0], a_im[0], log_dt[0], True), d_skip)
    df, dbf8 = _dfcum(dfk, dfq, rest, bf128, selp.T)

    gq, gk, gv, gga, ggb, gza, gu, gzb, gf = _grad_w_rows(h, [dq, dk, dv, dga, dgb, dza, du, dzb, df])
    g_in_t = jnp.concatenate([gq, gk, gv, gf[:HEADS], gza, gu, gzb, gga, ggb], axis=0)

    def shard_cols(g, j):
        n = g.shape[1] // 4
        return g[:, j * n:(j + 1) * n]

    def shard_rows(g, j):
        n = g.shape[0] // 4
        return g[j * n:(j + 1) * n]

    def halves4(a):
        return a.reshape((4, 2, a.shape[1] // 2) + a.shape[2:])

    gt_pack = halves4(jnp.stack([
        jnp.pad(g_in_t[j * nshard:(j + 1) * nshard].astype(BF16), ((0, SHARD_ROWS - nshard), (0, 0)))
        for j in range(4)]))
    gm_pack = halves4(jnp.stack([
        jnp.concatenate([shard_rows(g_glu, j).reshape(-1), shard_cols(g_ua, j).reshape(-1),
                         shard_cols(g_ub, j).reshape(-1), shard_rows(g_out, j).reshape(-1)]).astype(BF16)
        .reshape(-1, LANES) for j in range(4)]))
    recv_in, recv_misc = _swap_sibling([gt_pack, gm_pack], "pair_swap_weight_grads", other_half=True)
    own_in = lax.dynamic_index_in_dim(gt_pack, ci, axis=1, keepdims=False)
    own_misc = lax.dynamic_index_in_dim(gm_pack, ci, axis=1, keepdims=False)
    pair_in = _pair_add(own_in, recv_in, "pair_add_w_in")
    pair_misc = _pair_add(own_misc, recv_misc, "pair_add_misc")

    grad_x, vdh, parts_in, parts_misc = _dh(dq, dk, dv, dga, dgb, dza, du, dzb, df, wqkv_t, wrest_t, x2d, dx2, gs,
                                            [pair_in, pair_misc])
    parts_in = lax.dynamic_update_slice_in_dim(parts_in, lax.dynamic_slice_in_dim(pair_in, chip, 1, 0), chip, 0)
    parts_misc = lax.dynamic_update_slice_in_dim(parts_misc, lax.dynamic_slice_in_dim(pair_misc, chip, 1, 0), chip, 0)
    half_in, half_misc = _sum4(parts_in, "sum4_w_in"), _sum4(parts_misc, "sum4_misc")
    sib_in, sib_misc = _swap_sibling([half_in, half_misc], "swap_weight_grads")

    def both_halves(mine, theirs):
        return jnp.concatenate([jnp.where(ci == 0, mine, theirs), jnp.where(ci == 0, theirs, mine)], axis=0)

    tot_in, tot_misc = both_halves(half_in, sib_in), both_halves(half_misc, sib_misc)
    g_glu_s, g_ua_s, g_ub_s, g_out_s = _unpack(tot_misc, misc_shapes)

    dgs, dshift = vdh[0:1], vdh[1:2]
    dmod = jnp.concatenate([dshift, dgs * g_norm, vmid[1:2]], axis=1)
    da = jnp.sum(da8, axis=0)
    g_bd = g_bd.reshape(GROUPS, GCH, 2 * STATE)
    g_cdt = g_cdt.reshape(GROUPS, GCH, 2 * STATE)
    g_bbr = jnp.swapaxes(g_bd[:, :, :STATE], 1, 2)
    g_bbi = jnp.swapaxes(g_bd[:, :, STATE:], 1, 2)
    g_cre = g_cdt[:, :, :STATE]
    g_cim = -g_cdt[:, :, STATE:]
    small_shapes = [(1,), (3 * D_MODEL,), (D_MODEL,), (HEADS,), (GROUPS, STATE), (GROUPS, STATE),
                    (GROUPS, STATE, GCH), (GROUPS, STATE, GCH), (GROUPS, GCH, STATE), (GROUPS, GCH, STATE),
                    (S5_W,), (S5_W,), (D_MODEL,)]
    small = _pack([vmid[3, 0:1], dmod, dgs * (1.0 + scale), dbf8[0, :HEADS], da[:NSTATE], da[NSTATE:],
                   g_bbr, g_bbi, g_cre, g_cim, vmid[2, :S5_W], vmid[2, S5_W:], vmid[0]])
    small_all, small_sum = _allgather8(small, "gather_small_grads")
    (loss_s, g_b_ada, g_g_norm, g_b_f, g_abr, g_abi, g_bbr_s, g_bbi_s, g_c_re, g_c_im, g_d_skip, g_b_glu,
     g_g_final) = _unpack(small_sum, small_shapes)
    loss = loss_s[0]
    dmod_all = _unpack(small_all, small_shapes)[1]
    dmod_cols = lax.dynamic_slice_in_dim(dmod_all, chip * ncol, ncol, axis=1)
    g_w_ada = _grad_w_ada(c_all, dmod_cols)
    _, ssm_vjp = jax.vjp(_ssm_block_params, a_re[0], a_im[0], log_dt[0], b_re[0], b_im[0])
    g_a_re, g_a_im, g_log_dt, g_b_re, g_b_im = ssm_vjp((g_abr, g_abi, g_bbr_s, g_bbi_s))

    def adam(name, w, g, m, v):
        shape = w.shape
        total = math.prod(shape)
        if len(shape) > 1 and shape[-1] >= LANES:
            cols = shape[-1]
        elif total % LANES == 0:
            cols = LANES
        else:
            cols = total
        two = lambda a: a.reshape(-1, cols)
        d, nm, nv = _adamw(two(w), two(g), two(m), two(v), "adamw_" + name)
        return g.reshape(shape), d.reshape(shape), nm.reshape(shape), nv.reshape(shape)

    back = lambda a: jnp.swapaxes(a, 0, 1)[None]
    d_in_t, nm_in_t, nv_in_t = _adamw(w_in_t, tot_in, m_in_t, v_in_t, "adamw_w_in")
    res_w_in = (back(tot_in[:nshard]), back(d_in_t), back(nm_in_t), back(nv_in_t))

    res = [
        adam("w_ada", w_ada, g_w_ada, m_w_ada, v_w_ada),
        adam("b_ada", b_ada, g_b_ada, m_b_ada, v_b_ada),
        adam("g_norm", g_norm, g_g_norm, m_g_norm, v_g_norm),
        res_w_in,
        adam("b_f", b_f, g_b_f, m_b_f, v_b_f),
        adam("a_re", a_re, g_a_re, m_a_re, v_a_re),
        adam("a_im", a_im, g_a_im, m_a_im, v_a_im),
        adam("log_dt", log_dt, g_log_dt, m_log_dt, v_log_dt),
        adam("b_re", b_re, g_b_re, m_b_re, v_b_re),
        adam("b_im", b_im, g_b_im, m_b_im, v_b_im),
        adam("c_re", c_re, g_c_re, m_c_re, v_c_re),
        adam("c_im", c_im, g_c_im, m_c_im, v_c_im),
        adam("d_skip", d_skip, g_d_skip, m_d_skip, v_d_skip),
        adam("w_glu", w_glu, g_glu_s, m_w_glu, v_w_glu),
        adam("b_glu", b_glu, g_b_glu, m_b_glu, v_b_glu),
        adam("w_up_a", w_up_a, g_ua_s, m_w_up_a, v_w_up_a),
        adam("w_up_b", w_up_b, g_ub_s, m_w_up_b, v_w_up_b),
        adam("w_out", w_out, g_out_s, m_w_out, v_w_out),
        adam("g_final", g_final, g_g_final, m_g_final, v_g_final),
    ]
    grads = [r[0] for r in res]
    deltas = [r[1] for r in res]
    new_m = [r[2] for r in res]
    new_v = [r[3] for r in res]
    return (loss, grad_x[None], *grads, *deltas, *new_m, *new_v)
```

```python
import functools
import math

import jax
import jax.numpy as jnp
from jax import lax
from jax.experimental import pallas as pl
from jax.experimental.pallas import tpu as pltpu

F32 = jnp.float32
BF16 = jnp.bfloat16
HI = lax.Precision.HIGHEST
MESH = pl.DeviceIdType.MESH

D_MODEL = 1024
HEADS = 8
HEAD_DIM = 64
FOX_W = 512
S5_W = 512
GROUPS = 32
STATE = 64
GCH = 16
NSTATE = GROUPS * STATE
EPS = 1e-6
NEG = -1e30

ADAM_LR = 0.001
ADAM_B1 = 0.9
ADAM_B2 = 0.999
ADAM_EPS = 1e-08
ADAM_WD = 0.01
ADAM_STEP = 10

VMEM_LIMIT = 56 * 1024 * 1024
LANES = 128

TM = 256
TM_PROJ = 512
T_ATT = 512
ATT_CHUNK = 32
ATT_PAIRS = 4
TB_SSM = 512
TK_ACC = 512
TB_CUM = 256
SHARD_ROWS = 1312

O_Q, O_K, O_V, O_F, O_ZA, O_U, O_ZB, O_GA, O_GB, O_END = 0, 512, 1024, 1536, 1544, 2056, 2568, 3080, 4104, 5128
REST_W = 3712
R_GA, R_GB, R_ZA, R_U, R_ZB, R_F = 0, 1024, 2048, 2560, 3072, 3584


def _cparams(sem=None):
    kw = dict(vmem_limit_bytes=VMEM_LIMIT)
    if sem is not None:
        kw["dimension_semantics"] = sem
    return pltpu.CompilerParams(**kw)


def _const(shape):
    nd = len(shape)
    return pl.BlockSpec(shape, lambda *_: (0,) * nd, pipeline_mode=pl.Buffered(1))


def _dot(a, b, precision=None):
    return jnp.dot(a, b, preferred_element_type=F32, precision=precision)


def _dot_nt(a, b):
    return lax.dot_general(a, b, (((1,), (1,)), ((), ())), preferred_element_type=F32)


def _dot_tn(a, b, precision=None):
    return lax.dot_general(a, b, (((0,), (0,)), ((), ())), preferred_element_type=F32, precision=precision)


def _sigmoid(z):
    return 1.0 / (1.0 + jnp.exp(-z))


def _split3(x):
    hi = x.astype(BF16)
    r1 = x - hi.astype(F32)
    mid = r1.astype(BF16)
    lo = (r1 - mid.astype(F32)).astype(BF16)
    return hi, mid, lo


def _dot_sel(sel, x):
    s16 = sel.astype(BF16)
    return sum(_dot(s16, part) for part in _split3(x))


def _dot_by_sel(x, sel):
    s16 = sel.astype(BF16)
    return sum(_dot(part, s16) for part in _split3(x))


def _allgather8(xs, name):
    rows = xs.shape[0]

    def body(x_ref, out_ref, sum_ref, send_sems, recv_sems, local_sem):
        x, y, c = lax.axis_index("x"), lax.axis_index("y"), lax.axis_index("c")
        me, sibling = (x, y, c), (x, y, 1 - c)
        chips = [(1 - x, y), (x, 1 - y), (1 - x, 1 - y)]

        def slot(px, py, pc):
            return out_ref.at[4 * px + 2 * py + pc]

        def copy(k, block, to, src=None):
            return pltpu.make_async_remote_copy(
                src_ref=slot(*block) if src is None else src, dst_ref=slot(*block),
                send_sem=send_sems.at[k], recv_sem=recv_sems.at[k], device_id=to, device_id_type=MESH)

        mine = pltpu.make_async_copy(x_ref, slot(*me), local_sem)
        mine.start()
        first = [copy(0, me, sibling, src=x_ref)]
        first += [copy(1 + j, me, (*chip, c), src=x_ref) for j, chip in enumerate(chips)]
        for cp in first:
            cp.start()
        passed = [copy(4 + j, (*chip, c), sibling) for j, chip in enumerate(chips)]
        for j, chip in enumerate(chips):
            copy(1 + j, (*chip, c), me).wait_recv()
            passed[j].start()
        copy(0, sibling, me).wait_recv()
        for j, chip in enumerate(chips):
            copy(4 + j, (*chip, 1 - c), me).wait_recv()
        for cp in first + passed:
            cp.wait_send()
        mine.wait()
        acc = out_ref[0]
        for d in range(1, 8):
            acc = acc + out_ref[d]
        sum_ref[...] = acc

    return pl.pallas_call(
        body, name=name,
        out_shape=(jax.ShapeDtypeStruct((8, rows, LANES), F32), jax.ShapeDtypeStruct((rows, LANES), F32)),
        in_specs=[pl.BlockSpec(memory_space=pltpu.VMEM)],
        out_specs=(pl.BlockSpec(memory_space=pltpu.VMEM), pl.BlockSpec(memory_space=pltpu.VMEM)),
        scratch_shapes=[pltpu.SemaphoreType.DMA((7,)), pltpu.SemaphoreType.DMA((7,)), pltpu.SemaphoreType.DMA],
        compiler_params=_cparams(),
    )(xs)


def _gather_shards(srcs, name):
    na = len(srcs)

    def body(*refs):
        src_refs, out_refs = refs[:na], refs[na:2 * na]
        send_sems, recv_sems = refs[2 * na:]
        x, y, c = lax.axis_index("x"), lax.axis_index("y"), lax.axis_index("c")
        sibling = (x, y, 1 - c)
        peers = [(1 - x, y), (x, 1 - y), (1 - x, 1 - y)]

        def copy(a, k, src, slot, which, to):
            return pltpu.make_async_remote_copy(
                src_ref=src, dst_ref=out_refs[a].at[slot, which],
                send_sem=send_sems.at[a * 6 + k], recv_sem=recv_sems.at[a * 6 + k],
                device_id=to, device_id_type=MESH)

        mine = 2 * x + y
        first = [copy(a, k, src_refs[a].at[c], mine, c, (px, py, c))
                 for a in range(na) for k, (px, py) in enumerate(peers)]
        for cp in first:
            cp.start()
        passed = []
        for a in range(na):
            for k, (px, py) in enumerate(peers):
                slot = 2 * px + py
                landed = out_refs[a].at[slot, c]
                copy(a, k, landed, slot, c, (px, py, c)).wait_recv()
                fwd = copy(a, 3 + k, landed, slot, c, sibling)
                fwd.start()
                passed.append(fwd)
        for a in range(na):
            for k, (px, py) in enumerate(peers):
                slot = 2 * px + py
                copy(a, 3 + k, out_refs[a].at[slot, 1 - c], slot, 1 - c, sibling).wait_recv()
        for cp in first + passed:
            cp.wait_send()

    anyspace = pl.BlockSpec(memory_space=pl.ANY)
    return pl.pallas_call(
        body, name=name,
        out_shape=tuple(jax.ShapeDtypeStruct((4,) + tuple(a.shape), a.dtype) for a in srcs),
        in_specs=[anyspace] * na, out_specs=(anyspace,) * na,
        scratch_shapes=[pltpu.SemaphoreType.DMA((6 * na,)), pltpu.SemaphoreType.DMA((6 * na,))],
        compiler_params=_cparams(),
    )(*srcs)


def _swap_sibling(srcs, name, other_half=False):
    na = len(srcs)

    def body(*refs):
        src_refs, out_refs = refs[:na], refs[na:2 * na]
        send_sems, recv_sems = refs[2 * na:]
        x, y, c = lax.axis_index("x"), lax.axis_index("y"), lax.axis_index("c")
        copies = [pltpu.make_async_remote_copy(
            src_ref=src_refs[a].at[:, 1 - c] if other_half else src_refs[a],
            dst_ref=out_refs[a], send_sem=send_sems.at[a], recv_sem=recv_sems.at[a],
            device_id=(x, y, 1 - c), device_id_type=MESH) for a in range(na)]
        for cp in copies:
            cp.start()
        for cp in copies:
            cp.wait()

    def out_of(a):
        shape = (a.shape[0],) + tuple(a.shape[2:]) if other_half else a.shape
        return jax.ShapeDtypeStruct(shape, a.dtype)

    anyspace = pl.BlockSpec(memory_space=pl.ANY)
    return pl.pallas_call(
        body, name=name, out_shape=tuple(out_of(a) for a in srcs),
        in_specs=[anyspace] * na, out_specs=(anyspace,) * na,
        scratch_shapes=[pltpu.SemaphoreType.DMA((na,)), pltpu.SemaphoreType.DMA((na,))],
        compiler_params=_cparams(),
    )(*srcs)


def _mod_cols(c_all, w, b):
    n = w.shape[1]

    def body(c_ref, w_ref, b_ref, o_ref):
        o_ref[...] = _dot(c_ref[...], w_ref[...], HI) + b_ref[...]

    return pl.pallas_call(
        body, name="mod_cols", out_shape=jax.ShapeDtypeStruct((8, n), F32),
        compiler_params=_cparams())(c_all, w, b)


def _grad_w_ada(c_all, dmod_cols):
    n = dmod_cols.shape[1]

    def body(c_ref, d_ref, o_ref):
        o_ref[...] = _dot_tn(c_ref[...], d_ref[...], HI)

    return pl.pallas_call(
        body, name="grad_w_ada", out_shape=jax.ShapeDtypeStruct((D_MODEL, n), F32),
        compiler_params=_cparams())(c_all, dmod_cols)


def _prenorm_proj(x, gs, shift, wqkv_t, wrest_t):
    s = x.shape[0]
    tm = min(TM_PROJ, s)
    nq, nr = wqkv_t.shape[0], wrest_t.shape[0]

    def body(x_ref, gs_ref, sh_ref, wq_ref, wr_ref, h_ref, qkv_ref, rest_ref):
        xv = x_ref[...]
        r = lax.rsqrt(jnp.mean(xv * xv, axis=-1, keepdims=True) + EPS)
        h = (xv * r * gs_ref[...] + sh_ref[...]).astype(BF16)
        h_ref[...] = h
        qkv_ref[...] = _dot_nt(h, wq_ref[...]).astype(BF16)
        rest_ref[...] = _dot_nt(h, wr_ref[...])

    def rows(width):
        return pl.BlockSpec((tm, width), lambda i: (i, 0))

    return pl.pallas_call(
        body, name="prenorm_proj", grid=(s // tm,),
        in_specs=[rows(D_MODEL), _const((1, D_MODEL)), _const((1, D_MODEL)), _const((nq, D_MODEL)),
                  _const((nr, D_MODEL))],
        out_specs=(rows(D_MODEL), rows(nq), rows(nr)),
        out_shape=(jax.ShapeDtypeStruct((s, D_MODEL), BF16), jax.ShapeDtypeStruct((s, nq), BF16),
                   jax.ShapeDtypeStruct((s, nr), F32)),
        compiler_params=_cparams(("parallel",)))(x, gs, shift, wqkv_t, wrest_t)


def _grad_w_rows(h, ds):
    s = h.shape[0]
    tk = min(TK_ACC, s)
    nd = len(ds)
    widths = [d.shape[1] for d in ds]

    def body(*refs):
        h_ref, d_refs = refs[0], refs[1:1 + nd]
        out_refs, accs = refs[1 + nd:1 + 2 * nd], refs[1 + 2 * nd:]
        step = pl.program_id(0)

        @pl.when(step == 0)
        def _():
            for acc in accs:
                acc[...] = jnp.zeros_like(acc)

        hv = h_ref[...]
        for d_ref, acc in zip(d_refs, accs):
            acc[...] += _dot_tn(d_ref[...], hv)

        @pl.when(step == s // tk - 1)
        def _():
            for acc, out in zip(accs, out_refs):
                pltpu.sync_copy(acc, out)

    anyspace = pl.BlockSpec(memory_space=pl.ANY)
    return pl.pallas_call(
        body, name="grad_w_in", grid=(s // tk,),
        in_specs=[pl.BlockSpec((tk, D_MODEL), lambda k: (k, 0))]
                 + [pl.BlockSpec((tk, w), lambda k: (k, 0)) for w in widths],
        out_specs=(anyspace,) * nd,
        out_shape=tuple(jax.ShapeDtypeStruct((w, D_MODEL), F32) for w in widths),
        scratch_shapes=[pltpu.VMEM((w, D_MODEL), F32) for w in widths],
        compiler_params=_cparams(("arbitrary",)))(h, *ds)


def _head_pair_selector():
    rows = jnp.arange(LANES)[:, None]
    cols = jnp.arange(4 * LANES)[None, :]
    return ((rows < HEADS) & (cols == (rows // 2) * LANES + rows % 2)).astype(F32)


def _fcum(rest, bf128, selp):
    s = rest.shape[0]
    tb = min(TB_CUM, s)

    def body(fz_ref, bf_ref, sel_ref, fpc_ref, ft_ref, carry_ref):
        @pl.when(pl.program_id(0) == 0)
        def _():
            carry_ref[...] = jnp.zeros_like(carry_ref)

        z = fz_ref[...] + bf_ref[...]
        logf = jnp.minimum(z, 0.0) - jnp.log(1.0 + jnp.exp(-jnp.abs(z)))
        r = lax.broadcasted_iota(jnp.int32, (tb, tb), 0)
        c = lax.broadcasted_iota(jnp.int32, (tb, tb), 1)
        tri = (c <= r).astype(F32)
        f = _dot_sel(tri, logf) + carry_ref[0:1, :]
        carry_ref[0:1, :] = f[tb - 1:tb, :]
        fpc_ref[...] = _dot_by_sel(f, sel_ref[...])
        ft_ref[...] = jnp.transpose(f)[0:HEADS, :]

    return pl.pallas_call(
        body, name="forget_cumsum", grid=(s // tb,),
        in_specs=[pl.BlockSpec((tb, LANES), lambda i: (i, R_F // LANES)), _const((1, LANES)), _const((LANES, 4 * LANES))],
        out_specs=(pl.BlockSpec((tb, 4 * LANES), lambda i: (i, 0)), pl.BlockSpec((HEADS, tb), lambda i: (0, i))),
        out_shape=(jax.ShapeDtypeStruct((s, 4 * LANES), F32), jax.ShapeDtypeStruct((HEADS, s), F32)),
        scratch_shapes=[pltpu.VMEM((8, LANES), F32)],
        compiler_params=_cparams(("arbitrary",)))(rest, bf128, selp)


def _dfcum(dfk, dfq, rest, bf128, selq):
    s = rest.shape[0]
    tb = min(TB_CUM, s)
    nb = s // tb

    def body(dk_ref, dq_ref, fz_ref, bf_ref, sel_ref, df_ref, dbf_ref, carry_ref):
        @pl.when(pl.program_id(0) == 0)
        def _():
            carry_ref[...] = jnp.zeros_like(carry_ref)
            dbf_ref[...] = jnp.zeros_like(dbf_ref)

        d = _dot_by_sel(dk_ref[...] + dq_ref[...], sel_ref[...])
        r = lax.broadcasted_iota(jnp.int32, (tb, tb), 0)
        c = lax.broadcasted_iota(jnp.int32, (tb, tb), 1)
        triu = (c >= r).astype(F32)
        dlogf = _dot_sel(triu, d) + carry_ref[0:1, :]
        carry_ref[0:1, :] = dlogf[0:1, :]
        z = fz_ref[...] + bf_ref[...]
        df = dlogf * (1.0 / (1.0 + jnp.exp(z)))
        df_ref[...] = df.astype(BF16)
        dbf_ref[0:1, :] += jnp.sum(df, axis=0, keepdims=True)

    return pl.pallas_call(
        body, name="forget_grad", grid=(nb,),
        in_specs=[pl.BlockSpec((tb, 4 * LANES), lambda i: (nb - 1 - i, 0)),
                  pl.BlockSpec((tb, 4 * LANES), lambda i: (nb - 1 - i, 0)),
                  pl.BlockSpec((tb, LANES), lambda i: (nb - 1 - i, R_F // LANES)),
                  _const((1, LANES)), _const((4 * LANES, LANES))],
        out_specs=(pl.BlockSpec((tb, LANES), lambda i: (nb - 1 - i, 0)), pl.BlockSpec((8, LANES), lambda i: (0, 0))),
        out_shape=(jax.ShapeDtypeStruct((s, LANES), BF16), jax.ShapeDtypeStruct((8, LANES), F32)),
        scratch_shapes=[pltpu.VMEM((8, LANES), F32)],
        compiler_params=_cparams(("arbitrary",)))(dfk, dfq, rest, bf128, selq)


def _scaled(q):
    return (q.astype(F32) * (HEAD_DIM ** -0.5)).astype(BF16)


def _attn_fwd(qkv, frow5, fpc):
    s = qkv.shape[0]
    t = min(T_ATT, s)
    n = s // t
    ch = min(ATT_CHUNK, t)
    wide = 2 * LANES
    pairs = ATT_PAIRS
    width = pairs * LANES
    groups = 4 // pairs

    def body(q_ref, k_ref, v_ref, fr_ref, fc_ref, o_ref, lse_ref, s_scr, p_scr, m_scr, a_scr, fq_scr, acc_scr):
        i = pl.program_id(1)
        lane = lax.broadcasted_iota(jnp.int32, (t, LANES), 1)
        first = lane < HEAD_DIM
        ones_col = ((lane == 0).astype(BF16), (lane == 1).astype(BF16))
        m_scr[...] = jnp.full(m_scr.shape, NEG, F32)
        acc_scr[...] = jnp.zeros_like(acc_scr)
        qm = []
        for pp in range(pairs):
            q = _scaled(q_ref[:, pp * LANES:(pp + 1) * LANES])
            zq = jnp.zeros_like(q)
            qm += [jnp.where(first, q, zq), jnp.where(first, zq, q)]
            fq_scr[2 * pp] = fc_ref[:, pp * LANES:pp * LANES + 1]
            fq_scr[2 * pp + 1] = fc_ref[:, pp * LANES + 1:pp * LANES + 2]

        def step(j, masked):
            r0 = pl.multiple_of(j * t, t)
            vaug = []
            for pp in range(pairs):
                kb = k_ref[pl.ds(r0, t), pp * LANES:(pp + 1) * LANES]
                vb = v_ref[pl.ds(r0, t), pp * LANES:(pp + 1) * LANES]
                zv = jnp.zeros_like(vb)
                vaug += [jnp.concatenate([jnp.where(first, vb, zv), ones_col[0]], axis=1),
                         jnp.concatenate([jnp.where(first, zv, vb), ones_col[1]], axis=1)]
                for hh in range(2):
                    s_scr[2 * pp + hh] = _dot_nt(qm[2 * pp + hh], kb)
            pv = []
            for hd in range(2 * pairs):
                fk = fr_ref[hd // 2, hd % 2, j]
                for c in range(t // ch):
                    rows = pl.ds(c * ch, ch)
                    hi = min(t, (c * ch // LANES + 1) * LANES) if masked else t
                    sc = s_scr[hd, rows, 0:hi] - fk[:, 0:hi]
                    if masked:
                        rq = c * ch + lax.broadcasted_iota(jnp.int32, (ch, hi), 0)
                        ck = lax.broadcasted_iota(jnp.int32, (ch, hi), 1)
                        sc = jnp.where(ck <= rq, sc, NEG)
                    fq = fq_scr[hd, rows, :]
                    m_old = m_scr[hd, rows, :]
                    m_new = jnp.maximum(m_old, fq + jnp.max(sc, axis=1, keepdims=True))
                    p_scr[hd, rows, 0:hi] = jnp.exp(sc + (fq - m_new)).astype(BF16)
                    if hi < t:
                        p_scr[hd, rows, hi:t] = jnp.zeros((ch, t - hi), BF16)
                    a_scr[hd, rows, :] = jnp.exp(m_old - m_new)
                    m_scr[hd, rows, :] = m_new
                pv.append(_dot(p_scr[hd], vaug[hd]))
            for pp in range(pairs):
                a0, a1 = a_scr[2 * pp], a_scr[2 * pp + 1]
                alpha = jnp.concatenate([jnp.where(first, a0, a1), jnp.where(lane == 0, a0, a1)], axis=1)
                acc_scr[pp] = acc_scr[pp] * alpha + pv[2 * pp] + pv[2 * pp + 1]
            return 0

        lax.fori_loop(0, i, lambda j, _: step(j, False), 0)
        step(i, True)
        lse = jnp.zeros((t, LANES), F32)
        for pp in range(pairs):
            l0 = acc_scr[pp, :, LANES:LANES + 1]
            l1 = acc_scr[pp, :, LANES + 1:LANES + 2]
            o_ref[:, pp * LANES:(pp + 1) * LANES] = acc_scr[pp, :, 0:LANES] * jnp.where(first, 1.0 / l0, 1.0 / l1)
            lse = jnp.where(lane == 2 * pp, m_scr[2 * pp] + jnp.log(l0), lse)
            lse = jnp.where(lane == 2 * pp + 1, m_scr[2 * pp + 1] + jnp.log(l1), lse)
        lse_ref[...] = lse

    blk = pl.BlockSpec((t, width), lambda g, i: (i, g))
    return pl.pallas_call(
        body, name="attn_fwd", grid=(groups, n),
        in_specs=[blk,
                  pl.BlockSpec((s, width), lambda g, i: (0, groups + g)),
                  pl.BlockSpec((s, width), lambda g, i: (0, 2 * groups + g)),
                  pl.BlockSpec((pairs, 2, n, 1, t), lambda g, i: (g, 0, 0, 0, 0)),
                  blk],
        out_specs=(blk, pl.BlockSpec((t, LANES), lambda g, i: (i, g))),
        out_shape=(jax.ShapeDtypeStruct((s, FOX_W), F32), jax.ShapeDtypeStruct((s, groups * LANES), F32)),
        scratch_shapes=[pltpu.VMEM((2 * pairs, t, t), F32), pltpu.VMEM((2 * pairs, t, t), BF16),
                        pltpu.VMEM((2 * pairs, t, 1), F32), pltpu.VMEM((2 * pairs, t, 1), F32),
                        pltpu.VMEM((2 * pairs, t, 1), F32), pltpu.VMEM((pairs, t, wide), F32)],
        compiler_params=_cparams(("parallel", "arbitrary")))(qkv, qkv, qkv, frow5, fpc)


def _attn_bwd(qkv, do, lse5, dlt5, frow5, fpc):
    s = qkv.shape[0]
    t = min(T_ATT, s)
    n = s // t
    wide = 2 * LANES

    ch = min(ATT_CHUNK, t)

    def body(q_ref, do_ref, k_ref, v_ref, lse_ref, dl_ref, fr_ref, fc_ref,
             dq_ref, dk_ref, dv_ref, dfk_ref, dfq_ref, dq_acc, st_scr, dp_scr, pt_scr, ds_scr, dk_acc, dv_acc, fk_scr):
        j = pl.program_id(1)

        @pl.when(j == 0)
        def _():
            dq_acc[...] = jnp.zeros_like(dq_acc)

        dk_acc[...] = jnp.zeros_like(dk_acc)
        dv_acc[...] = jnp.zeros_like(dv_acc)
        lane = lax.broadcasted_iota(jnp.int32, (t, LANES), 1)
        first = lane < HEAD_DIM
        ones_col = ((lane == 0).astype(BF16), (lane == 1).astype(BF16))
        kb = k_ref[...]
        vb = v_ref[...]
        zk = jnp.zeros_like(kb)
        kaug = (jnp.concatenate([jnp.where(first, kb, zk), ones_col[0]], axis=1),
                jnp.concatenate([jnp.where(first, zk, kb), ones_col[1]], axis=1))
        fk_scr[0] = fc_ref[:, 0:1]
        fk_scr[1] = fc_ref[:, 1:2]

        def step(blocks, masked):
            chains = []
            for bi, i in enumerate(blocks):
                r0 = pl.multiple_of(i * t, t)
                qb = _scaled(q_ref[pl.ds(r0, t), :])
                dob = do_ref[pl.ds(r0, t), :]
                zq = jnp.zeros_like(qb)
                qm = (jnp.where(first, qb, zq), jnp.where(first, zq, qb))
                dom = (jnp.where(first, dob, zq), jnp.where(first, zq, dob))
                for hh in range(2):
                    st_scr[2 * bi + hh] = _dot_nt(kb, qm[hh])
                    dp_scr[2 * bi + hh] = _dot_nt(vb, dom[hh])
                    chains.append((i, hh, qm[hh], dom[hh]))
            dq_add = [jnp.zeros((t, wide), F32) for _ in blocks]
            for cn, (i, hh, qmh, domh) in enumerate(chains):
                bias = fr_ref[0, hh, i] - lse_ref[0, hh, i]
                dl = dl_ref[0, hh, i]
                for c in range(t // ch):
                    rows = pl.ds(c * ch, ch)
                    lo = c * ch // LANES * LANES if masked else 0
                    st = st_scr[cn, rows, lo:t] + (bias[:, lo:t] - fk_scr[hh, rows, :])
                    if masked:
                        rk = c * ch + lax.broadcasted_iota(jnp.int32, (ch, t - lo), 0)
                        cq = lo + lax.broadcasted_iota(jnp.int32, (ch, t - lo), 1)
                        st = jnp.where(rk <= cq, st, NEG)
                    pt = jnp.exp(st)
                    pt_scr[cn, rows, lo:t] = pt.astype(BF16)
                    ds_scr[cn, rows, lo:t] = (pt * (dp_scr[cn, rows, lo:t] - dl[:, lo:t])).astype(BF16)
                    if lo > 0:
                        pt_scr[cn, rows, 0:lo] = jnp.zeros((ch, lo), BF16)
                        ds_scr[cn, rows, 0:lo] = jnp.zeros((ch, lo), BF16)
                dsb = ds_scr[cn]
                dv_acc[...] += _dot(pt_scr[cn], domh)
                dk_acc[...] += _dot(dsb, jnp.concatenate([qmh, ones_col[hh]], axis=1))
                dq_add[cn // 2] = dq_add[cn // 2] + _dot_tn(dsb, kaug[hh])
            for bi, i in enumerate(blocks):
                dq_acc[pl.ds(pl.multiple_of(i * t, t), t), :] += dq_add[bi]
            return 0

        step([j], True)
        odd = (n - 1 - j) % 2
        lax.fori_loop(0, odd, lambda _, carry: step([j + 1], False), 0)
        first_pair = j + 1 + odd
        lax.fori_loop(0, (n - first_pair) // 2,
                      lambda p, _: step([first_pair + 2 * p, first_pair + 2 * p + 1], False), 0)
        dk_ref[...] = dk_acc[:, 0:LANES].astype(BF16)
        dv_ref[...] = dv_acc[...].astype(BF16)
        dfk_ref[...] = -dk_acc[:, LANES:wide]

        @pl.when(j == n - 1)
        def _():
            dq_ref[...] = (dq_acc[:, 0:LANES] * (HEAD_DIM ** -0.5)).astype(BF16)
            dfq_ref[...] = dq_acc[:, LANES:wide]

    stat = pl.BlockSpec((1, 2, n, 1, t), lambda h, j: (h, 0, 0, 0, 0))
    blk = pl.BlockSpec((t, LANES), lambda h, j: (j, h))
    full = pl.BlockSpec((s, LANES), lambda h, j: (0, h))
    return pl.pallas_call(
        body, name="attn_bwd", grid=(4, n),
        in_specs=[full, full,
                  pl.BlockSpec((t, LANES), lambda h, j: (j, 4 + h)),
                  pl.BlockSpec((t, LANES), lambda h, j: (j, 8 + h)),
                  stat, stat, stat, blk],
        out_specs=(full, blk, blk, blk, full),
        out_shape=(jax.ShapeDtypeStruct((s, FOX_W), BF16), jax.ShapeDtypeStruct((s, FOX_W), BF16),
                   jax.ShapeDtypeStruct((s, FOX_W), BF16), jax.ShapeDtypeStruct((s, 4 * LANES), F32),
                   jax.ShapeDtypeStruct((s, 4 * LANES), F32)),
        scratch_shapes=[pltpu.VMEM((s, wide), F32), pltpu.VMEM((4, t, t), F32), pltpu.VMEM((4, t, t), F32),
                        pltpu.VMEM((4, t, t), BF16), pltpu.VMEM((4, t, t), BF16), pltpu.VMEM((t, wide), F32),
                        pltpu.VMEM((t, LANES), F32), pltpu.VMEM((2, t, 1), F32)],
        compiler_params=_cparams(("parallel", "arbitrary")))(qkv, do, qkv, qkv, lse5, dlt5, frow5, fpc)


def _ssm_block_params(a_re, a_im, log_dt, b_re, b_im):
    dt = jnp.exp(log_dt)[:, None]
    mag = jnp.exp(a_re * dt)
    ar = mag * jnp.cos(a_im * dt)
    ai = mag * jnp.sin(a_im * dt)
    den = a_re * a_re + a_im * a_im
    nr = ar - 1.0
    cr = (nr * a_re + ai * a_im) / den
    ci = (ai * a_re - nr * a_im) / den
    bbr = cr[:, :, None] * b_re - ci[:, :, None] * b_im
    bbi = cr[:, :, None] * b_im + ci[:, :, None] * b_re
    return ar, ai, bbr, bbi


def _block_diag(blocks):
    g, r, c = blocks.shape
    eye = jnp.eye(g, dtype=blocks.dtype)
    return (blocks[:, :, None, :] * eye[:, None, :, None]).reshape(g * r, g * c)


def _scan_consts(a_re, a_im, log_dt, reverse):
    dt = jnp.exp(log_dt)[:, None]
    lr = (a_re * dt).reshape(1, NSTATE)
    li = (a_im * dt).reshape(1, NSTATE)
    if reverse:
        li = -li
    rows = jnp.arange(8, dtype=F32)[:, None]

    def power(k):
        mag = jnp.exp(k * lr)
        return mag * jnp.cos(k * li), mag * jnp.sin(k * li)

    tiles = []
    for k in (1, 2, 4):
        keep = (rows < 8 - k) if reverse else (rows >= k)
        pr, pi_ = power(float(k))
        tiles += [jnp.where(keep, pr, 0.0), jnp.where(keep, pi_, 0.0)]
    expo = (8.0 - rows) if reverse else (rows + 1.0)
    tiles += list(power(expo))
    return jnp.stack([jnp.broadcast_to(tl, (8, NSTATE)) for tl in tiles])


_SCAN_W = 512
_HALF_W = S5_W // 2
_HALF_S = NSTATE // 2


def _compact_diag(blocks_re, blocks_im):
    hg = GROUPS // 2
    return jnp.concatenate([_block_diag(b[h * hg:(h + 1) * hg]) for b in (blocks_re, blocks_im) for h in range(2)],
                           axis=1)


def _half_expand(v, w_ref, out_ref):
    for half in range(2):
        vh = v[:, half * _HALF_W:(half + 1) * _HALF_W]
        for part in range(2):
            c0 = part * NSTATE + half * _HALF_S
            out_ref[:, c0:c0 + _HALF_S] = _dot(vh, w_ref[:, c0:c0 + _HALF_S])


def _half_contract(x_ref, w_ref, half):
    out = None
    for part in range(2):
        r0 = part * NSTATE + half * _HALF_S
        term = _dot_nt(x_ref[:, r0:r0 + _HALF_S].astype(BF16), w_ref[:, r0:r0 + _HALF_S])
        out = term if out is None else out + term
    return out


def _half_outer(v, x_ref, acc_ref):
    for half in range(2):
        vh = v[:, half * _HALF_W:(half + 1) * _HALF_W]
        for part in range(2):
            c0 = part * NSTATE + half * _HALF_S
            acc_ref[:, c0:c0 + _HALF_S] += _dot_tn(vh, x_ref[:, c0:c0 + _HALF_S].astype(BF16))


def _ssm_fwd(rest, bd, cd, consts):
    s = rest.shape[0]
    tb = min(TB_SSM, s)
    ns2 = 2 * NSTATE

    def body(u_ref, bd_ref, cd_ref, cf_ref, y_ref, x_ref, cb_ref):
        @pl.when(pl.program_id(0) == 0)
        def _():
            cb_ref[...] = jnp.zeros_like(cb_ref)

        _half_expand(u_ref[...].astype(BF16), bd_ref, x_ref)

        def tile(ti, _):
            r0 = pl.multiple_of(ti * 8, 8)
            for cc in range(NSTATE // _SCAN_W):
                cr = pl.ds(cc * _SCAN_W, _SCAN_W)
                ci = pl.ds(NSTATE + cc * _SCAN_W, _SCAN_W)
                re = x_ref[pl.ds(r0, 8), cr]
                im = x_ref[pl.ds(r0, 8), ci]
                for n_, k in enumerate((1, 2, 4)):
                    ar = cf_ref[2 * n_, :, cr]
                    ai = cf_ref[2 * n_ + 1, :, cr]
                    sr = pltpu.roll(re, k, 0)
                    si = pltpu.roll(im, k, 0)
                    re, im = re + ar * sr - ai * si, im + ar * si + ai * sr
                pr = cf_ref[6, :, cr]
                pi_ = cf_ref[7, :, cr]
                cbr = cb_ref[:, cr]
                cbi = cb_ref[:, ci]
                re, im = re + pr * cbr - pi_ * cbi, im + pr * cbi + pi_ * cbr
                x_ref[pl.ds(r0, 8), cr] = re
                x_ref[pl.ds(r0, 8), ci] = im
                cb_ref[:, cr] = jnp.broadcast_to(re[7:8, :], (8, _SCAN_W))
                cb_ref[:, ci] = jnp.broadcast_to(im[7:8, :], (8, _SCAN_W))
            return 0

        lax.fori_loop(0, tb // 8, tile, 0)
        for half in range(2):
            y_ref[:, half * _HALF_W:(half + 1) * _HALF_W] = _half_contract(x_ref, cd_ref, half)

    return pl.pallas_call(
        body, name="ssm_fwd", grid=(s // tb,),
        in_specs=[pl.BlockSpec((tb, S5_W), lambda i: (i, R_U // S5_W)), _const((_HALF_W, ns2)), _const((_HALF_W, ns2)),
                  _const((8, 8, NSTATE))],
        out_specs=(pl.BlockSpec((tb, S5_W), lambda i: (i, 0)), pl.BlockSpec((tb, ns2), lambda i: (i, 0))),
        out_shape=(jax.ShapeDtypeStruct((s, S5_W), F32), jax.ShapeDtypeStruct((s, ns2), F32)),
        scratch_shapes=[pltpu.VMEM((8, ns2), F32)],
        compiler_params=_cparams(("arbitrary",)))(rest, bd, cd, consts)


def _ssm_bwd(dys, xs, rest, bd, cd, consts, dskip):
    s = dys.shape[0]
    tb = min(TB_SSM, s)
    nb = s // tb
    ns2 = 2 * NSTATE
    nt = tb // 8

    def body(dy_ref, x_ref, u_ref, bd_ref, cd_ref, cf_ref, dsk_ref, du_ref, gb_ref, gc_ref, da_ref,
             g_ref, cb_ref, acc_b, acc_c):
        step = pl.program_id(0)

        @pl.when(step == 0)
        def _():
            cb_ref[...] = jnp.zeros_like(cb_ref)
            acc_b[...] = jnp.zeros_like(acc_b)
            acc_c[...] = jnp.zeros_like(acc_c)
            da_ref[...] = jnp.zeros_like(da_ref)

        dy = dy_ref[...]
        dyb = dy.astype(BF16)
        _half_expand(dyb, cd_ref, g_ref)
        last_row = lax.broadcasted_iota(jnp.int32, (8, _SCAN_W), 0) == 7

        def tile(tt, _):
            r0 = pl.multiple_of((nt - 1 - tt) * 8, 8)
            for cc in range(NSTATE // _SCAN_W):
                cr = pl.ds(cc * _SCAN_W, _SCAN_W)
                ci = pl.ds(NSTATE + cc * _SCAN_W, _SCAN_W)
                re = g_ref[pl.ds(r0, 8), cr]
                im = g_ref[pl.ds(r0, 8), ci]
                for n_, k in enumerate((1, 2, 4)):
                    ar = cf_ref[2 * n_, :, cr]
                    ai = cf_ref[2 * n_ + 1, :, cr]
                    sr = pltpu.roll(re, 8 - k, 0)
                    si = pltpu.roll(im, 8 - k, 0)
                    re, im = re + ar * sr - ai * si, im + ar * si + ai * sr
                pr = cf_ref[6, :, cr]
                pi_ = cf_ref[7, :, cr]
                cbr = cb_ref[:, cr]
                cbi = cb_ref[:, ci]
                re, im = re + pr * cbr - pi_ * cbi, im + pr * cbi + pi_ * cbr
                g_ref[pl.ds(r0, 8), cr] = re
                g_ref[pl.ds(r0, 8), ci] = im
                gnr = jnp.where(last_row, cbr, pltpu.roll(re, 7, 0))
                gni = jnp.where(last_row, cbi, pltpu.roll(im, 7, 0))
                xr = x_ref[pl.ds(r0, 8), cr]
                xi = x_ref[pl.ds(r0, 8), ci]
                da_ref[:, cr] += gnr * xr + gni * xi
                da_ref[:, ci] += gni * xr - gnr * xi
                cb_ref[:, cr] = jnp.broadcast_to(re[0:1, :], (8, _SCAN_W))
                cb_ref[:, ci] = jnp.broadcast_to(im[0:1, :], (8, _SCAN_W))
            return 0

        lax.fori_loop(0, nt, tile, 0)
        for half in range(2):
            cols = slice(half * _HALF_W, (half + 1) * _HALF_W)
            du_ref[:, cols] = (_half_contract(g_ref, bd_ref, half) + dy[:, cols] * dsk_ref[:, cols]).astype(BF16)
        _half_outer(u_ref[...].astype(BF16), g_ref, acc_b)
        _half_outer(dyb, x_ref, acc_c)

        @pl.when(step == nb - 1)
        def _():
            for g in range(GROUPS):
                src = slice((g % (GROUPS // 2)) * GCH, (g % (GROUPS // 2) + 1) * GCH)
                dst = slice(g * GCH, (g + 1) * GCH)
                for part in range(2):
                    cols = slice(part * NSTATE + g * STATE, part * NSTATE + (g + 1) * STATE)
                    gb_ref[dst, part * STATE:(part + 1) * STATE] = acc_b[src, cols]
                    gc_ref[dst, part * STATE:(part + 1) * STATE] = acc_c[src, cols]

    rev = lambda i: (nb - 1 - i, 0)
    small = pl.BlockSpec((S5_W, 2 * STATE), lambda i: (0, 0))
    return pl.pallas_call(
        body, name="ssm_bwd", grid=(nb,),
        in_specs=[pl.BlockSpec((tb, S5_W), rev), pl.BlockSpec((tb, ns2), rev),
                  pl.BlockSpec((tb, S5_W), lambda i: (nb - 1 - i, R_U // S5_W)),
                  _const((_HALF_W, ns2)), _const((_HALF_W, ns2)), _const((8, 8, NSTATE)), _const((1, S5_W))],
        out_specs=(pl.BlockSpec((tb, S5_W), rev), small, small, pl.BlockSpec((8, ns2), lambda i: (0, 0))),
        out_shape=(jax.ShapeDtypeStruct((s, S5_W), BF16), jax.ShapeDtypeStruct((S5_W, 2 * STATE), F32),
                   jax.ShapeDtypeStruct((S5_W, 2 * STATE), F32), jax.ShapeDtypeStruct((8, ns2), F32)),
        scratch_shapes=[pltpu.VMEM((tb, ns2), F32), pltpu.VMEM((8, ns2), F32),
                        pltpu.VMEM((_HALF_W, ns2), F32), pltpu.VMEM((_HALF_W, ns2), F32)],
        compiler_params=_cparams(("arbitrary",)))(dys, xs, rest, bd, cd, consts, dskip)


_GELU_C = math.sqrt(2.0 / math.pi)
_GELU_A = 0.044715


def _mid(o, rest, ys0, x, tgt, w, vec, hsel):
    s = o.shape[0]
    tm = min(TM, s)
    nsteps = s // tm
    half = FOX_W

    def body(o_ref, ga_ref, gb_ref, za_ref, u_ref, zb_ref, ys0_ref, x_ref, t_ref,
             wglu_ref, wua_ref, wub_ref, wout_ref, vec_ref, hsel_ref,
             dx2_ref, dga_ref, dgb_ref, do_ref, dza_ref, dzb_ref, dys_ref, dlt_ref,
             gout_hbm, gua_hbm, gub_hbm, gglu_hbm, vout_ref,
             a_out, a_ua, a_ub, a_glu):
        step = pl.program_id(0)

        @pl.when(step == 0)
        def _():
            a_out[...] = jnp.zeros_like(a_out)
            a_ua[...] = jnp.zeros_like(a_ua)
            a_ub[...] = jnp.zeros_like(a_ub)
            a_glu[...] = jnp.zeros_like(a_glu)
            vout_ref[...] = jnp.zeros_like(vout_ref)

        gate = vec_ref[0:1, :]
        gfin = vec_ref[1:2, :]
        dsk = vec_ref[2:3, 0:half]
        bglu = vec_ref[2:3, half:2 * half]

        o_v = o_ref[...]
        za = za_ref[...]
        sza = _sigmoid(za)
        silu_za = za * sza
        ya_b = (o_v * silu_za).astype(BF16)
        u_v = u_ref[...]
        ys = ys0_ref[...] + dsk * u_v
        inner = _GELU_C * (ys + _GELU_A * ys * ys * ys)
        th = jnp.tanh(inner)
        yg = 0.5 * ys * (1.0 + th)
        yg_b = yg.astype(BF16)
        st = _sigmoid(_dot(yg_b, wglu_ref[...]) + bglu)
        yb1 = yg * st
        zb = zb_ref[...]
        szb = _sigmoid(zb)
        silu_zb = zb * szb
        yb_b = (yb1 * silu_zb).astype(BF16)
        ua = _dot(ya_b, wua_ref[...])
        ub = _dot(yb_b, wub_ref[...])
        sga = _sigmoid(ga_ref[...])
        sgb = _sigmoid(gb_ref[...])
        merged_b = (sga * ua + sgb * ub).astype(BF16)
        mo = _dot(merged_b, wout_ref[...])
        x2 = x_ref[...] + gate * mo
        r2 = lax.rsqrt(jnp.mean(x2 * x2, axis=-1, keepdims=True) + EPS)
        x2n = x2 * r2
        diff = x2n * gfin - t_ref[...]
        loss = 0.5 * jnp.sum(jnp.mean(diff * diff, axis=-1, keepdims=True), axis=0, keepdims=True)
        dy = diff * (1.0 / D_MODEL)
        dx2n = dy * gfin
        dx2 = r2 * (dx2n - x2n * jnp.mean(dx2n * x2n, axis=-1, keepdims=True))
        dx2_ref[...] = dx2
        vout_ref[0:1, :] += jnp.sum(dy * x2n, axis=0, keepdims=True)
        vout_ref[1:2, :] += jnp.sum(dx2 * mo, axis=0, keepdims=True)
        vout_ref[3:4, :] += jnp.broadcast_to(loss, (1, D_MODEL))
        dmo_b = (dx2 * gate).astype(BF16)
        dmerged = _dot_nt(dmo_b, wout_ref[...])
        a_out[...] += _dot_tn(merged_b, dmo_b)
        dua_b = (dmerged * sga).astype(BF16)
        dub_b = (dmerged * sgb).astype(BF16)
        dga_ref[...] = (dmerged * ua * sga * (1.0 - sga)).astype(BF16)
        dgb_ref[...] = (dmerged * ub * sgb * (1.0 - sgb)).astype(BF16)
        dya = _dot_nt(dua_b, wua_ref[...])
        dyb = _dot_nt(dub_b, wub_ref[...])
        a_ua[...] += _dot_tn(ya_b, dua_b)
        a_ub[...] += _dot_tn(yb_b, dub_b)
        do_b = (dya * silu_za).astype(BF16)
        do_ref[...] = do_b
        dza_ref[...] = (dya * o_v * (sza * (1.0 + za * (1.0 - sza)))).astype(BF16)
        hsel = hsel_ref[...].astype(BF16)
        dlt_ref[...] = sum(_dot_nt(hsel, part) for part in _split3(do_b.astype(F32) * o_v))
        dyb1 = dyb * silu_zb
        dzb_ref[...] = (dyb * yb1 * (szb * (1.0 + zb * (1.0 - szb)))).astype(BF16)
        dt = dyb1 * yg * st * (1.0 - st)
        dt_b = dt.astype(BF16)
        dyg = dyb1 * st + _dot_nt(dt_b, wglu_ref[...])
        a_glu[...] += _dot_tn(yg_b, dt_b)
        dgelu = 0.5 * (1.0 + th) + 0.5 * ys * (1.0 - th * th) * _GELU_C * (1.0 + 3.0 * _GELU_A * ys * ys)
        dys = dyg * dgelu
        dys_ref[...] = dys
        vout_ref[2:3, 0:half] += jnp.sum(dys * u_v, axis=0, keepdims=True)
        vout_ref[2:3, half:2 * half] += jnp.sum(dt, axis=0, keepdims=True)

        @pl.when(step == nsteps - 1)
        def _():
            pltpu.sync_copy(a_out, gout_hbm)
            pltpu.sync_copy(a_ua, gua_hbm)
            pltpu.sync_copy(a_ub, gub_hbm)
            pltpu.sync_copy(a_glu, gglu_hbm)

    def rows(width, col=0):
        return pl.BlockSpec((tm, width), lambda i, col=col: (i, col))

    anyspace = pl.BlockSpec(memory_space=pl.ANY)
    wshapes = [(S5_W, S5_W), (FOX_W, D_MODEL), (S5_W, D_MODEL), (D_MODEL, D_MODEL)]
    return pl.pallas_call(
        body, name="mid", grid=(nsteps,),
        in_specs=[rows(FOX_W), rows(D_MODEL, R_GA // D_MODEL), rows(D_MODEL, R_GB // D_MODEL),
                  rows(FOX_W, R_ZA // FOX_W), rows(S5_W, R_U // S5_W), rows(S5_W, R_ZB // S5_W),
                  rows(S5_W), rows(D_MODEL), rows(D_MODEL)]
                 + [_const(sh) for sh in wshapes]
                 + [_const((8, D_MODEL)), _const((HEADS, FOX_W))],
        out_specs=(rows(D_MODEL), rows(D_MODEL), rows(D_MODEL), rows(FOX_W), rows(FOX_W), rows(S5_W), rows(S5_W),
                   pl.BlockSpec((HEADS, tm), lambda i: (0, i)),
                   anyspace, anyspace, anyspace, anyspace, pl.BlockSpec((8, D_MODEL), lambda i: (0, 0))),
        out_shape=(jax.ShapeDtypeStruct((s, D_MODEL), F32), jax.ShapeDtypeStruct((s, D_MODEL), BF16),
                   jax.ShapeDtypeStruct((s, D_MODEL), BF16), jax.ShapeDtypeStruct((s, FOX_W), BF16),
                   jax.ShapeDtypeStruct((s, FOX_W), BF16), jax.ShapeDtypeStruct((s, S5_W), BF16),
                   jax.ShapeDtypeStruct((s, S5_W), F32), jax.ShapeDtypeStruct((HEADS, s), F32),
                   jax.ShapeDtypeStruct((D_MODEL, D_MODEL), F32), jax.ShapeDtypeStruct((FOX_W, D_MODEL), F32),
                   jax.ShapeDtypeStruct((S5_W, D_MODEL), F32), jax.ShapeDtypeStruct((S5_W, S5_W), F32),
                   jax.ShapeDtypeStruct((8, D_MODEL), F32)),
        scratch_shapes=[pltpu.VMEM((D_MODEL, D_MODEL), F32), pltpu.VMEM((FOX_W, D_MODEL), F32),
                        pltpu.VMEM((S5_W, D_MODEL), F32), pltpu.VMEM((S5_W, S5_W), F32)],
        compiler_params=_cparams(("arbitrary",)),
    )(o, rest, rest, rest, rest, rest, ys0, x, tgt, *w, vec, hsel)


def _dh(dq, dk, dv, dga, dgb, dza, du, dzb, df, wqkv_t, wrest_t, x, dx2, gs, scatter_srcs):
    s = x.shape[0]
    tm = min(TM_PROJ, s)
    nsteps = s // tm
    na = len(scatter_srcs)

    def body(dq_ref, dk_ref, dv_ref, dga_ref, dgb_ref, dza_ref, du_ref, dzb_ref, df_ref, wq_ref, wr_ref,
             x_ref, dx2_ref, gs_ref, *rest_refs):
        src_refs = rest_refs[:na]
        gx_ref, vout_ref = rest_refs[na:na + 2]
        out_refs = rest_refs[na + 2:2 * na + 2]
        send_sems, recv_sems = rest_refs[2 * na + 2:]
        step = pl.program_id(0)
        cx, cy, cc = lax.axis_index("x"), lax.axis_index("y"), lax.axis_index("c")
        peers = [(1 - cx, cy), (cx, 1 - cy), (1 - cx, 1 - cy)]

        def copy(a, k, px, py, slot):
            return pltpu.make_async_remote_copy(
                src_ref=src_refs[a].at[2 * px + py], dst_ref=out_refs[a].at[slot],
                send_sem=send_sems.at[a * 3 + k], recv_sem=recv_sems.at[a * 3 + k],
                device_id=(px, py, cc), device_id_type=MESH)

        @pl.when(step == 0)
        def _():
            vout_ref[...] = jnp.zeros_like(vout_ref)
            for a in range(na):
                for k, (px, py) in enumerate(peers):
                    copy(a, k, px, py, 2 * cx + cy).start()

        dh = _dot(dq_ref[...], wq_ref[0:512, :])
        dh += _dot(dk_ref[...], wq_ref[512:1024, :])
        dh += _dot(dv_ref[...], wq_ref[1024:1536, :])
        dh += _dot(dga_ref[...], wr_ref[R_GA:R_GB, :])
        dh += _dot(dgb_ref[...], wr_ref[R_GB:R_ZA, :])
        dh += _dot(dza_ref[...], wr_ref[R_ZA:R_U, :])
        dh += _dot(du_ref[...], wr_ref[R_U:R_ZB, :])
        dh += _dot(dzb_ref[...], wr_ref[R_ZB:R_F, :])
        dh += _dot(df_ref[...], wr_ref[R_F:REST_W, :])
        xv = x_ref[...]
        r = lax.rsqrt(jnp.mean(xv * xv, axis=-1, keepdims=True) + EPS)
        xn = xv * r
        dxn = dh * gs_ref[...]
        gx_ref[...] = dx2_ref[...] + r * (dxn - xn * jnp.mean(dxn * xn, axis=-1, keepdims=True))
        vout_ref[0:1, :] += jnp.sum(dh * xn, axis=0, keepdims=True)
        vout_ref[1:2, :] += jnp.sum(dh, axis=0, keepdims=True)

        @pl.when(step == nsteps - 1)
        def _():
            for a in range(na):
                for k, (px, py) in enumerate(peers):
                    copy(a, k, px, py, 2 * px + py).wait_recv()
            for a in range(na):
                for k, (px, py) in enumerate(peers):
                    copy(a, k, px, py, 2 * cx + cy).wait_send()

    def rows(width):
        return pl.BlockSpec((tm, width), lambda i: (i, 0))

    anyspace = pl.BlockSpec(memory_space=pl.ANY)
    return pl.pallas_call(
        body, name="dh", grid=(nsteps,),
        in_specs=[rows(512), rows(512), rows(512), rows(1024), rows(1024), rows(512), rows(512), rows(512), rows(128),
                  _const((1536, D_MODEL)), _const((REST_W, D_MODEL)), rows(D_MODEL), rows(D_MODEL), _const((1, D_MODEL))]
                 + [anyspace] * na,
        out_specs=(rows(D_MODEL), pl.BlockSpec((8, D_MODEL), lambda i: (0, 0))) + (anyspace,) * na,
        out_shape=(jax.ShapeDtypeStruct((s, D_MODEL), F32), jax.ShapeDtypeStruct((8, D_MODEL), F32))
                  + tuple(jax.ShapeDtypeStruct(a.shape, a.dtype) for a in scatter_srcs),
        scratch_shapes=[pltpu.SemaphoreType.DMA((3 * na,)), pltpu.SemaphoreType.DMA((3 * na,))],
        compiler_params=_cparams(("arbitrary",)),
    )(dq, dk, dv, dga, dgb, dza, du, dzb, df, wqkv_t, wrest_t, x, dx2, gs, *scatter_srcs)


def _row_block(rows, mult=8, cap=512):
    if rows <= mult:
        return rows
    padded = -(-rows // mult) * mult
    for cand in range(min(cap, padded) // mult * mult, 0, -mult):
        if padded % cand == 0:
            return cand
    return padded


def _sum4(parts, name):
    rows, cols = parts.shape[1:]
    br = _row_block(rows, 16, 1024)

    def body(p_ref, o_ref):
        acc = p_ref[0].astype(F32)
        for k in range(1, 4):
            acc = acc + p_ref[k].astype(F32)
        o_ref[...] = acc

    return pl.pallas_call(
        body, name=name, grid=(pl.cdiv(rows, br),),
        in_specs=[pl.BlockSpec((4, br, cols), lambda i: (0, i, 0))],
        out_specs=pl.BlockSpec((br, cols), lambda i: (i, 0)),
        out_shape=jax.ShapeDtypeStruct((rows, cols), F32), compiler_params=_cparams(("parallel",)))(parts)


def _pair_add(a, b, name):
    shape = a.shape
    a, b = a.reshape(-1, shape[-1]), b.reshape(-1, shape[-1])
    rows, cols = a.shape
    br = _row_block(rows, 16, 1024)

    def body(a_ref, b_ref, o_ref):
        o_ref[...] = (a_ref[...].astype(F32) + b_ref[...].astype(F32)).astype(BF16)

    spec = pl.BlockSpec((br, cols), lambda i: (i, 0))
    return pl.pallas_call(
        body, name=name, grid=(pl.cdiv(rows, br),), in_specs=[spec, spec], out_specs=spec,
        out_shape=jax.ShapeDtypeStruct((rows, cols), BF16), compiler_params=_cparams(("parallel",)))(a, b).reshape(shape)


def _adamw(w, g, m, v, name):
    rows, cols = w.shape
    br = _row_block(rows)

    def body(w_ref, g_ref, m_ref, v_ref, d_ref, nm_ref, nv_ref):
        gv = g_ref[...]
        nm = ADAM_B1 * m_ref[...] + (1.0 - ADAM_B1) * gv
        nv = ADAM_B2 * v_ref[...] + (1.0 - ADAM_B2) * (gv * gv)
        m_hat = nm / (1.0 - ADAM_B1 ** ADAM_STEP)
        v_hat = nv / (1.0 - ADAM_B2 ** ADAM_STEP)
        d_ref[...] = -ADAM_LR * (m_hat / (jnp.sqrt(v_hat) + ADAM_EPS) + ADAM_WD * w_ref[...])
        nm_ref[...] = nm
        nv_ref[...] = nv

    spec = pl.BlockSpec((br, cols), lambda i: (i, 0))
    shape = jax.ShapeDtypeStruct((rows, cols), F32)
    return pl.pallas_call(
        body, name=name, grid=(pl.cdiv(rows, br),), in_specs=[spec] * 4, out_specs=(spec,) * 3,
        out_shape=(shape,) * 3, compiler_params=_cparams(("parallel",)))(w, g, m, v)


def _pack(parts, row_multiple=8):
    flat = []
    for p in parts:
        v = p.reshape(-1).astype(F32)
        pad = (-v.shape[0]) % LANES
        if pad:
            v = jnp.concatenate([v, jnp.zeros((pad,), F32)])
        flat.append(v)
    v = jnp.concatenate(flat)
    rows = v.shape[0] // LANES
    pad_rows = (-rows) % row_multiple
    if pad_rows:
        v = jnp.concatenate([v, jnp.zeros((pad_rows * LANES,), F32)])
    return v.reshape(-1, LANES)


def _unpack(packed, shapes):
    lead = packed.shape[:-2]
    flat = packed.reshape(lead + (-1,))
    out, off = [], 0
    for sh in shapes:
        size = math.prod(sh)
        out.append(flat[..., off:off + size].reshape(lead + tuple(sh)))
        off += size + (-size) % LANES
    return out


def kernel(x, c, w_ada, b_ada, g_norm, w_in, b_f, a_re, a_im, log_dt, b_re, b_im, c_re, c_im, d_skip, w_glu, b_glu, w_up_a, w_up_b, w_out, g_final, loss_target, m_w_ada, m_b_ada, m_g_norm, m_w_in, m_b_f, m_a_re, m_a_im, m_log_dt, m_b_re, m_b_im, m_c_re, m_c_im, m_d_skip, m_w_glu, m_b_glu, m_w_up_a, m_w_up_b, m_w_out, m_g_final, v_w_ada, v_b_ada, v_g_norm, v_w_in, v_b_f, v_a_re, v_a_im, v_log_dt, v_b_re, v_b_im, v_c_re, v_c_im, v_d_skip, v_w_glu, v_b_glu, v_w_up_a, v_w_up_b, v_w_out, v_g_final):
    xi, yi, ci = lax.axis_index("x"), lax.axis_index("y"), lax.axis_index("c")
    chip = 2 * xi + yi
    me = 4 * xi + 2 * yi + ci
    s = x.shape[1]
    x2d = x[0]
    tgt = loss_target[0]
    n_att = s // min(T_ATT, s)
    t_att = min(T_ATT, s)

    c_all, _ = _allgather8(c.reshape(8, LANES), "gather_c")
    c_all = c_all.reshape(8, D_MODEL)
    ncol = w_ada.shape[2]
    b_cols = lax.dynamic_slice_in_dim(b_ada, chip * ncol, ncol, axis=1)
    mod_cols = _mod_cols(c_all, w_ada[0], b_cols)
    mod_all, _ = _allgather8(mod_cols.reshape(-1, LANES), "gather_mod")
    mod_all = mod_all.reshape(4, 2, 8, ncol)[:, 0]
    mod_me = lax.dynamic_index_in_dim(mod_all, me, axis=1, keepdims=False).reshape(1, 3 * D_MODEL)
    shift, scale, gate = mod_me[:, :D_MODEL], mod_me[:, D_MODEL:2 * D_MODEL], mod_me[:, 2 * D_MODEL:]
    gs = g_norm * (1.0 + scale)

    nshard = w_in.shape[2]
    w_in_t, m_in_t, v_in_t = (jnp.swapaxes(a[0], 0, 1) for a in (w_in, m_w_in, v_w_in))
    wt_pack = jnp.pad(w_in_t.astype(BF16), ((0, SHARD_ROWS - nshard), (0, 0)))
    misc_shapes = [w_glu.shape[1:], w_up_a.shape[1:], w_up_b.shape[1:], w_out.shape[1:]]
    misc_pack = jnp.concatenate([w.reshape(-1) for w in (w_glu, w_up_a, w_up_b, w_out)]).astype(BF16).reshape(-1, LANES)
    def halves(a):
        return a.reshape((2, a.shape[0] // 2) + a.shape[1:])

    wt_all, misc_all = _gather_shards([halves(wt_pack), halves(misc_pack)], "gather_weights")
    wt_all = lax.dynamic_update_index_in_dim(wt_all, halves(wt_pack), chip, 0).reshape((4,) + wt_pack.shape)
    misc_all = lax.dynamic_update_index_in_dim(misc_all, halves(misc_pack), chip, 0).reshape((4,) + misc_pack.shape)
    p_glu, p_ua, p_ub, p_out = _unpack(misc_all, misc_shapes)

    def w_rows(lo, hi):
        out = []
        for j in range(4):
            a, b = max(lo, j * nshard), min(hi, (j + 1) * nshard)
            if a < b:
                out.append(wt_all[j, a - j * nshard:b - j * nshard])
        return out

    wqkv_t = jnp.concatenate(w_rows(O_Q, O_F), axis=0)
    wrest_t = jnp.concatenate(w_rows(O_GA, O_GB) + w_rows(O_GB, O_END) + w_rows(O_ZA, O_U) + w_rows(O_U, O_ZB)
                              + w_rows(O_ZB, O_GA) + w_rows(O_F, O_ZA)
                              + [jnp.zeros((REST_W - R_F - HEADS, D_MODEL), BF16)], axis=0)
    wmid = (p_glu.reshape(S5_W, S5_W), jnp.concatenate([p_ua[j] for j in range(4)], axis=1),
            jnp.concatenate([p_ub[j] for j in range(4)], axis=1), p_out.reshape(D_MODEL, D_MODEL))

    h, qkv, rest = _prenorm_proj(x2d, gs, shift, wqkv_t, wrest_t)
    bf128 = jnp.pad(b_f, ((0, 0), (0, LANES - HEADS)))
    selp = _head_pair_selector()
    fpc, f_t = _fcum(rest, bf128, selp)
    frow5 = f_t.reshape(4, 2, n_att, 1, t_att)
    o, lse_pc = _attn_fwd(qkv, frow5, fpc)

    abar_r, abar_i, bb_r, bb_i = _ssm_block_params(a_re[0], a_im[0], log_dt[0], b_re[0], b_im[0])
    bb_rt, bb_it = jnp.swapaxes(bb_r, 1, 2).astype(BF16), jnp.swapaxes(bb_i, 1, 2).astype(BF16)
    cr_b, ci_b = c_re[0].astype(BF16), (-c_im[0]).astype(BF16)
    bd_c, cd_c = _compact_diag(bb_rt, bb_it), _compact_diag(cr_b, ci_b)
    ys0, xs = _ssm_fwd(rest, bd_c, cd_c, _scan_consts(a_re[0], a_im[0], log_dt[0], False))

    vec = jnp.concatenate([gate, g_final.reshape(1, D_MODEL), jnp.concatenate([d_skip, b_glu], axis=1),
                           jnp.zeros((5, D_MODEL), F32)], axis=0)
    hsel = jnp.repeat(jnp.eye(HEADS, dtype=F32), HEAD_DIM, axis=1)
    (dx2, dga, dgb, do, dza, dzb, dys, dlt_t, g_out, g_ua, g_ub, g_glu, vmid) = _mid(
        o, rest, ys0, x2d, tgt, wmid, vec, hsel)

    lse_t = jnp.transpose(lse_pc.reshape(s, 4 // ATT_PAIRS, LANES)[:, :, :2 * ATT_PAIRS], (1, 2, 0))
    lse5 = lse_t.reshape(4, 2, n_att, 1, t_att)
    dlt5 = dlt_t.reshape(4, 2, n_att, 1, t_att)
    dq, dk, dv, dfk, dfq = _attn_bwd(qkv, do, lse5, dlt5, frow5, fpc)
    du, g_bd, g_cdt, da8 = _ssm_bwd(dys, xs, rest, bd_c, cd_c, _scan_consts(a_re[0], a_im[0], log_dt[0], True), d_skip)
    df, dbf8 = _dfcum(dfk, dfq, rest, bf128, selp.T)

    gq, gk, gv, gga, ggb, gza, gu, gzb, gf = _grad_w_rows(h, [dq, dk, dv, dga, dgb, dza, du, dzb, df])
    g_in_t = jnp.concatenate([gq, gk, gv, gf[:HEADS], gza, gu, gzb, gga, ggb], axis=0)

    def shard_cols(g, j):
        n = g.shape[1] // 4
        return g[:, j * n:(j + 1) * n]

    def shard_rows(g, j):
        n = g.shape[0] // 4
        return g[j * n:(j + 1) * n]

    def halves4(a):
        return a.reshape((4, 2, a.shape[1] // 2) + a.shape[2:])

    gt_pack = halves4(jnp.stack([
        jnp.pad(g_in_t[j * nshard:(j + 1) * nshard].astype(BF16), ((0, SHARD_ROWS - nshard), (0, 0)))
        for j in range(4)]))
    gm_pack = halves4(jnp.stack([
        jnp.concatenate([shard_rows(g_glu, j).reshape(-1), shard_cols(g_ua, j).reshape(-1),
                         shard_cols(g_ub, j).reshape(-1), shard_rows(g_out, j).reshape(-1)]).astype(BF16)
        .reshape(-1, LANES) for j in range(4)]))
    recv_in, recv_misc = _swap_sibling([gt_pack, gm_pack], "pair_swap_weight_grads", other_half=True)
    own_in = lax.dynamic_index_in_dim(gt_pack, ci, axis=1, keepdims=False)
    own_misc = lax.dynamic_index_in_dim(gm_pack, ci, axis=1, keepdims=False)
    pair_in = _pair_add(own_in, recv_in, "pair_add_w_in")
    pair_misc = _pair_add(own_misc, recv_misc, "pair_add_misc")

    grad_x, vdh, parts_in, parts_misc = _dh(dq, dk, dv, dga, dgb, dza, du, dzb, df, wqkv_t, wrest_t, x2d, dx2, gs,
                                            [pair_in, pair_misc])
    parts_in = lax.dynamic_update_slice_in_dim(parts_in, lax.dynamic_slice_in_dim(pair_in, chip, 1, 0), chip, 0)
    parts_misc = lax.dynamic_update_slice_in_dim(parts_misc, lax.dynamic_slice_in_dim(pair_misc, chip, 1, 0), chip, 0)
    half_in, half_misc = _sum4(parts_in, "sum4_w_in"), _sum4(parts_misc, "sum4_misc")
    sib_in, sib_misc = _swap_sibling([half_in, half_misc], "swap_weight_grads")

    def both_halves(mine, theirs):
        return jnp.concatenate([jnp.where(ci == 0, mine, theirs), jnp.where(ci == 0, theirs, mine)], axis=0)

    tot_in, tot_misc = both_halves(half_in, sib_in), both_halves(half_misc, sib_misc)
    g_glu_s, g_ua_s, g_ub_s, g_out_s = _unpack(tot_misc, misc_shapes)

    dgs, dshift = vdh[0:1], vdh[1:2]
    dmod = jnp.concatenate([dshift, dgs * g_norm, vmid[1:2]], axis=1)
    da = jnp.sum(da8, axis=0)
    g_bd = g_bd.reshape(GROUPS, GCH, 2 * STATE)
    g_cdt = g_cdt.reshape(GROUPS, GCH, 2 * STATE)
    g_bbr = jnp.swapaxes(g_bd[:, :, :STATE], 1, 2)
    g_bbi = jnp.swapaxes(g_bd[:, :, STATE:], 1, 2)
    g_cre = g_cdt[:, :, :STATE]
    g_cim = -g_cdt[:, :, STATE:]
    small_shapes = [(1,), (3 * D_MODEL,), (D_MODEL,), (HEADS,), (GROUPS, STATE), (GROUPS, STATE),
                    (GROUPS, STATE, GCH), (GROUPS, STATE, GCH), (GROUPS, GCH, STATE), (GROUPS, GCH, STATE),
                    (S5_W,), (S5_W,), (D_MODEL,)]
    small = _pack([vmid[3, 0:1], dmod, dgs * (1.0 + scale), dbf8[0, :HEADS], da[:NSTATE], da[NSTATE:],
                   g_bbr, g_bbi, g_cre, g_cim, vmid[2, :S5_W], vmid[2, S5_W:], vmid[0]])
    small_all, small_sum = _allgather8(small, "gather_small_grads")
    (loss_s, g_b_ada, g_g_norm, g_b_f, g_abr, g_abi, g_bbr_s, g_bbi_s, g_c_re, g_c_im, g_d_skip, g_b_glu,
     g_g_final) = _unpack(small_sum, small_shapes)
    loss = loss_s[0]
    dmod_all = _unpack(small_all, small_shapes)[1]
    dmod_cols = lax.dynamic_slice_in_dim(dmod_all, chip * ncol, ncol, axis=1)
    g_w_ada = _grad_w_ada(c_all, dmod_cols)
    _, ssm_vjp = jax.vjp(_ssm_block_params, a_re[0], a_im[0], log_dt[0], b_re[0], b_im[0])
    g_a_re, g_a_im, g_log_dt, g_b_re, g_b_im = ssm_vjp((g_abr, g_abi, g_bbr_s, g_bbi_s))

    def adam(name, w, g, m, v):
        shape = w.shape
        total = math.prod(shape)
        if len(shape) > 1 and shape[-1] >= LANES:
            cols = shape[-1]
        elif total % LANES == 0:
            cols = LANES
        else:
            cols = total
        two = lambda a: a.reshape(-1, cols)
        d, nm, nv = _adamw(two(w), two(g), two(m), two(v), "adamw_" + name)
        return g.reshape(shape), d.reshape(shape), nm.reshape(shape), nv.reshape(shape)

    back = lambda a: jnp.swapaxes(a, 0, 1)[None]
    d_in_t, nm_in_t, nv_in_t = _adamw(w_in_t, tot_in, m_in_t, v_in_t, "adamw_w_in")
    res_w_in = (back(tot_in[:nshard]), back(d_in_t), back(nm_in_t), back(nv_in_t))

    res = [
        adam("w_ada", w_ada, g_w_ada, m_w_ada, v_w_ada),
        adam("b_ada", b_ada, g_b_ada, m_b_ada, v_b_ada),
        adam("g_norm", g_norm, g_g_norm, m_g_norm, v_g_norm),
        res_w_in,
        adam("b_f", b_f, g_b_f, m_b_f, v_b_f),
        adam("a_re", a_re, g_a_re, m_a_re, v_a_re),
        adam("a_im", a_im, g_a_im, m_a_im, v_a_im),
        adam("log_dt", log_dt, g_log_dt, m_log_dt, v_log_dt),
        adam("b_re", b_re, g_b_re, m_b_re, v_b_re),
        adam("b_im", b_im, g_b_im, m_b_im, v_b_im),
        adam("c_re", c_re, g_c_re, m_c_re, v_c_re),
        adam("c_im", c_im, g_c_im, m_c_im, v_c_im),
        adam("d_skip", d_skip, g_d_skip, m_d_skip, v_d_skip),
        adam("w_glu", w_glu, g_glu_s, m_w_glu, v_w_glu),
        adam("b_glu", b_glu, g_b_glu, m_b_glu, v_b_glu),
        adam("w_up_a", w_up_a, g_ua_s, m_w_up_a, v_w_up_a),
        adam("w_up_b", w_up_b, g_ub_s, m_w_up_b, v_w_up_b),
        adam("w_out", w_out, g_out_s, m_w_out, v_w_out),
        adam("g_final", g_final, g_g_final, m_g_final, v_g_final),
    ]
    grads = [r[0] for r in res]
    deltas = [r[1] for r in res]
    new_m = [r[2] for r in res]
    new_v = [r[3] for r in res]
    return (loss, grad_x[None], *grads, *deltas, *new_m, *new_v)
```

```python
import functools
import math

import jax
import jax.numpy as jnp
from jax import lax
from jax.experimental import pallas as pl
from jax.experimental.pallas import tpu as pltpu

F32 = jnp.float32
BF16 = jnp.bfloat16
HI = lax.Precision.HIGHEST
MESH = pl.DeviceIdType.MESH

D_MODEL = 1024
HEADS = 8
HEAD_DIM = 64
FOX_W = 512
S5_W = 512
GROUPS = 32
STATE = 64
GCH = 16
NSTATE = GROUPS * STATE
EPS = 1e-6
NEG = -1e30

ADAM_LR = 0.001
ADAM_B1 = 0.9
ADAM_B2 = 0.999
ADAM_EPS = 1e-08
ADAM_WD = 0.01
ADAM_STEP = 10

VMEM_LIMIT = 56 * 1024 * 1024
LANES = 128

TM = 256
TM_PROJ = 512
T_ATT = 512
ATT_CHUNK = 32
ATT_PAIRS = 4
TB_SSM = 512
TK_ACC = 512
TB_CUM = 256
SHARD_ROWS = 1312

O_Q, O_K, O_V, O_F, O_ZA, O_U, O_ZB, O_GA, O_GB, O_END = 0, 512, 1024, 1536, 1544, 2056, 2568, 3080, 4104, 5128
REST_W = 3712
R_GA, R_GB, R_ZA, R_U, R_ZB, R_F = 0, 1024, 2048, 2560, 3072, 3584


def _cparams(sem=None):
    kw = dict(vmem_limit_bytes=VMEM_LIMIT)
    if sem is not None:
        kw["dimension_semantics"] = sem
    return pltpu.CompilerParams(**kw)


def _const(shape):
    nd = len(shape)
    return pl.BlockSpec(shape, lambda *_: (0,) * nd, pipeline_mode=pl.Buffered(1))


def _dot(a, b, precision=None):
    return jnp.dot(a, b, preferred_element_type=F32, precision=precision)


def _dot_nt(a, b):
    return lax.dot_general(a, b, (((1,), (1,)), ((), ())), preferred_element_type=F32)


def _dot_tn(a, b, precision=None):
    return lax.dot_general(a, b, (((0,), (0,)), ((), ())), preferred_element_type=F32, precision=precision)


def _sigmoid(z):
    return 1.0 / (1.0 + jnp.exp(-z))


def _split3(x):
    hi = x.astype(BF16)
    r1 = x - hi.astype(F32)
    mid = r1.astype(BF16)
    lo = (r1 - mid.astype(F32)).astype(BF16)
    return hi, mid, lo


def _dot_sel(sel, x, terms=3):
    s16 = sel.astype(BF16)
    return sum(_dot(s16, part) for part in _split3(x)[:terms])


def _dot_by_sel(x, sel):
    s16 = sel.astype(BF16)
    return sum(_dot(part, s16) for part in _split3(x))


def _allgather8(xs, name):
    rows = xs.shape[0]

    def body(x_ref, out_ref, sum_ref, send_sems, recv_sems, local_sem):
        x, y, c = lax.axis_index("x"), lax.axis_index("y"), lax.axis_index("c")
        me, sibling = (x, y, c), (x, y, 1 - c)
        chips = [(1 - x, y), (x, 1 - y), (1 - x, 1 - y)]

        def slot(px, py, pc):
            return out_ref.at[4 * px + 2 * py + pc]

        def copy(k, block, to, src=None):
            return pltpu.make_async_remote_copy(
                src_ref=slot(*block) if src is None else src, dst_ref=slot(*block),
                send_sem=send_sems.at[k], recv_sem=recv_sems.at[k], device_id=to, device_id_type=MESH)

        mine = pltpu.make_async_copy(x_ref, slot(*me), local_sem)
        mine.start()
        first = [copy(0, me, sibling, src=x_ref)]
        first += [copy(1 + j, me, (*chip, c), src=x_ref) for j, chip in enumerate(chips)]
        for cp in first:
            cp.start()
        passed = [copy(4 + j, (*chip, c), sibling) for j, chip in enumerate(chips)]
        for j, chip in enumerate(chips):
            copy(1 + j, (*chip, c), me).wait_recv()
            passed[j].start()
        copy(0, sibling, me).wait_recv()
        for j, chip in enumerate(chips):
            copy(4 + j, (*chip, 1 - c), me).wait_recv()
        for cp in first + passed:
            cp.wait_send()
        mine.wait()
        acc = out_ref[0]
        for d in range(1, 8):
            acc = acc + out_ref[d]
        sum_ref[...] = acc

    return pl.pallas_call(
        body, name=name,
        out_shape=(jax.ShapeDtypeStruct((8, rows, LANES), F32), jax.ShapeDtypeStruct((rows, LANES), F32)),
        in_specs=[pl.BlockSpec(memory_space=pltpu.VMEM)],
        out_specs=(pl.BlockSpec(memory_space=pltpu.VMEM), pl.BlockSpec(memory_space=pltpu.VMEM)),
        scratch_shapes=[pltpu.SemaphoreType.DMA((7,)), pltpu.SemaphoreType.DMA((7,)), pltpu.SemaphoreType.DMA],
        compiler_params=_cparams(),
    )(xs)


def _gather_shards(srcs, name):
    na = len(srcs)

    def body(*refs):
        src_refs, out_refs = refs[:na], refs[na:2 * na]
        send_sems, recv_sems = refs[2 * na:]
        x, y, c = lax.axis_index("x"), lax.axis_index("y"), lax.axis_index("c")
        sibling = (x, y, 1 - c)
        peers = [(1 - x, y), (x, 1 - y), (1 - x, 1 - y)]

        def copy(a, k, src, slot, which, to):
            return pltpu.make_async_remote_copy(
                src_ref=src, dst_ref=out_refs[a].at[slot, which],
                send_sem=send_sems.at[a * 6 + k], recv_sem=recv_sems.at[a * 6 + k],
                device_id=to, device_id_type=MESH)

        mine = 2 * x + y
        first = [copy(a, k, src_refs[a].at[c], mine, c, (px, py, c))
                 for a in range(na) for k, (px, py) in enumerate(peers)]
        for cp in first:
            cp.start()
        passed = []
        for a in range(na):
            for k, (px, py) in enumerate(peers):
                slot = 2 * px + py
                landed = out_refs[a].at[slot, c]
                copy(a, k, landed, slot, c, (px, py, c)).wait_recv()
                fwd = copy(a, 3 + k, landed, slot, c, sibling)
                fwd.start()
                passed.append(fwd)
        for a in range(na):
            for k, (px, py) in enumerate(peers):
                slot = 2 * px + py
                copy(a, 3 + k, out_refs[a].at[slot, 1 - c], slot, 1 - c, sibling).wait_recv()
        for cp in first + passed:
            cp.wait_send()

    anyspace = pl.BlockSpec(memory_space=pl.ANY)
    return pl.pallas_call(
        body, name=name,
        out_shape=tuple(jax.ShapeDtypeStruct((4,) + tuple(a.shape), a.dtype) for a in srcs),
        in_specs=[anyspace] * na, out_specs=(anyspace,) * na,
        scratch_shapes=[pltpu.SemaphoreType.DMA((6 * na,)), pltpu.SemaphoreType.DMA((6 * na,))],
        compiler_params=_cparams(),
    )(*srcs)


def _swap_sibling(srcs, name, other_half=False):
    na = len(srcs)

    def body(*refs):
        src_refs, out_refs = refs[:na], refs[na:2 * na]
        send_sems, recv_sems = refs[2 * na:]
        x, y, c = lax.axis_index("x"), lax.axis_index("y"), lax.axis_index("c")
        copies = [pltpu.make_async_remote_copy(
            src_ref=src_refs[a].at[:, 1 - c] if other_half else src_refs[a],
            dst_ref=out_refs[a], send_sem=send_sems.at[a], recv_sem=recv_sems.at[a],
            device_id=(x, y, 1 - c), device_id_type=MESH) for a in range(na)]
        for cp in copies:
            cp.start()
        for cp in copies:
            cp.wait()

    def out_of(a):
        shape = (a.shape[0],) + tuple(a.shape[2:]) if other_half else a.shape
        return jax.ShapeDtypeStruct(shape, a.dtype)

    anyspace = pl.BlockSpec(memory_space=pl.ANY)
    return pl.pallas_call(
        body, name=name, out_shape=tuple(out_of(a) for a in srcs),
        in_specs=[anyspace] * na, out_specs=(anyspace,) * na,
        scratch_shapes=[pltpu.SemaphoreType.DMA((na,)), pltpu.SemaphoreType.DMA((na,))],
        compiler_params=_cparams(),
    )(*srcs)


def _mod_cols(c_all, w, b):
    n = w.shape[1]

    def body(c_ref, w_ref, b_ref, o_ref):
        o_ref[...] = _dot(c_ref[...], w_ref[...], HI) + b_ref[...]

    return pl.pallas_call(
        body, name="mod_cols", out_shape=jax.ShapeDtypeStruct((8, n), F32),
        compiler_params=_cparams())(c_all, w, b)


def _grad_w_ada(c_all, dmod_cols):
    n = dmod_cols.shape[1]

    def body(c_ref, d_ref, o_ref):
        o_ref[...] = _dot_tn(c_ref[...], d_ref[...], HI)

    return pl.pallas_call(
        body, name="grad_w_ada", out_shape=jax.ShapeDtypeStruct((D_MODEL, n), F32),
        compiler_params=_cparams())(c_all, dmod_cols)


def _prenorm_proj(x, gs, shift, wqkv_t, wrest_t):
    s = x.shape[0]
    tm = min(TM_PROJ, s)
    nq, nr = wqkv_t.shape[0], wrest_t.shape[0]

    def body(x_ref, gs_ref, sh_ref, wq_ref, wr_ref, h_ref, qkv_ref, rest_ref):
        xv = x_ref[...]
        r = lax.rsqrt(jnp.mean(xv * xv, axis=-1, keepdims=True) + EPS)
        h = (xv * r * gs_ref[...] + sh_ref[...]).astype(BF16)
        h_ref[...] = h
        qkv_ref[...] = _dot_nt(h, wq_ref[...]).astype(BF16)
        rest_ref[...] = _dot_nt(h, wr_ref[...])

    def rows(width):
        return pl.BlockSpec((tm, width), lambda i: (i, 0))

    return pl.pallas_call(
        body, name="prenorm_proj", grid=(s // tm,),
        in_specs=[rows(D_MODEL), _const((1, D_MODEL)), _const((1, D_MODEL)), _const((nq, D_MODEL)),
                  _const((nr, D_MODEL))],
        out_specs=(rows(D_MODEL), rows(nq), rows(nr)),
        out_shape=(jax.ShapeDtypeStruct((s, D_MODEL), BF16), jax.ShapeDtypeStruct((s, nq), BF16),
                   jax.ShapeDtypeStruct((s, nr), F32)),
        compiler_params=_cparams(("parallel",)))(x, gs, shift, wqkv_t, wrest_t)


def _grad_w_rows(h, ds):
    s = h.shape[0]
    tk = min(TK_ACC, s)
    nd = len(ds)
    widths = [d.shape[1] for d in ds]

    def body(*refs):
        h_ref, d_refs = refs[0], refs[1:1 + nd]
        out_refs, accs = refs[1 + nd:1 + 2 * nd], refs[1 + 2 * nd:]
        step = pl.program_id(0)

        @pl.when(step == 0)
        def _():
            for acc in accs:
                acc[...] = jnp.zeros_like(acc)

        hv = h_ref[...]
        for d_ref, acc in zip(d_refs, accs):
            acc[...] += _dot_tn(d_ref[...], hv)

        @pl.when(step == s // tk - 1)
        def _():
            for acc, out in zip(accs, out_refs):
                pltpu.sync_copy(acc, out)

    anyspace = pl.BlockSpec(memory_space=pl.ANY)
    return pl.pallas_call(
        body, name="grad_w_in", grid=(s // tk,),
        in_specs=[pl.BlockSpec((tk, D_MODEL), lambda k: (k, 0))]
                 + [pl.BlockSpec((tk, w), lambda k: (k, 0)) for w in widths],
        out_specs=(anyspace,) * nd,
        out_shape=tuple(jax.ShapeDtypeStruct((w, D_MODEL), F32) for w in widths),
        scratch_shapes=[pltpu.VMEM((w, D_MODEL), F32) for w in widths],
        compiler_params=_cparams(("arbitrary",)))(h, *ds)


def _head_pair_selector():
    rows = jnp.arange(LANES)[:, None]
    cols = jnp.arange(4 * LANES)[None, :]
    return ((rows < HEADS) & (cols == (rows // 2) * LANES + rows % 2)).astype(F32)


def _fcum(rest, bf128, selp):
    s = rest.shape[0]
    tb = min(TB_CUM, s)

    def body(fz_ref, bf_ref, sel_ref, fpc_ref, ft_ref, carry_ref):
        @pl.when(pl.program_id(0) == 0)
        def _():
            carry_ref[...] = jnp.zeros_like(carry_ref)

        z = fz_ref[...] + bf_ref[...]
        logf = jnp.minimum(z, 0.0) - jnp.log(1.0 + jnp.exp(-jnp.abs(z)))
        r = lax.broadcasted_iota(jnp.int32, (tb, tb), 0)
        c = lax.broadcasted_iota(jnp.int32, (tb, tb), 1)
        tri = (c <= r).astype(F32)
        f = _dot_sel(tri, logf) + carry_ref[0:1, :]
        carry_ref[0:1, :] = f[tb - 1:tb, :]
        fpc_ref[...] = _dot_by_sel(f, sel_ref[...])
        ft_ref[...] = jnp.transpose(f)[0:HEADS, :]

    return pl.pallas_call(
        body, name="forget_cumsum", grid=(s // tb,),
        in_specs=[pl.BlockSpec((tb, LANES), lambda i: (i, R_F // LANES)), _const((1, LANES)), _const((LANES, 4 * LANES))],
        out_specs=(pl.BlockSpec((tb, 4 * LANES), lambda i: (i, 0)), pl.BlockSpec((HEADS, tb), lambda i: (0, i))),
        out_shape=(jax.ShapeDtypeStruct((s, 4 * LANES), F32), jax.ShapeDtypeStruct((HEADS, s), F32)),
        scratch_shapes=[pltpu.VMEM((8, LANES), F32)],
        compiler_params=_cparams(("arbitrary",)))(rest, bf128, selp)


def _dfcum(dfk, dfq, rest, bf128, selq):
    s = rest.shape[0]
    tb = min(TB_CUM, s)
    nb = s // tb

    def body(dk_ref, dq_ref, fz_ref, bf_ref, sel_ref, df_ref, dbf_ref, carry_ref):
        @pl.when(pl.program_id(0) == 0)
        def _():
            carry_ref[...] = jnp.zeros_like(carry_ref)
            dbf_ref[...] = jnp.zeros_like(dbf_ref)

        d = _dot_by_sel(dk_ref[...] + dq_ref[...], sel_ref[...])
        r = lax.broadcasted_iota(jnp.int32, (tb, tb), 0)
        c = lax.broadcasted_iota(jnp.int32, (tb, tb), 1)
        triu = (c >= r).astype(F32)
        dlogf = _dot_sel(triu, d) + carry_ref[0:1, :]
        carry_ref[0:1, :] = dlogf[0:1, :]
        z = fz_ref[...] + bf_ref[...]
        df = dlogf * (1.0 / (1.0 + jnp.exp(z)))
        df_ref[...] = df.astype(BF16)
        dbf_ref[0:1, :] += jnp.sum(df, axis=0, keepdims=True)

    return pl.pallas_call(
        body, name="forget_grad", grid=(nb,),
        in_specs=[pl.BlockSpec((tb, 4 * LANES), lambda i: (nb - 1 - i, 0)),
                  pl.BlockSpec((tb, 4 * LANES), lambda i: (nb - 1 - i, 0)),
                  pl.BlockSpec((tb, LANES), lambda i: (nb - 1 - i, R_F // LANES)),
                  _const((1, LANES)), _const((4 * LANES, LANES))],
        out_specs=(pl.BlockSpec((tb, LANES), lambda i: (nb - 1 - i, 0)), pl.BlockSpec((8, LANES), lambda i: (0, 0))),
        out_shape=(jax.ShapeDtypeStruct((s, LANES), BF16), jax.ShapeDtypeStruct((8, LANES), F32)),
        scratch_shapes=[pltpu.VMEM((8, LANES), F32)],
        compiler_params=_cparams(("arbitrary",)))(dfk, dfq, rest, bf128, selq)


def _scaled(q):
    return (q.astype(F32) * (HEAD_DIM ** -0.5)).astype(BF16)


def _attn_fwd(qkv, frow5, fpc):
    s = qkv.shape[0]
    t = min(T_ATT, s)
    n = s // t
    ch = min(ATT_CHUNK, t)
    wide = 2 * LANES
    pairs = ATT_PAIRS
    width = pairs * LANES
    groups = 4 // pairs

    def body(q_ref, k_ref, v_ref, fr_ref, fc_ref, o_ref, lse_ref, s_scr, p_scr, m_scr, a_scr, fq_scr, acc_scr):
        i = pl.program_id(1)
        lane = lax.broadcasted_iota(jnp.int32, (t, LANES), 1)
        first = lane < HEAD_DIM
        ones_col = ((lane == 0).astype(BF16), (lane == 1).astype(BF16))
        m_scr[...] = jnp.full(m_scr.shape, NEG, F32)
        acc_scr[...] = jnp.zeros_like(acc_scr)
        qm = []
        for pp in range(pairs):
            q = _scaled(q_ref[:, pp * LANES:(pp + 1) * LANES])
            zq = jnp.zeros_like(q)
            qm += [jnp.where(first, q, zq), jnp.where(first, zq, q)]
            fq_scr[2 * pp] = fc_ref[:, pp * LANES:pp * LANES + 1]
            fq_scr[2 * pp + 1] = fc_ref[:, pp * LANES + 1:pp * LANES + 2]

        def step(j, masked):
            r0 = pl.multiple_of(j * t, t)
            vaug = []
            for pp in range(pairs):
                kb = k_ref[pl.ds(r0, t), pp * LANES:(pp + 1) * LANES]
                vb = v_ref[pl.ds(r0, t), pp * LANES:(pp + 1) * LANES]
                zv = jnp.zeros_like(vb)
                vaug += [jnp.concatenate([jnp.where(first, vb, zv), ones_col[0]], axis=1),
                         jnp.concatenate([jnp.where(first, zv, vb), ones_col[1]], axis=1)]
                for hh in range(2):
                    s_scr[2 * pp + hh] = _dot_nt(qm[2 * pp + hh], kb)
            pv = []
            for hd in range(2 * pairs):
                fk = fr_ref[hd // 2, hd % 2, j]
                for c in range(t // ch):
                    rows = pl.ds(c * ch, ch)
                    hi = min(t, (c * ch // LANES + 1) * LANES) if masked else t
                    sc = s_scr[hd, rows, 0:hi] - fk[:, 0:hi]
                    if masked:
                        rq = c * ch + lax.broadcasted_iota(jnp.int32, (ch, hi), 0)
                        ck = lax.broadcasted_iota(jnp.int32, (ch, hi), 1)
                        sc = jnp.where(ck <= rq, sc, NEG)
                    fq = fq_scr[hd, rows, :]
                    m_old = m_scr[hd, rows, :]
                    m_new = jnp.maximum(m_old, fq + jnp.max(sc, axis=1, keepdims=True))
                    p_scr[hd, rows, 0:hi] = jnp.exp(sc + (fq - m_new)).astype(BF16)
                    if hi < t:
                        p_scr[hd, rows, hi:t] = jnp.zeros((ch, t - hi), BF16)
                    a_scr[hd, rows, :] = jnp.exp(m_old - m_new)
                    m_scr[hd, rows, :] = m_new
                pv.append(_dot(p_scr[hd], vaug[hd]))
            for pp in range(pairs):
                a0, a1 = a_scr[2 * pp], a_scr[2 * pp + 1]
                alpha = jnp.concatenate([jnp.where(first, a0, a1), jnp.where(lane == 0, a0, a1)], axis=1)
                acc_scr[pp] = acc_scr[pp] * alpha + pv[2 * pp] + pv[2 * pp + 1]
            return 0

        lax.fori_loop(0, i, lambda j, _: step(j, False), 0)
        step(i, True)
        lse = jnp.zeros((t, LANES), F32)
        for pp in range(pairs):
            l0 = acc_scr[pp, :, LANES:LANES + 1]
            l1 = acc_scr[pp, :, LANES + 1:LANES + 2]
            o_ref[:, pp * LANES:(pp + 1) * LANES] = acc_scr[pp, :, 0:LANES] * jnp.where(first, 1.0 / l0, 1.0 / l1)
            lse = jnp.where(lane == 2 * pp, m_scr[2 * pp] + jnp.log(l0), lse)
            lse = jnp.where(lane == 2 * pp + 1, m_scr[2 * pp + 1] + jnp.log(l1), lse)
        lse_ref[...] = lse

    blk = pl.BlockSpec((t, width), lambda g, i: (i, g))
    return pl.pallas_call(
        body, name="attn_fwd", grid=(groups, n),
        in_specs=[blk,
                  pl.BlockSpec((s, width), lambda g, i: (0, groups + g)),
                  pl.BlockSpec((s, width), lambda g, i: (0, 2 * groups + g)),
                  pl.BlockSpec((pairs, 2, n, 1, t), lambda g, i: (g, 0, 0, 0, 0)),
                  blk],
        out_specs=(blk, pl.BlockSpec((t, LANES), lambda g, i: (i, g))),
        out_shape=(jax.ShapeDtypeStruct((s, FOX_W), F32), jax.ShapeDtypeStruct((s, groups * LANES), F32)),
        scratch_shapes=[pltpu.VMEM((2 * pairs, t, t), F32), pltpu.VMEM((2 * pairs, t, t), BF16),
                        pltpu.VMEM((2 * pairs, t, 1), F32), pltpu.VMEM((2 * pairs, t, 1), F32),
                        pltpu.VMEM((2 * pairs, t, 1), F32), pltpu.VMEM((pairs, t, wide), F32)],
        compiler_params=_cparams(("parallel", "arbitrary")))(qkv, qkv, qkv, frow5, fpc)


def _attn_bwd(qkv, do, lse5, dlt5, frow5, fpc):
    s = qkv.shape[0]
    t = min(T_ATT, s)
    n = s // t
    wide = 2 * LANES

    ch = min(ATT_CHUNK, t)

    def body(q_ref, do_ref, k_ref, v_ref, lse_ref, dl_ref, fr_ref, fc_ref,
             dq_ref, dk_ref, dv_ref, dfk_ref, dfq_ref, dq_acc, st_scr, dp_scr, pt_scr, ds_scr, dk_acc, dv_acc, fk_scr):
        j = pl.program_id(1)

        @pl.when(j == 0)
        def _():
            dq_acc[...] = jnp.zeros_like(dq_acc)

        dk_acc[...] = jnp.zeros_like(dk_acc)
        dv_acc[...] = jnp.zeros_like(dv_acc)
        lane = lax.broadcasted_iota(jnp.int32, (t, LANES), 1)
        first = lane < HEAD_DIM
        ones_col = ((lane == 0).astype(BF16), (lane == 1).astype(BF16))
        kb = k_ref[...]
        vb = v_ref[...]
        zk = jnp.zeros_like(kb)
        kaug = (jnp.concatenate([jnp.where(first, kb, zk), ones_col[0]], axis=1),
                jnp.concatenate([jnp.where(first, zk, kb), ones_col[1]], axis=1))
        fk_scr[0] = fc_ref[:, 0:1]
        fk_scr[1] = fc_ref[:, 1:2]

        def step(blocks, masked):
            chains = []
            for bi, i in enumerate(blocks):
                r0 = pl.multiple_of(i * t, t)
                qb = _scaled(q_ref[pl.ds(r0, t), :])
                dob = do_ref[pl.ds(r0, t), :]
                zq = jnp.zeros_like(qb)
                qm = (jnp.where(first, qb, zq), jnp.where(first, zq, qb))
                dom = (jnp.where(first, dob, zq), jnp.where(first, zq, dob))
                for hh in range(2):
                    st_scr[2 * bi + hh] = _dot_nt(kb, qm[hh])
                    dp_scr[2 * bi + hh] = _dot_nt(vb, dom[hh])
                    chains.append((i, hh, qm[hh], dom[hh]))
            dq_add = [jnp.zeros((t, wide), F32) for _ in blocks]
            for cn, (i, hh, qmh, domh) in enumerate(chains):
                bias = fr_ref[0, hh, i] - lse_ref[0, hh, i]
                dl = dl_ref[0, hh, i]
                for c in range(t // ch):
                    rows = pl.ds(c * ch, ch)
                    lo = c * ch // LANES * LANES if masked else 0
                    st = st_scr[cn, rows, lo:t] + (bias[:, lo:t] - fk_scr[hh, rows, :])
                    if masked:
                        rk = c * ch + lax.broadcasted_iota(jnp.int32, (ch, t - lo), 0)
                        cq = lo + lax.broadcasted_iota(jnp.int32, (ch, t - lo), 1)
                        st = jnp.where(rk <= cq, st, NEG)
                    pt = jnp.exp(st)
                    pt_scr[cn, rows, lo:t] = pt.astype(BF16)
                    ds_scr[cn, rows, lo:t] = (pt * (dp_scr[cn, rows, lo:t] - dl[:, lo:t])).astype(BF16)
                    if lo > 0:
                        pt_scr[cn, rows, 0:lo] = jnp.zeros((ch, lo), BF16)
                        ds_scr[cn, rows, 0:lo] = jnp.zeros((ch, lo), BF16)
                dsb = ds_scr[cn]
                dv_acc[...] += _dot(pt_scr[cn], domh)
                dk_acc[...] += _dot(dsb, jnp.concatenate([qmh, ones_col[hh]], axis=1))
                dq_add[cn // 2] = dq_add[cn // 2] + _dot_tn(dsb, kaug[hh])
            for bi, i in enumerate(blocks):
                dq_acc[pl.ds(pl.multiple_of(i * t, t), t), :] += dq_add[bi]
            return 0

        step([j], True)
        odd = (n - 1 - j) % 2
        lax.fori_loop(0, odd, lambda _, carry: step([j + 1], False), 0)
        first_pair = j + 1 + odd
        lax.fori_loop(0, (n - first_pair) // 2,
                      lambda p, _: step([first_pair + 2 * p, first_pair + 2 * p + 1], False), 0)
        dk_ref[...] = dk_acc[:, 0:LANES].astype(BF16)
        dv_ref[...] = dv_acc[...].astype(BF16)
        dfk_ref[...] = -dk_acc[:, LANES:wide]

        @pl.when(j == n - 1)
        def _():
            dq_ref[...] = (dq_acc[:, 0:LANES] * (HEAD_DIM ** -0.5)).astype(BF16)
            dfq_ref[...] = dq_acc[:, LANES:wide]

    stat = pl.BlockSpec((1, 2, n, 1, t), lambda h, j: (h, 0, 0, 0, 0))
    blk = pl.BlockSpec((t, LANES), lambda h, j: (j, h))
    full = pl.BlockSpec((s, LANES), lambda h, j: (0, h))
    return pl.pallas_call(
        body, name="attn_bwd", grid=(4, n),
        in_specs=[full, full,
                  pl.BlockSpec((t, LANES), lambda h, j: (j, 4 + h)),
                  pl.BlockSpec((t, LANES), lambda h, j: (j, 8 + h)),
                  stat, stat, stat, blk],
        out_specs=(full, blk, blk, blk, full),
        out_shape=(jax.ShapeDtypeStruct((s, FOX_W), BF16), jax.ShapeDtypeStruct((s, FOX_W), BF16),
                   jax.ShapeDtypeStruct((s, FOX_W), BF16), jax.ShapeDtypeStruct((s, 4 * LANES), F32),
                   jax.ShapeDtypeStruct((s, 4 * LANES), F32)),
        scratch_shapes=[pltpu.VMEM((s, wide), F32), pltpu.VMEM((4, t, t), F32), pltpu.VMEM((4, t, t), F32),
                        pltpu.VMEM((4, t, t), BF16), pltpu.VMEM((4, t, t), BF16), pltpu.VMEM((t, wide), F32),
                        pltpu.VMEM((t, LANES), F32), pltpu.VMEM((2, t, 1), F32)],
        compiler_params=_cparams(("parallel", "arbitrary")))(qkv, do, qkv, qkv, lse5, dlt5, frow5, fpc)


def _ssm_block_params(a_re, a_im, log_dt, b_re, b_im):
    dt = jnp.exp(log_dt)[:, None]
    mag = jnp.exp(a_re * dt)
    ar = mag * jnp.cos(a_im * dt)
    ai = mag * jnp.sin(a_im * dt)
    den = a_re * a_re + a_im * a_im
    nr = ar - 1.0
    cr = (nr * a_re + ai * a_im) / den
    ci = (ai * a_re - nr * a_im) / den
    bbr = cr[:, :, None] * b_re - ci[:, :, None] * b_im
    bbi = cr[:, :, None] * b_im + ci[:, :, None] * b_re
    return ar, ai, bbr, bbi


def _block_diag(blocks):
    g, r, c = blocks.shape
    eye = jnp.eye(g, dtype=blocks.dtype)
    return (blocks[:, :, None, :] * eye[:, None, :, None]).reshape(g * r, g * c)


def _scan_consts(a_re, a_im, log_dt, seg, reverse):
    dt = jnp.exp(log_dt)[:, None]
    lr = (a_re * dt).reshape(1, NSTATE)
    li = (a_im * dt).reshape(1, NSTATE)
    if reverse:
        li = -li
    rows = jnp.arange(8, dtype=F32)[:, None]

    def power(k):
        mag = jnp.exp(k * lr)
        return mag * jnp.cos(k * li), mag * jnp.sin(k * li)

    tiles = list(power(1.0))
    for k in (1, 2, 4):
        keep = (rows < 8 - k) if reverse else (rows >= k)
        pr, pi_ = power(float(k * seg))
        tiles += [jnp.where(keep, pr, 0.0), jnp.where(keep, pi_, 0.0)]
    tiles += list(power(seg * ((8.0 - rows) if reverse else (rows + 1.0))))
    tiles = jnp.stack([jnp.broadcast_to(tl, (8, NSTATE)) for tl in tiles])
    steps = jnp.arange(seg, dtype=F32)[:, None]
    table = jnp.stack([jnp.broadcast_to(p[:, None, :], (seg, 8, NSTATE))
                       for p in power((seg - steps) if reverse else (steps + 1.0))])
    return tiles, table


_SCAN_W = 1024
_HALF_W = S5_W // 2
_HALF_S = NSTATE // 2


def _compact_diag(blocks_re, blocks_im):
    hg = GROUPS // 2
    return jnp.concatenate([_block_diag(b[h * hg:(h + 1) * hg]) for b in (blocks_re, blocks_im) for h in range(2)],
                           axis=1)


def _half_expand(v, w_ref, out_ref):
    for half in range(2):
        vh = v[:, half * _HALF_W:(half + 1) * _HALF_W]
        for part in range(2):
            c0 = part * NSTATE + half * _HALF_S
            out_ref[:, c0:c0 + _HALF_S] = _dot(vh, w_ref[:, c0:c0 + _HALF_S])


def _half_contract(x_ref, w_ref, half):
    out = None
    for part in range(2):
        r0 = part * NSTATE + half * _HALF_S
        term = _dot_nt(x_ref[:, r0:r0 + _HALF_S].astype(BF16), w_ref[:, r0:r0 + _HALF_S])
        out = term if out is None else out + term
    return out


def _half_outer(v, x_ref, acc_ref):
    for half in range(2):
        vh = v[:, half * _HALF_W:(half + 1) * _HALF_W]
        for part in range(2):
            c0 = part * NSTATE + half * _HALF_S
            acc_ref[:, c0:c0 + _HALF_S] += _dot_tn(vh, x_ref[:, c0:c0 + _HALF_S].astype(BF16))


def _segment_perm(tb):
    seg = tb // 8
    row = lax.broadcasted_iota(jnp.int32, (tb, tb), 0)
    col = lax.broadcasted_iota(jnp.int32, (tb, tb), 1)
    perm = (col == (row % 8) * seg + row // 8).astype(BF16)
    back = (col == (row % seg) * 8 + row // seg).astype(BF16)
    return perm, back


def _segment_ends(re, im, cf_ref, cb_ref, cr, ci, reverse):
    for n_, k in enumerate((1, 2, 4)):
        kr = cf_ref[2 + 2 * n_, :, cr]
        ki = cf_ref[3 + 2 * n_, :, cr]
        sr = pltpu.roll(re, 8 - k if reverse else k, 0)
        si = pltpu.roll(im, 8 - k if reverse else k, 0)
        re, im = re + kr * sr - ki * si, im + kr * si + ki * sr
    cbr, cbi = cb_ref[:, cr], cb_ref[:, ci]
    pr, pi_ = cf_ref[8, :, cr], cf_ref[9, :, cr]
    re, im = re + pr * cbr - pi_ * cbi, im + pr * cbi + pi_ * cbr
    edge = lax.broadcasted_iota(jnp.int32, re.shape, 0) == (7 if reverse else 0)
    in_r = jnp.where(edge, cbr, pltpu.roll(re, 7 if reverse else 1, 0))
    in_i = jnp.where(edge, cbi, pltpu.roll(im, 7 if reverse else 1, 0))
    out = slice(0, 1) if reverse else slice(7, 8)
    cb_ref[:, cr] = jnp.broadcast_to(re[out, :], re.shape)
    cb_ref[:, ci] = jnp.broadcast_to(im[out, :], im.shape)
    return in_r, in_i


def _ssm_fwd(rest, bd, cd, consts, table):
    s = rest.shape[0]
    tb = min(TB_SSM, s)
    seg = tb // 8
    ns2 = 2 * NSTATE

    def body(u_ref, bd_ref, cd_ref, cf_ref, tab_ref, y_ref, x_ref, cb_ref):
        @pl.when(pl.program_id(0) == 0)
        def _():
            cb_ref[...] = jnp.zeros_like(cb_ref)

        perm, back = _segment_perm(tb)
        _half_expand(_dot(perm, u_ref[...].astype(BF16)).astype(BF16), bd_ref, x_ref)
        for cc in range(NSTATE // _SCAN_W):
            cr = pl.ds(cc * _SCAN_W, _SCAN_W)
            ci = pl.ds(NSTATE + cc * _SCAN_W, _SCAN_W)
            ar, ai = cf_ref[0, :, cr], cf_ref[1, :, cr]

            def local(i, carry, cr=cr, ci=ci, ar=ar, ai=ai):
                re, im = carry
                rows = pl.ds(pl.multiple_of(i * 8, 8), 8)
                re, im = ar * re - ai * im + x_ref[rows, cr], ar * im + ai * re + x_ref[rows, ci]
                x_ref[rows, cr] = re
                x_ref[rows, ci] = im
                return re, im

            zero = jnp.zeros((8, _SCAN_W), F32)
            re, im = lax.fori_loop(0, seg, local, (zero, zero))
            in_r, in_i = _segment_ends(re, im, cf_ref, cb_ref, cr, ci, False)

            def fix(i, _, cr=cr, ci=ci, in_r=in_r, in_i=in_i):
                rows = pl.ds(pl.multiple_of(i * 8, 8), 8)
                tr, ti = tab_ref[0, i, :, cr], tab_ref[1, i, :, cr]
                x_ref[rows, cr] += tr * in_r - ti * in_i
                x_ref[rows, ci] += tr * in_i + ti * in_r
                return 0

            lax.fori_loop(0, seg, fix, 0)
        y_p = jnp.concatenate([_half_contract(x_ref, cd_ref, half) for half in range(2)], axis=1)
        y_ref[...] = _dot_sel(back, y_p, terms=2)

    return pl.pallas_call(
        body, name="ssm_fwd", grid=(s // tb,),
        in_specs=[pl.BlockSpec((tb, S5_W), lambda i: (i, R_U // S5_W)), _const((_HALF_W, ns2)), _const((_HALF_W, ns2)),
                  _const((10, 8, NSTATE)), _const((2, seg, 8, NSTATE))],
        out_specs=(pl.BlockSpec((tb, S5_W), lambda i: (i, 0)), pl.BlockSpec((tb, ns2), lambda i: (i, 0))),
        out_shape=(jax.ShapeDtypeStruct((s, S5_W), F32), jax.ShapeDtypeStruct((s, ns2), F32)),
        scratch_shapes=[pltpu.VMEM((8, ns2), F32)],
        compiler_params=_cparams(("arbitrary",)))(rest, bd, cd, consts, table)


def _ssm_bwd(dys, xs, rest, bd, cd, consts, table, dskip):
    s = dys.shape[0]
    tb = min(TB_SSM, s)
    seg = tb // 8
    nb = s // tb
    ns2 = 2 * NSTATE

    def body(dy_ref, x_ref, u_ref, bd_ref, cd_ref, cf_ref, tab_ref, dsk_ref, du_ref, gb_ref, gc_ref, da_ref,
             g_ref, cb_ref, acc_b, acc_c):
        step = pl.program_id(0)

        @pl.when(step == 0)
        def _():
            cb_ref[...] = jnp.zeros_like(cb_ref)
            acc_b[...] = jnp.zeros_like(acc_b)
            acc_c[...] = jnp.zeros_like(acc_c)
            da_ref[...] = jnp.zeros_like(da_ref)

        perm, back = _segment_perm(tb)
        dy = dy_ref[...]
        dy_p = _dot(perm, dy.astype(BF16)).astype(BF16)
        u_p = _dot(perm, u_ref[...].astype(BF16)).astype(BF16)
        _half_expand(dy_p, cd_ref, g_ref)
        for cc in range(NSTATE // _SCAN_W):
            cr = pl.ds(cc * _SCAN_W, _SCAN_W)
            ci = pl.ds(NSTATE + cc * _SCAN_W, _SCAN_W)
            ar, ai = cf_ref[0, :, cr], cf_ref[1, :, cr]

            def local(ii, carry, cr=cr, ci=ci, ar=ar, ai=ai):
                re, im = carry
                rows = pl.ds(pl.multiple_of((seg - 1 - ii) * 8, 8), 8)
                re, im = ar * re - ai * im + g_ref[rows, cr], ar * im + ai * re + g_ref[rows, ci]
                g_ref[rows, cr] = re
                g_ref[rows, ci] = im
                return re, im

            zero = jnp.zeros((8, _SCAN_W), F32)
            re, im = lax.fori_loop(0, seg, local, (zero, zero))
            in_r, in_i = _segment_ends(re, im, cf_ref, cb_ref, cr, ci, True)

            def fix(ii, carry, cr=cr, ci=ci, in_r=in_r, in_i=in_i):
                nr, ni, acr, aci = carry
                i = seg - 1 - ii
                rows = pl.ds(pl.multiple_of(i * 8, 8), 8)
                tr, ti = tab_ref[0, i, :, cr], tab_ref[1, i, :, cr]
                gr = g_ref[rows, cr] + tr * in_r - ti * in_i
                gi = g_ref[rows, ci] + tr * in_i + ti * in_r
                g_ref[rows, cr] = gr
                g_ref[rows, ci] = gi
                xr, xi = x_ref[rows, cr], x_ref[rows, ci]
                return gr, gi, acr + nr * xr + ni * xi, aci + ni * xr - nr * xi

            _, _, acr, aci = lax.fori_loop(0, seg, fix, (in_r, in_i, zero, zero))
            da_ref[:, cr] += acr
            da_ref[:, ci] += aci
        du_p = jnp.concatenate([_half_contract(g_ref, bd_ref, half) for half in range(2)], axis=1)
        du_ref[...] = (_dot_sel(back, du_p, terms=2) + dy * dsk_ref[...]).astype(BF16)
        _half_outer(u_p, g_ref, acc_b)
        _half_outer(dy_p, x_ref, acc_c)

        @pl.when(step == nb - 1)
        def _():
            for g in range(GROUPS):
                src = slice((g % (GROUPS // 2)) * GCH, (g % (GROUPS // 2) + 1) * GCH)
                dst = slice(g * GCH, (g + 1) * GCH)
                for part in range(2):
                    cols = slice(part * NSTATE + g * STATE, part * NSTATE + (g + 1) * STATE)
                    gb_ref[dst, part * STATE:(part + 1) * STATE] = acc_b[src, cols]
                    gc_ref[dst, part * STATE:(part + 1) * STATE] = acc_c[src, cols]

    rev = lambda i: (nb - 1 - i, 0)
    small = pl.BlockSpec((S5_W, 2 * STATE), lambda i: (0, 0))
    return pl.pallas_call(
        body, name="ssm_bwd", grid=(nb,),
        in_specs=[pl.BlockSpec((tb, S5_W), rev), pl.BlockSpec((tb, ns2), rev),
                  pl.BlockSpec((tb, S5_W), lambda i: (nb - 1 - i, R_U // S5_W)),
                  _const((_HALF_W, ns2)), _const((_HALF_W, ns2)), _const((10, 8, NSTATE)), _const((2, seg, 8, NSTATE)),
                  _const((1, S5_W))],
        out_specs=(pl.BlockSpec((tb, S5_W), rev), small, small, pl.BlockSpec((8, ns2), lambda i: (0, 0))),
        out_shape=(jax.ShapeDtypeStruct((s, S5_W), BF16), jax.ShapeDtypeStruct((S5_W, 2 * STATE), F32),
                   jax.ShapeDtypeStruct((S5_W, 2 * STATE), F32), jax.ShapeDtypeStruct((8, ns2), F32)),
        scratch_shapes=[pltpu.VMEM((tb, ns2), F32), pltpu.VMEM((8, ns2), F32),
                        pltpu.VMEM((_HALF_W, ns2), F32), pltpu.VMEM((_HALF_W, ns2), F32)],
        compiler_params=_cparams(("arbitrary",)))(dys, xs, rest, bd, cd, consts, table, dskip)


_GELU_C = math.sqrt(2.0 / math.pi)
_GELU_A = 0.044715


def _mid(o, rest, ys0, x, tgt, w, vec, hsel):
    s = o.shape[0]
    tm = min(TM, s)
    nsteps = s // tm
    half = FOX_W

    def body(o_ref, ga_ref, gb_ref, za_ref, u_ref, zb_ref, ys0_ref, x_ref, t_ref,
             wglu_ref, wua_ref, wub_ref, wout_ref, vec_ref, hsel_ref,
             dx2_ref, dga_ref, dgb_ref, do_ref, dza_ref, dzb_ref, dys_ref, dlt_ref,
             gout_hbm, gua_hbm, gub_hbm, gglu_hbm, vout_ref,
             a_out, a_ua, a_ub, a_glu):
        step = pl.program_id(0)

        @pl.when(step == 0)
        def _():
            a_out[...] = jnp.zeros_like(a_out)
            a_ua[...] = jnp.zeros_like(a_ua)
            a_ub[...] = jnp.zeros_like(a_ub)
            a_glu[...] = jnp.zeros_like(a_glu)
            vout_ref[...] = jnp.zeros_like(vout_ref)

        gate = vec_ref[0:1, :]
        gfin = vec_ref[1:2, :]
        dsk = vec_ref[2:3, 0:half]
        bglu = vec_ref[2:3, half:2 * half]

        o_v = o_ref[...]
        za = za_ref[...]
        sza = _sigmoid(za)
        silu_za = za * sza
        ya_b = (o_v * silu_za).astype(BF16)
        u_v = u_ref[...]
        ys = ys0_ref[...] + dsk * u_v
        inner = _GELU_C * (ys + _GELU_A * ys * ys * ys)
        th = jnp.tanh(inner)
        yg = 0.5 * ys * (1.0 + th)
        yg_b = yg.astype(BF16)
        st = _sigmoid(_dot(yg_b, wglu_ref[...]) + bglu)
        yb1 = yg * st
        zb = zb_ref[...]
        szb = _sigmoid(zb)
        silu_zb = zb * szb
        yb_b = (yb1 * silu_zb).astype(BF16)
        ua = _dot(ya_b, wua_ref[...])
        ub = _dot(yb_b, wub_ref[...])
        sga = _sigmoid(ga_ref[...])
        sgb = _sigmoid(gb_ref[...])
        merged_b = (sga * ua + sgb * ub).astype(BF16)
        mo = _dot(merged_b, wout_ref[...])
        x2 = x_ref[...] + gate * mo
        r2 = lax.rsqrt(jnp.mean(x2 * x2, axis=-1, keepdims=True) + EPS)
        x2n = x2 * r2
        diff = x2n * gfin - t_ref[...]
        loss = 0.5 * jnp.sum(jnp.mean(diff * diff, axis=-1, keepdims=True), axis=0, keepdims=True)
        dy = diff * (1.0 / D_MODEL)
        dx2n = dy * gfin
        dx2 = r2 * (dx2n - x2n * jnp.mean(dx2n * x2n, axis=-1, keepdims=True))
        dx2_ref[...] = dx2
        vout_ref[0:1, :] += jnp.sum(dy * x2n, axis=0, keepdims=True)
        vout_ref[1:2, :] += jnp.sum(dx2 * mo, axis=0, keepdims=True)
        vout_ref[3:4, :] += jnp.broadcast_to(loss, (1, D_MODEL))
        dmo_b = (dx2 * gate).astype(BF16)
        dmerged = _dot_nt(dmo_b, wout_ref[...])
        a_out[...] += _dot_tn(merged_b, dmo_b)
        dua_b = (dmerged * sga).astype(BF16)
        dub_b = (dmerged * sgb).astype(BF16)
        dga_ref[...] = (dmerged * ua * sga * (1.0 - sga)).astype(BF16)
        dgb_ref[...] = (dmerged * ub * sgb * (1.0 - sgb)).astype(BF16)
        dya = _dot_nt(dua_b, wua_ref[...])
        dyb = _dot_nt(dub_b, wub_ref[...])
        a_ua[...] += _dot_tn(ya_b, dua_b)
        a_ub[...] += _dot_tn(yb_b, dub_b)
        do_b = (dya * silu_za).astype(BF16)
        do_ref[...] = do_b
        dza_ref[...] = (dya * o_v * (sza * (1.0 + za * (1.0 - sza)))).astype(BF16)
        hsel = hsel_ref[...].astype(BF16)
        dlt_ref[...] = sum(_dot_nt(hsel, part) for part in _split3(do_b.astype(F32) * o_v))
        dyb1 = dyb * silu_zb
        dzb_ref[...] = (dyb * yb1 * (szb * (1.0 + zb * (1.0 - szb)))).astype(BF16)
        dt = dyb1 * yg * st * (1.0 - st)
        dt_b = dt.astype(BF16)
        dyg = dyb1 * st + _dot_nt(dt_b, wglu_ref[...])
        a_glu[...] += _dot_tn(yg_b, dt_b)
        dgelu = 0.5 * (1.0 + th) + 0.5 * ys * (1.0 - th * th) * _GELU_C * (1.0 + 3.0 * _GELU_A * ys * ys)
        dys = dyg * dgelu
        dys_ref[...] = dys
        vout_ref[2:3, 0:half] += jnp.sum(dys * u_v, axis=0, keepdims=True)
        vout_ref[2:3, half:2 * half] += jnp.sum(dt, axis=0, keepdims=True)

        @pl.when(step == nsteps - 1)
        def _():
            pltpu.sync_copy(a_out, gout_hbm)
            pltpu.sync_copy(a_ua, gua_hbm)
            pltpu.sync_copy(a_ub, gub_hbm)
            pltpu.sync_copy(a_glu, gglu_hbm)

    def rows(width, col=0):
        return pl.BlockSpec((tm, width), lambda i, col=col: (i, col))

    anyspace = pl.BlockSpec(memory_space=pl.ANY)
    wshapes = [(S5_W, S5_W), (FOX_W, D_MODEL), (S5_W, D_MODEL), (D_MODEL, D_MODEL)]
    return pl.pallas_call(
        body, name="mid", grid=(nsteps,),
        in_specs=[rows(FOX_W), rows(D_MODEL, R_GA // D_MODEL), rows(D_MODEL, R_GB // D_MODEL),
                  rows(FOX_W, R_ZA // FOX_W), rows(S5_W, R_U // S5_W), rows(S5_W, R_ZB // S5_W),
                  rows(S5_W), rows(D_MODEL), rows(D_MODEL)]
                 + [_const(sh) for sh in wshapes]
                 + [_const((8, D_MODEL)), _const((HEADS, FOX_W))],
        out_specs=(rows(D_MODEL), rows(D_MODEL), rows(D_MODEL), rows(FOX_W), rows(FOX_W), rows(S5_W), rows(S5_W),
                   pl.BlockSpec((HEADS, tm), lambda i: (0, i)),
                   anyspace, anyspace, anyspace, anyspace, pl.BlockSpec((8, D_MODEL), lambda i: (0, 0))),
        out_shape=(jax.ShapeDtypeStruct((s, D_MODEL), F32), jax.ShapeDtypeStruct((s, D_MODEL), BF16),
                   jax.ShapeDtypeStruct((s, D_MODEL), BF16), jax.ShapeDtypeStruct((s, FOX_W), BF16),
                   jax.ShapeDtypeStruct((s, FOX_W), BF16), jax.ShapeDtypeStruct((s, S5_W), BF16),
                   jax.ShapeDtypeStruct((s, S5_W), F32), jax.ShapeDtypeStruct((HEADS, s), F32),
                   jax.ShapeDtypeStruct((D_MODEL, D_MODEL), F32), jax.ShapeDtypeStruct((FOX_W, D_MODEL), F32),
                   jax.ShapeDtypeStruct((S5_W, D_MODEL), F32), jax.ShapeDtypeStruct((S5_W, S5_W), F32),
                   jax.ShapeDtypeStruct((8, D_MODEL), F32)),
        scratch_shapes=[pltpu.VMEM((D_MODEL, D_MODEL), F32), pltpu.VMEM((FOX_W, D_MODEL), F32),
                        pltpu.VMEM((S5_W, D_MODEL), F32), pltpu.VMEM((S5_W, S5_W), F32)],
        compiler_params=_cparams(("arbitrary",)),
    )(o, rest, rest, rest, rest, rest, ys0, x, tgt, *w, vec, hsel)


def _dh(dq, dk, dv, dga, dgb, dza, du, dzb, df, wqkv_t, wrest_t, x, dx2, gs, scatter_srcs):
    s = x.shape[0]
    tm = min(TM_PROJ, s)
    nsteps = s // tm
    na = len(scatter_srcs)

    def body(dq_ref, dk_ref, dv_ref, dga_ref, dgb_ref, dza_ref, du_ref, dzb_ref, df_ref, wq_ref, wr_ref,
             x_ref, dx2_ref, gs_ref, *rest_refs):
        src_refs = rest_refs[:na]
        gx_ref, vout_ref = rest_refs[na:na + 2]
        out_refs = rest_refs[na + 2:2 * na + 2]
        send_sems, recv_sems = rest_refs[2 * na + 2:]
        step = pl.program_id(0)
        cx, cy, cc = lax.axis_index("x"), lax.axis_index("y"), lax.axis_index("c")
        peers = [(1 - cx, cy), (cx, 1 - cy), (1 - cx, 1 - cy)]

        def copy(a, k, px, py, slot):
            return pltpu.make_async_remote_copy(
                src_ref=src_refs[a].at[2 * px + py], dst_ref=out_refs[a].at[slot],
                send_sem=send_sems.at[a * 3 + k], recv_sem=recv_sems.at[a * 3 + k],
                device_id=(px, py, cc), device_id_type=MESH)

        @pl.when(step == 0)
        def _():
            vout_ref[...] = jnp.zeros_like(vout_ref)
            for a in range(na):
                for k, (px, py) in enumerate(peers):
                    copy(a, k, px, py, 2 * cx + cy).start()

        dh = _dot(dq_ref[...], wq_ref[0:512, :])
        dh += _dot(dk_ref[...], wq_ref[512:1024, :])
        dh += _dot(dv_ref[...], wq_ref[1024:1536, :])
        dh += _dot(dga_ref[...], wr_ref[R_GA:R_GB, :])
        dh += _dot(dgb_ref[...], wr_ref[R_GB:R_ZA, :])
        dh += _dot(dza_ref[...], wr_ref[R_ZA:R_U, :])
        dh += _dot(du_ref[...], wr_ref[R_U:R_ZB, :])
        dh += _dot(dzb_ref[...], wr_ref[R_ZB:R_F, :])
        dh += _dot(df_ref[...], wr_ref[R_F:REST_W, :])
        xv = x_ref[...]
        r = lax.rsqrt(jnp.mean(xv * xv, axis=-1, keepdims=True) + EPS)
        xn = xv * r
        dxn = dh * gs_ref[...]
        gx_ref[...] = dx2_ref[...] + r * (dxn - xn * jnp.mean(dxn * xn, axis=-1, keepdims=True))
        vout_ref[0:1, :] += jnp.sum(dh * xn, axis=0, keepdims=True)
        vout_ref[1:2, :] += jnp.sum(dh, axis=0, keepdims=True)

        @pl.when(step == nsteps - 1)
        def _():
            for a in range(na):
                for k, (px, py) in enumerate(peers):
                    copy(a, k, px, py, 2 * px + py).wait_recv()
            for a in range(na):
                for k, (px, py) in enumerate(peers):
                    copy(a, k, px, py, 2 * cx + cy).wait_send()

    def rows(width):
        return pl.BlockSpec((tm, width), lambda i: (i, 0))

    anyspace = pl.BlockSpec(memory_space=pl.ANY)
    return pl.pallas_call(
        body, name="dh", grid=(nsteps,),
        in_specs=[rows(512), rows(512), rows(512), rows(1024), rows(1024), rows(512), rows(512), rows(512), rows(128),
                  _const((1536, D_MODEL)), _const((REST_W, D_MODEL)), rows(D_MODEL), rows(D_MODEL), _const((1, D_MODEL))]
                 + [anyspace] * na,
        out_specs=(rows(D_MODEL), pl.BlockSpec((8, D_MODEL), lambda i: (0, 0))) + (anyspace,) * na,
        out_shape=(jax.ShapeDtypeStruct((s, D_MODEL), F32), jax.ShapeDtypeStruct((8, D_MODEL), F32))
                  + tuple(jax.ShapeDtypeStruct(a.shape, a.dtype) for a in scatter_srcs),
        scratch_shapes=[pltpu.SemaphoreType.DMA((3 * na,)), pltpu.SemaphoreType.DMA((3 * na,))],
        compiler_params=_cparams(("arbitrary",)),
    )(dq, dk, dv, dga, dgb, dza, du, dzb, df, wqkv_t, wrest_t, x, dx2, gs, *scatter_srcs)


def _row_block(rows, mult=8, cap=512):
    if rows <= mult:
        return rows
    padded = -(-rows // mult) * mult
    for cand in range(min(cap, padded) // mult * mult, 0, -mult):
        if padded % cand == 0:
            return cand
    return padded


def _sum4(parts, name):
    rows, cols = parts.shape[1:]
    br = _row_block(rows, 16, 1024)

    def body(p_ref, o_ref):
        acc = p_ref[0].astype(F32)
        for k in range(1, 4):
            acc = acc + p_ref[k].astype(F32)
        o_ref[...] = acc

    return pl.pallas_call(
        body, name=name, grid=(pl.cdiv(rows, br),),
        in_specs=[pl.BlockSpec((4, br, cols), lambda i: (0, i, 0))],
        out_specs=pl.BlockSpec((br, cols), lambda i: (i, 0)),
        out_shape=jax.ShapeDtypeStruct((rows, cols), F32), compiler_params=_cparams(("parallel",)))(parts)


def _pair_add(a, b, name):
    shape = a.shape
    a, b = a.reshape(-1, shape[-1]), b.reshape(-1, shape[-1])
    rows, cols = a.shape
    br = _row_block(rows, 16, 1024)

    def body(a_ref, b_ref, o_ref):
        o_ref[...] = (a_ref[...].astype(F32) + b_ref[...].astype(F32)).astype(BF16)

    spec = pl.BlockSpec((br, cols), lambda i: (i, 0))
    return pl.pallas_call(
        body, name=name, grid=(pl.cdiv(rows, br),), in_specs=[spec, spec], out_specs=spec,
        out_shape=jax.ShapeDtypeStruct((rows, cols), BF16), compiler_params=_cparams(("parallel",)))(a, b).reshape(shape)


def _adamw(w, g, m, v, name):
    rows, cols = w.shape
    br = _row_block(rows)

    def body(w_ref, g_ref, m_ref, v_ref, d_ref, nm_ref, nv_ref):
        gv = g_ref[...]
        nm = ADAM_B1 * m_ref[...] + (1.0 - ADAM_B1) * gv
        nv = ADAM_B2 * v_ref[...] + (1.0 - ADAM_B2) * (gv * gv)
        m_hat = nm / (1.0 - ADAM_B1 ** ADAM_STEP)
        v_hat = nv / (1.0 - ADAM_B2 ** ADAM_STEP)
        d_ref[...] = -ADAM_LR * (m_hat / (jnp.sqrt(v_hat) + ADAM_EPS) + ADAM_WD * w_ref[...])
        nm_ref[...] = nm
        nv_ref[...] = nv

    spec = pl.BlockSpec((br, cols), lambda i: (i, 0))
    shape = jax.ShapeDtypeStruct((rows, cols), F32)
    return pl.pallas_call(
        body, name=name, grid=(pl.cdiv(rows, br),), in_specs=[spec] * 4, out_specs=(spec,) * 3,
        out_shape=(shape,) * 3, compiler_params=_cparams(("parallel",)))(w, g, m, v)


def _pack(parts, row_multiple=8):
    flat = []
    for p in parts:
        v = p.reshape(-1).astype(F32)
        pad = (-v.shape[0]) % LANES
        if pad:
            v = jnp.concatenate([v, jnp.zeros((pad,), F32)])
        flat.append(v)
    v = jnp.concatenate(flat)
    rows = v.shape[0] // LANES
    pad_rows = (-rows) % row_multiple
    if pad_rows:
        v = jnp.concatenate([v, jnp.zeros((pad_rows * LANES,), F32)])
    return v.reshape(-1, LANES)


def _unpack(packed, shapes):
    lead = packed.shape[:-2]
    flat = packed.reshape(lead + (-1,))
    out, off = [], 0
    for sh in shapes:
        size = math.prod(sh)
        out.append(flat[..., off:off + size].reshape(lead + tuple(sh)))
        off += size + (-size) % LANES
    return out


def kernel(x, c, w_ada, b_ada, g_norm, w_in, b_f, a_re, a_im, log_dt, b_re, b_im, c_re, c_im, d_skip, w_glu, b_glu, w_up_a, w_up_b, w_out, g_final, loss_target, m_w_ada, m_b_ada, m_g_norm, m_w_in, m_b_f, m_a_re, m_a_im, m_log_dt, m_b_re, m_b_im, m_c_re, m_c_im, m_d_skip, m_w_glu, m_b_glu, m_w_up_a, m_w_up_b, m_w_out, m_g_final, v_w_ada, v_b_ada, v_g_norm, v_w_in, v_b_f, v_a_re, v_a_im, v_log_dt, v_b_re, v_b_im, v_c_re, v_c_im, v_d_skip, v_w_glu, v_b_glu, v_w_up_a, v_w_up_b, v_w_out, v_g_final):
    xi, yi, ci = lax.axis_index("x"), lax.axis_index("y"), lax.axis_index("c")
    chip = 2 * xi + yi
    me = 4 * xi + 2 * yi + ci
    s = x.shape[1]
    x2d = x[0]
    tgt = loss_target[0]
    n_att = s // min(T_ATT, s)
    t_att = min(T_ATT, s)

    c_all, _ = _allgather8(c.reshape(8, LANES), "gather_c")
    c_all = c_all.reshape(8, D_MODEL)
    ncol = w_ada.shape[2]
    b_cols = lax.dynamic_slice_in_dim(b_ada, chip * ncol, ncol, axis=1)
    mod_cols = _mod_cols(c_all, w_ada[0], b_cols)
    mod_all, _ = _allgather8(mod_cols.reshape(-1, LANES), "gather_mod")
    mod_all = mod_all.reshape(4, 2, 8, ncol)[:, 0]
    mod_me = lax.dynamic_index_in_dim(mod_all, me, axis=1, keepdims=False).reshape(1, 3 * D_MODEL)
    shift, scale, gate = mod_me[:, :D_MODEL], mod_me[:, D_MODEL:2 * D_MODEL], mod_me[:, 2 * D_MODEL:]
    gs = g_norm * (1.0 + scale)

    nshard = w_in.shape[2]
    w_in_t, m_in_t, v_in_t = (jnp.swapaxes(a[0], 0, 1) for a in (w_in, m_w_in, v_w_in))
    wt_pack = jnp.pad(w_in_t.astype(BF16), ((0, SHARD_ROWS - nshard), (0, 0)))
    misc_shapes = [w_glu.shape[1:], w_up_a.shape[1:], w_up_b.shape[1:], w_out.shape[1:]]
    misc_pack = jnp.concatenate([w.reshape(-1) for w in (w_glu, w_up_a, w_up_b, w_out)]).astype(BF16).reshape(-1, LANES)
    def halves(a):
        return a.reshape((2, a.shape[0] // 2) + a.shape[1:])

    wt_all, misc_all = _gather_shards([halves(wt_pack), halves(misc_pack)], "gather_weights")
    wt_all = lax.dynamic_update_index_in_dim(wt_all, halves(wt_pack), chip, 0).reshape((4,) + wt_pack.shape)
    misc_all = lax.dynamic_update_index_in_dim(misc_all, halves(misc_pack), chip, 0).reshape((4,) + misc_pack.shape)
    p_glu, p_ua, p_ub, p_out = _unpack(misc_all, misc_shapes)

    def w_rows(lo, hi):
        out = []
        for j in range(4):
            a, b = max(lo, j * nshard), min(hi, (j + 1) * nshard)
            if a < b:
                out.append(wt_all[j, a - j * nshard:b - j * nshard])
        return out

    wqkv_t = jnp.concatenate(w_rows(O_Q, O_F), axis=0)
    wrest_t = jnp.concatenate(w_rows(O_GA, O_GB) + w_rows(O_GB, O_END) + w_rows(O_ZA, O_U) + w_rows(O_U, O_ZB)
                              + w_rows(O_ZB, O_GA) + w_rows(O_F, O_ZA)
                              + [jnp.zeros((REST_W - R_F - HEADS, D_MODEL), BF16)], axis=0)
    wmid = (p_glu.reshape(S5_W, S5_W), jnp.concatenate([p_ua[j] for j in range(4)], axis=1),
            jnp.concatenate([p_ub[j] for j in range(4)], axis=1), p_out.reshape(D_MODEL, D_MODEL))

    h, qkv, rest = _prenorm_proj(x2d, gs, shift, wqkv_t, wrest_t)
    bf128 = jnp.pad(b_f, ((0, 0), (0, LANES - HEADS)))
    selp = _head_pair_selector()
    fpc, f_t = _fcum(rest, bf128, selp)
    frow5 = f_t.reshape(4, 2, n_att, 1, t_att)
    o, lse_pc = _attn_fwd(qkv, frow5, fpc)

    abar_r, abar_i, bb_r, bb_i = _ssm_block_params(a_re[0], a_im[0], log_dt[0], b_re[0], b_im[0])
    bb_rt, bb_it = jnp.swapaxes(bb_r, 1, 2).astype(BF16), jnp.swapaxes(bb_i, 1, 2).astype(BF16)
    cr_b, ci_b = c_re[0].astype(BF16), (-c_im[0]).astype(BF16)
    bd_c, cd_c = _compact_diag(bb_rt, bb_it), _compact_diag(cr_b, ci_b)
    seg = min(TB_SSM, s) // 8
    ys0, xs = _ssm_fwd(rest, bd_c, cd_c, *_scan_consts(a_re[0], a_im[0], log_dt[0], seg, False))

    vec = jnp.concatenate([gate, g_final.reshape(1, D_MODEL), jnp.concatenate([d_skip, b_glu], axis=1),
                           jnp.zeros((5, D_MODEL), F32)], axis=0)
    hsel = jnp.repeat(jnp.eye(HEADS, dtype=F32), HEAD_DIM, axis=1)
    (dx2, dga, dgb, do, dza, dzb, dys, dlt_t, g_out, g_ua, g_ub, g_glu, vmid) = _mid(
        o, rest, ys0, x2d, tgt, wmid, vec, hsel)

    lse_t = jnp.transpose(lse_pc.reshape(s, 4 // ATT_PAIRS, LANES)[:, :, :2 * ATT_PAIRS], (1, 2, 0))
    lse5 = lse_t.reshape(4, 2, n_att, 1, t_att)
    dlt5 = dlt_t.reshape(4, 2, n_att, 1, t_att)
    dq, dk, dv, dfk, dfq = _attn_bwd(qkv, do, lse5, dlt5, frow5, fpc)
    du, g_bd, g_cdt, da8 = _ssm_bwd(dys, xs, rest, bd_c, cd_c, *_scan_consts(a_re[0], a_im[0], log_dt[0], seg, True),
                                    d_skip)
    df, dbf8 = _dfcum(dfk, dfq, rest, bf128, selp.T)

    gq, gk, gv, gga, ggb, gza, gu, gzb, gf = _grad_w_rows(h, [dq, dk, dv, dga, dgb, dza, du, dzb, df])
    g_in_t = jnp.concatenate([gq, gk, gv, gf[:HEADS], gza, gu, gzb, gga, ggb], axis=0)

    def shard_cols(g, j):
        n = g.shape[1] // 4
        return g[:, j * n:(j + 1) * n]

    def shard_rows(g, j):
        n = g.shape[0] // 4
        return g[j * n:(j + 1) * n]

    def halves4(a):
        return a.reshape((4, 2, a.shape[1] // 2) + a.shape[2:])

    gt_pack = halves4(jnp.stack([
        jnp.pad(g_in_t[j * nshard:(j + 1) * nshard].astype(BF16), ((0, SHARD_ROWS - nshard), (0, 0)))
        for j in range(4)]))
    gm_pack = halves4(jnp.stack([
        jnp.concatenate([shard_rows(g_glu, j).reshape(-1), shard_cols(g_ua, j).reshape(-1),
                         shard_cols(g_ub, j).reshape(-1), shard_rows(g_out, j).reshape(-1)]).astype(BF16)
        .reshape(-1, LANES) for j in range(4)]))
    recv_in, recv_misc = _swap_sibling([gt_pack, gm_pack], "pair_swap_weight_grads", other_half=True)
    own_in = lax.dynamic_index_in_dim(gt_pack, ci, axis=1, keepdims=False)
    own_misc = lax.dynamic_index_in_dim(gm_pack, ci, axis=1, keepdims=False)
    pair_in = _pair_add(own_in, recv_in, "pair_add_w_in")
    pair_misc = _pair_add(own_misc, recv_misc, "pair_add_misc")

    grad_x, vdh, parts_in, parts_misc = _dh(dq, dk, dv, dga, dgb, dza, du, dzb, df, wqkv_t, wrest_t, x2d, dx2, gs,
                                            [pair_in, pair_misc])
    parts_in = lax.dynamic_update_slice_in_dim(parts_in, lax.dynamic_slice_in_dim(pair_in, chip, 1, 0), chip, 0)
    parts_misc = lax.dynamic_update_slice_in_dim(parts_misc, lax.dynamic_slice_in_dim(pair_misc, chip, 1, 0), chip, 0)
    half_in, half_misc = _sum4(parts_in, "sum4_w_in"), _sum4(parts_misc, "sum4_misc")
    sib_in, sib_misc = _swap_sibling([half_in, half_misc], "swap_weight_grads")

    def both_halves(mine, theirs):
        return jnp.concatenate([jnp.where(ci == 0, mine, theirs), jnp.where(ci == 0, theirs, mine)], axis=0)

    tot_in, tot_misc = both_halves(half_in, sib_in), both_halves(half_misc, sib_misc)
    g_glu_s, g_ua_s, g_ub_s, g_out_s = _unpack(tot_misc, misc_shapes)

    dgs, dshift = vdh[0:1], vdh[1:2]
    dmod = jnp.concatenate([dshift, dgs * g_norm, vmid[1:2]], axis=1)
    da = jnp.sum(da8, axis=0)
    g_bd = g_bd.reshape(GROUPS, GCH, 2 * STATE)
    g_cdt = g_cdt.reshape(GROUPS, GCH, 2 * STATE)
    g_bbr = jnp.swapaxes(g_bd[:, :, :STATE], 1, 2)
    g_bbi = jnp.swapaxes(g_bd[:, :, STATE:], 1, 2)
    g_cre = g_cdt[:, :, :STATE]
    g_cim = -g_cdt[:, :, STATE:]
    small_shapes = [(1,), (3 * D_MODEL,), (D_MODEL,), (HEADS,), (GROUPS, STATE), (GROUPS, STATE),
                    (GROUPS, STATE, GCH), (GROUPS, STATE, GCH), (GROUPS, GCH, STATE), (GROUPS, GCH, STATE),
                    (S5_W,), (S5_W,), (D_MODEL,)]
    small = _pack([vmid[3, 0:1], dmod, dgs * (1.0 + scale), dbf8[0, :HEADS], da[:NSTATE], da[NSTATE:],
                   g_bbr, g_bbi, g_cre, g_cim, vmid[2, :S5_W], vmid[2, S5_W:], vmid[0]])
    small_all, small_sum = _allgather8(small, "gather_small_grads")
    (loss_s, g_b_ada, g_g_norm, g_b_f, g_abr, g_abi, g_bbr_s, g_bbi_s, g_c_re, g_c_im, g_d_skip, g_b_glu,
     g_g_final) = _unpack(small_sum, small_shapes)
    loss = loss_s[0]
    dmod_all = _unpack(small_all, small_shapes)[1]
    dmod_cols = lax.dynamic_slice_in_dim(dmod_all, chip * ncol, ncol, axis=1)
    g_w_ada = _grad_w_ada(c_all, dmod_cols)
    _, ssm_vjp = jax.vjp(_ssm_block_params, a_re[0], a_im[0], log_dt[0], b_re[0], b_im[0])
    g_a_re, g_a_im, g_log_dt, g_b_re, g_b_im = ssm_vjp((g_abr, g_abi, g_bbr_s, g_bbi_s))

    def adam(name, w, g, m, v):
        shape = w.shape
        total = math.prod(shape)
        if len(shape) > 1 and shape[-1] >= LANES:
            cols = shape[-1]
        elif total % LANES == 0:
            cols = LANES
        else:
            cols = total
        two = lambda a: a.reshape(-1, cols)
        d, nm, nv = _adamw(two(w), two(g), two(m), two(v), "adamw_" + name)
        return g.reshape(shape), d.reshape(shape), nm.reshape(shape), nv.reshape(shape)

    back = lambda a: jnp.swapaxes(a, 0, 1)[None]
    d_in_t, nm_in_t, nv_in_t = _adamw(w_in_t, tot_in, m_in_t, v_in_t, "adamw_w_in")
    res_w_in = (back(tot_in[:nshard]), back(d_in_t), back(nm_in_t), back(nv_in_t))

    res = [
        adam("w_ada", w_ada, g_w_ada, m_w_ada, v_w_ada),
        adam("b_ada", b_ada, g_b_ada, m_b_ada, v_b_ada),
        adam("g_norm", g_norm, g_g_norm, m_g_norm, v_g_norm),
        res_w_in,
        adam("b_f", b_f, g_b_f, m_b_f, v_b_f),
        adam("a_re", a_re, g_a_re, m_a_re, v_a_re),
        adam("a_im", a_im, g_a_im, m_a_im, v_a_im),
        adam("log_dt", log_dt, g_log_dt, m_log_dt, v_log_dt),
        adam("b_re", b_re, g_b_re, m_b_re, v_b_re),
        adam("b_im", b_im, g_b_im, m_b_im, v_b_im),
        adam("c_re", c_re, g_c_re, m_c_re, v_c_re),
        adam("c_im", c_im, g_c_im, m_c_im, v_c_im),
        adam("d_skip", d_skip, g_d_skip, m_d_skip, v_d_skip),
        adam("w_glu", w_glu, g_glu_s, m_w_glu, v_w_glu),
        adam("b_glu", b_glu, g_b_glu, m_b_glu, v_b_glu),
        adam("w_up_a", w_up_a, g_ua_s, m_w_up_a, v_w_up_a),
        adam("w_up_b", w_up_b, g_ub_s, m_w_up_b, v_w_up_b),
        adam("w_out", w_out, g_out_s, m_w_out, v_w_out),
        adam("g_final", g_final, g_g_final, m_g_final, v_g_final),
    ]
    grads = [r[0] for r in res]
    deltas = [r[1] for r in res]
    new_m = [r[2] for r in res]
    new_v = [r[3] for r in res]
    return (loss, grad_x[None], *grads, *deltas, *new_m, *new_v)
```

```python
import math

import jax
import jax.numpy as jnp
from jax import lax
from jax.experimental import pallas as pl
from jax.experimental.pallas import tpu as pltpu

F32 = jnp.float32
BF16 = jnp.bfloat16
HI = lax.Precision.HIGHEST
MESH = pl.DeviceIdType.MESH

D_MODEL = 1024
HEADS = 8
HEAD_DIM = 64
FOX_W = 512
S5_W = 512
GROUPS = 32
STATE = 64
GCH = 16
NSTATE = GROUPS * STATE
EPS = 1e-6
NEG = -1e30

ADAM_LR = 0.001
ADAM_B1 = 0.9
ADAM_B2 = 0.999
ADAM_EPS = 1e-08
ADAM_WD = 0.01
ADAM_STEP = 10

VMEM_LIMIT = 56 * 1024 * 1024
LANES = 128

TM = 256
TM_PROJ = 512
T_ATT = 512
ATT_CHUNK = 32
ATT_PAIRS = 4
TB_SSM = 512
TK_ACC = 512
TB_CUM = 256
SHARD_ROWS = 1312

O_Q, O_K, O_V, O_F, O_ZA, O_U, O_ZB, O_GA, O_GB, O_END = 0, 512, 1024, 1536, 1544, 2056, 2568, 3080, 4104, 5128
REST_W = 3712
R_GA, R_GB, R_ZA, R_U, R_ZB, R_F = 0, 1024, 2048, 2560, 3072, 3584


def _cparams(sem=None):
    kw = dict(vmem_limit_bytes=VMEM_LIMIT)
    if sem is not None:
        kw["dimension_semantics"] = sem
    return pltpu.CompilerParams(**kw)


def _const(shape):
    nd = len(shape)
    return pl.BlockSpec(shape, lambda *_: (0,) * nd, pipeline_mode=pl.Buffered(1))


def _dot(a, b, precision=None):
    return jnp.dot(a, b, preferred_element_type=F32, precision=precision)


def _dot_nt(a, b):
    return lax.dot_general(a, b, (((1,), (1,)), ((), ())), preferred_element_type=F32)


def _dot_tn(a, b, precision=None):
    return lax.dot_general(a, b, (((0,), (0,)), ((), ())), preferred_element_type=F32, precision=precision)


def _sigmoid(z):
    return 1.0 / (1.0 + jnp.exp(-z))


def _split3(x):
    hi = x.astype(BF16)
    r1 = x - hi.astype(F32)
    mid = r1.astype(BF16)
    lo = (r1 - mid.astype(F32)).astype(BF16)
    return hi, mid, lo


def _dot_sel(sel, x, terms=3):
    s16 = sel.astype(BF16)
    return sum(_dot(s16, part) for part in _split3(x)[:terms])


def _dot_by_sel(x, sel):
    s16 = sel.astype(BF16)
    return sum(_dot(part, s16) for part in _split3(x))


def _allgather8(xs, name):
    rows = xs.shape[0]

    def body(x_ref, out_ref, sum_ref, send_sems, recv_sems, local_sem):
        x, y, c = lax.axis_index("x"), lax.axis_index("y"), lax.axis_index("c")
        me, sibling = (x, y, c), (x, y, 1 - c)
        chips = [(1 - x, y), (x, 1 - y), (1 - x, 1 - y)]

        def slot(px, py, pc):
            return out_ref.at[4 * px + 2 * py + pc]

        def copy(k, block, to, src=None):
            return pltpu.make_async_remote_copy(
                src_ref=slot(*block) if src is None else src, dst_ref=slot(*block),
                send_sem=send_sems.at[k], recv_sem=recv_sems.at[k], device_id=to, device_id_type=MESH)

        mine = pltpu.make_async_copy(x_ref, slot(*me), local_sem)
        mine.start()
        first = [copy(0, me, sibling, src=x_ref)]
        first += [copy(1 + j, me, (*chip, c), src=x_ref) for j, chip in enumerate(chips)]
        for cp in first:
            cp.start()
        passed = [copy(4 + j, (*chip, c), sibling) for j, chip in enumerate(chips)]
        for j, chip in enumerate(chips):
            copy(1 + j, (*chip, c), me).wait_recv()
            passed[j].start()
        copy(0, sibling, me).wait_recv()
        for j, chip in enumerate(chips):
            copy(4 + j, (*chip, 1 - c), me).wait_recv()
        for cp in first + passed:
            cp.wait_send()
        mine.wait()
        acc = out_ref[0]
        for d in range(1, 8):
            acc = acc + out_ref[d]
        sum_ref[...] = acc

    return pl.pallas_call(
        body, name=name,
        out_shape=(jax.ShapeDtypeStruct((8, rows, LANES), F32), jax.ShapeDtypeStruct((rows, LANES), F32)),
        in_specs=[pl.BlockSpec(memory_space=pltpu.VMEM)],
        out_specs=(pl.BlockSpec(memory_space=pltpu.VMEM), pl.BlockSpec(memory_space=pltpu.VMEM)),
        scratch_shapes=[pltpu.SemaphoreType.DMA((7,)), pltpu.SemaphoreType.DMA((7,)), pltpu.SemaphoreType.DMA],
        compiler_params=_cparams(),
    )(xs)


def _gather_shards(srcs, name):
    na = len(srcs)

    def body(*refs):
        src_refs, out_refs = refs[:na], refs[na:2 * na]
        send_sems, recv_sems = refs[2 * na:]
        x, y, c = lax.axis_index("x"), lax.axis_index("y"), lax.axis_index("c")
        sibling = (x, y, 1 - c)
        peers = [(1 - x, y), (x, 1 - y), (1 - x, 1 - y)]

        def copy(a, k, src, slot, which, to):
            return pltpu.make_async_remote_copy(
                src_ref=src, dst_ref=out_refs[a].at[slot, which],
                send_sem=send_sems.at[a * 6 + k], recv_sem=recv_sems.at[a * 6 + k],
                device_id=to, device_id_type=MESH)

        mine = 2 * x + y
        first = [copy(a, k, src_refs[a].at[c], mine, c, (px, py, c))
                 for a in range(na) for k, (px, py) in enumerate(peers)]
        for cp in first:
            cp.start()
        passed = []
        for a in range(na):
            for k, (px, py) in enumerate(peers):
                slot = 2 * px + py
                landed = out_refs[a].at[slot, c]
                copy(a, k, landed, slot, c, (px, py, c)).wait_recv()
                fwd = copy(a, 3 + k, landed, slot, c, sibling)
                fwd.start()
                passed.append(fwd)
        for a in range(na):
            for k, (px, py) in enumerate(peers):
                slot = 2 * px + py
                copy(a, 3 + k, out_refs[a].at[slot, 1 - c], slot, 1 - c, sibling).wait_recv()
        for cp in first + passed:
            cp.wait_send()

    anyspace = pl.BlockSpec(memory_space=pl.ANY)
    return pl.pallas_call(
        body, name=name,
        out_shape=tuple(jax.ShapeDtypeStruct((4,) + tuple(a.shape), a.dtype) for a in srcs),
        in_specs=[anyspace] * na, out_specs=(anyspace,) * na,
        scratch_shapes=[pltpu.SemaphoreType.DMA((6 * na,)), pltpu.SemaphoreType.DMA((6 * na,))],
        compiler_params=_cparams(),
    )(*srcs)


def _swap_sibling(srcs, name, other_half=False):
    na = len(srcs)

    def body(*refs):
        src_refs, out_refs = refs[:na], refs[na:2 * na]
        send_sems, recv_sems = refs[2 * na:]
        x, y, c = lax.axis_index("x"), lax.axis_index("y"), lax.axis_index("c")
        copies = [pltpu.make_async_remote_copy(
            src_ref=src_refs[a].at[:, 1 - c] if other_half else src_refs[a],
            dst_ref=out_refs[a], send_sem=send_sems.at[a], recv_sem=recv_sems.at[a],
            device_id=(x, y, 1 - c), device_id_type=MESH) for a in range(na)]
        for cp in copies:
            cp.start()
        for cp in copies:
            cp.wait()

    def out_of(a):
        shape = (a.shape[0],) + tuple(a.shape[2:]) if other_half else a.shape
        return jax.ShapeDtypeStruct(shape, a.dtype)

    anyspace = pl.BlockSpec(memory_space=pl.ANY)
    return pl.pallas_call(
        body, name=name, out_shape=tuple(out_of(a) for a in srcs),
        in_specs=[anyspace] * na, out_specs=(anyspace,) * na,
        scratch_shapes=[pltpu.SemaphoreType.DMA((na,)), pltpu.SemaphoreType.DMA((na,))],
        compiler_params=_cparams(),
    )(*srcs)


def _mod_cols(c_all, w, b):
    n = w.shape[1]

    def body(c_ref, w_ref, b_ref, o_ref):
        o_ref[...] = _dot(c_ref[...], w_ref[...], HI) + b_ref[...]

    return pl.pallas_call(
        body, name="mod_cols", out_shape=jax.ShapeDtypeStruct((8, n), F32),
        compiler_params=_cparams())(c_all, w, b)


def _grad_w_ada(c_all, dmod_cols):
    n = dmod_cols.shape[1]

    def body(c_ref, d_ref, o_ref):
        o_ref[...] = _dot_tn(c_ref[...], d_ref[...], HI)

    return pl.pallas_call(
        body, name="grad_w_ada", out_shape=jax.ShapeDtypeStruct((D_MODEL, n), F32),
        compiler_params=_cparams())(c_all, dmod_cols)


def _prenorm_proj(x, gs, shift, wqkv_t, wrest_t):
    s = x.shape[0]
    tm = min(TM_PROJ, s)
    nq, nr = wqkv_t.shape[0], wrest_t.shape[0]

    def body(x_ref, gs_ref, sh_ref, wq_ref, wr_ref, h_ref, qkv_ref, rest_ref):
        xv = x_ref[...]
        r = lax.rsqrt(jnp.mean(xv * xv, axis=-1, keepdims=True) + EPS)
        h = (xv * r * gs_ref[...] + sh_ref[...]).astype(BF16)
        h_ref[...] = h
        qkv_ref[...] = _dot_nt(h, wq_ref[...]).astype(BF16)
        rest_ref[...] = _dot_nt(h, wr_ref[...])

    def rows(width):
        return pl.BlockSpec((tm, width), lambda i: (i, 0))

    return pl.pallas_call(
        body, name="prenorm_proj", grid=(s // tm,),
        in_specs=[rows(D_MODEL), _const((1, D_MODEL)), _const((1, D_MODEL)), _const((nq, D_MODEL)),
                  _const((nr, D_MODEL))],
        out_specs=(rows(D_MODEL), rows(nq), rows(nr)),
        out_shape=(jax.ShapeDtypeStruct((s, D_MODEL), BF16), jax.ShapeDtypeStruct((s, nq), BF16),
                   jax.ShapeDtypeStruct((s, nr), F32)),
        compiler_params=_cparams(("parallel",)))(x, gs, shift, wqkv_t, wrest_t)


def _grad_w_rows(h, ds):
    s = h.shape[0]
    tk = min(TK_ACC, s)
    nd = len(ds)
    widths = [d.shape[1] for d in ds]

    def body(*refs):
        h_ref, d_refs = refs[0], refs[1:1 + nd]
        out_refs, accs = refs[1 + nd:1 + 2 * nd], refs[1 + 2 * nd:]
        step = pl.program_id(0)

        @pl.when(step == 0)
        def _():
            for acc in accs:
                acc[...] = jnp.zeros_like(acc)

        hv = h_ref[...]
        for d_ref, acc in zip(d_refs, accs):
            acc[...] += _dot_tn(d_ref[...], hv)

        @pl.when(step == s // tk - 1)
        def _():
            for acc, out in zip(accs, out_refs):
                pltpu.sync_copy(acc, out)

    anyspace = pl.BlockSpec(memory_space=pl.ANY)
    return pl.pallas_call(
        body, name="grad_w_in", grid=(s // tk,),
        in_specs=[pl.BlockSpec((tk, D_MODEL), lambda k: (k, 0))]
                 + [pl.BlockSpec((tk, w), lambda k: (k, 0)) for w in widths],
        out_specs=(anyspace,) * nd,
        out_shape=tuple(jax.ShapeDtypeStruct((w, D_MODEL), F32) for w in widths),
        scratch_shapes=[pltpu.VMEM((w, D_MODEL), F32) for w in widths],
        compiler_params=_cparams(("arbitrary",)))(h, *ds)


def _head_pair_selector():
    rows = jnp.arange(LANES)[:, None]
    cols = jnp.arange(4 * LANES)[None, :]
    return ((rows < HEADS) & (cols == (rows // 2) * LANES + rows % 2)).astype(F32)


BIAS_ONES = 32


def _bias_selectors():
    rows = jnp.arange(LANES)[None, :, None]
    cols = jnp.arange(LANES)[None, None, :]
    term = jnp.arange(3)[:, None, None]
    return ((rows < HEADS) & (cols == 3 * rows + term)).astype(F32)


def _fcum(rest, bf128, selp, selk):
    s = rest.shape[0]
    tb = min(TB_CUM, s)

    def body(fz_ref, bf_ref, sel_ref, selk_ref, fpc_ref, ft_ref, kb_ref, carry_ref):
        @pl.when(pl.program_id(0) == 0)
        def _():
            carry_ref[...] = jnp.zeros_like(carry_ref)

        z = fz_ref[...] + bf_ref[...]
        logf = jnp.minimum(z, 0.0) - jnp.log(1.0 + jnp.exp(-jnp.abs(z)))
        r = lax.broadcasted_iota(jnp.int32, (tb, tb), 0)
        c = lax.broadcasted_iota(jnp.int32, (tb, tb), 1)
        tri = (c <= r).astype(F32)
        f = _dot_sel(tri, logf) + carry_ref[0:1, :]
        carry_ref[0:1, :] = f[tb - 1:tb, :]
        fpc_ref[...] = _dot_by_sel(f, sel_ref[...])
        ft_ref[...] = jnp.transpose(f)[0:HEADS, :]
        lane = lax.broadcasted_iota(jnp.int32, (tb, LANES), 1)
        ones = ((lane >= BIAS_ONES) & (lane < BIAS_ONES + 3 * HEADS)).astype(F32)
        terms = sum(_dot(part, selk_ref[j].astype(BF16)) for j, part in enumerate(_split3(-f)))
        kb_ref[...] = (terms + ones).astype(BF16)

    return pl.pallas_call(
        body, name="forget_cumsum", grid=(s // tb,),
        in_specs=[pl.BlockSpec((tb, LANES), lambda i: (i, R_F // LANES)), _const((1, LANES)), _const((LANES, 4 * LANES)),
                  _const((3, LANES, LANES))],
        out_specs=(pl.BlockSpec((tb, 4 * LANES), lambda i: (i, 0)), pl.BlockSpec((HEADS, tb), lambda i: (0, i)),
                   pl.BlockSpec((tb, LANES), lambda i: (i, 0))),
        out_shape=(jax.ShapeDtypeStruct((s, 4 * LANES), F32), jax.ShapeDtypeStruct((HEADS, s), F32),
                   jax.ShapeDtypeStruct((s, LANES), BF16)),
        scratch_shapes=[pltpu.VMEM((8, LANES), F32)],
        compiler_params=_cparams(("arbitrary",)))(rest, bf128, selp, selk)


def _dfcum(dfk, dfq, rest, bf128, selq):
    s = rest.shape[0]
    tb = min(TB_CUM, s)
    nb = s // tb

    def body(dk_ref, dq_ref, fz_ref, bf_ref, sel_ref, df_ref, dbf_ref, carry_ref):
        @pl.when(pl.program_id(0) == 0)
        def _():
            carry_ref[...] = jnp.zeros_like(carry_ref)
            dbf_ref[...] = jnp.zeros_like(dbf_ref)

        d = _dot_by_sel(dk_ref[...] + dq_ref[...], sel_ref[...])
        r = lax.broadcasted_iota(jnp.int32, (tb, tb), 0)
        c = lax.broadcasted_iota(jnp.int32, (tb, tb), 1)
        triu = (c >= r).astype(F32)
        dlogf = _dot_sel(triu, d) + carry_ref[0:1, :]
        carry_ref[0:1, :] = dlogf[0:1, :]
        z = fz_ref[...] + bf_ref[...]
        df = dlogf * (1.0 / (1.0 + jnp.exp(z)))
        df_ref[...] = df.astype(BF16)
        dbf_ref[0:1, :] += jnp.sum(df, axis=0, keepdims=True)

    return pl.pallas_call(
        body, name="forget_grad", grid=(nb,),
        in_specs=[pl.BlockSpec((tb, 4 * LANES), lambda i: (nb - 1 - i, 0)),
                  pl.BlockSpec((tb, 4 * LANES), lambda i: (nb - 1 - i, 0)),
                  pl.BlockSpec((tb, LANES), lambda i: (nb - 1 - i, R_F // LANES)),
                  _const((1, LANES)), _const((4 * LANES, LANES))],
        out_specs=(pl.BlockSpec((tb, LANES), lambda i: (nb - 1 - i, 0)), pl.BlockSpec((8, LANES), lambda i: (0, 0))),
        out_shape=(jax.ShapeDtypeStruct((s, LANES), BF16), jax.ShapeDtypeStruct((8, LANES), F32)),
        scratch_shapes=[pltpu.VMEM((8, LANES), F32)],
        compiler_params=_cparams(("arbitrary",)))(dfk, dfq, rest, bf128, selq)


def _scaled(q):
    return (q.astype(F32) * (HEAD_DIM ** -0.5)).astype(BF16)


def _attn_fwd(qkv, kbias, fpc):
    s = qkv.shape[0]
    t = min(T_ATT, s)
    n = s // t
    ch = min(ATT_CHUNK, t)
    wide = 2 * LANES
    pairs = ATT_PAIRS
    width = pairs * LANES
    groups = 4 // pairs

    def body(q_ref, k_ref, v_ref, kb_ref, fc_ref, o_ref, lse_ref, s_scr, p_scr, m_scr, a_scr, acc_scr):
        i = pl.program_id(1)
        g = pl.program_id(0)
        lane = lax.broadcasted_iota(jnp.int32, (t, LANES), 1)
        first = lane < HEAD_DIM
        ones_col = ((lane == 0).astype(BF16), (lane == 1).astype(BF16))
        m_scr[...] = jnp.full(m_scr.shape, NEG, F32)
        acc_scr[...] = jnp.zeros_like(acc_scr)
        qm = []
        for pp in range(pairs):
            q = _scaled(q_ref[:, pp * LANES:(pp + 1) * LANES])
            zq = jnp.zeros_like(q)
            for hh in range(2):
                head = 2 * (g * pairs + pp) + hh
                fq = _split3(fc_ref[:, pp * LANES + hh:pp * LANES + hh + 1])
                bias = jnp.where((lane >= 3 * head) & (lane < 3 * head + 3), 1.0, 0.0).astype(BF16)
                for term in range(3):
                    bias = jnp.where(lane == BIAS_ONES + 3 * head + term, fq[term], bias)
                qh = jnp.where(first, q, zq) if hh == 0 else jnp.where(first, zq, q)
                qm.append(jnp.concatenate([qh, bias], axis=1))

        def step(j, masked):
            r0 = pl.multiple_of(j * t, t)
            vaug = []
            kbias_blk = kb_ref[pl.ds(r0, t), :]
            for pp in range(pairs):
                kb = jnp.concatenate([k_ref[pl.ds(r0, t), pp * LANES:(pp + 1) * LANES], kbias_blk], axis=1)
                vb = v_ref[pl.ds(r0, t), pp * LANES:(pp + 1) * LANES]
                zv = jnp.zeros_like(vb)
                vaug += [jnp.concatenate([jnp.where(first, vb, zv), ones_col[0]], axis=1),
                         jnp.concatenate([jnp.where(first, zv, vb), ones_col[1]], axis=1)]
                for hh in range(2):
                    s_scr[2 * pp + hh] = _dot_nt(qm[2 * pp + hh], kb)
            pv = []
            for hd in range(2 * pairs):
                for c in range(t // ch):
                    rows = pl.ds(c * ch, ch)
                    hi = min(t, (c * ch // LANES + 1) * LANES) if masked else t
                    sc = s_scr[hd, rows, 0:hi]
                    if masked:
                        rq = c * ch + lax.broadcasted_iota(jnp.int32, (ch, hi), 0)
                        ck = lax.broadcasted_iota(jnp.int32, (ch, hi), 1)
                        sc = jnp.where(ck <= rq, sc, NEG)
                    m_old = m_scr[hd, rows, :]
                    m_new = jnp.maximum(m_old, jnp.max(sc, axis=1, keepdims=True))
                    p_scr[hd, rows, 0:hi] = jnp.exp(sc - m_new).astype(BF16)
                    if hi < t:
                        p_scr[hd, rows, hi:t] = jnp.zeros((ch, t - hi), BF16)
                    a_scr[hd, rows, :] = jnp.exp(m_old - m_new)
                    m_scr[hd, rows, :] = m_new
                pv.append(_dot(p_scr[hd], vaug[hd]))
            for pp in range(pairs):
                a0, a1 = a_scr[2 * pp], a_scr[2 * pp + 1]
                alpha = jnp.concatenate([jnp.where(first, a0, a1), jnp.where(lane == 0, a0, a1)], axis=1)
                acc_scr[pp] = acc_scr[pp] * alpha + pv[2 * pp] + pv[2 * pp + 1]
            return 0

        lax.fori_loop(0, i, lambda j, _: step(j, False), 0)
        step(i, True)
        lse = jnp.zeros((t, LANES), F32)
        for pp in range(pairs):
            l0 = acc_scr[pp, :, LANES:LANES + 1]
            l1 = acc_scr[pp, :, LANES + 1:LANES + 2]
            o_ref[:, pp * LANES:(pp + 1) * LANES] = acc_scr[pp, :, 0:LANES] * jnp.where(first, 1.0 / l0, 1.0 / l1)
            lse = jnp.where(lane == 2 * pp, m_scr[2 * pp] + jnp.log(l0), lse)
            lse = jnp.where(lane == 2 * pp + 1, m_scr[2 * pp + 1] + jnp.log(l1), lse)
        lse_ref[...] = lse

    blk = pl.BlockSpec((t, width), lambda g, i: (i, g))
    return pl.pallas_call(
        body, name="attn_fwd", grid=(groups, n),
        in_specs=[blk,
                  pl.BlockSpec((s, width), lambda g, i: (0, groups + g)),
                  pl.BlockSpec((s, width), lambda g, i: (0, 2 * groups + g)),
                  pl.BlockSpec((s, LANES), lambda g, i: (0, 0)),
                  blk],
        out_specs=(blk, pl.BlockSpec((t, LANES), lambda g, i: (i, g))),
        out_shape=(jax.ShapeDtypeStruct((s, FOX_W), F32), jax.ShapeDtypeStruct((s, groups * LANES), F32)),
        scratch_shapes=[pltpu.VMEM((2 * pairs, t, t), F32), pltpu.VMEM((2 * pairs, t, t), BF16),
                        pltpu.VMEM((2 * pairs, t, 1), F32), pltpu.VMEM((2 * pairs, t, 1), F32),
                        pltpu.VMEM((pairs, t, wide), F32)],
        compiler_params=_cparams(("parallel", "arbitrary")))(qkv, qkv, qkv, kbias, fpc)


def _attn_bwd(qkv, do, lse5, dlt5, frow5, fpc):
    s = qkv.shape[0]
    t = min(T_ATT, s)
    n = s // t
    wide = 2 * LANES

    ch = min(ATT_CHUNK, t)

    def body(q_ref, do_ref, k_ref, v_ref, lse_ref, dl_ref, fr_ref, fc_ref,
             dq_ref, dk_ref, dv_ref, dfk_ref, dfq_ref, dq_acc, st_scr, dp_scr, pt_scr, ds_scr, dk_acc, dv_acc, fk_scr):
        j = pl.program_id(1)

        @pl.when(j == 0)
        def _():
            dq_acc[...] = jnp.zeros_like(dq_acc)

        dk_acc[...] = jnp.zeros_like(dk_acc)
        dv_acc[...] = jnp.zeros_like(dv_acc)
        lane = lax.broadcasted_iota(jnp.int32, (t, LANES), 1)
        first = lane < HEAD_DIM
        ones_col = ((lane == 0).astype(BF16), (lane == 1).astype(BF16))
        kb = k_ref[...]
        vb = v_ref[...]
        zk = jnp.zeros_like(kb)
        kaug = (jnp.concatenate([jnp.where(first, kb, zk), ones_col[0]], axis=1),
                jnp.concatenate([jnp.where(first, zk, kb), ones_col[1]], axis=1))
        fk_scr[0] = fc_ref[:, 0:1]
        fk_scr[1] = fc_ref[:, 1:2]

        def step(blocks, masked):
            chains = []
            for bi, i in enumerate(blocks):
                r0 = pl.multiple_of(i * t, t)
                qb = _scaled(q_ref[pl.ds(r0, t), :])
                dob = do_ref[pl.ds(r0, t), :]
                zq = jnp.zeros_like(qb)
                qm = (jnp.where(first, qb, zq), jnp.where(first, zq, qb))
                dom = (jnp.where(first, dob, zq), jnp.where(first, zq, dob))
                for hh in range(2):
                    st_scr[2 * bi + hh] = _dot_nt(kb, qm[hh])
                    dp_scr[2 * bi + hh] = _dot_nt(vb, dom[hh])
                    chains.append((i, hh, qm[hh], dom[hh]))
            dq_add = [jnp.zeros((t, wide), F32) for _ in blocks]
            for cn, (i, hh, qmh, domh) in enumerate(chains):
                bias = fr_ref[0, hh, i] - lse_ref[0, hh, i]
                dl = dl_ref[0, hh, i]
                for c in range(t // ch):
                    rows = pl.ds(c * ch, ch)
                    lo = c * ch // LANES * LANES if masked else 0
                    st = st_scr[cn, rows, lo:t] + (bias[:, lo:t] - fk_scr[hh, rows, :])
                    if masked:
                        rk = c * ch + lax.broadcasted_iota(jnp.int32, (ch, t - lo), 0)
                        cq = lo + lax.broadcasted_iota(jnp.int32, (ch, t - lo), 1)
                        st = jnp.where(rk <= cq, st, NEG)
                    pt = jnp.exp(st)
                    pt_scr[cn, rows, lo:t] = pt.astype(BF16)
                    ds_scr[cn, rows, lo:t] = (pt * (dp_scr[cn, rows, lo:t] - dl[:, lo:t])).astype(BF16)
                    if lo > 0:
                        pt_scr[cn, rows, 0:lo] = jnp.zeros((ch, lo), BF16)
                        ds_scr[cn, rows, 0:lo] = jnp.zeros((ch, lo), BF16)
                dsb = ds_scr[cn]
                dv_acc[...] += _dot(pt_scr[cn], domh)
                dk_acc[...] += _dot(dsb, jnp.concatenate([qmh, ones_col[hh]], axis=1))
                dq_add[cn // 2] = dq_add[cn // 2] + _dot_tn(dsb, kaug[hh])
            for bi, i in enumerate(blocks):
                dq_acc[pl.ds(pl.multiple_of(i * t, t), t), :] += dq_add[bi]
            return 0

        step([j], True)
        odd = (n - 1 - j) % 2
        lax.fori_loop(0, odd, lambda _, carry: step([j + 1], False), 0)
        first_pair = j + 1 + odd
        lax.fori_loop(0, (n - first_pair) // 2,
                      lambda p, _: step([first_pair + 2 * p, first_pair + 2 * p + 1], False), 0)
        dk_ref[...] = dk_acc[:, 0:LANES].astype(BF16)
        dv_ref[...] = dv_acc[...].astype(BF16)
        dfk_ref[...] = -dk_acc[:, LANES:wide]

        @pl.when(j == n - 1)
        def _():
            dq_ref[...] = (dq_acc[:, 0:LANES] * (HEAD_DIM ** -0.5)).astype(BF16)
            dfq_ref[...] = dq_acc[:, LANES:wide]

    stat = pl.BlockSpec((1, 2, n, 1, t), lambda h, j: (h, 0, 0, 0, 0))
    blk = pl.BlockSpec((t, LANES), lambda h, j: (j, h))
    full = pl.BlockSpec((s, LANES), lambda h, j: (0, h))
    return pl.pallas_call(
        body, name="attn_bwd", grid=(4, n),
        in_specs=[full, full,
                  pl.BlockSpec((t, LANES), lambda h, j: (j, 4 + h)),
                  pl.BlockSpec((t, LANES), lambda h, j: (j, 8 + h)),
                  stat, stat, stat, blk],
        out_specs=(full, blk, blk, blk, full),
        out_shape=(jax.ShapeDtypeStruct((s, FOX_W), BF16), jax.ShapeDtypeStruct((s, FOX_W), BF16),
                   jax.ShapeDtypeStruct((s, FOX_W), BF16), jax.ShapeDtypeStruct((s, 4 * LANES), F32),
                   jax.ShapeDtypeStruct((s, 4 * LANES), F32)),
        scratch_shapes=[pltpu.VMEM((s, wide), F32), pltpu.VMEM((4, t, t), F32), pltpu.VMEM((4, t, t), F32),
                        pltpu.VMEM((4, t, t), BF16), pltpu.VMEM((4, t, t), BF16), pltpu.VMEM((t, wide), F32),
                        pltpu.VMEM((t, LANES), F32), pltpu.VMEM((2, t, 1), F32)],
        compiler_params=_cparams(("parallel", "arbitrary")))(qkv, do, qkv, qkv, lse5, dlt5, frow5, fpc)


def _ssm_block_params(a_re, a_im, log_dt, b_re, b_im):
    dt = jnp.exp(log_dt)[:, None]
    mag = jnp.exp(a_re * dt)
    ar = mag * jnp.cos(a_im * dt)
    ai = mag * jnp.sin(a_im * dt)
    den = a_re * a_re + a_im * a_im
    nr = ar - 1.0
    cr = (nr * a_re + ai * a_im) / den
    ci = (ai * a_re - nr * a_im) / den
    bbr = cr[:, :, None] * b_re - ci[:, :, None] * b_im
    bbi = cr[:, :, None] * b_im + ci[:, :, None] * b_re
    return ar, ai, bbr, bbi


def _block_diag(blocks):
    g, r, c = blocks.shape
    eye = jnp.eye(g, dtype=blocks.dtype)
    return (blocks[:, :, None, :] * eye[:, None, :, None]).reshape(g * r, g * c)


def _scan_consts(a_re, a_im, log_dt, seg, reverse):
    dt = jnp.exp(log_dt)[:, None]
    lr = (a_re * dt).reshape(1, NSTATE)
    li = (a_im * dt).reshape(1, NSTATE)
    if reverse:
        li = -li
    rows = jnp.arange(8, dtype=F32)[:, None]

    def power(k):
        mag = jnp.exp(k * lr)
        return mag * jnp.cos(k * li), mag * jnp.sin(k * li)

    tiles = list(power(1.0))
    for k in (1, 2, 4):
        keep = (rows < 8 - k) if reverse else (rows >= k)
        pr, pi_ = power(float(k * seg))
        tiles += [jnp.where(keep, pr, 0.0), jnp.where(keep, pi_, 0.0)]
    tiles += list(power(seg * ((8.0 - rows) if reverse else (rows + 1.0))))
    tiles = jnp.stack([jnp.broadcast_to(tl, (8, NSTATE)) for tl in tiles])
    steps = jnp.arange(seg, dtype=F32)[:, None]
    table = jnp.stack([jnp.broadcast_to(p[:, None, :], (seg, 8, NSTATE))
                       for p in power((seg - steps) if reverse else (steps + 1.0))])
    return tiles, table


_SCAN_W = 1024
_HALF_W = S5_W // 2
_HALF_S = NSTATE // 2


def _compact_diag(blocks_re, blocks_im):
    hg = GROUPS // 2
    return jnp.concatenate([_block_diag(b[h * hg:(h + 1) * hg]) for b in (blocks_re, blocks_im) for h in range(2)],
                           axis=1)


def _half_expand(v, w_ref, out_ref):
    for half in range(2):
        vh = v[:, half * _HALF_W:(half + 1) * _HALF_W]
        for part in range(2):
            c0 = part * NSTATE + half * _HALF_S
            out_ref[:, c0:c0 + _HALF_S] = _dot(vh, w_ref[:, c0:c0 + _HALF_S])


def _half_contract(x_ref, w_ref, half):
    out = None
    for part in range(2):
        r0 = part * NSTATE + half * _HALF_S
        term = _dot_nt(x_ref[:, r0:r0 + _HALF_S].astype(BF16), w_ref[:, r0:r0 + _HALF_S])
        out = term if out is None else out + term
    return out


def _half_outer(v, x_ref, acc_ref):
    for half in range(2):
        vh = v[:, half * _HALF_W:(half + 1) * _HALF_W]
        for part in range(2):
            c0 = part * NSTATE + half * _HALF_S
            acc_ref[:, c0:c0 + _HALF_S] += _dot_tn(vh, x_ref[:, c0:c0 + _HALF_S].astype(BF16))


def _segment_perm(tb):
    seg = tb // 8
    row = lax.broadcasted_iota(jnp.int32, (tb, tb), 0)
    col = lax.broadcasted_iota(jnp.int32, (tb, tb), 1)
    perm = (col == (row % 8) * seg + row // 8).astype(BF16)
    back = (col == (row % seg) * 8 + row // seg).astype(BF16)
    return perm, back


def _segment_ends(re, im, cf_ref, cb_ref, cr, ci, reverse):
    for n_, k in enumerate((1, 2, 4)):
        kr = cf_ref[2 + 2 * n_, :, cr]
        ki = cf_ref[3 + 2 * n_, :, cr]
        sr = pltpu.roll(re, 8 - k if reverse else k, 0)
        si = pltpu.roll(im, 8 - k if reverse else k, 0)
        re, im = re + kr * sr - ki * si, im + kr * si + ki * sr
    cbr, cbi = cb_ref[:, cr], cb_ref[:, ci]
    pr, pi_ = cf_ref[8, :, cr], cf_ref[9, :, cr]
    re, im = re + pr * cbr - pi_ * cbi, im + pr * cbi + pi_ * cbr
    edge = lax.broadcasted_iota(jnp.int32, re.shape, 0) == (7 if reverse else 0)
    in_r = jnp.where(edge, cbr, pltpu.roll(re, 7 if reverse else 1, 0))
    in_i = jnp.where(edge, cbi, pltpu.roll(im, 7 if reverse else 1, 0))
    out = slice(0, 1) if reverse else slice(7, 8)
    cb_ref[:, cr] = jnp.broadcast_to(re[out, :], re.shape)
    cb_ref[:, ci] = jnp.broadcast_to(im[out, :], im.shape)
    return in_r, in_i


def _ssm_fwd(rest, bd, cd, consts, table):
    s = rest.shape[0]
    tb = min(TB_SSM, s)
    seg = tb // 8
    ns2 = 2 * NSTATE

    def body(u_ref, bd_ref, cd_ref, cf_ref, tab_ref, y_ref, x_ref, cb_ref):
        @pl.when(pl.program_id(0) == 0)
        def _():
            cb_ref[...] = jnp.zeros_like(cb_ref)

        perm, back = _segment_perm(tb)
        _half_expand(_dot(perm, u_ref[...].astype(BF16)).astype(BF16), bd_ref, x_ref)
        for cc in range(NSTATE // _SCAN_W):
            cr = pl.ds(cc * _SCAN_W, _SCAN_W)
            ci = pl.ds(NSTATE + cc * _SCAN_W, _SCAN_W)
            ar, ai = cf_ref[0, :, cr], cf_ref[1, :, cr]

            def local(i, carry, cr=cr, ci=ci, ar=ar, ai=ai):
                re, im = carry
                rows = pl.ds(pl.multiple_of(i * 8, 8), 8)
                re, im = ar * re - ai * im + x_ref[rows, cr], ar * im + ai * re + x_ref[rows, ci]
                x_ref[rows, cr] = re
                x_ref[rows, ci] = im
                return re, im

            zero = jnp.zeros((8, _SCAN_W), F32)
            re, im = lax.fori_loop(0, seg, local, (zero, zero))
            in_r, in_i = _segment_ends(re, im, cf_ref, cb_ref, cr, ci, False)

            def fix(i, _, cr=cr, ci=ci, in_r=in_r, in_i=in_i):
                rows = pl.ds(pl.multiple_of(i * 8, 8), 8)
                tr, ti = tab_ref[0, i, :, cr], tab_ref[1, i, :, cr]
                x_ref[rows, cr] += tr * in_r - ti * in_i
                x_ref[rows, ci] += tr * in_i + ti * in_r
                return 0

            lax.fori_loop(0, seg, fix, 0)
        y_p = jnp.concatenate([_half_contract(x_ref, cd_ref, half) for half in range(2)], axis=1)
        y_ref[...] = _dot_sel(back, y_p, terms=2)

    return pl.pallas_call(
        body, name="ssm_fwd", grid=(s // tb,),
        in_specs=[pl.BlockSpec((tb, S5_W), lambda i: (i, R_U // S5_W)), _const((_HALF_W, ns2)), _const((_HALF_W, ns2)),
                  _const((10, 8, NSTATE)), _const((2, seg, 8, NSTATE))],
        out_specs=(pl.BlockSpec((tb, S5_W), lambda i: (i, 0)), pl.BlockSpec((tb, ns2), lambda i: (i, 0))),
        out_shape=(jax.ShapeDtypeStruct((s, S5_W), F32), jax.ShapeDtypeStruct((s, ns2), F32)),
        scratch_shapes=[pltpu.VMEM((8, ns2), F32)],
        compiler_params=_cparams(("arbitrary",)))(rest, bd, cd, consts, table)


def _ssm_bwd(dys, xs, rest, bd, cd, consts, table, dskip):
    s = dys.shape[0]
    tb = min(TB_SSM, s)
    seg = tb // 8
    nb = s // tb
    ns2 = 2 * NSTATE

    def body(dy_ref, x_ref, u_ref, bd_ref, cd_ref, cf_ref, tab_ref, dsk_ref, du_ref, gb_ref, gc_ref, da_ref,
             g_ref, cb_ref, acc_b, acc_c):
        step = pl.program_id(0)

        @pl.when(step == 0)
        def _():
            cb_ref[...] = jnp.zeros_like(cb_ref)
            acc_b[...] = jnp.zeros_like(acc_b)
            acc_c[...] = jnp.zeros_like(acc_c)
            da_ref[...] = jnp.zeros_like(da_ref)

        perm, back = _segment_perm(tb)
        dy = dy_ref[...]
        dy_p = _dot(perm, dy.astype(BF16)).astype(BF16)
        u_p = _dot(perm, u_ref[...].astype(BF16)).astype(BF16)
        _half_expand(dy_p, cd_ref, g_ref)
        for cc in range(NSTATE // _SCAN_W):
            cr = pl.ds(cc * _SCAN_W, _SCAN_W)
            ci = pl.ds(NSTATE + cc * _SCAN_W, _SCAN_W)
            ar, ai = cf_ref[0, :, cr], cf_ref[1, :, cr]

            def local(ii, carry, cr=cr, ci=ci, ar=ar, ai=ai):
                re, im = carry
                rows = pl.ds(pl.multiple_of((seg - 1 - ii) * 8, 8), 8)
                re, im = ar * re - ai * im + g_ref[rows, cr], ar * im + ai * re + g_ref[rows, ci]
                g_ref[rows, cr] = re
                g_ref[rows, ci] = im
                return re, im

            zero = jnp.zeros((8, _SCAN_W), F32)
            re, im = lax.fori_loop(0, seg, local, (zero, zero))
            in_r, in_i = _segment_ends(re, im, cf_ref, cb_ref, cr, ci, True)

            def fix(ii, carry, cr=cr, ci=ci, in_r=in_r, in_i=in_i):
                nr, ni, acr, aci = carry
                i = seg - 1 - ii
                rows = pl.ds(pl.multiple_of(i * 8, 8), 8)
                tr, ti = tab_ref[0, i, :, cr], tab_ref[1, i, :, cr]
                gr = g_ref[rows, cr] + tr * in_r - ti * in_i
                gi = g_ref[rows, ci] + tr * in_i + ti * in_r
                g_ref[rows, cr] = gr
                g_ref[rows, ci] = gi
                xr, xi = x_ref[rows, cr], x_ref[rows, ci]
                return gr, gi, acr + nr * xr + ni * xi, aci + ni * xr - nr * xi

            _, _, acr, aci = lax.fori_loop(0, seg, fix, (in_r, in_i, zero, zero))
            da_ref[:, cr] += acr
            da_ref[:, ci] += aci
        du_p = jnp.concatenate([_half_contract(g_ref, bd_ref, half) for half in range(2)], axis=1)
        du_ref[...] = (_dot_sel(back, du_p, terms=2) + dy * dsk_ref[...]).astype(BF16)
        _half_outer(u_p, g_ref, acc_b)
        _half_outer(dy_p, x_ref, acc_c)

        @pl.when(step == nb - 1)
        def _():
            for g in range(GROUPS):
                src = slice((g % (GROUPS // 2)) * GCH, (g % (GROUPS // 2) + 1) * GCH)
                dst = slice(g * GCH, (g + 1) * GCH)
                for part in range(2):
                    cols = slice(part * NSTATE + g * STATE, part * NSTATE + (g + 1) * STATE)
                    gb_ref[dst, part * STATE:(part + 1) * STATE] = acc_b[src, cols]
                    gc_ref[dst, part * STATE:(part + 1) * STATE] = acc_c[src, cols]

    rev = lambda i: (nb - 1 - i, 0)
    small = pl.BlockSpec((S5_W, 2 * STATE), lambda i: (0, 0))
    return pl.pallas_call(
        body, name="ssm_bwd", grid=(nb,),
        in_specs=[pl.BlockSpec((tb, S5_W), rev), pl.BlockSpec((tb, ns2), rev),
                  pl.BlockSpec((tb, S5_W), lambda i: (nb - 1 - i, R_U // S5_W)),
                  _const((_HALF_W, ns2)), _const((_HALF_W, ns2)), _const((10, 8, NSTATE)), _const((2, seg, 8, NSTATE)),
                  _const((1, S5_W))],
        out_specs=(pl.BlockSpec((tb, S5_W), rev), small, small, pl.BlockSpec((8, ns2), lambda i: (0, 0))),
        out_shape=(jax.ShapeDtypeStruct((s, S5_W), BF16), jax.ShapeDtypeStruct((S5_W, 2 * STATE), F32),
                   jax.ShapeDtypeStruct((S5_W, 2 * STATE), F32), jax.ShapeDtypeStruct((8, ns2), F32)),
        scratch_shapes=[pltpu.VMEM((tb, ns2), F32), pltpu.VMEM((8, ns2), F32),
                        pltpu.VMEM((_HALF_W, ns2), F32), pltpu.VMEM((_HALF_W, ns2), F32)],
        compiler_params=_cparams(("arbitrary",)))(dys, xs, rest, bd, cd, consts, table, dskip)


_GELU_C = math.sqrt(2.0 / math.pi)
_GELU_A = 0.044715


def _mid(o, rest, ys0, x, tgt, w, vec, hsel):
    s = o.shape[0]
    tm = min(TM, s)
    nsteps = s // tm
    half = FOX_W

    def body(o_ref, ga_ref, gb_ref, za_ref, u_ref, zb_ref, ys0_ref, x_ref, t_ref,
             wglu_ref, wua_ref, wub_ref, wout_ref, vec_ref, hsel_ref,
             dx2_ref, dga_ref, dgb_ref, do_ref, dza_ref, dzb_ref, dys_ref, dlt_ref,
             gout_hbm, gua_hbm, gub_hbm, gglu_hbm, vout_ref,
             a_out, a_ua, a_ub, a_glu):
        step = pl.program_id(0)

        @pl.when(step == 0)
        def _():
            a_out[...] = jnp.zeros_like(a_out)
            a_ua[...] = jnp.zeros_like(a_ua)
            a_ub[...] = jnp.zeros_like(a_ub)
            a_glu[...] = jnp.zeros_like(a_glu)
            vout_ref[...] = jnp.zeros_like(vout_ref)

        gate = vec_ref[0:1, :]
        gfin = vec_ref[1:2, :]
        dsk = vec_ref[2:3, 0:half]
        bglu = vec_ref[2:3, half:2 * half]

        o_v = o_ref[...]
        za = za_ref[...]
        sza = _sigmoid(za)
        silu_za = za * sza
        ya_b = (o_v * silu_za).astype(BF16)
        u_v = u_ref[...]
        ys = ys0_ref[...] + dsk * u_v
        inner = _GELU_C * (ys + _GELU_A * ys * ys * ys)
        th = jnp.tanh(inner)
        yg = 0.5 * ys * (1.0 + th)
        yg_b = yg.astype(BF16)
        st = _sigmoid(_dot(yg_b, wglu_ref[...]) + bglu)
        yb1 = yg * st
        zb = zb_ref[...]
        szb = _sigmoid(zb)
        silu_zb = zb * szb
        yb_b = (yb1 * silu_zb).astype(BF16)
        ua = _dot(ya_b, wua_ref[...])
        ub = _dot(yb_b, wub_ref[...])
        sga = _sigmoid(ga_ref[...])
        sgb = _sigmoid(gb_ref[...])
        merged_b = (sga * ua + sgb * ub).astype(BF16)
        mo = _dot(merged_b, wout_ref[...])
        x2 = x_ref[...] + gate * mo
        r2 = lax.rsqrt(jnp.mean(x2 * x2, axis=-1, keepdims=True) + EPS)
        x2n = x2 * r2
        diff = x2n * gfin - t_ref[...]
        loss = 0.5 * jnp.sum(jnp.mean(diff * diff, axis=-1, keepdims=True), axis=0, keepdims=True)
        dy = diff * (1.0 / D_MODEL)
        dx2n = dy * gfin
        dx2 = r2 * (dx2n - x2n * jnp.mean(dx2n * x2n, axis=-1, keepdims=True))
        dx2_ref[...] = dx2
        vout_ref[0:1, :] += jnp.sum(dy * x2n, axis=0, keepdims=True)
        vout_ref[1:2, :] += jnp.sum(dx2 * mo, axis=0, keepdims=True)
        vout_ref[3:4, :] += jnp.broadcast_to(loss, (1, D_MODEL))
        dmo_b = (dx2 * gate).astype(BF16)
        dmerged = _dot_nt(dmo_b, wout_ref[...])
        a_out[...] += _dot_tn(merged_b, dmo_b)
        dua_b = (dmerged * sga).astype(BF16)
        dub_b = (dmerged * sgb).astype(BF16)
        dga_ref[...] = (dmerged * ua * sga * (1.0 - sga)).astype(BF16)
        dgb_ref[...] = (dmerged * ub * sgb * (1.0 - sgb)).astype(BF16)
        dya = _dot_nt(dua_b, wua_ref[...])
        dyb = _dot_nt(dub_b, wub_ref[...])
        a_ua[...] += _dot_tn(ya_b, dua_b)
        a_ub[...] += _dot_tn(yb_b, dub_b)
        do_b = (dya * silu_za).astype(BF16)
        do_ref[...] = do_b
        dza_ref[...] = (dya * o_v * (sza * (1.0 + za * (1.0 - sza)))).astype(BF16)
        hsel = hsel_ref[...].astype(BF16)
        dlt_ref[...] = sum(_dot_nt(hsel, part) for part in _split3(do_b.astype(F32) * o_v))
        dyb1 = dyb * silu_zb
        dzb_ref[...] = (dyb * yb1 * (szb * (1.0 + zb * (1.0 - szb)))).astype(BF16)
        dt = dyb1 * yg * st * (1.0 - st)
        dt_b = dt.astype(BF16)
        dyg = dyb1 * st + _dot_nt(dt_b, wglu_ref[...])
        a_glu[...] += _dot_tn(yg_b, dt_b)
        dgelu = 0.5 * (1.0 + th) + 0.5 * ys * (1.0 - th * th) * _GELU_C * (1.0 + 3.0 * _GELU_A * ys * ys)
        dys = dyg * dgelu
        dys_ref[...] = dys
        vout_ref[2:3, 0:half] += jnp.sum(dys * u_v, axis=0, keepdims=True)
        vout_ref[2:3, half:2 * half] += jnp.sum(dt, axis=0, keepdims=True)

        @pl.when(step == nsteps - 1)
        def _():
            pltpu.sync_copy(a_out, gout_hbm)
            pltpu.sync_copy(a_ua, gua_hbm)
            pltpu.sync_copy(a_ub, gub_hbm)
            pltpu.sync_copy(a_glu, gglu_hbm)

    def rows(width, col=0):
        return pl.BlockSpec((tm, width), lambda i, col=col: (i, col))

    anyspace = pl.BlockSpec(memory_space=pl.ANY)
    wshapes = [(S5_W, S5_W), (FOX_W, D_MODEL), (S5_W, D_MODEL), (D_MODEL, D_MODEL)]
    return pl.pallas_call(
        body, name="mid", grid=(nsteps,),
        in_specs=[rows(FOX_W), rows(D_MODEL, R_GA // D_MODEL), rows(D_MODEL, R_GB // D_MODEL),
                  rows(FOX_W, R_ZA // FOX_W), rows(S5_W, R_U // S5_W), rows(S5_W, R_ZB // S5_W),
                  rows(S5_W), rows(D_MODEL), rows(D_MODEL)]
                 + [_const(sh) for sh in wshapes]
                 + [_const((8, D_MODEL)), _const((HEADS, FOX_W))],
        out_specs=(rows(D_MODEL), rows(D_MODEL), rows(D_MODEL), rows(FOX_W), rows(FOX_W), rows(S5_W), rows(S5_W),
                   pl.BlockSpec((HEADS, tm), lambda i: (0, i)),
                   anyspace, anyspace, anyspace, anyspace, pl.BlockSpec((8, D_MODEL), lambda i: (0, 0))),
        out_shape=(jax.ShapeDtypeStruct((s, D_MODEL), F32), jax.ShapeDtypeStruct((s, D_MODEL), BF16),
                   jax.ShapeDtypeStruct((s, D_MODEL), BF16), jax.ShapeDtypeStruct((s, FOX_W), BF16),
                   jax.ShapeDtypeStruct((s, FOX_W), BF16), jax.ShapeDtypeStruct((s, S5_W), BF16),
                   jax.ShapeDtypeStruct((s, S5_W), F32), jax.ShapeDtypeStruct((HEADS, s), F32),
                   jax.ShapeDtypeStruct((D_MODEL, D_MODEL), F32), jax.ShapeDtypeStruct((FOX_W, D_MODEL), F32),
                   jax.ShapeDtypeStruct((S5_W, D_MODEL), F32), jax.ShapeDtypeStruct((S5_W, S5_W), F32),
                   jax.ShapeDtypeStruct((8, D_MODEL), F32)),
        scratch_shapes=[pltpu.VMEM((D_MODEL, D_MODEL), F32), pltpu.VMEM((FOX_W, D_MODEL), F32),
                        pltpu.VMEM((S5_W, D_MODEL), F32), pltpu.VMEM((S5_W, S5_W), F32)],
        compiler_params=_cparams(("arbitrary",)),
    )(o, rest, rest, rest, rest, rest, ys0, x, tgt, *w, vec, hsel)


def _dh(dq, dk, dv, dga, dgb, dza, du, dzb, df, wqkv_t, wrest_t, x, dx2, gs, scatter_srcs):
    s = x.shape[0]
    tm = min(TM_PROJ, s)
    nsteps = s // tm
    na = len(scatter_srcs)

    def body(dq_ref, dk_ref, dv_ref, dga_ref, dgb_ref, dza_ref, du_ref, dzb_ref, df_ref, wq_ref, wr_ref,
             x_ref, dx2_ref, gs_ref, *rest_refs):
        src_refs = rest_refs[:na]
        gx_ref, vout_ref = rest_refs[na:na + 2]
        out_refs = rest_refs[na + 2:2 * na + 2]
        send_sems, recv_sems = rest_refs[2 * na + 2:]
        step = pl.program_id(0)
        cx, cy, cc = lax.axis_index("x"), lax.axis_index("y"), lax.axis_index("c")
        peers = [(1 - cx, cy), (cx, 1 - cy), (1 - cx, 1 - cy)]

        def copy(a, k, px, py, slot):
            return pltpu.make_async_remote_copy(
                src_ref=src_refs[a].at[2 * px + py], dst_ref=out_refs[a].at[slot],
                send_sem=send_sems.at[a * 3 + k], recv_sem=recv_sems.at[a * 3 + k],
                device_id=(px, py, cc), device_id_type=MESH)

        @pl.when(step == 0)
        def _():
            vout_ref[...] = jnp.zeros_like(vout_ref)
            for a in range(na):
                for k, (px, py) in enumerate(peers):
                    copy(a, k, px, py, 2 * cx + cy).start()

        dh = _dot(dq_ref[...], wq_ref[0:512, :])
        dh += _dot(dk_ref[...], wq_ref[512:1024, :])
        dh += _dot(dv_ref[...], wq_ref[1024:1536, :])
        dh += _dot(dga_ref[...], wr_ref[R_GA:R_GB, :])
        dh += _dot(dgb_ref[...], wr_ref[R_GB:R_ZA, :])
        dh += _dot(dza_ref[...], wr_ref[R_ZA:R_U, :])
        dh += _dot(du_ref[...], wr_ref[R_U:R_ZB, :])
        dh += _dot(dzb_ref[...], wr_ref[R_ZB:R_F, :])
        dh += _dot(df_ref[...], wr_ref[R_F:REST_W, :])
        xv = x_ref[...]
        r = lax.rsqrt(jnp.mean(xv * xv, axis=-1, keepdims=True) + EPS)
        xn = xv * r
        dxn = dh * gs_ref[...]
        gx_ref[...] = dx2_ref[...] + r * (dxn - xn * jnp.mean(dxn * xn, axis=-1, keepdims=True))
        vout_ref[0:1, :] += jnp.sum(dh * xn, axis=0, keepdims=True)
        vout_ref[1:2, :] += jnp.sum(dh, axis=0, keepdims=True)

        @pl.when(step == nsteps - 1)
        def _():
            for a in range(na):
                for k, (px, py) in enumerate(peers):
                    copy(a, k, px, py, 2 * px + py).wait_recv()
            for a in range(na):
                for k, (px, py) in enumerate(peers):
                    copy(a, k, px, py, 2 * cx + cy).wait_send()

    def rows(width):
        return pl.BlockSpec((tm, width), lambda i: (i, 0))

    anyspace = pl.BlockSpec(memory_space=pl.ANY)
    return pl.pallas_call(
        body, name="dh", grid=(nsteps,),
        in_specs=[rows(512), rows(512), rows(512), rows(1024), rows(1024), rows(512), rows(512), rows(512), rows(128),
                  _const((1536, D_MODEL)), _const((REST_W, D_MODEL)), rows(D_MODEL), rows(D_MODEL), _const((1, D_MODEL))]
                 + [anyspace] * na,
        out_specs=(rows(D_MODEL), pl.BlockSpec((8, D_MODEL), lambda i: (0, 0))) + (anyspace,) * na,
        out_shape=(jax.ShapeDtypeStruct((s, D_MODEL), F32), jax.ShapeDtypeStruct((8, D_MODEL), F32))
                  + tuple(jax.ShapeDtypeStruct(a.shape, a.dtype) for a in scatter_srcs),
        scratch_shapes=[pltpu.SemaphoreType.DMA((3 * na,)), pltpu.SemaphoreType.DMA((3 * na,))],
        compiler_params=_cparams(("arbitrary",)),
    )(dq, dk, dv, dga, dgb, dza, du, dzb, df, wqkv_t, wrest_t, x, dx2, gs, *scatter_srcs)


def _row_block(rows, mult=8, cap=512):
    if rows <= mult:
        return rows
    padded = -(-rows // mult) * mult
    for cand in range(min(cap, padded) // mult * mult, 0, -mult):
        if padded % cand == 0:
            return cand
    return padded


def _sum4(parts, name):
    rows, cols = parts.shape[1:]
    br = _row_block(rows, 16, 1024)

    def body(p_ref, o_ref):
        acc = p_ref[0].astype(F32)
        for k in range(1, 4):
            acc = acc + p_ref[k].astype(F32)
        o_ref[...] = acc

    return pl.pallas_call(
        body, name=name, grid=(pl.cdiv(rows, br),),
        in_specs=[pl.BlockSpec((4, br, cols), lambda i: (0, i, 0))],
        out_specs=pl.BlockSpec((br, cols), lambda i: (i, 0)),
        out_shape=jax.ShapeDtypeStruct((rows, cols), F32), compiler_params=_cparams(("parallel",)))(parts)


def _pair_add(a, b, name):
    shape = a.shape
    a, b = a.reshape(-1, shape[-1]), b.reshape(-1, shape[-1])
    rows, cols = a.shape
    br = _row_block(rows, 16, 1024)

    def body(a_ref, b_ref, o_ref):
        o_ref[...] = (a_ref[...].astype(F32) + b_ref[...].astype(F32)).astype(BF16)

    spec = pl.BlockSpec((br, cols), lambda i: (i, 0))
    return pl.pallas_call(
        body, name=name, grid=(pl.cdiv(rows, br),), in_specs=[spec, spec], out_specs=spec,
        out_shape=jax.ShapeDtypeStruct((rows, cols), BF16), compiler_params=_cparams(("parallel",)))(a, b).reshape(shape)


def _adamw(w, g, m, v, name):
    rows, cols = w.shape
    br = _row_block(rows)

    def body(w_ref, g_ref, m_ref, v_ref, d_ref, nm_ref, nv_ref):
        gv = g_ref[...]
        nm = ADAM_B1 * m_ref[...] + (1.0 - ADAM_B1) * gv
        nv = ADAM_B2 * v_ref[...] + (1.0 - ADAM_B2) * (gv * gv)
        m_hat = nm / (1.0 - ADAM_B1 ** ADAM_STEP)
        v_hat = nv / (1.0 - ADAM_B2 ** ADAM_STEP)
        d_ref[...] = -ADAM_LR * (m_hat / (jnp.sqrt(v_hat) + ADAM_EPS) + ADAM_WD * w_ref[...])
        nm_ref[...] = nm
        nv_ref[...] = nv

    spec = pl.BlockSpec((br, cols), lambda i: (i, 0))
    shape = jax.ShapeDtypeStruct((rows, cols), F32)
    return pl.pallas_call(
        body, name=name, grid=(pl.cdiv(rows, br),), in_specs=[spec] * 4, out_specs=(spec,) * 3,
        out_shape=(shape,) * 3, compiler_params=_cparams(("parallel",)))(w, g, m, v)


def _pack(parts, row_multiple=8):
    flat = []
    for p in parts:
        v = p.reshape(-1).astype(F32)
        pad = (-v.shape[0]) % LANES
        if pad:
            v = jnp.concatenate([v, jnp.zeros((pad,), F32)])
        flat.append(v)
    v = jnp.concatenate(flat)
    rows = v.shape[0] // LANES
    pad_rows = (-rows) % row_multiple
    if pad_rows:
        v = jnp.concatenate([v, jnp.zeros((pad_rows * LANES,), F32)])
    return v.reshape(-1, LANES)


def _unpack(packed, shapes):
    lead = packed.shape[:-2]
    flat = packed.reshape(lead + (-1,))
    out, off = [], 0
    for sh in shapes:
        size = math.prod(sh)
        out.append(flat[..., off:off + size].reshape(lead + tuple(sh)))
        off += size + (-size) % LANES
    return out


def kernel(x, c, w_ada, b_ada, g_norm, w_in, b_f, a_re, a_im, log_dt, b_re, b_im, c_re, c_im, d_skip, w_glu, b_glu, w_up_a, w_up_b, w_out, g_final, loss_target, m_w_ada, m_b_ada, m_g_norm, m_w_in, m_b_f, m_a_re, m_a_im, m_log_dt, m_b_re, m_b_im, m_c_re, m_c_im, m_d_skip, m_w_glu, m_b_glu, m_w_up_a, m_w_up_b, m_w_out, m_g_final, v_w_ada, v_b_ada, v_g_norm, v_w_in, v_b_f, v_a_re, v_a_im, v_log_dt, v_b_re, v_b_im, v_c_re, v_c_im, v_d_skip, v_w_glu, v_b_glu, v_w_up_a, v_w_up_b, v_w_out, v_g_final):
    xi, yi, ci = lax.axis_index("x"), lax.axis_index("y"), lax.axis_index("c")
    chip = 2 * xi + yi
    me = 4 * xi + 2 * yi + ci
    s = x.shape[1]
    x2d = x[0]
    tgt = loss_target[0]
    n_att = s // min(T_ATT, s)
    t_att = min(T_ATT, s)

    c_all, _ = _allgather8(c.reshape(8, LANES), "gather_c")
    c_all = c_all.reshape(8, D_MODEL)
    ncol = w_ada.shape[2]
    b_cols = lax.dynamic_slice_in_dim(b_ada, chip * ncol, ncol, axis=1)
    mod_cols = _mod_cols(c_all, w_ada[0], b_cols)
    mod_all, _ = _allgather8(mod_cols.reshape(-1, LANES), "gather_mod")
    mod_all = mod_all.reshape(4, 2, 8, ncol)[:, 0]
    mod_me = lax.dynamic_index_in_dim(mod_all, me, axis=1, keepdims=False).reshape(1, 3 * D_MODEL)
    shift, scale, gate = mod_me[:, :D_MODEL], mod_me[:, D_MODEL:2 * D_MODEL], mod_me[:, 2 * D_MODEL:]
    gs = g_norm * (1.0 + scale)

    nshard = w_in.shape[2]
    w_in_t, m_in_t, v_in_t = (jnp.swapaxes(a[0], 0, 1) for a in (w_in, m_w_in, v_w_in))
    wt_pack = jnp.pad(w_in_t.astype(BF16), ((0, SHARD_ROWS - nshard), (0, 0)))
    misc_shapes = [w_glu.shape[1:], w_up_a.shape[1:], w_up_b.shape[1:], w_out.shape[1:]]
    misc_pack = jnp.concatenate([w.reshape(-1) for w in (w_glu, w_up_a, w_up_b, w_out)]).astype(BF16).reshape(-1, LANES)
    def halves(a):
        return a.reshape((2, a.shape[0] // 2) + a.shape[1:])

    wt_all, misc_all = _gather_shards([halves(wt_pack), halves(misc_pack)], "gather_weights")
    wt_all = lax.dynamic_update_index_in_dim(wt_all, halves(wt_pack), chip, 0).reshape((4,) + wt_pack.shape)
    misc_all = lax.dynamic_update_index_in_dim(misc_all, halves(misc_pack), chip, 0).reshape((4,) + misc_pack.shape)
    p_glu, p_ua, p_ub, p_out = _unpack(misc_all, misc_shapes)

    def w_rows(lo, hi):
        out = []
        for j in range(4):
            a, b = max(lo, j * nshard), min(hi, (j + 1) * nshard)
            if a < b:
                out.append(wt_all[j, a - j * nshard:b - j * nshard])
        return out

    wqkv_t = jnp.concatenate(w_rows(O_Q, O_F), axis=0)
    wrest_t = jnp.concatenate(w_rows(O_GA, O_GB) + w_rows(O_GB, O_END) + w_rows(O_ZA, O_U) + w_rows(O_U, O_ZB)
                              + w_rows(O_ZB, O_GA) + w_rows(O_F, O_ZA)
                              + [jnp.zeros((REST_W - R_F - HEADS, D_MODEL), BF16)], axis=0)
    wmid = (p_glu.reshape(S5_W, S5_W), jnp.concatenate([p_ua[j] for j in range(4)], axis=1),
            jnp.concatenate([p_ub[j] for j in range(4)], axis=1), p_out.reshape(D_MODEL, D_MODEL))

    h, qkv, rest = _prenorm_proj(x2d, gs, shift, wqkv_t, wrest_t)
    bf128 = jnp.pad(b_f, ((0, 0), (0, LANES - HEADS)))
    selp = _head_pair_selector()
    fpc, f_t, kbias = _fcum(rest, bf128, selp, _bias_selectors())
    frow5 = f_t.reshape(4, 2, n_att, 1, t_att)
    o, lse_pc = _attn_fwd(qkv, kbias, fpc)

    abar_r, abar_i, bb_r, bb_i = _ssm_block_params(a_re[0], a_im[0], log_dt[0], b_re[0], b_im[0])
    bb_rt, bb_it = jnp.swapaxes(bb_r, 1, 2).astype(BF16), jnp.swapaxes(bb_i, 1, 2).astype(BF16)
    cr_b, ci_b = c_re[0].astype(BF16), (-c_im[0]).astype(BF16)
    bd_c, cd_c = _compact_diag(bb_rt, bb_it), _compact_diag(cr_b, ci_b)
    seg = min(TB_SSM, s) // 8
    ys0, xs = _ssm_fwd(rest, bd_c, cd_c, *_scan_consts(a_re[0], a_im[0], log_dt[0], seg, False))

    vec = jnp.concatenate([gate, g_final.reshape(1, D_MODEL), jnp.concatenate([d_skip, b_glu], axis=1),
                           jnp.zeros((5, D_MODEL), F32)], axis=0)
    hsel = jnp.repeat(jnp.eye(HEADS, dtype=F32), HEAD_DIM, axis=1)
    (dx2, dga, dgb, do, dza, dzb, dys, dlt_t, g_out, g_ua, g_ub, g_glu, vmid) = _mid(
        o, rest, ys0, x2d, tgt, wmid, vec, hsel)

    lse_t = jnp.transpose(lse_pc.reshape(s, 4 // ATT_PAIRS, LANES)[:, :, :2 * ATT_PAIRS], (1, 2, 0))
    lse5 = lse_t.reshape(4, 2, n_att, 1, t_att)
    dlt5 = dlt_t.reshape(4, 2, n_att, 1, t_att)
    dq, dk, dv, dfk, dfq = _attn_bwd(qkv, do, lse5, dlt5, frow5, fpc)
    du, g_bd, g_cdt, da8 = _ssm_bwd(dys, xs, rest, bd_c, cd_c, *_scan_consts(a_re[0], a_im[0], log_dt[0], seg, True),
                                    d_skip)
    df, dbf8 = _dfcum(dfk, dfq, rest, bf128, selp.T)

    gq, gk, gv, gga, ggb, gza, gu, gzb, gf = _grad_w_rows(h, [dq, dk, dv, dga, dgb, dza, du, dzb, df])
    g_in_t = jnp.concatenate([gq, gk, gv, gf[:HEADS], gza, gu, gzb, gga, ggb], axis=0)

    def shard_cols(g, j):
        n = g.shape[1] // 4
        return g[:, j * n:(j + 1) * n]

    def shard_rows(g, j):
        n = g.shape[0] // 4
        return g[j * n:(j + 1) * n]

    def halves4(a):
        return a.reshape((4, 2, a.shape[1] // 2) + a.shape[2:])

    gt_pack = halves4(jnp.stack([
        jnp.pad(g_in_t[j * nshard:(j + 1) * nshard].astype(BF16), ((0, SHARD_ROWS - nshard), (0, 0)))
        for j in range(4)]))
    gm_pack = halves4(jnp.stack([
        jnp.concatenate([shard_rows(g_glu, j).reshape(-1), shard_cols(g_ua, j).reshape(-1),
                         shard_cols(g_ub, j).reshape(-1), shard_rows(g_out, j).reshape(-1)]).astype(BF16)
        .reshape(-1, LANES) for j in range(4)]))
    recv_in, recv_misc = _swap_sibling([gt_pack, gm_pack], "pair_swap_weight_grads", other_half=True)
    own_in = lax.dynamic_index_in_dim(gt_pack, ci, axis=1, keepdims=False)
    own_misc = lax.dynamic_index_in_dim(gm_pack, ci, axis=1, keepdims=False)
    pair_in = _pair_add(own_in, recv_in, "pair_add_w_in")
    pair_misc = _pair_add(own_misc, recv_misc, "pair_add_misc")

    grad_x, vdh, parts_in, parts_misc = _dh(dq, dk, dv, dga, dgb, dza, du, dzb, df, wqkv_t, wrest_t, x2d, dx2, gs,
                                            [pair_in, pair_misc])
    parts_in = lax.dynamic_update_slice_in_dim(parts_in, lax.dynamic_slice_in_dim(pair_in, chip, 1, 0), chip, 0)
    parts_misc = lax.dynamic_update_slice_in_dim(parts_misc, lax.dynamic_slice_in_dim(pair_misc, chip, 1, 0), chip, 0)
    half_in, half_misc = _sum4(parts_in, "sum4_w_in"), _sum4(parts_misc, "sum4_misc")
    sib_in, sib_misc = _swap_sibling([half_in, half_misc], "swap_weight_grads")

    def both_halves(mine, theirs):
        return jnp.concatenate([jnp.where(ci == 0, mine, theirs), jnp.where(ci == 0, theirs, mine)], axis=0)

    tot_in, tot_misc = both_halves(half_in, sib_in), both_halves(half_misc, sib_misc)
    g_glu_s, g_ua_s, g_ub_s, g_out_s = _unpack(tot_misc, misc_shapes)

    dgs, dshift = vdh[0:1], vdh[1:2]
    dmod = jnp.concatenate([dshift, dgs * g_norm, vmid[1:2]], axis=1)
    da = jnp.sum(da8, axis=0)
    g_bd = g_bd.reshape(GROUPS, GCH, 2 * STATE)
    g_cdt = g_cdt.reshape(GROUPS, GCH, 2 * STATE)
    g_bbr = jnp.swapaxes(g_bd[:, :, :STATE], 1, 2)
    g_bbi = jnp.swapaxes(g_bd[:, :, STATE:], 1, 2)
    g_cre = g_cdt[:, :, :STATE]
    g_cim = -g_cdt[:, :, STATE:]
    small_shapes = [(1,), (3 * D_MODEL,), (D_MODEL,), (HEADS,), (GROUPS, STATE), (GROUPS, STATE),
                    (GROUPS, STATE, GCH), (GROUPS, STATE, GCH), (GROUPS, GCH, STATE), (GROUPS, GCH, STATE),
                    (S5_W,), (S5_W,), (D_MODEL,)]
    small = _pack([vmid[3, 0:1], dmod, dgs * (1.0 + scale), dbf8[0, :HEADS], da[:NSTATE], da[NSTATE:],
                   g_bbr, g_bbi, g_cre, g_cim, vmid[2, :S5_W], vmid[2, S5_W:], vmid[0]])
    small_all, small_sum = _allgather8(small, "gather_small_grads")
    (loss_s, g_b_ada, g_g_norm, g_b_f, g_abr, g_abi, g_bbr_s, g_bbi_s, g_c_re, g_c_im, g_d_skip, g_b_glu,
     g_g_final) = _unpack(small_sum, small_shapes)
    loss = loss_s[0]
    dmod_all = _unpack(small_all, small_shapes)[1]
    dmod_cols = lax.dynamic_slice_in_dim(dmod_all, chip * ncol, ncol, axis=1)
    g_w_ada = _grad_w_ada(c_all, dmod_cols)
    _, ssm_vjp = jax.vjp(_ssm_block_params, a_re[0], a_im[0], log_dt[0], b_re[0], b_im[0])
    g_a_re, g_a_im, g_log_dt, g_b_re, g_b_im = ssm_vjp((g_abr, g_abi, g_bbr_s, g_bbi_s))

    def adam(name, w, g, m, v):
        shape = w.shape
        total = math.prod(shape)
        if len(shape) > 1 and shape[-1] >= LANES:
            cols = shape[-1]
        elif total % LANES == 0:
            cols = LANES
        else:
            cols = total
        two = lambda a: a.reshape(-1, cols)
        d, nm, nv = _adamw(two(w), two(g), two(m), two(v), "adamw_" + name)
        return g.reshape(shape), d.reshape(shape), nm.reshape(shape), nv.reshape(shape)

    back = lambda a: jnp.swapaxes(a, 0, 1)[None]
    d_in_t, nm_in_t, nv_in_t = _adamw(w_in_t, tot_in, m_in_t, v_in_t, "adamw_w_in")
    res_w_in = (back(tot_in[:nshard]), back(d_in_t), back(nm_in_t), back(nv_in_t))

    res = [
        adam("w_ada", w_ada, g_w_ada, m_w_ada, v_w_ada),
        adam("b_ada", b_ada, g_b_ada, m_b_ada, v_b_ada),
        adam("g_norm", g_norm, g_g_norm, m_g_norm, v_g_norm),
        res_w_in,
        adam("b_f", b_f, g_b_f, m_b_f, v_b_f),
        adam("a_re", a_re, g_a_re, m_a_re, v_a_re),
        adam("a_im", a_im, g_a_im, m_a_im, v_a_im),
        adam("log_dt", log_dt, g_log_dt, m_log_dt, v_log_dt),
        adam("b_re", b_re, g_b_re, m_b_re, v_b_re),
        adam("b_im", b_im, g_b_im, m_b_im, v_b_im),
        adam("c_re", c_re, g_c_re, m_c_re, v_c_re),
        adam("c_im", c_im, g_c_im, m_c_im, v_c_im),
        adam("d_skip", d_skip, g_d_skip, m_d_skip, v_d_skip),
        adam("w_glu", w_glu, g_glu_s, m_w_glu, v_w_glu),
        adam("b_glu", b_glu, g_b_glu, m_b_glu, v_b_glu),
        adam("w_up_a", w_up_a, g_ua_s, m_w_up_a, v_w_up_a),
        adam("w_up_b", w_up_b, g_ub_s, m_w_up_b, v_w_up_b),
        adam("w_out", w_out, g_out_s, m_w_out, v_w_out),
        adam("g_final", g_final, g_g_final, m_g_final, v_g_final),
    ]
    grads = [r[0] for r in res]
    deltas = [r[1] for r in res]
    new_m = [r[2] for r in res]
    new_v = [r[3] for r in res]
    return (loss, grad_x[None], *grads, *deltas, *new_m, *new_v)
```

```python
import math

import jax
import jax.numpy as jnp
from jax import lax
from jax.experimental import pallas as pl
from jax.experimental.pallas import tpu as pltpu

F32 = jnp.float32
BF16 = jnp.bfloat16
HI = lax.Precision.HIGHEST
MESH = pl.DeviceIdType.MESH

D_MODEL = 1024
HEADS = 8
HEAD_DIM = 64
FOX_W = 512
S5_W = 512
GROUPS = 32
STATE = 64
GCH = 16
NSTATE = GROUPS * STATE
EPS = 1e-6
NEG = -1e30

ADAM_LR = 0.001
ADAM_B1 = 0.9
ADAM_B2 = 0.999
ADAM_EPS = 1e-08
ADAM_WD = 0.01
ADAM_STEP = 10

VMEM_LIMIT = 56 * 1024 * 1024
LANES = 128

TM = 256
TM_PROJ = 512
T_ATT = 512
ATT_CHUNK = 32
ATT_PAIRS = 4
TB_SSM = 512
TK_ACC = 512
TB_CUM = 256
SHARD_ROWS = 1312

O_Q, O_K, O_V, O_F, O_ZA, O_U, O_ZB, O_GA, O_GB, O_END = 0, 512, 1024, 1536, 1544, 2056, 2568, 3080, 4104, 5128
REST_W = 3712
R_GA, R_GB, R_ZA, R_U, R_ZB, R_F = 0, 1024, 2048, 2560, 3072, 3584


def _cparams(sem=None):
    kw = dict(vmem_limit_bytes=VMEM_LIMIT)
    if sem is not None:
        kw["dimension_semantics"] = sem
    return pltpu.CompilerParams(**kw)


def _const(shape):
    nd = len(shape)
    return pl.BlockSpec(shape, lambda *_: (0,) * nd, pipeline_mode=pl.Buffered(1))


def _dot(a, b, precision=None):
    return jnp.dot(a, b, preferred_element_type=F32, precision=precision)


def _dot_nt(a, b):
    return lax.dot_general(a, b, (((1,), (1,)), ((), ())), preferred_element_type=F32)


def _dot_tn(a, b, precision=None):
    return lax.dot_general(a, b, (((0,), (0,)), ((), ())), preferred_element_type=F32, precision=precision)


def _sigmoid(z):
    return 1.0 / (1.0 + jnp.exp(-z))


def _split3(x):
    hi = x.astype(BF16)
    r1 = x - hi.astype(F32)
    mid = r1.astype(BF16)
    lo = (r1 - mid.astype(F32)).astype(BF16)
    return hi, mid, lo


def _dot_sel(sel, x, terms=3):
    s16 = sel.astype(BF16)
    return sum(_dot(s16, part) for part in _split3(x)[:terms])


def _dot_by_sel(x, sel):
    s16 = sel.astype(BF16)
    return sum(_dot(part, s16) for part in _split3(x))


def _allgather8(xs, name):
    rows = xs.shape[0]

    def body(x_ref, out_ref, sum_ref, send_sems, recv_sems, local_sem):
        x, y, c = lax.axis_index("x"), lax.axis_index("y"), lax.axis_index("c")
        me, sibling = (x, y, c), (x, y, 1 - c)
        chips = [(1 - x, y), (x, 1 - y), (1 - x, 1 - y)]

        def slot(px, py, pc):
            return out_ref.at[4 * px + 2 * py + pc]

        def copy(k, block, to, src=None):
            return pltpu.make_async_remote_copy(
                src_ref=slot(*block) if src is None else src, dst_ref=slot(*block),
                send_sem=send_sems.at[k], recv_sem=recv_sems.at[k], device_id=to, device_id_type=MESH)

        mine = pltpu.make_async_copy(x_ref, slot(*me), local_sem)
        mine.start()
        first = [copy(0, me, sibling, src=x_ref)]
        first += [copy(1 + j, me, (*chip, c), src=x_ref) for j, chip in enumerate(chips)]
        for cp in first:
            cp.start()
        passed = [copy(4 + j, (*chip, c), sibling) for j, chip in enumerate(chips)]
        for j, chip in enumerate(chips):
            copy(1 + j, (*chip, c), me).wait_recv()
            passed[j].start()
        copy(0, sibling, me).wait_recv()
        for j, chip in enumerate(chips):
            copy(4 + j, (*chip, 1 - c), me).wait_recv()
        for cp in first + passed:
            cp.wait_send()
        mine.wait()
        acc = out_ref[0]
        for d in range(1, 8):
            acc = acc + out_ref[d]
        sum_ref[...] = acc

    return pl.pallas_call(
        body, name=name,
        out_shape=(jax.ShapeDtypeStruct((8, rows, LANES), F32), jax.ShapeDtypeStruct((rows, LANES), F32)),
        in_specs=[pl.BlockSpec(memory_space=pltpu.VMEM)],
        out_specs=(pl.BlockSpec(memory_space=pltpu.VMEM), pl.BlockSpec(memory_space=pltpu.VMEM)),
        scratch_shapes=[pltpu.SemaphoreType.DMA((7,)), pltpu.SemaphoreType.DMA((7,)), pltpu.SemaphoreType.DMA],
        compiler_params=_cparams(),
    )(xs)


def _gather_shards(srcs, name):
    na = len(srcs)

    def body(*refs):
        src_refs, out_refs = refs[:na], refs[na:2 * na]
        send_sems, recv_sems = refs[2 * na:]
        x, y, c = lax.axis_index("x"), lax.axis_index("y"), lax.axis_index("c")
        sibling = (x, y, 1 - c)
        peers = [(1 - x, y), (x, 1 - y), (1 - x, 1 - y)]

        def copy(a, k, src, slot, which, to):
            return pltpu.make_async_remote_copy(
                src_ref=src, dst_ref=out_refs[a].at[slot, which],
                send_sem=send_sems.at[a * 6 + k], recv_sem=recv_sems.at[a * 6 + k],
                device_id=to, device_id_type=MESH)

        mine = 2 * x + y
        first = [copy(a, k, src_refs[a].at[c], mine, c, (px, py, c))
                 for a in range(na) for k, (px, py) in enumerate(peers)]
        for cp in first:
            cp.start()
        passed = []
        for a in range(na):
            for k, (px, py) in enumerate(peers):
                slot = 2 * px + py
                landed = out_refs[a].at[slot, c]
                copy(a, k, landed, slot, c, (px, py, c)).wait_recv()
                fwd = copy(a, 3 + k, landed, slot, c, sibling)
                fwd.start()
                passed.append(fwd)
        for a in range(na):
            for k, (px, py) in enumerate(peers):
                slot = 2 * px + py
                copy(a, 3 + k, out_refs[a].at[slot, 1 - c], slot, 1 - c, sibling).wait_recv()
        for cp in first + passed:
            cp.wait_send()

    anyspace = pl.BlockSpec(memory_space=pl.ANY)
    return pl.pallas_call(
        body, name=name,
        out_shape=tuple(jax.ShapeDtypeStruct((4,) + tuple(a.shape), a.dtype) for a in srcs),
        in_specs=[anyspace] * na, out_specs=(anyspace,) * na,
        scratch_shapes=[pltpu.SemaphoreType.DMA((6 * na,)), pltpu.SemaphoreType.DMA((6 * na,))],
        compiler_params=_cparams(),
    )(*srcs)


def _swap_sibling(srcs, name, other_half=False):
    na = len(srcs)

    def body(*refs):
        src_refs, out_refs = refs[:na], refs[na:2 * na]
        send_sems, recv_sems = refs[2 * na:]
        x, y, c = lax.axis_index("x"), lax.axis_index("y"), lax.axis_index("c")
        copies = [pltpu.make_async_remote_copy(
            src_ref=src_refs[a].at[:, 1 - c] if other_half else src_refs[a],
            dst_ref=out_refs[a], send_sem=send_sems.at[a], recv_sem=recv_sems.at[a],
            device_id=(x, y, 1 - c), device_id_type=MESH) for a in range(na)]
        for cp in copies:
            cp.start()
        for cp in copies:
            cp.wait()

    def out_of(a):
        shape = (a.shape[0],) + tuple(a.shape[2:]) if other_half else a.shape
        return jax.ShapeDtypeStruct(shape, a.dtype)

    anyspace = pl.BlockSpec(memory_space=pl.ANY)
    return pl.pallas_call(
        body, name=name, out_shape=tuple(out_of(a) for a in srcs),
        in_specs=[anyspace] * na, out_specs=(anyspace,) * na,
        scratch_shapes=[pltpu.SemaphoreType.DMA((na,)), pltpu.SemaphoreType.DMA((na,))],
        compiler_params=_cparams(),
    )(*srcs)


def _mod_cols(c_all, w, b):
    n = w.shape[1]

    def body(c_ref, w_ref, b_ref, o_ref):
        o_ref[...] = _dot(c_ref[...], w_ref[...], HI) + b_ref[...]

    return pl.pallas_call(
        body, name="mod_cols", out_shape=jax.ShapeDtypeStruct((8, n), F32),
        compiler_params=_cparams())(c_all, w, b)


def _grad_w_ada(c_all, dmod_cols):
    n = dmod_cols.shape[1]

    def body(c_ref, d_ref, o_ref):
        o_ref[...] = _dot_tn(c_ref[...], d_ref[...], HI)

    return pl.pallas_call(
        body, name="grad_w_ada", out_shape=jax.ShapeDtypeStruct((D_MODEL, n), F32),
        compiler_params=_cparams())(c_all, dmod_cols)


def _prenorm_proj(x, gs, shift, wqkv_t, wrest_t):
    s = x.shape[0]
    tm = min(TM_PROJ, s)
    nq, nr = wqkv_t.shape[0], wrest_t.shape[0]

    def body(x_ref, gs_ref, sh_ref, wq_ref, wr_ref, h_ref, qkv_ref, rest_ref):
        xv = x_ref[...]
        r = lax.rsqrt(jnp.mean(xv * xv, axis=-1, keepdims=True) + EPS)
        h = (xv * r * gs_ref[...] + sh_ref[...]).astype(BF16)
        h_ref[...] = h
        qkv_ref[...] = _dot_nt(h, wq_ref[...]).astype(BF16)
        rest_ref[...] = _dot_nt(h, wr_ref[...])

    def rows(width):
        return pl.BlockSpec((tm, width), lambda i: (i, 0))

    return pl.pallas_call(
        body, name="prenorm_proj", grid=(s // tm,),
        in_specs=[rows(D_MODEL), _const((1, D_MODEL)), _const((1, D_MODEL)), _const((nq, D_MODEL)),
                  _const((nr, D_MODEL))],
        out_specs=(rows(D_MODEL), rows(nq), rows(nr)),
        out_shape=(jax.ShapeDtypeStruct((s, D_MODEL), BF16), jax.ShapeDtypeStruct((s, nq), BF16),
                   jax.ShapeDtypeStruct((s, nr), F32)),
        compiler_params=_cparams(("parallel",)))(x, gs, shift, wqkv_t, wrest_t)


def _grad_w_rows(h, ds):
    s = h.shape[0]
    tk = min(TK_ACC, s)
    nd = len(ds)
    widths = [d.shape[1] for d in ds]

    def body(*refs):
        h_ref, d_refs = refs[0], refs[1:1 + nd]
        out_refs, accs = refs[1 + nd:1 + 2 * nd], refs[1 + 2 * nd:]
        step = pl.program_id(0)

        @pl.when(step == 0)
        def _():
            for acc in accs:
                acc[...] = jnp.zeros_like(acc)

        hv = h_ref[...]
        for d_ref, acc in zip(d_refs, accs):
            acc[...] += _dot_tn(d_ref[...], hv)

        @pl.when(step == s // tk - 1)
        def _():
            for acc, out in zip(accs, out_refs):
                pltpu.sync_copy(acc, out)

    anyspace = pl.BlockSpec(memory_space=pl.ANY)
    return pl.pallas_call(
        body, name="grad_w_in", grid=(s // tk,),
        in_specs=[pl.BlockSpec((tk, D_MODEL), lambda k: (k, 0))]
                 + [pl.BlockSpec((tk, w), lambda k: (k, 0)) for w in widths],
        out_specs=(anyspace,) * nd,
        out_shape=tuple(jax.ShapeDtypeStruct((w, D_MODEL), F32) for w in widths),
        scratch_shapes=[pltpu.VMEM((w, D_MODEL), F32) for w in widths],
        compiler_params=_cparams(("arbitrary",)))(h, *ds)


def _head_pair_selector():
    rows = jnp.arange(LANES)[:, None]
    cols = jnp.arange(4 * LANES)[None, :]
    return ((rows < HEADS) & (cols == (rows // 2) * LANES + rows % 2)).astype(F32)


BIAS_ONES = 32


def _bias_selectors():
    rows = jnp.arange(LANES)[None, :, None]
    cols = jnp.arange(LANES)[None, None, :]
    term = jnp.arange(3)[:, None, None]
    return ((rows < HEADS) & (cols == 3 * rows + term)).astype(F32)


def _fcum(rest, bf128, selp, selk):
    s = rest.shape[0]
    tb = min(TB_CUM, s)

    def body(fz_ref, bf_ref, sel_ref, selk_ref, fpc_ref, ft_ref, kb_ref, carry_ref):
        @pl.when(pl.program_id(0) == 0)
        def _():
            carry_ref[...] = jnp.zeros_like(carry_ref)

        z = fz_ref[...] + bf_ref[...]
        logf = jnp.minimum(z, 0.0) - jnp.log(1.0 + jnp.exp(-jnp.abs(z)))
        r = lax.broadcasted_iota(jnp.int32, (tb, tb), 0)
        c = lax.broadcasted_iota(jnp.int32, (tb, tb), 1)
        tri = (c <= r).astype(F32)
        f = _dot_sel(tri, logf) + carry_ref[0:1, :]
        carry_ref[0:1, :] = f[tb - 1:tb, :]
        fpc_ref[...] = _dot_by_sel(f, sel_ref[...])
        ft_ref[...] = jnp.transpose(f)[0:HEADS, :]
        lane = lax.broadcasted_iota(jnp.int32, (tb, LANES), 1)
        ones = ((lane >= BIAS_ONES) & (lane < BIAS_ONES + 3 * HEADS)).astype(F32)
        terms = sum(_dot(part, selk_ref[j].astype(BF16)) for j, part in enumerate(_split3(-f)))
        kb_ref[...] = (terms + ones).astype(BF16)

    return pl.pallas_call(
        body, name="forget_cumsum", grid=(s // tb,),
        in_specs=[pl.BlockSpec((tb, LANES), lambda i: (i, R_F // LANES)), _const((1, LANES)), _const((LANES, 4 * LANES)),
                  _const((3, LANES, LANES))],
        out_specs=(pl.BlockSpec((tb, 4 * LANES), lambda i: (i, 0)), pl.BlockSpec((HEADS, tb), lambda i: (0, i)),
                   pl.BlockSpec((tb, LANES), lambda i: (i, 0))),
        out_shape=(jax.ShapeDtypeStruct((s, 4 * LANES), F32), jax.ShapeDtypeStruct((HEADS, s), F32),
                   jax.ShapeDtypeStruct((s, LANES), BF16)),
        scratch_shapes=[pltpu.VMEM((8, LANES), F32)],
        compiler_params=_cparams(("arbitrary",)))(rest, bf128, selp, selk)


def _dfcum(dfk, dfq, rest, bf128, selq):
    s = rest.shape[0]
    tb = min(TB_CUM, s)
    nb = s // tb

    def body(dk_ref, dq_ref, fz_ref, bf_ref, sel_ref, df_ref, dbf_ref, carry_ref):
        @pl.when(pl.program_id(0) == 0)
        def _():
            carry_ref[...] = jnp.zeros_like(carry_ref)
            dbf_ref[...] = jnp.zeros_like(dbf_ref)

        d = _dot_by_sel(dk_ref[...] + dq_ref[...], sel_ref[...])
        r = lax.broadcasted_iota(jnp.int32, (tb, tb), 0)
        c = lax.broadcasted_iota(jnp.int32, (tb, tb), 1)
        triu = (c >= r).astype(F32)
        dlogf = _dot_sel(triu, d) + carry_ref[0:1, :]
        carry_ref[0:1, :] = dlogf[0:1, :]
        z = fz_ref[...] + bf_ref[...]
        df = dlogf * (1.0 / (1.0 + jnp.exp(z)))
        df_ref[...] = df.astype(BF16)
        dbf_ref[0:1, :] += jnp.sum(df, axis=0, keepdims=True)

    return pl.pallas_call(
        body, name="forget_grad", grid=(nb,),
        in_specs=[pl.BlockSpec((tb, 4 * LANES), lambda i: (nb - 1 - i, 0)),
                  pl.BlockSpec((tb, 4 * LANES), lambda i: (nb - 1 - i, 0)),
                  pl.BlockSpec((tb, LANES), lambda i: (nb - 1 - i, R_F // LANES)),
                  _const((1, LANES)), _const((4 * LANES, LANES))],
        out_specs=(pl.BlockSpec((tb, LANES), lambda i: (nb - 1 - i, 0)), pl.BlockSpec((8, LANES), lambda i: (0, 0))),
        out_shape=(jax.ShapeDtypeStruct((s, LANES), BF16), jax.ShapeDtypeStruct((8, LANES), F32)),
        scratch_shapes=[pltpu.VMEM((8, LANES), F32)],
        compiler_params=_cparams(("arbitrary",)))(dfk, dfq, rest, bf128, selq)


def _scaled(q):
    return (q.astype(F32) * (HEAD_DIM ** -0.5)).astype(BF16)


def _attn_fwd(qkv, kbias, fpc):
    s = qkv.shape[0]
    t = min(T_ATT, s)
    n = s // t
    ch = min(ATT_CHUNK, t)
    wide = 2 * LANES
    pairs = ATT_PAIRS
    width = pairs * LANES
    groups = 4 // pairs

    def body(q_ref, k_ref, v_ref, kb_ref, fc_ref, o_ref, lse_ref, s_scr, p_scr, m_scr, a_scr, acc_scr):
        i = pl.program_id(1)
        g = pl.program_id(0)
        lane = lax.broadcasted_iota(jnp.int32, (t, LANES), 1)
        first = lane < HEAD_DIM
        ones_col = ((lane == 0).astype(BF16), (lane == 1).astype(BF16))
        m_scr[...] = jnp.full(m_scr.shape, NEG, F32)
        acc_scr[...] = jnp.zeros_like(acc_scr)
        qm = []
        for pp in range(pairs):
            q = _scaled(q_ref[:, pp * LANES:(pp + 1) * LANES])
            zq = jnp.zeros_like(q)
            for hh in range(2):
                head = 2 * (g * pairs + pp) + hh
                fq = _split3(fc_ref[:, pp * LANES + hh:pp * LANES + hh + 1])
                bias = jnp.where((lane >= 3 * head) & (lane < 3 * head + 3), 1.0, 0.0).astype(BF16)
                for term in range(3):
                    bias = jnp.where(lane == BIAS_ONES + 3 * head + term, fq[term], bias)
                qh = jnp.where(first, q, zq) if hh == 0 else jnp.where(first, zq, q)
                qm.append(jnp.concatenate([qh, bias], axis=1))

        def step(j, masked):
            r0 = pl.multiple_of(j * t, t)
            vaug = []
            kbias_blk = kb_ref[pl.ds(r0, t), :]
            for pp in range(pairs):
                kb = jnp.concatenate([k_ref[pl.ds(r0, t), pp * LANES:(pp + 1) * LANES], kbias_blk], axis=1)
                vb = v_ref[pl.ds(r0, t), pp * LANES:(pp + 1) * LANES]
                zv = jnp.zeros_like(vb)
                vaug += [jnp.concatenate([jnp.where(first, vb, zv), ones_col[0]], axis=1),
                         jnp.concatenate([jnp.where(first, zv, vb), ones_col[1]], axis=1)]
                for hh in range(2):
                    s_scr[2 * pp + hh] = _dot_nt(qm[2 * pp + hh], kb)
            pv = []
            for hd in range(2 * pairs):
                for c in range(t // ch):
                    rows = pl.ds(c * ch, ch)
                    hi = min(t, (c * ch // LANES + 1) * LANES) if masked else t
                    sc = s_scr[hd, rows, 0:hi]
                    if masked:
                        rq = c * ch + lax.broadcasted_iota(jnp.int32, (ch, hi), 0)
                        ck = lax.broadcasted_iota(jnp.int32, (ch, hi), 1)
                        sc = jnp.where(ck <= rq, sc, NEG)
                    m_old = m_scr[hd, rows, :]
                    m_new = jnp.maximum(m_old, jnp.max(sc, axis=1, keepdims=True))
                    p_scr[hd, rows, 0:hi] = jnp.exp(sc - m_new).astype(BF16)
                    if hi < t:
                        p_scr[hd, rows, hi:t] = jnp.zeros((ch, t - hi), BF16)
                    a_scr[hd, rows, :] = jnp.exp(m_old - m_new)
                    m_scr[hd, rows, :] = m_new
                pv.append(_dot(p_scr[hd], vaug[hd]))
            for pp in range(pairs):
                a0, a1 = a_scr[2 * pp], a_scr[2 * pp + 1]
                alpha = jnp.concatenate([jnp.where(first, a0, a1), jnp.where(lane == 0, a0, a1)], axis=1)
                acc_scr[pp] = acc_scr[pp] * alpha + pv[2 * pp] + pv[2 * pp + 1]
            return 0

        lax.fori_loop(0, i, lambda j, _: step(j, False), 0)
        step(i, True)
        lse = jnp.zeros((t, LANES), F32)
        for pp in range(pairs):
            l0 = acc_scr[pp, :, LANES:LANES + 1]
            l1 = acc_scr[pp, :, LANES + 1:LANES + 2]
            o_ref[:, pp * LANES:(pp + 1) * LANES] = acc_scr[pp, :, 0:LANES] * jnp.where(first, 1.0 / l0, 1.0 / l1)
            lse = jnp.where(lane == 2 * pp, m_scr[2 * pp] + jnp.log(l0), lse)
            lse = jnp.where(lane == 2 * pp + 1, m_scr[2 * pp + 1] + jnp.log(l1), lse)
        lse_ref[...] = lse

    blk = pl.BlockSpec((t, width), lambda g, i: (i, g))
    return pl.pallas_call(
        body, name="attn_fwd", grid=(groups, n),
        in_specs=[blk,
                  pl.BlockSpec((s, width), lambda g, i: (0, groups + g)),
                  pl.BlockSpec((s, width), lambda g, i: (0, 2 * groups + g)),
                  pl.BlockSpec((s, LANES), lambda g, i: (0, 0)),
                  blk],
        out_specs=(blk, pl.BlockSpec((t, LANES), lambda g, i: (i, g))),
        out_shape=(jax.ShapeDtypeStruct((s, FOX_W), F32), jax.ShapeDtypeStruct((s, groups * LANES), F32)),
        scratch_shapes=[pltpu.VMEM((2 * pairs, t, t), F32), pltpu.VMEM((2 * pairs, t, t), BF16),
                        pltpu.VMEM((2 * pairs, t, 1), F32), pltpu.VMEM((2 * pairs, t, 1), F32),
                        pltpu.VMEM((pairs, t, wide), F32)],
        compiler_params=_cparams(("parallel", "arbitrary")))(qkv, qkv, qkv, kbias, fpc)


def _attn_bwd(qkv, do, dlt5, kbias, qbias):
    s = qkv.shape[0]
    t = min(T_ATT, s)
    n = s // t
    wide = 2 * LANES

    ch = min(ATT_CHUNK, t)

    def body(q_ref, do_ref, k_ref, v_ref, dl_ref, kbias_ref, qbias_ref,
             dq_ref, dk_ref, dv_ref, dfk_ref, dfq_ref, dq_acc, st_scr, dp_scr, pt_scr, ds_scr, dk_acc, dv_acc):
        j = pl.program_id(1)
        hp = pl.program_id(0)

        @pl.when(j == 0)
        def _():
            dq_acc[...] = jnp.zeros_like(dq_acc)

        dk_acc[...] = jnp.zeros_like(dk_acc)
        dv_acc[...] = jnp.zeros_like(dv_acc)
        lane = lax.broadcasted_iota(jnp.int32, (t, LANES), 1)
        first = lane < HEAD_DIM
        ones_col = ((lane == 0).astype(BF16), (lane == 1).astype(BF16))
        kb = k_ref[...]
        vb = v_ref[...]
        zk = jnp.zeros_like(kb)
        kaug = (jnp.concatenate([jnp.where(first, kb, zk), ones_col[0]], axis=1),
                jnp.concatenate([jnp.where(first, zk, kb), ones_col[1]], axis=1))
        kb_bias = jnp.concatenate([kb, kbias_ref[...]], axis=1)
        own_lanes = []
        for hh in range(2):
            lo3 = 3 * (2 * hp + hh)
            own_lanes.append(((lane >= lo3) & (lane < lo3 + 3))
                             | ((lane >= BIAS_ONES + lo3) & (lane < BIAS_ONES + lo3 + 3)))

        def step(blocks, masked):
            chains = []
            for bi, i in enumerate(blocks):
                r0 = pl.multiple_of(i * t, t)
                qb = _scaled(q_ref[pl.ds(r0, t), :])
                dob = do_ref[pl.ds(r0, t), :]
                zq = jnp.zeros_like(qb)
                qm = (jnp.where(first, qb, zq), jnp.where(first, zq, qb))
                dom = (jnp.where(first, dob, zq), jnp.where(first, zq, dob))
                qbb = qbias_ref[pl.ds(r0, t), :]
                for hh in range(2):
                    q_bias = jnp.concatenate([qm[hh], jnp.where(own_lanes[hh], qbb, zq)], axis=1)
                    st_scr[2 * bi + hh] = _dot_nt(kb_bias, q_bias)
                    dp_scr[2 * bi + hh] = _dot_nt(vb, dom[hh])
                    chains.append((i, hh, qm[hh], dom[hh]))
            dq_add = [jnp.zeros((t, wide), F32) for _ in blocks]
            for cn, (i, hh, qmh, domh) in enumerate(chains):
                dl = dl_ref[0, hh, i]
                for c in range(t // ch):
                    rows = pl.ds(c * ch, ch)
                    lo = c * ch // LANES * LANES if masked else 0
                    st = st_scr[cn, rows, lo:t]
                    if masked:
                        rk = c * ch + lax.broadcasted_iota(jnp.int32, (ch, t - lo), 0)
                        cq = lo + lax.broadcasted_iota(jnp.int32, (ch, t - lo), 1)
                        st = jnp.where(rk <= cq, st, NEG)
                    pt = jnp.exp(st)
                    pt_scr[cn, rows, lo:t] = pt.astype(BF16)
                    ds_scr[cn, rows, lo:t] = (pt * (dp_scr[cn, rows, lo:t] - dl[:, lo:t])).astype(BF16)
                    if lo > 0:
                        pt_scr[cn, rows, 0:lo] = jnp.zeros((ch, lo), BF16)
                        ds_scr[cn, rows, 0:lo] = jnp.zeros((ch, lo), BF16)
                dsb = ds_scr[cn]
                dv_acc[...] += _dot(pt_scr[cn], domh)
                dk_acc[...] += _dot(dsb, jnp.concatenate([qmh, ones_col[hh]], axis=1))
                dq_add[cn // 2] = dq_add[cn // 2] + _dot_tn(dsb, kaug[hh])
            for bi, i in enumerate(blocks):
                dq_acc[pl.ds(pl.multiple_of(i * t, t), t), :] += dq_add[bi]
            return 0

        step([j], True)
        odd = (n - 1 - j) % 2
        lax.fori_loop(0, odd, lambda _, carry: step([j + 1], False), 0)
        first_pair = j + 1 + odd
        lax.fori_loop(0, (n - first_pair) // 2,
                      lambda p, _: step([first_pair + 2 * p, first_pair + 2 * p + 1], False), 0)
        dk_ref[...] = dk_acc[:, 0:LANES].astype(BF16)
        dv_ref[...] = dv_acc[...].astype(BF16)
        dfk_ref[...] = -dk_acc[:, LANES:wide]

        @pl.when(j == n - 1)
        def _():
            dq_ref[...] = (dq_acc[:, 0:LANES] * (HEAD_DIM ** -0.5)).astype(BF16)
            dfq_ref[...] = dq_acc[:, LANES:wide]

    stat = pl.BlockSpec((1, 2, n, 1, t), lambda h, j: (h, 0, 0, 0, 0))
    blk = pl.BlockSpec((t, LANES), lambda h, j: (j, h))
    full = pl.BlockSpec((s, LANES), lambda h, j: (0, h))
    return pl.pallas_call(
        body, name="attn_bwd", grid=(4, n),
        in_specs=[full, full,
                  pl.BlockSpec((t, LANES), lambda h, j: (j, 4 + h)),
                  pl.BlockSpec((t, LANES), lambda h, j: (j, 8 + h)),
                  stat, pl.BlockSpec((t, LANES), lambda h, j: (j, 0)), pl.BlockSpec((s, LANES), lambda h, j: (0, 0))],
        out_specs=(full, blk, blk, blk, full),
        out_shape=(jax.ShapeDtypeStruct((s, FOX_W), BF16), jax.ShapeDtypeStruct((s, FOX_W), BF16),
                   jax.ShapeDtypeStruct((s, FOX_W), BF16), jax.ShapeDtypeStruct((s, 4 * LANES), F32),
                   jax.ShapeDtypeStruct((s, 4 * LANES), F32)),
        scratch_shapes=[pltpu.VMEM((s, wide), F32), pltpu.VMEM((4, t, t), F32), pltpu.VMEM((4, t, t), F32),
                        pltpu.VMEM((4, t, t), BF16), pltpu.VMEM((4, t, t), BF16), pltpu.VMEM((t, wide), F32),
                        pltpu.VMEM((t, LANES), F32)],
        compiler_params=_cparams(("parallel", "arbitrary")))(qkv, do, qkv, qkv, dlt5, kbias, qbias)


def _ssm_block_params(a_re, a_im, log_dt, b_re, b_im):
    dt = jnp.exp(log_dt)[:, None]
    mag = jnp.exp(a_re * dt)
    ar = mag * jnp.cos(a_im * dt)
    ai = mag * jnp.sin(a_im * dt)
    den = a_re * a_re + a_im * a_im
    nr = ar - 1.0
    cr = (nr * a_re + ai * a_im) / den
    ci = (ai * a_re - nr * a_im) / den
    bbr = cr[:, :, None] * b_re - ci[:, :, None] * b_im
    bbi = cr[:, :, None] * b_im + ci[:, :, None] * b_re
    return ar, ai, bbr, bbi


def _block_diag(blocks):
    g, r, c = blocks.shape
    eye = jnp.eye(g, dtype=blocks.dtype)
    return (blocks[:, :, None, :] * eye[:, None, :, None]).reshape(g * r, g * c)


def _scan_consts(a_re, a_im, log_dt, seg, reverse):
    dt = jnp.exp(log_dt)[:, None]
    lr = (a_re * dt).reshape(1, NSTATE)
    li = (a_im * dt).reshape(1, NSTATE)
    if reverse:
        li = -li
    rows = jnp.arange(8, dtype=F32)[:, None]

    def power(k):
        mag = jnp.exp(k * lr)
        return mag * jnp.cos(k * li), mag * jnp.sin(k * li)

    tiles = list(power(1.0))
    for k in (1, 2, 4):
        keep = (rows < 8 - k) if reverse else (rows >= k)
        pr, pi_ = power(float(k * seg))
        tiles += [jnp.where(keep, pr, 0.0), jnp.where(keep, pi_, 0.0)]
    tiles += list(power(seg * ((8.0 - rows) if reverse else (rows + 1.0))))
    tiles = jnp.stack([jnp.broadcast_to(tl, (8, NSTATE)) for tl in tiles])
    steps = jnp.arange(seg, dtype=F32)[:, None]
    table = jnp.stack([jnp.broadcast_to(p[:, None, :], (seg, 8, NSTATE))
                       for p in power((seg - steps) if reverse else (steps + 1.0))])
    return tiles, table


_SCAN_W = 1024
_HALF_W = S5_W // 2
_HALF_S = NSTATE // 2


def _compact_diag(blocks_re, blocks_im):
    hg = GROUPS // 2
    return jnp.concatenate([_block_diag(b[h * hg:(h + 1) * hg]) for b in (blocks_re, blocks_im) for h in range(2)],
                           axis=1)


def _half_expand(v, w_ref, out_ref):
    for half in range(2):
        vh = v[:, half * _HALF_W:(half + 1) * _HALF_W]
        for part in range(2):
            c0 = part * NSTATE + half * _HALF_S
            out_ref[:, c0:c0 + _HALF_S] = _dot(vh, w_ref[:, c0:c0 + _HALF_S])


def _half_contract(x_ref, w_ref, half):
    out = None
    for part in range(2):
        r0 = part * NSTATE + half * _HALF_S
        term = _dot_nt(x_ref[:, r0:r0 + _HALF_S].astype(BF16), w_ref[:, r0:r0 + _HALF_S])
        out = term if out is None else out + term
    return out


def _half_outer(v, x_ref, acc_ref):
    for half in range(2):
        vh = v[:, half * _HALF_W:(half + 1) * _HALF_W]
        for part in range(2):
            c0 = part * NSTATE + half * _HALF_S
            acc_ref[:, c0:c0 + _HALF_S] += _dot_tn(vh, x_ref[:, c0:c0 + _HALF_S].astype(BF16))


def _segment_perm(tb):
    seg = tb // 8
    row = lax.broadcasted_iota(jnp.int32, (tb, tb), 0)
    col = lax.broadcasted_iota(jnp.int32, (tb, tb), 1)
    perm = (col == (row % 8) * seg + row // 8).astype(BF16)
    back = (col == (row % seg) * 8 + row // seg).astype(BF16)
    return perm, back


def _segment_ends(re, im, cf_ref, cb_ref, cr, ci, reverse):
    for n_, k in enumerate((1, 2, 4)):
        kr = cf_ref[2 + 2 * n_, :, cr]
        ki = cf_ref[3 + 2 * n_, :, cr]
        sr = pltpu.roll(re, 8 - k if reverse else k, 0)
        si = pltpu.roll(im, 8 - k if reverse else k, 0)
        re, im = re + kr * sr - ki * si, im + kr * si + ki * sr
    cbr, cbi = cb_ref[:, cr], cb_ref[:, ci]
    pr, pi_ = cf_ref[8, :, cr], cf_ref[9, :, cr]
    re, im = re + pr * cbr - pi_ * cbi, im + pr * cbi + pi_ * cbr
    edge = lax.broadcasted_iota(jnp.int32, re.shape, 0) == (7 if reverse else 0)
    in_r = jnp.where(edge, cbr, pltpu.roll(re, 7 if reverse else 1, 0))
    in_i = jnp.where(edge, cbi, pltpu.roll(im, 7 if reverse else 1, 0))
    out = slice(0, 1) if reverse else slice(7, 8)
    cb_ref[:, cr] = jnp.broadcast_to(re[out, :], re.shape)
    cb_ref[:, ci] = jnp.broadcast_to(im[out, :], im.shape)
    return in_r, in_i


def _ssm_fwd(rest, bd, cd, consts, table):
    s = rest.shape[0]
    tb = min(TB_SSM, s)
    seg = tb // 8
    ns2 = 2 * NSTATE

    def body(u_ref, bd_ref, cd_ref, cf_ref, tab_ref, y_ref, x_ref, cb_ref):
        @pl.when(pl.program_id(0) == 0)
        def _():
            cb_ref[...] = jnp.zeros_like(cb_ref)

        perm, back = _segment_perm(tb)
        _half_expand(_dot(perm, u_ref[...].astype(BF16)).astype(BF16), bd_ref, x_ref)
        for cc in range(NSTATE // _SCAN_W):
            cr = pl.ds(cc * _SCAN_W, _SCAN_W)
            ci = pl.ds(NSTATE + cc * _SCAN_W, _SCAN_W)
            ar, ai = cf_ref[0, :, cr], cf_ref[1, :, cr]

            def local(i, carry, cr=cr, ci=ci, ar=ar, ai=ai):
                re, im = carry
                rows = pl.ds(pl.multiple_of(i * 8, 8), 8)
                re, im = ar * re - ai * im + x_ref[rows, cr], ar * im + ai * re + x_ref[rows, ci]
                x_ref[rows, cr] = re
                x_ref[rows, ci] = im
                return re, im

            zero = jnp.zeros((8, _SCAN_W), F32)
            re, im = lax.fori_loop(0, seg, local, (zero, zero))
            in_r, in_i = _segment_ends(re, im, cf_ref, cb_ref, cr, ci, False)

            def fix(i, _, cr=cr, ci=ci, in_r=in_r, in_i=in_i):
                rows = pl.ds(pl.multiple_of(i * 8, 8), 8)
                tr, ti = tab_ref[0, i, :, cr], tab_ref[1, i, :, cr]
                x_ref[rows, cr] += tr * in_r - ti * in_i
                x_ref[rows, ci] += tr * in_i + ti * in_r
                return 0

            lax.fori_loop(0, seg, fix, 0)
        y_p = jnp.concatenate([_half_contract(x_ref, cd_ref, half) for half in range(2)], axis=1)
        y_ref[...] = _dot_sel(back, y_p, terms=2)

    return pl.pallas_call(
        body, name="ssm_fwd", grid=(s // tb,),
        in_specs=[pl.BlockSpec((tb, S5_W), lambda i: (i, R_U // S5_W)), _const((_HALF_W, ns2)), _const((_HALF_W, ns2)),
                  _const((10, 8, NSTATE)), _const((2, seg, 8, NSTATE))],
        out_specs=(pl.BlockSpec((tb, S5_W), lambda i: (i, 0)), pl.BlockSpec((tb, ns2), lambda i: (i, 0))),
        out_shape=(jax.ShapeDtypeStruct((s, S5_W), F32), jax.ShapeDtypeStruct((s, ns2), F32)),
        scratch_shapes=[pltpu.VMEM((8, ns2), F32)],
        compiler_params=_cparams(("arbitrary",)))(rest, bd, cd, consts, table)


def _ssm_bwd(dys, xs, rest, bd, cd, consts, table, dskip):
    s = dys.shape[0]
    tb = min(TB_SSM, s)
    seg = tb // 8
    nb = s // tb
    ns2 = 2 * NSTATE

    def body(dy_ref, x_ref, u_ref, bd_ref, cd_ref, cf_ref, tab_ref, dsk_ref, du_ref, gb_ref, gc_ref, da_ref,
             g_ref, cb_ref, acc_b, acc_c):
        step = pl.program_id(0)

        @pl.when(step == 0)
        def _():
            cb_ref[...] = jnp.zeros_like(cb_ref)
            acc_b[...] = jnp.zeros_like(acc_b)
            acc_c[...] = jnp.zeros_like(acc_c)
            da_ref[...] = jnp.zeros_like(da_ref)

        perm, back = _segment_perm(tb)
        dy = dy_ref[...]
        dy_p = _dot(perm, dy.astype(BF16)).astype(BF16)
        u_p = _dot(perm, u_ref[...].astype(BF16)).astype(BF16)
        _half_expand(dy_p, cd_ref, g_ref)
        for cc in range(NSTATE // _SCAN_W):
            cr = pl.ds(cc * _SCAN_W, _SCAN_W)
            ci = pl.ds(NSTATE + cc * _SCAN_W, _SCAN_W)
            ar, ai = cf_ref[0, :, cr], cf_ref[1, :, cr]

            def local(ii, carry, cr=cr, ci=ci, ar=ar, ai=ai):
                re, im = carry
                rows = pl.ds(pl.multiple_of((seg - 1 - ii) * 8, 8), 8)
                re, im = ar * re - ai * im + g_ref[rows, cr], ar * im + ai * re + g_ref[rows, ci]
                g_ref[rows, cr] = re
                g_ref[rows, ci] = im
                return re, im

            zero = jnp.zeros((8, _SCAN_W), F32)
            re, im = lax.fori_loop(0, seg, local, (zero, zero))
            in_r, in_i = _segment_ends(re, im, cf_ref, cb_ref, cr, ci, True)

            def fix(ii, carry, cr=cr, ci=ci, in_r=in_r, in_i=in_i):
                nr, ni, acr, aci = carry
                i = seg - 1 - ii
                rows = pl.ds(pl.multiple_of(i * 8, 8), 8)
                tr, ti = tab_ref[0, i, :, cr], tab_ref[1, i, :, cr]
                gr = g_ref[rows, cr] + tr * in_r - ti * in_i
                gi = g_ref[rows, ci] + tr * in_i + ti * in_r
                g_ref[rows, cr] = gr
                g_ref[rows, ci] = gi
                xr, xi = x_ref[rows, cr], x_ref[rows, ci]
                return gr, gi, acr + nr * xr + ni * xi, aci + ni * xr - nr * xi

            _, _, acr, aci = lax.fori_loop(0, seg, fix, (in_r, in_i, zero, zero))
            da_ref[:, cr] += acr
            da_ref[:, ci] += aci
        du_p = jnp.concatenate([_half_contract(g_ref, bd_ref, half) for half in range(2)], axis=1)
        du_ref[...] = (_dot_sel(back, du_p, terms=2) + dy * dsk_ref[...]).astype(BF16)
        _half_outer(u_p, g_ref, acc_b)
        _half_outer(dy_p, x_ref, acc_c)

        @pl.when(step == nb - 1)
        def _():
            for g in range(GROUPS):
                src = slice((g % (GROUPS // 2)) * GCH, (g % (GROUPS // 2) + 1) * GCH)
                dst = slice(g * GCH, (g + 1) * GCH)
                for part in range(2):
                    cols = slice(part * NSTATE + g * STATE, part * NSTATE + (g + 1) * STATE)
                    gb_ref[dst, part * STATE:(part + 1) * STATE] = acc_b[src, cols]
                    gc_ref[dst, part * STATE:(part + 1) * STATE] = acc_c[src, cols]

    rev = lambda i: (nb - 1 - i, 0)
    small = pl.BlockSpec((S5_W, 2 * STATE), lambda i: (0, 0))
    return pl.pallas_call(
        body, name="ssm_bwd", grid=(nb,),
        in_specs=[pl.BlockSpec((tb, S5_W), rev), pl.BlockSpec((tb, ns2), rev),
                  pl.BlockSpec((tb, S5_W), lambda i: (nb - 1 - i, R_U // S5_W)),
                  _const((_HALF_W, ns2)), _const((_HALF_W, ns2)), _const((10, 8, NSTATE)), _const((2, seg, 8, NSTATE)),
                  _const((1, S5_W))],
        out_specs=(pl.BlockSpec((tb, S5_W), rev), small, small, pl.BlockSpec((8, ns2), lambda i: (0, 0))),
        out_shape=(jax.ShapeDtypeStruct((s, S5_W), BF16), jax.ShapeDtypeStruct((S5_W, 2 * STATE), F32),
                   jax.ShapeDtypeStruct((S5_W, 2 * STATE), F32), jax.ShapeDtypeStruct((8, ns2), F32)),
        scratch_shapes=[pltpu.VMEM((tb, ns2), F32), pltpu.VMEM((8, ns2), F32),
                        pltpu.VMEM((_HALF_W, ns2), F32), pltpu.VMEM((_HALF_W, ns2), F32)],
        compiler_params=_cparams(("arbitrary",)))(dys, xs, rest, bd, cd, consts, table, dskip)


_GELU_C = math.sqrt(2.0 / math.pi)
_GELU_A = 0.044715


def _mid(o, rest, ys0, x, tgt, w, vec, hsel):
    s = o.shape[0]
    tm = min(TM, s)
    nsteps = s // tm
    half = FOX_W

    def body(o_ref, ga_ref, gb_ref, za_ref, u_ref, zb_ref, ys0_ref, x_ref, t_ref,
             wglu_ref, wua_ref, wub_ref, wout_ref, vec_ref, hsel_ref,
             dx2_ref, dga_ref, dgb_ref, do_ref, dza_ref, dzb_ref, dys_ref, dlt_ref,
             gout_hbm, gua_hbm, gub_hbm, gglu_hbm, vout_ref,
             a_out, a_ua, a_ub, a_glu):
        step = pl.program_id(0)

        @pl.when(step == 0)
        def _():
            a_out[...] = jnp.zeros_like(a_out)
            a_ua[...] = jnp.zeros_like(a_ua)
            a_ub[...] = jnp.zeros_like(a_ub)
            a_glu[...] = jnp.zeros_like(a_glu)
            vout_ref[...] = jnp.zeros_like(vout_ref)

        gate = vec_ref[0:1, :]
        gfin = vec_ref[1:2, :]
        dsk = vec_ref[2:3, 0:half]
        bglu = vec_ref[2:3, half:2 * half]

        o_v = o_ref[...]
        za = za_ref[...]
        sza = _sigmoid(za)
        silu_za = za * sza
        ya_b = (o_v * silu_za).astype(BF16)
        u_v = u_ref[...]
        ys = ys0_ref[...] + dsk * u_v
        inner = _GELU_C * (ys + _GELU_A * ys * ys * ys)
        th = jnp.tanh(inner)
        yg = 0.5 * ys * (1.0 + th)
        yg_b = yg.astype(BF16)
        st = _sigmoid(_dot(yg_b, wglu_ref[...]) + bglu)
        yb1 = yg * st
        zb = zb_ref[...]
        szb = _sigmoid(zb)
        silu_zb = zb * szb
        yb_b = (yb1 * silu_zb).astype(BF16)
        ua = _dot(ya_b, wua_ref[...])
        ub = _dot(yb_b, wub_ref[...])
        sga = _sigmoid(ga_ref[...])
        sgb = _sigmoid(gb_ref[...])
        merged_b = (sga * ua + sgb * ub).astype(BF16)
        mo = _dot(merged_b, wout_ref[...])
        x2 = x_ref[...] + gate * mo
        r2 = lax.rsqrt(jnp.mean(x2 * x2, axis=-1, keepdims=True) + EPS)
        x2n = x2 * r2
        diff = x2n * gfin - t_ref[...]
        loss = 0.5 * jnp.sum(jnp.mean(diff * diff, axis=-1, keepdims=True), axis=0, keepdims=True)
        dy = diff * (1.0 / D_MODEL)
        dx2n = dy * gfin
        dx2 = r2 * (dx2n - x2n * jnp.mean(dx2n * x2n, axis=-1, keepdims=True))
        dx2_ref[...] = dx2
        vout_ref[0:1, :] += jnp.sum(dy * x2n, axis=0, keepdims=True)
        vout_ref[1:2, :] += jnp.sum(dx2 * mo, axis=0, keepdims=True)
        vout_ref[3:4, :] += jnp.broadcast_to(loss, (1, D_MODEL))
        dmo_b = (dx2 * gate).astype(BF16)
        dmerged = _dot_nt(dmo_b, wout_ref[...])
        a_out[...] += _dot_tn(merged_b, dmo_b)
        dua_b = (dmerged * sga).astype(BF16)
        dub_b = (dmerged * sgb).astype(BF16)
        dga_ref[...] = (dmerged * ua * sga * (1.0 - sga)).astype(BF16)
        dgb_ref[...] = (dmerged * ub * sgb * (1.0 - sgb)).astype(BF16)
        dya = _dot_nt(dua_b, wua_ref[...])
        dyb = _dot_nt(dub_b, wub_ref[...])
        a_ua[...] += _dot_tn(ya_b, dua_b)
        a_ub[...] += _dot_tn(yb_b, dub_b)
        do_b = (dya * silu_za).astype(BF16)
        do_ref[...] = do_b
        dza_ref[...] = (dya * o_v * (sza * (1.0 + za * (1.0 - sza)))).astype(BF16)
        hsel = hsel_ref[...].astype(BF16)
        dlt_ref[...] = sum(_dot_nt(hsel, part) for part in _split3(do_b.astype(F32) * o_v))
        dyb1 = dyb * silu_zb
        dzb_ref[...] = (dyb * yb1 * (szb * (1.0 + zb * (1.0 - szb)))).astype(BF16)
        dt = dyb1 * yg * st * (1.0 - st)
        dt_b = dt.astype(BF16)
        dyg = dyb1 * st + _dot_nt(dt_b, wglu_ref[...])
        a_glu[...] += _dot_tn(yg_b, dt_b)
        dgelu = 0.5 * (1.0 + th) + 0.5 * ys * (1.0 - th * th) * _GELU_C * (1.0 + 3.0 * _GELU_A * ys * ys)
        dys = dyg * dgelu
        dys_ref[...] = dys
        vout_ref[2:3, 0:half] += jnp.sum(dys * u_v, axis=0, keepdims=True)
        vout_ref[2:3, half:2 * half] += jnp.sum(dt, axis=0, keepdims=True)

        @pl.when(step == nsteps - 1)
        def _():
            pltpu.sync_copy(a_out, gout_hbm)
            pltpu.sync_copy(a_ua, gua_hbm)
            pltpu.sync_copy(a_ub, gub_hbm)
            pltpu.sync_copy(a_glu, gglu_hbm)

    def rows(width, col=0):
        return pl.BlockSpec((tm, width), lambda i, col=col: (i, col))

    anyspace = pl.BlockSpec(memory_space=pl.ANY)
    wshapes = [(S5_W, S5_W), (FOX_W, D_MODEL), (S5_W, D_MODEL), (D_MODEL, D_MODEL)]
    return pl.pallas_call(
        body, name="mid", grid=(nsteps,),
        in_specs=[rows(FOX_W), rows(D_MODEL, R_GA // D_MODEL), rows(D_MODEL, R_GB // D_MODEL),
                  rows(FOX_W, R_ZA // FOX_W), rows(S5_W, R_U // S5_W), rows(S5_W, R_ZB // S5_W),
                  rows(S5_W), rows(D_MODEL), rows(D_MODEL)]
                 + [_const(sh) for sh in wshapes]
                 + [_const((8, D_MODEL)), _const((HEADS, FOX_W))],
        out_specs=(rows(D_MODEL), rows(D_MODEL), rows(D_MODEL), rows(FOX_W), rows(FOX_W), rows(S5_W), rows(S5_W),
                   pl.BlockSpec((HEADS, tm), lambda i: (0, i)),
                   anyspace, anyspace, anyspace, anyspace, pl.BlockSpec((8, D_MODEL), lambda i: (0, 0))),
        out_shape=(jax.ShapeDtypeStruct((s, D_MODEL), F32), jax.ShapeDtypeStruct((s, D_MODEL), BF16),
                   jax.ShapeDtypeStruct((s, D_MODEL), BF16), jax.ShapeDtypeStruct((s, FOX_W), BF16),
                   jax.ShapeDtypeStruct((s, FOX_W), BF16), jax.ShapeDtypeStruct((s, S5_W), BF16),
                   jax.ShapeDtypeStruct((s, S5_W), F32), jax.ShapeDtypeStruct((HEADS, s), F32),
                   jax.ShapeDtypeStruct((D_MODEL, D_MODEL), F32), jax.ShapeDtypeStruct((FOX_W, D_MODEL), F32),
                   jax.ShapeDtypeStruct((S5_W, D_MODEL), F32), jax.ShapeDtypeStruct((S5_W, S5_W), F32),
                   jax.ShapeDtypeStruct((8, D_MODEL), F32)),
        scratch_shapes=[pltpu.VMEM((D_MODEL, D_MODEL), F32), pltpu.VMEM((FOX_W, D_MODEL), F32),
                        pltpu.VMEM((S5_W, D_MODEL), F32), pltpu.VMEM((S5_W, S5_W), F32)],
        compiler_params=_cparams(("arbitrary",)),
    )(o, rest, rest, rest, rest, rest, ys0, x, tgt, *w, vec, hsel)


def _dh(dq, dk, dv, dga, dgb, dza, du, dzb, df, wqkv_t, wrest_t, x, dx2, gs, scatter_srcs):
    s = x.shape[0]
    tm = min(TM_PROJ, s)
    nsteps = s // tm
    na = len(scatter_srcs)

    def body(dq_ref, dk_ref, dv_ref, dga_ref, dgb_ref, dza_ref, du_ref, dzb_ref, df_ref, wq_ref, wr_ref,
             x_ref, dx2_ref, gs_ref, *rest_refs):
        src_refs = rest_refs[:na]
        gx_ref, vout_ref = rest_refs[na:na + 2]
        out_refs = rest_refs[na + 2:2 * na + 2]
        send_sems, recv_sems = rest_refs[2 * na + 2:]
        step = pl.program_id(0)
        cx, cy, cc = lax.axis_index("x"), lax.axis_index("y"), lax.axis_index("c")
        peers = [(1 - cx, cy), (cx, 1 - cy), (1 - cx, 1 - cy)]

        def copy(a, k, px, py, slot):
            return pltpu.make_async_remote_copy(
                src_ref=src_refs[a].at[2 * px + py], dst_ref=out_refs[a].at[slot],
                send_sem=send_sems.at[a * 3 + k], recv_sem=recv_sems.at[a * 3 + k],
                device_id=(px, py, cc), device_id_type=MESH)

        @pl.when(step == 0)
        def _():
            vout_ref[...] = jnp.zeros_like(vout_ref)
            for a in range(na):
                for k, (px, py) in enumerate(peers):
                    copy(a, k, px, py, 2 * cx + cy).start()

        dh = _dot(dq_ref[...], wq_ref[0:512, :])
        dh += _dot(dk_ref[...], wq_ref[512:1024, :])
        dh += _dot(dv_ref[...], wq_ref[1024:1536, :])
        dh += _dot(dga_ref[...], wr_ref[R_GA:R_GB, :])
        dh += _dot(dgb_ref[...], wr_ref[R_GB:R_ZA, :])
        dh += _dot(dza_ref[...], wr_ref[R_ZA:R_U, :])
        dh += _dot(du_ref[...], wr_ref[R_U:R_ZB, :])
        dh += _dot(dzb_ref[...], wr_ref[R_ZB:R_F, :])
        dh += _dot(df_ref[...], wr_ref[R_F:REST_W, :])
        xv = x_ref[...]
        r = lax.rsqrt(jnp.mean(xv * xv, axis=-1, keepdims=True) + EPS)
        xn = xv * r
        dxn = dh * gs_ref[...]
        gx_ref[...] = dx2_ref[...] + r * (dxn - xn * jnp.mean(dxn * xn, axis=-1, keepdims=True))
        vout_ref[0:1, :] += jnp.sum(dh * xn, axis=0, keepdims=True)
        vout_ref[1:2, :] += jnp.sum(dh, axis=0, keepdims=True)

        @pl.when(step == nsteps - 1)
        def _():
            for a in range(na):
                for k, (px, py) in enumerate(peers):
                    copy(a, k, px, py, 2 * px + py).wait_recv()
            for a in range(na):
                for k, (px, py) in enumerate(peers):
                    copy(a, k, px, py, 2 * cx + cy).wait_send()

    def rows(width):
        return pl.BlockSpec((tm, width), lambda i: (i, 0))

    anyspace = pl.BlockSpec(memory_space=pl.ANY)
    return pl.pallas_call(
        body, name="dh", grid=(nsteps,),
        in_specs=[rows(512), rows(512), rows(512), rows(1024), rows(1024), rows(512), rows(512), rows(512), rows(128),
                  _const((1536, D_MODEL)), _const((REST_W, D_MODEL)), rows(D_MODEL), rows(D_MODEL), _const((1, D_MODEL))]
                 + [anyspace] * na,
        out_specs=(rows(D_MODEL), pl.BlockSpec((8, D_MODEL), lambda i: (0, 0))) + (anyspace,) * na,
        out_shape=(jax.ShapeDtypeStruct((s, D_MODEL), F32), jax.ShapeDtypeStruct((8, D_MODEL), F32))
                  + tuple(jax.ShapeDtypeStruct(a.shape, a.dtype) for a in scatter_srcs),
        scratch_shapes=[pltpu.SemaphoreType.DMA((3 * na,)), pltpu.SemaphoreType.DMA((3 * na,))],
        compiler_params=_cparams(("arbitrary",)),
    )(dq, dk, dv, dga, dgb, dza, du, dzb, df, wqkv_t, wrest_t, x, dx2, gs, *scatter_srcs)


def _row_block(rows, mult=8, cap=512):
    if rows <= mult:
        return rows
    padded = -(-rows // mult) * mult
    for cand in range(min(cap, padded) // mult * mult, 0, -mult):
        if padded % cand == 0:
            return cand
    return padded


def _sum4(parts, name):
    rows, cols = parts.shape[1:]
    br = _row_block(rows, 16, 1024)

    def body(p_ref, o_ref):
        acc = p_ref[0].astype(F32)
        for k in range(1, 4):
            acc = acc + p_ref[k].astype(F32)
        o_ref[...] = acc

    return pl.pallas_call(
        body, name=name, grid=(pl.cdiv(rows, br),),
        in_specs=[pl.BlockSpec((4, br, cols), lambda i: (0, i, 0))],
        out_specs=pl.BlockSpec((br, cols), lambda i: (i, 0)),
        out_shape=jax.ShapeDtypeStruct((rows, cols), F32), compiler_params=_cparams(("parallel",)))(parts)


def _pair_add(a, b, name):
    shape = a.shape
    a, b = a.reshape(-1, shape[-1]), b.reshape(-1, shape[-1])
    rows, cols = a.shape
    br = _row_block(rows, 16, 1024)

    def body(a_ref, b_ref, o_ref):
        o_ref[...] = (a_ref[...].astype(F32) + b_ref[...].astype(F32)).astype(BF16)

    spec = pl.BlockSpec((br, cols), lambda i: (i, 0))
    return pl.pallas_call(
        body, name=name, grid=(pl.cdiv(rows, br),), in_specs=[spec, spec], out_specs=spec,
        out_shape=jax.ShapeDtypeStruct((rows, cols), BF16), compiler_params=_cparams(("parallel",)))(a, b).reshape(shape)


def _adamw(w, g, m, v, name):
    rows, cols = w.shape
    br = _row_block(rows)

    def body(w_ref, g_ref, m_ref, v_ref, d_ref, nm_ref, nv_ref):
        gv = g_ref[...]
        nm = ADAM_B1 * m_ref[...] + (1.0 - ADAM_B1) * gv
        nv = ADAM_B2 * v_ref[...] + (1.0 - ADAM_B2) * (gv * gv)
        m_hat = nm / (1.0 - ADAM_B1 ** ADAM_STEP)
        v_hat = nv / (1.0 - ADAM_B2 ** ADAM_STEP)
        d_ref[...] = -ADAM_LR * (m_hat / (jnp.sqrt(v_hat) + ADAM_EPS) + ADAM_WD * w_ref[...])
        nm_ref[...] = nm
        nv_ref[...] = nv

    spec = pl.BlockSpec((br, cols), lambda i: (i, 0))
    shape = jax.ShapeDtypeStruct((rows, cols), F32)
    return pl.pallas_call(
        body, name=name, grid=(pl.cdiv(rows, br),), in_specs=[spec] * 4, out_specs=(spec,) * 3,
        out_shape=(shape,) * 3, compiler_params=_cparams(("parallel",)))(w, g, m, v)


def _pack(parts, row_multiple=8):
    flat = []
    for p in parts:
        v = p.reshape(-1).astype(F32)
        pad = (-v.shape[0]) % LANES
        if pad:
            v = jnp.concatenate([v, jnp.zeros((pad,), F32)])
        flat.append(v)
    v = jnp.concatenate(flat)
    rows = v.shape[0] // LANES
    pad_rows = (-rows) % row_multiple
    if pad_rows:
        v = jnp.concatenate([v, jnp.zeros((pad_rows * LANES,), F32)])
    return v.reshape(-1, LANES)


def _unpack(packed, shapes):
    lead = packed.shape[:-2]
    flat = packed.reshape(lead + (-1,))
    out, off = [], 0
    for sh in shapes:
        size = math.prod(sh)
        out.append(flat[..., off:off + size].reshape(lead + tuple(sh)))
        off += size + (-size) % LANES
    return out


def kernel(x, c, w_ada, b_ada, g_norm, w_in, b_f, a_re, a_im, log_dt, b_re, b_im, c_re, c_im, d_skip, w_glu, b_glu, w_up_a, w_up_b, w_out, g_final, loss_target, m_w_ada, m_b_ada, m_g_norm, m_w_in, m_b_f, m_a_re, m_a_im, m_log_dt, m_b_re, m_b_im, m_c_re, m_c_im, m_d_skip, m_w_glu, m_b_glu, m_w_up_a, m_w_up_b, m_w_out, m_g_final, v_w_ada, v_b_ada, v_g_norm, v_w_in, v_b_f, v_a_re, v_a_im, v_log_dt, v_b_re, v_b_im, v_c_re, v_c_im, v_d_skip, v_w_glu, v_b_glu, v_w_up_a, v_w_up_b, v_w_out, v_g_final):
    xi, yi, ci = lax.axis_index("x"), lax.axis_index("y"), lax.axis_index("c")
    chip = 2 * xi + yi
    me = 4 * xi + 2 * yi + ci
    s = x.shape[1]
    x2d = x[0]
    tgt = loss_target[0]
    n_att = s // min(T_ATT, s)
    t_att = min(T_ATT, s)

    c_all, _ = _allgather8(c.reshape(8, LANES), "gather_c")
    c_all = c_all.reshape(8, D_MODEL)
    ncol = w_ada.shape[2]
    b_cols = lax.dynamic_slice_in_dim(b_ada, chip * ncol, ncol, axis=1)
    mod_cols = _mod_cols(c_all, w_ada[0], b_cols)
    mod_all, _ = _allgather8(mod_cols.reshape(-1, LANES), "gather_mod")
    mod_all = mod_all.reshape(4, 2, 8, ncol)[:, 0]
    mod_me = lax.dynamic_index_in_dim(mod_all, me, axis=1, keepdims=False).reshape(1, 3 * D_MODEL)
    shift, scale, gate = mod_me[:, :D_MODEL], mod_me[:, D_MODEL:2 * D_MODEL], mod_me[:, 2 * D_MODEL:]
    gs = g_norm * (1.0 + scale)

    nshard = w_in.shape[2]
    w_in_t, m_in_t, v_in_t = (jnp.swapaxes(a[0], 0, 1) for a in (w_in, m_w_in, v_w_in))
    wt_pack = jnp.pad(w_in_t.astype(BF16), ((0, SHARD_ROWS - nshard), (0, 0)))
    misc_shapes = [w_glu.shape[1:], w_up_a.shape[1:], w_up_b.shape[1:], w_out.shape[1:]]
    misc_pack = jnp.concatenate([w.reshape(-1) for w in (w_glu, w_up_a, w_up_b, w_out)]).astype(BF16).reshape(-1, LANES)
    def halves(a):
        return a.reshape((2, a.shape[0] // 2) + a.shape[1:])

    wt_all, misc_all = _gather_shards([halves(wt_pack), halves(misc_pack)], "gather_weights")
    wt_all = lax.dynamic_update_index_in_dim(wt_all, halves(wt_pack), chip, 0).reshape((4,) + wt_pack.shape)
    misc_all = lax.dynamic_update_index_in_dim(misc_all, halves(misc_pack), chip, 0).reshape((4,) + misc_pack.shape)
    p_glu, p_ua, p_ub, p_out = _unpack(misc_all, misc_shapes)

    def w_rows(lo, hi):
        out = []
        for j in range(4):
            a, b = max(lo, j * nshard), min(hi, (j + 1) * nshard)
            if a < b:
                out.append(wt_all[j, a - j * nshard:b - j * nshard])
        return out

    wqkv_t = jnp.concatenate(w_rows(O_Q, O_F), axis=0)
    wrest_t = jnp.concatenate(w_rows(O_GA, O_GB) + w_rows(O_GB, O_END) + w_rows(O_ZA, O_U) + w_rows(O_U, O_ZB)
                              + w_rows(O_ZB, O_GA) + w_rows(O_F, O_ZA)
                              + [jnp.zeros((REST_W - R_F - HEADS, D_MODEL), BF16)], axis=0)
    wmid = (p_glu.reshape(S5_W, S5_W), jnp.concatenate([p_ua[j] for j in range(4)], axis=1),
            jnp.concatenate([p_ub[j] for j in range(4)], axis=1), p_out.reshape(D_MODEL, D_MODEL))

    h, qkv, rest = _prenorm_proj(x2d, gs, shift, wqkv_t, wrest_t)
    bf128 = jnp.pad(b_f, ((0, 0), (0, LANES - HEADS)))
    selp = _head_pair_selector()
    fpc, f_t, kbias = _fcum(rest, bf128, selp, _bias_selectors())
    o, lse_pc = _attn_fwd(qkv, kbias, fpc)

    abar_r, abar_i, bb_r, bb_i = _ssm_block_params(a_re[0], a_im[0], log_dt[0], b_re[0], b_im[0])
    bb_rt, bb_it = jnp.swapaxes(bb_r, 1, 2).astype(BF16), jnp.swapaxes(bb_i, 1, 2).astype(BF16)
    cr_b, ci_b = c_re[0].astype(BF16), (-c_im[0]).astype(BF16)
    bd_c, cd_c = _compact_diag(bb_rt, bb_it), _compact_diag(cr_b, ci_b)
    seg = min(TB_SSM, s) // 8
    ys0, xs = _ssm_fwd(rest, bd_c, cd_c, *_scan_consts(a_re[0], a_im[0], log_dt[0], seg, False))

    vec = jnp.concatenate([gate, g_final.reshape(1, D_MODEL), jnp.concatenate([d_skip, b_glu], axis=1),
                           jnp.zeros((5, D_MODEL), F32)], axis=0)
    hsel = jnp.repeat(jnp.eye(HEADS, dtype=F32), HEAD_DIM, axis=1)
    (dx2, dga, dgb, do, dza, dzb, dys, dlt_t, g_out, g_ua, g_ub, g_glu, vmid) = _mid(
        o, rest, ys0, x2d, tgt, wmid, vec, hsel)

    lse_s = lse_pc.reshape(s, 4 // ATT_PAIRS, LANES)[:, :, :2 * ATT_PAIRS].reshape(s, HEADS)
    rest_f = jnp.transpose(f_t) - lse_s
    terms = []
    for _ in range(3):
        term = lax.reduce_precision(rest_f, exponent_bits=8, mantissa_bits=7)
        terms.append(term.astype(BF16))
        rest_f = rest_f - term
    terms = jnp.stack(terms, axis=-1).reshape(s, 3 * HEADS)
    qbias = jnp.concatenate([jnp.ones((s, 3 * HEADS), BF16), jnp.zeros((s, BIAS_ONES - 3 * HEADS), BF16), terms,
                             jnp.zeros((s, LANES - BIAS_ONES - 3 * HEADS), BF16)], axis=1)
    dlt5 = dlt_t.reshape(4, 2, n_att, 1, t_att)
    dq, dk, dv, dfk, dfq = _attn_bwd(qkv, do, dlt5, kbias, qbias)
    du, g_bd, g_cdt, da8 = _ssm_bwd(dys, xs, rest, bd_c, cd_c, *_scan_consts(a_re[0], a_im[0], log_dt[0], seg, True),
                                    d_skip)
    df, dbf8 = _dfcum(dfk, dfq, rest, bf128, selp.T)

    gq, gk, gv, gga, ggb, gza, gu, gzb, gf = _grad_w_rows(h, [dq, dk, dv, dga, dgb, dza, du, dzb, df])
    g_in_t = jnp.concatenate([gq, gk, gv, gf[:HEADS], gza, gu, gzb, gga, ggb], axis=0)

    def shard_cols(g, j):
        n = g.shape[1] // 4
        return g[:, j * n:(j + 1) * n]

    def shard_rows(g, j):
        n = g.shape[0] // 4
        return g[j * n:(j + 1) * n]

    def halves4(a):
        return a.reshape((4, 2, a.shape[1] // 2) + a.shape[2:])

    gt_pack = halves4(jnp.stack([
        jnp.pad(g_in_t[j * nshard:(j + 1) * nshard].astype(BF16), ((0, SHARD_ROWS - nshard), (0, 0)))
        for j in range(4)]))
    gm_pack = halves4(jnp.stack([
        jnp.concatenate([shard_rows(g_glu, j).reshape(-1), shard_cols(g_ua, j).reshape(-1),
                         shard_cols(g_ub, j).reshape(-1), shard_rows(g_out, j).reshape(-1)]).astype(BF16)
        .reshape(-1, LANES) for j in range(4)]))
    recv_in, recv_misc = _swap_sibling([gt_pack, gm_pack], "pair_swap_weight_grads", other_half=True)
    own_in = lax.dynamic_index_in_dim(gt_pack, ci, axis=1, keepdims=False)
    own_misc = lax.dynamic_index_in_dim(gm_pack, ci, axis=1, keepdims=False)
    pair_in = _pair_add(own_in, recv_in, "pair_add_w_in")
    pair_misc = _pair_add(own_misc, recv_misc, "pair_add_misc")

    grad_x, vdh, parts_in, parts_misc = _dh(dq, dk, dv, dga, dgb, dza, du, dzb, df, wqkv_t, wrest_t, x2d, dx2, gs,
                                            [pair_in, pair_misc])
    parts_in = lax.dynamic_update_slice_in_dim(parts_in, lax.dynamic_slice_in_dim(pair_in, chip, 1, 0), chip, 0)
    parts_misc = lax.dynamic_update_slice_in_dim(parts_misc, lax.dynamic_slice_in_dim(pair_misc, chip, 1, 0), chip, 0)
    half_in, half_misc = _sum4(parts_in, "sum4_w_in"), _sum4(parts_misc, "sum4_misc")
    sib_in, sib_misc = _swap_sibling([half_in, half_misc], "swap_weight_grads")

    def both_halves(mine, theirs):
        return jnp.concatenate([jnp.where(ci == 0, mine, theirs), jnp.where(ci == 0, theirs, mine)], axis=0)

    tot_in, tot_misc = both_halves(half_in, sib_in), both_halves(half_misc, sib_misc)
    g_glu_s, g_ua_s, g_ub_s, g_out_s = _unpack(tot_misc, misc_shapes)

    dgs, dshift = vdh[0:1], vdh[1:2]
    dmod = jnp.concatenate([dshift, dgs * g_norm, vmid[1:2]], axis=1)
    da = jnp.sum(da8, axis=0)
    g_bd = g_bd.reshape(GROUPS, GCH, 2 * STATE)
    g_cdt = g_cdt.reshape(GROUPS, GCH, 2 * STATE)
    g_bbr = jnp.swapaxes(g_bd[:, :, :STATE], 1, 2)
    g_bbi = jnp.swapaxes(g_bd[:, :, STATE:], 1, 2)
    g_cre = g_cdt[:, :, :STATE]
    g_cim = -g_cdt[:, :, STATE:]
    small_shapes = [(1,), (3 * D_MODEL,), (D_MODEL,), (HEADS,), (GROUPS, STATE), (GROUPS, STATE),
                    (GROUPS, STATE, GCH), (GROUPS, STATE, GCH), (GROUPS, GCH, STATE), (GROUPS, GCH, STATE),
                    (S5_W,), (S5_W,), (D_MODEL,)]
    small = _pack([vmid[3, 0:1], dmod, dgs * (1.0 + scale), dbf8[0, :HEADS], da[:NSTATE], da[NSTATE:],
                   g_bbr, g_bbi, g_cre, g_cim, vmid[2, :S5_W], vmid[2, S5_W:], vmid[0]])
    small_all, small_sum = _allgather8(small, "gather_small_grads")
    (loss_s, g_b_ada, g_g_norm, g_b_f, g_abr, g_abi, g_bbr_s, g_bbi_s, g_c_re, g_c_im, g_d_skip, g_b_glu,
     g_g_final) = _unpack(small_sum, small_shapes)
    loss = loss_s[0]
    dmod_all = _unpack(small_all, small_shapes)[1]
    dmod_cols = lax.dynamic_slice_in_dim(dmod_all, chip * ncol, ncol, axis=1)
    g_w_ada = _grad_w_ada(c_all, dmod_cols)
    _, ssm_vjp = jax.vjp(_ssm_block_params, a_re[0], a_im[0], log_dt[0], b_re[0], b_im[0])
    g_a_re, g_a_im, g_log_dt, g_b_re, g_b_im = ssm_vjp((g_abr, g_abi, g_bbr_s, g_bbi_s))

    def adam(name, w, g, m, v):
        shape = w.shape
        total = math.prod(shape)
        if len(shape) > 1 and shape[-1] >= LANES:
            cols = shape[-1]
        elif total % LANES == 0:
            cols = LANES
        else:
            cols = total
        two = lambda a: a.reshape(-1, cols)
        d, nm, nv = _adamw(two(w), two(g), two(m), two(v), "adamw_" + name)
        return g.reshape(shape), d.reshape(shape), nm.reshape(shape), nv.reshape(shape)

    back = lambda a: jnp.swapaxes(a, 0, 1)[None]
    d_in_t, nm_in_t, nv_in_t = _adamw(w_in_t, tot_in, m_in_t, v_in_t, "adamw_w_in")
    res_w_in = (back(tot_in[:nshard]), back(d_in_t), back(nm_in_t), back(nv_in_t))

    res = [
        adam("w_ada", w_ada, g_w_ada, m_w_ada, v_w_ada),
        adam("b_ada", b_ada, g_b_ada, m_b_ada, v_b_ada),
        adam("g_norm", g_norm, g_g_norm, m_g_norm, v_g_norm),
        res_w_in,
        adam("b_f", b_f, g_b_f, m_b_f, v_b_f),
        adam("a_re", a_re, g_a_re, m_a_re, v_a_re),
        adam("a_im", a_im, g_a_im, m_a_im, v_a_im),
        adam("log_dt", log_dt, g_log_dt, m_log_dt, v_log_dt),
        adam("b_re", b_re, g_b_re, m_b_re, v_b_re),
        adam("b_im", b_im, g_b_im, m_b_im, v_b_im),
        adam("c_re", c_re, g_c_re, m_c_re, v_c_re),
        adam("c_im", c_im, g_c_im, m_c_im, v_c_im),
        adam("d_skip", d_skip, g_d_skip, m_d_skip, v_d_skip),
        adam("w_glu", w_glu, g_glu_s, m_w_glu, v_w_glu),
        adam("b_glu", b_glu, g_b_glu, m_b_glu, v_b_glu),
        adam("w_up_a", w_up_a, g_ua_s, m_w_up_a, v_w_up_a),
        adam("w_up_b", w_up_b, g_ub_s, m_w_up_b, v_w_up_b),
        adam("w_out", w_out, g_out_s, m_w_out, v_w_out),
        adam("g_final", g_final, g_g_final, m_g_final, v_g_final),
    ]
    grads = [r[0] for r in res]
    deltas = [r[1] for r in res]
    new_m = [r[2] for r in res]
    new_v = [r[3] for r in res]
    return (loss, grad_x[None], *grads, *deltas, *new_m, *new_v)
```

```python
import math

import jax
import jax.numpy as jnp
from jax import lax
from jax.experimental import pallas as pl
from jax.experimental.pallas import tpu as pltpu

F32 = jnp.float32
BF16 = jnp.bfloat16
HI = lax.Precision.HIGHEST
MESH = pl.DeviceIdType.MESH

D_MODEL = 1024
HEADS = 8
HEAD_DIM = 64
FOX_W = 512
S5_W = 512
GROUPS = 32
STATE = 64
GCH = 16
NSTATE = GROUPS * STATE
EPS = 1e-6
NEG = -1e30

ADAM_LR = 0.001
ADAM_B1 = 0.9
ADAM_B2 = 0.999
ADAM_EPS = 1e-08
ADAM_WD = 0.01
ADAM_STEP = 10

VMEM_LIMIT = 56 * 1024 * 1024
LANES = 128

TM = 256
TM_PROJ = 512
T_ATT = 512
ATT_CHUNK = 32
ATT_PAIRS = 4
TB_SSM = 512
TK_ACC = 512
TB_CUM = 256
SHARD_ROWS = 1312

O_Q, O_K, O_V, O_F, O_ZA, O_U, O_ZB, O_GA, O_GB, O_END = 0, 512, 1024, 1536, 1544, 2056, 2568, 3080, 4104, 5128
REST_W = 3712
R_GA, R_GB, R_ZA, R_U, R_ZB, R_F = 0, 1024, 2048, 2560, 3072, 3584


def _cparams(sem=None):
    kw = dict(vmem_limit_bytes=VMEM_LIMIT)
    if sem is not None:
        kw["dimension_semantics"] = sem
    return pltpu.CompilerParams(**kw)


def _const(shape):
    nd = len(shape)
    return pl.BlockSpec(shape, lambda *_: (0,) * nd, pipeline_mode=pl.Buffered(1))


def _dot(a, b, precision=None):
    return jnp.dot(a, b, preferred_element_type=F32, precision=precision)


def _dot_nt(a, b):
    return lax.dot_general(a, b, (((1,), (1,)), ((), ())), preferred_element_type=F32)


def _dot_tn(a, b, precision=None):
    return lax.dot_general(a, b, (((0,), (0,)), ((), ())), preferred_element_type=F32, precision=precision)


def _sigmoid(z):
    return 1.0 / (1.0 + jnp.exp(-z))


def _split3(x):
    hi = x.astype(BF16)
    r1 = x - hi.astype(F32)
    mid = r1.astype(BF16)
    lo = (r1 - mid.astype(F32)).astype(BF16)
    return hi, mid, lo


def _dot_sel(sel, x, terms=3):
    s16 = sel.astype(BF16)
    return sum(_dot(s16, part) for part in _split3(x)[:terms])


def _allgather8(xs, name):
    rows = xs.shape[0]

    def body(x_ref, out_ref, sum_ref, send_sems, recv_sems, local_sem):
        x, y, c = lax.axis_index("x"), lax.axis_index("y"), lax.axis_index("c")
        me, sibling = (x, y, c), (x, y, 1 - c)
        chips = [(1 - x, y), (x, 1 - y), (1 - x, 1 - y)]

        def slot(px, py, pc):
            return out_ref.at[4 * px + 2 * py + pc]

        def copy(k, block, to, src=None):
            return pltpu.make_async_remote_copy(
                src_ref=slot(*block) if src is None else src, dst_ref=slot(*block),
                send_sem=send_sems.at[k], recv_sem=recv_sems.at[k], device_id=to, device_id_type=MESH)

        mine = pltpu.make_async_copy(x_ref, slot(*me), local_sem)
        mine.start()
        first = [copy(0, me, sibling, src=x_ref)]
        first += [copy(1 + j, me, (*chip, c), src=x_ref) for j, chip in enumerate(chips)]
        for cp in first:
            cp.start()
        passed = [copy(4 + j, (*chip, c), sibling) for j, chip in enumerate(chips)]
        for j, chip in enumerate(chips):
            copy(1 + j, (*chip, c), me).wait_recv()
            passed[j].start()
        copy(0, sibling, me).wait_recv()
        for j, chip in enumerate(chips):
            copy(4 + j, (*chip, 1 - c), me).wait_recv()
        for cp in first + passed:
            cp.wait_send()
        mine.wait()
        acc = out_ref[0]
        for d in range(1, 8):
            acc = acc + out_ref[d]
        sum_ref[...] = acc

    return pl.pallas_call(
        body, name=name,
        out_shape=(jax.ShapeDtypeStruct((8, rows, LANES), F32), jax.ShapeDtypeStruct((rows, LANES), F32)),
        in_specs=[pl.BlockSpec(memory_space=pltpu.VMEM)],
        out_specs=(pl.BlockSpec(memory_space=pltpu.VMEM), pl.BlockSpec(memory_space=pltpu.VMEM)),
        scratch_shapes=[pltpu.SemaphoreType.DMA((7,)), pltpu.SemaphoreType.DMA((7,)), pltpu.SemaphoreType.DMA],
        compiler_params=_cparams(),
    )(xs)


def _gather_shards(srcs, name):
    na = len(srcs)

    def body(*refs):
        src_refs, out_refs = refs[:na], refs[na:2 * na]
        send_sems, recv_sems = refs[2 * na:]
        x, y, c = lax.axis_index("x"), lax.axis_index("y"), lax.axis_index("c")
        sibling = (x, y, 1 - c)
        peers = [(1 - x, y), (x, 1 - y), (1 - x, 1 - y)]

        def copy(a, k, src, slot, which, to):
            return pltpu.make_async_remote_copy(
                src_ref=src, dst_ref=out_refs[a].at[slot, which],
                send_sem=send_sems.at[a * 6 + k], recv_sem=recv_sems.at[a * 6 + k],
                device_id=to, device_id_type=MESH)

        mine = 2 * x + y
        first = [copy(a, k, src_refs[a].at[c], mine, c, (px, py, c))
                 for a in range(na) for k, (px, py) in enumerate(peers)]
        for cp in first:
            cp.start()
        passed = []
        for a in range(na):
            for k, (px, py) in enumerate(peers):
                slot = 2 * px + py
                landed = out_refs[a].at[slot, c]
                copy(a, k, landed, slot, c, (px, py, c)).wait_recv()
                fwd = copy(a, 3 + k, landed, slot, c, sibling)
                fwd.start()
                passed.append(fwd)
        for a in range(na):
            for k, (px, py) in enumerate(peers):
                slot = 2 * px + py
                copy(a, 3 + k, out_refs[a].at[slot, 1 - c], slot, 1 - c, sibling).wait_recv()
        for cp in first + passed:
            cp.wait_send()

    anyspace = pl.BlockSpec(memory_space=pl.ANY)
    return pl.pallas_call(
        body, name=name,
        out_shape=tuple(jax.ShapeDtypeStruct((4,) + tuple(a.shape), a.dtype) for a in srcs),
        in_specs=[anyspace] * na, out_specs=(anyspace,) * na,
        scratch_shapes=[pltpu.SemaphoreType.DMA((6 * na,)), pltpu.SemaphoreType.DMA((6 * na,))],
        compiler_params=_cparams(),
    )(*srcs)


def _swap_sibling(srcs, name, other_half=False):
    na = len(srcs)

    def body(*refs):
        src_refs, out_refs = refs[:na], refs[na:2 * na]
        send_sems, recv_sems = refs[2 * na:]
        x, y, c = lax.axis_index("x"), lax.axis_index("y"), lax.axis_index("c")
        copies = [pltpu.make_async_remote_copy(
            src_ref=src_refs[a].at[:, 1 - c] if other_half else src_refs[a],
            dst_ref=out_refs[a], send_sem=send_sems.at[a], recv_sem=recv_sems.at[a],
            device_id=(x, y, 1 - c), device_id_type=MESH) for a in range(na)]
        for cp in copies:
            cp.start()
        for cp in copies:
            cp.wait()

    def out_of(a):
        shape = (a.shape[0],) + tuple(a.shape[2:]) if other_half else a.shape
        return jax.ShapeDtypeStruct(shape, a.dtype)

    anyspace = pl.BlockSpec(memory_space=pl.ANY)
    return pl.pallas_call(
        body, name=name, out_shape=tuple(out_of(a) for a in srcs),
        in_specs=[anyspace] * na, out_specs=(anyspace,) * na,
        scratch_shapes=[pltpu.SemaphoreType.DMA((na,)), pltpu.SemaphoreType.DMA((na,))],
        compiler_params=_cparams(),
    )(*srcs)


def _mod_cols(c_all, w, b):
    n = w.shape[1]

    def body(c_ref, w_ref, b_ref, o_ref):
        o_ref[...] = _dot(c_ref[...], w_ref[...], HI) + b_ref[...]

    return pl.pallas_call(
        body, name="mod_cols", out_shape=jax.ShapeDtypeStruct((8, n), F32),
        compiler_params=_cparams())(c_all, w, b)


def _grad_w_ada(c_all, dmod_cols):
    n = dmod_cols.shape[1]

    def body(c_ref, d_ref, o_ref):
        o_ref[...] = _dot_tn(c_ref[...], d_ref[...], HI)

    return pl.pallas_call(
        body, name="grad_w_ada", out_shape=jax.ShapeDtypeStruct((D_MODEL, n), F32),
        compiler_params=_cparams())(c_all, dmod_cols)


def _prenorm_proj(x, gs, shift, wqkv_t, wrest_t):
    s = x.shape[0]
    tm = min(TM_PROJ, s)
    nq, nr = wqkv_t.shape[0], wrest_t.shape[0]

    def body(x_ref, gs_ref, sh_ref, wq_ref, wr_ref, h_ref, qkv_ref, rest_ref):
        xv = x_ref[...]
        r = lax.rsqrt(jnp.mean(xv * xv, axis=-1, keepdims=True) + EPS)
        h = (xv * r * gs_ref[...] + sh_ref[...]).astype(BF16)
        h_ref[...] = h
        qkv_ref[...] = _dot_nt(h, wq_ref[...]).astype(BF16)
        rest_ref[...] = _dot_nt(h, wr_ref[...])

    def rows(width):
        return pl.BlockSpec((tm, width), lambda i: (i, 0))

    return pl.pallas_call(
        body, name="prenorm_proj", grid=(s // tm,),
        in_specs=[rows(D_MODEL), _const((1, D_MODEL)), _const((1, D_MODEL)), _const((nq, D_MODEL)),
                  _const((nr, D_MODEL))],
        out_specs=(rows(D_MODEL), rows(nq), rows(nr)),
        out_shape=(jax.ShapeDtypeStruct((s, D_MODEL), BF16), jax.ShapeDtypeStruct((s, nq), BF16),
                   jax.ShapeDtypeStruct((s, nr), F32)),
        compiler_params=_cparams(("parallel",)))(x, gs, shift, wqkv_t, wrest_t)


def _grad_w_rows(h, ds):
    s = h.shape[0]
    tk = min(TK_ACC, s)
    nd = len(ds)
    widths = [d.shape[1] for d in ds]

    def body(*refs):
        h_ref, d_refs = refs[0], refs[1:1 + nd]
        out_refs, accs = refs[1 + nd:1 + 2 * nd], refs[1 + 2 * nd:]
        step = pl.program_id(0)

        @pl.when(step == 0)
        def _():
            for acc in accs:
                acc[...] = jnp.zeros_like(acc)

        hv = h_ref[...]
        for d_ref, acc in zip(d_refs, accs):
            acc[...] += _dot_tn(d_ref[...], hv)

        @pl.when(step == s // tk - 1)
        def _():
            for acc, out in zip(accs, out_refs):
                pltpu.sync_copy(acc, out)

    anyspace = pl.BlockSpec(memory_space=pl.ANY)
    return pl.pallas_call(
        body, name="grad_w_in", grid=(s // tk,),
        in_specs=[pl.BlockSpec((tk, D_MODEL), lambda k: (k, 0))]
                 + [pl.BlockSpec((tk, w), lambda k: (k, 0)) for w in widths],
        out_specs=(anyspace,) * nd,
        out_shape=tuple(jax.ShapeDtypeStruct((w, D_MODEL), F32) for w in widths),
        scratch_shapes=[pltpu.VMEM((w, D_MODEL), F32) for w in widths],
        compiler_params=_cparams(("arbitrary",)))(h, *ds)


BIAS_ONES = 32


def _bias_selectors():
    rows = jnp.arange(LANES)[None, :, None]
    cols = jnp.arange(LANES)[None, None, :]
    term = jnp.arange(3)[:, None, None]
    return ((rows < HEADS) & (cols == 3 * rows + term)).astype(F32)


def _fcum(rest, bf128, selk):
    s = rest.shape[0]
    tb = min(TB_CUM, s)

    def body(fz_ref, bf_ref, selk_ref, fpc_ref, ft_ref, kb_ref, carry_ref):
        @pl.when(pl.program_id(0) == 0)
        def _():
            carry_ref[...] = jnp.zeros_like(carry_ref)

        z = fz_ref[...] + bf_ref[...]
        logf = jnp.minimum(z, 0.0) - jnp.log(1.0 + jnp.exp(-jnp.abs(z)))
        r = lax.broadcasted_iota(jnp.int32, (tb, tb), 0)
        c = lax.broadcasted_iota(jnp.int32, (tb, tb), 1)
        tri = (c <= r).astype(F32)
        f = _dot_sel(tri, logf) + carry_ref[0:1, :]
        carry_ref[0:1, :] = f[tb - 1:tb, :]
        for pp in range(4):
            fpc_ref[:, pp * LANES:(pp + 1) * LANES] = f if pp == 0 else pltpu.roll(f, LANES - 2 * pp, 1)
        ft_ref[...] = jnp.transpose(f)[0:HEADS, :]
        lane = lax.broadcasted_iota(jnp.int32, (tb, LANES), 1)
        ones = ((lane >= BIAS_ONES) & (lane < BIAS_ONES + 3 * HEADS)).astype(F32)
        terms = sum(_dot(part, selk_ref[j].astype(BF16)) for j, part in enumerate(_split3(-f)))
        kb_ref[...] = (terms + ones).astype(BF16)

    return pl.pallas_call(
        body, name="forget_cumsum", grid=(s // tb,),
        in_specs=[pl.BlockSpec((tb, LANES), lambda i: (i, R_F // LANES)), _const((1, LANES)), _const((3, LANES, LANES))],
        out_specs=(pl.BlockSpec((tb, 4 * LANES), lambda i: (i, 0)), pl.BlockSpec((HEADS, tb), lambda i: (0, i)),
                   pl.BlockSpec((tb, LANES), lambda i: (i, 0))),
        out_shape=(jax.ShapeDtypeStruct((s, 4 * LANES), F32), jax.ShapeDtypeStruct((HEADS, s), F32),
                   jax.ShapeDtypeStruct((s, LANES), BF16)),
        scratch_shapes=[pltpu.VMEM((8, LANES), F32)],
        compiler_params=_cparams(("arbitrary",)))(rest, bf128, selk)


def _dfcum(dfk, dfq, rest, bf128):
    s = rest.shape[0]
    tb = min(TB_CUM, s)
    nb = s // tb

    def body(dk_ref, dq_ref, fz_ref, bf_ref, df_ref, dbf_ref, carry_ref):
        @pl.when(pl.program_id(0) == 0)
        def _():
            carry_ref[...] = jnp.zeros_like(carry_ref)
            dbf_ref[...] = jnp.zeros_like(dbf_ref)

        pair_lanes = lax.broadcasted_iota(jnp.int32, (tb, LANES), 1) < 2
        d = jnp.zeros((tb, LANES), F32)
        for pp in range(4):
            tile = dk_ref[:, pp * LANES:(pp + 1) * LANES] + dq_ref[:, pp * LANES:(pp + 1) * LANES]
            tile = jnp.where(pair_lanes, tile, 0.0)
            d = d + (tile if pp == 0 else pltpu.roll(tile, 2 * pp, 1))
        r = lax.broadcasted_iota(jnp.int32, (tb, tb), 0)
        c = lax.broadcasted_iota(jnp.int32, (tb, tb), 1)
        triu = (c >= r).astype(F32)
        dlogf = _dot_sel(triu, d) + carry_ref[0:1, :]
        carry_ref[0:1, :] = dlogf[0:1, :]
        z = fz_ref[...] + bf_ref[...]
        df = dlogf * (1.0 / (1.0 + jnp.exp(z)))
        df_ref[...] = df.astype(BF16)
        dbf_ref[0:1, :] += jnp.sum(df, axis=0, keepdims=True)

    return pl.pallas_call(
        body, name="forget_grad", grid=(nb,),
        in_specs=[pl.BlockSpec((tb, 4 * LANES), lambda i: (nb - 1 - i, 0)),
                  pl.BlockSpec((tb, 4 * LANES), lambda i: (nb - 1 - i, 0)),
                  pl.BlockSpec((tb, LANES), lambda i: (nb - 1 - i, R_F // LANES)),
                  _const((1, LANES))],
        out_specs=(pl.BlockSpec((tb, LANES), lambda i: (nb - 1 - i, 0)), pl.BlockSpec((8, LANES), lambda i: (0, 0))),
        out_shape=(jax.ShapeDtypeStruct((s, LANES), BF16), jax.ShapeDtypeStruct((8, LANES), F32)),
        scratch_shapes=[pltpu.VMEM((8, LANES), F32)],
        compiler_params=_cparams(("arbitrary",)))(dfk, dfq, rest, bf128)


def _scaled(q):
    return (q.astype(F32) * (HEAD_DIM ** -0.5)).astype(BF16)


def _attn_fwd(qkv, kbias, fpc):
    s = qkv.shape[0]
    t = min(T_ATT, s)
    n = s // t
    ch = min(ATT_CHUNK, t)
    wide = 2 * LANES
    pairs = ATT_PAIRS
    width = pairs * LANES
    groups = 4 // pairs

    def body(q_ref, k_ref, v_ref, kb_ref, fc_ref, o_ref, lse_ref, s_scr, p_scr, m_scr, a_scr, acc_scr):
        i = pl.program_id(1)
        g = pl.program_id(0)
        lane = lax.broadcasted_iota(jnp.int32, (t, LANES), 1)
        first = lane < HEAD_DIM
        ones_col = ((lane == 0).astype(BF16), (lane == 1).astype(BF16))
        m_scr[...] = jnp.full(m_scr.shape, NEG, F32)
        acc_scr[...] = jnp.zeros_like(acc_scr)
        qm = []
        for pp in range(pairs):
            q = _scaled(q_ref[:, pp * LANES:(pp + 1) * LANES])
            zq = jnp.zeros_like(q)
            for hh in range(2):
                head = 2 * (g * pairs + pp) + hh
                fq = _split3(fc_ref[:, pp * LANES + hh:pp * LANES + hh + 1])
                bias = jnp.where((lane >= 3 * head) & (lane < 3 * head + 3), 1.0, 0.0).astype(BF16)
                for term in range(3):
                    bias = jnp.where(lane == BIAS_ONES + 3 * head + term, fq[term], bias)
                qh = jnp.where(first, q, zq) if hh == 0 else jnp.where(first, zq, q)
                qm.append(jnp.concatenate([qh, bias], axis=1))

        def step(j, masked):
            r0 = pl.multiple_of(j * t, t)
            vaug = []
            kbias_blk = kb_ref[pl.ds(r0, t), :]
            for pp in range(pairs):
                kb = jnp.concatenate([k_ref[pl.ds(r0, t), pp * LANES:(pp + 1) * LANES], kbias_blk], axis=1)
                vb = v_ref[pl.ds(r0, t), pp * LANES:(pp + 1) * LANES]
                zv = jnp.zeros_like(vb)
                vaug += [jnp.concatenate([jnp.where(first, vb, zv), ones_col[0]], axis=1),
                         jnp.concatenate([jnp.where(first, zv, vb), ones_col[1]], axis=1)]
                for hh in range(2):
                    s_scr[2 * pp + hh] = _dot_nt(qm[2 * pp + hh], kb)
            pv = []
            for hd in range(2 * pairs):
                for c in range(t // ch):
                    rows = pl.ds(c * ch, ch)
                    hi = min(t, (c * ch // LANES + 1) * LANES) if masked else t
                    sc = s_scr[hd, rows, 0:hi]
                    if masked:
                        rq = c * ch + lax.broadcasted_iota(jnp.int32, (ch, hi), 0)
                        ck = lax.broadcasted_iota(jnp.int32, (ch, hi), 1)
                        sc = jnp.where(ck <= rq, sc, NEG)
                    m_old = m_scr[hd, rows, :]
                    m_new = jnp.maximum(m_old, jnp.max(sc, axis=1, keepdims=True))
                    p_scr[hd, rows, 0:hi] = jnp.exp(sc - m_new).astype(BF16)
                    if hi < t:
                        p_scr[hd, rows, hi:t] = jnp.zeros((ch, t - hi), BF16)
                    a_scr[hd, rows, :] = jnp.exp(m_old - m_new)
                    m_scr[hd, rows, :] = m_new
                pv.append(_dot(p_scr[hd], vaug[hd]))
            for pp in range(pairs):
                a0, a1 = a_scr[2 * pp], a_scr[2 * pp + 1]
                alpha = jnp.concatenate([jnp.where(first, a0, a1), jnp.where(lane == 0, a0, a1)], axis=1)
                acc_scr[pp] = acc_scr[pp] * alpha + pv[2 * pp] + pv[2 * pp + 1]
            return 0

        lax.fori_loop(0, i, lambda j, _: step(j, False), 0)
        step(i, True)
        lse = jnp.zeros((t, LANES), F32)
        for pp in range(pairs):
            l0 = acc_scr[pp, :, LANES:LANES + 1]
            l1 = acc_scr[pp, :, LANES + 1:LANES + 2]
            o_ref[:, pp * LANES:(pp + 1) * LANES] = acc_scr[pp, :, 0:LANES] * jnp.where(first, 1.0 / l0, 1.0 / l1)
            lse = jnp.where(lane == 2 * pp, m_scr[2 * pp] + jnp.log(l0), lse)
            lse = jnp.where(lane == 2 * pp + 1, m_scr[2 * pp + 1] + jnp.log(l1), lse)
        lse_ref[...] = lse

    blk = pl.BlockSpec((t, width), lambda g, i: (i, g))
    return pl.pallas_call(
        body, name="attn_fwd", grid=(groups, n),
        in_specs=[blk,
                  pl.BlockSpec((s, width), lambda g, i: (0, groups + g)),
                  pl.BlockSpec((s, width), lambda g, i: (0, 2 * groups + g)),
                  pl.BlockSpec((s, LANES), lambda g, i: (0, 0)),
                  blk],
        out_specs=(blk, pl.BlockSpec((t, LANES), lambda g, i: (i, g))),
        out_shape=(jax.ShapeDtypeStruct((s, FOX_W), F32), jax.ShapeDtypeStruct((s, groups * LANES), F32)),
        scratch_shapes=[pltpu.VMEM((2 * pairs, t, t), F32), pltpu.VMEM((2 * pairs, t, t), BF16),
                        pltpu.VMEM((2 * pairs, t, 1), F32), pltpu.VMEM((2 * pairs, t, 1), F32),
                        pltpu.VMEM((pairs, t, wide), F32)],
        compiler_params=_cparams(("parallel", "arbitrary")))(qkv, qkv, qkv, kbias, fpc)


def _attn_bwd(qkv, do, lse5, dlt5, frow5, fpc):
    s = qkv.shape[0]
    t = min(T_ATT, s)
    n = s // t
    wide = 2 * LANES

    ch = min(ATT_CHUNK, t)

    def body(q_ref, do_ref, k_ref, v_ref, lse_ref, dl_ref, fr_ref, fc_ref,
             dq_ref, dk_ref, dv_ref, dfk_ref, dfq_ref, dq_acc, st_scr, dp_scr, pt_scr, ds_scr, dk_acc, dv_acc, fk_scr):
        j = pl.program_id(1)

        @pl.when(j == 0)
        def _():
            dq_acc[...] = jnp.zeros_like(dq_acc)

        dk_acc[...] = jnp.zeros_like(dk_acc)
        dv_acc[...] = jnp.zeros_like(dv_acc)
        lane = lax.broadcasted_iota(jnp.int32, (t, LANES), 1)
        first = lane < HEAD_DIM
        ones_col = ((lane == 0).astype(BF16), (lane == 1).astype(BF16))
        kb = k_ref[...]
        vb = v_ref[...]
        zk = jnp.zeros_like(kb)
        kaug = (jnp.concatenate([jnp.where(first, kb, zk), ones_col[0]], axis=1),
                jnp.concatenate([jnp.where(first, zk, kb), ones_col[1]], axis=1))
        fk_scr[0] = fc_ref[:, 0:1]
        fk_scr[1] = fc_ref[:, 1:2]

        def step(blocks, masked):
            chains = []
            for bi, i in enumerate(blocks):
                r0 = pl.multiple_of(i * t, t)
                qb = _scaled(q_ref[pl.ds(r0, t), :])
                dob = do_ref[pl.ds(r0, t), :]
                zq = jnp.zeros_like(qb)
                qm = (jnp.where(first, qb, zq), jnp.where(first, zq, qb))
                dom = (jnp.where(first, dob, zq), jnp.where(first, zq, dob))
                for hh in range(2):
                    st_scr[2 * bi + hh] = _dot_nt(kb, qm[hh])
                    dp_scr[2 * bi + hh] = _dot_nt(vb, dom[hh])
                    chains.append((i, hh, qm[hh], dom[hh]))
            dq_add = [jnp.zeros((t, wide), F32) for _ in blocks]
            for cn, (i, hh, qmh, domh) in enumerate(chains):
                bias = fr_ref[0, hh, i] - lse_ref[0, hh, i]
                dl = dl_ref[0, hh, i]
                for c in range(t // ch):
                    rows = pl.ds(c * ch, ch)
                    lo = c * ch // LANES * LANES if masked else 0
                    st = st_scr[cn, rows, lo:t] + (bias[:, lo:t] - fk_scr[hh, rows, :])
                    if masked:
                        rk = c * ch + lax.broadcasted_iota(jnp.int32, (ch, t - lo), 0)
                        cq = lo + lax.broadcasted_iota(jnp.int32, (ch, t - lo), 1)
                        st = jnp.where(rk <= cq, st, NEG)
                    pt = jnp.exp(st)
                    pt_scr[cn, rows, lo:t] = pt.astype(BF16)
                    ds_scr[cn, rows, lo:t] = (pt * (dp_scr[cn, rows, lo:t] - dl[:, lo:t])).astype(BF16)
                    if lo > 0:
                        pt_scr[cn, rows, 0:lo] = jnp.zeros((ch, lo), BF16)
                        ds_scr[cn, rows, 0:lo] = jnp.zeros((ch, lo), BF16)
                dsb = ds_scr[cn]
                dv_acc[...] += _dot(pt_scr[cn], domh)
                dk_acc[...] += _dot(dsb, jnp.concatenate([qmh, ones_col[hh]], axis=1))
                dq_add[cn // 2] = dq_add[cn // 2] + _dot_tn(dsb, kaug[hh])
            for bi, i in enumerate(blocks):
                dq_acc[pl.ds(pl.multiple_of(i * t, t), t), :] += dq_add[bi]
            return 0

        step([j], True)
        odd = (n - 1 - j) % 2
        lax.fori_loop(0, odd, lambda _, carry: step([j + 1], False), 0)
        first_pair = j + 1 + odd
        lax.fori_loop(0, (n - first_pair) // 2,
                      lambda p, _: step([first_pair + 2 * p, first_pair + 2 * p + 1], False), 0)
        dk_ref[...] = dk_acc[:, 0:LANES].astype(BF16)
        dv_ref[...] = dv_acc[...].astype(BF16)
        dfk_ref[...] = -dk_acc[:, LANES:wide]

        @pl.when(j == n - 1)
        def _():
            dq_ref[...] = (dq_acc[:, 0:LANES] * (HEAD_DIM ** -0.5)).astype(BF16)
            dfq_ref[...] = dq_acc[:, LANES:wide]

    stat = pl.BlockSpec((1, 2, n, 1, t), lambda h, j: (h, 0, 0, 0, 0))
    blk = pl.BlockSpec((t, LANES), lambda h, j: (j, h))
    full = pl.BlockSpec((s, LANES), lambda h, j: (0, h))
    return pl.pallas_call(
        body, name="attn_bwd", grid=(4, n),
        in_specs=[full, full,
                  pl.BlockSpec((t, LANES), lambda h, j: (j, 4 + h)),
                  pl.BlockSpec((t, LANES), lambda h, j: (j, 8 + h)),
                  stat, stat, stat, blk],
        out_specs=(full, blk, blk, blk, full),
        out_shape=(jax.ShapeDtypeStruct((s, FOX_W), BF16), jax.ShapeDtypeStruct((s, FOX_W), BF16),
                   jax.ShapeDtypeStruct((s, FOX_W), BF16), jax.ShapeDtypeStruct((s, 4 * LANES), F32),
                   jax.ShapeDtypeStruct((s, 4 * LANES), F32)),
        scratch_shapes=[pltpu.VMEM((s, wide), F32), pltpu.VMEM((4, t, t), F32), pltpu.VMEM((4, t, t), F32),
                        pltpu.VMEM((4, t, t), BF16), pltpu.VMEM((4, t, t), BF16), pltpu.VMEM((t, wide), F32),
                        pltpu.VMEM((t, LANES), F32), pltpu.VMEM((2, t, 1), F32)],
        compiler_params=_cparams(("parallel", "arbitrary")))(qkv, do, qkv, qkv, lse5, dlt5, frow5, fpc)


def _ssm_block_params(a_re, a_im, log_dt, b_re, b_im):
    dt = jnp.exp(log_dt)[:, None]
    mag = jnp.exp(a_re * dt)
    ar = mag * jnp.cos(a_im * dt)
    ai = mag * jnp.sin(a_im * dt)
    den = a_re * a_re + a_im * a_im
    nr = ar - 1.0
    cr = (nr * a_re + ai * a_im) / den
    ci = (ai * a_re - nr * a_im) / den
    bbr = cr[:, :, None] * b_re - ci[:, :, None] * b_im
    bbi = cr[:, :, None] * b_im + ci[:, :, None] * b_re
    return ar, ai, bbr, bbi


def _block_diag(blocks):
    g, r, c = blocks.shape
    eye = jnp.eye(g, dtype=blocks.dtype)
    return (blocks[:, :, None, :] * eye[:, None, :, None]).reshape(g * r, g * c)


def _scan_consts(a_re, a_im, log_dt, seg, reverse):
    dt = jnp.exp(log_dt)[:, None]
    lr = (a_re * dt).reshape(1, NSTATE)
    li = (a_im * dt).reshape(1, NSTATE)
    if reverse:
        li = -li
    rows = jnp.arange(8, dtype=F32)[:, None]

    def power(k):
        mag = jnp.exp(k * lr)
        return mag * jnp.cos(k * li), mag * jnp.sin(k * li)

    tiles = list(power(1.0))
    for k in (1, 2, 4):
        keep = (rows < 8 - k) if reverse else (rows >= k)
        pr, pi_ = power(float(k * seg))
        tiles += [jnp.where(keep, pr, 0.0), jnp.where(keep, pi_, 0.0)]
    tiles += list(power(seg * ((8.0 - rows) if reverse else (rows + 1.0))))
    tiles = jnp.stack([jnp.broadcast_to(tl, (8, NSTATE)) for tl in tiles])
    steps = jnp.arange(seg, dtype=F32)[:, None]
    table = jnp.stack([jnp.broadcast_to(p[:, None, :], (seg, 8, NSTATE))
                       for p in power((seg - steps) if reverse else (steps + 1.0))])
    return tiles, table


_SCAN_W = 1024
_HALF_W = S5_W // 2
_HALF_S = NSTATE // 2


def _compact_diag(blocks_re, blocks_im):
    hg = GROUPS // 2
    return jnp.concatenate([_block_diag(b[h * hg:(h + 1) * hg]) for b in (blocks_re, blocks_im) for h in range(2)],
                           axis=1)


def _half_expand(v, w_ref, out_ref):
    for half in range(2):
        vh = v[:, half * _HALF_W:(half + 1) * _HALF_W]
        for part in range(2):
            c0 = part * NSTATE + half * _HALF_S
            out_ref[:, c0:c0 + _HALF_S] = _dot(vh, w_ref[:, c0:c0 + _HALF_S])


def _half_contract(x_ref, w_ref, half):
    out = None
    for part in range(2):
        r0 = part * NSTATE + half * _HALF_S
        term = _dot_nt(x_ref[:, r0:r0 + _HALF_S].astype(BF16), w_ref[:, r0:r0 + _HALF_S])
        out = term if out is None else out + term
    return out


def _half_outer(v, x_ref, acc_ref):
    for half in range(2):
        vh = v[:, half * _HALF_W:(half + 1) * _HALF_W]
        for part in range(2):
            c0 = part * NSTATE + half * _HALF_S
            acc_ref[:, c0:c0 + _HALF_S] += _dot_tn(vh, x_ref[:, c0:c0 + _HALF_S].astype(BF16))


def _segment_perm(tb):
    seg = tb // 8
    row = lax.broadcasted_iota(jnp.int32, (tb, tb), 0)
    col = lax.broadcasted_iota(jnp.int32, (tb, tb), 1)
    perm = (col == (row % 8) * seg + row // 8).astype(BF16)
    back = (col == (row % seg) * 8 + row // seg).astype(BF16)
    return perm, back


def _segment_ends(re, im, cf_ref, cb_ref, cr, ci, reverse):
    for n_, k in enumerate((1, 2, 4)):
        kr = cf_ref[2 + 2 * n_, :, cr]
        ki = cf_ref[3 + 2 * n_, :, cr]
        sr = pltpu.roll(re, 8 - k if reverse else k, 0)
        si = pltpu.roll(im, 8 - k if reverse else k, 0)
        re, im = re + kr * sr - ki * si, im + kr * si + ki * sr
    cbr, cbi = cb_ref[:, cr], cb_ref[:, ci]
    pr, pi_ = cf_ref[8, :, cr], cf_ref[9, :, cr]
    re, im = re + pr * cbr - pi_ * cbi, im + pr * cbi + pi_ * cbr
    edge = lax.broadcasted_iota(jnp.int32, re.shape, 0) == (7 if reverse else 0)
    in_r = jnp.where(edge, cbr, pltpu.roll(re, 7 if reverse else 1, 0))
    in_i = jnp.where(edge, cbi, pltpu.roll(im, 7 if reverse else 1, 0))
    out = slice(0, 1) if reverse else slice(7, 8)
    cb_ref[:, cr] = jnp.broadcast_to(re[out, :], re.shape)
    cb_ref[:, ci] = jnp.broadcast_to(im[out, :], im.shape)
    return in_r, in_i


def _ssm_fwd(rest, bd, cd, consts, table):
    s = rest.shape[0]
    tb = min(TB_SSM, s)
    seg = tb // 8
    ns2 = 2 * NSTATE

    def body(u_ref, bd_ref, cd_ref, cf_ref, tab_ref, y_ref, x_ref, cb_ref):
        @pl.when(pl.program_id(0) == 0)
        def _():
            cb_ref[...] = jnp.zeros_like(cb_ref)

        perm, back = _segment_perm(tb)
        _half_expand(_dot(perm, u_ref[...].astype(BF16)).astype(BF16), bd_ref, x_ref)
        for cc in range(NSTATE // _SCAN_W):
            cr = pl.ds(cc * _SCAN_W, _SCAN_W)
            ci = pl.ds(NSTATE + cc * _SCAN_W, _SCAN_W)
            ar, ai = cf_ref[0, :, cr], cf_ref[1, :, cr]

            def local(i, carry, cr=cr, ci=ci, ar=ar, ai=ai):
                re, im = carry
                rows = pl.ds(pl.multiple_of(i * 8, 8), 8)
                re, im = ar * re - ai * im + x_ref[rows, cr], ar * im + ai * re + x_ref[rows, ci]
                x_ref[rows, cr] = re
                x_ref[rows, ci] = im
                return re, im

            zero = jnp.zeros((8, _SCAN_W), F32)
            re, im = lax.fori_loop(0, seg, local, (zero, zero))
            in_r, in_i = _segment_ends(re, im, cf_ref, cb_ref, cr, ci, False)

            def fix(i, _, cr=cr, ci=ci, in_r=in_r, in_i=in_i):
                rows = pl.ds(pl.multiple_of(i * 8, 8), 8)
                tr, ti = tab_ref[0, i, :, cr], tab_ref[1, i, :, cr]
                x_ref[rows, cr] += tr * in_r - ti * in_i
                x_ref[rows, ci] += tr * in_i + ti * in_r
                return 0

            lax.fori_loop(0, seg, fix, 0)
        y_p = jnp.concatenate([_half_contract(x_ref, cd_ref, half) for half in range(2)], axis=1)
        y_ref[...] = _dot_sel(back, y_p, terms=2)

    return pl.pallas_call(
        body, name="ssm_fwd", grid=(s // tb,),
        in_specs=[pl.BlockSpec((tb, S5_W), lambda i: (i, R_U // S5_W)), _const((_HALF_W, ns2)), _const((_HALF_W, ns2)),
                  _const((10, 8, NSTATE)), _const((2, seg, 8, NSTATE))],
        out_specs=(pl.BlockSpec((tb, S5_W), lambda i: (i, 0)), pl.BlockSpec((tb, ns2), lambda i: (i, 0))),
        out_shape=(jax.ShapeDtypeStruct((s, S5_W), F32), jax.ShapeDtypeStruct((s, ns2), F32)),
        scratch_shapes=[pltpu.VMEM((8, ns2), F32)],
        compiler_params=_cparams(("arbitrary",)))(rest, bd, cd, consts, table)


def _ssm_bwd(dys, xs, rest, bd, cd, consts, table, dskip):
    s = dys.shape[0]
    tb = min(TB_SSM, s)
    seg = tb // 8
    nb = s // tb
    ns2 = 2 * NSTATE

    def body(dy_ref, x_ref, u_ref, bd_ref, cd_ref, cf_ref, tab_ref, dsk_ref, du_ref, gb_ref, gc_ref, da_ref,
             g_ref, cb_ref, acc_b, acc_c):
        step = pl.program_id(0)

        @pl.when(step == 0)
        def _():
            cb_ref[...] = jnp.zeros_like(cb_ref)
            acc_b[...] = jnp.zeros_like(acc_b)
            acc_c[...] = jnp.zeros_like(acc_c)
            da_ref[...] = jnp.zeros_like(da_ref)

        perm, back = _segment_perm(tb)
        dy = dy_ref[...]
        dy_p = _dot(perm, dy.astype(BF16)).astype(BF16)
        u_p = _dot(perm, u_ref[...].astype(BF16)).astype(BF16)
        _half_expand(dy_p, cd_ref, g_ref)
        for cc in range(NSTATE // _SCAN_W):
            cr = pl.ds(cc * _SCAN_W, _SCAN_W)
            ci = pl.ds(NSTATE + cc * _SCAN_W, _SCAN_W)
            ar, ai = cf_ref[0, :, cr], cf_ref[1, :, cr]

            def local(ii, carry, cr=cr, ci=ci, ar=ar, ai=ai):
                re, im = carry
                rows = pl.ds(pl.multiple_of((seg - 1 - ii) * 8, 8), 8)
                re, im = ar * re - ai * im + g_ref[rows, cr], ar * im + ai * re + g_ref[rows, ci]
                g_ref[rows, cr] = re
                g_ref[rows, ci] = im
                return re, im

            zero = jnp.zeros((8, _SCAN_W), F32)
            re, im = lax.fori_loop(0, seg, local, (zero, zero))
            in_r, in_i = _segment_ends(re, im, cf_ref, cb_ref, cr, ci, True)

            def fix(ii, carry, cr=cr, ci=ci, in_r=in_r, in_i=in_i):
                nr, ni, acr, aci = carry
                i = seg - 1 - ii
                rows = pl.ds(pl.multiple_of(i * 8, 8), 8)
                tr, ti = tab_ref[0, i, :, cr], tab_ref[1, i, :, cr]
                gr = g_ref[rows, cr] + tr * in_r - ti * in_i
                gi = g_ref[rows, ci] + tr * in_i + ti * in_r
                g_ref[rows, cr] = gr
                g_ref[rows, ci] = gi
                xr, xi = x_ref[rows, cr], x_ref[rows, ci]
                return gr, gi, acr + nr * xr + ni * xi, aci + ni * xr - nr * xi

            _, _, acr, aci = lax.fori_loop(0, seg, fix, (in_r, in_i, zero, zero))
            da_ref[:, cr] += acr
            da_ref[:, ci] += aci
        du_p = jnp.concatenate([_half_contract(g_ref, bd_ref, half) for half in range(2)], axis=1)
        du_ref[...] = (_dot_sel(back, du_p, terms=2) + dy * dsk_ref[...]).astype(BF16)
        _half_outer(u_p, g_ref, acc_b)
        _half_outer(dy_p, x_ref, acc_c)

        @pl.when(step == nb - 1)
        def _():
            for g in range(GROUPS):
                src = slice((g % (GROUPS // 2)) * GCH, (g % (GROUPS // 2) + 1) * GCH)
                dst = slice(g * GCH, (g + 1) * GCH)
                for part in range(2):
                    cols = slice(part * NSTATE + g * STATE, part * NSTATE + (g + 1) * STATE)
                    gb_ref[dst, part * STATE:(part + 1) * STATE] = acc_b[src, cols]
                    gc_ref[dst, part * STATE:(part + 1) * STATE] = acc_c[src, cols]

    rev = lambda i: (nb - 1 - i, 0)
    small = pl.BlockSpec((S5_W, 2 * STATE), lambda i: (0, 0))
    return pl.pallas_call(
        body, name="ssm_bwd", grid=(nb,),
        in_specs=[pl.BlockSpec((tb, S5_W), rev), pl.BlockSpec((tb, ns2), rev),
                  pl.BlockSpec((tb, S5_W), lambda i: (nb - 1 - i, R_U // S5_W)),
                  _const((_HALF_W, ns2)), _const((_HALF_W, ns2)), _const((10, 8, NSTATE)), _const((2, seg, 8, NSTATE)),
                  _const((1, S5_W))],
        out_specs=(pl.BlockSpec((tb, S5_W), rev), small, small, pl.BlockSpec((8, ns2), lambda i: (0, 0))),
        out_shape=(jax.ShapeDtypeStruct((s, S5_W), BF16), jax.ShapeDtypeStruct((S5_W, 2 * STATE), F32),
                   jax.ShapeDtypeStruct((S5_W, 2 * STATE), F32), jax.ShapeDtypeStruct((8, ns2), F32)),
        scratch_shapes=[pltpu.VMEM((tb, ns2), F32), pltpu.VMEM((8, ns2), F32),
                        pltpu.VMEM((_HALF_W, ns2), F32), pltpu.VMEM((_HALF_W, ns2), F32)],
        compiler_params=_cparams(("arbitrary",)))(dys, xs, rest, bd, cd, consts, table, dskip)


_GELU_C = math.sqrt(2.0 / math.pi)
_GELU_A = 0.044715


def _mid(o, rest, ys0, x, tgt, w, vec, hsel):
    s = o.shape[0]
    tm = min(TM, s)
    nsteps = s // tm
    half = FOX_W

    def body(o_ref, ga_ref, gb_ref, za_ref, u_ref, zb_ref, ys0_ref, x_ref, t_ref,
             wglu_ref, wua_ref, wub_ref, wout_ref, vec_ref, hsel_ref,
             dx2_ref, dga_ref, dgb_ref, do_ref, dza_ref, dzb_ref, dys_ref, dlt_ref,
             gout_hbm, gua_hbm, gub_hbm, gglu_hbm, vout_ref,
             a_out, a_ua, a_ub, a_glu):
        step = pl.program_id(0)

        @pl.when(step == 0)
        def _():
            a_out[...] = jnp.zeros_like(a_out)
            a_ua[...] = jnp.zeros_like(a_ua)
            a_ub[...] = jnp.zeros_like(a_ub)
            a_glu[...] = jnp.zeros_like(a_glu)
            vout_ref[...] = jnp.zeros_like(vout_ref)

        gate = vec_ref[0:1, :]
        gfin = vec_ref[1:2, :]
        dsk = vec_ref[2:3, 0:half]
        bglu = vec_ref[2:3, half:2 * half]

        o_v = o_ref[...]
        za = za_ref[...]
        sza = _sigmoid(za)
        silu_za = za * sza
        ya_b = (o_v * silu_za).astype(BF16)
        u_v = u_ref[...]
        ys = ys0_ref[...] + dsk * u_v
        inner = _GELU_C * (ys + _GELU_A * ys * ys * ys)
        th = jnp.tanh(inner)
        yg = 0.5 * ys * (1.0 + th)
        yg_b = yg.astype(BF16)
        st = _sigmoid(_dot(yg_b, wglu_ref[...]) + bglu)
        yb1 = yg * st
        zb = zb_ref[...]
        szb = _sigmoid(zb)
        silu_zb = zb * szb
        yb_b = (yb1 * silu_zb).astype(BF16)
        ua = _dot(ya_b, wua_ref[...])
        ub = _dot(yb_b, wub_ref[...])
        sga = _sigmoid(ga_ref[...])
        sgb = _sigmoid(gb_ref[...])
        merged_b = (sga * ua + sgb * ub).astype(BF16)
        mo = _dot(merged_b, wout_ref[...])
        x2 = x_ref[...] + gate * mo
        r2 = lax.rsqrt(jnp.mean(x2 * x2, axis=-1, keepdims=True) + EPS)
        x2n = x2 * r2
        diff = x2n * gfin - t_ref[...]
        loss = 0.5 * jnp.sum(jnp.mean(diff * diff, axis=-1, keepdims=True), axis=0, keepdims=True)
        dy = diff * (1.0 / D_MODEL)
        dx2n = dy * gfin
        dx2 = r2 * (dx2n - x2n * jnp.mean(dx2n * x2n, axis=-1, keepdims=True))
        dx2_ref[...] = dx2
        vout_ref[0:1, :] += jnp.sum(dy * x2n, axis=0, keepdims=True)
        vout_ref[1:2, :] += jnp.sum(dx2 * mo, axis=0, keepdims=True)
        vout_ref[3:4, :] += jnp.broadcast_to(loss, (1, D_MODEL))
        dmo_b = (dx2 * gate).astype(BF16)
        dmerged = _dot_nt(dmo_b, wout_ref[...])
        a_out[...] += _dot_tn(merged_b, dmo_b)
        dua_b = (dmerged * sga).astype(BF16)
        dub_b = (dmerged * sgb).astype(BF16)
        dga_ref[...] = (dmerged * ua * sga * (1.0 - sga)).astype(BF16)
        dgb_ref[...] = (dmerged * ub * sgb * (1.0 - sgb)).astype(BF16)
        dya = _dot_nt(dua_b, wua_ref[...])
        dyb = _dot_nt(dub_b, wub_ref[...])
        a_ua[...] += _dot_tn(ya_b, dua_b)
        a_ub[...] += _dot_tn(yb_b, dub_b)
        do_b = (dya * silu_za).astype(BF16)
        do_ref[...] = do_b
        dza_ref[...] = (dya * o_v * (sza * (1.0 + za * (1.0 - sza)))).astype(BF16)
        hsel = hsel_ref[...].astype(BF16)
        dlt_ref[...] = sum(_dot_nt(hsel, part) for part in _split3(do_b.astype(F32) * o_v))
        dyb1 = dyb * silu_zb
        dzb_ref[...] = (dyb * yb1 * (szb * (1.0 + zb * (1.0 - szb)))).astype(BF16)
        dt = dyb1 * yg * st * (1.0 - st)
        dt_b = dt.astype(BF16)
        dyg = dyb1 * st + _dot_nt(dt_b, wglu_ref[...])
        a_glu[...] += _dot_tn(yg_b, dt_b)
        dgelu = 0.5 * (1.0 + th) + 0.5 * ys * (1.0 - th * th) * _GELU_C * (1.0 + 3.0 * _GELU_A * ys * ys)
        dys = dyg * dgelu
        dys_ref[...] = dys
        vout_ref[2:3, 0:half] += jnp.sum(dys * u_v, axis=0, keepdims=True)
        vout_ref[2:3, half:2 * half] += jnp.sum(dt, axis=0, keepdims=True)

        @pl.when(step == nsteps - 1)
        def _():
            pltpu.sync_copy(a_out, gout_hbm)
            pltpu.sync_copy(a_ua, gua_hbm)
            pltpu.sync_copy(a_ub, gub_hbm)
            pltpu.sync_copy(a_glu, gglu_hbm)

    def rows(width, col=0):
        return pl.BlockSpec((tm, width), lambda i, col=col: (i, col))

    anyspace = pl.BlockSpec(memory_space=pl.ANY)
    wshapes = [(S5_W, S5_W), (FOX_W, D_MODEL), (S5_W, D_MODEL), (D_MODEL, D_MODEL)]
    return pl.pallas_call(
        body, name="mid", grid=(nsteps,),
        in_specs=[rows(FOX_W), rows(D_MODEL, R_GA // D_MODEL), rows(D_MODEL, R_GB // D_MODEL),
                  rows(FOX_W, R_ZA // FOX_W), rows(S5_W, R_U // S5_W), rows(S5_W, R_ZB // S5_W),
                  rows(S5_W), rows(D_MODEL), rows(D_MODEL)]
                 + [_const(sh) for sh in wshapes]
                 + [_const((8, D_MODEL)), _const((HEADS, FOX_W))],
        out_specs=(rows(D_MODEL), rows(D_MODEL), rows(D_MODEL), rows(FOX_W), rows(FOX_W), rows(S5_W), rows(S5_W),
                   pl.BlockSpec((HEADS, tm), lambda i: (0, i)),
                   anyspace, anyspace, anyspace, anyspace, pl.BlockSpec((8, D_MODEL), lambda i: (0, 0))),
        out_shape=(jax.ShapeDtypeStruct((s, D_MODEL), F32), jax.ShapeDtypeStruct((s, D_MODEL), BF16),
                   jax.ShapeDtypeStruct((s, D_MODEL), BF16), jax.ShapeDtypeStruct((s, FOX_W), BF16),
                   jax.ShapeDtypeStruct((s, FOX_W), BF16), jax.ShapeDtypeStruct((s, S5_W), BF16),
                   jax.ShapeDtypeStruct((s, S5_W), F32), jax.ShapeDtypeStruct((HEADS, s), F32),
                   jax.ShapeDtypeStruct((D_MODEL, D_MODEL), F32), jax.ShapeDtypeStruct((FOX_W, D_MODEL), F32),
                   jax.ShapeDtypeStruct((S5_W, D_MODEL), F32), jax.ShapeDtypeStruct((S5_W, S5_W), F32),
                   jax.ShapeDtypeStruct((8, D_MODEL), F32)),
        scratch_shapes=[pltpu.VMEM((D_MODEL, D_MODEL), F32), pltpu.VMEM((FOX_W, D_MODEL), F32),
                        pltpu.VMEM((S5_W, D_MODEL), F32), pltpu.VMEM((S5_W, S5_W), F32)],
        compiler_params=_cparams(("arbitrary",)),
    )(o, rest, rest, rest, rest, rest, ys0, x, tgt, *w, vec, hsel)


def _dh(dq, dk, dv, dga, dgb, dza, du, dzb, df, wqkv_t, wrest_t, x, dx2, gs, scatter_srcs):
    s = x.shape[0]
    tm = min(TM_PROJ, s)
    nsteps = s // tm
    na = len(scatter_srcs)

    def body(dq_ref, dk_ref, dv_ref, dga_ref, dgb_ref, dza_ref, du_ref, dzb_ref, df_ref, wq_ref, wr_ref,
             x_ref, dx2_ref, gs_ref, *rest_refs):
        src_refs = rest_refs[:na]
        gx_ref, vout_ref = rest_refs[na:na + 2]
        out_refs = rest_refs[na + 2:2 * na + 2]
        send_sems, recv_sems = rest_refs[2 * na + 2:]
        step = pl.program_id(0)
        cx, cy, cc = lax.axis_index("x"), lax.axis_index("y"), lax.axis_index("c")
        peers = [(1 - cx, cy), (cx, 1 - cy), (1 - cx, 1 - cy)]

        def copy(a, k, px, py, slot):
            return pltpu.make_async_remote_copy(
                src_ref=src_refs[a].at[2 * px + py], dst_ref=out_refs[a].at[slot],
                send_sem=send_sems.at[a * 3 + k], recv_sem=recv_sems.at[a * 3 + k],
                device_id=(px, py, cc), device_id_type=MESH)

        @pl.when(step == 0)
        def _():
            vout_ref[...] = jnp.zeros_like(vout_ref)
            for a in range(na):
                for k, (px, py) in enumerate(peers):
                    copy(a, k, px, py, 2 * cx + cy).start()

        dh = _dot(dq_ref[...], wq_ref[0:512, :])
        dh += _dot(dk_ref[...], wq_ref[512:1024, :])
        dh += _dot(dv_ref[...], wq_ref[1024:1536, :])
        dh += _dot(dga_ref[...], wr_ref[R_GA:R_GB, :])
        dh += _dot(dgb_ref[...], wr_ref[R_GB:R_ZA, :])
        dh += _dot(dza_ref[...], wr_ref[R_ZA:R_U, :])
        dh += _dot(du_ref[...], wr_ref[R_U:R_ZB, :])
        dh += _dot(dzb_ref[...], wr_ref[R_ZB:R_F, :])
        dh += _dot(df_ref[...], wr_ref[R_F:REST_W, :])
        xv = x_ref[...]
        r = lax.rsqrt(jnp.mean(xv * xv, axis=-1, keepdims=True) + EPS)
        xn = xv * r
        dxn = dh * gs_ref[...]
        gx_ref[...] = dx2_ref[...] + r * (dxn - xn * jnp.mean(dxn * xn, axis=-1, keepdims=True))
        vout_ref[0:1, :] += jnp.sum(dh * xn, axis=0, keepdims=True)
        vout_ref[1:2, :] += jnp.sum(dh, axis=0, keepdims=True)

        @pl.when(step == nsteps - 1)
        def _():
            for a in range(na):
                for k, (px, py) in enumerate(peers):
                    copy(a, k, px, py, 2 * px + py).wait_recv()
            for a in range(na):
                for k, (px, py) in enumerate(peers):
                    copy(a, k, px, py, 2 * cx + cy).wait_send()

    def rows(width):
        return pl.BlockSpec((tm, width), lambda i: (i, 0))

    anyspace = pl.BlockSpec(memory_space=pl.ANY)
    return pl.pallas_call(
        body, name="dh", grid=(nsteps,),
        in_specs=[rows(512), rows(512), rows(512), rows(1024), rows(1024), rows(512), rows(512), rows(512), rows(128),
                  _const((1536, D_MODEL)), _const((REST_W, D_MODEL)), rows(D_MODEL), rows(D_MODEL), _const((1, D_MODEL))]
                 + [anyspace] * na,
        out_specs=(rows(D_MODEL), pl.BlockSpec((8, D_MODEL), lambda i: (0, 0))) + (anyspace,) * na,
        out_shape=(jax.ShapeDtypeStruct((s, D_MODEL), F32), jax.ShapeDtypeStruct((8, D_MODEL), F32))
                  + tuple(jax.ShapeDtypeStruct(a.shape, a.dtype) for a in scatter_srcs),
        scratch_shapes=[pltpu.SemaphoreType.DMA((3 * na,)), pltpu.SemaphoreType.DMA((3 * na,))],
        compiler_params=_cparams(("arbitrary",)),
    )(dq, dk, dv, dga, dgb, dza, du, dzb, df, wqkv_t, wrest_t, x, dx2, gs, *scatter_srcs)


def _row_block(rows, mult=8, cap=512):
    if rows <= mult:
        return rows
    padded = -(-rows // mult) * mult
    for cand in range(min(cap, padded) // mult * mult, 0, -mult):
        if padded % cand == 0:
            return cand
    return padded


def _sum4(parts, name):
    rows, cols = parts.shape[1:]
    br = _row_block(rows, 16, 1024)

    def body(p_ref, o_ref):
        acc = p_ref[0].astype(F32)
        for k in range(1, 4):
            acc = acc + p_ref[k].astype(F32)
        o_ref[...] = acc

    return pl.pallas_call(
        body, name=name, grid=(pl.cdiv(rows, br),),
        in_specs=[pl.BlockSpec((4, br, cols), lambda i: (0, i, 0))],
        out_specs=pl.BlockSpec((br, cols), lambda i: (i, 0)),
        out_shape=jax.ShapeDtypeStruct((rows, cols), F32), compiler_params=_cparams(("parallel",)))(parts)


def _pair_add(a, b, name):
    shape = a.shape
    a, b = a.reshape(-1, shape[-1]), b.reshape(-1, shape[-1])
    rows, cols = a.shape
    br = _row_block(rows, 16, 1024)

    def body(a_ref, b_ref, o_ref):
        o_ref[...] = (a_ref[...].astype(F32) + b_ref[...].astype(F32)).astype(BF16)

    spec = pl.BlockSpec((br, cols), lambda i: (i, 0))
    return pl.pallas_call(
        body, name=name, grid=(pl.cdiv(rows, br),), in_specs=[spec, spec], out_specs=spec,
        out_shape=jax.ShapeDtypeStruct((rows, cols), BF16), compiler_params=_cparams(("parallel",)))(a, b).reshape(shape)


def _adamw(w, g, m, v, name):
    rows, cols = w.shape
    br = _row_block(rows)

    def body(w_ref, g_ref, m_ref, v_ref, d_ref, nm_ref, nv_ref):
        gv = g_ref[...]
        nm = ADAM_B1 * m_ref[...] + (1.0 - ADAM_B1) * gv
        nv = ADAM_B2 * v_ref[...] + (1.0 - ADAM_B2) * (gv * gv)
        m_hat = nm / (1.0 - ADAM_B1 ** ADAM_STEP)
        v_hat = nv / (1.0 - ADAM_B2 ** ADAM_STEP)
        d_ref[...] = -ADAM_LR * (m_hat / (jnp.sqrt(v_hat) + ADAM_EPS) + ADAM_WD * w_ref[...])
        nm_ref[...] = nm
        nv_ref[...] = nv

    spec = pl.BlockSpec((br, cols), lambda i: (i, 0))
    shape = jax.ShapeDtypeStruct((rows, cols), F32)
    return pl.pallas_call(
        body, name=name, grid=(pl.cdiv(rows, br),), in_specs=[spec] * 4, out_specs=(spec,) * 3,
        out_shape=(shape,) * 3, compiler_params=_cparams(("parallel",)))(w, g, m, v)


def _pack(parts, row_multiple=8):
    flat = []
    for p in parts:
        v = p.reshape(-1).astype(F32)
        pad = (-v.shape[0]) % LANES
        if pad:
            v = jnp.concatenate([v, jnp.zeros((pad,), F32)])
        flat.append(v)
    v = jnp.concatenate(flat)
    rows = v.shape[0] // LANES
    pad_rows = (-rows) % row_multiple
    if pad_rows:
        v = jnp.concatenate([v, jnp.zeros((pad_rows * LANES,), F32)])
    return v.reshape(-1, LANES)


def _unpack(packed, shapes):
    lead = packed.shape[:-2]
    flat = packed.reshape(lead + (-1,))
    out, off = [], 0
    for sh in shapes:
        size = math.prod(sh)
        out.append(flat[..., off:off + size].reshape(lead + tuple(sh)))
        off += size + (-size) % LANES
    return out


def kernel(x, c, w_ada, b_ada, g_norm, w_in, b_f, a_re, a_im, log_dt, b_re, b_im, c_re, c_im, d_skip, w_glu, b_glu, w_up_a, w_up_b, w_out, g_final, loss_target, m_w_ada, m_b_ada, m_g_norm, m_w_in, m_b_f, m_a_re, m_a_im, m_log_dt, m_b_re, m_b_im, m_c_re, m_c_im, m_d_skip, m_w_glu, m_b_glu, m_w_up_a, m_w_up_b, m_w_out, m_g_final, v_w_ada, v_b_ada, v_g_norm, v_w_in, v_b_f, v_a_re, v_a_im, v_log_dt, v_b_re, v_b_im, v_c_re, v_c_im, v_d_skip, v_w_glu, v_b_glu, v_w_up_a, v_w_up_b, v_w_out, v_g_final):
    xi, yi, ci = lax.axis_index("x"), lax.axis_index("y"), lax.axis_index("c")
    chip = 2 * xi + yi
    me = 4 * xi + 2 * yi + ci
    s = x.shape[1]
    x2d = x[0]
    tgt = loss_target[0]
    n_att = s // min(T_ATT, s)
    t_att = min(T_ATT, s)

    c_all, _ = _allgather8(c.reshape(8, LANES), "gather_c")
    c_all = c_all.reshape(8, D_MODEL)
    ncol = w_ada.shape[2]
    b_cols = lax.dynamic_slice_in_dim(b_ada, chip * ncol, ncol, axis=1)
    mod_cols = _mod_cols(c_all, w_ada[0], b_cols)
    mod_all, _ = _allgather8(mod_cols.reshape(-1, LANES), "gather_mod")
    mod_all = mod_all.reshape(4, 2, 8, ncol)[:, 0]
    mod_me = lax.dynamic_index_in_dim(mod_all, me, axis=1, keepdims=False).reshape(1, 3 * D_MODEL)
    shift, scale, gate = mod_me[:, :D_MODEL], mod_me[:, D_MODEL:2 * D_MODEL], mod_me[:, 2 * D_MODEL:]
    gs = g_norm * (1.0 + scale)

    nshard = w_in.shape[2]
    w_in_t, m_in_t, v_in_t = (jnp.swapaxes(a[0], 0, 1) for a in (w_in, m_w_in, v_w_in))
    wt_pack = jnp.pad(w_in_t.astype(BF16), ((0, SHARD_ROWS - nshard), (0, 0)))
    misc_shapes = [w_glu.shape[1:], w_up_a.shape[1:], w_up_b.shape[1:], w_out.shape[1:]]
    misc_pack = jnp.concatenate([w.reshape(-1) for w in (w_glu, w_up_a, w_up_b, w_out)]).astype(BF16).reshape(-1, LANES)
    def halves(a):
        return a.reshape((2, a.shape[0] // 2) + a.shape[1:])

    wt_all, misc_all = _gather_shards([halves(wt_pack), halves(misc_pack)], "gather_weights")
    wt_all = lax.dynamic_update_index_in_dim(wt_all, halves(wt_pack), chip, 0).reshape((4,) + wt_pack.shape)
    misc_all = lax.dynamic_update_index_in_dim(misc_all, halves(misc_pack), chip, 0).reshape((4,) + misc_pack.shape)
    p_glu, p_ua, p_ub, p_out = _unpack(misc_all, misc_shapes)

    def w_rows(lo, hi):
        out = []
        for j in range(4):
            a, b = max(lo, j * nshard), min(hi, (j + 1) * nshard)
            if a < b:
                out.append(wt_all[j, a - j * nshard:b - j * nshard])
        return out

    wqkv_t = jnp.concatenate(w_rows(O_Q, O_F), axis=0)
    wrest_t = jnp.concatenate(w_rows(O_GA, O_GB) + w_rows(O_GB, O_END) + w_rows(O_ZA, O_U) + w_rows(O_U, O_ZB)
                              + w_rows(O_ZB, O_GA) + w_rows(O_F, O_ZA)
                              + [jnp.zeros((REST_W - R_F - HEADS, D_MODEL), BF16)], axis=0)
    wmid = (p_glu.reshape(S5_W, S5_W), jnp.concatenate([p_ua[j] for j in range(4)], axis=1),
            jnp.concatenate([p_ub[j] for j in range(4)], axis=1), p_out.reshape(D_MODEL, D_MODEL))

    h, qkv, rest = _prenorm_proj(x2d, gs, shift, wqkv_t, wrest_t)
    bf128 = jnp.pad(b_f, ((0, 0), (0, LANES - HEADS)))
    fpc, f_t, kbias = _fcum(rest, bf128, _bias_selectors())
    frow5 = f_t.reshape(4, 2, n_att, 1, t_att)
    o, lse_pc = _attn_fwd(qkv, kbias, fpc)

    abar_r, abar_i, bb_r, bb_i = _ssm_block_params(a_re[0], a_im[0], log_dt[0], b_re[0], b_im[0])
    bb_rt, bb_it = jnp.swapaxes(bb_r, 1, 2).astype(BF16), jnp.swapaxes(bb_i, 1, 2).astype(BF16)
    cr_b, ci_b = c_re[0].astype(BF16), (-c_im[0]).astype(BF16)
    bd_c, cd_c = _compact_diag(bb_rt, bb_it), _compact_diag(cr_b, ci_b)
    seg = min(TB_SSM, s) // 8
    ys0, xs = _ssm_fwd(rest, bd_c, cd_c, *_scan_consts(a_re[0], a_im[0], log_dt[0], seg, False))

    vec = jnp.concatenate([gate, g_final.reshape(1, D_MODEL), jnp.concatenate([d_skip, b_glu], axis=1),
                           jnp.zeros((5, D_MODEL), F32)], axis=0)
    hsel = jnp.repeat(jnp.eye(HEADS, dtype=F32), HEAD_DIM, axis=1)
    (dx2, dga, dgb, do, dza, dzb, dys, dlt_t, g_out, g_ua, g_ub, g_glu, vmid) = _mid(
        o, rest, ys0, x2d, tgt, wmid, vec, hsel)

    lse_t = jnp.transpose(lse_pc.reshape(s, 4 // ATT_PAIRS, LANES)[:, :, :2 * ATT_PAIRS], (1, 2, 0))
    lse5 = lse_t.reshape(4, 2, n_att, 1, t_att)
    dlt5 = dlt_t.reshape(4, 2, n_att, 1, t_att)
    dq, dk, dv, dfk, dfq = _attn_bwd(qkv, do, lse5, dlt5, frow5, fpc)
    du, g_bd, g_cdt, da8 = _ssm_bwd(dys, xs, rest, bd_c, cd_c, *_scan_consts(a_re[0], a_im[0], log_dt[0], seg, True),
                                    d_skip)
    df, dbf8 = _dfcum(dfk, dfq, rest, bf128)

    gq, gk, gv, gga, ggb, gza, gu, gzb, gf = _grad_w_rows(h, [dq, dk, dv, dga, dgb, dza, du, dzb, df])
    g_in_t = jnp.concatenate([gq, gk, gv, gf[:HEADS], gza, gu, gzb, gga, ggb], axis=0)

    def shard_cols(g, j):
        n = g.shape[1] // 4
        return g[:, j * n:(j + 1) * n]

    def shard_rows(g, j):
        n = g.shape[0] // 4
        return g[j * n:(j + 1) * n]

    def halves4(a):
        return a.reshape((4, 2, a.shape[1] // 2) + a.shape[2:])

    gt_pack = halves4(jnp.stack([
        jnp.pad(g_in_t[j * nshard:(j + 1) * nshard].astype(BF16), ((0, SHARD_ROWS - nshard), (0, 0)))
        for j in range(4)]))
    gm_pack = halves4(jnp.stack([
        jnp.concatenate([shard_rows(g_glu, j).reshape(-1), shard_cols(g_ua, j).reshape(-1),
                         shard_cols(g_ub, j).reshape(-1), shard_rows(g_out, j).reshape(-1)]).astype(BF16)
        .reshape(-1, LANES) for j in range(4)]))
    recv_in, recv_misc = _swap_sibling([gt_pack, gm_pack], "pair_swap_weight_grads", other_half=True)
    own_in = lax.dynamic_index_in_dim(gt_pack, ci, axis=1, keepdims=False)
    own_misc = lax.dynamic_index_in_dim(gm_pack, ci, axis=1, keepdims=False)
    pair_in = _pair_add(own_in, recv_in, "pair_add_w_in")
    pair_misc = _pair_add(own_misc, recv_misc, "pair_add_misc")

    grad_x, vdh, parts_in, parts_misc = _dh(dq, dk, dv, dga, dgb, dza, du, dzb, df, wqkv_t, wrest_t, x2d, dx2, gs,
                                            [pair_in, pair_misc])
    parts_in = lax.dynamic_update_slice_in_dim(parts_in, lax.dynamic_slice_in_dim(pair_in, chip, 1, 0), chip, 0)
    parts_misc = lax.dynamic_update_slice_in_dim(parts_misc, lax.dynamic_slice_in_dim(pair_misc, chip, 1, 0), chip, 0)
    half_in, half_misc = _sum4(parts_in, "sum4_w_in"), _sum4(parts_misc, "sum4_misc")
    sib_in, sib_misc = _swap_sibling([half_in, half_misc], "swap_weight_grads")

    def both_halves(mine, theirs):
        return jnp.concatenate([jnp.where(ci == 0, mine, theirs), jnp.where(ci == 0, theirs, mine)], axis=0)

    tot_in, tot_misc = both_halves(half_in, sib_in), both_halves(half_misc, sib_misc)
    g_glu_s, g_ua_s, g_ub_s, g_out_s = _unpack(tot_misc, misc_shapes)

    dgs, dshift = vdh[0:1], vdh[1:2]
    dmod = jnp.concatenate([dshift, dgs * g_norm, vmid[1:2]], axis=1)
    da = jnp.sum(da8, axis=0)
    g_bd = g_bd.reshape(GROUPS, GCH, 2 * STATE)
    g_cdt = g_cdt.reshape(GROUPS, GCH, 2 * STATE)
    g_bbr = jnp.swapaxes(g_bd[:, :, :STATE], 1, 2)
    g_bbi = jnp.swapaxes(g_bd[:, :, STATE:], 1, 2)
    g_cre = g_cdt[:, :, :STATE]
    g_cim = -g_cdt[:, :, STATE:]
    small_shapes = [(1,), (3 * D_MODEL,), (D_MODEL,), (HEADS,), (GROUPS, STATE), (GROUPS, STATE),
                    (GROUPS, STATE, GCH), (GROUPS, STATE, GCH), (GROUPS, GCH, STATE), (GROUPS, GCH, STATE),
                    (S5_W,), (S5_W,), (D_MODEL,)]
    small = _pack([vmid[3, 0:1], dmod, dgs * (1.0 + scale), dbf8[0, :HEADS], da[:NSTATE], da[NSTATE:],
                   g_bbr, g_bbi, g_cre, g_cim, vmid[2, :S5_W], vmid[2, S5_W:], vmid[0]])
    small_all, small_sum = _allgather8(small, "gather_small_grads")
    (loss_s, g_b_ada, g_g_norm, g_b_f, g_abr, g_abi, g_bbr_s, g_bbi_s, g_c_re, g_c_im, g_d_skip, g_b_glu,
     g_g_final) = _unpack(small_sum, small_shapes)
    loss = loss_s[0]
    dmod_all = _unpack(small_all, small_shapes)[1]
    dmod_cols = lax.dynamic_slice_in_dim(dmod_all, chip * ncol, ncol, axis=1)
    g_w_ada = _grad_w_ada(c_all, dmod_cols)
    _, ssm_vjp = jax.vjp(_ssm_block_params, a_re[0], a_im[0], log_dt[0], b_re[0], b_im[0])
    g_a_re, g_a_im, g_log_dt, g_b_re, g_b_im = ssm_vjp((g_abr, g_abi, g_bbr_s, g_bbi_s))

    def adam(name, w, g, m, v):
        shape = w.shape
        total = math.prod(shape)
        if len(shape) > 1 and shape[-1] >= LANES:
            cols = shape[-1]
        elif total % LANES == 0:
            cols = LANES
        else:
            cols = total
        two = lambda a: a.reshape(-1, cols)
        d, nm, nv = _adamw(two(w), two(g), two(m), two(v), "adamw_" + name)
        return g.reshape(shape), d.reshape(shape), nm.reshape(shape), nv.reshape(shape)

    back = lambda a: jnp.swapaxes(a, 0, 1)[None]
    d_in_t, nm_in_t, nv_in_t = _adamw(w_in_t, tot_in, m_in_t, v_in_t, "adamw_w_in")
    res_w_in = (back(tot_in[:nshard]), back(d_in_t), back(nm_in_t), back(nv_in_t))

    res = [
        adam("w_ada", w_ada, g_w_ada, m_w_ada, v_w_ada),
        adam("b_ada", b_ada, g_b_ada, m_b_ada, v_b_ada),
        adam("g_norm", g_norm, g_g_norm, m_g_norm, v_g_norm),
        res_w_in,
        adam("b_f", b_f, g_b_f, m_b_f, v_b_f),
        adam("a_re", a_re, g_a_re, m_a_re, v_a_re),
        adam("a_im", a_im, g_a_im, m_a_im, v_a_im),
        adam("log_dt", log_dt, g_log_dt, m_log_dt, v_log_dt),
        adam("b_re", b_re, g_b_re, m_b_re, v_b_re),
        adam("b_im", b_im, g_b_im, m_b_im, v_b_im),
        adam("c_re", c_re, g_c_re, m_c_re, v_c_re),
        adam("c_im", c_im, g_c_im, m_c_im, v_c_im),
        adam("d_skip", d_skip, g_d_skip, m_d_skip, v_d_skip),
        adam("w_glu", w_glu, g_glu_s, m_w_glu, v_w_glu),
        adam("b_glu", b_glu, g_b_glu, m_b_glu, v_b_glu),
        adam("w_up_a", w_up_a, g_ua_s, m_w_up_a, v_w_up_a),
        adam("w_up_b", w_up_b, g_ub_s, m_w_up_b, v_w_up_b),
        adam("w_out", w_out, g_out_s, m_w_out, v_w_out),
        adam("g_final", g_final, g_g_final, m_g_final, v_g_final),
    ]
    grads = [r[0] for r in res]
    deltas = [r[1] for r in res]
    new_m = [r[2] for r in res]
    new_v = [r[3] for r in res]
    return (loss, grad_x[None], *grads, *deltas, *new_m, *new_v)
```

```python
import math

import jax
import jax.numpy as jnp
from jax import lax
from jax.experimental import pallas as pl
from jax.experimental.pallas import tpu as pltpu

F32 = jnp.float32
BF16 = jnp.bfloat16
HI = lax.Precision.HIGHEST
MESH = pl.DeviceIdType.MESH

D_MODEL = 1024
HEADS = 8
HEAD_DIM = 64
FOX_W = 512
S5_W = 512
GROUPS = 32
STATE = 64
GCH = 16
NSTATE = GROUPS * STATE
EPS = 1e-6
NEG = -1e30

ADAM_LR = 0.001
ADAM_B1 = 0.9
ADAM_B2 = 0.999
ADAM_EPS = 1e-08
ADAM_WD = 0.01
ADAM_STEP = 10

VMEM_LIMIT = 56 * 1024 * 1024
LANES = 128

TM = 256
TM_PROJ = 512
T_ATT = 512
ATT_CHUNK = 32
ATT_PAIRS = 4
TB_SSM = 512
TK_ACC = 1024
TB_CUM = 256
SHARD_ROWS = 1312

O_Q, O_K, O_V, O_F, O_ZA, O_U, O_ZB, O_GA, O_GB, O_END = 0, 512, 1024, 1536, 1544, 2056, 2568, 3080, 4104, 5128
REST_W = 3712
R_GA, R_GB, R_ZA, R_U, R_ZB, R_F = 0, 1024, 2048, 2560, 3072, 3584


def _cparams(sem=None):
    kw = dict(vmem_limit_bytes=VMEM_LIMIT)
    if sem is not None:
        kw["dimension_semantics"] = sem
    return pltpu.CompilerParams(**kw)


def _const(shape):
    nd = len(shape)
    return pl.BlockSpec(shape, lambda *_: (0,) * nd, pipeline_mode=pl.Buffered(1))


def _dot(a, b, precision=None):
    return jnp.dot(a, b, preferred_element_type=F32, precision=precision)


def _dot_nt(a, b):
    return lax.dot_general(a, b, (((1,), (1,)), ((), ())), preferred_element_type=F32)


def _dot_tn(a, b, precision=None):
    return lax.dot_general(a, b, (((0,), (0,)), ((), ())), preferred_element_type=F32, precision=precision)


def _sigmoid(z):
    return 1.0 / (1.0 + jnp.exp(-z))


def _split3(x):
    hi = x.astype(BF16)
    r1 = x - hi.astype(F32)
    mid = r1.astype(BF16)
    lo = (r1 - mid.astype(F32)).astype(BF16)
    return hi, mid, lo


def _dot_sel(sel, x, terms=3):
    s16 = sel.astype(BF16)
    return sum(_dot(s16, part) for part in _split3(x)[:terms])


def _allgather8(xs, name):
    rows = xs.shape[0]

    def body(x_ref, out_ref, sum_ref, send_sems, recv_sems, local_sem):
        x, y, c = lax.axis_index("x"), lax.axis_index("y"), lax.axis_index("c")
        me, sibling = (x, y, c), (x, y, 1 - c)
        chips = [(1 - x, y), (x, 1 - y), (1 - x, 1 - y)]

        def slot(px, py, pc):
            return out_ref.at[4 * px + 2 * py + pc]

        def copy(k, block, to, src=None):
            return pltpu.make_async_remote_copy(
                src_ref=slot(*block) if src is None else src, dst_ref=slot(*block),
                send_sem=send_sems.at[k], recv_sem=recv_sems.at[k], device_id=to, device_id_type=MESH)

        mine = pltpu.make_async_copy(x_ref, slot(*me), local_sem)
        mine.start()
        first = [copy(0, me, sibling, src=x_ref)]
        first += [copy(1 + j, me, (*chip, c), src=x_ref) for j, chip in enumerate(chips)]
        for cp in first:
            cp.start()
        passed = [copy(4 + j, (*chip, c), sibling) for j, chip in enumerate(chips)]
        for j, chip in enumerate(chips):
            copy(1 + j, (*chip, c), me).wait_recv()
            passed[j].start()
        copy(0, sibling, me).wait_recv()
        for j, chip in enumerate(chips):
            copy(4 + j, (*chip, 1 - c), me).wait_recv()
        for cp in first + passed:
            cp.wait_send()
        mine.wait()
        acc = out_ref[0]
        for d in range(1, 8):
            acc = acc + out_ref[d]
        sum_ref[...] = acc

    return pl.pallas_call(
        body, name=name,
        out_shape=(jax.ShapeDtypeStruct((8, rows, LANES), F32), jax.ShapeDtypeStruct((rows, LANES), F32)),
        in_specs=[pl.BlockSpec(memory_space=pltpu.VMEM)],
        out_specs=(pl.BlockSpec(memory_space=pltpu.VMEM), pl.BlockSpec(memory_space=pltpu.VMEM)),
        scratch_shapes=[pltpu.SemaphoreType.DMA((7,)), pltpu.SemaphoreType.DMA((7,)), pltpu.SemaphoreType.DMA],
        compiler_params=_cparams(),
    )(xs)


def _gather_shards(srcs, name):
    na = len(srcs)

    def body(*refs):
        src_refs, out_refs = refs[:na], refs[na:2 * na]
        send_sems, recv_sems = refs[2 * na:]
        x, y, c = lax.axis_index("x"), lax.axis_index("y"), lax.axis_index("c")
        sibling = (x, y, 1 - c)
        peers = [(1 - x, y), (x, 1 - y), (1 - x, 1 - y)]

        def copy(a, k, src, slot, which, to):
            return pltpu.make_async_remote_copy(
                src_ref=src, dst_ref=out_refs[a].at[slot, which],
                send_sem=send_sems.at[a * 6 + k], recv_sem=recv_sems.at[a * 6 + k],
                device_id=to, device_id_type=MESH)

        mine = 2 * x + y
        first = [copy(a, k, src_refs[a].at[c], mine, c, (px, py, c))
                 for a in range(na) for k, (px, py) in enumerate(peers)]
        for cp in first:
            cp.start()
        passed = []
        for a in range(na):
            for k, (px, py) in enumerate(peers):
                slot = 2 * px + py
                landed = out_refs[a].at[slot, c]
                copy(a, k, landed, slot, c, (px, py, c)).wait_recv()
                fwd = copy(a, 3 + k, landed, slot, c, sibling)
                fwd.start()
                passed.append(fwd)
        for a in range(na):
            for k, (px, py) in enumerate(peers):
                slot = 2 * px + py
                copy(a, 3 + k, out_refs[a].at[slot, 1 - c], slot, 1 - c, sibling).wait_recv()
        for cp in first + passed:
            cp.wait_send()

    anyspace = pl.BlockSpec(memory_space=pl.ANY)
    return pl.pallas_call(
        body, name=name,
        out_shape=tuple(jax.ShapeDtypeStruct((4,) + tuple(a.shape), a.dtype) for a in srcs),
        in_specs=[anyspace] * na, out_specs=(anyspace,) * na,
        scratch_shapes=[pltpu.SemaphoreType.DMA((6 * na,)), pltpu.SemaphoreType.DMA((6 * na,))],
        compiler_params=_cparams(),
    )(*srcs)


def _swap_sibling(srcs, name, other_half=False):
    na = len(srcs)

    def body(*refs):
        src_refs, out_refs = refs[:na], refs[na:2 * na]
        send_sems, recv_sems = refs[2 * na:]
        x, y, c = lax.axis_index("x"), lax.axis_index("y"), lax.axis_index("c")
        copies = [pltpu.make_async_remote_copy(
            src_ref=src_refs[a].at[:, 1 - c] if other_half else src_refs[a],
            dst_ref=out_refs[a], send_sem=send_sems.at[a], recv_sem=recv_sems.at[a],
            device_id=(x, y, 1 - c), device_id_type=MESH) for a in range(na)]
        for cp in copies:
            cp.start()
        for cp in copies:
            cp.wait()

    def out_of(a):
        shape = (a.shape[0],) + tuple(a.shape[2:]) if other_half else a.shape
        return jax.ShapeDtypeStruct(shape, a.dtype)

    anyspace = pl.BlockSpec(memory_space=pl.ANY)
    return pl.pallas_call(
        body, name=name, out_shape=tuple(out_of(a) for a in srcs),
        in_specs=[anyspace] * na, out_specs=(anyspace,) * na,
        scratch_shapes=[pltpu.SemaphoreType.DMA((na,)), pltpu.SemaphoreType.DMA((na,))],
        compiler_params=_cparams(),
    )(*srcs)


def _mod_cols(c_all, w, b):
    n = w.shape[1]

    def body(c_ref, w_ref, b_ref, o_ref):
        o_ref[...] = _dot(c_ref[...], w_ref[...], HI) + b_ref[...]

    return pl.pallas_call(
        body, name="mod_cols", out_shape=jax.ShapeDtypeStruct((8, n), F32),
        compiler_params=_cparams())(c_all, w, b)


def _grad_w_ada(c_all, dmod_cols):
    n = dmod_cols.shape[1]

    def body(c_ref, d_ref, o_ref):
        o_ref[...] = _dot_tn(c_ref[...], d_ref[...], HI)

    return pl.pallas_call(
        body, name="grad_w_ada", out_shape=jax.ShapeDtypeStruct((D_MODEL, n), F32),
        compiler_params=_cparams())(c_all, dmod_cols)


def _prenorm_proj(x, gs, shift, wqkv_t, wrest_t):
    s = x.shape[0]
    tm = min(TM_PROJ, s)
    nq, nr = wqkv_t.shape[0], wrest_t.shape[0]

    def body(x_ref, gs_ref, sh_ref, wq_ref, wr_ref, h_ref, qkv_ref, rest_ref):
        xv = x_ref[...]
        r = lax.rsqrt(jnp.mean(xv * xv, axis=-1, keepdims=True) + EPS)
        h = (xv * r * gs_ref[...] + sh_ref[...]).astype(BF16)
        h_ref[...] = h
        qkv_ref[...] = _dot_nt(h, wq_ref[...]).astype(BF16)
        rest_ref[...] = _dot_nt(h, wr_ref[...])

    def rows(width):
        return pl.BlockSpec((tm, width), lambda i: (i, 0))

    return pl.pallas_call(
        body, name="prenorm_proj", grid=(s // tm,),
        in_specs=[rows(D_MODEL), _const((1, D_MODEL)), _const((1, D_MODEL)), _const((nq, D_MODEL)),
                  _const((nr, D_MODEL))],
        out_specs=(rows(D_MODEL), rows(nq), rows(nr)),
        out_shape=(jax.ShapeDtypeStruct((s, D_MODEL), BF16), jax.ShapeDtypeStruct((s, nq), BF16),
                   jax.ShapeDtypeStruct((s, nr), F32)),
        compiler_params=_cparams(("parallel",)))(x, gs, shift, wqkv_t, wrest_t)


def _grad_w_rows(h, ds):
    s = h.shape[0]
    tk = min(TK_ACC, s)
    nd = len(ds)
    widths = [d.shape[1] for d in ds]

    def body(*refs):
        h_ref, d_refs = refs[0], refs[1:1 + nd]
        out_refs, accs = refs[1 + nd:1 + 2 * nd], refs[1 + 2 * nd:]
        step = pl.program_id(0)

        @pl.when(step == 0)
        def _():
            for acc in accs:
                acc[...] = jnp.zeros_like(acc)

        hv = h_ref[...]
        for d_ref, acc in zip(d_refs, accs):
            acc[...] += _dot_tn(d_ref[...], hv)

        @pl.when(step == s // tk - 1)
        def _():
            for acc, out in zip(accs, out_refs):
                pltpu.sync_copy(acc, out)

    anyspace = pl.BlockSpec(memory_space=pl.ANY)
    return pl.pallas_call(
        body, name="grad_w_in", grid=(s // tk,),
        in_specs=[pl.BlockSpec((tk, D_MODEL), lambda k: (k, 0))]
                 + [pl.BlockSpec((tk, w), lambda k: (k, 0)) for w in widths],
        out_specs=(anyspace,) * nd,
        out_shape=tuple(jax.ShapeDtypeStruct((w, D_MODEL), F32) for w in widths),
        scratch_shapes=[pltpu.VMEM((w, D_MODEL), F32) for w in widths],
        compiler_params=_cparams(("arbitrary",)))(h, *ds)


BIAS_ONES = 32


def _bias_selectors():
    rows = jnp.arange(LANES)[None, :, None]
    cols = jnp.arange(LANES)[None, None, :]
    term = jnp.arange(3)[:, None, None]
    return ((rows < HEADS) & (cols == 3 * rows + term)).astype(F32)


def _fcum(rest, bf128, selk):
    s = rest.shape[0]
    tb = min(TB_CUM, s)

    def body(fz_ref, bf_ref, selk_ref, fpc_ref, ft_ref, kb_ref, carry_ref):
        @pl.when(pl.program_id(0) == 0)
        def _():
            carry_ref[...] = jnp.zeros_like(carry_ref)

        z = fz_ref[...] + bf_ref[...]
        logf = jnp.minimum(z, 0.0) - jnp.log(1.0 + jnp.exp(-jnp.abs(z)))
        r = lax.broadcasted_iota(jnp.int32, (tb, tb), 0)
        c = lax.broadcasted_iota(jnp.int32, (tb, tb), 1)
        tri = (c <= r).astype(F32)
        f = _dot_sel(tri, logf) + carry_ref[0:1, :]
        carry_ref[0:1, :] = f[tb - 1:tb, :]
        for pp in range(4):
            fpc_ref[:, pp * LANES:(pp + 1) * LANES] = f if pp == 0 else pltpu.roll(f, LANES - 2 * pp, 1)
        ft_ref[...] = jnp.transpose(f)[0:HEADS, :]
        lane = lax.broadcasted_iota(jnp.int32, (tb, LANES), 1)
        ones = ((lane >= BIAS_ONES) & (lane < BIAS_ONES + 3 * HEADS)).astype(F32)
        terms = sum(_dot(part, selk_ref[j].astype(BF16)) for j, part in enumerate(_split3(-f)))
        kb_ref[...] = (terms + ones).astype(BF16)

    return pl.pallas_call(
        body, name="forget_cumsum", grid=(s // tb,),
        in_specs=[pl.BlockSpec((tb, LANES), lambda i: (i, R_F // LANES)), _const((1, LANES)), _const((3, LANES, LANES))],
        out_specs=(pl.BlockSpec((tb, 4 * LANES), lambda i: (i, 0)), pl.BlockSpec((HEADS, tb), lambda i: (0, i)),
                   pl.BlockSpec((tb, LANES), lambda i: (i, 0))),
        out_shape=(jax.ShapeDtypeStruct((s, 4 * LANES), F32), jax.ShapeDtypeStruct((HEADS, s), F32),
                   jax.ShapeDtypeStruct((s, LANES), BF16)),
        scratch_shapes=[pltpu.VMEM((8, LANES), F32)],
        compiler_params=_cparams(("arbitrary",)))(rest, bf128, selk)


def _dfcum(dfk, dfq, rest, bf128):
    s = rest.shape[0]
    tb = min(TB_CUM, s)
    nb = s // tb

    def body(dk_ref, dq_ref, fz_ref, bf_ref, df_ref, dbf_ref, carry_ref):
        @pl.when(pl.program_id(0) == 0)
        def _():
            carry_ref[...] = jnp.zeros_like(carry_ref)
            dbf_ref[...] = jnp.zeros_like(dbf_ref)

        pair_lanes = lax.broadcasted_iota(jnp.int32, (tb, LANES), 1) < 2
        d = jnp.zeros((tb, LANES), F32)
        for pp in range(4):
            tile = dk_ref[:, pp * LANES:(pp + 1) * LANES] + dq_ref[:, pp * LANES:(pp + 1) * LANES]
            tile = jnp.where(pair_lanes, tile, 0.0)
            d = d + (tile if pp == 0 else pltpu.roll(tile, 2 * pp, 1))
        r = lax.broadcasted_iota(jnp.int32, (tb, tb), 0)
        c = lax.broadcasted_iota(jnp.int32, (tb, tb), 1)
        triu = (c >= r).astype(F32)
        dlogf = _dot_sel(triu, d) + carry_ref[0:1, :]
        carry_ref[0:1, :] = dlogf[0:1, :]
        z = fz_ref[...] + bf_ref[...]
        df = dlogf * (1.0 / (1.0 + jnp.exp(z)))
        df_ref[...] = df.astype(BF16)
        dbf_ref[0:1, :] += jnp.sum(df, axis=0, keepdims=True)

    return pl.pallas_call(
        body, name="forget_grad", grid=(nb,),
        in_specs=[pl.BlockSpec((tb, 4 * LANES), lambda i: (nb - 1 - i, 0)),
                  pl.BlockSpec((tb, 4 * LANES), lambda i: (nb - 1 - i, 0)),
                  pl.BlockSpec((tb, LANES), lambda i: (nb - 1 - i, R_F // LANES)),
                  _const((1, LANES))],
        out_specs=(pl.BlockSpec((tb, LANES), lambda i: (nb - 1 - i, 0)), pl.BlockSpec((8, LANES), lambda i: (0, 0))),
        out_shape=(jax.ShapeDtypeStruct((s, LANES), BF16), jax.ShapeDtypeStruct((8, LANES), F32)),
        scratch_shapes=[pltpu.VMEM((8, LANES), F32)],
        compiler_params=_cparams(("arbitrary",)))(dfk, dfq, rest, bf128)


def _scaled(q):
    return (q.astype(F32) * (HEAD_DIM ** -0.5)).astype(BF16)


def _attn_fwd(qkv, kbias, fpc):
    s = qkv.shape[0]
    t = min(T_ATT, s)
    n = s // t
    ch = min(ATT_CHUNK, t)
    wide = 2 * LANES
    pairs = ATT_PAIRS
    width = pairs * LANES
    groups = 4 // pairs

    def body(q_ref, k_ref, v_ref, kb_ref, fc_ref, o_ref, lse_ref, s_scr, p_scr, m_scr, a_scr, acc_scr):
        i = pl.program_id(1)
        g = pl.program_id(0)
        lane = lax.broadcasted_iota(jnp.int32, (t, LANES), 1)
        first = lane < HEAD_DIM
        ones_col = ((lane == 0).astype(BF16), (lane == 1).astype(BF16))
        m_scr[...] = jnp.full(m_scr.shape, NEG, F32)
        acc_scr[...] = jnp.zeros_like(acc_scr)
        qm = []
        for pp in range(pairs):
            q = _scaled(q_ref[:, pp * LANES:(pp + 1) * LANES])
            zq = jnp.zeros_like(q)
            for hh in range(2):
                head = 2 * (g * pairs + pp) + hh
                fq = _split3(fc_ref[:, pp * LANES + hh:pp * LANES + hh + 1])
                bias = jnp.where((lane >= 3 * head) & (lane < 3 * head + 3), 1.0, 0.0).astype(BF16)
                for term in range(3):
                    bias = jnp.where(lane == BIAS_ONES + 3 * head + term, fq[term], bias)
                qh = jnp.where(first, q, zq) if hh == 0 else jnp.where(first, zq, q)
                qm.append(jnp.concatenate([qh, bias], axis=1))

        def step(j, masked):
            r0 = pl.multiple_of(j * t, t)
            vaug = []
            kbias_blk = kb_ref[pl.ds(r0, t), :]
            for pp in range(pairs):
                kb = jnp.concatenate([k_ref[pl.ds(r0, t), pp * LANES:(pp + 1) * LANES], kbias_blk], axis=1)
                vb = v_ref[pl.ds(r0, t), pp * LANES:(pp + 1) * LANES]
                zv = jnp.zeros_like(vb)
                vaug += [jnp.concatenate([jnp.where(first, vb, zv), ones_col[0]], axis=1),
                         jnp.concatenate([jnp.where(first, zv, vb), ones_col[1]], axis=1)]
                for hh in range(2):
                    s_scr[2 * pp + hh] = _dot_nt(qm[2 * pp + hh], kb)
            pv = []
            for hd in range(2 * pairs):
                for c in range(t // ch):
                    rows = pl.ds(c * ch, ch)
                    hi = min(t, (c * ch // LANES + 1) * LANES) if masked else t
                    sc = s_scr[hd, rows, 0:hi]
                    if masked:
                        rq = c * ch + lax.broadcasted_iota(jnp.int32, (ch, hi), 0)
                        ck = lax.broadcasted_iota(jnp.int32, (ch, hi), 1)
                        sc = jnp.where(ck <= rq, sc, NEG)
                    m_old = m_scr[hd, rows, :]
                    m_new = jnp.maximum(m_old, jnp.max(sc, axis=1, keepdims=True))
                    p_scr[hd, rows, 0:hi] = jnp.exp(sc - m_new).astype(BF16)
                    if hi < t:
                        p_scr[hd, rows, hi:t] = jnp.zeros((ch, t - hi), BF16)
                    a_scr[hd, rows, :] = jnp.exp(m_old - m_new)
                    m_scr[hd, rows, :] = m_new
                pv.append(_dot(p_scr[hd], vaug[hd]))
            for pp in range(pairs):
                a0, a1 = a_scr[2 * pp], a_scr[2 * pp + 1]
                alpha = jnp.concatenate([jnp.where(first, a0, a1), jnp.where(lane == 0, a0, a1)], axis=1)
                acc_scr[pp] = acc_scr[pp] * alpha + pv[2 * pp] + pv[2 * pp + 1]
            return 0

        lax.fori_loop(0, i, lambda j, _: step(j, False), 0)
        step(i, True)
        lse = jnp.zeros((t, LANES), F32)
        for pp in range(pairs):
            l0 = acc_scr[pp, :, LANES:LANES + 1]
            l1 = acc_scr[pp, :, LANES + 1:LANES + 2]
            o_ref[:, pp * LANES:(pp + 1) * LANES] = acc_scr[pp, :, 0:LANES] * jnp.where(first, 1.0 / l0, 1.0 / l1)
            lse = jnp.where(lane == 2 * pp, m_scr[2 * pp] + jnp.log(l0), lse)
            lse = jnp.where(lane == 2 * pp + 1, m_scr[2 * pp + 1] + jnp.log(l1), lse)
        lse_ref[...] = lse

    blk = pl.BlockSpec((t, width), lambda g, i: (i, g))
    return pl.pallas_call(
        body, name="attn_fwd", grid=(groups, n),
        in_specs=[blk,
                  pl.BlockSpec((s, width), lambda g, i: (0, groups + g)),
                  pl.BlockSpec((s, width), lambda g, i: (0, 2 * groups + g)),
                  pl.BlockSpec((s, LANES), lambda g, i: (0, 0)),
                  blk],
        out_specs=(blk, pl.BlockSpec((t, LANES), lambda g, i: (i, g))),
        out_shape=(jax.ShapeDtypeStruct((s, FOX_W), F32), jax.ShapeDtypeStruct((s, groups * LANES), F32)),
        scratch_shapes=[pltpu.VMEM((2 * pairs, t, t), F32), pltpu.VMEM((2 * pairs, t, t), BF16),
                        pltpu.VMEM((2 * pairs, t, 1), F32), pltpu.VMEM((2 * pairs, t, 1), F32),
                        pltpu.VMEM((pairs, t, wide), F32)],
        compiler_params=_cparams(("parallel", "arbitrary")))(qkv, qkv, qkv, kbias, fpc)


def _attn_bwd(qkv, do, lse5, dlt5, frow5, fpc):
    s = qkv.shape[0]
    t = min(T_ATT, s)
    n = s // t
    wide = 2 * LANES

    ch = min(ATT_CHUNK, t)

    def body(q_ref, do_ref, k_ref, v_ref, lse_ref, dl_ref, fr_ref, fc_ref,
             dq_ref, dk_ref, dv_ref, dfk_ref, dfq_ref, dq_acc, st_scr, dp_scr, pt_scr, ds_scr, dk_acc, dv_acc, fk_scr):
        j = pl.program_id(1)

        @pl.when(j == 0)
        def _():
            dq_acc[...] = jnp.zeros_like(dq_acc)

        dk_acc[...] = jnp.zeros_like(dk_acc)
        dv_acc[...] = jnp.zeros_like(dv_acc)
        lane = lax.broadcasted_iota(jnp.int32, (t, LANES), 1)
        first = lane < HEAD_DIM
        ones_col = ((lane == 0).astype(BF16), (lane == 1).astype(BF16))
        kb = k_ref[...]
        vb = v_ref[...]
        zk = jnp.zeros_like(kb)
        kaug = (jnp.concatenate([jnp.where(first, kb, zk), ones_col[0]], axis=1),
                jnp.concatenate([jnp.where(first, zk, kb), ones_col[1]], axis=1))
        fk_scr[0] = fc_ref[:, 0:1]
        fk_scr[1] = fc_ref[:, 1:2]

        def step(blocks, masked):
            chains = []
            for bi, i in enumerate(blocks):
                r0 = pl.multiple_of(i * t, t)
                qb = _scaled(q_ref[pl.ds(r0, t), :])
                dob = do_ref[pl.ds(r0, t), :]
                zq = jnp.zeros_like(qb)
                qm = (jnp.where(first, qb, zq), jnp.where(first, zq, qb))
                dom = (jnp.where(first, dob, zq), jnp.where(first, zq, dob))
                for hh in range(2):
                    st_scr[2 * bi + hh] = _dot_nt(kb, qm[hh])
                    dp_scr[2 * bi + hh] = _dot_nt(vb, dom[hh])
                    chains.append((i, hh, qm[hh], dom[hh]))
            dq_add = [jnp.zeros((t, wide), F32) for _ in blocks]
            for cn, (i, hh, qmh, domh) in enumerate(chains):
                bias = fr_ref[0, hh, i] - lse_ref[0, hh, i]
                dl = dl_ref[0, hh, i]
                for c in range(t // ch):
                    rows = pl.ds(c * ch, ch)
                    lo = c * ch // LANES * LANES if masked else 0
                    st = st_scr[cn, rows, lo:t] + (bias[:, lo:t] - fk_scr[hh, rows, :])
                    if masked:
                        rk = c * ch + lax.broadcasted_iota(jnp.int32, (ch, t - lo), 0)
                        cq = lo + lax.broadcasted_iota(jnp.int32, (ch, t - lo), 1)
                        st = jnp.where(rk <= cq, st, NEG)
                    pt = jnp.exp(st)
                    pt_scr[cn, rows, lo:t] = pt.astype(BF16)
                    ds_scr[cn, rows, lo:t] = (pt * (dp_scr[cn, rows, lo:t] - dl[:, lo:t])).astype(BF16)
                    if lo > 0:
                        pt_scr[cn, rows, 0:lo] = jnp.zeros((ch, lo), BF16)
                        ds_scr[cn, rows, 0:lo] = jnp.zeros((ch, lo), BF16)
                dsb = ds_scr[cn]
                dv_acc[...] += _dot(pt_scr[cn], domh)
                dk_acc[...] += _dot(dsb, jnp.concatenate([qmh, ones_col[hh]], axis=1))
                dq_add[cn // 2] = dq_add[cn // 2] + _dot_tn(dsb, kaug[hh])
            for bi, i in enumerate(blocks):
                dq_acc[pl.ds(pl.multiple_of(i * t, t), t), :] += dq_add[bi]
            return 0

        step([j], True)
        odd = (n - 1 - j) % 2
        lax.fori_loop(0, odd, lambda _, carry: step([j + 1], False), 0)
        first_pair = j + 1 + odd
        lax.fori_loop(0, (n - first_pair) // 2,
                      lambda p, _: step([first_pair + 2 * p, first_pair + 2 * p + 1], False), 0)
        dk_ref[...] = dk_acc[:, 0:LANES].astype(BF16)
        dv_ref[...] = dv_acc[...].astype(BF16)
        dfk_ref[...] = -dk_acc[:, LANES:wide]

        @pl.when(j == n - 1)
        def _():
            dq_ref[...] = (dq_acc[:, 0:LANES] * (HEAD_DIM ** -0.5)).astype(BF16)
            dfq_ref[...] = dq_acc[:, LANES:wide]

    stat = pl.BlockSpec((1, 2, n, 1, t), lambda h, j: (h, 0, 0, 0, 0))
    blk = pl.BlockSpec((t, LANES), lambda h, j: (j, h))
    full = pl.BlockSpec((s, LANES), lambda h, j: (0, h))
    return pl.pallas_call(
        body, name="attn_bwd", grid=(4, n),
        in_specs=[full, full,
                  pl.BlockSpec((t, LANES), lambda h, j: (j, 4 + h)),
                  pl.BlockSpec((t, LANES), lambda h, j: (j, 8 + h)),
                  stat, stat, stat, blk],
        out_specs=(full, blk, blk, blk, full),
        out_shape=(jax.ShapeDtypeStruct((s, FOX_W), BF16), jax.ShapeDtypeStruct((s, FOX_W), BF16),
                   jax.ShapeDtypeStruct((s, FOX_W), BF16), jax.ShapeDtypeStruct((s, 4 * LANES), F32),
                   jax.ShapeDtypeStruct((s, 4 * LANES), F32)),
        scratch_shapes=[pltpu.VMEM((s, wide), F32), pltpu.VMEM((4, t, t), F32), pltpu.VMEM((4, t, t), F32),
                        pltpu.VMEM((4, t, t), BF16), pltpu.VMEM((4, t, t), BF16), pltpu.VMEM((t, wide), F32),
                        pltpu.VMEM((t, LANES), F32), pltpu.VMEM((2, t, 1), F32)],
        compiler_params=_cparams(("parallel", "arbitrary")))(qkv, do, qkv, qkv, lse5, dlt5, frow5, fpc)


def _ssm_block_params(a_re, a_im, log_dt, b_re, b_im):
    dt = jnp.exp(log_dt)[:, None]
    mag = jnp.exp(a_re * dt)
    ar = mag * jnp.cos(a_im * dt)
    ai = mag * jnp.sin(a_im * dt)
    den = a_re * a_re + a_im * a_im
    nr = ar - 1.0
    cr = (nr * a_re + ai * a_im) / den
    ci = (ai * a_re - nr * a_im) / den
    bbr = cr[:, :, None] * b_re - ci[:, :, None] * b_im
    bbi = cr[:, :, None] * b_im + ci[:, :, None] * b_re
    return ar, ai, bbr, bbi


def _block_diag(blocks):
    g, r, c = blocks.shape
    eye = jnp.eye(g, dtype=blocks.dtype)
    return (blocks[:, :, None, :] * eye[:, None, :, None]).reshape(g * r, g * c)


def _scan_consts(a_re, a_im, log_dt, seg, reverse):
    dt = jnp.exp(log_dt)[:, None]
    lr = (a_re * dt).reshape(1, NSTATE)
    li = (a_im * dt).reshape(1, NSTATE)
    if reverse:
        li = -li
    rows = jnp.arange(8, dtype=F32)[:, None]

    def power(k):
        mag = jnp.exp(k * lr)
        return mag * jnp.cos(k * li), mag * jnp.sin(k * li)

    tiles = list(power(1.0))
    for k in (1, 2, 4):
        keep = (rows < 8 - k) if reverse else (rows >= k)
        pr, pi_ = power(float(k * seg))
        tiles += [jnp.where(keep, pr, 0.0), jnp.where(keep, pi_, 0.0)]
    tiles += list(power(seg * ((8.0 - rows) if reverse else (rows + 1.0))))
    tiles = jnp.stack([jnp.broadcast_to(tl, (8, NSTATE)) for tl in tiles])
    steps = jnp.arange(seg, dtype=F32)[:, None]
    table = jnp.stack([jnp.broadcast_to(p[:, None, :], (seg, 8, NSTATE))
                       for p in power((seg - steps) if reverse else (steps + 1.0))])
    return tiles, table


_SCAN_W = 1024
_HALF_W = S5_W // 2
_HALF_S = NSTATE // 2


def _compact_diag(blocks_re, blocks_im):
    hg = GROUPS // 2
    return jnp.concatenate([_block_diag(b[h * hg:(h + 1) * hg]) for b in (blocks_re, blocks_im) for h in range(2)],
                           axis=1)


def _half_expand(v, w_ref, out_ref):
    for half in range(2):
        vh = v[:, half * _HALF_W:(half + 1) * _HALF_W]
        for part in range(2):
            c0 = part * NSTATE + half * _HALF_S
            out_ref[:, c0:c0 + _HALF_S] = _dot(vh, w_ref[:, c0:c0 + _HALF_S])


def _half_contract(x_ref, w_ref, half):
    out = None
    for part in range(2):
        r0 = part * NSTATE + half * _HALF_S
        term = _dot_nt(x_ref[:, r0:r0 + _HALF_S].astype(BF16), w_ref[:, r0:r0 + _HALF_S])
        out = term if out is None else out + term
    return out


def _half_outer(v, x_ref, acc_ref):
    for half in range(2):
        vh = v[:, half * _HALF_W:(half + 1) * _HALF_W]
        for part in range(2):
            c0 = part * NSTATE + half * _HALF_S
            acc_ref[:, c0:c0 + _HALF_S] += _dot_tn(vh, x_ref[:, c0:c0 + _HALF_S].astype(BF16))


def _segment_perm(tb):
    seg = tb // 8
    row = lax.broadcasted_iota(jnp.int32, (tb, tb), 0)
    col = lax.broadcasted_iota(jnp.int32, (tb, tb), 1)
    perm = (col == (row % 8) * seg + row // 8).astype(BF16)
    back = (col == (row % seg) * 8 + row // seg).astype(BF16)
    return perm, back


def _segment_ends(re, im, cf_ref, cb_ref, cr, ci, reverse):
    for n_, k in enumerate((1, 2, 4)):
        kr = cf_ref[2 + 2 * n_, :, cr]
        ki = cf_ref[3 + 2 * n_, :, cr]
        sr = pltpu.roll(re, 8 - k if reverse else k, 0)
        si = pltpu.roll(im, 8 - k if reverse else k, 0)
        re, im = re + kr * sr - ki * si, im + kr * si + ki * sr
    cbr, cbi = cb_ref[:, cr], cb_ref[:, ci]
    pr, pi_ = cf_ref[8, :, cr], cf_ref[9, :, cr]
    re, im = re + pr * cbr - pi_ * cbi, im + pr * cbi + pi_ * cbr
    edge = lax.broadcasted_iota(jnp.int32, re.shape, 0) == (7 if reverse else 0)
    in_r = jnp.where(edge, cbr, pltpu.roll(re, 7 if reverse else 1, 0))
    in_i = jnp.where(edge, cbi, pltpu.roll(im, 7 if reverse else 1, 0))
    out = slice(0, 1) if reverse else slice(7, 8)
    cb_ref[:, cr] = jnp.broadcast_to(re[out, :], re.shape)
    cb_ref[:, ci] = jnp.broadcast_to(im[out, :], im.shape)
    return in_r, in_i


def _ssm_fwd(rest, bd, cd, consts, table):
    s = rest.shape[0]
    tb = min(TB_SSM, s)
    seg = tb // 8
    ns2 = 2 * NSTATE

    def body(u_ref, bd_ref, cd_ref, cf_ref, tab_ref, y_ref, x_ref, cb_ref):
        @pl.when(pl.program_id(0) == 0)
        def _():
            cb_ref[...] = jnp.zeros_like(cb_ref)

        perm, back = _segment_perm(tb)
        _half_expand(_dot(perm, u_ref[...].astype(BF16)).astype(BF16), bd_ref, x_ref)
        for cc in range(NSTATE // _SCAN_W):
            cr = pl.ds(cc * _SCAN_W, _SCAN_W)
            ci = pl.ds(NSTATE + cc * _SCAN_W, _SCAN_W)
            ar, ai = cf_ref[0, :, cr], cf_ref[1, :, cr]

            def local(i, carry, cr=cr, ci=ci, ar=ar, ai=ai):
                re, im = carry
                rows = pl.ds(pl.multiple_of(i * 8, 8), 8)
                re, im = ar * re - ai * im + x_ref[rows, cr], ar * im + ai * re + x_ref[rows, ci]
                x_ref[rows, cr] = re
                x_ref[rows, ci] = im
                return re, im

            zero = jnp.zeros((8, _SCAN_W), F32)
            re, im = lax.fori_loop(0, seg, local, (zero, zero))
            in_r, in_i = _segment_ends(re, im, cf_ref, cb_ref, cr, ci, False)

            def fix(i, _, cr=cr, ci=ci, in_r=in_r, in_i=in_i):
                rows = pl.ds(pl.multiple_of(i * 8, 8), 8)
                tr, ti = tab_ref[0, i, :, cr], tab_ref[1, i, :, cr]
                x_ref[rows, cr] += tr * in_r - ti * in_i
                x_ref[rows, ci] += tr * in_i + ti * in_r
                return 0

            lax.fori_loop(0, seg, fix, 0)
        y_p = jnp.concatenate([_half_contract(x_ref, cd_ref, half) for half in range(2)], axis=1)
        y_ref[...] = _dot_sel(back, y_p, terms=2)

    return pl.pallas_call(
        body, name="ssm_fwd", grid=(s // tb,),
        in_specs=[pl.BlockSpec((tb, S5_W), lambda i: (i, R_U // S5_W)), _const((_HALF_W, ns2)), _const((_HALF_W, ns2)),
                  _const((10, 8, NSTATE)), _const((2, seg, 8, NSTATE))],
        out_specs=(pl.BlockSpec((tb, S5_W), lambda i: (i, 0)), pl.BlockSpec((tb, ns2), lambda i: (i, 0))),
        out_shape=(jax.ShapeDtypeStruct((s, S5_W), F32), jax.ShapeDtypeStruct((s, ns2), F32)),
        scratch_shapes=[pltpu.VMEM((8, ns2), F32)],
        compiler_params=_cparams(("arbitrary",)))(rest, bd, cd, consts, table)


def _ssm_bwd(dys, xs, rest, bd, cd, consts, table, dskip):
    s = dys.shape[0]
    tb = min(TB_SSM, s)
    seg = tb // 8
    nb = s // tb
    ns2 = 2 * NSTATE

    def body(dy_ref, x_ref, u_ref, bd_ref, cd_ref, cf_ref, tab_ref, dsk_ref, du_ref, gb_ref, gc_ref, da_ref,
             g_ref, cb_ref, acc_b, acc_c):
        step = pl.program_id(0)

        @pl.when(step == 0)
        def _():
            cb_ref[...] = jnp.zeros_like(cb_ref)
            acc_b[...] = jnp.zeros_like(acc_b)
            acc_c[...] = jnp.zeros_like(acc_c)
            da_ref[...] = jnp.zeros_like(da_ref)

        perm, back = _segment_perm(tb)
        dy = dy_ref[...]
        dy_p = _dot(perm, dy.astype(BF16)).astype(BF16)
        u_p = _dot(perm, u_ref[...].astype(BF16)).astype(BF16)
        _half_expand(dy_p, cd_ref, g_ref)
        for cc in range(NSTATE // _SCAN_W):
            cr = pl.ds(cc * _SCAN_W, _SCAN_W)
            ci = pl.ds(NSTATE + cc * _SCAN_W, _SCAN_W)
            ar, ai = cf_ref[0, :, cr], cf_ref[1, :, cr]

            def local(ii, carry, cr=cr, ci=ci, ar=ar, ai=ai):
                re, im = carry
                rows = pl.ds(pl.multiple_of((seg - 1 - ii) * 8, 8), 8)
                re, im = ar * re - ai * im + g_ref[rows, cr], ar * im + ai * re + g_ref[rows, ci]
                g_ref[rows, cr] = re
                g_ref[rows, ci] = im
                return re, im

            zero = jnp.zeros((8, _SCAN_W), F32)
            re, im = lax.fori_loop(0, seg, local, (zero, zero))
            in_r, in_i = _segment_ends(re, im, cf_ref, cb_ref, cr, ci, True)

            def fix(ii, carry, cr=cr, ci=ci, in_r=in_r, in_i=in_i):
                nr, ni, acr, aci = carry
                i = seg - 1 - ii
                rows = pl.ds(pl.multiple_of(i * 8, 8), 8)
                tr, ti = tab_ref[0, i, :, cr], tab_ref[1, i, :, cr]
                gr = g_ref[rows, cr] + tr * in_r - ti * in_i
                gi = g_ref[rows, ci] + tr * in_i + ti * in_r
                g_ref[rows, cr] = gr
                g_ref[rows, ci] = gi
                xr, xi = x_ref[rows, cr], x_ref[rows, ci]
                return gr, gi, acr + nr * xr + ni * xi, aci + ni * xr - nr * xi

            _, _, acr, aci = lax.fori_loop(0, seg, fix, (in_r, in_i, zero, zero))
            da_ref[:, cr] += acr
            da_ref[:, ci] += aci
        du_p = jnp.concatenate([_half_contract(g_ref, bd_ref, half) for half in range(2)], axis=1)
        du_ref[...] = (_dot_sel(back, du_p, terms=2) + dy * dsk_ref[...]).astype(BF16)
        _half_outer(u_p, g_ref, acc_b)
        _half_outer(dy_p, x_ref, acc_c)

        @pl.when(step == nb - 1)
        def _():
            for g in range(GROUPS):
                src = slice((g % (GROUPS // 2)) * GCH, (g % (GROUPS // 2) + 1) * GCH)
                dst = slice(g * GCH, (g + 1) * GCH)
                for part in range(2):
                    cols = slice(part * NSTATE + g * STATE, part * NSTATE + (g + 1) * STATE)
                    gb_ref[dst, part * STATE:(part + 1) * STATE] = acc_b[src, cols]
                    gc_ref[dst, part * STATE:(part + 1) * STATE] = acc_c[src, cols]

    rev = lambda i: (nb - 1 - i, 0)
    small = pl.BlockSpec((S5_W, 2 * STATE), lambda i: (0, 0))
    return pl.pallas_call(
        body, name="ssm_bwd", grid=(nb,),
        in_specs=[pl.BlockSpec((tb, S5_W), rev), pl.BlockSpec((tb, ns2), rev),
                  pl.BlockSpec((tb, S5_W), lambda i: (nb - 1 - i, R_U // S5_W)),
                  _const((_HALF_W, ns2)), _const((_HALF_W, ns2)), _const((10, 8, NSTATE)), _const((2, seg, 8, NSTATE)),
                  _const((1, S5_W))],
        out_specs=(pl.BlockSpec((tb, S5_W), rev), small, small, pl.BlockSpec((8, ns2), lambda i: (0, 0))),
        out_shape=(jax.ShapeDtypeStruct((s, S5_W), BF16), jax.ShapeDtypeStruct((S5_W, 2 * STATE), F32),
                   jax.ShapeDtypeStruct((S5_W, 2 * STATE), F32), jax.ShapeDtypeStruct((8, ns2), F32)),
        scratch_shapes=[pltpu.VMEM((tb, ns2), F32), pltpu.VMEM((8, ns2), F32),
                        pltpu.VMEM((_HALF_W, ns2), F32), pltpu.VMEM((_HALF_W, ns2), F32)],
        compiler_params=_cparams(("arbitrary",)))(dys, xs, rest, bd, cd, consts, table, dskip)


_GELU_C = math.sqrt(2.0 / math.pi)
_GELU_A = 0.044715


def _mid(o, rest, ys0, x, tgt, w, vec, hsel):
    s = o.shape[0]
    tm = min(TM, s)
    nsteps = s // tm
    half = FOX_W

    def body(o_ref, ga_ref, gb_ref, za_ref, u_ref, zb_ref, ys0_ref, x_ref, t_ref,
             wglu_ref, wua_ref, wub_ref, wout_ref, vec_ref, hsel_ref,
             dx2_ref, dga_ref, dgb_ref, do_ref, dza_ref, dzb_ref, dys_ref, dlt_ref,
             gout_hbm, gua_hbm, gub_hbm, gglu_hbm, vout_ref,
             a_out, a_ua, a_ub, a_glu):
        step = pl.program_id(0)

        @pl.when(step == 0)
        def _():
            a_out[...] = jnp.zeros_like(a_out)
            a_ua[...] = jnp.zeros_like(a_ua)
            a_ub[...] = jnp.zeros_like(a_ub)
            a_glu[...] = jnp.zeros_like(a_glu)
            vout_ref[...] = jnp.zeros_like(vout_ref)

        gate = vec_ref[0:1, :]
        gfin = vec_ref[1:2, :]
        dsk = vec_ref[2:3, 0:half]
        bglu = vec_ref[2:3, half:2 * half]

        o_v = o_ref[...]
        za = za_ref[...]
        sza = _sigmoid(za)
        silu_za = za * sza
        ya_b = (o_v * silu_za).astype(BF16)
        u_v = u_ref[...]
        ys = ys0_ref[...] + dsk * u_v
        inner = _GELU_C * (ys + _GELU_A * ys * ys * ys)
        th = jnp.tanh(inner)
        yg = 0.5 * ys * (1.0 + th)
        yg_b = yg.astype(BF16)
        st = _sigmoid(_dot(yg_b, wglu_ref[...]) + bglu)
        yb1 = yg * st
        zb = zb_ref[...]
        szb = _sigmoid(zb)
        silu_zb = zb * szb
        yb_b = (yb1 * silu_zb).astype(BF16)
        ua = _dot(ya_b, wua_ref[...])
        ub = _dot(yb_b, wub_ref[...])
        sga = _sigmoid(ga_ref[...])
        sgb = _sigmoid(gb_ref[...])
        merged_b = (sga * ua + sgb * ub).astype(BF16)
        mo = _dot(merged_b, wout_ref[...])
        x2 = x_ref[...] + gate * mo
        r2 = lax.rsqrt(jnp.mean(x2 * x2, axis=-1, keepdims=True) + EPS)
        x2n = x2 * r2
        diff = x2n * gfin - t_ref[...]
        loss = 0.5 * jnp.sum(jnp.mean(diff * diff, axis=-1, keepdims=True), axis=0, keepdims=True)
        dy = diff * (1.0 / D_MODEL)
        dx2n = dy * gfin
        dx2 = r2 * (dx2n - x2n * jnp.mean(dx2n * x2n, axis=-1, keepdims=True))
        dx2_ref[...] = dx2
        vout_ref[0:1, :] += jnp.sum(dy * x2n, axis=0, keepdims=True)
        vout_ref[1:2, :] += jnp.sum(dx2 * mo, axis=0, keepdims=True)
        vout_ref[3:4, :] += jnp.broadcast_to(loss, (1, D_MODEL))
        dmo_b = (dx2 * gate).astype(BF16)
        dmerged = _dot_nt(dmo_b, wout_ref[...])
        a_out[...] += _dot_tn(merged_b, dmo_b)
        dua_b = (dmerged * sga).astype(BF16)
        dub_b = (dmerged * sgb).astype(BF16)
        dga_ref[...] = (dmerged * ua * sga * (1.0 - sga)).astype(BF16)
        dgb_ref[...] = (dmerged * ub * sgb * (1.0 - sgb)).astype(BF16)
        dya = _dot_nt(dua_b, wua_ref[...])
        dyb = _dot_nt(dub_b, wub_ref[...])
        a_ua[...] += _dot_tn(ya_b, dua_b)
        a_ub[...] += _dot_tn(yb_b, dub_b)
        do_b = (dya * silu_za).astype(BF16)
        do_ref[...] = do_b
        dza_ref[...] = (dya * o_v * (sza * (1.0 + za * (1.0 - sza)))).astype(BF16)
        hsel = hsel_ref[...].astype(BF16)
        dlt_ref[...] = sum(_dot_nt(hsel, part) for part in _split3(do_b.astype(F32) * o_v))
        dyb1 = dyb * silu_zb
        dzb_ref[...] = (dyb * yb1 * (szb * (1.0 + zb * (1.0 - szb)))).astype(BF16)
        dt = dyb1 * yg * st * (1.0 - st)
        dt_b = dt.astype(BF16)
        dyg = dyb1 * st + _dot_nt(dt_b, wglu_ref[...])
        a_glu[...] += _dot_tn(yg_b, dt_b)
        dgelu = 0.5 * (1.0 + th) + 0.5 * ys * (1.0 - th * th) * _GELU_C * (1.0 + 3.0 * _GELU_A * ys * ys)
        dys = dyg * dgelu
        dys_ref[...] = dys
        vout_ref[2:3, 0:half] += jnp.sum(dys * u_v, axis=0, keepdims=True)
        vout_ref[2:3, half:2 * half] += jnp.sum(dt, axis=0, keepdims=True)

        @pl.when(step == nsteps - 1)
        def _():
            pltpu.sync_copy(a_out, gout_hbm)
            pltpu.sync_copy(a_ua, gua_hbm)
            pltpu.sync_copy(a_ub, gub_hbm)
            pltpu.sync_copy(a_glu, gglu_hbm)

    def rows(width, col=0):
        return pl.BlockSpec((tm, width), lambda i, col=col: (i, col))

    anyspace = pl.BlockSpec(memory_space=pl.ANY)
    wshapes = [(S5_W, S5_W), (FOX_W, D_MODEL), (S5_W, D_MODEL), (D_MODEL, D_MODEL)]
    return pl.pallas_call(
        body, name="mid", grid=(nsteps,),
        in_specs=[rows(FOX_W), rows(D_MODEL, R_GA // D_MODEL), rows(D_MODEL, R_GB // D_MODEL),
                  rows(FOX_W, R_ZA // FOX_W), rows(S5_W, R_U // S5_W), rows(S5_W, R_ZB // S5_W),
                  rows(S5_W), rows(D_MODEL), rows(D_MODEL)]
                 + [_const(sh) for sh in wshapes]
                 + [_const((8, D_MODEL)), _const((HEADS, FOX_W))],
        out_specs=(rows(D_MODEL), rows(D_MODEL), rows(D_MODEL), rows(FOX_W), rows(FOX_W), rows(S5_W), rows(S5_W),
                   pl.BlockSpec((HEADS, tm), lambda i: (0, i)),
                   anyspace, anyspace, anyspace, anyspace, pl.BlockSpec((8, D_MODEL), lambda i: (0, 0))),
        out_shape=(jax.ShapeDtypeStruct((s, D_MODEL), F32), jax.ShapeDtypeStruct((s, D_MODEL), BF16),
                   jax.ShapeDtypeStruct((s, D_MODEL), BF16), jax.ShapeDtypeStruct((s, FOX_W), BF16),
                   jax.ShapeDtypeStruct((s, FOX_W), BF16), jax.ShapeDtypeStruct((s, S5_W), BF16),
                   jax.ShapeDtypeStruct((s, S5_W), F32), jax.ShapeDtypeStruct((HEADS, s), F32),
                   jax.ShapeDtypeStruct((D_MODEL, D_MODEL), F32), jax.ShapeDtypeStruct((FOX_W, D_MODEL), F32),
                   jax.ShapeDtypeStruct((S5_W, D_MODEL), F32), jax.ShapeDtypeStruct((S5_W, S5_W), F32),
                   jax.ShapeDtypeStruct((8, D_MODEL), F32)),
        scratch_shapes=[pltpu.VMEM((D_MODEL, D_MODEL), F32), pltpu.VMEM((FOX_W, D_MODEL), F32),
                        pltpu.VMEM((S5_W, D_MODEL), F32), pltpu.VMEM((S5_W, S5_W), F32)],
        compiler_params=_cparams(("arbitrary",)),
    )(o, rest, rest, rest, rest, rest, ys0, x, tgt, *w, vec, hsel)


def _dh(dq, dk, dv, dga, dgb, dza, du, dzb, df, wqkv_t, wrest_t, x, dx2, gs, scatter_srcs):
    s = x.shape[0]
    tm = min(TM_PROJ, s)
    nsteps = s // tm
    na = len(scatter_srcs)

    def body(dq_ref, dk_ref, dv_ref, dga_ref, dgb_ref, dza_ref, du_ref, dzb_ref, df_ref, wq_ref, wr_ref,
             x_ref, dx2_ref, gs_ref, *rest_refs):
        src_refs = rest_refs[:na]
        gx_ref, vout_ref = rest_refs[na:na + 2]
        out_refs = rest_refs[na + 2:2 * na + 2]
        send_sems, recv_sems = rest_refs[2 * na + 2:]
        step = pl.program_id(0)
        cx, cy, cc = lax.axis_index("x"), lax.axis_index("y"), lax.axis_index("c")
        peers = [(1 - cx, cy), (cx, 1 - cy), (1 - cx, 1 - cy)]

        def copy(a, k, px, py, slot):
            return pltpu.make_async_remote_copy(
                src_ref=src_refs[a].at[2 * px + py], dst_ref=out_refs[a].at[slot],
                send_sem=send_sems.at[a * 3 + k], recv_sem=recv_sems.at[a * 3 + k],
                device_id=(px, py, cc), device_id_type=MESH)

        @pl.when(step == 0)
        def _():
            vout_ref[...] = jnp.zeros_like(vout_ref)
            for a in range(na):
                for k, (px, py) in enumerate(peers):
                    copy(a, k, px, py, 2 * cx + cy).start()

        dh = _dot(dq_ref[...], wq_ref[0:512, :])
        dh += _dot(dk_ref[...], wq_ref[512:1024, :])
        dh += _dot(dv_ref[...], wq_ref[1024:1536, :])
        dh += _dot(dga_ref[...], wr_ref[R_GA:R_GB, :])
        dh += _dot(dgb_ref[...], wr_ref[R_GB:R_ZA, :])
        dh += _dot(dza_ref[...], wr_ref[R_ZA:R_U, :])
        dh += _dot(du_ref[...], wr_ref[R_U:R_ZB, :])
        dh += _dot(dzb_ref[...], wr_ref[R_ZB:R_F, :])
        dh += _dot(df_ref[...], wr_ref[R_F:REST_W, :])
        xv = x_ref[...]
        r = lax.rsqrt(jnp.mean(xv * xv, axis=-1, keepdims=True) + EPS)
        xn = xv * r
        dxn = dh * gs_ref[...]
        gx_ref[...] = dx2_ref[...] + r * (dxn - xn * jnp.mean(dxn * xn, axis=-1, keepdims=True))
        vout_ref[0:1, :] += jnp.sum(dh * xn, axis=0, keepdims=True)
        vout_ref[1:2, :] += jnp.sum(dh, axis=0, keepdims=True)

        @pl.when(step == nsteps - 1)
        def _():
            for a in range(na):
                for k, (px, py) in enumerate(peers):
                    copy(a, k, px, py, 2 * px + py).wait_recv()
            for a in range(na):
                for k, (px, py) in enumerate(peers):
                    copy(a, k, px, py, 2 * cx + cy).wait_send()

    def rows(width):
        return pl.BlockSpec((tm, width), lambda i: (i, 0))

    anyspace = pl.BlockSpec(memory_space=pl.ANY)
    return pl.pallas_call(
        body, name="dh", grid=(nsteps,),
        in_specs=[rows(512), rows(512), rows(512), rows(1024), rows(1024), rows(512), rows(512), rows(512), rows(128),
                  _const((1536, D_MODEL)), _const((REST_W, D_MODEL)), rows(D_MODEL), rows(D_MODEL), _const((1, D_MODEL))]
                 + [anyspace] * na,
        out_specs=(rows(D_MODEL), pl.BlockSpec((8, D_MODEL), lambda i: (0, 0))) + (anyspace,) * na,
        out_shape=(jax.ShapeDtypeStruct((s, D_MODEL), F32), jax.ShapeDtypeStruct((8, D_MODEL), F32))
                  + tuple(jax.ShapeDtypeStruct(a.shape, a.dtype) for a in scatter_srcs),
        scratch_shapes=[pltpu.SemaphoreType.DMA((3 * na,)), pltpu.SemaphoreType.DMA((3 * na,))],
        compiler_params=_cparams(("arbitrary",)),
    )(dq, dk, dv, dga, dgb, dza, du, dzb, df, wqkv_t, wrest_t, x, dx2, gs, *scatter_srcs)


def _row_block(rows, mult=8, cap=512):
    if rows <= mult:
        return rows
    padded = -(-rows // mult) * mult
    for cand in range(min(cap, padded) // mult * mult, 0, -mult):
        if padded % cand == 0:
            return cand
    return padded


def _sum4(parts, name):
    rows, cols = parts.shape[1:]
    br = _row_block(rows, 16, 1024)

    def body(p_ref, o_ref):
        acc = p_ref[0].astype(F32)
        for k in range(1, 4):
            acc = acc + p_ref[k].astype(F32)
        o_ref[...] = acc

    return pl.pallas_call(
        body, name=name, grid=(pl.cdiv(rows, br),),
        in_specs=[pl.BlockSpec((4, br, cols), lambda i: (0, i, 0))],
        out_specs=pl.BlockSpec((br, cols), lambda i: (i, 0)),
        out_shape=jax.ShapeDtypeStruct((rows, cols), F32), compiler_params=_cparams(("parallel",)))(parts)


def _pair_add(a, b, name):
    shape = a.shape
    a, b = a.reshape(-1, shape[-1]), b.reshape(-1, shape[-1])
    rows, cols = a.shape
    br = _row_block(rows, 16, 1024)

    def body(a_ref, b_ref, o_ref):
        o_ref[...] = (a_ref[...].astype(F32) + b_ref[...].astype(F32)).astype(BF16)

    spec = pl.BlockSpec((br, cols), lambda i: (i, 0))
    return pl.pallas_call(
        body, name=name, grid=(pl.cdiv(rows, br),), in_specs=[spec, spec], out_specs=spec,
        out_shape=jax.ShapeDtypeStruct((rows, cols), BF16), compiler_params=_cparams(("parallel",)))(a, b).reshape(shape)


def _adamw(w, g, m, v, name):
    rows, cols = w.shape
    br = _row_block(rows)

    def body(w_ref, g_ref, m_ref, v_ref, d_ref, nm_ref, nv_ref):
        gv = g_ref[...]
        nm = ADAM_B1 * m_ref[...] + (1.0 - ADAM_B1) * gv
        nv = ADAM_B2 * v_ref[...] + (1.0 - ADAM_B2) * (gv * gv)
        m_hat = nm / (1.0 - ADAM_B1 ** ADAM_STEP)
        v_hat = nv / (1.0 - ADAM_B2 ** ADAM_STEP)
        d_ref[...] = -ADAM_LR * (m_hat / (jnp.sqrt(v_hat) + ADAM_EPS) + ADAM_WD * w_ref[...])
        nm_ref[...] = nm
        nv_ref[...] = nv

    spec = pl.BlockSpec((br, cols), lambda i: (i, 0))
    shape = jax.ShapeDtypeStruct((rows, cols), F32)
    return pl.pallas_call(
        body, name=name, grid=(pl.cdiv(rows, br),), in_specs=[spec] * 4, out_specs=(spec,) * 3,
        out_shape=(shape,) * 3, compiler_params=_cparams(("parallel",)))(w, g, m, v)


def _pack(parts, row_multiple=8):
    flat = []
    for p in parts:
        v = p.reshape(-1).astype(F32)
        pad = (-v.shape[0]) % LANES
        if pad:
            v = jnp.concatenate([v, jnp.zeros((pad,), F32)])
        flat.append(v)
    v = jnp.concatenate(flat)
    rows = v.shape[0] // LANES
    pad_rows = (-rows) % row_multiple
    if pad_rows:
        v = jnp.concatenate([v, jnp.zeros((pad_rows * LANES,), F32)])
    return v.reshape(-1, LANES)


def _unpack(packed, shapes):
    lead = packed.shape[:-2]
    flat = packed.reshape(lead + (-1,))
    out, off = [], 0
    for sh in shapes:
        size = math.prod(sh)
        out.append(flat[..., off:off + size].reshape(lead + tuple(sh)))
        off += size + (-size) % LANES
    return out


def kernel(x, c, w_ada, b_ada, g_norm, w_in, b_f, a_re, a_im, log_dt, b_re, b_im, c_re, c_im, d_skip, w_glu, b_glu, w_up_a, w_up_b, w_out, g_final, loss_target, m_w_ada, m_b_ada, m_g_norm, m_w_in, m_b_f, m_a_re, m_a_im, m_log_dt, m_b_re, m_b_im, m_c_re, m_c_im, m_d_skip, m_w_glu, m_b_glu, m_w_up_a, m_w_up_b, m_w_out, m_g_final, v_w_ada, v_b_ada, v_g_norm, v_w_in, v_b_f, v_a_re, v_a_im, v_log_dt, v_b_re, v_b_im, v_c_re, v_c_im, v_d_skip, v_w_glu, v_b_glu, v_w_up_a, v_w_up_b, v_w_out, v_g_final):
    xi, yi, ci = lax.axis_index("x"), lax.axis_index("y"), lax.axis_index("c")
    chip = 2 * xi + yi
    me = 4 * xi + 2 * yi + ci
    s = x.shape[1]
    x2d = x[0]
    tgt = loss_target[0]
    n_att = s // min(T_ATT, s)
    t_att = min(T_ATT, s)

    c_all, _ = _allgather8(c.reshape(8, LANES), "gather_c")
    c_all = c_all.reshape(8, D_MODEL)
    ncol = w_ada.shape[2]
    b_cols = lax.dynamic_slice_in_dim(b_ada, chip * ncol, ncol, axis=1)
    mod_cols = _mod_cols(c_all, w_ada[0], b_cols)
    mod_all, _ = _allgather8(mod_cols.reshape(-1, LANES), "gather_mod")
    mod_all = mod_all.reshape(4, 2, 8, ncol)[:, 0]
    mod_me = lax.dynamic_index_in_dim(mod_all, me, axis=1, keepdims=False).reshape(1, 3 * D_MODEL)
    shift, scale, gate = mod_me[:, :D_MODEL], mod_me[:, D_MODEL:2 * D_MODEL], mod_me[:, 2 * D_MODEL:]
    gs = g_norm * (1.0 + scale)

    nshard = w_in.shape[2]
    w_in_t, m_in_t, v_in_t = (jnp.swapaxes(a[0], 0, 1) for a in (w_in, m_w_in, v_w_in))
    wt_pack = jnp.pad(w_in_t.astype(BF16), ((0, SHARD_ROWS - nshard), (0, 0)))
    misc_shapes = [w_glu.shape[1:], w_up_a.shape[1:], w_up_b.shape[1:], w_out.shape[1:]]
    misc_pack = jnp.concatenate([w.reshape(-1) for w in (w_glu, w_up_a, w_up_b, w_out)]).astype(BF16).reshape(-1, LANES)
    def halves(a):
        return a.reshape((2, a.shape[0] // 2) + a.shape[1:])

    wt_all, misc_all = _gather_shards([halves(wt_pack), halves(misc_pack)], "gather_weights")
    wt_all = lax.dynamic_update_index_in_dim(wt_all, halves(wt_pack), chip, 0).reshape((4,) + wt_pack.shape)
    misc_all = lax.dynamic_update_index_in_dim(misc_all, halves(misc_pack), chip, 0).reshape((4,) + misc_pack.shape)
    p_glu, p_ua, p_ub, p_out = _unpack(misc_all, misc_shapes)

    def w_rows(lo, hi):
        out = []
        for j in range(4):
            a, b = max(lo, j * nshard), min(hi, (j + 1) * nshard)
            if a < b:
                out.append(wt_all[j, a - j * nshard:b - j * nshard])
        return out

    wqkv_t = jnp.concatenate(w_rows(O_Q, O_F), axis=0)
    wrest_t = jnp.concatenate(w_rows(O_GA, O_GB) + w_rows(O_GB, O_END) + w_rows(O_ZA, O_U) + w_rows(O_U, O_ZB)
                              + w_rows(O_ZB, O_GA) + w_rows(O_F, O_ZA)
                              + [jnp.zeros((REST_W - R_F - HEADS, D_MODEL), BF16)], axis=0)
    wmid = (p_glu.reshape(S5_W, S5_W), jnp.concatenate([p_ua[j] for j in range(4)], axis=1),
            jnp.concatenate([p_ub[j] for j in range(4)], axis=1), p_out.reshape(D_MODEL, D_MODEL))

    h, qkv, rest = _prenorm_proj(x2d, gs, shift, wqkv_t, wrest_t)
    bf128 = jnp.pad(b_f, ((0, 0), (0, LANES - HEADS)))
    fpc, f_t, kbias = _fcum(rest, bf128, _bias_selectors())
    frow5 = f_t.reshape(4, 2, n_att, 1, t_att)
    o, lse_pc = _attn_fwd(qkv, kbias, fpc)

    abar_r, abar_i, bb_r, bb_i = _ssm_block_params(a_re[0], a_im[0], log_dt[0], b_re[0], b_im[0])
    bb_rt, bb_it = jnp.swapaxes(bb_r, 1, 2).astype(BF16), jnp.swapaxes(bb_i, 1, 2).astype(BF16)
    cr_b, ci_b = c_re[0].astype(BF16), (-c_im[0]).astype(BF16)
    bd_c, cd_c = _compact_diag(bb_rt, bb_it), _compact_diag(cr_b, ci_b)
    seg = min(TB_SSM, s) // 8
    ys0, xs = _ssm_fwd(rest, bd_c, cd_c, *_scan_consts(a_re[0], a_im[0], log_dt[0], seg, False))

    vec = jnp.concatenate([gate, g_final.reshape(1, D_MODEL), jnp.concatenate([d_skip, b_glu], axis=1),
                           jnp.zeros((5, D_MODEL), F32)], axis=0)
    hsel = jnp.repeat(jnp.eye(HEADS, dtype=F32), HEAD_DIM, axis=1)
    (dx2, dga, dgb, do, dza, dzb, dys, dlt_t, g_out, g_ua, g_ub, g_glu, vmid) = _mid(
        o, rest, ys0, x2d, tgt, wmid, vec, hsel)

    lse_t = jnp.transpose(lse_pc.reshape(s, 4 // ATT_PAIRS, LANES)[:, :, :2 * ATT_PAIRS], (1, 2, 0))
    lse5 = lse_t.reshape(4, 2, n_att, 1, t_att)
    dlt5 = dlt_t.reshape(4, 2, n_att, 1, t_att)
    dq, dk, dv, dfk, dfq = _attn_bwd(qkv, do, lse5, dlt5, frow5, fpc)
    du, g_bd, g_cdt, da8 = _ssm_bwd(dys, xs, rest, bd_c, cd_c, *_scan_consts(a_re[0], a_im[0], log_dt[0], seg, True),
                                    d_skip)
    df, dbf8 = _dfcum(dfk, dfq, rest, bf128)

    gq, gk, gv, gga, ggb, gza, gu, gzb, gf = _grad_w_rows(h, [dq, dk, dv, dga, dgb, dza, du, dzb, df])
    g_in_t = jnp.concatenate([gq, gk, gv, gf[:HEADS], gza, gu, gzb, gga, ggb], axis=0)

    def shard_cols(g, j):
        n = g.shape[1] // 4
        return g[:, j * n:(j + 1) * n]

    def shard_rows(g, j):
        n = g.shape[0] // 4
        return g[j * n:(j + 1) * n]

    def halves4(a):
        return a.reshape((4, 2, a.shape[1] // 2) + a.shape[2:])

    gt_pack = halves4(jnp.stack([
        jnp.pad(g_in_t[j * nshard:(j + 1) * nshard].astype(BF16), ((0, SHARD_ROWS - nshard), (0, 0)))
        for j in range(4)]))
    gm_pack = halves4(jnp.stack([
        jnp.concatenate([shard_rows(g_glu, j).reshape(-1), shard_cols(g_ua, j).reshape(-1),
                         shard_cols(g_ub, j).reshape(-1), shard_rows(g_out, j).reshape(-1)]).astype(BF16)
        .reshape(-1, LANES) for j in range(4)]))
    recv_in, recv_misc = _swap_sibling([gt_pack, gm_pack], "pair_swap_weight_grads", other_half=True)
    own_in = lax.dynamic_index_in_dim(gt_pack, ci, axis=1, keepdims=False)
    own_misc = lax.dynamic_index_in_dim(gm_pack, ci, axis=1, keepdims=False)
    pair_in = _pair_add(own_in, recv_in, "pair_add_w_in")
    pair_misc = _pair_add(own_misc, recv_misc, "pair_add_misc")

    grad_x, vdh, parts_in, parts_misc = _dh(dq, dk, dv, dga, dgb, dza, du, dzb, df, wqkv_t, wrest_t, x2d, dx2, gs,
                                            [pair_in, pair_misc])
    parts_in = lax.dynamic_update_slice_in_dim(parts_in, lax.dynamic_slice_in_dim(pair_in, chip, 1, 0), chip, 0)
    parts_misc = lax.dynamic_update_slice_in_dim(parts_misc, lax.dynamic_slice_in_dim(pair_misc, chip, 1, 0), chip, 0)
    half_in, half_misc = _sum4(parts_in, "sum4_w_in"), _sum4(parts_misc, "sum4_misc")
    sib_in, sib_misc = _swap_sibling([half_in, half_misc], "swap_weight_grads")

    def both_halves(mine, theirs):
        return jnp.concatenate([jnp.where(ci == 0, mine, theirs), jnp.where(ci == 0, theirs, mine)], axis=0)

    tot_in, tot_misc = both_halves(half_in, sib_in), both_halves(half_misc, sib_misc)
    g_glu_s, g_ua_s, g_ub_s, g_out_s = _unpack(tot_misc, misc_shapes)

    dgs, dshift = vdh[0:1], vdh[1:2]
    dmod = jnp.concatenate([dshift, dgs * g_norm, vmid[1:2]], axis=1)
    da = jnp.sum(da8, axis=0)
    g_bd = g_bd.reshape(GROUPS, GCH, 2 * STATE)
    g_cdt = g_cdt.reshape(GROUPS, GCH, 2 * STATE)
    g_bbr = jnp.swapaxes(g_bd[:, :, :STATE], 1, 2)
    g_bbi = jnp.swapaxes(g_bd[:, :, STATE:], 1, 2)
    g_cre = g_cdt[:, :, :STATE]
    g_cim = -g_cdt[:, :, STATE:]
    small_shapes = [(1,), (3 * D_MODEL,), (D_MODEL,), (HEADS,), (GROUPS, STATE), (GROUPS, STATE),
                    (GROUPS, STATE, GCH), (GROUPS, STATE, GCH), (GROUPS, GCH, STATE), (GROUPS, GCH, STATE),
                    (S5_W,), (S5_W,), (D_MODEL,)]
    small = _pack([vmid[3, 0:1], dmod, dgs * (1.0 + scale), dbf8[0, :HEADS], da[:NSTATE], da[NSTATE:],
                   g_bbr, g_bbi, g_cre, g_cim, vmid[2, :S5_W], vmid[2, S5_W:], vmid[0]])
    small_all, small_sum = _allgather8(small, "gather_small_grads")
    (loss_s, g_b_ada, g_g_norm, g_b_f, g_abr, g_abi, g_bbr_s, g_bbi_s, g_c_re, g_c_im, g_d_skip, g_b_glu,
     g_g_final) = _unpack(small_sum, small_shapes)
    loss = loss_s[0]
    dmod_all = _unpack(small_all, small_shapes)[1]
    dmod_cols = lax.dynamic_slice_in_dim(dmod_all, chip * ncol, ncol, axis=1)
    g_w_ada = _grad_w_ada(c_all, dmod_cols)
    _, ssm_vjp = jax.vjp(_ssm_block_params, a_re[0], a_im[0], log_dt[0], b_re[0], b_im[0])
    g_a_re, g_a_im, g_log_dt, g_b_re, g_b_im = ssm_vjp((g_abr, g_abi, g_bbr_s, g_bbi_s))

    def adam(name, w, g, m, v):
        shape = w.shape
        total = math.prod(shape)
        if len(shape) > 1 and shape[-1] >= LANES:
            cols = shape[-1]
        elif total % LANES == 0:
            cols = LANES
        else:
            cols = total
        two = lambda a: a.reshape(-1, cols)
        d, nm, nv = _adamw(two(w), two(g), two(m), two(v), "adamw_" + name)
        return g.reshape(shape), d.reshape(shape), nm.reshape(shape), nv.reshape(shape)

    back = lambda a: jnp.swapaxes(a, 0, 1)[None]
    d_in_t, nm_in_t, nv_in_t = _adamw(w_in_t, tot_in, m_in_t, v_in_t, "adamw_w_in")
    res_w_in = (back(tot_in[:nshard]), back(d_in_t), back(nm_in_t), back(nv_in_t))

    res = [
        adam("w_ada", w_ada, g_w_ada, m_w_ada, v_w_ada),
        adam("b_ada", b_ada, g_b_ada, m_b_ada, v_b_ada),
        adam("g_norm", g_norm, g_g_norm, m_g_norm, v_g_norm),
        res_w_in,
        adam("b_f", b_f, g_b_f, m_b_f, v_b_f),
        adam("a_re", a_re, g_a_re, m_a_re, v_a_re),
        adam("a_im", a_im, g_a_im, m_a_im, v_a_im),
        adam("log_dt", log_dt, g_log_dt, m_log_dt, v_log_dt),
        adam("b_re", b_re, g_b_re, m_b_re, v_b_re),
        adam("b_im", b_im, g_b_im, m_b_im, v_b_im),
        adam("c_re", c_re, g_c_re, m_c_re, v_c_re),
        adam("c_im", c_im, g_c_im, m_c_im, v_c_im),
        adam("d_skip", d_skip, g_d_skip, m_d_skip, v_d_skip),
        adam("w_glu", w_glu, g_glu_s, m_w_glu, v_w_glu),
        adam("b_glu", b_glu, g_b_glu, m_b_glu, v_b_glu),
        adam("w_up_a", w_up_a, g_ua_s, m_w_up_a, v_w_up_a),
        adam("w_up_b", w_up_b, g_ub_s, m_w_up_b, v_w_up_b),
        adam("w_out", w_out, g_out_s, m_w_out, v_w_out),
        adam("g_final", g_final, g_g_final, m_g_final, v_g_final),
    ]
    grads = [r[0] for r in res]
    deltas = [r[1] for r in res]
    new_m = [r[2] for r in res]
    new_v = [r[3] for r in res]
    return (loss, grad_x[None], *grads, *deltas, *new_m, *new_v)
```

```python
import math

import jax
import jax.numpy as jnp
from jax import lax
from jax.experimental import pallas as pl
from jax.experimental.pallas import tpu as pltpu

F32 = jnp.float32
BF16 = jnp.bfloat16
HI = lax.Precision.HIGHEST
MESH = pl.DeviceIdType.MESH

D_MODEL = 1024
HEADS = 8
HEAD_DIM = 64
FOX_W = 512
S5_W = 512
GROUPS = 32
STATE = 64
GCH = 16
NSTATE = GROUPS * STATE
EPS = 1e-6
NEG = -1e30

ADAM_LR = 0.001
ADAM_B1 = 0.9
ADAM_B2 = 0.999
ADAM_EPS = 1e-08
ADAM_WD = 0.01
ADAM_STEP = 10

VMEM_LIMIT = 56 * 1024 * 1024
LANES = 128

TM = 256
TM_PROJ = 512
T_ATT = 512
ATT_CHUNK = 32
ATT_PAIRS = 4
TB_SSM = 512
TK_ACC = 512
TB_CUM = 256
SHARD_ROWS = 1312

O_Q, O_K, O_V, O_F, O_ZA, O_U, O_ZB, O_GA, O_GB, O_END = 0, 512, 1024, 1536, 1544, 2056, 2568, 3080, 4104, 5128
REST_W = 3712
R_GA, R_GB, R_ZA, R_U, R_ZB, R_F = 0, 1024, 2048, 2560, 3072, 3584


def _cparams(sem=None):
    kw = dict(vmem_limit_bytes=VMEM_LIMIT)
    if sem is not None:
        kw["dimension_semantics"] = sem
    return pltpu.CompilerParams(**kw)


def _const(shape):
    nd = len(shape)
    return pl.BlockSpec(shape, lambda *_: (0,) * nd, pipeline_mode=pl.Buffered(1))


def _dot(a, b, precision=None):
    return jnp.dot(a, b, preferred_element_type=F32, precision=precision)


def _dot_nt(a, b):
    return lax.dot_general(a, b, (((1,), (1,)), ((), ())), preferred_element_type=F32)


def _dot_tn(a, b, precision=None):
    return lax.dot_general(a, b, (((0,), (0,)), ((), ())), preferred_element_type=F32, precision=precision)


def _sigmoid(z):
    return 1.0 / (1.0 + jnp.exp(-z))


def _split3(x):
    hi = x.astype(BF16)
    r1 = x - hi.astype(F32)
    mid = r1.astype(BF16)
    lo = (r1 - mid.astype(F32)).astype(BF16)
    return hi, mid, lo


def _dot_sel(sel, x, terms=3):
    s16 = sel.astype(BF16)
    return sum(_dot(s16, part) for part in _split3(x)[:terms])


def _allgather8(xs, name):
    rows = xs.shape[0]

    def body(x_ref, out_ref, sum_ref, send_sems, recv_sems, local_sem):
        x, y, c = lax.axis_index("x"), lax.axis_index("y"), lax.axis_index("c")
        me, sibling = (x, y, c), (x, y, 1 - c)
        chips = [(1 - x, y), (x, 1 - y), (1 - x, 1 - y)]

        def slot(px, py, pc):
            return out_ref.at[4 * px + 2 * py + pc]

        def copy(k, block, to, src=None):
            return pltpu.make_async_remote_copy(
                src_ref=slot(*block) if src is None else src, dst_ref=slot(*block),
                send_sem=send_sems.at[k], recv_sem=recv_sems.at[k], device_id=to, device_id_type=MESH)

        mine = pltpu.make_async_copy(x_ref, slot(*me), local_sem)
        mine.start()
        first = [copy(0, me, sibling, src=x_ref)]
        first += [copy(1 + j, me, (*chip, c), src=x_ref) for j, chip in enumerate(chips)]
        for cp in first:
            cp.start()
        passed = [copy(4 + j, (*chip, c), sibling) for j, chip in enumerate(chips)]
        for j, chip in enumerate(chips):
            copy(1 + j, (*chip, c), me).wait_recv()
            passed[j].start()
        copy(0, sibling, me).wait_recv()
        for j, chip in enumerate(chips):
            copy(4 + j, (*chip, 1 - c), me).wait_recv()
        for cp in first + passed:
            cp.wait_send()
        mine.wait()
        acc = out_ref[0]
        for d in range(1, 8):
            acc = acc + out_ref[d]
        sum_ref[...] = acc

    return pl.pallas_call(
        body, name=name,
        out_shape=(jax.ShapeDtypeStruct((8, rows, LANES), F32), jax.ShapeDtypeStruct((rows, LANES), F32)),
        in_specs=[pl.BlockSpec(memory_space=pltpu.VMEM)],
        out_specs=(pl.BlockSpec(memory_space=pltpu.VMEM), pl.BlockSpec(memory_space=pltpu.VMEM)),
        scratch_shapes=[pltpu.SemaphoreType.DMA((7,)), pltpu.SemaphoreType.DMA((7,)), pltpu.SemaphoreType.DMA],
        compiler_params=_cparams(),
    )(xs)


def _gather_shards(srcs, name):
    na = len(srcs)

    def body(*refs):
        src_refs, out_refs = refs[:na], refs[na:2 * na]
        send_sems, recv_sems = refs[2 * na:]
        x, y, c = lax.axis_index("x"), lax.axis_index("y"), lax.axis_index("c")
        sibling = (x, y, 1 - c)
        peers = [(1 - x, y), (x, 1 - y), (1 - x, 1 - y)]

        def copy(a, k, src, slot, which, to):
            return pltpu.make_async_remote_copy(
                src_ref=src, dst_ref=out_refs[a].at[slot, which],
                send_sem=send_sems.at[a * 6 + k], recv_sem=recv_sems.at[a * 6 + k],
                device_id=to, device_id_type=MESH)

        mine = 2 * x + y
        first = [copy(a, k, src_refs[a].at[c], mine, c, (px, py, c))
                 for a in range(na) for k, (px, py) in enumerate(peers)]
        for cp in first:
            cp.start()
        passed = []
        for a in range(na):
            for k, (px, py) in enumerate(peers):
                slot = 2 * px + py
                landed = out_refs[a].at[slot, c]
                copy(a, k, landed, slot, c, (px, py, c)).wait_recv()
                fwd = copy(a, 3 + k, landed, slot, c, sibling)
                fwd.start()
                passed.append(fwd)
        for a in range(na):
            for k, (px, py) in enumerate(peers):
                slot = 2 * px + py
                copy(a, 3 + k, out_refs[a].at[slot, 1 - c], slot, 1 - c, sibling).wait_recv()
        for cp in first + passed:
            cp.wait_send()

    anyspace = pl.BlockSpec(memory_space=pl.ANY)
    return pl.pallas_call(
        body, name=name,
        out_shape=tuple(jax.ShapeDtypeStruct((4,) + tuple(a.shape), a.dtype) for a in srcs),
        in_specs=[anyspace] * na, out_specs=(anyspace,) * na,
        scratch_shapes=[pltpu.SemaphoreType.DMA((6 * na,)), pltpu.SemaphoreType.DMA((6 * na,))],
        compiler_params=_cparams(),
    )(*srcs)


def _swap_sibling(srcs, name, other_half=False):
    na = len(srcs)

    def body(*refs):
        src_refs, out_refs = refs[:na], refs[na:2 * na]
        send_sems, recv_sems = refs[2 * na:]
        x, y, c = lax.axis_index("x"), lax.axis_index("y"), lax.axis_index("c")
        copies = [pltpu.make_async_remote_copy(
            src_ref=src_refs[a].at[:, 1 - c] if other_half else src_refs[a],
            dst_ref=out_refs[a], send_sem=send_sems.at[a], recv_sem=recv_sems.at[a],
            device_id=(x, y, 1 - c), device_id_type=MESH) for a in range(na)]
        for cp in copies:
            cp.start()
        for cp in copies:
            cp.wait()

    def out_of(a):
        shape = (a.shape[0],) + tuple(a.shape[2:]) if other_half else a.shape
        return jax.ShapeDtypeStruct(shape, a.dtype)

    anyspace = pl.BlockSpec(memory_space=pl.ANY)
    return pl.pallas_call(
        body, name=name, out_shape=tuple(out_of(a) for a in srcs),
        in_specs=[anyspace] * na, out_specs=(anyspace,) * na,
        scratch_shapes=[pltpu.SemaphoreType.DMA((na,)), pltpu.SemaphoreType.DMA((na,))],
        compiler_params=_cparams(),
    )(*srcs)


def _mod_cols(c_all, w, b):
    n = w.shape[1]

    def body(c_ref, w_ref, b_ref, o_ref):
        o_ref[...] = _dot(c_ref[...], w_ref[...], HI) + b_ref[...]

    return pl.pallas_call(
        body, name="mod_cols", out_shape=jax.ShapeDtypeStruct((8, n), F32),
        compiler_params=_cparams())(c_all, w, b)


def _grad_w_ada(c_all, dmod_cols):
    n = dmod_cols.shape[1]

    def body(c_ref, d_ref, o_ref):
        o_ref[...] = _dot_tn(c_ref[...], d_ref[...], HI)

    return pl.pallas_call(
        body, name="grad_w_ada", out_shape=jax.ShapeDtypeStruct((D_MODEL, n), F32),
        compiler_params=_cparams())(c_all, dmod_cols)


def _prenorm_proj(x, gs, shift, wqkv_t, wrest_t):
    s = x.shape[0]
    tm = min(TM_PROJ, s)
    nq, nr = wqkv_t.shape[0], wrest_t.shape[0]

    def body(x_ref, gs_ref, sh_ref, wq_ref, wr_ref, h_ref, qkv_ref, rest_ref):
        xv = x_ref[...]
        r = lax.rsqrt(jnp.mean(xv * xv, axis=-1, keepdims=True) + EPS)
        h = (xv * r * gs_ref[...] + sh_ref[...]).astype(BF16)
        h_ref[...] = h
        qkv_ref[...] = _dot_nt(h, wq_ref[...]).astype(BF16)
        rest_ref[...] = _dot_nt(h, wr_ref[...])

    def rows(width):
        return pl.BlockSpec((tm, width), lambda i: (i, 0))

    return pl.pallas_call(
        body, name="prenorm_proj", grid=(s // tm,),
        in_specs=[rows(D_MODEL), _const((1, D_MODEL)), _const((1, D_MODEL)), _const((nq, D_MODEL)),
                  _const((nr, D_MODEL))],
        out_specs=(rows(D_MODEL), rows(nq), rows(nr)),
        out_shape=(jax.ShapeDtypeStruct((s, D_MODEL), BF16), jax.ShapeDtypeStruct((s, nq), BF16),
                   jax.ShapeDtypeStruct((s, nr), F32)),
        compiler_params=_cparams(("parallel",)))(x, gs, shift, wqkv_t, wrest_t)


def _grad_w_rows(h, ds):
    s = h.shape[0]
    tk = min(TK_ACC, s)
    nd = len(ds)
    widths = [d.shape[1] for d in ds]

    def body(*refs):
        h_ref, d_refs = refs[0], refs[1:1 + nd]
        out_refs, accs = refs[1 + nd:1 + 2 * nd], refs[1 + 2 * nd:]
        step = pl.program_id(0)

        @pl.when(step == 0)
        def _():
            for acc in accs:
                acc[...] = jnp.zeros_like(acc)

        hv = h_ref[...]
        for d_ref, acc in zip(d_refs, accs):
            acc[...] += _dot_tn(d_ref[...], hv)

        @pl.when(step == s // tk - 1)
        def _():
            for acc, out in zip(accs, out_refs):
                pltpu.sync_copy(acc, out)

    anyspace = pl.BlockSpec(memory_space=pl.ANY)
    return pl.pallas_call(
        body, name="grad_w_in", grid=(s // tk,),
        in_specs=[pl.BlockSpec((tk, D_MODEL), lambda k: (k, 0))]
                 + [pl.BlockSpec((tk, w), lambda k: (k, 0)) for w in widths],
        out_specs=(anyspace,) * nd,
        out_shape=tuple(jax.ShapeDtypeStruct((w, D_MODEL), F32) for w in widths),
        scratch_shapes=[pltpu.VMEM((w, D_MODEL), F32) for w in widths],
        compiler_params=_cparams(("arbitrary",)))(h, *ds)


BIAS_ONES = 32


def _bias_selectors():
    rows = jnp.arange(LANES)[None, :, None]
    cols = jnp.arange(LANES)[None, None, :]
    term = jnp.arange(3)[:, None, None]
    return ((rows < HEADS) & (cols == 3 * rows + term)).astype(F32)


def _fcum(rest, bf128, selk):
    s = rest.shape[0]
    tb = min(TB_CUM, s)

    def body(fz_ref, bf_ref, selk_ref, fpc_ref, ft_ref, kb_ref, qb_ref, carry_ref):
        @pl.when(pl.program_id(0) == 0)
        def _():
            carry_ref[...] = jnp.zeros_like(carry_ref)

        z = fz_ref[...] + bf_ref[...]
        logf = jnp.minimum(z, 0.0) - jnp.log(1.0 + jnp.exp(-jnp.abs(z)))
        r = lax.broadcasted_iota(jnp.int32, (tb, tb), 0)
        c = lax.broadcasted_iota(jnp.int32, (tb, tb), 1)
        tri = (c <= r).astype(F32)
        f = _dot_sel(tri, logf) + carry_ref[0:1, :]
        carry_ref[0:1, :] = f[tb - 1:tb, :]
        for pp in range(4):
            fpc_ref[:, pp * LANES:(pp + 1) * LANES] = f if pp == 0 else pltpu.roll(f, LANES - 2 * pp, 1)
        ft_ref[...] = jnp.transpose(f)[0:HEADS, :]
        lane = lax.broadcasted_iota(jnp.int32, (tb, LANES), 1)
        ones = ((lane >= BIAS_ONES) & (lane < BIAS_ONES + 3 * HEADS)).astype(F32)
        terms = sum(_dot(part, selk_ref[j].astype(BF16)) for j, part in enumerate(_split3(-f)))
        kb_ref[...] = (terms + ones).astype(BF16)
        qb_ref[...] = ((lane < 3 * HEADS).astype(F32) - pltpu.roll(terms, BIAS_ONES, 1)).astype(BF16)

    return pl.pallas_call(
        body, name="forget_cumsum", grid=(s // tb,),
        in_specs=[pl.BlockSpec((tb, LANES), lambda i: (i, R_F // LANES)), _const((1, LANES)), _const((3, LANES, LANES))],
        out_specs=(pl.BlockSpec((tb, 4 * LANES), lambda i: (i, 0)), pl.BlockSpec((HEADS, tb), lambda i: (0, i)),
                   pl.BlockSpec((tb, LANES), lambda i: (i, 0)), pl.BlockSpec((tb, LANES), lambda i: (i, 0))),
        out_shape=(jax.ShapeDtypeStruct((s, 4 * LANES), F32), jax.ShapeDtypeStruct((HEADS, s), F32),
                   jax.ShapeDtypeStruct((s, LANES), BF16), jax.ShapeDtypeStruct((s, LANES), BF16)),
        scratch_shapes=[pltpu.VMEM((8, LANES), F32)],
        compiler_params=_cparams(("arbitrary",)))(rest, bf128, selk)


def _dfcum(dfk, dfq, rest, bf128):
    s = rest.shape[0]
    tb = min(TB_CUM, s)
    nb = s // tb

    def body(dk_ref, dq_ref, fz_ref, bf_ref, df_ref, dbf_ref, carry_ref):
        @pl.when(pl.program_id(0) == 0)
        def _():
            carry_ref[...] = jnp.zeros_like(carry_ref)
            dbf_ref[...] = jnp.zeros_like(dbf_ref)

        pair_lanes = lax.broadcasted_iota(jnp.int32, (tb, LANES), 1) < 2
        d = jnp.zeros((tb, LANES), F32)
        for pp in range(4):
            tile = dk_ref[:, pp * LANES:(pp + 1) * LANES] + dq_ref[:, pp * LANES:(pp + 1) * LANES]
            tile = jnp.where(pair_lanes, tile, 0.0)
            d = d + (tile if pp == 0 else pltpu.roll(tile, 2 * pp, 1))
        r = lax.broadcasted_iota(jnp.int32, (tb, tb), 0)
        c = lax.broadcasted_iota(jnp.int32, (tb, tb), 1)
        triu = (c >= r).astype(F32)
        dlogf = _dot_sel(triu, d) + carry_ref[0:1, :]
        carry_ref[0:1, :] = dlogf[0:1, :]
        z = fz_ref[...] + bf_ref[...]
        df = dlogf * (1.0 / (1.0 + jnp.exp(z)))
        df_ref[...] = df.astype(BF16)
        dbf_ref[0:1, :] += jnp.sum(df, axis=0, keepdims=True)

    return pl.pallas_call(
        body, name="forget_grad", grid=(nb,),
        in_specs=[pl.BlockSpec((tb, 4 * LANES), lambda i: (nb - 1 - i, 0)),
                  pl.BlockSpec((tb, 4 * LANES), lambda i: (nb - 1 - i, 0)),
                  pl.BlockSpec((tb, LANES), lambda i: (nb - 1 - i, R_F // LANES)),
                  _const((1, LANES))],
        out_specs=(pl.BlockSpec((tb, LANES), lambda i: (nb - 1 - i, 0)), pl.BlockSpec((8, LANES), lambda i: (0, 0))),
        out_shape=(jax.ShapeDtypeStruct((s, LANES), BF16), jax.ShapeDtypeStruct((8, LANES), F32)),
        scratch_shapes=[pltpu.VMEM((8, LANES), F32)],
        compiler_params=_cparams(("arbitrary",)))(dfk, dfq, rest, bf128)


def _scaled(q):
    return (q.astype(F32) * (HEAD_DIM ** -0.5)).astype(BF16)


def _attn_fwd(qkv, kbias, qbias):
    s = qkv.shape[0]
    t = min(T_ATT, s)
    n = s // t
    ch = min(ATT_CHUNK, t)
    wide = 2 * LANES
    pairs = ATT_PAIRS
    width = pairs * LANES
    groups = 4 // pairs

    def body(q_ref, k_ref, v_ref, kb_ref, qb_ref, o_ref, lse_ref, s_scr, p_scr, m_scr, a_scr, acc_scr):
        i = pl.program_id(1)
        g = pl.program_id(0)
        lane = lax.broadcasted_iota(jnp.int32, (t, LANES), 1)
        first = lane < HEAD_DIM
        ones_col = ((lane == 0).astype(BF16), (lane == 1).astype(BF16))
        m_scr[...] = jnp.full(m_scr.shape, NEG, F32)
        acc_scr[...] = jnp.zeros_like(acc_scr)
        qm = []
        qbb = qb_ref[...]
        for pp in range(pairs):
            q = _scaled(q_ref[:, pp * LANES:(pp + 1) * LANES])
            zq = jnp.zeros_like(q)
            for hh in range(2):
                lo3 = 3 * (2 * (g * pairs + pp) + hh)
                own = ((lane >= lo3) & (lane < lo3 + 3)) | ((lane >= BIAS_ONES + lo3) & (lane < BIAS_ONES + lo3 + 3))
                qh = jnp.where(first, q, zq) if hh == 0 else jnp.where(first, zq, q)
                qm.append(jnp.concatenate([qh, jnp.where(own, qbb, zq)], axis=1))

        def step(j, masked):
            r0 = pl.multiple_of(j * t, t)
            vaug = []
            kbias_blk = kb_ref[pl.ds(r0, t), :]
            for pp in range(pairs):
                kb = jnp.concatenate([k_ref[pl.ds(r0, t), pp * LANES:(pp + 1) * LANES], kbias_blk], axis=1)
                vb = v_ref[pl.ds(r0, t), pp * LANES:(pp + 1) * LANES]
                zv = jnp.zeros_like(vb)
                vaug += [jnp.concatenate([jnp.where(first, vb, zv), ones_col[0]], axis=1),
                         jnp.concatenate([jnp.where(first, zv, vb), ones_col[1]], axis=1)]
                for hh in range(2):
                    s_scr[2 * pp + hh] = _dot_nt(qm[2 * pp + hh], kb)
            pv = []
            for hd in range(2 * pairs):
                for c in range(t // ch):
                    rows = pl.ds(c * ch, ch)
                    hi = min(t, (c * ch // LANES + 1) * LANES) if masked else t
                    sc = s_scr[hd, rows, 0:hi]
                    if masked:
                        rq = c * ch + lax.broadcasted_iota(jnp.int32, (ch, hi), 0)
                        ck = lax.broadcasted_iota(jnp.int32, (ch, hi), 1)
                        sc = jnp.where(ck <= rq, sc, NEG)
                    m_old = m_scr[hd, rows, :]
                    m_new = jnp.maximum(m_old, jnp.max(sc, axis=1, keepdims=True))
                    p_scr[hd, rows, 0:hi] = jnp.exp(sc - m_new).astype(BF16)
                    if hi < t:
                        p_scr[hd, rows, hi:t] = jnp.zeros((ch, t - hi), BF16)
                    a_scr[hd, rows, :] = jnp.exp(m_old - m_new)
                    m_scr[hd, rows, :] = m_new
                pv.append(_dot(p_scr[hd], vaug[hd]))
            for pp in range(pairs):
                a0, a1 = a_scr[2 * pp], a_scr[2 * pp + 1]
                alpha = jnp.concatenate([jnp.where(first, a0, a1), jnp.where(lane == 0, a0, a1)], axis=1)
                acc_scr[pp] = acc_scr[pp] * alpha + pv[2 * pp] + pv[2 * pp + 1]
            return 0

        lax.fori_loop(0, i, lambda j, _: step(j, False), 0)
        step(i, True)
        lse = jnp.zeros((t, LANES), F32)
        for pp in range(pairs):
            l0 = acc_scr[pp, :, LANES:LANES + 1]
            l1 = acc_scr[pp, :, LANES + 1:LANES + 2]
            o_ref[:, pp * LANES:(pp + 1) * LANES] = acc_scr[pp, :, 0:LANES] * jnp.where(first, 1.0 / l0, 1.0 / l1)
            lse = jnp.where(lane == 2 * pp, m_scr[2 * pp] + jnp.log(l0), lse)
            lse = jnp.where(lane == 2 * pp + 1, m_scr[2 * pp + 1] + jnp.log(l1), lse)
        lse_ref[...] = lse

    blk = pl.BlockSpec((t, width), lambda g, i: (i, g))
    return pl.pallas_call(
        body, name="attn_fwd", grid=(groups, n),
        in_specs=[blk,
                  pl.BlockSpec((s, width), lambda g, i: (0, groups + g)),
                  pl.BlockSpec((s, width), lambda g, i: (0, 2 * groups + g)),
                  pl.BlockSpec((s, LANES), lambda g, i: (0, 0)),
                  pl.BlockSpec((t, LANES), lambda g, i: (i, 0))],
        out_specs=(blk, pl.BlockSpec((t, LANES), lambda g, i: (i, g))),
        out_shape=(jax.ShapeDtypeStruct((s, FOX_W), F32), jax.ShapeDtypeStruct((s, groups * LANES), F32)),
        scratch_shapes=[pltpu.VMEM((2 * pairs, t, t), F32), pltpu.VMEM((2 * pairs, t, t), BF16),
                        pltpu.VMEM((2 * pairs, t, 1), F32), pltpu.VMEM((2 * pairs, t, 1), F32),
                        pltpu.VMEM((pairs, t, wide), F32)],
        compiler_params=_cparams(("parallel", "arbitrary")))(qkv, qkv, qkv, kbias, qbias)


def _attn_bwd(qkv, do, lse5, dlt5, frow5, fpc):
    s = qkv.shape[0]
    t = min(T_ATT, s)
    n = s // t
    wide = 2 * LANES

    ch = min(ATT_CHUNK, t)

    def body(q_ref, do_ref, k_ref, v_ref, lse_ref, dl_ref, fr_ref, fc_ref,
             dq_ref, dk_ref, dv_ref, dfk_ref, dfq_ref, dq_acc, st_scr, dp_scr, pt_scr, ds_scr, dk_acc, dv_acc, fk_scr):
        j = pl.program_id(1)

        @pl.when(j == 0)
        def _():
            dq_acc[...] = jnp.zeros_like(dq_acc)

        dk_acc[...] = jnp.zeros_like(dk_acc)
        dv_acc[...] = jnp.zeros_like(dv_acc)
        lane = lax.broadcasted_iota(jnp.int32, (t, LANES), 1)
        first = lane < HEAD_DIM
        ones_col = ((lane == 0).astype(BF16), (lane == 1).astype(BF16))
        kb = k_ref[...]
        vb = v_ref[...]
        zk = jnp.zeros_like(kb)
        kaug = (jnp.concatenate([jnp.where(first, kb, zk), ones_col[0]], axis=1),
                jnp.concatenate([jnp.where(first, zk, kb), ones_col[1]], axis=1))
        fk_scr[0] = fc_ref[:, 0:1]
        fk_scr[1] = fc_ref[:, 1:2]

        def step(blocks, masked):
            chains = []
            for bi, i in enumerate(blocks):
                r0 = pl.multiple_of(i * t, t)
                qb = _scaled(q_ref[pl.ds(r0, t), :])
                dob = do_ref[pl.ds(r0, t), :]
                zq = jnp.zeros_like(qb)
                qm = (jnp.where(first, qb, zq), jnp.where(first, zq, qb))
                dom = (jnp.where(first, dob, zq), jnp.where(first, zq, dob))
                for hh in range(2):
                    st_scr[2 * bi + hh] = _dot_nt(kb, qm[hh])
                    dp_scr[2 * bi + hh] = _dot_nt(vb, dom[hh])
                    chains.append((i, hh, qm[hh], dom[hh]))
            dq_add = [jnp.zeros((t, wide), F32) for _ in blocks]
            for cn, (i, hh, qmh, domh) in enumerate(chains):
                bias = fr_ref[0, hh, i] - lse_ref[0, hh, i]
                dl = dl_ref[0, hh, i]
                for c in range(t // ch):
                    rows = pl.ds(c * ch, ch)
                    lo = c * ch // LANES * LANES if masked else 0
                    st = st_scr[cn, rows, lo:t] + (bias[:, lo:t] - fk_scr[hh, rows, :])
                    if masked:
                        rk = c * ch + lax.broadcasted_iota(jnp.int32, (ch, t - lo), 0)
                        cq = lo + lax.broadcasted_iota(jnp.int32, (ch, t - lo), 1)
                        st = jnp.where(rk <= cq, st, NEG)
                    pt = jnp.exp(st)
                    pt_scr[cn, rows, lo:t] = pt.astype(BF16)
                    ds_scr[cn, rows, lo:t] = (pt * (dp_scr[cn, rows, lo:t] - dl[:, lo:t])).astype(BF16)
                    if lo > 0:
                        pt_scr[cn, rows, 0:lo] = jnp.zeros((ch, lo), BF16)
                        ds_scr[cn, rows, 0:lo] = jnp.zeros((ch, lo), BF16)
                dsb = ds_scr[cn]
                dv_acc[...] += _dot(pt_scr[cn], domh)
                dk_acc[...] += _dot(dsb, jnp.concatenate([qmh, ones_col[hh]], axis=1))
                dq_add[cn // 2] = dq_add[cn // 2] + _dot_tn(dsb, kaug[hh])
            for bi, i in enumerate(blocks):
                dq_acc[pl.ds(pl.multiple_of(i * t, t), t), :] += dq_add[bi]
            return 0

        step([j], True)
        odd = (n - 1 - j) % 2
        lax.fori_loop(0, odd, lambda _, carry: step([j + 1], False), 0)
        first_pair = j + 1 + odd
        lax.fori_loop(0, (n - first_pair) // 2,
                      lambda p, _: step([first_pair + 2 * p, first_pair + 2 * p + 1], False), 0)
        dk_ref[...] = dk_acc[:, 0:LANES].astype(BF16)
        dv_ref[...] = dv_acc[...].astype(BF16)
        dfk_ref[...] = -dk_acc[:, LANES:wide]

        @pl.when(j == n - 1)
        def _():
            dq_ref[...] = (dq_acc[:, 0:LANES] * (HEAD_DIM ** -0.5)).astype(BF16)
            dfq_ref[...] = dq_acc[:, LANES:wide]

    stat = pl.BlockSpec((1, 2, n, 1, t), lambda h, j: (h, 0, 0, 0, 0))
    blk = pl.BlockSpec((t, LANES), lambda h, j: (j, h))
    full = pl.BlockSpec((s, LANES), lambda h, j: (0, h))
    return pl.pallas_call(
        body, name="attn_bwd", grid=(4, n),
        in_specs=[full, full,
                  pl.BlockSpec((t, LANES), lambda h, j: (j, 4 + h)),
                  pl.BlockSpec((t, LANES), lambda h, j: (j, 8 + h)),
                  stat, stat, stat, blk],
        out_specs=(full, blk, blk, blk, full),
        out_shape=(jax.ShapeDtypeStruct((s, FOX_W), BF16), jax.ShapeDtypeStruct((s, FOX_W), BF16),
                   jax.ShapeDtypeStruct((s, FOX_W), BF16), jax.ShapeDtypeStruct((s, 4 * LANES), F32),
                   jax.ShapeDtypeStruct((s, 4 * LANES), F32)),
        scratch_shapes=[pltpu.VMEM((s, wide), F32), pltpu.VMEM((4, t, t), F32), pltpu.VMEM((4, t, t), F32),
                        pltpu.VMEM((4, t, t), BF16), pltpu.VMEM((4, t, t), BF16), pltpu.VMEM((t, wide), F32),
                        pltpu.VMEM((t, LANES), F32), pltpu.VMEM((2, t, 1), F32)],
        compiler_params=_cparams(("parallel", "arbitrary")))(qkv, do, qkv, qkv, lse5, dlt5, frow5, fpc)


def _ssm_block_params(a_re, a_im, log_dt, b_re, b_im):
    dt = jnp.exp(log_dt)[:, None]
    mag = jnp.exp(a_re * dt)
    ar = mag * jnp.cos(a_im * dt)
    ai = mag * jnp.sin(a_im * dt)
    den = a_re * a_re + a_im * a_im
    nr = ar - 1.0
    cr = (nr * a_re + ai * a_im) / den
    ci = (ai * a_re - nr * a_im) / den
    bbr = cr[:, :, None] * b_re - ci[:, :, None] * b_im
    bbi = cr[:, :, None] * b_im + ci[:, :, None] * b_re
    return ar, ai, bbr, bbi


def _block_diag(blocks):
    g, r, c = blocks.shape
    eye = jnp.eye(g, dtype=blocks.dtype)
    return (blocks[:, :, None, :] * eye[:, None, :, None]).reshape(g * r, g * c)


def _scan_consts(a_re, a_im, log_dt, seg, reverse):
    dt = jnp.exp(log_dt)[:, None]
    lr = (a_re * dt).reshape(1, NSTATE)
    li = (a_im * dt).reshape(1, NSTATE)
    if reverse:
        li = -li
    rows = jnp.arange(8, dtype=F32)[:, None]

    def power(k):
        mag = jnp.exp(k * lr)
        return mag * jnp.cos(k * li), mag * jnp.sin(k * li)

    tiles = list(power(1.0))
    for k in (1, 2, 4):
        keep = (rows < 8 - k) if reverse else (rows >= k)
        pr, pi_ = power(float(k * seg))
        tiles += [jnp.where(keep, pr, 0.0), jnp.where(keep, pi_, 0.0)]
    tiles += list(power(seg * ((8.0 - rows) if reverse else (rows + 1.0))))
    tiles = jnp.stack([jnp.broadcast_to(tl, (8, NSTATE)) for tl in tiles])
    steps = jnp.arange(seg, dtype=F32)[:, None]
    table = jnp.stack([jnp.broadcast_to(p[:, None, :], (seg, 8, NSTATE))
                       for p in power((seg - steps) if reverse else (steps + 1.0))])
    return tiles, table


_SCAN_W = 1024
_HALF_W = S5_W // 2
_HALF_S = NSTATE // 2


def _compact_diag(blocks_re, blocks_im):
    hg = GROUPS // 2
    return jnp.concatenate([_block_diag(b[h * hg:(h + 1) * hg]) for b in (blocks_re, blocks_im) for h in range(2)],
                           axis=1)


def _half_expand(v, w_ref, out_ref):
    for half in range(2):
        vh = v[:, half * _HALF_W:(half + 1) * _HALF_W]
        for part in range(2):
            c0 = part * NSTATE + half * _HALF_S
            out_ref[:, c0:c0 + _HALF_S] = _dot(vh, w_ref[:, c0:c0 + _HALF_S])


def _half_contract(x_ref, w_ref, half):
    out = None
    for part in range(2):
        r0 = part * NSTATE + half * _HALF_S
        term = _dot_nt(x_ref[:, r0:r0 + _HALF_S].astype(BF16), w_ref[:, r0:r0 + _HALF_S])
        out = term if out is None else out + term
    return out


def _half_outer(v, x_ref, acc_ref):
    for half in range(2):
        vh = v[:, half * _HALF_W:(half + 1) * _HALF_W]
        for part in range(2):
            c0 = part * NSTATE + half * _HALF_S
            acc_ref[:, c0:c0 + _HALF_S] += _dot_tn(vh, x_ref[:, c0:c0 + _HALF_S].astype(BF16))


def _segment_perm(tb):
    seg = tb // 8
    row = lax.broadcasted_iota(jnp.int32, (tb, tb), 0)
    col = lax.broadcasted_iota(jnp.int32, (tb, tb), 1)
    perm = (col == (row % 8) * seg + row // 8).astype(BF16)
    back = (col == (row % seg) * 8 + row // seg).astype(BF16)
    return perm, back


def _segment_ends(re, im, cf_ref, cb_ref, cr, ci, reverse):
    for n_, k in enumerate((1, 2, 4)):
        kr = cf_ref[2 + 2 * n_, :, cr]
        ki = cf_ref[3 + 2 * n_, :, cr]
        sr = pltpu.roll(re, 8 - k if reverse else k, 0)
        si = pltpu.roll(im, 8 - k if reverse else k, 0)
        re, im = re + kr * sr - ki * si, im + kr * si + ki * sr
    cbr, cbi = cb_ref[:, cr], cb_ref[:, ci]
    pr, pi_ = cf_ref[8, :, cr], cf_ref[9, :, cr]
    re, im = re + pr * cbr - pi_ * cbi, im + pr * cbi + pi_ * cbr
    edge = lax.broadcasted_iota(jnp.int32, re.shape, 0) == (7 if reverse else 0)
    in_r = jnp.where(edge, cbr, pltpu.roll(re, 7 if reverse else 1, 0))
    in_i = jnp.where(edge, cbi, pltpu.roll(im, 7 if reverse else 1, 0))
    out = slice(0, 1) if reverse else slice(7, 8)
    cb_ref[:, cr] = jnp.broadcast_to(re[out, :], re.shape)
    cb_ref[:, ci] = jnp.broadcast_to(im[out, :], im.shape)
    return in_r, in_i


def _ssm_fwd(rest, bd, cd, consts, table):
    s = rest.shape[0]
    tb = min(TB_SSM, s)
    seg = tb // 8
    ns2 = 2 * NSTATE

    def body(u_ref, bd_ref, cd_ref, cf_ref, tab_ref, y_ref, x_ref, cb_ref):
        @pl.when(pl.program_id(0) == 0)
        def _():
            cb_ref[...] = jnp.zeros_like(cb_ref)

        perm, back = _segment_perm(tb)
        _half_expand(_dot(perm, u_ref[...].astype(BF16)).astype(BF16), bd_ref, x_ref)
        for cc in range(NSTATE // _SCAN_W):
            cr = pl.ds(cc * _SCAN_W, _SCAN_W)
            ci = pl.ds(NSTATE + cc * _SCAN_W, _SCAN_W)
            ar, ai = cf_ref[0, :, cr], cf_ref[1, :, cr]

            def local(i, carry, cr=cr, ci=ci, ar=ar, ai=ai):
                re, im = carry
                rows = pl.ds(pl.multiple_of(i * 8, 8), 8)
                re, im = ar * re - ai * im + x_ref[rows, cr], ar * im + ai * re + x_ref[rows, ci]
                x_ref[rows, cr] = re
                x_ref[rows, ci] = im
                return re, im

            zero = jnp.zeros((8, _SCAN_W), F32)
            re, im = lax.fori_loop(0, seg, local, (zero, zero))
            in_r, in_i = _segment_ends(re, im, cf_ref, cb_ref, cr, ci, False)

            def fix(i, _, cr=cr, ci=ci, in_r=in_r, in_i=in_i):
                rows = pl.ds(pl.multiple_of(i * 8, 8), 8)
                tr, ti = tab_ref[0, i, :, cr], tab_ref[1, i, :, cr]
                x_ref[rows, cr] += tr * in_r - ti * in_i
                x_ref[rows, ci] += tr * in_i + ti * in_r
                return 0

            lax.fori_loop(0, seg, fix, 0)
        y_p = jnp.concatenate([_half_contract(x_ref, cd_ref, half) for half in range(2)], axis=1)
        y_ref[...] = _dot_sel(back, y_p, terms=2)

    return pl.pallas_call(
        body, name="ssm_fwd", grid=(s // tb,),
        in_specs=[pl.BlockSpec((tb, S5_W), lambda i: (i, R_U // S5_W)), _const((_HALF_W, ns2)), _const((_HALF_W, ns2)),
                  _const((10, 8, NSTATE)), _const((2, seg, 8, NSTATE))],
        out_specs=(pl.BlockSpec((tb, S5_W), lambda i: (i, 0)), pl.BlockSpec((tb, ns2), lambda i: (i, 0))),
        out_shape=(jax.ShapeDtypeStruct((s, S5_W), F32), jax.ShapeDtypeStruct((s, ns2), F32)),
        scratch_shapes=[pltpu.VMEM((8, ns2), F32)],
        compiler_params=_cparams(("arbitrary",)))(rest, bd, cd, consts, table)


def _ssm_bwd(dys, xs, rest, bd, cd, consts, table, dskip):
    s = dys.shape[0]
    tb = min(TB_SSM, s)
    seg = tb // 8
    nb = s // tb
    ns2 = 2 * NSTATE

    def body(dy_ref, x_ref, u_ref, bd_ref, cd_ref, cf_ref, tab_ref, dsk_ref, du_ref, gb_ref, gc_ref, da_ref,
             g_ref, cb_ref, acc_b, acc_c):
        step = pl.program_id(0)

        @pl.when(step == 0)
        def _():
            cb_ref[...] = jnp.zeros_like(cb_ref)
            acc_b[...] = jnp.zeros_like(acc_b)
            acc_c[...] = jnp.zeros_like(acc_c)
            da_ref[...] = jnp.zeros_like(da_ref)

        perm, back = _segment_perm(tb)
        dy = dy_ref[...]
        dy_p = _dot(perm, dy.astype(BF16)).astype(BF16)
        u_p = _dot(perm, u_ref[...].astype(BF16)).astype(BF16)
        _half_expand(dy_p, cd_ref, g_ref)
        for cc in range(NSTATE // _SCAN_W):
            cr = pl.ds(cc * _SCAN_W, _SCAN_W)
            ci = pl.ds(NSTATE + cc * _SCAN_W, _SCAN_W)
            ar, ai = cf_ref[0, :, cr], cf_ref[1, :, cr]

            def local(ii, carry, cr=cr, ci=ci, ar=ar, ai=ai):
                re, im = carry
                rows = pl.ds(pl.multiple_of((seg - 1 - ii) * 8, 8), 8)
                re, im = ar * re - ai * im + g_ref[rows, cr], ar * im + ai * re + g_ref[rows, ci]
                g_ref[rows, cr] = re
                g_ref[rows, ci] = im
                return re, im

            zero = jnp.zeros((8, _SCAN_W), F32)
            re, im = lax.fori_loop(0, seg, local, (zero, zero))
            in_r, in_i = _segment_ends(re, im, cf_ref, cb_ref, cr, ci, True)

            def fix(ii, carry, cr=cr, ci=ci, in_r=in_r, in_i=in_i):
                nr, ni, acr, aci = carry
                i = seg - 1 - ii
                rows = pl.ds(pl.multiple_of(i * 8, 8), 8)
                tr, ti = tab_ref[0, i, :, cr], tab_ref[1, i, :, cr]
                gr = g_ref[rows, cr] + tr * in_r - ti * in_i
                gi = g_ref[rows, ci] + tr * in_i + ti * in_r
                g_ref[rows, cr] = gr
                g_ref[rows, ci] = gi
                xr, xi = x_ref[rows, cr], x_ref[rows, ci]
                return gr, gi, acr + nr * xr + ni * xi, aci + ni * xr - nr * xi

            _, _, acr, aci = lax.fori_loop(0, seg, fix, (in_r, in_i, zero, zero))
            da_ref[:, cr] += acr
            da_ref[:, ci] += aci
        du_p = jnp.concatenate([_half_contract(g_ref, bd_ref, half) for half in range(2)], axis=1)
        du_ref[...] = (_dot_sel(back, du_p, terms=2) + dy * dsk_ref[...]).astype(BF16)
        _half_outer(u_p, g_ref, acc_b)
        _half_outer(dy_p, x_ref, acc_c)

        @pl.when(step == nb - 1)
        def _():
            for g in range(GROUPS):
                src = slice((g % (GROUPS // 2)) * GCH, (g % (GROUPS // 2) + 1) * GCH)
                dst = slice(g * GCH, (g + 1) * GCH)
                for part in range(2):
                    cols = slice(part * NSTATE + g * STATE, part * NSTATE + (g + 1) * STATE)
                    gb_ref[dst, part * STATE:(part + 1) * STATE] = acc_b[src, cols]
                    gc_ref[dst, part * STATE:(part + 1) * STATE] = acc_c[src, cols]

    rev = lambda i: (nb - 1 - i, 0)
    small = pl.BlockSpec((S5_W, 2 * STATE), lambda i: (0, 0))
    return pl.pallas_call(
        body, name="ssm_bwd", grid=(nb,),
        in_specs=[pl.BlockSpec((tb, S5_W), rev), pl.BlockSpec((tb, ns2), rev),
                  pl.BlockSpec((tb, S5_W), lambda i: (nb - 1 - i, R_U // S5_W)),
                  _const((_HALF_W, ns2)), _const((_HALF_W, ns2)), _const((10, 8, NSTATE)), _const((2, seg, 8, NSTATE)),
                  _const((1, S5_W))],
        out_specs=(pl.BlockSpec((tb, S5_W), rev), small, small, pl.BlockSpec((8, ns2), lambda i: (0, 0))),
        out_shape=(jax.ShapeDtypeStruct((s, S5_W), BF16), jax.ShapeDtypeStruct((S5_W, 2 * STATE), F32),
                   jax.ShapeDtypeStruct((S5_W, 2 * STATE), F32), jax.ShapeDtypeStruct((8, ns2), F32)),
        scratch_shapes=[pltpu.VMEM((tb, ns2), F32), pltpu.VMEM((8, ns2), F32),
                        pltpu.VMEM((_HALF_W, ns2), F32), pltpu.VMEM((_HALF_W, ns2), F32)],
        compiler_params=_cparams(("arbitrary",)))(dys, xs, rest, bd, cd, consts, table, dskip)


_GELU_C = math.sqrt(2.0 / math.pi)
_GELU_A = 0.044715


def _mid(o, rest, ys0, x, tgt, w, vec, hsel):
    s = o.shape[0]
    tm = min(TM, s)
    nsteps = s // tm
    half = FOX_W

    def body(o_ref, ga_ref, gb_ref, za_ref, u_ref, zb_ref, ys0_ref, x_ref, t_ref,
             wglu_ref, wua_ref, wub_ref, wout_ref, vec_ref, hsel_ref,
             dx2_ref, dga_ref, dgb_ref, do_ref, dza_ref, dzb_ref, dys_ref, dlt_ref,
             gout_hbm, gua_hbm, gub_hbm, gglu_hbm, vout_ref,
             a_out, a_ua, a_ub, a_glu):
        step = pl.program_id(0)

        @pl.when(step == 0)
        def _():
            a_out[...] = jnp.zeros_like(a_out)
            a_ua[...] = jnp.zeros_like(a_ua)
            a_ub[...] = jnp.zeros_like(a_ub)
            a_glu[...] = jnp.zeros_like(a_glu)
            vout_ref[...] = jnp.zeros_like(vout_ref)

        gate = vec_ref[0:1, :]
        gfin = vec_ref[1:2, :]
        dsk = vec_ref[2:3, 0:half]
        bglu = vec_ref[2:3, half:2 * half]

        o_v = o_ref[...]
        za = za_ref[...]
        sza = _sigmoid(za)
        silu_za = za * sza
        ya_b = (o_v * silu_za).astype(BF16)
        u_v = u_ref[...]
        ys = ys0_ref[...] + dsk * u_v
        inner = _GELU_C * (ys + _GELU_A * ys * ys * ys)
        th = jnp.tanh(inner)
        yg = 0.5 * ys * (1.0 + th)
        yg_b = yg.astype(BF16)
        st = _sigmoid(_dot(yg_b, wglu_ref[...]) + bglu)
        yb1 = yg * st
        zb = zb_ref[...]
        szb = _sigmoid(zb)
        silu_zb = zb * szb
        yb_b = (yb1 * silu_zb).astype(BF16)
        ua = _dot(ya_b, wua_ref[...])
        ub = _dot(yb_b, wub_ref[...])
        sga = _sigmoid(ga_ref[...])
        sgb = _sigmoid(gb_ref[...])
        merged_b = (sga * ua + sgb * ub).astype(BF16)
        mo = _dot(merged_b, wout_ref[...])
        x2 = x_ref[...] + gate * mo
        r2 = lax.rsqrt(jnp.mean(x2 * x2, axis=-1, keepdims=True) + EPS)
        x2n = x2 * r2
        diff = x2n * gfin - t_ref[...]
        loss = 0.5 * jnp.sum(jnp.mean(diff * diff, axis=-1, keepdims=True), axis=0, keepdims=True)
        dy = diff * (1.0 / D_MODEL)
        dx2n = dy * gfin
        dx2 = r2 * (dx2n - x2n * jnp.mean(dx2n * x2n, axis=-1, keepdims=True))
        dx2_ref[...] = dx2
        vout_ref[0:1, :] += jnp.sum(dy * x2n, axis=0, keepdims=True)
        vout_ref[1:2, :] += jnp.sum(dx2 * mo, axis=0, keepdims=True)
        vout_ref[3:4, :] += jnp.broadcast_to(loss, (1, D_MODEL))
        dmo_b = (dx2 * gate).astype(BF16)
        dmerged = _dot_nt(dmo_b, wout_ref[...])
        a_out[...] += _dot_tn(merged_b, dmo_b)
        dua_b = (dmerged * sga).astype(BF16)
        dub_b = (dmerged * sgb).astype(BF16)
        dga_ref[...] = (dmerged * ua * sga * (1.0 - sga)).astype(BF16)
        dgb_ref[...] = (dmerged * ub * sgb * (1.0 - sgb)).astype(BF16)
        dya = _dot_nt(dua_b, wua_ref[...])
        dyb = _dot_nt(dub_b, wub_ref[...])
        a_ua[...] += _dot_tn(ya_b, dua_b)
        a_ub[...] += _dot_tn(yb_b, dub_b)
        do_b = (dya * silu_za).astype(BF16)
        do_ref[...] = do_b
        dza_ref[...] = (dya * o_v * (sza * (1.0 + za * (1.0 - sza)))).astype(BF16)
        hsel = hsel_ref[...].astype(BF16)
        dlt_ref[...] = sum(_dot_nt(hsel, part) for part in _split3(do_b.astype(F32) * o_v))
        dyb1 = dyb * silu_zb
        dzb_ref[...] = (dyb * yb1 * (szb * (1.0 + zb * (1.0 - szb)))).astype(BF16)
        dt = dyb1 * yg * st * (1.0 - st)
        dt_b = dt.astype(BF16)
        dyg = dyb1 * st + _dot_nt(dt_b, wglu_ref[...])
        a_glu[...] += _dot_tn(yg_b, dt_b)
        dgelu = 0.5 * (1.0 + th) + 0.5 * ys * (1.0 - th * th) * _GELU_C * (1.0 + 3.0 * _GELU_A * ys * ys)
        dys = dyg * dgelu
        dys_ref[...] = dys
        vout_ref[2:3, 0:half] += jnp.sum(dys * u_v, axis=0, keepdims=True)
        vout_ref[2:3, half:2 * half] += jnp.sum(dt, axis=0, keepdims=True)

        @pl.when(step == nsteps - 1)
        def _():
            pltpu.sync_copy(a_out, gout_hbm)
            pltpu.sync_copy(a_ua, gua_hbm)
            pltpu.sync_copy(a_ub, gub_hbm)
            pltpu.sync_copy(a_glu, gglu_hbm)

    def rows(width, col=0):
        return pl.BlockSpec((tm, width), lambda i, col=col: (i, col))

    anyspace = pl.BlockSpec(memory_space=pl.ANY)
    wshapes = [(S5_W, S5_W), (FOX_W, D_MODEL), (S5_W, D_MODEL), (D_MODEL, D_MODEL)]
    return pl.pallas_call(
        body, name="mid", grid=(nsteps,),
        in_specs=[rows(FOX_W), rows(D_MODEL, R_GA // D_MODEL), rows(D_MODEL, R_GB // D_MODEL),
                  rows(FOX_W, R_ZA // FOX_W), rows(S5_W, R_U // S5_W), rows(S5_W, R_ZB // S5_W),
                  rows(S5_W), rows(D_MODEL), rows(D_MODEL)]
                 + [_const(sh) for sh in wshapes]
                 + [_const((8, D_MODEL)), _const((HEADS, FOX_W))],
        out_specs=(rows(D_MODEL), rows(D_MODEL), rows(D_MODEL), rows(FOX_W), rows(FOX_W), rows(S5_W), rows(S5_W),
                   pl.BlockSpec((HEADS, tm), lambda i: (0, i)),
                   anyspace, anyspace, anyspace, anyspace, pl.BlockSpec((8, D_MODEL), lambda i: (0, 0))),
        out_shape=(jax.ShapeDtypeStruct((s, D_MODEL), F32), jax.ShapeDtypeStruct((s, D_MODEL), BF16),
                   jax.ShapeDtypeStruct((s, D_MODEL), BF16), jax.ShapeDtypeStruct((s, FOX_W), BF16),
                   jax.ShapeDtypeStruct((s, FOX_W), BF16), jax.ShapeDtypeStruct((s, S5_W), BF16),
                   jax.ShapeDtypeStruct((s, S5_W), F32), jax.ShapeDtypeStruct((HEADS, s), F32),
                   jax.ShapeDtypeStruct((D_MODEL, D_MODEL), F32), jax.ShapeDtypeStruct((FOX_W, D_MODEL), F32),
                   jax.ShapeDtypeStruct((S5_W, D_MODEL), F32), jax.ShapeDtypeStruct((S5_W, S5_W), F32),
                   jax.ShapeDtypeStruct((8, D_MODEL), F32)),
        scratch_shapes=[pltpu.VMEM((D_MODEL, D_MODEL), F32), pltpu.VMEM((FOX_W, D_MODEL), F32),
                        pltpu.VMEM((S5_W, D_MODEL), F32), pltpu.VMEM((S5_W, S5_W), F32)],
        compiler_params=_cparams(("arbitrary",)),
    )(o, rest, rest, rest, rest, rest, ys0, x, tgt, *w, vec, hsel)


def _dh(dq, dk, dv, dga, dgb, dza, du, dzb, df, wqkv_t, wrest_t, x, dx2, gs, scatter_srcs):
    s = x.shape[0]
    tm = min(TM_PROJ, s)
    nsteps = s // tm
    na = len(scatter_srcs)

    def body(dq_ref, dk_ref, dv_ref, dga_ref, dgb_ref, dza_ref, du_ref, dzb_ref, df_ref, wq_ref, wr_ref,
             x_ref, dx2_ref, gs_ref, *rest_refs):
        src_refs = rest_refs[:na]
        gx_ref, vout_ref = rest_refs[na:na + 2]
        out_refs = rest_refs[na + 2:2 * na + 2]
        send_sems, recv_sems = rest_refs[2 * na + 2:]
        step = pl.program_id(0)
        cx, cy, cc = lax.axis_index("x"), lax.axis_index("y"), lax.axis_index("c")
        peers = [(1 - cx, cy), (cx, 1 - cy), (1 - cx, 1 - cy)]

        def copy(a, k, px, py, slot):
            return pltpu.make_async_remote_copy(
                src_ref=src_refs[a].at[2 * px + py], dst_ref=out_refs[a].at[slot],
                send_sem=send_sems.at[a * 3 + k], recv_sem=recv_sems.at[a * 3 + k],
                device_id=(px, py, cc), device_id_type=MESH)

        @pl.when(step == 0)
        def _():
            vout_ref[...] = jnp.zeros_like(vout_ref)
            for a in range(na):
                for k, (px, py) in enumerate(peers):
                    copy(a, k, px, py, 2 * cx + cy).start()

        dh = _dot(dq_ref[...], wq_ref[0:512, :])
        dh += _dot(dk_ref[...], wq_ref[512:1024, :])
        dh += _dot(dv_ref[...], wq_ref[1024:1536, :])
        dh += _dot(dga_ref[...], wr_ref[R_GA:R_GB, :])
        dh += _dot(dgb_ref[...], wr_ref[R_GB:R_ZA, :])
        dh += _dot(dza_ref[...], wr_ref[R_ZA:R_U, :])
        dh += _dot(du_ref[...], wr_ref[R_U:R_ZB, :])
        dh += _dot(dzb_ref[...], wr_ref[R_ZB:R_F, :])
        dh += _dot(df_ref[...], wr_ref[R_F:REST_W, :])
        xv = x_ref[...]
        r = lax.rsqrt(jnp.mean(xv * xv, axis=-1, keepdims=True) + EPS)
        xn = xv * r
        dxn = dh * gs_ref[...]
        gx_ref[...] = dx2_ref[...] + r * (dxn - xn * jnp.mean(dxn * xn, axis=-1, keepdims=True))
        vout_ref[0:1, :] += jnp.sum(dh * xn, axis=0, keepdims=True)
        vout_ref[1:2, :] += jnp.sum(dh, axis=0, keepdims=True)

        @pl.when(step == nsteps - 1)
        def _():
            for a in range(na):
                for k, (px, py) in enumerate(peers):
                    copy(a, k, px, py, 2 * px + py).wait_recv()
            for a in range(na):
                for k, (px, py) in enumerate(peers):
                    copy(a, k, px, py, 2 * cx + cy).wait_send()

    def rows(width):
        return pl.BlockSpec((tm, width), lambda i: (i, 0))

    anyspace = pl.BlockSpec(memory_space=pl.ANY)
    return pl.pallas_call(
        body, name="dh", grid=(nsteps,),
        in_specs=[rows(512), rows(512), rows(512), rows(1024), rows(1024), rows(512), rows(512), rows(512), rows(128),
                  _const((1536, D_MODEL)), _const((REST_W, D_MODEL)), rows(D_MODEL), rows(D_MODEL), _const((1, D_MODEL))]
                 + [anyspace] * na,
        out_specs=(rows(D_MODEL), pl.BlockSpec((8, D_MODEL), lambda i: (0, 0))) + (anyspace,) * na,
        out_shape=(jax.ShapeDtypeStruct((s, D_MODEL), F32), jax.ShapeDtypeStruct((8, D_MODEL), F32))
                  + tuple(jax.ShapeDtypeStruct(a.shape, a.dtype) for a in scatter_srcs),
        scratch_shapes=[pltpu.SemaphoreType.DMA((3 * na,)), pltpu.SemaphoreType.DMA((3 * na,))],
        compiler_params=_cparams(("arbitrary",)),
    )(dq, dk, dv, dga, dgb, dza, du, dzb, df, wqkv_t, wrest_t, x, dx2, gs, *scatter_srcs)


def _row_block(rows, mult=8, cap=512):
    if rows <= mult:
        return rows
    padded = -(-rows // mult) * mult
    for cand in range(min(cap, padded) // mult * mult, 0, -mult):
        if padded % cand == 0:
            return cand
    return padded


def _sum4(parts, name):
    rows, cols = parts.shape[1:]
    br = _row_block(rows, 16, 1024)

    def body(p_ref, o_ref):
        acc = p_ref[0].astype(F32)
        for k in range(1, 4):
            acc = acc + p_ref[k].astype(F32)
        o_ref[...] = acc

    return pl.pallas_call(
        body, name=name, grid=(pl.cdiv(rows, br),),
        in_specs=[pl.BlockSpec((4, br, cols), lambda i: (0, i, 0))],
        out_specs=pl.BlockSpec((br, cols), lambda i: (i, 0)),
        out_shape=jax.ShapeDtypeStruct((rows, cols), F32), compiler_params=_cparams(("parallel",)))(parts)


def _pair_add(a, b, name):
    shape = a.shape
    a, b = a.reshape(-1, shape[-1]), b.reshape(-1, shape[-1])
    rows, cols = a.shape
    br = _row_block(rows, 16, 1024)

    def body(a_ref, b_ref, o_ref):
        o_ref[...] = (a_ref[...].astype(F32) + b_ref[...].astype(F32)).astype(BF16)

    spec = pl.BlockSpec((br, cols), lambda i: (i, 0))
    return pl.pallas_call(
        body, name=name, grid=(pl.cdiv(rows, br),), in_specs=[spec, spec], out_specs=spec,
        out_shape=jax.ShapeDtypeStruct((rows, cols), BF16), compiler_params=_cparams(("parallel",)))(a, b).reshape(shape)


def _adamw(w, g, m, v, name):
    rows, cols = w.shape
    br = _row_block(rows)

    def body(w_ref, g_ref, m_ref, v_ref, d_ref, nm_ref, nv_ref):
        gv = g_ref[...]
        nm = ADAM_B1 * m_ref[...] + (1.0 - ADAM_B1) * gv
        nv = ADAM_B2 * v_ref[...] + (1.0 - ADAM_B2) * (gv * gv)
        m_hat = nm / (1.0 - ADAM_B1 ** ADAM_STEP)
        v_hat = nv / (1.0 - ADAM_B2 ** ADAM_STEP)
        d_ref[...] = -ADAM_LR * (m_hat / (jnp.sqrt(v_hat) + ADAM_EPS) + ADAM_WD * w_ref[...])
        nm_ref[...] = nm
        nv_ref[...] = nv

    spec = pl.BlockSpec((br, cols), lambda i: (i, 0))
    shape = jax.ShapeDtypeStruct((rows, cols), F32)
    return pl.pallas_call(
        body, name=name, grid=(pl.cdiv(rows, br),), in_specs=[spec] * 4, out_specs=(spec,) * 3,
        out_shape=(shape,) * 3, compiler_params=_cparams(("parallel",)))(w, g, m, v)


def _pack(parts, row_multiple=8):
    flat = []
    for p in parts:
        v = p.reshape(-1).astype(F32)
        pad = (-v.shape[0]) % LANES
        if pad:
            v = jnp.concatenate([v, jnp.zeros((pad,), F32)])
        flat.append(v)
    v = jnp.concatenate(flat)
    rows = v.shape[0] // LANES
    pad_rows = (-rows) % row_multiple
    if pad_rows:
        v = jnp.concatenate([v, jnp.zeros((pad_rows * LANES,), F32)])
    return v.reshape(-1, LANES)


def _unpack(packed, shapes):
    lead = packed.shape[:-2]
    flat = packed.reshape(lead + (-1,))
    out, off = [], 0
    for sh in shapes:
        size = math.prod(sh)
        out.append(flat[..., off:off + size].reshape(lead + tuple(sh)))
        off += size + (-size) % LANES
    return out


def kernel(x, c, w_ada, b_ada, g_norm, w_in, b_f, a_re, a_im, log_dt, b_re, b_im, c_re, c_im, d_skip, w_glu, b_glu, w_up_a, w_up_b, w_out, g_final, loss_target, m_w_ada, m_b_ada, m_g_norm, m_w_in, m_b_f, m_a_re, m_a_im, m_log_dt, m_b_re, m_b_im, m_c_re, m_c_im, m_d_skip, m_w_glu, m_b_glu, m_w_up_a, m_w_up_b, m_w_out, m_g_final, v_w_ada, v_b_ada, v_g_norm, v_w_in, v_b_f, v_a_re, v_a_im, v_log_dt, v_b_re, v_b_im, v_c_re, v_c_im, v_d_skip, v_w_glu, v_b_glu, v_w_up_a, v_w_up_b, v_w_out, v_g_final):
    xi, yi, ci = lax.axis_index("x"), lax.axis_index("y"), lax.axis_index("c")
    chip = 2 * xi + yi
    me = 4 * xi + 2 * yi + ci
    s = x.shape[1]
    x2d = x[0]
    tgt = loss_target[0]
    n_att = s // min(T_ATT, s)
    t_att = min(T_ATT, s)

    c_all, _ = _allgather8(c.reshape(8, LANES), "gather_c")
    c_all = c_all.reshape(8, D_MODEL)
    ncol = w_ada.shape[2]
    b_cols = lax.dynamic_slice_in_dim(b_ada, chip * ncol, ncol, axis=1)
    mod_cols = _mod_cols(c_all, w_ada[0], b_cols)
    mod_all, _ = _allgather8(mod_cols.reshape(-1, LANES), "gather_mod")
    mod_all = mod_all.reshape(4, 2, 8, ncol)[:, 0]
    mod_me = lax.dynamic_index_in_dim(mod_all, me, axis=1, keepdims=False).reshape(1, 3 * D_MODEL)
    shift, scale, gate = mod_me[:, :D_MODEL], mod_me[:, D_MODEL:2 * D_MODEL], mod_me[:, 2 * D_MODEL:]
    gs = g_norm * (1.0 + scale)

    nshard = w_in.shape[2]
    w_in_t, m_in_t, v_in_t = (jnp.swapaxes(a[0], 0, 1) for a in (w_in, m_w_in, v_w_in))
    wt_pack = jnp.pad(w_in_t.astype(BF16), ((0, SHARD_ROWS - nshard), (0, 0)))
    misc_shapes = [w_glu.shape[1:], w_up_a.shape[1:], w_up_b.shape[1:], w_out.shape[1:]]
    misc_pack = jnp.concatenate([w.reshape(-1) for w in (w_glu, w_up_a, w_up_b, w_out)]).astype(BF16).reshape(-1, LANES)
    def halves(a):
        return a.reshape((2, a.shape[0] // 2) + a.shape[1:])

    wt_all, misc_all = _gather_shards([halves(wt_pack), halves(misc_pack)], "gather_weights")
    wt_all = lax.dynamic_update_index_in_dim(wt_all, halves(wt_pack), chip, 0).reshape((4,) + wt_pack.shape)
    misc_all = lax.dynamic_update_index_in_dim(misc_all, halves(misc_pack), chip, 0).reshape((4,) + misc_pack.shape)
    p_glu, p_ua, p_ub, p_out = _unpack(misc_all, misc_shapes)

    def w_rows(lo, hi):
        out = []
        for j in range(4):
            a, b = max(lo, j * nshard), min(hi, (j + 1) * nshard)
            if a < b:
                out.append(wt_all[j, a - j * nshard:b - j * nshard])
        return out

    wqkv_t = jnp.concatenate(w_rows(O_Q, O_F), axis=0)
    wrest_t = jnp.concatenate(w_rows(O_GA, O_GB) + w_rows(O_GB, O_END) + w_rows(O_ZA, O_U) + w_rows(O_U, O_ZB)
                              + w_rows(O_ZB, O_GA) + w_rows(O_F, O_ZA)
                              + [jnp.zeros((REST_W - R_F - HEADS, D_MODEL), BF16)], axis=0)
    wmid = (p_glu.reshape(S5_W, S5_W), jnp.concatenate([p_ua[j] for j in range(4)], axis=1),
            jnp.concatenate([p_ub[j] for j in range(4)], axis=1), p_out.reshape(D_MODEL, D_MODEL))

    h, qkv, rest = _prenorm_proj(x2d, gs, shift, wqkv_t, wrest_t)
    bf128 = jnp.pad(b_f, ((0, 0), (0, LANES - HEADS)))
    fpc, f_t, kbias, qbias = _fcum(rest, bf128, _bias_selectors())
    frow5 = f_t.reshape(4, 2, n_att, 1, t_att)
    o, lse_pc = _attn_fwd(qkv, kbias, qbias)

    abar_r, abar_i, bb_r, bb_i = _ssm_block_params(a_re[0], a_im[0], log_dt[0], b_re[0], b_im[0])
    bb_rt, bb_it = jnp.swapaxes(bb_r, 1, 2).astype(BF16), jnp.swapaxes(bb_i, 1, 2).astype(BF16)
    cr_b, ci_b = c_re[0].astype(BF16), (-c_im[0]).astype(BF16)
    bd_c, cd_c = _compact_diag(bb_rt, bb_it), _compact_diag(cr_b, ci_b)
    seg = min(TB_SSM, s) // 8
    ys0, xs = _ssm_fwd(rest, bd_c, cd_c, *_scan_consts(a_re[0], a_im[0], log_dt[0], seg, False))

    vec = jnp.concatenate([gate, g_final.reshape(1, D_MODEL), jnp.concatenate([d_skip, b_glu], axis=1),
                           jnp.zeros((5, D_MODEL), F32)], axis=0)
    hsel = jnp.repeat(jnp.eye(HEADS, dtype=F32), HEAD_DIM, axis=1)
    (dx2, dga, dgb, do, dza, dzb, dys, dlt_t, g_out, g_ua, g_ub, g_glu, vmid) = _mid(
        o, rest, ys0, x2d, tgt, wmid, vec, hsel)

    lse_t = jnp.transpose(lse_pc.reshape(s, 4 // ATT_PAIRS, LANES)[:, :, :2 * ATT_PAIRS], (1, 2, 0))
    lse5 = lse_t.reshape(4, 2, n_att, 1, t_att)
    dlt5 = dlt_t.reshape(4, 2, n_att, 1, t_att)
    dq, dk, dv, dfk, dfq = _attn_bwd(qkv, do, lse5, dlt5, frow5, fpc)
    du, g_bd, g_cdt, da8 = _ssm_bwd(dys, xs, rest, bd_c, cd_c, *_scan_consts(a_re[0], a_im[0], log_dt[0], seg, True),
                                    d_skip)
    df, dbf8 = _dfcum(dfk, dfq, rest, bf128)

    gq, gk, gv, gga, ggb, gza, gu, gzb, gf = _grad_w_rows(h, [dq, dk, dv, dga, dgb, dza, du, dzb, df])
    g_in_t = jnp.concatenate([gq, gk, gv, gf[:HEADS], gza, gu, gzb, gga, ggb], axis=0)

    def shard_cols(g, j):
        n = g.shape[1] // 4
        return g[:, j * n:(j + 1) * n]

    def shard_rows(g, j):
        n = g.shape[0] // 4
        return g[j * n:(j + 1) * n]

    def halves4(a):
        return a.reshape((4, 2, a.shape[1] // 2) + a.shape[2:])

    gt_pack = halves4(jnp.stack([
        jnp.pad(g_in_t[j * nshard:(j + 1) * nshard].astype(BF16), ((0, SHARD_ROWS - nshard), (0, 0)))
        for j in range(4)]))
    gm_pack = halves4(jnp.stack([
        jnp.concatenate([shard_rows(g_glu, j).reshape(-1), shard_cols(g_ua, j).reshape(-1),
                         shard_cols(g_ub, j).reshape(-1), shard_rows(g_out, j).reshape(-1)]).astype(BF16)
        .reshape(-1, LANES) for j in range(4)]))
    recv_in, recv_misc = _swap_sibling([gt_pack, gm_pack], "pair_swap_weight_grads", other_half=True)
    own_in = lax.dynamic_index_in_dim(gt_pack, ci, axis=1, keepdims=False)
    own_misc = lax.dynamic_index_in_dim(gm_pack, ci, axis=1, keepdims=False)
    pair_in = _pair_add(own_in, recv_in, "pair_add_w_in")
    pair_misc = _pair_add(own_misc, recv_misc, "pair_add_misc")

    grad_x, vdh, parts_in, parts_misc = _dh(dq, dk, dv, dga, dgb, dza, du, dzb, df, wqkv_t, wrest_t, x2d, dx2, gs,
                                            [pair_in, pair_misc])
    parts_in = lax.dynamic_update_slice_in_dim(parts_in, lax.dynamic_slice_in_dim(pair_in, chip, 1, 0), chip, 0)
    parts_misc = lax.dynamic_update_slice_in_dim(parts_misc, lax.dynamic_slice_in_dim(pair_misc, chip, 1, 0), chip, 0)
    half_in, half_misc = _sum4(parts_in, "sum4_w_in"), _sum4(parts_misc, "sum4_misc")
    sib_in, sib_misc = _swap_sibling([half_in, half_misc], "swap_weight_grads")

    def both_halves(mine, theirs):
        return jnp.concatenate([jnp.where(ci == 0, mine, theirs), jnp.where(ci == 0, theirs, mine)], axis=0)

    tot_in, tot_misc = both_halves(half_in, sib_in), both_halves(half_misc, sib_misc)
    g_glu_s, g_ua_s, g_ub_s, g_out_s = _unpack(tot_misc, misc_shapes)

    dgs, dshift = vdh[0:1], vdh[1:2]
    dmod = jnp.concatenate([dshift, dgs * g_norm, vmid[1:2]], axis=1)
    da = jnp.sum(da8, axis=0)
    g_bd = g_bd.reshape(GROUPS, GCH, 2 * STATE)
    g_cdt = g_cdt.reshape(GROUPS, GCH, 2 * STATE)
    g_bbr = jnp.swapaxes(g_bd[:, :, :STATE], 1, 2)
    g_bbi = jnp.swapaxes(g_bd[:, :, STATE:], 1, 2)
    g_cre = g_cdt[:, :, :STATE]
    g_cim = -g_cdt[:, :, STATE:]
    small_shapes = [(1,), (3 * D_MODEL,), (D_MODEL,), (HEADS,), (GROUPS, STATE), (GROUPS, STATE),
                    (GROUPS, STATE, GCH), (GROUPS, STATE, GCH), (GROUPS, GCH, STATE), (GROUPS, GCH, STATE),
                    (S5_W,), (S5_W,), (D_MODEL,)]
    small = _pack([vmid[3, 0:1], dmod, dgs * (1.0 + scale), dbf8[0, :HEADS], da[:NSTATE], da[NSTATE:],
                   g_bbr, g_bbi, g_cre, g_cim, vmid[2, :S5_W], vmid[2, S5_W:], vmid[0]])
    small_all, small_sum = _allgather8(small, "gather_small_grads")
    (loss_s, g_b_ada, g_g_norm, g_b_f, g_abr, g_abi, g_bbr_s, g_bbi_s, g_c_re, g_c_im, g_d_skip, g_b_glu,
     g_g_final) = _unpack(small_sum, small_shapes)
    loss = loss_s[0]
    dmod_all = _unpack(small_all, small_shapes)[1]
    dmod_cols = lax.dynamic_slice_in_dim(dmod_all, chip * ncol, ncol, axis=1)
    g_w_ada = _grad_w_ada(c_all, dmod_cols)
    _, ssm_vjp = jax.vjp(_ssm_block_params, a_re[0], a_im[0], log_dt[0], b_re[0], b_im[0])
    g_a_re, g_a_im, g_log_dt, g_b_re, g_b_im = ssm_vjp((g_abr, g_abi, g_bbr_s, g_bbi_s))

    def adam(name, w, g, m, v):
        shape = w.shape
        total = math.prod(shape)
        if len(shape) > 1 and shape[-1] >= LANES:
            cols = shape[-1]
        elif total % LANES == 0:
            cols = LANES
        else:
            cols = total
        two = lambda a: a.reshape(-1, cols)
        d, nm, nv = _adamw(two(w), two(g), two(m), two(v), "adamw_" + name)
        return g.reshape(shape), d.reshape(shape), nm.reshape(shape), nv.reshape(shape)

    back = lambda a: jnp.swapaxes(a, 0, 1)[None]
    d_in_t, nm_in_t, nv_in_t = _adamw(w_in_t, tot_in, m_in_t, v_in_t, "adamw_w_in")
    res_w_in = (back(tot_in[:nshard]), back(d_in_t), back(nm_in_t), back(nv_in_t))

    res = [
        adam("w_ada", w_ada, g_w_ada, m_w_ada, v_w_ada),
        adam("b_ada", b_ada, g_b_ada, m_b_ada, v_b_ada),
        adam("g_norm", g_norm, g_g_norm, m_g_norm, v_g_norm),
        res_w_in,
        adam("b_f", b_f, g_b_f, m_b_f, v_b_f),
        adam("a_re", a_re, g_a_re, m_a_re, v_a_re),
        adam("a_im", a_im, g_a_im, m_a_im, v_a_im),
        adam("log_dt", log_dt, g_log_dt, m_log_dt, v_log_dt),
        adam("b_re", b_re, g_b_re, m_b_re, v_b_re),
        adam("b_im", b_im, g_b_im, m_b_im, v_b_im),
        adam("c_re", c_re, g_c_re, m_c_re, v_c_re),
        adam("c_im", c_im, g_c_im, m_c_im, v_c_im),
        adam("d_skip", d_skip, g_d_skip, m_d_skip, v_d_skip),
        adam("w_glu", w_glu, g_glu_s, m_w_glu, v_w_glu),
        adam("b_glu", b_glu, g_b_glu, m_b_glu, v_b_glu),
        adam("w_up_a", w_up_a, g_ua_s, m_w_up_a, v_w_up_a),
        adam("w_up_b", w_up_b, g_ub_s, m_w_up_b, v_w_up_b),
        adam("w_out", w_out, g_out_s, m_w_out, v_w_out),
        adam("g_final", g_final, g_g_final, m_g_final, v_g_final),
    ]
    grads = [r[0] for r in res]
    deltas = [r[1] for r in res]
    new_m = [r[2] for r in res]
    new_v = [r[3] for r in res]
    return (loss, grad_x[None], *grads, *deltas, *new_m, *new_v)
```

```python
import math

import jax
import jax.numpy as jnp
from jax import lax
from jax.experimental import pallas as pl
from jax.experimental.pallas import tpu as pltpu

F32 = jnp.float32
BF16 = jnp.bfloat16
HI = lax.Precision.HIGHEST
MESH = pl.DeviceIdType.MESH

D_MODEL = 1024
HEADS = 8
HEAD_DIM = 64
FOX_W = 512
S5_W = 512
GROUPS = 32
STATE = 64
GCH = 16
NSTATE = GROUPS * STATE
EPS = 1e-6
NEG = -1e30

ADAM_LR = 0.001
ADAM_B1 = 0.9
ADAM_B2 = 0.999
ADAM_EPS = 1e-08
ADAM_WD = 0.01
ADAM_STEP = 10

VMEM_LIMIT = 56 * 1024 * 1024
LANES = 128

TM = 256
TM_PROJ = 512
T_ATT = 512
ATT_CHUNK = 32
ATT_PAIRS = 4
TB_SSM = 512
TK_ACC = 512
TB_CUM = 256
SHARD_ROWS = 1312

O_Q, O_K, O_V, O_F, O_ZA, O_U, O_ZB, O_GA, O_GB, O_END = 0, 512, 1024, 1536, 1544, 2056, 2568, 3080, 4104, 5128
REST_W = 3712
R_GA, R_GB, R_ZA, R_U, R_ZB, R_F = 0, 1024, 2048, 2560, 3072, 3584


def _cparams(sem=None):
    kw = dict(vmem_limit_bytes=VMEM_LIMIT)
    if sem is not None:
        kw["dimension_semantics"] = sem
    return pltpu.CompilerParams(**kw)


def _const(shape):
    nd = len(shape)
    return pl.BlockSpec(shape, lambda *_: (0,) * nd, pipeline_mode=pl.Buffered(1))


def _dot(a, b, precision=None):
    return jnp.dot(a, b, preferred_element_type=F32, precision=precision)


def _dot_nt(a, b):
    return lax.dot_general(a, b, (((1,), (1,)), ((), ())), preferred_element_type=F32)


def _dot_tn(a, b, precision=None):
    return lax.dot_general(a, b, (((0,), (0,)), ((), ())), preferred_element_type=F32, precision=precision)


def _sigmoid(z):
    return 1.0 / (1.0 + jnp.exp(-z))


def _split3(x):
    hi = x.astype(BF16)
    r1 = x - hi.astype(F32)
    mid = r1.astype(BF16)
    lo = (r1 - mid.astype(F32)).astype(BF16)
    return hi, mid, lo


def _dot_sel(sel, x, terms=3):
    s16 = sel.astype(BF16)
    return sum(_dot(s16, part) for part in _split3(x)[:terms])


def _allgather8(xs, name):
    rows = xs.shape[0]

    def body(x_ref, out_ref, sum_ref, send_sems, recv_sems, local_sem):
        x, y, c = lax.axis_index("x"), lax.axis_index("y"), lax.axis_index("c")
        me, sibling = (x, y, c), (x, y, 1 - c)
        chips = [(1 - x, y), (x, 1 - y), (1 - x, 1 - y)]

        def slot(px, py, pc):
            return out_ref.at[4 * px + 2 * py + pc]

        def copy(k, block, to, src=None):
            return pltpu.make_async_remote_copy(
                src_ref=slot(*block) if src is None else src, dst_ref=slot(*block),
                send_sem=send_sems.at[k], recv_sem=recv_sems.at[k], device_id=to, device_id_type=MESH)

        mine = pltpu.make_async_copy(x_ref, slot(*me), local_sem)
        mine.start()
        first = [copy(0, me, sibling, src=x_ref)]
        first += [copy(1 + j, me, (*chip, c), src=x_ref) for j, chip in enumerate(chips)]
        for cp in first:
            cp.start()
        passed = [copy(4 + j, (*chip, c), sibling) for j, chip in enumerate(chips)]
        for j, chip in enumerate(chips):
            copy(1 + j, (*chip, c), me).wait_recv()
            passed[j].start()
        copy(0, sibling, me).wait_recv()
        for j, chip in enumerate(chips):
            copy(4 + j, (*chip, 1 - c), me).wait_recv()
        for cp in first + passed:
            cp.wait_send()
        mine.wait()
        acc = out_ref[0]
        for d in range(1, 8):
            acc = acc + out_ref[d]
        sum_ref[...] = acc

    return pl.pallas_call(
        body, name=name,
        out_shape=(jax.ShapeDtypeStruct((8, rows, LANES), F32), jax.ShapeDtypeStruct((rows, LANES), F32)),
        in_specs=[pl.BlockSpec(memory_space=pltpu.VMEM)],
        out_specs=(pl.BlockSpec(memory_space=pltpu.VMEM), pl.BlockSpec(memory_space=pltpu.VMEM)),
        scratch_shapes=[pltpu.SemaphoreType.DMA((7,)), pltpu.SemaphoreType.DMA((7,)), pltpu.SemaphoreType.DMA],
        compiler_params=_cparams(),
    )(xs)


def _gather_shards(srcs, name):
    na = len(srcs)

    def body(*refs):
        src_refs, out_refs = refs[:na], refs[na:2 * na]
        send_sems, recv_sems = refs[2 * na:]
        x, y, c = lax.axis_index("x"), lax.axis_index("y"), lax.axis_index("c")
        sibling = (x, y, 1 - c)
        peers = [(1 - x, y), (x, 1 - y), (1 - x, 1 - y)]

        def copy(a, k, src, slot, which, to):
            return pltpu.make_async_remote_copy(
                src_ref=src, dst_ref=out_refs[a].at[slot, which],
                send_sem=send_sems.at[a * 6 + k], recv_sem=recv_sems.at[a * 6 + k],
                device_id=to, device_id_type=MESH)

        mine = 2 * x + y
        first = [copy(a, k, src_refs[a].at[c], mine, c, (px, py, c))
                 for a in range(na) for k, (px, py) in enumerate(peers)]
        for cp in first:
            cp.start()
        passed = []
        for a in range(na):
            for k, (px, py) in enumerate(peers):
                slot = 2 * px + py
                landed = out_refs[a].at[slot, c]
                copy(a, k, landed, slot, c, (px, py, c)).wait_recv()
                fwd = copy(a, 3 + k, landed, slot, c, sibling)
                fwd.start()
                passed.append(fwd)
        for a in range(na):
            for k, (px, py) in enumerate(peers):
                slot = 2 * px + py
                copy(a, 3 + k, out_refs[a].at[slot, 1 - c], slot, 1 - c, sibling).wait_recv()
        for cp in first + passed:
            cp.wait_send()

    anyspace = pl.BlockSpec(memory_space=pl.ANY)
    return pl.pallas_call(
        body, name=name,
        out_shape=tuple(jax.ShapeDtypeStruct((4,) + tuple(a.shape), a.dtype) for a in srcs),
        in_specs=[anyspace] * na, out_specs=(anyspace,) * na,
        scratch_shapes=[pltpu.SemaphoreType.DMA((6 * na,)), pltpu.SemaphoreType.DMA((6 * na,))],
        compiler_params=_cparams(),
    )(*srcs)


def _swap_sibling(srcs, name, other_half=False):
    na = len(srcs)

    def body(*refs):
        src_refs, out_refs = refs[:na], refs[na:2 * na]
        send_sems, recv_sems = refs[2 * na:]
        x, y, c = lax.axis_index("x"), lax.axis_index("y"), lax.axis_index("c")
        copies = [pltpu.make_async_remote_copy(
            src_ref=src_refs[a].at[:, 1 - c] if other_half else src_refs[a],
            dst_ref=out_refs[a], send_sem=send_sems.at[a], recv_sem=recv_sems.at[a],
            device_id=(x, y, 1 - c), device_id_type=MESH) for a in range(na)]
        for cp in copies:
            cp.start()
        for cp in copies:
            cp.wait()

    def out_of(a):
        shape = (a.shape[0],) + tuple(a.shape[2:]) if other_half else a.shape
        return jax.ShapeDtypeStruct(shape, a.dtype)

    anyspace = pl.BlockSpec(memory_space=pl.ANY)
    return pl.pallas_call(
        body, name=name, out_shape=tuple(out_of(a) for a in srcs),
        in_specs=[anyspace] * na, out_specs=(anyspace,) * na,
        scratch_shapes=[pltpu.SemaphoreType.DMA((na,)), pltpu.SemaphoreType.DMA((na,))],
        compiler_params=_cparams(),
    )(*srcs)


def _mod_cols(c_all, w, b):
    n = w.shape[1]

    def body(c_ref, w_ref, b_ref, o_ref):
        o_ref[...] = _dot(c_ref[...], w_ref[...], HI) + b_ref[...]

    return pl.pallas_call(
        body, name="mod_cols", out_shape=jax.ShapeDtypeStruct((8, n), F32),
        compiler_params=_cparams())(c_all, w, b)


def _grad_w_ada(c_all, dmod_cols):
    n = dmod_cols.shape[1]

    def body(c_ref, d_ref, o_ref):
        o_ref[...] = _dot_tn(c_ref[...], d_ref[...], HI)

    return pl.pallas_call(
        body, name="grad_w_ada", out_shape=jax.ShapeDtypeStruct((D_MODEL, n), F32),
        compiler_params=_cparams())(c_all, dmod_cols)


def _prenorm_proj(x, gs, shift, wqkv_t, wrest_t):
    s = x.shape[0]
    tm = min(TM_PROJ, s)
    nq, nr = wqkv_t.shape[0], wrest_t.shape[0]

    def body(x_ref, gs_ref, sh_ref, wq_ref, wr_ref, h_ref, qkv_ref, rest_ref):
        xv = x_ref[...]
        r = lax.rsqrt(jnp.mean(xv * xv, axis=-1, keepdims=True) + EPS)
        h = (xv * r * gs_ref[...] + sh_ref[...]).astype(BF16)
        h_ref[...] = h
        qkv_ref[...] = _dot_nt(h, wq_ref[...]).astype(BF16)
        rest_ref[...] = _dot_nt(h, wr_ref[...])

    def rows(width):
        return pl.BlockSpec((tm, width), lambda i: (i, 0))

    return pl.pallas_call(
        body, name="prenorm_proj", grid=(s // tm,),
        in_specs=[rows(D_MODEL), _const((1, D_MODEL)), _const((1, D_MODEL)), _const((nq, D_MODEL)),
                  _const((nr, D_MODEL))],
        out_specs=(rows(D_MODEL), rows(nq), rows(nr)),
        out_shape=(jax.ShapeDtypeStruct((s, D_MODEL), BF16), jax.ShapeDtypeStruct((s, nq), BF16),
                   jax.ShapeDtypeStruct((s, nr), F32)),
        compiler_params=_cparams(("parallel",)))(x, gs, shift, wqkv_t, wrest_t)


def _grad_w_rows(h, ds):
    s = h.shape[0]
    tk = min(TK_ACC, s)
    nd = len(ds)
    widths = [d.shape[1] for d in ds]

    def body(*refs):
        h_ref, d_refs = refs[0], refs[1:1 + nd]
        out_refs, accs = refs[1 + nd:1 + 2 * nd], refs[1 + 2 * nd:]
        step = pl.program_id(0)

        @pl.when(step == 0)
        def _():
            for acc in accs:
                acc[...] = jnp.zeros_like(acc)

        hv = h_ref[...]
        for d_ref, acc in zip(d_refs, accs):
            acc[...] += _dot_tn(d_ref[...], hv)

        @pl.when(step == s // tk - 1)
        def _():
            for acc, out in zip(accs, out_refs):
                pltpu.sync_copy(acc, out)

    anyspace = pl.BlockSpec(memory_space=pl.ANY)
    return pl.pallas_call(
        body, name="grad_w_in", grid=(s // tk,),
        in_specs=[pl.BlockSpec((tk, D_MODEL), lambda k: (k, 0))]
                 + [pl.BlockSpec((tk, w), lambda k: (k, 0)) for w in widths],
        out_specs=(anyspace,) * nd,
        out_shape=tuple(jax.ShapeDtypeStruct((w, D_MODEL), F32) for w in widths),
        scratch_shapes=[pltpu.VMEM((w, D_MODEL), F32) for w in widths],
        compiler_params=_cparams(("arbitrary",)))(h, *ds)


BIAS_ONES = 32


def _bias_selectors():
    rows = jnp.arange(LANES)[None, :, None]
    cols = jnp.arange(LANES)[None, None, :]
    term = jnp.arange(3)[:, None, None]
    return ((rows < HEADS) & (cols == 3 * rows + term)).astype(F32)


def _fcum(rest, bf128, selk):
    s = rest.shape[0]
    tb = min(TB_CUM, s)

    def body(fz_ref, bf_ref, selk_ref, fpc_ref, ft_ref, kb_ref, qb_ref, carry_ref):
        @pl.when(pl.program_id(0) == 0)
        def _():
            carry_ref[...] = jnp.zeros_like(carry_ref)

        z = fz_ref[...] + bf_ref[...]
        logf = jnp.minimum(z, 0.0) - jnp.log(1.0 + jnp.exp(-jnp.abs(z)))
        r = lax.broadcasted_iota(jnp.int32, (tb, tb), 0)
        c = lax.broadcasted_iota(jnp.int32, (tb, tb), 1)
        tri = (c <= r).astype(F32)
        f = _dot_sel(tri, logf) + carry_ref[0:1, :]
        carry_ref[0:1, :] = f[tb - 1:tb, :]
        for pp in range(4):
            fpc_ref[:, pp * LANES:(pp + 1) * LANES] = f if pp == 0 else pltpu.roll(f, LANES - 2 * pp, 1)
        ft_ref[...] = jnp.transpose(f)[0:HEADS, :]
        lane = lax.broadcasted_iota(jnp.int32, (tb, LANES), 1)
        ones = ((lane >= BIAS_ONES) & (lane < BIAS_ONES + 3 * HEADS)).astype(F32)
        terms = sum(_dot(part, selk_ref[j].astype(BF16)) for j, part in enumerate(_split3(-f)))
        kb_ref[...] = (terms + ones).astype(BF16)
        qb_ref[...] = ((lane < 3 * HEADS).astype(F32) - pltpu.roll(terms, BIAS_ONES, 1)).astype(BF16)

    return pl.pallas_call(
        body, name="forget_cumsum", grid=(s // tb,),
        in_specs=[pl.BlockSpec((tb, LANES), lambda i: (i, R_F // LANES)), _const((1, LANES)), _const((3, LANES, LANES))],
        out_specs=(pl.BlockSpec((tb, 4 * LANES), lambda i: (i, 0)), pl.BlockSpec((HEADS, tb), lambda i: (0, i)),
                   pl.BlockSpec((tb, LANES), lambda i: (i, 0)), pl.BlockSpec((tb, LANES), lambda i: (i, 0))),
        out_shape=(jax.ShapeDtypeStruct((s, 4 * LANES), F32), jax.ShapeDtypeStruct((HEADS, s), F32),
                   jax.ShapeDtypeStruct((s, LANES), BF16), jax.ShapeDtypeStruct((s, LANES), BF16)),
        scratch_shapes=[pltpu.VMEM((8, LANES), F32)],
        compiler_params=_cparams(("arbitrary",)))(rest, bf128, selk)


def _dfcum(dfk, dfq, rest, bf128):
    s = rest.shape[0]
    tb = min(TB_CUM, s)
    nb = s // tb

    def body(dk_ref, dq_ref, fz_ref, bf_ref, df_ref, dbf_ref, carry_ref):
        @pl.when(pl.program_id(0) == 0)
        def _():
            carry_ref[...] = jnp.zeros_like(carry_ref)
            dbf_ref[...] = jnp.zeros_like(dbf_ref)

        pair_lanes = lax.broadcasted_iota(jnp.int32, (tb, LANES), 1) < 2
        d = jnp.zeros((tb, LANES), F32)
        for pp in range(4):
            tile = dk_ref[:, pp * LANES:(pp + 1) * LANES] + dq_ref[:, pp * LANES:(pp + 1) * LANES]
            tile = jnp.where(pair_lanes, tile, 0.0)
            d = d + (tile if pp == 0 else pltpu.roll(tile, 2 * pp, 1))
        r = lax.broadcasted_iota(jnp.int32, (tb, tb), 0)
        c = lax.broadcasted_iota(jnp.int32, (tb, tb), 1)
        triu = (c >= r).astype(F32)
        dlogf = _dot_sel(triu, d) + carry_ref[0:1, :]
        carry_ref[0:1, :] = dlogf[0:1, :]
        z = fz_ref[...] + bf_ref[...]
        df = dlogf * (1.0 / (1.0 + jnp.exp(z)))
        df_ref[...] = df.astype(BF16)
        dbf_ref[0:1, :] += jnp.sum(df, axis=0, keepdims=True)

    return pl.pallas_call(
        body, name="forget_grad", grid=(nb,),
        in_specs=[pl.BlockSpec((tb, 4 * LANES), lambda i: (nb - 1 - i, 0)),
                  pl.BlockSpec((tb, 4 * LANES), lambda i: (nb - 1 - i, 0)),
                  pl.BlockSpec((tb, LANES), lambda i: (nb - 1 - i, R_F // LANES)),
                  _const((1, LANES))],
        out_specs=(pl.BlockSpec((tb, LANES), lambda i: (nb - 1 - i, 0)), pl.BlockSpec((8, LANES), lambda i: (0, 0))),
        out_shape=(jax.ShapeDtypeStruct((s, LANES), BF16), jax.ShapeDtypeStruct((8, LANES), F32)),
        scratch_shapes=[pltpu.VMEM((8, LANES), F32)],
        compiler_params=_cparams(("arbitrary",)))(dfk, dfq, rest, bf128)


def _scaled(q):
    return (q.astype(F32) * (HEAD_DIM ** -0.5)).astype(BF16)


def _attn_fwd(qkv, kbias, qbias):
    s = qkv.shape[0]
    t = min(T_ATT, s)
    n = s // t
    ch = min(ATT_CHUNK, t)
    wide = 2 * LANES
    pairs = ATT_PAIRS
    width = pairs * LANES
    groups = 4 // pairs

    def body(q_ref, k_ref, v_ref, kb_ref, qb_ref, o_ref, lse_ref, s_scr, p_scr, m_scr, a_scr, acc_scr):
        i = pl.program_id(1)
        g = pl.program_id(0)
        lane = lax.broadcasted_iota(jnp.int32, (t, LANES), 1)
        first = lane < HEAD_DIM
        ones_col = ((lane == 0).astype(BF16), (lane == 1).astype(BF16))
        m_scr[...] = jnp.full(m_scr.shape, NEG, F32)
        acc_scr[...] = jnp.zeros_like(acc_scr)
        qm = []
        qbb = qb_ref[...]
        for pp in range(pairs):
            q = _scaled(q_ref[:, pp * LANES:(pp + 1) * LANES])
            zq = jnp.zeros_like(q)
            for hh in range(2):
                lo3 = 3 * (2 * (g * pairs + pp) + hh)
                own = ((lane >= lo3) & (lane < lo3 + 3)) | ((lane >= BIAS_ONES + lo3) & (lane < BIAS_ONES + lo3 + 3))
                qh = jnp.where(first, q, zq) if hh == 0 else jnp.where(first, zq, q)
                qm.append(jnp.concatenate([qh, jnp.where(own, qbb, zq)], axis=1))

        def step(j, masked):
            r0 = pl.multiple_of(j * t, t)
            vaug = []
            kbias_blk = kb_ref[pl.ds(r0, t), :]
            for pp in range(pairs):
                kb = jnp.concatenate([k_ref[pl.ds(r0, t), pp * LANES:(pp + 1) * LANES], kbias_blk], axis=1)
                vb = v_ref[pl.ds(r0, t), pp * LANES:(pp + 1) * LANES]
                zv = jnp.zeros_like(vb)
                vaug += [jnp.concatenate([jnp.where(first, vb, zv), ones_col[0]], axis=1),
                         jnp.concatenate([jnp.where(first, zv, vb), ones_col[1]], axis=1)]
                for hh in range(2):
                    s_scr[2 * pp + hh] = _dot_nt(qm[2 * pp + hh], kb)
            pv = []
            for hd in range(2 * pairs):
                for c in range(t // ch):
                    rows = pl.ds(c * ch, ch)
                    hi = min(t, (c * ch // LANES + 1) * LANES) if masked else t
                    m_old = m_scr[hd, rows, :]
                    if masked:
                        lo = hi - LANES
                        rq = c * ch + lax.broadcasted_iota(jnp.int32, (ch, LANES), 0)
                        ck = lo + lax.broadcasted_iota(jnp.int32, (ch, LANES), 1)
                        diag = jnp.where(ck <= rq, s_scr[hd, rows, lo:hi], NEG)
                        top = jnp.max(diag, axis=1, keepdims=True)
                        if lo > 0:
                            seen = s_scr[hd, rows, 0:lo]
                            top = jnp.maximum(top, jnp.max(seen, axis=1, keepdims=True))
                        m_new = jnp.maximum(m_old, top)
                        if lo > 0:
                            p_scr[hd, rows, 0:lo] = jnp.exp(seen - m_new).astype(BF16)
                        p_scr[hd, rows, lo:hi] = jnp.exp(diag - m_new).astype(BF16)
                    else:
                        sc = s_scr[hd, rows, :]
                        m_new = jnp.maximum(m_old, jnp.max(sc, axis=1, keepdims=True))
                        p_scr[hd, rows, :] = jnp.exp(sc - m_new).astype(BF16)
                    if hi < t:
                        p_scr[hd, rows, hi:t] = jnp.zeros((ch, t - hi), BF16)
                    a_scr[hd, rows, :] = jnp.exp(m_old - m_new)
                    m_scr[hd, rows, :] = m_new
                pv.append(_dot(p_scr[hd], vaug[hd]))
            for pp in range(pairs):
                a0, a1 = a_scr[2 * pp], a_scr[2 * pp + 1]
                alpha = jnp.concatenate([jnp.where(first, a0, a1), jnp.where(lane == 0, a0, a1)], axis=1)
                acc_scr[pp] = acc_scr[pp] * alpha + pv[2 * pp] + pv[2 * pp + 1]
            return 0

        lax.fori_loop(0, i, lambda j, _: step(j, False), 0)
        step(i, True)
        lse = jnp.zeros((t, LANES), F32)
        for pp in range(pairs):
            l0 = acc_scr[pp, :, LANES:LANES + 1]
            l1 = acc_scr[pp, :, LANES + 1:LANES + 2]
            o_ref[:, pp * LANES:(pp + 1) * LANES] = acc_scr[pp, :, 0:LANES] * jnp.where(first, 1.0 / l0, 1.0 / l1)
            lse = jnp.where(lane == 2 * pp, m_scr[2 * pp] + jnp.log(l0), lse)
            lse = jnp.where(lane == 2 * pp + 1, m_scr[2 * pp + 1] + jnp.log(l1), lse)
        lse_ref[...] = lse

    blk = pl.BlockSpec((t, width), lambda g, i: (i, g))
    return pl.pallas_call(
        body, name="attn_fwd", grid=(groups, n),
        in_specs=[blk,
                  pl.BlockSpec((s, width), lambda g, i: (0, groups + g)),
                  pl.BlockSpec((s, width), lambda g, i: (0, 2 * groups + g)),
                  pl.BlockSpec((s, LANES), lambda g, i: (0, 0)),
                  pl.BlockSpec((t, LANES), lambda g, i: (i, 0))],
        out_specs=(blk, pl.BlockSpec((t, LANES), lambda g, i: (i, g))),
        out_shape=(jax.ShapeDtypeStruct((s, FOX_W), F32), jax.ShapeDtypeStruct((s, groups * LANES), F32)),
        scratch_shapes=[pltpu.VMEM((2 * pairs, t, t), F32), pltpu.VMEM((2 * pairs, t, t), BF16),
                        pltpu.VMEM((2 * pairs, t, 1), F32), pltpu.VMEM((2 * pairs, t, 1), F32),
                        pltpu.VMEM((pairs, t, wide), F32)],
        compiler_params=_cparams(("parallel", "arbitrary")))(qkv, qkv, qkv, kbias, qbias)


def _attn_bwd(qkv, do, lse5, dlt5, frow5, fpc):
    s = qkv.shape[0]
    t = min(T_ATT, s)
    n = s // t
    wide = 2 * LANES

    ch = min(ATT_CHUNK, t)

    def body(q_ref, do_ref, k_ref, v_ref, lse_ref, dl_ref, fr_ref, fc_ref,
             dq_ref, dk_ref, dv_ref, dfk_ref, dfq_ref, dq_acc, st_scr, dp_scr, pt_scr, ds_scr, dk_acc, dv_acc, fk_scr):
        j = pl.program_id(1)

        @pl.when(j == 0)
        def _():
            dq_acc[...] = jnp.zeros_like(dq_acc)

        dk_acc[...] = jnp.zeros_like(dk_acc)
        dv_acc[...] = jnp.zeros_like(dv_acc)
        lane = lax.broadcasted_iota(jnp.int32, (t, LANES), 1)
        first = lane < HEAD_DIM
        ones_col = ((lane == 0).astype(BF16), (lane == 1).astype(BF16))
        kb = k_ref[...]
        vb = v_ref[...]
        zk = jnp.zeros_like(kb)
        kaug = (jnp.concatenate([jnp.where(first, kb, zk), ones_col[0]], axis=1),
                jnp.concatenate([jnp.where(first, zk, kb), ones_col[1]], axis=1))
        fk_scr[0] = fc_ref[:, 0:1]
        fk_scr[1] = fc_ref[:, 1:2]

        def step(blocks, masked):
            chains = []
            for bi, i in enumerate(blocks):
                r0 = pl.multiple_of(i * t, t)
                qb = _scaled(q_ref[pl.ds(r0, t), :])
                dob = do_ref[pl.ds(r0, t), :]
                zq = jnp.zeros_like(qb)
                qm = (jnp.where(first, qb, zq), jnp.where(first, zq, qb))
                dom = (jnp.where(first, dob, zq), jnp.where(first, zq, dob))
                for hh in range(2):
                    st_scr[2 * bi + hh] = _dot_nt(kb, qm[hh])
                    dp_scr[2 * bi + hh] = _dot_nt(vb, dom[hh])
                    chains.append((i, hh, qm[hh], dom[hh]))
            dq_add = [jnp.zeros((t, wide), F32) for _ in blocks]
            for cn, (i, hh, qmh, domh) in enumerate(chains):
                bias = fr_ref[0, hh, i] - lse_ref[0, hh, i]
                dl = dl_ref[0, hh, i]
                for c in range(t // ch):
                    rows = pl.ds(c * ch, ch)
                    lo = c * ch // LANES * LANES if masked else 0
                    st = st_scr[cn, rows, lo:t] + (bias[:, lo:t] - fk_scr[hh, rows, :])
                    if masked:
                        rk = c * ch + lax.broadcasted_iota(jnp.int32, (ch, t - lo), 0)
                        cq = lo + lax.broadcasted_iota(jnp.int32, (ch, t - lo), 1)
                        st = jnp.where(rk <= cq, st, NEG)
                    pt = jnp.exp(st)
                    pt_scr[cn, rows, lo:t] = pt.astype(BF16)
                    ds_scr[cn, rows, lo:t] = (pt * (dp_scr[cn, rows, lo:t] - dl[:, lo:t])).astype(BF16)
                    if lo > 0:
                        pt_scr[cn, rows, 0:lo] = jnp.zeros((ch, lo), BF16)
                        ds_scr[cn, rows, 0:lo] = jnp.zeros((ch, lo), BF16)
                dsb = ds_scr[cn]
                dv_acc[...] += _dot(pt_scr[cn], domh)
                dk_acc[...] += _dot(dsb, jnp.concatenate([qmh, ones_col[hh]], axis=1))
                dq_add[cn // 2] = dq_add[cn // 2] + _dot_tn(dsb, kaug[hh])
            for bi, i in enumerate(blocks):
                dq_acc[pl.ds(pl.multiple_of(i * t, t), t), :] += dq_add[bi]
            return 0

        step([j], True)
        odd = (n - 1 - j) % 2
        lax.fori_loop(0, odd, lambda _, carry: step([j + 1], False), 0)
        first_pair = j + 1 + odd
        lax.fori_loop(0, (n - first_pair) // 2,
                      lambda p, _: step([first_pair + 2 * p, first_pair + 2 * p + 1], False), 0)
        dk_ref[...] = dk_acc[:, 0:LANES].astype(BF16)
        dv_ref[...] = dv_acc[...].astype(BF16)
        dfk_ref[...] = -dk_acc[:, LANES:wide]

        @pl.when(j == n - 1)
        def _():
            dq_ref[...] = (dq_acc[:, 0:LANES] * (HEAD_DIM ** -0.5)).astype(BF16)
            dfq_ref[...] = dq_acc[:, LANES:wide]

    stat = pl.BlockSpec((1, 2, n, 1, t), lambda h, j: (h, 0, 0, 0, 0))
    blk = pl.BlockSpec((t, LANES), lambda h, j: (j, h))
    full = pl.BlockSpec((s, LANES), lambda h, j: (0, h))
    return pl.pallas_call(
        body, name="attn_bwd", grid=(4, n),
        in_specs=[full, full,
                  pl.BlockSpec((t, LANES), lambda h, j: (j, 4 + h)),
                  pl.BlockSpec((t, LANES), lambda h, j: (j, 8 + h)),
                  stat, stat, stat, blk],
        out_specs=(full, blk, blk, blk, full),
        out_shape=(jax.ShapeDtypeStruct((s, FOX_W), BF16), jax.ShapeDtypeStruct((s, FOX_W), BF16),
                   jax.ShapeDtypeStruct((s, FOX_W), BF16), jax.ShapeDtypeStruct((s, 4 * LANES), F32),
                   jax.ShapeDtypeStruct((s, 4 * LANES), F32)),
        scratch_shapes=[pltpu.VMEM((s, wide), F32), pltpu.VMEM((4, t, t), F32), pltpu.VMEM((4, t, t), F32),
                        pltpu.VMEM((4, t, t), BF16), pltpu.VMEM((4, t, t), BF16), pltpu.VMEM((t, wide), F32),
                        pltpu.VMEM((t, LANES), F32), pltpu.VMEM((2, t, 1), F32)],
        compiler_params=_cparams(("parallel", "arbitrary")))(qkv, do, qkv, qkv, lse5, dlt5, frow5, fpc)


def _ssm_block_params(a_re, a_im, log_dt, b_re, b_im):
    dt = jnp.exp(log_dt)[:, None]
    mag = jnp.exp(a_re * dt)
    ar = mag * jnp.cos(a_im * dt)
    ai = mag * jnp.sin(a_im * dt)
    den = a_re * a_re + a_im * a_im
    nr = ar - 1.0
    cr = (nr * a_re + ai * a_im) / den
    ci = (ai * a_re - nr * a_im) / den
    bbr = cr[:, :, None] * b_re - ci[:, :, None] * b_im
    bbi = cr[:, :, None] * b_im + ci[:, :, None] * b_re
    return ar, ai, bbr, bbi


def _block_diag(blocks):
    g, r, c = blocks.shape
    eye = jnp.eye(g, dtype=blocks.dtype)
    return (blocks[:, :, None, :] * eye[:, None, :, None]).reshape(g * r, g * c)


def _scan_consts(a_re, a_im, log_dt, seg, reverse):
    dt = jnp.exp(log_dt)[:, None]
    lr = (a_re * dt).reshape(1, NSTATE)
    li = (a_im * dt).reshape(1, NSTATE)
    if reverse:
        li = -li
    rows = jnp.arange(8, dtype=F32)[:, None]

    def power(k):
        mag = jnp.exp(k * lr)
        return mag * jnp.cos(k * li), mag * jnp.sin(k * li)

    tiles = list(power(1.0))
    for k in (1, 2, 4):
        keep = (rows < 8 - k) if reverse else (rows >= k)
        pr, pi_ = power(float(k * seg))
        tiles += [jnp.where(keep, pr, 0.0), jnp.where(keep, pi_, 0.0)]
    tiles += list(power(seg * ((8.0 - rows) if reverse else (rows + 1.0))))
    tiles = jnp.stack([jnp.broadcast_to(tl, (8, NSTATE)) for tl in tiles])
    steps = jnp.arange(seg, dtype=F32)[:, None]
    table = jnp.stack([jnp.broadcast_to(p[:, None, :], (seg, 8, NSTATE))
                       for p in power((seg - steps) if reverse else (steps + 1.0))])
    return tiles, table


_SCAN_W = 1024
_HALF_W = S5_W // 2
_HALF_S = NSTATE // 2


def _compact_diag(blocks_re, blocks_im):
    hg = GROUPS // 2
    return jnp.concatenate([_block_diag(b[h * hg:(h + 1) * hg]) for b in (blocks_re, blocks_im) for h in range(2)],
                           axis=1)


def _half_expand(v, w_ref, out_ref):
    for half in range(2):
        vh = v[:, half * _HALF_W:(half + 1) * _HALF_W]
        for part in range(2):
            c0 = part * NSTATE + half * _HALF_S
            out_ref[:, c0:c0 + _HALF_S] = _dot(vh, w_ref[:, c0:c0 + _HALF_S])


def _half_contract(x_ref, w_ref, half):
    out = None
    for part in range(2):
        r0 = part * NSTATE + half * _HALF_S
        term = _dot_nt(x_ref[:, r0:r0 + _HALF_S].astype(BF16), w_ref[:, r0:r0 + _HALF_S])
        out = term if out is None else out + term
    return out


def _half_outer(v, x_ref, acc_ref):
    for half in range(2):
        vh = v[:, half * _HALF_W:(half + 1) * _HALF_W]
        for part in range(2):
            c0 = part * NSTATE + half * _HALF_S
            acc_ref[:, c0:c0 + _HALF_S] += _dot_tn(vh, x_ref[:, c0:c0 + _HALF_S].astype(BF16))


def _segment_perm(tb):
    seg = tb // 8
    row = lax.broadcasted_iota(jnp.int32, (tb, tb), 0)
    col = lax.broadcasted_iota(jnp.int32, (tb, tb), 1)
    perm = (col == (row % 8) * seg + row // 8).astype(BF16)
    back = (col == (row % seg) * 8 + row // seg).astype(BF16)
    return perm, back


def _segment_ends(re, im, cf_ref, cb_ref, cr, ci, reverse):
    for n_, k in enumerate((1, 2, 4)):
        kr = cf_ref[2 + 2 * n_, :, cr]
        ki = cf_ref[3 + 2 * n_, :, cr]
        sr = pltpu.roll(re, 8 - k if reverse else k, 0)
        si = pltpu.roll(im, 8 - k if reverse else k, 0)
        re, im = re + kr * sr - ki * si, im + kr * si + ki * sr
    cbr, cbi = cb_ref[:, cr], cb_ref[:, ci]
    pr, pi_ = cf_ref[8, :, cr], cf_ref[9, :, cr]
    re, im = re + pr * cbr - pi_ * cbi, im + pr * cbi + pi_ * cbr
    edge = lax.broadcasted_iota(jnp.int32, re.shape, 0) == (7 if reverse else 0)
    in_r = jnp.where(edge, cbr, pltpu.roll(re, 7 if reverse else 1, 0))
    in_i = jnp.where(edge, cbi, pltpu.roll(im, 7 if reverse else 1, 0))
    out = slice(0, 1) if reverse else slice(7, 8)
    cb_ref[:, cr] = jnp.broadcast_to(re[out, :], re.shape)
    cb_ref[:, ci] = jnp.broadcast_to(im[out, :], im.shape)
    return in_r, in_i


def _ssm_fwd(rest, bd, cd, consts, table):
    s = rest.shape[0]
    tb = min(TB_SSM, s)
    seg = tb // 8
    ns2 = 2 * NSTATE

    def body(u_ref, bd_ref, cd_ref, cf_ref, tab_ref, y_ref, x_ref, cb_ref):
        @pl.when(pl.program_id(0) == 0)
        def _():
            cb_ref[...] = jnp.zeros_like(cb_ref)

        perm, back = _segment_perm(tb)
        _half_expand(_dot(perm, u_ref[...].astype(BF16)).astype(BF16), bd_ref, x_ref)
        for cc in range(NSTATE // _SCAN_W):
            cr = pl.ds(cc * _SCAN_W, _SCAN_W)
            ci = pl.ds(NSTATE + cc * _SCAN_W, _SCAN_W)
            ar, ai = cf_ref[0, :, cr], cf_ref[1, :, cr]

            def local(i, carry, cr=cr, ci=ci, ar=ar, ai=ai):
                re, im = carry
                rows = pl.ds(pl.multiple_of(i * 8, 8), 8)
                re, im = ar * re - ai * im + x_ref[rows, cr], ar * im + ai * re + x_ref[rows, ci]
                x_ref[rows, cr] = re
                x_ref[rows, ci] = im
                return re, im

            zero = jnp.zeros((8, _SCAN_W), F32)
            re, im = lax.fori_loop(0, seg, local, (zero, zero))
            in_r, in_i = _segment_ends(re, im, cf_ref, cb_ref, cr, ci, False)

            def fix(i, _, cr=cr, ci=ci, in_r=in_r, in_i=in_i):
                rows = pl.ds(pl.multiple_of(i * 8, 8), 8)
                tr, ti = tab_ref[0, i, :, cr], tab_ref[1, i, :, cr]
                x_ref[rows, cr] += tr * in_r - ti * in_i
                x_ref[rows, ci] += tr * in_i + ti * in_r
                return 0

            lax.fori_loop(0, seg, fix, 0)
        y_p = jnp.concatenate([_half_contract(x_ref, cd_ref, half) for half in range(2)], axis=1)
        y_ref[...] = _dot_sel(back, y_p, terms=2)

    return pl.pallas_call(
        body, name="ssm_fwd", grid=(s // tb,),
        in_specs=[pl.BlockSpec((tb, S5_W), lambda i: (i, R_U // S5_W)), _const((_HALF_W, ns2)), _const((_HALF_W, ns2)),
                  _const((10, 8, NSTATE)), _const((2, seg, 8, NSTATE))],
        out_specs=(pl.BlockSpec((tb, S5_W), lambda i: (i, 0)), pl.BlockSpec((tb, ns2), lambda i: (i, 0))),
        out_shape=(jax.ShapeDtypeStruct((s, S5_W), F32), jax.ShapeDtypeStruct((s, ns2), F32)),
        scratch_shapes=[pltpu.VMEM((8, ns2), F32)],
        compiler_params=_cparams(("arbitrary",)))(rest, bd, cd, consts, table)


def _ssm_bwd(dys, xs, rest, bd, cd, consts, table, dskip):
    s = dys.shape[0]
    tb = min(TB_SSM, s)
    seg = tb // 8
    nb = s // tb
    ns2 = 2 * NSTATE

    def body(dy_ref, x_ref, u_ref, bd_ref, cd_ref, cf_ref, tab_ref, dsk_ref, du_ref, gb_ref, gc_ref, da_ref,
             g_ref, cb_ref, acc_b, acc_c):
        step = pl.program_id(0)

        @pl.when(step == 0)
        def _():
            cb_ref[...] = jnp.zeros_like(cb_ref)
            acc_b[...] = jnp.zeros_like(acc_b)
            acc_c[...] = jnp.zeros_like(acc_c)
            da_ref[...] = jnp.zeros_like(da_ref)

        perm, back = _segment_perm(tb)
        dy = dy_ref[...]
        dy_p = _dot(perm, dy.astype(BF16)).astype(BF16)
        u_p = _dot(perm, u_ref[...].astype(BF16)).astype(BF16)
        _half_expand(dy_p, cd_ref, g_ref)
        for cc in range(NSTATE // _SCAN_W):
            cr = pl.ds(cc * _SCAN_W, _SCAN_W)
            ci = pl.ds(NSTATE + cc * _SCAN_W, _SCAN_W)
            ar, ai = cf_ref[0, :, cr], cf_ref[1, :, cr]

            def local(ii, carry, cr=cr, ci=ci, ar=ar, ai=ai):
                re, im = carry
                rows = pl.ds(pl.multiple_of((seg - 1 - ii) * 8, 8), 8)
                re, im = ar * re - ai * im + g_ref[rows, cr], ar * im + ai * re + g_ref[rows, ci]
                g_ref[rows, cr] = re
                g_ref[rows, ci] = im
                return re, im

            zero = jnp.zeros((8, _SCAN_W), F32)
            re, im = lax.fori_loop(0, seg, local, (zero, zero))
            in_r, in_i = _segment_ends(re, im, cf_ref, cb_ref, cr, ci, True)

            def fix(ii, carry, cr=cr, ci=ci, in_r=in_r, in_i=in_i):
                nr, ni, acr, aci = carry
                i = seg - 1 - ii
                rows = pl.ds(pl.multiple_of(i * 8, 8), 8)
                tr, ti = tab_ref[0, i, :, cr], tab_ref[1, i, :, cr]
                gr = g_ref[rows, cr] + tr * in_r - ti * in_i
                gi = g_ref[rows, ci] + tr * in_i + ti * in_r
                g_ref[rows, cr] = gr
                g_ref[rows, ci] = gi
                xr, xi = x_ref[rows, cr], x_ref[rows, ci]
                return gr, gi, acr + nr * xr + ni * xi, aci + ni * xr - nr * xi

            _, _, acr, aci = lax.fori_loop(0, seg, fix, (in_r, in_i, zero, zero))
            da_ref[:, cr] += acr
            da_ref[:, ci] += aci
        du_p = jnp.concatenate([_half_contract(g_ref, bd_ref, half) for half in range(2)], axis=1)
        du_ref[...] = (_dot_sel(back, du_p, terms=2) + dy * dsk_ref[...]).astype(BF16)
        _half_outer(u_p, g_ref, acc_b)
        _half_outer(dy_p, x_ref, acc_c)

        @pl.when(step == nb - 1)
        def _():
            for g in range(GROUPS):
                src = slice((g % (GROUPS // 2)) * GCH, (g % (GROUPS // 2) + 1) * GCH)
                dst = slice(g * GCH, (g + 1) * GCH)
                for part in range(2):
                    cols = slice(part * NSTATE + g * STATE, part * NSTATE + (g + 1) * STATE)
                    gb_ref[dst, part * STATE:(part + 1) * STATE] = acc_b[src, cols]
                    gc_ref[dst, part * STATE:(part + 1) * STATE] = acc_c[src, cols]

    rev = lambda i: (nb - 1 - i, 0)
    small = pl.BlockSpec((S5_W, 2 * STATE), lambda i: (0, 0))
    return pl.pallas_call(
        body, name="ssm_bwd", grid=(nb,),
        in_specs=[pl.BlockSpec((tb, S5_W), rev), pl.BlockSpec((tb, ns2), rev),
                  pl.BlockSpec((tb, S5_W), lambda i: (nb - 1 - i, R_U // S5_W)),
                  _const((_HALF_W, ns2)), _const((_HALF_W, ns2)), _const((10, 8, NSTATE)), _const((2, seg, 8, NSTATE)),
                  _const((1, S5_W))],
        out_specs=(pl.BlockSpec((tb, S5_W), rev), small, small, pl.BlockSpec((8, ns2), lambda i: (0, 0))),
        out_shape=(jax.ShapeDtypeStruct((s, S5_W), BF16), jax.ShapeDtypeStruct((S5_W, 2 * STATE), F32),
                   jax.ShapeDtypeStruct((S5_W, 2 * STATE), F32), jax.ShapeDtypeStruct((8, ns2), F32)),
        scratch_shapes=[pltpu.VMEM((tb, ns2), F32), pltpu.VMEM((8, ns2), F32),
                        pltpu.VMEM((_HALF_W, ns2), F32), pltpu.VMEM((_HALF_W, ns2), F32)],
        compiler_params=_cparams(("arbitrary",)))(dys, xs, rest, bd, cd, consts, table, dskip)


_GELU_C = math.sqrt(2.0 / math.pi)
_GELU_A = 0.044715


def _mid(o, rest, ys0, x, tgt, w, vec, hsel):
    s = o.shape[0]
    tm = min(TM, s)
    nsteps = s // tm
    half = FOX_W

    def body(o_ref, ga_ref, gb_ref, za_ref, u_ref, zb_ref, ys0_ref, x_ref, t_ref,
             wglu_ref, wua_ref, wub_ref, wout_ref, vec_ref, hsel_ref,
             dx2_ref, dga_ref, dgb_ref, do_ref, dza_ref, dzb_ref, dys_ref, dlt_ref,
             gout_hbm, gua_hbm, gub_hbm, gglu_hbm, vout_ref,
             a_out, a_ua, a_ub, a_glu):
        step = pl.program_id(0)

        @pl.when(step == 0)
        def _():
            a_out[...] = jnp.zeros_like(a_out)
            a_ua[...] = jnp.zeros_like(a_ua)
            a_ub[...] = jnp.zeros_like(a_ub)
            a_glu[...] = jnp.zeros_like(a_glu)
            vout_ref[...] = jnp.zeros_like(vout_ref)

        gate = vec_ref[0:1, :]
        gfin = vec_ref[1:2, :]
        dsk = vec_ref[2:3, 0:half]
        bglu = vec_ref[2:3, half:2 * half]

        o_v = o_ref[...]
        za = za_ref[...]
        sza = _sigmoid(za)
        silu_za = za * sza
        ya_b = (o_v * silu_za).astype(BF16)
        u_v = u_ref[...]
        ys = ys0_ref[...] + dsk * u_v
        inner = _GELU_C * (ys + _GELU_A * ys * ys * ys)
        th = jnp.tanh(inner)
        yg = 0.5 * ys * (1.0 + th)
        yg_b = yg.astype(BF16)
        st = _sigmoid(_dot(yg_b, wglu_ref[...]) + bglu)
        yb1 = yg * st
        zb = zb_ref[...]
        szb = _sigmoid(zb)
        silu_zb = zb * szb
        yb_b = (yb1 * silu_zb).astype(BF16)
        ua = _dot(ya_b, wua_ref[...])
        ub = _dot(yb_b, wub_ref[...])
        sga = _sigmoid(ga_ref[...])
        sgb = _sigmoid(gb_ref[...])
        merged_b = (sga * ua + sgb * ub).astype(BF16)
        mo = _dot(merged_b, wout_ref[...])
        x2 = x_ref[...] + gate * mo
        r2 = lax.rsqrt(jnp.mean(x2 * x2, axis=-1, keepdims=True) + EPS)
        x2n = x2 * r2
        diff = x2n * gfin - t_ref[...]
        loss = 0.5 * jnp.sum(jnp.mean(diff * diff, axis=-1, keepdims=True), axis=0, keepdims=True)
        dy = diff * (1.0 / D_MODEL)
        dx2n = dy * gfin
        dx2 = r2 * (dx2n - x2n * jnp.mean(dx2n * x2n, axis=-1, keepdims=True))
        dx2_ref[...] = dx2
        vout_ref[0:1, :] += jnp.sum(dy * x2n, axis=0, keepdims=True)
        vout_ref[1:2, :] += jnp.sum(dx2 * mo, axis=0, keepdims=True)
        vout_ref[3:4, :] += jnp.broadcast_to(loss, (1, D_MODEL))
        dmo_b = (dx2 * gate).astype(BF16)
        dmerged = _dot_nt(dmo_b, wout_ref[...])
        a_out[...] += _dot_tn(merged_b, dmo_b)
        dua_b = (dmerged * sga).astype(BF16)
        dub_b = (dmerged * sgb).astype(BF16)
        dga_ref[...] = (dmerged * ua * sga * (1.0 - sga)).astype(BF16)
        dgb_ref[...] = (dmerged * ub * sgb * (1.0 - sgb)).astype(BF16)
        dya = _dot_nt(dua_b, wua_ref[...])
        dyb = _dot_nt(dub_b, wub_ref[...])
        a_ua[...] += _dot_tn(ya_b, dua_b)
        a_ub[...] += _dot_tn(yb_b, dub_b)
        do_b = (dya * silu_za).astype(BF16)
        do_ref[...] = do_b
        dza_ref[...] = (dya * o_v * (sza * (1.0 + za * (1.0 - sza)))).astype(BF16)
        hsel = hsel_ref[...].astype(BF16)
        dlt_ref[...] = sum(_dot_nt(hsel, part) for part in _split3(do_b.astype(F32) * o_v))
        dyb1 = dyb * silu_zb
        dzb_ref[...] = (dyb * yb1 * (szb * (1.0 + zb * (1.0 - szb)))).astype(BF16)
        dt = dyb1 * yg * st * (1.0 - st)
        dt_b = dt.astype(BF16)
        dyg = dyb1 * st + _dot_nt(dt_b, wglu_ref[...])
        a_glu[...] += _dot_tn(yg_b, dt_b)
        dgelu = 0.5 * (1.0 + th) + 0.5 * ys * (1.0 - th * th) * _GELU_C * (1.0 + 3.0 * _GELU_A * ys * ys)
        dys = dyg * dgelu
        dys_ref[...] = dys
        vout_ref[2:3, 0:half] += jnp.sum(dys * u_v, axis=0, keepdims=True)
        vout_ref[2:3, half:2 * half] += jnp.sum(dt, axis=0, keepdims=True)

        @pl.when(step == nsteps - 1)
        def _():
            pltpu.sync_copy(a_out, gout_hbm)
            pltpu.sync_copy(a_ua, gua_hbm)
            pltpu.sync_copy(a_ub, gub_hbm)
            pltpu.sync_copy(a_glu, gglu_hbm)

    def rows(width, col=0):
        return pl.BlockSpec((tm, width), lambda i, col=col: (i, col))

    anyspace = pl.BlockSpec(memory_space=pl.ANY)
    wshapes = [(S5_W, S5_W), (FOX_W, D_MODEL), (S5_W, D_MODEL), (D_MODEL, D_MODEL)]
    return pl.pallas_call(
        body, name="mid", grid=(nsteps,),
        in_specs=[rows(FOX_W), rows(D_MODEL, R_GA // D_MODEL), rows(D_MODEL, R_GB // D_MODEL),
                  rows(FOX_W, R_ZA // FOX_W), rows(S5_W, R_U // S5_W), rows(S5_W, R_ZB // S5_W),
                  rows(S5_W), rows(D_MODEL), rows(D_MODEL)]
                 + [_const(sh) for sh in wshapes]
                 + [_const((8, D_MODEL)), _const((HEADS, FOX_W))],
        out_specs=(rows(D_MODEL), rows(D_MODEL), rows(D_MODEL), rows(FOX_W), rows(FOX_W), rows(S5_W), rows(S5_W),
                   pl.BlockSpec((HEADS, tm), lambda i: (0, i)),
                   anyspace, anyspace, anyspace, anyspace, pl.BlockSpec((8, D_MODEL), lambda i: (0, 0))),
        out_shape=(jax.ShapeDtypeStruct((s, D_MODEL), F32), jax.ShapeDtypeStruct((s, D_MODEL), BF16),
                   jax.ShapeDtypeStruct((s, D_MODEL), BF16), jax.ShapeDtypeStruct((s, FOX_W), BF16),
                   jax.ShapeDtypeStruct((s, FOX_W), BF16), jax.ShapeDtypeStruct((s, S5_W), BF16),
                   jax.ShapeDtypeStruct((s, S5_W), F32), jax.ShapeDtypeStruct((HEADS, s), F32),
                   jax.ShapeDtypeStruct((D_MODEL, D_MODEL), F32), jax.ShapeDtypeStruct((FOX_W, D_MODEL), F32),
                   jax.ShapeDtypeStruct((S5_W, D_MODEL), F32), jax.ShapeDtypeStruct((S5_W, S5_W), F32),
                   jax.ShapeDtypeStruct((8, D_MODEL), F32)),
        scratch_shapes=[pltpu.VMEM((D_MODEL, D_MODEL), F32), pltpu.VMEM((FOX_W, D_MODEL), F32),
                        pltpu.VMEM((S5_W, D_MODEL), F32), pltpu.VMEM((S5_W, S5_W), F32)],
        compiler_params=_cparams(("arbitrary",)),
    )(o, rest, rest, rest, rest, rest, ys0, x, tgt, *w, vec, hsel)


def _dh(dq, dk, dv, dga, dgb, dza, du, dzb, df, wqkv_t, wrest_t, x, dx2, gs, scatter_srcs):
    s = x.shape[0]
    tm = min(TM_PROJ, s)
    nsteps = s // tm
    na = len(scatter_srcs)

    def body(dq_ref, dk_ref, dv_ref, dga_ref, dgb_ref, dza_ref, du_ref, dzb_ref, df_ref, wq_ref, wr_ref,
             x_ref, dx2_ref, gs_ref, *rest_refs):
        src_refs = rest_refs[:na]
        gx_ref, vout_ref = rest_refs[na:na + 2]
        out_refs = rest_refs[na + 2:2 * na + 2]
        send_sems, recv_sems = rest_refs[2 * na + 2:]
        step = pl.program_id(0)
        cx, cy, cc = lax.axis_index("x"), lax.axis_index("y"), lax.axis_index("c")
        peers = [(1 - cx, cy), (cx, 1 - cy), (1 - cx, 1 - cy)]

        def copy(a, k, px, py, slot):
            return pltpu.make_async_remote_copy(
                src_ref=src_refs[a].at[2 * px + py], dst_ref=out_refs[a].at[slot],
                send_sem=send_sems.at[a * 3 + k], recv_sem=recv_sems.at[a * 3 + k],
                device_id=(px, py, cc), device_id_type=MESH)

        @pl.when(step == 0)
        def _():
            vout_ref[...] = jnp.zeros_like(vout_ref)
            for a in range(na):
                for k, (px, py) in enumerate(peers):
                    copy(a, k, px, py, 2 * cx + cy).start()

        dh = _dot(dq_ref[...], wq_ref[0:512, :])
        dh += _dot(dk_ref[...], wq_ref[512:1024, :])
        dh += _dot(dv_ref[...], wq_ref[1024:1536, :])
        dh += _dot(dga_ref[...], wr_ref[R_GA:R_GB, :])
        dh += _dot(dgb_ref[...], wr_ref[R_GB:R_ZA, :])
        dh += _dot(dza_ref[...], wr_ref[R_ZA:R_U, :])
        dh += _dot(du_ref[...], wr_ref[R_U:R_ZB, :])
        dh += _dot(dzb_ref[...], wr_ref[R_ZB:R_F, :])
        dh += _dot(df_ref[...], wr_ref[R_F:REST_W, :])
        xv = x_ref[...]
        r = lax.rsqrt(jnp.mean(xv * xv, axis=-1, keepdims=True) + EPS)
        xn = xv * r
        dxn = dh * gs_ref[...]
        gx_ref[...] = dx2_ref[...] + r * (dxn - xn * jnp.mean(dxn * xn, axis=-1, keepdims=True))
        vout_ref[0:1, :] += jnp.sum(dh * xn, axis=0, keepdims=True)
        vout_ref[1:2, :] += jnp.sum(dh, axis=0, keepdims=True)

        @pl.when(step == nsteps - 1)
        def _():
            for a in range(na):
                for k, (px, py) in enumerate(peers):
                    copy(a, k, px, py, 2 * px + py).wait_recv()
            for a in range(na):
                for k, (px, py) in enumerate(peers):
                    copy(a, k, px, py, 2 * cx + cy).wait_send()

    def rows(width):
        return pl.BlockSpec((tm, width), lambda i: (i, 0))

    anyspace = pl.BlockSpec(memory_space=pl.ANY)
    return pl.pallas_call(
        body, name="dh", grid=(nsteps,),
        in_specs=[rows(512), rows(512), rows(512), rows(1024), rows(1024), rows(512), rows(512), rows(512), rows(128),
                  _const((1536, D_MODEL)), _const((REST_W, D_MODEL)), rows(D_MODEL), rows(D_MODEL), _const((1, D_MODEL))]
                 + [anyspace] * na,
        out_specs=(rows(D_MODEL), pl.BlockSpec((8, D_MODEL), lambda i: (0, 0))) + (anyspace,) * na,
        out_shape=(jax.ShapeDtypeStruct((s, D_MODEL), F32), jax.ShapeDtypeStruct((8, D_MODEL), F32))
                  + tuple(jax.ShapeDtypeStruct(a.shape, a.dtype) for a in scatter_srcs),
        scratch_shapes=[pltpu.SemaphoreType.DMA((3 * na,)), pltpu.SemaphoreType.DMA((3 * na,))],
        compiler_params=_cparams(("arbitrary",)),
    )(dq, dk, dv, dga, dgb, dza, du, dzb, df, wqkv_t, wrest_t, x, dx2, gs, *scatter_srcs)


def _row_block(rows, mult=8, cap=512):
    if rows <= mult:
        return rows
    padded = -(-rows // mult) * mult
    for cand in range(min(cap, padded) // mult * mult, 0, -mult):
        if padded % cand == 0:
            return cand
    return padded


def _sum4(parts, name):
    rows, cols = parts.shape[1:]
    br = _row_block(rows, 16, 1024)

    def body(p_ref, o_ref):
        acc = p_ref[0].astype(F32)
        for k in range(1, 4):
            acc = acc + p_ref[k].astype(F32)
        o_ref[...] = acc

    return pl.pallas_call(
        body, name=name, grid=(pl.cdiv(rows, br),),
        in_specs=[pl.BlockSpec((4, br, cols), lambda i: (0, i, 0))],
        out_specs=pl.BlockSpec((br, cols), lambda i: (i, 0)),
        out_shape=jax.ShapeDtypeStruct((rows, cols), F32), compiler_params=_cparams(("parallel",)))(parts)


def _pair_add(a, b, name):
    shape = a.shape
    a, b = a.reshape(-1, shape[-1]), b.reshape(-1, shape[-1])
    rows, cols = a.shape
    br = _row_block(rows, 16, 1024)

    def body(a_ref, b_ref, o_ref):
        o_ref[...] = (a_ref[...].astype(F32) + b_ref[...].astype(F32)).astype(BF16)

    spec = pl.BlockSpec((br, cols), lambda i: (i, 0))
    return pl.pallas_call(
        body, name=name, grid=(pl.cdiv(rows, br),), in_specs=[spec, spec], out_specs=spec,
        out_shape=jax.ShapeDtypeStruct((rows, cols), BF16), compiler_params=_cparams(("parallel",)))(a, b).reshape(shape)


def _adamw(w, g, m, v, name):
    rows, cols = w.shape
    br = _row_block(rows)

    def body(w_ref, g_ref, m_ref, v_ref, d_ref, nm_ref, nv_ref):
        gv = g_ref[...]
        nm = ADAM_B1 * m_ref[...] + (1.0 - ADAM_B1) * gv
        nv = ADAM_B2 * v_ref[...] + (1.0 - ADAM_B2) * (gv * gv)
        m_hat = nm / (1.0 - ADAM_B1 ** ADAM_STEP)
        v_hat = nv / (1.0 - ADAM_B2 ** ADAM_STEP)
        d_ref[...] = -ADAM_LR * (m_hat / (jnp.sqrt(v_hat) + ADAM_EPS) + ADAM_WD * w_ref[...])
        nm_ref[...] = nm
        nv_ref[...] = nv

    spec = pl.BlockSpec((br, cols), lambda i: (i, 0))
    shape = jax.ShapeDtypeStruct((rows, cols), F32)
    return pl.pallas_call(
        body, name=name, grid=(pl.cdiv(rows, br),), in_specs=[spec] * 4, out_specs=(spec,) * 3,
        out_shape=(shape,) * 3, compiler_params=_cparams(("parallel",)))(w, g, m, v)


def _pack(parts, row_multiple=8):
    flat = []
    for p in parts:
        v = p.reshape(-1).astype(F32)
        pad = (-v.shape[0]) % LANES
        if pad:
            v = jnp.concatenate([v, jnp.zeros((pad,), F32)])
        flat.append(v)
    v = jnp.concatenate(flat)
    rows = v.shape[0] // LANES
    pad_rows = (-rows) % row_multiple
    if pad_rows:
        v = jnp.concatenate([v, jnp.zeros((pad_rows * LANES,), F32)])
    return v.reshape(-1, LANES)


def _unpack(packed, shapes):
    lead = packed.shape[:-2]
    flat = packed.reshape(lead + (-1,))
    out, off = [], 0
    for sh in shapes:
        size = math.prod(sh)
        out.append(flat[..., off:off + size].reshape(lead + tuple(sh)))
        off += size + (-size) % LANES
    return out


def kernel(x, c, w_ada, b_ada, g_norm, w_in, b_f, a_re, a_im, log_dt, b_re, b_im, c_re, c_im, d_skip, w_glu, b_glu, w_up_a, w_up_b, w_out, g_final, loss_target, m_w_ada, m_b_ada, m_g_norm, m_w_in, m_b_f, m_a_re, m_a_im, m_log_dt, m_b_re, m_b_im, m_c_re, m_c_im, m_d_skip, m_w_glu, m_b_glu, m_w_up_a, m_w_up_b, m_w_out, m_g_final, v_w_ada, v_b_ada, v_g_norm, v_w_in, v_b_f, v_a_re, v_a_im, v_log_dt, v_b_re, v_b_im, v_c_re, v_c_im, v_d_skip, v_w_glu, v_b_glu, v_w_up_a, v_w_up_b, v_w_out, v_g_final):
    xi, yi, ci = lax.axis_index("x"), lax.axis_index("y"), lax.axis_index("c")
    chip = 2 * xi + yi
    me = 4 * xi + 2 * yi + ci
    s = x.shape[1]
    x2d = x[0]
    tgt = loss_target[0]
    n_att = s // min(T_ATT, s)
    t_att = min(T_ATT, s)

    c_all, _ = _allgather8(c.reshape(8, LANES), "gather_c")
    c_all = c_all.reshape(8, D_MODEL)
    ncol = w_ada.shape[2]
    b_cols = lax.dynamic_slice_in_dim(b_ada, chip * ncol, ncol, axis=1)
    mod_cols = _mod_cols(c_all, w_ada[0], b_cols)
    mod_all, _ = _allgather8(mod_cols.reshape(-1, LANES), "gather_mod")
    mod_all = mod_all.reshape(4, 2, 8, ncol)[:, 0]
    mod_me = lax.dynamic_index_in_dim(mod_all, me, axis=1, keepdims=False).reshape(1, 3 * D_MODEL)
    shift, scale, gate = mod_me[:, :D_MODEL], mod_me[:, D_MODEL:2 * D_MODEL], mod_me[:, 2 * D_MODEL:]
    gs = g_norm * (1.0 + scale)

    nshard = w_in.shape[2]
    w_in_t, m_in_t, v_in_t = (jnp.swapaxes(a[0], 0, 1) for a in (w_in, m_w_in, v_w_in))
    wt_pack = jnp.pad(w_in_t.astype(BF16), ((0, SHARD_ROWS - nshard), (0, 0)))
    misc_shapes = [w_glu.shape[1:], w_up_a.shape[1:], w_up_b.shape[1:], w_out.shape[1:]]
    misc_pack = jnp.concatenate([w.reshape(-1) for w in (w_glu, w_up_a, w_up_b, w_out)]).astype(BF16).reshape(-1, LANES)
    def halves(a):
        return a.reshape((2, a.shape[0] // 2) + a.shape[1:])

    wt_all, misc_all = _gather_shards([halves(wt_pack), halves(misc_pack)], "gather_weights")
    wt_all = lax.dynamic_update_index_in_dim(wt_all, halves(wt_pack), chip, 0).reshape((4,) + wt_pack.shape)
    misc_all = lax.dynamic_update_index_in_dim(misc_all, halves(misc_pack), chip, 0).reshape((4,) + misc_pack.shape)
    p_glu, p_ua, p_ub, p_out = _unpack(misc_all, misc_shapes)

    def w_rows(lo, hi):
        out = []
        for j in range(4):
            a, b = max(lo, j * nshard), min(hi, (j + 1) * nshard)
            if a < b:
                out.append(wt_all[j, a - j * nshard:b - j * nshard])
        return out

    wqkv_t = jnp.concatenate(w_rows(O_Q, O_F), axis=0)
    wrest_t = jnp.concatenate(w_rows(O_GA, O_GB) + w_rows(O_GB, O_END) + w_rows(O_ZA, O_U) + w_rows(O_U, O_ZB)
                              + w_rows(O_ZB, O_GA) + w_rows(O_F, O_ZA)
                              + [jnp.zeros((REST_W - R_F - HEADS, D_MODEL), BF16)], axis=0)
    wmid = (p_glu.reshape(S5_W, S5_W), jnp.concatenate([p_ua[j] for j in range(4)], axis=1),
            jnp.concatenate([p_ub[j] for j in range(4)], axis=1), p_out.reshape(D_MODEL, D_MODEL))

    h, qkv, rest = _prenorm_proj(x2d, gs, shift, wqkv_t, wrest_t)
    bf128 = jnp.pad(b_f, ((0, 0), (0, LANES - HEADS)))
    fpc, f_t, kbias, qbias = _fcum(rest, bf128, _bias_selectors())
    frow5 = f_t.reshape(4, 2, n_att, 1, t_att)
    o, lse_pc = _attn_fwd(qkv, kbias, qbias)

    abar_r, abar_i, bb_r, bb_i = _ssm_block_params(a_re[0], a_im[0], log_dt[0], b_re[0], b_im[0])
    bb_rt, bb_it = jnp.swapaxes(bb_r, 1, 2).astype(BF16), jnp.swapaxes(bb_i, 1, 2).astype(BF16)
    cr_b, ci_b = c_re[0].astype(BF16), (-c_im[0]).astype(BF16)
    bd_c, cd_c = _compact_diag(bb_rt, bb_it), _compact_diag(cr_b, ci_b)
    seg = min(TB_SSM, s) // 8
    ys0, xs = _ssm_fwd(rest, bd_c, cd_c, *_scan_consts(a_re[0], a_im[0], log_dt[0], seg, False))

    vec = jnp.concatenate([gate, g_final.reshape(1, D_MODEL), jnp.concatenate([d_skip, b_glu], axis=1),
                           jnp.zeros((5, D_MODEL), F32)], axis=0)
    hsel = jnp.repeat(jnp.eye(HEADS, dtype=F32), HEAD_DIM, axis=1)
    (dx2, dga, dgb, do, dza, dzb, dys, dlt_t, g_out, g_ua, g_ub, g_glu, vmid) = _mid(
        o, rest, ys0, x2d, tgt, wmid, vec, hsel)

    lse_t = jnp.transpose(lse_pc.reshape(s, 4 // ATT_PAIRS, LANES)[:, :, :2 * ATT_PAIRS], (1, 2, 0))
    lse5 = lse_t.reshape(4, 2, n_att, 1, t_att)
    dlt5 = dlt_t.reshape(4, 2, n_att, 1, t_att)
    dq, dk, dv, dfk, dfq = _attn_bwd(qkv, do, lse5, dlt5, frow5, fpc)
    du, g_bd, g_cdt, da8 = _ssm_bwd(dys, xs, rest, bd_c, cd_c, *_scan_consts(a_re[0], a_im[0], log_dt[0], seg, True),
                                    d_skip)
    df, dbf8 = _dfcum(dfk, dfq, rest, bf128)

    gq, gk, gv, gga, ggb, gza, gu, gzb, gf = _grad_w_rows(h, [dq, dk, dv, dga, dgb, dza, du, dzb, df])
    g_in_t = jnp.concatenate([gq, gk, gv, gf[:HEADS], gza, gu, gzb, gga, ggb], axis=0)

    def shard_cols(g, j):
        n = g.shape[1] // 4
        return g[:, j * n:(j + 1) * n]

    def shard_rows(g, j):
        n = g.shape[0] // 4
        return g[j * n:(j + 1) * n]

    def halves4(a):
        return a.reshape((4, 2, a.shape[1] // 2) + a.shape[2:])

    gt_pack = halves4(jnp.stack([
        jnp.pad(g_in_t[j * nshard:(j + 1) * nshard].astype(BF16), ((0, SHARD_ROWS - nshard), (0, 0)))
        for j in range(4)]))
    gm_pack = halves4(jnp.stack([
        jnp.concatenate([shard_rows(g_glu, j).reshape(-1), shard_cols(g_ua, j).reshape(-1),
                         shard_cols(g_ub, j).reshape(-1), shard_rows(g_out, j).reshape(-1)]).astype(BF16)
        .reshape(-1, LANES) for j in range(4)]))
    recv_in, recv_misc = _swap_sibling([gt_pack, gm_pack], "pair_swap_weight_grads", other_half=True)
    own_in = lax.dynamic_index_in_dim(gt_pack, ci, axis=1, keepdims=False)
    own_misc = lax.dynamic_index_in_dim(gm_pack, ci, axis=1, keepdims=False)
    pair_in = _pair_add(own_in, recv_in, "pair_add_w_in")
    pair_misc = _pair_add(own_misc, recv_misc, "pair_add_misc")

    grad_x, vdh, parts_in, parts_misc = _dh(dq, dk, dv, dga, dgb, dza, du, dzb, df, wqkv_t, wrest_t, x2d, dx2, gs,
                                            [pair_in, pair_misc])
    parts_in = lax.dynamic_update_slice_in_dim(parts_in, lax.dynamic_slice_in_dim(pair_in, chip, 1, 0), chip, 0)
    parts_misc = lax.dynamic_update_slice_in_dim(parts_misc, lax.dynamic_slice_in_dim(pair_misc, chip, 1, 0), chip, 0)
    half_in, half_misc = _sum4(parts_in, "sum4_w_in"), _sum4(parts_misc, "sum4_misc")
    sib_in, sib_misc = _swap_sibling([half_in, half_misc], "swap_weight_grads")

    def both_halves(mine, theirs):
        return jnp.concatenate([jnp.where(ci == 0, mine, theirs), jnp.where(ci == 0, theirs, mine)], axis=0)

    tot_in, tot_misc = both_halves(half_in, sib_in), both_halves(half_misc, sib_misc)
    g_glu_s, g_ua_s, g_ub_s, g_out_s = _unpack(tot_misc, misc_shapes)

    dgs, dshift = vdh[0:1], vdh[1:2]
    dmod = jnp.concatenate([dshift, dgs * g_norm, vmid[1:2]], axis=1)
    da = jnp.sum(da8, axis=0)
    g_bd = g_bd.reshape(GROUPS, GCH, 2 * STATE)
    g_cdt = g_cdt.reshape(GROUPS, GCH, 2 * STATE)
    g_bbr = jnp.swapaxes(g_bd[:, :, :STATE], 1, 2)
    g_bbi = jnp.swapaxes(g_bd[:, :, STATE:], 1, 2)
    g_cre = g_cdt[:, :, :STATE]
    g_cim = -g_cdt[:, :, STATE:]
    small_shapes = [(1,), (3 * D_MODEL,), (D_MODEL,), (HEADS,), (GROUPS, STATE), (GROUPS, STATE),
                    (GROUPS, STATE, GCH), (GROUPS, STATE, GCH), (GROUPS, GCH, STATE), (GROUPS, GCH, STATE),
                    (S5_W,), (S5_W,), (D_MODEL,)]
    small = _pack([vmid[3, 0:1], dmod, dgs * (1.0 + scale), dbf8[0, :HEADS], da[:NSTATE], da[NSTATE:],
                   g_bbr, g_bbi, g_cre, g_cim, vmid[2, :S5_W], vmid[2, S5_W:], vmid[0]])
    small_all, small_sum = _allgather8(small, "gather_small_grads")
    (loss_s, g_b_ada, g_g_norm, g_b_f, g_abr, g_abi, g_bbr_s, g_bbi_s, g_c_re, g_c_im, g_d_skip, g_b_glu,
     g_g_final) = _unpack(small_sum, small_shapes)
    loss = loss_s[0]
    dmod_all = _unpack(small_all, small_shapes)[1]
    dmod_cols = lax.dynamic_slice_in_dim(dmod_all, chip * ncol, ncol, axis=1)
    g_w_ada = _grad_w_ada(c_all, dmod_cols)
    _, ssm_vjp = jax.vjp(_ssm_block_params, a_re[0], a_im[0], log_dt[0], b_re[0], b_im[0])
    g_a_re, g_a_im, g_log_dt, g_b_re, g_b_im = ssm_vjp((g_abr, g_abi, g_bbr_s, g_bbi_s))

    def adam(name, w, g, m, v):
        shape = w.shape
        total = math.prod(shape)
        if len(shape) > 1 and shape[-1] >= LANES:
            cols = shape[-1]
        elif total % LANES == 0:
            cols = LANES
        else:
            cols = total
        two = lambda a: a.reshape(-1, cols)
        d, nm, nv = _adamw(two(w), two(g), two(m), two(v), "adamw_" + name)
        return g.reshape(shape), d.reshape(shape), nm.reshape(shape), nv.reshape(shape)

    back = lambda a: jnp.swapaxes(a, 0, 1)[None]
    d_in_t, nm_in_t, nv_in_t = _adamw(w_in_t, tot_in, m_in_t, v_in_t, "adamw_w_in")
    res_w_in = (back(tot_in[:nshard]), back(d_in_t), back(nm_in_t), back(nv_in_t))

    res = [
        adam("w_ada", w_ada, g_w_ada, m_w_ada, v_w_ada),
        adam("b_ada", b_ada, g_b_ada, m_b_ada, v_b_ada),
        adam("g_norm", g_norm, g_g_norm, m_g_norm, v_g_norm),
        res_w_in,
        adam("b_f", b_f, g_b_f, m_b_f, v_b_f),
        adam("a_re", a_re, g_a_re, m_a_re, v_a_re),
        adam("a_im", a_im, g_a_im, m_a_im, v_a_im),
        adam("log_dt", log_dt, g_log_dt, m_log_dt, v_log_dt),
        adam("b_re", b_re, g_b_re, m_b_re, v_b_re),
        adam("b_im", b_im, g_b_im, m_b_im, v_b_im),
        adam("c_re", c_re, g_c_re, m_c_re, v_c_re),
        adam("c_im", c_im, g_c_im, m_c_im, v_c_im),
        adam("d_skip", d_skip, g_d_skip, m_d_skip, v_d_skip),
        adam("w_glu", w_glu, g_glu_s, m_w_glu, v_w_glu),
        adam("b_glu", b_glu, g_b_glu, m_b_glu, v_b_glu),
        adam("w_up_a", w_up_a, g_ua_s, m_w_up_a, v_w_up_a),
        adam("w_up_b", w_up_b, g_ub_s, m_w_up_b, v_w_up_b),
        adam("w_out", w_out, g_out_s, m_w_out, v_w_out),
        adam("g_final", g_final, g_g_final, m_g_final, v_g_final),
    ]
    grads = [r[0] for r in res]
    deltas = [r[1] for r in res]
    new_m = [r[2] for r in res]
    new_v = [r[3] for r in res]
    return (loss, grad_x[None], *grads, *deltas, *new_m, *new_v)
```
